```python
import jax, jax.numpy as jnp
from jax import lax
import numpy as np

D_MODEL = 1024
BATCH = 8
SEQ = 4096
DEPTH = 4

CHUNK = 64
N_A = DEPTH // 2
N_B = DEPTH - N_A
GMLP_BLOCK = 128
GMLP_WIDTH = D_MODEL
GMLP_GROUPS = 8
GMLP_GROUP_DIM = GMLP_WIDTH // GMLP_GROUPS
SB_HEADS = 16
SB_HEAD_DIM = D_MODEL // SB_HEADS
SB_QBLOCK = 128
D_FF = 4 * D_MODEL
ALPHA = float((2 * DEPTH) ** 0.25)
BETA = float((8 * DEPTH) ** -0.25)
LN_EPS = 1e-5

kernel_name = "yoco_gmlp_stickbreaking_deepnorm"


def layer_norm(x, g, b):
    xf = x.astype(jnp.float32)
    mu = jnp.mean(xf, axis=-1, keepdims=True)
    var = jnp.mean(jnp.square(xf - mu), axis=-1, keepdims=True)
    y = (xf - mu) * lax.rsqrt(var + LN_EPS) * g.astype(jnp.float32) + b.astype(jnp.float32)
    return y.astype(x.dtype)


def chunk_causal_mask(n):
    pos = jnp.arange(n)
    return (pos[None, :] // CHUNK) <= (pos[:, None] // CHUNK)


def gmlp_mixer(x, w_in, ln_g, ln_b, w_s, b_s, w_out):
    bsz, seq, _ = x.shape
    z = jax.nn.gelu(x @ w_in)
    u, v = jnp.split(z, 2, axis=-1)
    v = layer_norm(v, ln_g, ln_b)
    nblk = seq // GMLP_BLOCK
    v = v.reshape(bsz, nblk, GMLP_BLOCK, GMLP_GROUPS, GMLP_GROUP_DIM)
    ws = jnp.where(chunk_causal_mask(GMLP_BLOCK)[None], w_s, jnp.zeros((), w_s.dtype))
    s = jnp.einsum('gts,bnsgc->bntgc', ws, v)
    s = s + jnp.transpose(b_s)[None, None, :, :, None]
    s = s.reshape(bsz, seq, GMLP_WIDTH)
    return (u * s) @ w_out


def stick_breaking_mixer(x, w_q, w_o, k, v):
    bsz, seq, _ = x.shape
    q = (x @ w_q).reshape(bsz, seq, SB_HEADS, SB_HEAD_DIM).transpose(0, 2, 1, 3)
    scale = SB_HEAD_DIM ** -0.5
    outs = []
    for i in range(seq // SB_QBLOCK):
        q0, q1 = i * SB_QBLOCK, (i + 1) * SB_QBLOCK
        qb = q[:, :, q0:q1]
        kb = k[:, :, :q1]
        vb = v[:, :, :q1]
        zlog = jnp.einsum('bhtd,bhsd->bhts', qb, kb).astype(jnp.float32) * scale
        t_idx = q0 + jnp.arange(SB_QBLOCK)
        s_idx = jnp.arange(q1)
        causal = s_idx[None, :] < t_idx[:, None]
        log_rem = jnp.where(causal, jax.nn.log_sigmoid(-zlog), 0.0)
        excl = lax.cumsum(log_rem, axis=3, reverse=True) - log_rem
        log_a = jax.nn.log_sigmoid(zlog) + excl
        a = jnp.where(causal, jnp.exp(log_a), 0.0).astype(vb.dtype)
        outs.append(jnp.einsum('bhts,bhsd->bhtd', a, vb))
    o = jnp.concatenate(outs, axis=2)
    o = o.transpose(0, 2, 1, 3).reshape(bsz, seq, D_MODEL)
    return o @ w_o


def squared_relu_mlp(x, w1, w2):
    return jnp.square(jax.nn.relu(x @ w1)) @ w2


def _fwd_setup_inputs(seed: int = 0) -> dict:
    key = jax.random.key(seed)
    ks = jax.random.split(key, 20)

    def nrm(k, shape, scale):
        return jax.random.normal(k, shape, jnp.float32) * scale

    return {
        "x": nrm(ks[0], (BATCH, SEQ, D_MODEL), 1.0),
        "a_w_in": nrm(ks[1], (N_A, D_MODEL, 2 * GMLP_WIDTH), D_MODEL ** -0.5),
        "a_ln_g": 1.0 + nrm(ks[2], (N_A, GMLP_WIDTH), 0.02),
        "a_ln_b": nrm(ks[3], (N_A, GMLP_WIDTH), 0.02),
        "a_w_s": nrm(ks[4], (N_A, GMLP_GROUPS, GMLP_BLOCK, GMLP_BLOCK), GMLP_BLOCK ** -0.5),
        "a_b_s": 1.0 + nrm(ks[5], (N_A, GMLP_GROUPS, GMLP_BLOCK), 0.02),
        "a_w_out": nrm(ks[6], (N_A, GMLP_WIDTH, D_MODEL), BETA * GMLP_WIDTH ** -0.5),
        "sb_w_k": nrm(ks[7], (D_MODEL, D_MODEL), D_MODEL ** -0.5),
        "sb_w_v": nrm(ks[8], (D_MODEL, D_MODEL), BETA * D_MODEL ** -0.5),
        "b_w_q": nrm(ks[9], (N_B, D_MODEL, D_MODEL), D_MODEL ** -0.5),
        "b_w_o": nrm(ks[10], (N_B, D_MODEL, D_MODEL), BETA * D_MODEL ** -0.5),
        "mix_ln_g": 1.0 + nrm(ks[11], (DEPTH, D_MODEL), 0.02),
        "mix_ln_b": nrm(ks[12], (DEPTH, D_MODEL), 0.02),
        "ffn_ln_g": 1.0 + nrm(ks[13], (DEPTH, D_MODEL), 0.02),
        "ffn_ln_b": nrm(ks[14], (DEPTH, D_MODEL), 0.02),
        "ffn_w1": nrm(ks[15], (DEPTH, D_MODEL, D_FF), BETA * D_MODEL ** -0.5),
        "ffn_w2": nrm(ks[16], (DEPTH, D_FF, D_MODEL), BETA * D_FF ** -0.5),
    }


def _fwd_reference(x, a_w_in, a_ln_g, a_ln_b, a_w_s, a_b_s, a_w_out, sb_w_k, sb_w_v,
              b_w_q, b_w_o, mix_ln_g, mix_ln_b, ffn_ln_g, ffn_ln_b, ffn_w1, ffn_w2):
    bsz, seq, _ = x.shape
    k_shared = None
    v_shared = None
    for l in range(DEPTH):
        if l < N_A:
            mix = gmlp_mixer(x, a_w_in[l], a_ln_g[l], a_ln_b[l], a_w_s[l], a_b_s[l], a_w_out[l])
        else:
            if l == N_A:
                k_shared = (x @ sb_w_k).reshape(bsz, seq, SB_HEADS, SB_HEAD_DIM).transpose(0, 2, 1, 3)
                v_shared = (x @ sb_w_v).reshape(bsz, seq, SB_HEADS, SB_HEAD_DIM).transpose(0, 2, 1, 3)
            j = l - N_A
            mix = stick_breaking_mixer(x, b_w_q[j], b_w_o[j], k_shared, v_shared)
        x = layer_norm(ALPHA * x + mix, mix_ln_g[l], mix_ln_b[l])
        x = layer_norm(ALPHA * x + squared_relu_mlp(x, ffn_w1[l], ffn_w2[l]), ffn_ln_g[l], ffn_ln_b[l])
    return x


import jax as _jax
import jax.numpy as _jnp

TWIN_FORMAT = 'train_step'
FWD_PARAMS = ['x', 'a_w_in', 'a_ln_g', 'a_ln_b', 'a_w_s', 'a_b_s', 'a_w_out', 'sb_w_k', 'sb_w_v', 'b_w_q', 'b_w_o', 'mix_ln_g', 'mix_ln_b', 'ffn_ln_g', 'ffn_ln_b', 'ffn_w1', 'ffn_w2']
TWIN_WEIGHTS = ['a_w_in', 'a_ln_g', 'a_ln_b', 'a_w_s', 'a_b_s', 'a_w_out', 'sb_w_k', 'sb_w_v', 'b_w_q', 'b_w_o', 'mix_ln_g', 'mix_ln_b', 'ffn_ln_g', 'ffn_ln_b', 'ffn_w1', 'ffn_w2']
TWIN_DIFF_INPUT = 'x'
TWIN_INPUTS = ['x', 'a_w_in', 'a_ln_g', 'a_ln_b', 'a_w_s', 'a_b_s', 'a_w_out', 'sb_w_k', 'sb_w_v', 'b_w_q', 'b_w_o', 'mix_ln_g', 'mix_ln_b', 'ffn_ln_g', 'ffn_ln_b', 'ffn_w1', 'ffn_w2', 'loss_target', 'm_a_w_in', 'm_a_ln_g', 'm_a_ln_b', 'm_a_w_s', 'm_a_b_s', 'm_a_w_out', 'm_sb_w_k', 'm_sb_w_v', 'm_b_w_q', 'm_b_w_o', 'm_mix_ln_g', 'm_mix_ln_b', 'm_ffn_ln_g', 'm_ffn_ln_b', 'm_ffn_w1', 'm_ffn_w2', 'v_a_w_in', 'v_a_ln_g', 'v_a_ln_b', 'v_a_w_s', 'v_a_b_s', 'v_a_w_out', 'v_sb_w_k', 'v_sb_w_v', 'v_b_w_q', 'v_b_w_o', 'v_mix_ln_g', 'v_mix_ln_b', 'v_ffn_ln_g', 'v_ffn_ln_b', 'v_ffn_w1', 'v_ffn_w2']
TWIN_OUTPUTS = ['loss', 'grad_x', 'grad_a_w_in', 'grad_a_ln_g', 'grad_a_ln_b', 'grad_a_w_s', 'grad_a_b_s', 'grad_a_w_out', 'grad_sb_w_k', 'grad_sb_w_v', 'grad_b_w_q', 'grad_b_w_o', 'grad_mix_ln_g', 'grad_mix_ln_b', 'grad_ffn_ln_g', 'grad_ffn_ln_b', 'grad_ffn_w1', 'grad_ffn_w2', 'delta_a_w_in', 'delta_a_ln_g', 'delta_a_ln_b', 'delta_a_w_s', 'delta_a_b_s', 'delta_a_w_out', 'delta_sb_w_k', 'delta_sb_w_v', 'delta_b_w_q', 'delta_b_w_o', 'delta_mix_ln_g', 'delta_mix_ln_b', 'delta_ffn_ln_g', 'delta_ffn_ln_b', 'delta_ffn_w1', 'delta_ffn_w2', 'new_m_a_w_in', 'new_m_a_ln_g', 'new_m_a_ln_b', 'new_m_a_w_s', 'new_m_a_b_s', 'new_m_a_w_out', 'new_m_sb_w_k', 'new_m_sb_w_v', 'new_m_b_w_q', 'new_m_b_w_o', 'new_m_mix_ln_g', 'new_m_mix_ln_b', 'new_m_ffn_ln_g', 'new_m_ffn_ln_b', 'new_m_ffn_w1', 'new_m_ffn_w2', 'new_v_a_w_in', 'new_v_a_ln_g', 'new_v_a_ln_b', 'new_v_a_w_s', 'new_v_a_b_s', 'new_v_a_w_out', 'new_v_sb_w_k', 'new_v_sb_w_v', 'new_v_b_w_q', 'new_v_b_w_o', 'new_v_mix_ln_g', 'new_v_mix_ln_b', 'new_v_ffn_ln_g', 'new_v_ffn_ln_b', 'new_v_ffn_w1', 'new_v_ffn_w2']
TWIN_LEAF_KINDS = {'loss': 'loss', 'grad_x': 'grad_x', 'grad_a_w_in': 'grad_w', 'grad_a_ln_g': 'grad_w', 'grad_a_ln_b': 'grad_w', 'grad_a_w_s': 'grad_w', 'grad_a_b_s': 'grad_w', 'grad_a_w_out': 'grad_w', 'grad_sb_w_k': 'grad_w', 'grad_sb_w_v': 'grad_w', 'grad_b_w_q': 'grad_w', 'grad_b_w_o': 'grad_w', 'grad_mix_ln_g': 'grad_w', 'grad_mix_ln_b': 'grad_w', 'grad_ffn_ln_g': 'grad_w', 'grad_ffn_ln_b': 'grad_w', 'grad_ffn_w1': 'grad_w', 'grad_ffn_w2': 'grad_w', 'delta_a_w_in': 'delta_w', 'delta_a_ln_g': 'delta_w', 'delta_a_ln_b': 'delta_w', 'delta_a_w_s': 'delta_w', 'delta_a_b_s': 'delta_w', 'delta_a_w_out': 'delta_w', 'delta_sb_w_k': 'delta_w', 'delta_sb_w_v': 'delta_w', 'delta_b_w_q': 'delta_w', 'delta_b_w_o': 'delta_w', 'delta_mix_ln_g': 'delta_w', 'delta_mix_ln_b': 'delta_w', 'delta_ffn_ln_g': 'delta_w', 'delta_ffn_ln_b': 'delta_w', 'delta_ffn_w1': 'delta_w', 'delta_ffn_w2': 'delta_w', 'new_m_a_w_in': 'new_m', 'new_m_a_ln_g': 'new_m', 'new_m_a_ln_b': 'new_m', 'new_m_a_w_s': 'new_m', 'new_m_a_b_s': 'new_m', 'new_m_a_w_out': 'new_m', 'new_m_sb_w_k': 'new_m', 'new_m_sb_w_v': 'new_m', 'new_m_b_w_q': 'new_m', 'new_m_b_w_o': 'new_m', 'new_m_mix_ln_g': 'new_m', 'new_m_mix_ln_b': 'new_m', 'new_m_ffn_ln_g': 'new_m', 'new_m_ffn_ln_b': 'new_m', 'new_m_ffn_w1': 'new_m', 'new_m_ffn_w2': 'new_m', 'new_v_a_w_in': 'new_v', 'new_v_a_ln_g': 'new_v', 'new_v_a_ln_b': 'new_v', 'new_v_a_w_s': 'new_v', 'new_v_a_b_s': 'new_v', 'new_v_a_w_out': 'new_v', 'new_v_sb_w_k': 'new_v', 'new_v_sb_w_v': 'new_v', 'new_v_b_w_q': 'new_v', 'new_v_b_w_o': 'new_v', 'new_v_mix_ln_g': 'new_v', 'new_v_mix_ln_b': 'new_v', 'new_v_ffn_ln_g': 'new_v', 'new_v_ffn_ln_b': 'new_v', 'new_v_ffn_w1': 'new_v', 'new_v_ffn_w2': 'new_v'}


def _forward(args):
    return _fwd_reference(*[args[k] for k in FWD_PARAMS])


def _output_shape():
    out = _jax.eval_shape(lambda: _forward(_fwd_setup_inputs(0)))
    return out.shape, out.dtype

N_MICROBATCH = 1
ADAM_LR = 0.001
ADAM_B1 = 0.9
ADAM_B2 = 0.999
ADAM_EPS = 1e-08
ADAM_WD = 0.01
ADAM_STEP = 10
PER_EXAMPLE_BATCH_AXIS = {'x': 0, 'loss_target': 0}
SHARED_INPUTS = []
_WEIGHT_DTYPES = {'a_w_in': _jnp.float32, 'a_ln_g': _jnp.float32, 'a_ln_b': _jnp.float32, 'a_w_s': _jnp.float32, 'a_b_s': _jnp.float32, 'a_w_out': _jnp.float32, 'sb_w_k': _jnp.float32, 'sb_w_v': _jnp.float32, 'b_w_q': _jnp.float32, 'b_w_o': _jnp.float32, 'mix_ln_g': _jnp.float32, 'mix_ln_b': _jnp.float32, 'ffn_ln_g': _jnp.float32, 'ffn_ln_b': _jnp.float32, 'ffn_w1': _jnp.float32, 'ffn_w2': _jnp.float32}
MOMENT_SCALE = {'a_w_in': 3.406568e-02, 'a_ln_g': 2.522415e-02, 'a_ln_b': 2.543452e-02, 'a_w_s': 2.466655e-02, 'a_b_s': 3.066385e-02, 'a_w_out': 1.069500e-01, 'sb_w_k': 7.784662e-03, 'sb_w_v': 4.311205e-02, 'b_w_q': 5.520189e-03, 'b_w_o': 3.029190e-02, 'mix_ln_g': 9.917199e-01, 'mix_ln_b': 4.896976e-01, 'ffn_ln_g': 1.607182e+01, 'ffn_ln_b': 1.885069e+00, 'ffn_w1': 1.330684e-02, 'ffn_w2': 3.026845e-02}


def _to_microbatches(a, axis):
    t = _jnp.moveaxis(a, axis, 0)
    t = t.reshape((N_MICROBATCH, t.shape[0] // N_MICROBATCH) + t.shape[1:])
    return _jnp.moveaxis(t, 1, axis + 1)


def setup_inputs(seed: int = 0) -> dict:
    inp = _fwd_setup_inputs(seed)
    key = _jax.random.fold_in(_jax.random.key(seed), 7919)
    shape, _ = _output_shape()
    out = dict(inp)
    out["loss_target"] = _jax.random.normal(_jax.random.fold_in(key, 0), shape, _jnp.float32)
    for i, name in enumerate(TWIN_WEIGHTS):
        w = inp[name].astype(_jnp.float32)
        if MOMENT_SCALE is None:
            s = _jnp.sqrt(_jnp.mean(_jnp.square(w)) + 1e-30)
        else:
            s = MOMENT_SCALE[name]
        km, kv = _jax.random.split(_jax.random.fold_in(key, i + 1))
        out[name] = w
        out["m_" + name] = s * _jax.random.normal(km, w.shape, _jnp.float32)
        out["v_" + name] = (s * s) * _jax.random.uniform(kv, w.shape, _jnp.float32, 0.5, 1.5)
    if N_MICROBATCH > 1:
        for name, axis in PER_EXAMPLE_BATCH_AXIS.items():
            out[name] = _to_microbatches(out[name], axis)
    return {'x': out['x'], 'a_w_in': out['a_w_in'], 'a_ln_g': out['a_ln_g'], 'a_ln_b': out['a_ln_b'], 'a_w_s': out['a_w_s'], 'a_b_s': out['a_b_s'], 'a_w_out': out['a_w_out'], 'sb_w_k': out['sb_w_k'], 'sb_w_v': out['sb_w_v'], 'b_w_q': out['b_w_q'], 'b_w_o': out['b_w_o'], 'mix_ln_g': out['mix_ln_g'], 'mix_ln_b': out['mix_ln_b'], 'ffn_ln_g': out['ffn_ln_g'], 'ffn_ln_b': out['ffn_ln_b'], 'ffn_w1': out['ffn_w1'], 'ffn_w2': out['ffn_w2'], 'loss_target': out['loss_target'], 'm_a_w_in': out['m_a_w_in'], 'm_a_ln_g': out['m_a_ln_g'], 'm_a_ln_b': out['m_a_ln_b'], 'm_a_w_s': out['m_a_w_s'], 'm_a_b_s': out['m_a_b_s'], 'm_a_w_out': out['m_a_w_out'], 'm_sb_w_k': out['m_sb_w_k'], 'm_sb_w_v': out['m_sb_w_v'], 'm_b_w_q': out['m_b_w_q'], 'm_b_w_o': out['m_b_w_o'], 'm_mix_ln_g': out['m_mix_ln_g'], 'm_mix_ln_b': out['m_mix_ln_b'], 'm_ffn_ln_g': out['m_ffn_ln_g'], 'm_ffn_ln_b': out['m_ffn_ln_b'], 'm_ffn_w1': out['m_ffn_w1'], 'm_ffn_w2': out['m_ffn_w2'], 'v_a_w_in': out['v_a_w_in'], 'v_a_ln_g': out['v_a_ln_g'], 'v_a_ln_b': out['v_a_ln_b'], 'v_a_w_s': out['v_a_w_s'], 'v_a_b_s': out['v_a_b_s'], 'v_a_w_out': out['v_a_w_out'], 'v_sb_w_k': out['v_sb_w_k'], 'v_sb_w_v': out['v_sb_w_v'], 'v_b_w_q': out['v_b_w_q'], 'v_b_w_o': out['v_b_w_o'], 'v_mix_ln_g': out['v_mix_ln_g'], 'v_mix_ln_b': out['v_mix_ln_b'], 'v_ffn_ln_g': out['v_ffn_ln_g'], 'v_ffn_ln_b': out['v_ffn_ln_b'], 'v_ffn_w1': out['v_ffn_w1'], 'v_ffn_w2': out['v_ffn_w2']}


def _loss(weights, diff, rest, loss_target):
    with _jax.named_scope("forward"):
        args = {**rest, TWIN_DIFF_INPUT: diff, **{k: w.astype(_WEIGHT_DTYPES[k]) for k, w in weights.items()}}
        y = _forward(args)
    with _jax.named_scope("loss_head"):
        err = _jnp.square(y.astype(_jnp.float32) - loss_target)
        return 0.5 * _jnp.sum(_jnp.mean(err, axis=-1)) if err.ndim else 0.5 * err


def _adamw(w, g, m, v):
    m = ADAM_B1 * m + (1.0 - ADAM_B1) * g
    v = ADAM_B2 * v + (1.0 - ADAM_B2) * _jnp.square(g)
    m_hat = m / (1.0 - ADAM_B1 ** ADAM_STEP)
    v_hat = v / (1.0 - ADAM_B2 ** ADAM_STEP)
    delta = -ADAM_LR * (m_hat / (_jnp.sqrt(v_hat) + ADAM_EPS) + ADAM_WD * w)
    return delta, m, v


def reference(x, a_w_in, a_ln_g, a_ln_b, a_w_s, a_b_s, a_w_out, sb_w_k, sb_w_v, b_w_q, b_w_o, mix_ln_g, mix_ln_b, ffn_ln_g, ffn_ln_b, ffn_w1, ffn_w2, loss_target, m_a_w_in, m_a_ln_g, m_a_ln_b, m_a_w_s, m_a_b_s, m_a_w_out, m_sb_w_k, m_sb_w_v, m_b_w_q, m_b_w_o, m_mix_ln_g, m_mix_ln_b, m_ffn_ln_g, m_ffn_ln_b, m_ffn_w1, m_ffn_w2, v_a_w_in, v_a_ln_g, v_a_ln_b, v_a_w_s, v_a_b_s, v_a_w_out, v_sb_w_k, v_sb_w_v, v_b_w_q, v_b_w_o, v_mix_ln_g, v_mix_ln_b, v_ffn_ln_g, v_ffn_ln_b, v_ffn_w1, v_ffn_w2):
    given = dict(x=x, a_w_in=a_w_in, a_ln_g=a_ln_g, a_ln_b=a_ln_b, a_w_s=a_w_s, a_b_s=a_b_s, a_w_out=a_w_out, sb_w_k=sb_w_k, sb_w_v=sb_w_v, b_w_q=b_w_q, b_w_o=b_w_o, mix_ln_g=mix_ln_g, mix_ln_b=mix_ln_b, ffn_ln_g=ffn_ln_g, ffn_ln_b=ffn_ln_b, ffn_w1=ffn_w1, ffn_w2=ffn_w2, loss_target=loss_target, m_a_w_in=m_a_w_in, m_a_ln_g=m_a_ln_g, m_a_ln_b=m_a_ln_b, m_a_w_s=m_a_w_s, m_a_b_s=m_a_b_s, m_a_w_out=m_a_w_out, m_sb_w_k=m_sb_w_k, m_sb_w_v=m_sb_w_v, m_b_w_q=m_b_w_q, m_b_w_o=m_b_w_o, m_mix_ln_g=m_mix_ln_g, m_mix_ln_b=m_mix_ln_b, m_ffn_ln_g=m_ffn_ln_g, m_ffn_ln_b=m_ffn_ln_b, m_ffn_w1=m_ffn_w1, m_ffn_w2=m_ffn_w2, v_a_w_in=v_a_w_in, v_a_ln_g=v_a_ln_g, v_a_ln_b=v_a_ln_b, v_a_w_s=v_a_w_s, v_a_b_s=v_a_b_s, v_a_w_out=v_a_w_out, v_sb_w_k=v_sb_w_k, v_sb_w_v=v_sb_w_v, v_b_w_q=v_b_w_q, v_b_w_o=v_b_w_o, v_mix_ln_g=v_mix_ln_g, v_mix_ln_b=v_mix_ln_b, v_ffn_ln_g=v_ffn_ln_g, v_ffn_ln_b=v_ffn_ln_b, v_ffn_w1=v_ffn_w1, v_ffn_w2=v_ffn_w2)
    weights = {n: given[n] for n in TWIN_WEIGHTS}
    shared = {n: given[n] for n in SHARED_INPUTS}
    per_example = {n: given[n] for n in ['x']}
    grad_fn = _jax.value_and_grad(_loss, argnums=(0, 1))

    def one_microbatch(ex, loss_target):
        ex = dict(ex)
        diff = ex.pop(TWIN_DIFF_INPUT)
        return grad_fn(weights, diff, {**shared, **ex}, loss_target)

    if N_MICROBATCH == 1:
        loss, (grad_w, grad_x) = one_microbatch(per_example, given["loss_target"])
    else:
        def body(carry, xs):
            loss_sum, grad_sum = carry
            l_k, (gw_k, gx_k) = one_microbatch(xs[0], xs[1])
            with _jax.named_scope("update"):
                return (loss_sum + l_k, _jax.tree.map(_jnp.add, grad_sum, gw_k)), gx_k

        init = (_jnp.zeros((), _jnp.float32), _jax.tree.map(_jnp.zeros_like, weights))
        (loss, grad_w), grad_x = _jax.lax.scan(body, init, (per_example, given["loss_target"]))
    with _jax.named_scope("update"):
        delta_w, new_m, new_v = {}, {}, {}
        for n in TWIN_WEIGHTS:
            delta_w[n], new_m[n], new_v[n] = _adamw(weights[n], grad_w[n], given["m_" + n], given["v_" + n])
    return (loss, grad_x, *[grad_w[n] for n in TWIN_WEIGHTS], *[delta_w[n] for n in TWIN_WEIGHTS],
            *[new_m[n] for n in TWIN_WEIGHTS], *[new_v[n] for n in TWIN_WEIGHTS])
```

```python
import math

import jax
import jax.numpy as jnp
from jax import lax
from jax.experimental import pallas as pl
from jax.experimental.pallas import tpu as pltpu

F32 = jnp.float32
BF16 = jnp.bfloat16
MESH = pl.DeviceIdType.MESH

N_CHIPS = 4
DEPTH = 4
N_A = 2
ALPHA = float((2 * DEPTH) ** 0.25)
LN_EPS = 1e-5
CHUNK = 64
GMLP_BLOCK = 128
GMLP_GROUPS = 8
HEAD_DIM = 64
LANES = 128
ATT_T = 256
ADAM_LR = 0.001
ADAM_B1 = 0.9
ADAM_B2 = 0.999
ADAM_EPS = 1e-08
ADAM_WD = 0.01
ADAM_STEP = 10
VMEM_LIMIT = 56 * 1024 * 1024
TM = 512
TS = 512

NN = ((1,), (0,))
NT = ((1,), (1,))
TN = ((0,), (0,))


def _params(sem):
    return pltpu.CompilerParams(dimension_semantics=sem, vmem_limit_bytes=VMEM_LIMIT)


def _dot(a, b, contract):
    return lax.dot_general(a, b, (contract, ((), ())), preferred_element_type=F32)


def _matmul(name, operands, in_specs, out_shapes, out_specs, grid, contract, epilogue, acc_shape, aliases=None):
    nk = grid[2]
    n_in, n_out = len(operands), len(out_shapes)

    def body(*refs):
        ins, outs = refs[:n_in], refs[n_in:n_in + n_out]
        a, b = ins[0][...], ins[1][...]
        p = _dot(a.astype(BF16), b.astype(BF16), contract)
        if nk == 1:
            epilogue(p, ins[2:], outs)
            return
        acc = refs[-1]
        k = pl.program_id(2)

        @pl.when(k == 0)
        def _():
            acc[...] = p

        @pl.when((k > 0) & (k < nk - 1))
        def _():
            acc[...] += p

        @pl.when(k == nk - 1)
        def _():
            epilogue(acc[...] + p, ins[2:], outs)

    return pl.pallas_call(
        body, name=name, grid=grid, in_specs=in_specs, out_specs=out_specs, out_shape=out_shapes,
        scratch_shapes=[] if nk == 1 else [pltpu.VMEM(acc_shape, F32)],
        input_output_aliases=aliases or {},
        compiler_params=_params(("parallel", "parallel", "arbitrary")),
    )(*operands)


def _wspec(w, layer, shard_axis):
    r, c = w.shape[-2:]
    if w.ndim == 4:
        return pl.BlockSpec((None, None, r, c), lambda i, j, k: ((i, j, k)[shard_axis], layer, 0, 0))
    return pl.BlockSpec((None, r, c), lambda i, j, k: ((i, j, k)[shard_axis], 0, 0))


def _ep_store(p, ins, outs):
    for o in outs:
        o[...] = p.astype(o.dtype)


def _mm_fwd(name, a, w, layer, col_sharded, epilogue=_ep_store, extras=(), extra_specs=(), outs=None):
    s = a.shape[0]
    r, c = w.shape[-2:]
    if col_sharded:
        grid = (s // TM, N_CHIPS, 1)
        a_spec = pl.BlockSpec((TM, r), lambda i, j, k: (i, 0))
        o_map = lambda i, j, k: (i, j)
        n_out = N_CHIPS * c
        w_spec = _wspec(w, layer, 1)
    else:
        grid = (s // TM, 1, N_CHIPS)
        a_spec = pl.BlockSpec((TM, r), lambda i, j, k: (i, k))
        o_map = lambda i, j, k: (i, 0)
        n_out = c
        w_spec = _wspec(w, layer, 2)
    if outs is None:
        outs = [(n_out, F32)]
    out_shapes = [jax.ShapeDtypeStruct((s, n), dt) for n, dt in outs]
    out_specs = [pl.BlockSpec((TM, c if n == n_out else n), o_map) for n, _ in outs]
    return _matmul(name, (a, w) + tuple(extras), [a_spec, w_spec] + list(extra_specs), out_shapes, out_specs,
                   grid, NN, epilogue, (TM, c))


def _mm_bwd_act(name, dy, w, layer, col_sharded, epilogue=_ep_store, extras=(), extra_specs=(), out_dtype=F32):
    s = dy.shape[0]
    r, c = w.shape[-2:]
    if col_sharded:
        grid = (s // TM, 1, N_CHIPS)
        a_spec = pl.BlockSpec((TM, c), lambda i, j, k: (i, k))
        o_spec = pl.BlockSpec((TM, r), lambda i, j, k: (i, 0))
        n_out = r
        w_spec = _wspec(w, layer, 2)
    else:
        grid = (s // TM, N_CHIPS, 1)
        a_spec = pl.BlockSpec((TM, c), lambda i, j, k: (i, 0))
        o_spec = pl.BlockSpec((TM, r), lambda i, j, k: (i, j))
        n_out = N_CHIPS * r
        w_spec = _wspec(w, layer, 1)
    return _matmul(name, (dy, w) + tuple(extras), [a_spec, w_spec] + list(extra_specs),
                   [jax.ShapeDtypeStruct((s, n_out), out_dtype)], [o_spec], grid, NT, epilogue, (TM, r))[0]


def _mm_bwd_w(name, a, dy, buf, layer, col_sharded):
    s = a.shape[0]
    r, c = buf.shape[-2:]
    grid = (N_CHIPS, 1, s // TS)
    if col_sharded:
        a_spec = pl.BlockSpec((TS, r), lambda i, j, k: (k, 0))
        b_spec = pl.BlockSpec((TS, c), lambda i, j, k: (k, i))
    else:
        a_spec = pl.BlockSpec((TS, r), lambda i, j, k: (k, i))
        b_spec = pl.BlockSpec((TS, c), lambda i, j, k: (k, 0))

    def epilogue(p, ins, outs):
        outs[0][...] = p

    return _matmul(name, (a, dy, buf), [a_spec, b_spec, pl.BlockSpec(memory_space=pl.ANY)],
                   [jax.ShapeDtypeStruct(buf.shape, F32)], [_wspec(buf, layer, 0)], grid, TN, epilogue, (r, c),
                   aliases={2: 0})[0]


def _row_spec(n):
    return pl.BlockSpec((TM, n), lambda i, j, k: (i, 0))


def _vec_spec(layer, n):
    return pl.BlockSpec((None, 1, n), lambda i, j, k: (layer, 0, 0))


def _ep_resid_ln(p, ins, outs):
    x_ref, g_ref, b_ref = ins
    xf_ref, xb_ref, xhat_ref, rstd_ref = outs
    r = ALPHA * x_ref[...] + p
    mu = jnp.mean(r, axis=-1, keepdims=True)
    d = r - mu
    var = jnp.mean(d * d, axis=-1, keepdims=True)
    rstd = lax.rsqrt(var + LN_EPS)
    xhat = d * rstd
    y = xhat * g_ref[...] + b_ref[...]
    xf_ref[...] = y
    xb_ref[...] = y.astype(BF16)
    xhat_ref[...] = xhat
    rstd_ref[...] = rstd


def _mm_resid_ln(name, a, w, layer, x, g3, b3, ln_layer):
    d = x.shape[1]
    return _mm_fwd(name, a, w, layer, False, _ep_resid_ln, (x, g3, b3),
                   (_row_spec(d), _vec_spec(ln_layer, d), _vec_spec(ln_layer, d)),
                   outs=[(d, F32), (d, BF16), (d, F32), (1, F32)])


def _ep_relu2(p, ins, outs):
    h = jnp.maximum(p, 0.0)
    outs[0][...] = h.astype(BF16)
    outs[1][...] = (h * h).astype(BF16)


def _ep_scale_q(p, ins, outs):
    outs[0][...] = (p * (HEAD_DIM ** -0.5)).astype(BF16)


def _ep_bf16(p, ins, outs):
    outs[0][...] = p.astype(BF16)


def _ep_relu2_bwd(p, ins, outs):
    outs[0][...] = (p * (2.0 * ins[0][...].astype(F32))).astype(BF16)


def _ep_resid(p, ins, outs):
    outs[0][...] = ALPHA * ins[0][...] + p


def _ep_add(p, ins, outs):
    outs[0][...] = ins[0][...] + p


def _gelu_grad(x):
    c0 = math.sqrt(2.0 / math.pi)
    t = jnp.tanh(c0 * (x + 0.044715 * (x * x * x)))
    return 0.5 * (1.0 + t) + (0.5 * x) * (1.0 - t * t) * (c0 * (1.0 + 3.0 * 0.044715 * (x * x)))


def _cast_bf16(w2d):
    r, c = w2d.shape
    tr = min(r, 512)

    def body(w_ref, o_ref):
        o_ref[...] = w_ref[...].astype(BF16)

    return pl.pallas_call(
        body, name="cast_bf16", grid=(r // tr,),
        in_specs=[pl.BlockSpec((tr, c), lambda i: (i, 0))], out_specs=pl.BlockSpec((tr, c), lambda i: (i, 0)),
        out_shape=jax.ShapeDtypeStruct((r, c), BF16), compiler_params=_params(("parallel",)),
    )(w2d)


def _gmlp_norm_fwd(h, g3, b3, layer):
    s, w2 = h.shape
    w = w2 // 2

    def body(h_ref, g_ref, b_ref, o_ref):
        z = jax.nn.gelu(h_ref[...])
        mu = jnp.mean(z, axis=-1, keepdims=True)
        d = z - mu
        var = jnp.mean(d * d, axis=-1, keepdims=True)
        o_ref[...] = (d * lax.rsqrt(var + LN_EPS) * g_ref[...] + b_ref[...]).astype(BF16)

    vec = pl.BlockSpec((None, 1, w), lambda i: (layer, 0, 0))
    return pl.pallas_call(
        body, name="gmlp_norm_fwd", grid=(s // TM,),
        in_specs=[pl.BlockSpec((TM, w), lambda i: (i, 1)), vec, vec],
        out_specs=pl.BlockSpec((TM, w), lambda i: (i, 0)),
        out_shape=jax.ShapeDtypeStruct((s, w), BF16), compiler_params=_params(("parallel",)),
    )(h, g3, b3)


def _chunk_mask():
    t = lax.broadcasted_iota(jnp.int32, (GMLP_BLOCK, GMLP_BLOCK), 0)
    s = lax.broadcasted_iota(jnp.int32, (GMLP_BLOCK, GMLP_BLOCK), 1)
    return (s // CHUNK) <= (t // CHUNK)


SG_ROWS = 512


def _gate_fwd(h, vn, ws, bst):
    s, w = vn.shape
    gd = w // GMLP_GROUPS

    def body(h_ref, v_ref, ws_ref, bs_ref, o_ref):
        mask = _chunk_mask()
        for g in range(GMLP_GROUPS):
            wm = jnp.where(mask, ws_ref[g], 0.0).astype(BF16)
            bias = bs_ref[:, g:g + 1]
            cols = slice(g * gd, (g + 1) * gd)
            for n in range(SG_ROWS // GMLP_BLOCK):
                rows = slice(n * GMLP_BLOCK, (n + 1) * GMLP_BLOCK)
                sp = _dot(wm, v_ref[rows, cols], NN) + bias
                o_ref[rows, cols] = (jax.nn.gelu(h_ref[rows, cols]) * sp).astype(BF16)

    return pl.pallas_call(
        body, name="gate_fwd", grid=(s // SG_ROWS,),
        in_specs=[pl.BlockSpec((SG_ROWS, w), lambda i: (i, 0)), pl.BlockSpec((SG_ROWS, w), lambda i: (i, 0)),
                  pl.BlockSpec(ws.shape, lambda i: (0, 0, 0)), pl.BlockSpec(bst.shape, lambda i: (0, 0))],
        out_specs=pl.BlockSpec((SG_ROWS, w), lambda i: (i, 0)),
        out_shape=jax.ShapeDtypeStruct((s, w), BF16), compiler_params=_params(("parallel",)),
    )(h, vn, ws, bst)


def _gate_bwd(dgated, h, vn, ws, bst):
    s, w = vn.shape
    gd = w // GMLP_GROUPS
    nsteps = s // SG_ROWS

    def body(dg_ref, h_ref, v_ref, ws_ref, bs_ref, du_ref, dv_ref, dws_ref, dbs_ref, dsum):
        i = pl.program_id(0)

        @pl.when(i == 0)
        def _():
            dws_ref[...] = jnp.zeros_like(dws_ref)
            dsum[...] = jnp.zeros_like(dsum)

        mask = _chunk_mask()
        for g in range(GMLP_GROUPS):
            wm = jnp.where(mask, ws_ref[g], 0.0).astype(BF16)
            bias = bs_ref[:, g:g + 1]
            cols = slice(g * gd, (g + 1) * gd)
            dw = jnp.zeros((GMLP_BLOCK, GMLP_BLOCK), F32)
            dsg = jnp.zeros((GMLP_BLOCK, gd), F32)
            for n in range(SG_ROWS // GMLP_BLOCK):
                rows = slice(n * GMLP_BLOCK, (n + 1) * GMLP_BLOCK)
                vb = v_ref[rows, cols]
                sp = _dot(wm, vb, NN) + bias
                dg = dg_ref[rows, cols]
                du_ref[rows, cols] = dg * sp
                ds = dg * jax.nn.gelu(h_ref[rows, cols])
                dsb = ds.astype(BF16)
                dw += _dot(dsb, vb, NT)
                dsg += ds
                dv_ref[rows, cols] = _dot(wm, dsb, TN)
            dws_ref[g] += dw
            dsum[:, cols] += dsg

        @pl.when(i == nsteps - 1)
        def _():
            for g in range(GMLP_GROUPS):
                dws_ref[g] = jnp.where(mask, dws_ref[g], 0.0)
                tot = jnp.sum(dsum[:, g * gd:(g + 1) * gd], axis=-1, keepdims=True)
                dbs_ref[g] = jnp.broadcast_to(tot, (GMLP_BLOCK, LANES))

    tile = pl.BlockSpec((SG_ROWS, w), lambda i: (i, 0))
    return pl.pallas_call(
        body, name="gate_bwd", grid=(nsteps,),
        in_specs=[tile, tile, tile, pl.BlockSpec(ws.shape, lambda i: (0, 0, 0)), pl.BlockSpec(bst.shape, lambda i: (0, 0))],
        out_specs=[tile, tile, pl.BlockSpec(ws.shape, lambda i: (0, 0, 0)),
                   pl.BlockSpec((GMLP_GROUPS, GMLP_BLOCK, LANES), lambda i: (0, 0, 0))],
        out_shape=[jax.ShapeDtypeStruct((s, w), F32), jax.ShapeDtypeStruct((s, w), F32),
                   jax.ShapeDtypeStruct(ws.shape, F32), jax.ShapeDtypeStruct((GMLP_GROUPS, GMLP_BLOCK, LANES), F32)],
        scratch_shapes=[pltpu.VMEM((GMLP_BLOCK, w), F32)],
        compiler_params=_params(("arbitrary",)),
    )(dgated, h, vn, ws, bst)


GB_ROWS = 256


def _gmlp_in_bwd(h, du, dvn, g3, layer):
    s, w2 = h.shape
    w = w2 // 2
    nsteps = s // GB_ROWS

    def body(h_ref, du_ref, dv_ref, g_ref, dh_ref, dg_ref, db_ref):
        i = pl.program_id(0)

        @pl.when(i == 0)
        def _():
            dg_ref[...] = jnp.zeros_like(dg_ref)
            db_ref[...] = jnp.zeros_like(db_ref)

        hu = h_ref[:, :w]
        hv = h_ref[:, w:]
        dh_ref[:, :w] = (du_ref[...] * _gelu_grad(hu)).astype(BF16)
        z = jax.nn.gelu(hv)
        mu = jnp.mean(z, axis=-1, keepdims=True)
        d = z - mu
        var = jnp.mean(d * d, axis=-1, keepdims=True)
        rstd = lax.rsqrt(var + LN_EPS)
        xhat = d * rstd
        dy = dv_ref[...]
        db_ref[...] += jnp.sum(dy, axis=0, keepdims=True)
        dg_ref[...] += jnp.sum(dy * xhat, axis=0, keepdims=True)
        dxh = dy * g_ref[...]
        m1 = jnp.mean(dxh, axis=-1, keepdims=True)
        m2 = jnp.mean(dxh * xhat, axis=-1, keepdims=True)
        dz = rstd * (dxh - m1 - xhat * m2)
        dh_ref[:, w:] = (dz * _gelu_grad(hv)).astype(BF16)

    half = pl.BlockSpec((GB_ROWS, w), lambda i: (i, 0))
    vec = pl.BlockSpec((1, w), lambda i: (0, 0))
    return pl.pallas_call(
        body, name="gmlp_in_bwd", grid=(nsteps,),
        in_specs=[pl.BlockSpec((GB_ROWS, w2), lambda i: (i, 0)), half, half,
                  pl.BlockSpec((None, 1, w), lambda i: (layer, 0, 0))],
        out_specs=[pl.BlockSpec((GB_ROWS, w2), lambda i: (i, 0)), vec, vec],
        out_shape=[jax.ShapeDtypeStruct((s, w2), BF16), jax.ShapeDtypeStruct((1, w), F32), jax.ShapeDtypeStruct((1, w), F32)],
        compiler_params=_params(("arbitrary",)),
    )(h, du, dvn, g3)


def _ln_bwd(dy, xhat, rstd, g3, layer):
    s, d = dy.shape
    nsteps = s // TM

    def body(dy_ref, xh_ref, rs_ref, g_ref, dr_ref, drb_ref, dg_ref, db_ref):
        i = pl.program_id(0)

        @pl.when(i == 0)
        def _():
            dg_ref[...] = jnp.zeros_like(dg_ref)
            db_ref[...] = jnp.zeros_like(db_ref)

        dyv = dy_ref[...]
        xhat_v = xh_ref[...]
        db_ref[...] += jnp.sum(dyv, axis=0, keepdims=True)
        dg_ref[...] += jnp.sum(dyv * xhat_v, axis=0, keepdims=True)
        dxh = dyv * g_ref[...]
        m1 = jnp.mean(dxh, axis=-1, keepdims=True)
        m2 = jnp.mean(dxh * xhat_v, axis=-1, keepdims=True)
        dr = rs_ref[...] * (dxh - m1 - xhat_v * m2)
        dr_ref[...] = dr
        drb_ref[...] = dr.astype(BF16)

    tile = pl.BlockSpec((TM, d), lambda i: (i, 0))
    vec = pl.BlockSpec((1, d), lambda i: (0, 0))
    return pl.pallas_call(
        body, name="ln_bwd", grid=(nsteps,),
        in_specs=[tile, tile, pl.BlockSpec((TM, 1), lambda i: (i, 0)), pl.BlockSpec((None, 1, d), lambda i: (layer, 0, 0))],
        out_specs=[tile, tile, vec, vec],
        out_shape=[jax.ShapeDtypeStruct((s, d), F32), jax.ShapeDtypeStruct((s, d), BF16),
                   jax.ShapeDtypeStruct((1, d), F32), jax.ShapeDtypeStruct((1, d), F32)],
        compiler_params=_params(("arbitrary",)),
    )(dy, xhat, rstd, g3)


def _loss_head(y, target):
    s, d = y.shape

    def body(y_ref, t_ref, dy_ref, l_ref):
        i = pl.program_id(0)

        @pl.when(i == 0)
        def _():
            l_ref[...] = jnp.zeros_like(l_ref)

        e = y_ref[...] - t_ref[...]
        dy_ref[...] = e * (1.0 / d)
        l_ref[...] += jnp.sum(jnp.sum(e * e, axis=1, keepdims=True), axis=0, keepdims=True)

    tile = pl.BlockSpec((TM, d), lambda i: (i, 0))
    return pl.pallas_call(
        body, name="loss_head", grid=(s // TM,), in_specs=[tile, tile],
        out_specs=[tile, pl.BlockSpec((1, 1), lambda i: (0, 0))],
        out_shape=[jax.ShapeDtypeStruct((s, d), F32), jax.ShapeDtypeStruct((1, 1), F32)],
        compiler_params=_params(("arbitrary",)),
    )(y, target)


def _sb_terms(z, causal):
    e = jnp.exp(-jnp.abs(z))
    l1p = jnp.log(1.0 + e)
    lb = jnp.minimum(z, 0.0) - l1p
    lr = lb - z
    if causal is not None:
        lr = jnp.where(causal, lr, 0.0)
    return lb, lr, e


def _running_sum(x, tri):
    hi = x.astype(BF16)
    lo = (x - hi.astype(F32)).astype(BF16)
    return _dot(hi, tri, NN) + _dot(lo, tri, NN)


def _att_consts(prefix):
    r = lax.broadcasted_iota(jnp.int32, (ATT_T, ATT_T), 0)
    c = lax.broadcasted_iota(jnp.int32, (ATT_T, ATT_T), 1)
    tri = jnp.where((r <= c) if prefix else (r >= c), 1.0, 0.0).astype(BF16)
    causal = c < r
    head_a = lax.broadcasted_iota(jnp.int32, (1, LANES), 1) < HEAD_DIM
    return tri, causal, head_a


def _attn_fwd(q, k, v):
    s, d = q.shape
    nq = s // ATT_T

    def body(q_ref, k_ref, v_ref, ob_ref, lsum_ref, acc_a, acc_b, rem_a, rem_b):
        i = pl.program_id(1)
        tri, causal, head_a = _att_consts(prefix=False)
        q2 = q_ref[...]
        zero = jnp.zeros_like(q2)
        qa = jnp.where(head_a, q2, zero)
        qb = jnp.where(head_a, zero, q2)
        acc_a[...] = jnp.zeros_like(acc_a)
        acc_b[...] = jnp.zeros_like(acc_b)
        rem_a[...] = jnp.zeros_like(rem_a)
        rem_b[...] = jnp.zeros_like(rem_b)

        def block(kb, mask):
            rows = pl.ds(pl.multiple_of(kb * ATT_T, ATT_T), ATT_T)
            k2 = k_ref[rows, :]
            v2 = v_ref[rows, :]
            for qm, acc, rem in ((qa, acc_a, rem_a), (qb, acc_b, rem_b)):
                lb, lr, _ = _sb_terms(_dot(qm, k2, NT), mask)
                sincl = _running_sum(lr, tri)
                a = jnp.exp(lb + (sincl - lr) + rem[...])
                if mask is not None:
                    a = jnp.where(mask, a, 0.0)
                rem[...] += sincl[:, 0:1]
                acc[...] += _dot(a.astype(BF16), v2, NN)

        block(i, causal)

        def step(t, carry):
            block(i - 1 - t, None)
            return carry

        lax.fori_loop(0, i, step, 0)
        ob_ref[...] = jnp.where(head_a, acc_a[...], acc_b[...]).astype(BF16)
        lsum_ref[...] = jnp.where(head_a, rem_a[...], rem_b[...])

    qspec = pl.BlockSpec((ATT_T, LANES), lambda p, i: (i, p))
    kspec = pl.BlockSpec((s, LANES), lambda p, i: (0, p))
    return pl.pallas_call(
        body, name="attn_fwd", grid=(d // LANES, nq), in_specs=[qspec, kspec, kspec],
        out_specs=[qspec, qspec],
        out_shape=[jax.ShapeDtypeStruct((s, d), BF16), jax.ShapeDtypeStruct((s, d), F32)],
        scratch_shapes=[pltpu.VMEM((ATT_T, LANES), F32), pltpu.VMEM((ATT_T, LANES), F32),
                        pltpu.VMEM((ATT_T, 1), F32), pltpu.VMEM((ATT_T, 1), F32)],
        compiler_params=_params(("parallel", "arbitrary")),
    )(q, k, v)


def _attn_bwd(q, k, v, do, lsum, dk_prev=None, dv_prev=None):
    s, d = q.shape
    nq = s // ATT_T
    has_prev = dk_prev is not None

    def body(*refs):
        q_ref, k_ref, v_ref, do_ref, ls_ref = refs[:5]
        n_in = 7 if has_prev else 5
        dq_ref, dk_ref, dv_ref, acc_a, acc_b, pre_a, pre_b, gp_a, gp_b = refs[n_in:]
        i = pl.program_id(1)

        @pl.when(i == 0)
        def _():
            if has_prev:
                dk_ref[...] = refs[5][...]
                dv_ref[...] = refs[6][...]
            else:
                dk_ref[...] = jnp.zeros_like(dk_ref)
                dv_ref[...] = jnp.zeros_like(dv_ref)

        tri, causal, head_a = _att_consts(prefix=True)
        q2 = q_ref[...]
        zero = jnp.zeros_like(q2)
        qa = jnp.where(head_a, q2, zero)
        qb = jnp.where(head_a, zero, q2)
        do2 = do_ref[...]
        doa = jnp.where(head_a, do2, 0.0).astype(BF16)
        dob = jnp.where(head_a, 0.0, do2).astype(BF16)
        ls2 = ls_ref[...]
        tot_a = ls2[:, 0:1]
        tot_b = ls2[:, HEAD_DIM:HEAD_DIM + 1]
        for r in (acc_a, acc_b, pre_a, pre_b, gp_a, gp_b):
            r[...] = jnp.zeros_like(r)

        def block(kb, mask):
            rows = pl.ds(pl.multiple_of(kb * ATT_T, ATT_T), ATT_T)
            k2 = k_ref[rows, :]
            v2 = v_ref[rows, :]
            dk_new = jnp.zeros((ATT_T, LANES), F32)
            dv_new = jnp.zeros((ATT_T, LANES), F32)
            for qm, dom, tot, acc, pre, gpre in ((qa, doa, tot_a, acc_a, pre_a, gp_a), (qb, dob, tot_b, acc_b, pre_b, gp_b)):
                z = _dot(qm, k2, NT)
                lb, lr, e = _sb_terms(z, mask)
                pincl = _running_sum(lr, tri)
                a = jnp.exp(lb + (tot - (pre[...] + pincl)))
                if mask is not None:
                    a = jnp.where(mask, a, 0.0)
                pre[...] += pincl[:, ATT_T - 1:ATT_T]
                g = a * _dot(dom, v2, NT)
                gincl = _running_sum(g, tri)
                gbefore = gpre[...] + (gincl - g)
                gpre[...] += gincl[:, ATT_T - 1:ATT_T]
                inv = 1.0 / (1.0 + e)
                beta = jnp.where(z >= 0.0, inv, e * inv)
                dz = g - beta * (g + gbefore)
                if mask is not None:
                    dz = jnp.where(mask, dz, 0.0)
                ab = a.astype(BF16)
                dzb = dz.astype(BF16)
                dv_new += _dot(ab, dom, TN)
                dk_new += _dot(dzb, qm, TN)
                acc[...] += _dot(dzb, k2, NN)
            dk_ref[rows, :] += dk_new
            dv_ref[rows, :] += dv_new

        def step(kb, carry):
            block(kb, None)
            return carry

        lax.fori_loop(0, i, step, 0)
        block(i, causal)
        dq_ref[...] = (jnp.where(head_a, acc_a[...], acc_b[...]) * (HEAD_DIM ** -0.5)).astype(BF16)

    qspec = pl.BlockSpec((ATT_T, LANES), lambda p, i: (i, p))
    kspec = pl.BlockSpec((s, LANES), lambda p, i: (0, p))
    ins = [q, k, v, do, lsum] + ([dk_prev, dv_prev] if has_prev else [])
    return pl.pallas_call(
        body, name="attn_bwd", grid=(d // LANES, nq),
        in_specs=[qspec, kspec, kspec, qspec, qspec] + ([kspec, kspec] if has_prev else []),
        out_specs=[qspec, kspec, kspec],
        out_shape=[jax.ShapeDtypeStruct((s, d), BF16), jax.ShapeDtypeStruct((s, d), F32), jax.ShapeDtypeStruct((s, d), F32)],
        scratch_shapes=[pltpu.VMEM((ATT_T, LANES), F32), pltpu.VMEM((ATT_T, LANES), F32)]
        + [pltpu.VMEM((ATT_T, 1), F32)] * 4,
        compiler_params=_params(("parallel", "arbitrary")),
    )(*ins)


def _place():
    x, y, c = lax.axis_index("x"), lax.axis_index("y"), lax.axis_index("c")
    chips = [(1 - x, y), (x, 1 - y), (1 - x, 1 - y)]
    return x, y, c, chips


def _any_specs(n):
    return [pl.BlockSpec(memory_space=pl.ANY)] * n


def _gather_weights(shards):
    n = len(shards)

    def body(*refs):
        ins, outs = refs[:n], refs[n:2 * n]
        send_sems, recv_sems, loc_sems = refs[2 * n:]
        x, y, c, chips = _place()
        me = 2 * x + y
        sibling = (x, y, 1 - c)

        def half(a, blk, hc):
            h = ins[a].shape[0] // 2
            return outs[a].at[blk, pl.ds(hc * h, h)]

        def copy(a, k, src, dst, to):
            return pltpu.make_async_remote_copy(src_ref=src, dst_ref=dst, send_sem=send_sems.at[a, k],
                                                recv_sem=recv_sems.at[a, k], device_id=to, device_id_type=MESH)

        local = [pltpu.make_async_copy(ins[a], outs[a].at[me], loc_sems.at[a]) for a in range(n)]
        for cp in local:
            cp.start()
        sent = []
        for a in range(n):
            h = ins[a].shape[0] // 2
            for k, chip in enumerate(chips):
                sent.append(copy(a, k, ins[a].at[pl.ds(c * h, h)], half(a, me, c), (*chip, c)))
                sent[-1].start()
        for a in range(n):
            for k, chip in enumerate(chips):
                blk = 2 * chip[0] + chip[1]
                copy(a, k, half(a, blk, c), half(a, blk, c), sibling).wait_recv()
                sent.append(copy(a, 3 + k, half(a, blk, c), half(a, blk, c), sibling))
                sent[-1].start()
        for a in range(n):
            for k, chip in enumerate(chips):
                blk = 2 * chip[0] + chip[1]
                copy(a, 3 + k, half(a, blk, 1 - c), half(a, blk, 1 - c), sibling).wait_recv()
        for cp in sent:
            cp.wait_send()
        for cp in local:
            cp.wait()

    return pl.pallas_call(
        body, name="gather_weights", in_specs=_any_specs(n), out_specs=_any_specs(n),
        out_shape=[jax.ShapeDtypeStruct((N_CHIPS,) + w.shape, w.dtype) for w in shards],
        scratch_shapes=[pltpu.SemaphoreType.DMA((n, 6)), pltpu.SemaphoreType.DMA((n, 6)), pltpu.SemaphoreType.DMA((n,))],
        compiler_params=pltpu.CompilerParams(has_side_effects=True),
    )(*shards)


def _pair_exchange(grads):
    n = len(grads)

    def body(*refs):
        ins, outs = refs[:n], refs[n:2 * n]
        send_sems, recv_sems = refs[2 * n:]
        x, y, c, _ = _place()
        cps = []
        for a in range(n):
            h = ins[a].shape[1] // 2
            cps.append(pltpu.make_async_remote_copy(
                src_ref=ins[a].at[:, pl.ds((1 - c) * h, h)], dst_ref=outs[a], send_sem=send_sems.at[a],
                recv_sem=recv_sems.at[a], device_id=(x, y, 1 - c), device_id_type=MESH))
            cps[-1].start()
        for cp in cps:
            cp.wait()

    return pl.pallas_call(
        body, name="pair_exchange", in_specs=_any_specs(n), out_specs=_any_specs(n),
        out_shape=[jax.ShapeDtypeStruct((g.shape[0], g.shape[1] // 2, g.shape[2]), g.dtype) for g in grads],
        scratch_shapes=[pltpu.SemaphoreType.DMA((n,)), pltpu.SemaphoreType.DMA((n,))],
        compiler_params=pltpu.CompilerParams(has_side_effects=True),
    )(*grads)


def _chip_exchange(parts):
    n = len(parts)

    def body(*refs):
        ins, outs = refs[:n], refs[n:2 * n]
        send_sems, recv_sems = refs[2 * n:]
        x, y, c, chips = _place()
        me = 2 * x + y
        cps = []
        for a in range(n):
            for k, chip in enumerate(chips):
                blk = 2 * chip[0] + chip[1]
                cps.append(pltpu.make_async_remote_copy(
                    src_ref=ins[a].at[blk], dst_ref=outs[a].at[me], send_sem=send_sems.at[a, k],
                    recv_sem=recv_sems.at[a, k], device_id=(*chip, c), device_id_type=MESH))
                cps[-1].start()
        for a in range(n):
            for k, chip in enumerate(chips):
                blk = 2 * chip[0] + chip[1]
                pltpu.make_async_remote_copy(
                    src_ref=ins[a].at[blk], dst_ref=outs[a].at[blk], send_sem=send_sems.at[a, k],
                    recv_sem=recv_sems.at[a, k], device_id=(*chip, c), device_id_type=MESH).wait_recv()
        for cp in cps:
            cp.wait_send()

    return pl.pallas_call(
        body, name="chip_exchange", in_specs=_any_specs(n), out_specs=_any_specs(n),
        out_shape=[jax.ShapeDtypeStruct(p.shape, p.dtype) for p in parts],
        scratch_shapes=[pltpu.SemaphoreType.DMA((n, 3)), pltpu.SemaphoreType.DMA((n, 3))],
        compiler_params=pltpu.CompilerParams(has_side_effects=True),
    )(*parts)


def _half_swap(halves):
    n = len(halves)

    def body(*refs):
        ins, outs = refs[:n], refs[n:2 * n]
        send_sems, recv_sems, loc_sems = refs[2 * n:]
        x, y, c, _ = _place()
        cps = []
        for a in range(n):
            h = ins[a].shape[0]
            mine = outs[a].at[pl.ds(c * h, h)]
            cps.append(pltpu.make_async_copy(ins[a], mine, loc_sems.at[a]))
            cps[-1].start()
            cps.append(pltpu.make_async_remote_copy(
                src_ref=ins[a], dst_ref=mine, send_sem=send_sems.at[a], recv_sem=recv_sems.at[a],
                device_id=(x, y, 1 - c), device_id_type=MESH))
            cps[-1].start()
        for cp in cps:
            cp.wait()

    return pl.pallas_call(
        body, name="half_swap", in_specs=_any_specs(n), out_specs=_any_specs(n),
        out_shape=[jax.ShapeDtypeStruct((2 * p.shape[0], p.shape[1]), p.dtype) for p in halves],
        scratch_shapes=[pltpu.SemaphoreType.DMA((n,)), pltpu.SemaphoreType.DMA((n,)), pltpu.SemaphoreType.DMA((n,))],
        compiler_params=pltpu.CompilerParams(has_side_effects=True),
    )(*halves)


N_DEV = 8


def _all_reduce_small(v):
    nrow, ncol = v.shape

    def body(v_ref, o_ref, land, red, send_sems, recv_sems, send2, recv2, loc_sem):
        x, y, c, _ = _place()
        me = 4 * x + 2 * y + c
        peers = []
        for k in range(1, N_DEV):
            peers.append((x ^ ((k >> 2) & 1), y ^ ((k >> 1) & 1), c ^ (k & 1)))
        own = pltpu.make_async_copy(v_ref.at[pl.ds(me, 1)], land.at[pl.ds(me, 1)], loc_sem)
        own.start()
        cps = []
        for k, peer in enumerate(peers):
            dev = 4 * peer[0] + 2 * peer[1] + peer[2]
            cps.append(pltpu.make_async_remote_copy(
                src_ref=v_ref.at[pl.ds(dev, 1)], dst_ref=land.at[pl.ds(me, 1)], send_sem=send_sems.at[k],
                recv_sem=recv_sems.at[k], device_id=peer, device_id_type=MESH))
            cps[-1].start()
        for k, peer in enumerate(peers):
            dev = 4 * peer[0] + 2 * peer[1] + peer[2]
            pltpu.make_async_remote_copy(
                src_ref=v_ref.at[pl.ds(dev, 1)], dst_ref=land.at[pl.ds(dev, 1)], send_sem=send_sems.at[k],
                recv_sem=recv_sems.at[k], device_id=peer, device_id_type=MESH).wait_recv()
        for cp in cps:
            cp.wait_send()
        own.wait()
        terms = land[...]
        total = terms[0:1, :]
        for d in range(1, N_DEV):
            total = total + terms[d:d + 1, :]
        red[...] = total
        own = pltpu.make_async_copy(red, o_ref.at[pl.ds(me, 1)], loc_sem)
        own.start()
        cps = []
        for k, peer in enumerate(peers):
            cps.append(pltpu.make_async_remote_copy(
                src_ref=red, dst_ref=o_ref.at[pl.ds(me, 1)], send_sem=send2.at[k],
                recv_sem=recv2.at[k], device_id=peer, device_id_type=MESH))
            cps[-1].start()
        for k, peer in enumerate(peers):
            dev = 4 * peer[0] + 2 * peer[1] + peer[2]
            pltpu.make_async_remote_copy(
                src_ref=red, dst_ref=o_ref.at[pl.ds(dev, 1)], send_sem=send2.at[k],
                recv_sem=recv2.at[k], device_id=peer, device_id_type=MESH).wait_recv()
        for cp in cps:
            cp.wait_send()
        own.wait()

    vm = pl.BlockSpec(memory_space=pltpu.VMEM)
    return pl.pallas_call(
        body, name="all_reduce_small", in_specs=[vm], out_specs=vm,
        out_shape=jax.ShapeDtypeStruct((nrow, ncol), F32),
        scratch_shapes=[pltpu.VMEM((nrow, ncol), F32), pltpu.VMEM((1, ncol), F32)]
        + [pltpu.SemaphoreType.DMA((N_DEV - 1,))] * 4 + [pltpu.SemaphoreType.DMA],
        compiler_params=pltpu.CompilerParams(has_side_effects=True, vmem_limit_bytes=VMEM_LIMIT),
    )(v)


def _row_tile(rows):
    return min(rows, 512)


def _pair_sum(g, got, core):
    nb, r, c = g.shape
    h = r // 2
    tr = _row_tile(h)
    nt = h // tr

    def body(core_ref, g_ref, got_ref, p_ref, pb_ref):
        p = g_ref[...] + got_ref[...]
        p_ref[...] = p
        pb_ref[...] = p.astype(BF16)

    spec = pl.BlockSpec((None, tr, c), lambda j, t, core_ref: (j, t, 0))
    grid_spec = pltpu.PrefetchScalarGridSpec(
        num_scalar_prefetch=1, grid=(nb, nt),
        in_specs=[pl.BlockSpec((None, tr, c), lambda j, t, core_ref: (j, core_ref[0] * nt + t, 0)), spec],
        out_specs=[spec, spec])
    return pl.pallas_call(
        body, name="pair_sum", grid_spec=grid_spec,
        out_shape=[jax.ShapeDtypeStruct((nb, h, c), F32), jax.ShapeDtypeStruct((nb, h, c), BF16)],
        compiler_params=_params(("parallel", "parallel")),
    )(core, g, got)


def _chip_sum(p, got, chip):
    nb, h, c = p.shape
    tr = _row_tile(h)

    def body(chip_ref, p_ref, g1_ref, g2_ref, g3_ref, o_ref):
        o_ref[...] = ((p_ref[...] + g1_ref[...].astype(F32)) + g2_ref[...].astype(F32)) + g3_ref[...].astype(F32)

    def blk(off):
        return pl.BlockSpec((None, tr, c), lambda t, chip_ref: ((chip_ref[0] + off) % N_CHIPS, t, 0))

    grid_spec = pltpu.PrefetchScalarGridSpec(
        num_scalar_prefetch=1, grid=(h // tr,), in_specs=[blk(0), blk(1), blk(2), blk(3)],
        out_specs=pl.BlockSpec((tr, c), lambda t, chip_ref: (t, 0)))
    return pl.pallas_call(
        body, name="chip_sum", grid_spec=grid_spec, out_shape=jax.ShapeDtypeStruct((h, c), F32),
        compiler_params=_params(("parallel",)),
    )(chip, p, got, got, got)


def _adamw(w, g, m, v):
    r, c = w.shape
    tr = r if r < 8 else _row_tile(r)

    def body(w_ref, g_ref, m_ref, v_ref, d_ref, nm_ref, nv_ref):
        gv = g_ref[...]
        nm = ADAM_B1 * m_ref[...] + (1.0 - ADAM_B1) * gv
        nv = ADAM_B2 * v_ref[...] + (1.0 - ADAM_B2) * (gv * gv)
        m_hat = nm / (1.0 - ADAM_B1 ** ADAM_STEP)
        v_hat = nv / (1.0 - ADAM_B2 ** ADAM_STEP)
        d_ref[...] = -ADAM_LR * (m_hat / (jnp.sqrt(v_hat) + ADAM_EPS) + ADAM_WD * w_ref[...])
        nm_ref[...] = nm
        nv_ref[...] = nv

    tile = pl.BlockSpec((tr, c), lambda i: (i, 0))
    return pl.pallas_call(
        body, name="adamw", grid=(r // tr,), in_specs=[tile] * 4, out_specs=[tile] * 3,
        out_shape=[jax.ShapeDtypeStruct((r, c), F32)] * 3, compiler_params=_params(("parallel",)),
    )(w, g, m, v)


BIG = ("a_w_in", "a_w_out", "sb_w_k", "sb_w_v", "b_w_q", "b_w_o", "ffn_w1", "ffn_w2")
SMALL = ("a_ln_g", "a_ln_b", "a_w_s", "a_b_s", "mix_ln_g", "mix_ln_b", "ffn_ln_g", "ffn_ln_b")
COL_SHARDED = {"a_w_in": True, "a_w_out": False, "sb_w_k": False, "sb_w_v": False, "b_w_q": False, "b_w_o": False,
               "ffn_w1": True, "ffn_w2": False}


def kernel(x, a_w_in, a_ln_g, a_ln_b, a_w_s, a_b_s, a_w_out, sb_w_k, sb_w_v, b_w_q, b_w_o, mix_ln_g, mix_ln_b, ffn_ln_g, ffn_ln_b, ffn_w1, ffn_w2, loss_target, m_a_w_in, m_a_ln_g, m_a_ln_b, m_a_w_s, m_a_b_s, m_a_w_out, m_sb_w_k, m_sb_w_v, m_b_w_q, m_b_w_o, m_mix_ln_g, m_mix_ln_b, m_ffn_ln_g, m_ffn_ln_b, m_ffn_w1, m_ffn_w2, v_a_w_in, v_a_ln_g, v_a_ln_b, v_a_w_s, v_a_b_s, v_a_w_out, v_sb_w_k, v_sb_w_v, v_b_w_q, v_b_w_o, v_mix_ln_g, v_mix_ln_b, v_ffn_ln_g, v_ffn_ln_b, v_ffn_w1, v_ffn_w2):
    names = BIG + SMALL
    given = dict(a_w_in=a_w_in, a_ln_g=a_ln_g, a_ln_b=a_ln_b, a_w_s=a_w_s, a_b_s=a_b_s, a_w_out=a_w_out, sb_w_k=sb_w_k,
                 sb_w_v=sb_w_v, b_w_q=b_w_q, b_w_o=b_w_o, mix_ln_g=mix_ln_g, mix_ln_b=mix_ln_b, ffn_ln_g=ffn_ln_g,
                 ffn_ln_b=ffn_ln_b, ffn_w1=ffn_w1, ffn_w2=ffn_w2)
    mom = dict(a_w_in=m_a_w_in, a_ln_g=m_a_ln_g, a_ln_b=m_a_ln_b, a_w_s=m_a_w_s, a_b_s=m_a_b_s, a_w_out=m_a_w_out,
               sb_w_k=m_sb_w_k, sb_w_v=m_sb_w_v, b_w_q=m_b_w_q, b_w_o=m_b_w_o, mix_ln_g=m_mix_ln_g, mix_ln_b=m_mix_ln_b,
               ffn_ln_g=m_ffn_ln_g, ffn_ln_b=m_ffn_ln_b, ffn_w1=m_ffn_w1, ffn_w2=m_ffn_w2)
    var = dict(a_w_in=v_a_w_in, a_ln_g=v_a_ln_g, a_ln_b=v_a_ln_b, a_w_s=v_a_w_s, a_b_s=v_a_b_s, a_w_out=v_a_w_out,
               sb_w_k=v_sb_w_k, sb_w_v=v_sb_w_v, b_w_q=v_b_w_q, b_w_o=v_b_w_o, mix_ln_g=v_mix_ln_g, mix_ln_b=v_mix_ln_b,
               ffn_ln_g=v_ffn_ln_g, ffn_ln_b=v_ffn_ln_b, ffn_w1=v_ffn_w1, ffn_w2=v_ffn_w2)

    cx, cy, cc = lax.axis_index("x"), lax.axis_index("y"), lax.axis_index("c")
    chip = (2 * cx + cy).astype(jnp.int32)
    chip_arr = chip.reshape(1)
    core_arr = cc.astype(jnp.int32).reshape(1)

    s, d = x.shape[1], x.shape[2]
    xf = x.reshape(s, d)
    target = loss_target.reshape(s, d)

    def as2d(w):
        return w.reshape(-1, w.shape[-1])

    shards = [_cast_bf16(as2d(given[n])).reshape(given[n].shape) for n in BIG]
    ln_gb = jnp.stack([a_ln_g, a_ln_b])
    gathered = _gather_weights(shards + [ln_gb])
    gw = dict(zip(BIG, gathered[:-1]))
    ln_full = gathered[-1].transpose(1, 2, 0, 3).reshape(2, N_A, 1, -1)
    a_ln_g3, a_ln_b3 = ln_full[0], ln_full[1]
    mix_g3, mix_b3 = mix_ln_g[:, None, :], mix_ln_b[:, None, :]
    ffn_g3, ffn_b3 = ffn_ln_g[:, None, :], ffn_ln_b[:, None, :]
    bst = jnp.swapaxes(a_b_s, 1, 2)

    saved = []
    xb = _cast_bf16(xf)
    kb = vb = None
    for l in range(DEPTH):
        sv = dict(x_in=xb)
        if l < N_A:
            h = _mm_fwd("a_in", xb, gw["a_w_in"], l, True)[0]
            vn = _gmlp_norm_fwd(h, a_ln_g3, a_ln_b3, l)
            gated = _gate_fwd(h, vn, a_w_s[l], bst[l])
            xf, xb, xhat, rstd = _mm_resid_ln("a_out", gated, gw["a_w_out"], l, xf, mix_g3, mix_b3, l)
            sv.update(h=h, vn=vn, gated=gated)
        else:
            j = l - N_A
            if l == N_A:
                kb = _mm_fwd("sb_k", xb, gw["sb_w_k"], None, False, _ep_bf16, outs=[(d, BF16)])[0]
                vb = _mm_fwd("sb_v", xb, gw["sb_w_v"], None, False, _ep_bf16, outs=[(d, BF16)])[0]
            q = _mm_fwd("b_q", xb, gw["b_w_q"], j, False, _ep_scale_q, outs=[(d, BF16)])[0]
            ob, lsum = _attn_fwd(q, kb, vb)
            xf, xb, xhat, rstd = _mm_resid_ln("b_out", ob, gw["b_w_o"], j, xf, mix_g3, mix_b3, l)
            sv.update(q=q, lsum=lsum, ob=ob)
        sv.update(x_mid=xb, xhat1=xhat, rstd1=rstd)
        dff = gw["ffn_w1"].shape[-1] * N_CHIPS
        pr, act = _mm_fwd("ffn_1", xb, gw["ffn_w1"], l, True, _ep_relu2, outs=[(dff, BF16), (dff, BF16)])
        xf, xb, xhat, rstd = _mm_resid_ln("ffn_2", act, gw["ffn_w2"], l, xf, ffn_g3, ffn_b3, l)
        sv.update(pr=pr, act=act, xhat2=xhat, rstd2=rstd)
        saved.append(sv)

    dx, sq = _loss_head(xf, target)
    loss = lax.psum(0.5 * sq[0, 0] / d, ("x", "y", "c"))

    gbuf = {n: lax.empty((N_CHIPS,) + given[n].shape, F32) for n in BIG}
    small = {}
    d_mix_g, d_mix_b, d_ffn_g, d_ffn_b = [None] * DEPTH, [None] * DEPTH, [None] * DEPTH, [None] * DEPTH
    d_ln_g, d_ln_b, d_ws, d_bs = [None] * N_A, [None] * N_A, [None] * N_A, [None] * N_A
    dk = dv = None
    for l in reversed(range(DEPTH)):
        sv = saved[l]
        dr, drb, d_ffn_g[l], d_ffn_b[l] = _ln_bwd(dx, sv["xhat2"], sv["rstd2"], ffn_g3, l)
        dff = sv["pr"].shape[1]
        dhd = _mm_bwd_act("ffn_2_dx", drb, gw["ffn_w2"], l, False, _ep_relu2_bwd, (sv["pr"],),
                          (pl.BlockSpec((TM, dff // N_CHIPS), lambda i, j, k: (i, j)),), out_dtype=BF16)
        gbuf["ffn_w2"] = _mm_bwd_w("ffn_2_dw", sv["act"], drb, gbuf["ffn_w2"], l, False)
        dx = _mm_bwd_act("ffn_1_dx", dhd, gw["ffn_w1"], l, True, _ep_resid, (dr,), (_row_spec(d),))
        gbuf["ffn_w1"] = _mm_bwd_w("ffn_1_dw", sv["x_mid"], dhd, gbuf["ffn_w1"], l, True)

        dr, drb, d_mix_g[l], d_mix_b[l] = _ln_bwd(dx, sv["xhat1"], sv["rstd1"], mix_g3, l)
        quarter = pl.BlockSpec((TM, d // N_CHIPS), lambda i, j, k: (i, j))
        if l < N_A:
            dgated = _mm_bwd_act("a_out_dx", drb, gw["a_w_out"], l, False)
            gbuf["a_w_out"] = _mm_bwd_w("a_out_dw", sv["gated"], drb, gbuf["a_w_out"], l, False)
            du, dvn, d_ws[l], dbs_wide = _gate_bwd(dgated, sv["h"], sv["vn"], a_w_s[l], bst[l])
            d_bs[l] = dbs_wide[:, :, 0]
            dh, dlg, dlb = _gmlp_in_bwd(sv["h"], du, dvn, a_ln_g3, l)
            d_ln_g[l], d_ln_b[l] = dlg[0], dlb[0]
            dx = _mm_bwd_act("a_in_dx", dh, gw["a_w_in"], l, True, _ep_resid, (dr,), (_row_spec(d),))
            gbuf["a_w_in"] = _mm_bwd_w("a_in_dw", sv["x_in"], dh, gbuf["a_w_in"], l, True)
        else:
            j = l - N_A
            do = _mm_bwd_act("b_out_dx", drb, gw["b_w_o"], j, False)
            gbuf["b_w_o"] = _mm_bwd_w("b_out_dw", sv["ob"], drb, gbuf["b_w_o"], j, False)
            dq, dk, dv = _attn_bwd(sv["q"], kb, vb, do, sv["lsum"], dk, dv)
            dx = _mm_bwd_act("b_q_dx", dq, gw["b_w_q"], j, False, _ep_resid, (dr,), (quarter,))
            gbuf["b_w_q"] = _mm_bwd_w("b_q_dw", sv["x_in"], dq, gbuf["b_w_q"], j, False)
            if l == N_A:
                dx = _mm_bwd_act("sb_k_dx", dk, gw["sb_w_k"], None, False, _ep_add, (dx,), (quarter,))
                gbuf["sb_w_k"] = _mm_bwd_w("sb_k_dw", sv["x_in"], dk, gbuf["sb_w_k"], None, False)
                dx = _mm_bwd_act("sb_v_dx", dv, gw["sb_w_v"], None, False, _ep_add, (dx,), (quarter,))
                gbuf["sb_w_v"] = _mm_bwd_w("sb_v_dw", sv["x_in"], dv, gbuf["sb_w_v"], None, False)
    grad_x = dx.reshape(x.shape)

    flat = [gbuf[n].reshape(N_CHIPS, -1, gbuf[n].shape[-1]) for n in BIG]
    got = _pair_exchange(flat)
    sums = [_pair_sum(g, r, core_arr) for g, r in zip(flat, got)]
    landed = _chip_exchange([pb for _, pb in sums])
    halves = [_chip_sum(p, r, chip_arr) for (p, _), r in zip(sums, landed)]
    grads = dict(zip(BIG, [g.reshape(given[n].shape) for n, g in zip(BIG, _half_swap(halves))]))

    small_full = dict(a_ln_g=jnp.stack(d_ln_g), a_ln_b=jnp.stack(d_ln_b), a_w_s=jnp.stack(d_ws), a_b_s=jnp.stack(d_bs),
                      mix_ln_g=jnp.concatenate(d_mix_g), mix_ln_b=jnp.concatenate(d_mix_b),
                      ffn_ln_g=jnp.concatenate(d_ffn_g), ffn_ln_b=jnp.concatenate(d_ffn_b))
    packed = jnp.concatenate([small_full[n].reshape(-1) for n in SMALL])
    total = packed.shape[0]
    ncol = -(-total // (N_DEV * LANES)) * LANES
    packed = jnp.pad(packed, (0, N_DEV * ncol - total)).reshape(N_DEV, ncol)
    reduced = _all_reduce_small(packed).reshape(-1)
    off = 0
    for n in SMALL:
        size = small_full[n].size
        g = reduced[off:off + size].reshape(small_full[n].shape)
        off += size
        if n in ("a_ln_g", "a_ln_b"):
            wq = given[n].shape[1]
            g = lax.dynamic_slice_in_dim(g, chip * wq, wq, axis=1)
        grads[n] = g

    delta, new_m, new_v = {}, {}, {}
    for n in names:
        shape = given[n].shape
        dl, nm, nv = _adamw(as2d(given[n]), as2d(grads[n]), as2d(mom[n]), as2d(var[n]))
        delta[n], new_m[n], new_v[n] = dl.reshape(shape), nm.reshape(shape), nv.reshape(shape)

    order = ("a_w_in", "a_ln_g", "a_ln_b", "a_w_s", "a_b_s", "a_w_out", "sb_w_k", "sb_w_v", "b_w_q", "b_w_o",
             "mix_ln_g", "mix_ln_b", "ffn_ln_g", "ffn_ln_b", "ffn_w1", "ffn_w2")
    return (loss, grad_x, *[grads[n] for n in order], *[delta[n] for n in order],
            *[new_m[n] for n in order], *[new_v[n] for n in order])
```

```python
import math

import jax
import jax.numpy as jnp
from jax import lax
from jax.experimental import pallas as pl
from jax.experimental.pallas import tpu as pltpu

F32 = jnp.float32
BF16 = jnp.bfloat16
MESH = pl.DeviceIdType.MESH

N_CHIPS = 4
DEPTH = 4
N_A = 2
ALPHA = float((2 * DEPTH) ** 0.25)
LN_EPS = 1e-5
CHUNK = 64
GMLP_BLOCK = 128
GMLP_GROUPS = 8
HEAD_DIM = 64
LANES = 128
ATT_T = 256
ADAM_LR = 0.001
ADAM_B1 = 0.9
ADAM_B2 = 0.999
ADAM_EPS = 1e-08
ADAM_WD = 0.01
ADAM_STEP = 10
VMEM_LIMIT = 56 * 1024 * 1024
TM = 512
TS = 512

NN = ((1,), (0,))
NT = ((1,), (1,))
TN = ((0,), (0,))


def _params(sem):
    return pltpu.CompilerParams(dimension_semantics=sem, vmem_limit_bytes=VMEM_LIMIT)


def _dot(a, b, contract):
    return lax.dot_general(a, b, (contract, ((), ())), preferred_element_type=F32)


def _matmul(name, operands, in_specs, out_shapes, out_specs, grid, contract, epilogue, acc_shape, aliases=None):
    nk = grid[2]
    n_in, n_out = len(operands), len(out_shapes)

    def body(*refs):
        ins, outs = refs[:n_in], refs[n_in:n_in + n_out]
        a, b = ins[0][...], ins[1][...]
        p = _dot(a.astype(BF16), b.astype(BF16), contract)
        if nk == 1:
            epilogue(p, ins[2:], outs)
            return
        acc = refs[-1]
        k = pl.program_id(2)

        @pl.when(k == 0)
        def _():
            acc[...] = p

        @pl.when((k > 0) & (k < nk - 1))
        def _():
            acc[...] += p

        @pl.when(k == nk - 1)
        def _():
            epilogue(acc[...] + p, ins[2:], outs)

    return pl.pallas_call(
        body, name=name, grid=grid, in_specs=in_specs, out_specs=out_specs, out_shape=out_shapes,
        scratch_shapes=[] if nk == 1 else [pltpu.VMEM(acc_shape, F32)],
        input_output_aliases=aliases or {},
        compiler_params=_params(("parallel", "parallel", "arbitrary")),
    )(*operands)


def _wspec(w, layer, shard_axis):
    r, c = w.shape[-2:]
    if w.ndim == 4:
        return pl.BlockSpec((None, None, r, c), lambda i, j, k: ((i, j, k)[shard_axis], layer, 0, 0))
    return pl.BlockSpec((None, r, c), lambda i, j, k: ((i, j, k)[shard_axis], 0, 0))


def _ep_store(p, ins, outs):
    for o in outs:
        o[...] = p.astype(o.dtype)


def _mm_fwd(name, a, w, layer, col_sharded, epilogue=_ep_store, extras=(), extra_specs=(), outs=None):
    s = a.shape[0]
    r, c = w.shape[-2:]
    if col_sharded:
        grid = (s // TM, N_CHIPS, 1)
        a_spec = pl.BlockSpec((TM, r), lambda i, j, k: (i, 0))
        o_map = lambda i, j, k: (i, j)
        n_out = N_CHIPS * c
        w_spec = _wspec(w, layer, 1)
    else:
        grid = (s // TM, 1, N_CHIPS)
        a_spec = pl.BlockSpec((TM, r), lambda i, j, k: (i, k))
        o_map = lambda i, j, k: (i, 0)
        n_out = c
        w_spec = _wspec(w, layer, 2)
    if outs is None:
        outs = [(n_out, F32)]
    out_shapes = [jax.ShapeDtypeStruct((s, n), dt) for n, dt in outs]
    out_specs = [pl.BlockSpec((TM, c if n == n_out else n), o_map) for n, _ in outs]
    return _matmul(name, (a, w) + tuple(extras), [a_spec, w_spec] + list(extra_specs), out_shapes, out_specs,
                   grid, NN, epilogue, (TM, c))


def _mm_bwd_act(name, dy, w, layer, col_sharded, epilogue=_ep_store, extras=(), extra_specs=(), out_dtype=F32):
    s = dy.shape[0]
    r, c = w.shape[-2:]
    if col_sharded:
        grid = (s // TM, 1, N_CHIPS)
        a_spec = pl.BlockSpec((TM, c), lambda i, j, k: (i, k))
        o_spec = pl.BlockSpec((TM, r), lambda i, j, k: (i, 0))
        n_out = r
        w_spec = _wspec(w, layer, 2)
    else:
        grid = (s // TM, N_CHIPS, 1)
        a_spec = pl.BlockSpec((TM, c), lambda i, j, k: (i, 0))
        o_spec = pl.BlockSpec((TM, r), lambda i, j, k: (i, j))
        n_out = N_CHIPS * r
        w_spec = _wspec(w, layer, 1)
    return _matmul(name, (dy, w) + tuple(extras), [a_spec, w_spec] + list(extra_specs),
                   [jax.ShapeDtypeStruct((s, n_out), out_dtype)], [o_spec], grid, NT, epilogue, (TM, r))[0]


def _mm_bwd_w(name, a, dy, buf, layer, col_sharded):
    s = a.shape[0]
    r, c = buf.shape[-2:]
    grid = (N_CHIPS, 1, s // TS)
    if col_sharded:
        a_spec = pl.BlockSpec((TS, r), lambda i, j, k: (k, 0))
        b_spec = pl.BlockSpec((TS, c), lambda i, j, k: (k, i))
    else:
        a_spec = pl.BlockSpec((TS, r), lambda i, j, k: (k, i))
        b_spec = pl.BlockSpec((TS, c), lambda i, j, k: (k, 0))

    def epilogue(p, ins, outs):
        outs[0][...] = p

    return _matmul(name, (a, dy, buf), [a_spec, b_spec, pl.BlockSpec(memory_space=pl.ANY)],
                   [jax.ShapeDtypeStruct(buf.shape, F32)], [_wspec(buf, layer, 0)], grid, TN, epilogue, (r, c),
                   aliases={2: 0})[0]


def _row_spec(n):
    return pl.BlockSpec((TM, n), lambda i, j, k: (i, 0))


def _vec_spec(layer, n):
    return pl.BlockSpec((None, 1, n), lambda i, j, k: (layer, 0, 0))


def _ep_resid_ln(p, ins, outs):
    x_ref, g_ref, b_ref = ins
    xf_ref, xb_ref, xhat_ref, rstd_ref = outs
    r = ALPHA * x_ref[...] + p
    mu = jnp.mean(r, axis=-1, keepdims=True)
    d = r - mu
    var = jnp.mean(d * d, axis=-1, keepdims=True)
    rstd = lax.rsqrt(var + LN_EPS)
    xhat = d * rstd
    y = xhat * g_ref[...] + b_ref[...]
    xf_ref[...] = y
    xb_ref[...] = y.astype(BF16)
    xhat_ref[...] = xhat
    rstd_ref[...] = rstd


def _mm_resid_ln(name, a, w, layer, x, g3, b3, ln_layer):
    d = x.shape[1]
    return _mm_fwd(name, a, w, layer, False, _ep_resid_ln, (x, g3, b3),
                   (_row_spec(d), _vec_spec(ln_layer, d), _vec_spec(ln_layer, d)),
                   outs=[(d, F32), (d, BF16), (d, F32), (1, F32)])


def _ep_relu2(p, ins, outs):
    h = jnp.maximum(p, 0.0)
    outs[0][...] = h.astype(BF16)
    outs[1][...] = (h * h).astype(BF16)


def _ep_scale_q(p, ins, outs):
    outs[0][...] = (p * (HEAD_DIM ** -0.5)).astype(BF16)


def _ep_bf16(p, ins, outs):
    outs[0][...] = p.astype(BF16)


def _ep_relu2_bwd(p, ins, outs):
    outs[0][...] = (p * (2.0 * ins[0][...].astype(F32))).astype(BF16)


def _ep_resid(p, ins, outs):
    outs[0][...] = ALPHA * ins[0][...] + p


def _ep_add(p, ins, outs):
    outs[0][...] = ins[0][...] + p


def _gelu_grad(x):
    c0 = math.sqrt(2.0 / math.pi)
    t = jnp.tanh(c0 * (x + 0.044715 * (x * x * x)))
    return 0.5 * (1.0 + t) + (0.5 * x) * (1.0 - t * t) * (c0 * (1.0 + 3.0 * 0.044715 * (x * x)))


def _cast_bf16(w2d):
    r, c = w2d.shape
    tr = min(r, 512)

    def body(w_ref, o_ref):
        o_ref[...] = w_ref[...].astype(BF16)

    return pl.pallas_call(
        body, name="cast_bf16", grid=(r // tr,),
        in_specs=[pl.BlockSpec((tr, c), lambda i: (i, 0))], out_specs=pl.BlockSpec((tr, c), lambda i: (i, 0)),
        out_shape=jax.ShapeDtypeStruct((r, c), BF16), compiler_params=_params(("parallel",)),
    )(w2d)


def _cast_into_slot(w2d, chip):
    r, c = w2d.shape
    tr = min(r, 512)

    def body(chip_ref, w_ref, o_ref):
        o_ref[...] = w_ref[...].astype(BF16)

    grid_spec = pltpu.PrefetchScalarGridSpec(
        num_scalar_prefetch=1, grid=(r // tr,),
        in_specs=[pl.BlockSpec((tr, c), lambda i, chip_ref: (i, 0))],
        out_specs=pl.BlockSpec((None, tr, c), lambda i, chip_ref: (chip_ref[0], i, 0)))
    return pl.pallas_call(
        body, name="cast_into_slot", grid_spec=grid_spec,
        out_shape=jax.ShapeDtypeStruct((N_CHIPS, r, c), BF16), compiler_params=_params(("parallel",)),
    )(chip, w2d)


def _gmlp_norm_fwd(h, g3, b3, layer):
    s, w2 = h.shape
    w = w2 // 2

    def body(h_ref, g_ref, b_ref, o_ref):
        z = jax.nn.gelu(h_ref[...])
        mu = jnp.mean(z, axis=-1, keepdims=True)
        d = z - mu
        var = jnp.mean(d * d, axis=-1, keepdims=True)
        o_ref[...] = (d * lax.rsqrt(var + LN_EPS) * g_ref[...] + b_ref[...]).astype(BF16)

    vec = pl.BlockSpec((None, 1, w), lambda i: (layer, 0, 0))
    return pl.pallas_call(
        body, name="gmlp_norm_fwd", grid=(s // TM,),
        in_specs=[pl.BlockSpec((TM, w), lambda i: (i, 1)), vec, vec],
        out_specs=pl.BlockSpec((TM, w), lambda i: (i, 0)),
        out_shape=jax.ShapeDtypeStruct((s, w), BF16), compiler_params=_params(("parallel",)),
    )(h, g3, b3)


def _chunk_mask():
    t = lax.broadcasted_iota(jnp.int32, (GMLP_BLOCK, GMLP_BLOCK), 0)
    s = lax.broadcasted_iota(jnp.int32, (GMLP_BLOCK, GMLP_BLOCK), 1)
    return (s // CHUNK) <= (t // CHUNK)


SG_ROWS = 512


def _gate_fwd(h, vn, ws, bst):
    s, w = vn.shape
    gd = w // GMLP_GROUPS

    def body(h_ref, v_ref, ws_ref, bs_ref, o_ref):
        mask = _chunk_mask()
        for g in range(GMLP_GROUPS):
            wm = jnp.where(mask, ws_ref[g], 0.0).astype(BF16)
            bias = bs_ref[:, g:g + 1]
            cols = slice(g * gd, (g + 1) * gd)
            for n in range(SG_ROWS // GMLP_BLOCK):
                rows = slice(n * GMLP_BLOCK, (n + 1) * GMLP_BLOCK)
                sp = _dot(wm, v_ref[rows, cols], NN) + bias
                o_ref[rows, cols] = (jax.nn.gelu(h_ref[rows, cols]) * sp).astype(BF16)

    return pl.pallas_call(
        body, name="gate_fwd", grid=(s // SG_ROWS,),
        in_specs=[pl.BlockSpec((SG_ROWS, w), lambda i: (i, 0)), pl.BlockSpec((SG_ROWS, w), lambda i: (i, 0)),
                  pl.BlockSpec(ws.shape, lambda i: (0, 0, 0)), pl.BlockSpec(bst.shape, lambda i: (0, 0))],
        out_specs=pl.BlockSpec((SG_ROWS, w), lambda i: (i, 0)),
        out_shape=jax.ShapeDtypeStruct((s, w), BF16), compiler_params=_params(("parallel",)),
    )(h, vn, ws, bst)


def _gate_bwd(dgated, h, vn, ws, bst):
    s, w = vn.shape
    gd = w // GMLP_GROUPS
    nsteps = s // SG_ROWS

    def body(dg_ref, h_ref, v_ref, ws_ref, bs_ref, du_ref, dv_ref, dws_ref, dbs_ref, dsum):
        i = pl.program_id(0)

        @pl.when(i == 0)
        def _():
            dws_ref[...] = jnp.zeros_like(dws_ref)
            dsum[...] = jnp.zeros_like(dsum)

        mask = _chunk_mask()
        for g in range(GMLP_GROUPS):
            wm = jnp.where(mask, ws_ref[g], 0.0).astype(BF16)
            bias = bs_ref[:, g:g + 1]
            cols = slice(g * gd, (g + 1) * gd)
            dw = jnp.zeros((GMLP_BLOCK, GMLP_BLOCK), F32)
            dsg = jnp.zeros((GMLP_BLOCK, gd), F32)
            for n in range(SG_ROWS // GMLP_BLOCK):
                rows = slice(n * GMLP_BLOCK, (n + 1) * GMLP_BLOCK)
                vb = v_ref[rows, cols]
                sp = _dot(wm, vb, NN) + bias
                dg = dg_ref[rows, cols]
                du_ref[rows, cols] = dg * sp
                ds = dg * jax.nn.gelu(h_ref[rows, cols])
                dsb = ds.astype(BF16)
                dw += _dot(dsb, vb, NT)
                dsg += ds
                dv_ref[rows, cols] = _dot(wm, dsb, TN)
            dws_ref[g] += dw
            dsum[:, cols] += dsg

        @pl.when(i == nsteps - 1)
        def _():
            for g in range(GMLP_GROUPS):
                dws_ref[g] = jnp.where(mask, dws_ref[g], 0.0)
                tot = jnp.sum(dsum[:, g * gd:(g + 1) * gd], axis=-1, keepdims=True)
                dbs_ref[g] = jnp.broadcast_to(tot, (GMLP_BLOCK, LANES))

    tile = pl.BlockSpec((SG_ROWS, w), lambda i: (i, 0))
    return pl.pallas_call(
        body, name="gate_bwd", grid=(nsteps,),
        in_specs=[tile, tile, tile, pl.BlockSpec(ws.shape, lambda i: (0, 0, 0)), pl.BlockSpec(bst.shape, lambda i: (0, 0))],
        out_specs=[tile, tile, pl.BlockSpec(ws.shape, lambda i: (0, 0, 0)),
                   pl.BlockSpec((GMLP_GROUPS, GMLP_BLOCK, LANES), lambda i: (0, 0, 0))],
        out_shape=[jax.ShapeDtypeStruct((s, w), F32), jax.ShapeDtypeStruct((s, w), F32),
                   jax.ShapeDtypeStruct(ws.shape, F32), jax.ShapeDtypeStruct((GMLP_GROUPS, GMLP_BLOCK, LANES), F32)],
        scratch_shapes=[pltpu.VMEM((GMLP_BLOCK, w), F32)],
        compiler_params=_params(("arbitrary",)),
    )(dgated, h, vn, ws, bst)


GB_ROWS = 256


def _gmlp_in_bwd(h, du, dvn, g3, layer):
    s, w2 = h.shape
    w = w2 // 2
    nsteps = s // GB_ROWS

    def body(h_ref, du_ref, dv_ref, g_ref, dh_ref, dg_ref, db_ref):
        i = pl.program_id(0)

        @pl.when(i == 0)
        def _():
            dg_ref[...] = jnp.zeros_like(dg_ref)
            db_ref[...] = jnp.zeros_like(db_ref)

        hu = h_ref[:, :w]
        hv = h_ref[:, w:]
        dh_ref[:, :w] = (du_ref[...] * _gelu_grad(hu)).astype(BF16)
        z = jax.nn.gelu(hv)
        mu = jnp.mean(z, axis=-1, keepdims=True)
        d = z - mu
        var = jnp.mean(d * d, axis=-1, keepdims=True)
        rstd = lax.rsqrt(var + LN_EPS)
        xhat = d * rstd
        dy = dv_ref[...]
        db_ref[...] += jnp.sum(dy, axis=0, keepdims=True)
        dg_ref[...] += jnp.sum(dy * xhat, axis=0, keepdims=True)
        dxh = dy * g_ref[...]
        m1 = jnp.mean(dxh, axis=-1, keepdims=True)
        m2 = jnp.mean(dxh * xhat, axis=-1, keepdims=True)
        dz = rstd * (dxh - m1 - xhat * m2)
        dh_ref[:, w:] = (dz * _gelu_grad(hv)).astype(BF16)

    half = pl.BlockSpec((GB_ROWS, w), lambda i: (i, 0))
    vec = pl.BlockSpec((1, w), lambda i: (0, 0))
    return pl.pallas_call(
        body, name="gmlp_in_bwd", grid=(nsteps,),
        in_specs=[pl.BlockSpec((GB_ROWS, w2), lambda i: (i, 0)), half, half,
                  pl.BlockSpec((None, 1, w), lambda i: (layer, 0, 0))],
        out_specs=[pl.BlockSpec((GB_ROWS, w2), lambda i: (i, 0)), vec, vec],
        out_shape=[jax.ShapeDtypeStruct((s, w2), BF16), jax.ShapeDtypeStruct((1, w), F32), jax.ShapeDtypeStruct((1, w), F32)],
        compiler_params=_params(("arbitrary",)),
    )(h, du, dvn, g3)


def _ln_bwd(dy, xhat, rstd, g3, layer):
    s, d = dy.shape
    nsteps = s // TM

    def body(dy_ref, xh_ref, rs_ref, g_ref, dr_ref, drb_ref, dg_ref, db_ref):
        i = pl.program_id(0)

        @pl.when(i == 0)
        def _():
            dg_ref[...] = jnp.zeros_like(dg_ref)
            db_ref[...] = jnp.zeros_like(db_ref)

        dyv = dy_ref[...]
        xhat_v = xh_ref[...]
        db_ref[...] += jnp.sum(dyv, axis=0, keepdims=True)
        dg_ref[...] += jnp.sum(dyv * xhat_v, axis=0, keepdims=True)
        dxh = dyv * g_ref[...]
        m1 = jnp.mean(dxh, axis=-1, keepdims=True)
        m2 = jnp.mean(dxh * xhat_v, axis=-1, keepdims=True)
        dr = rs_ref[...] * (dxh - m1 - xhat_v * m2)
        dr_ref[...] = dr
        drb_ref[...] = dr.astype(BF16)

    tile = pl.BlockSpec((TM, d), lambda i: (i, 0))
    vec = pl.BlockSpec((1, d), lambda i: (0, 0))
    return pl.pallas_call(
        body, name="ln_bwd", grid=(nsteps,),
        in_specs=[tile, tile, pl.BlockSpec((TM, 1), lambda i: (i, 0)), pl.BlockSpec((None, 1, d), lambda i: (layer, 0, 0))],
        out_specs=[tile, tile, vec, vec],
        out_shape=[jax.ShapeDtypeStruct((s, d), F32), jax.ShapeDtypeStruct((s, d), BF16),
                   jax.ShapeDtypeStruct((1, d), F32), jax.ShapeDtypeStruct((1, d), F32)],
        compiler_params=_params(("arbitrary",)),
    )(dy, xhat, rstd, g3)


def _loss_head(y, target):
    s, d = y.shape

    def body(y_ref, t_ref, dy_ref, l_ref):
        i = pl.program_id(0)

        @pl.when(i == 0)
        def _():
            l_ref[...] = jnp.zeros_like(l_ref)

        e = y_ref[...] - t_ref[...]
        dy_ref[...] = e * (1.0 / d)
        l_ref[...] += jnp.sum(jnp.sum(e * e, axis=1, keepdims=True), axis=0, keepdims=True)

    tile = pl.BlockSpec((TM, d), lambda i: (i, 0))
    return pl.pallas_call(
        body, name="loss_head", grid=(s // TM,), in_specs=[tile, tile],
        out_specs=[tile, pl.BlockSpec((1, 1), lambda i: (0, 0))],
        out_shape=[jax.ShapeDtypeStruct((s, d), F32), jax.ShapeDtypeStruct((1, 1), F32)],
        compiler_params=_params(("arbitrary",)),
    )(y, target)


LOG2E = 1.4426950408889634


def _sb_terms(z, causal):
    z2 = z * LOG2E
    e = jnp.exp2(-jnp.abs(z2))
    l1p = jnp.log2(1.0 + e)
    lb = jnp.minimum(z2, 0.0) - l1p
    lr = lb - z2
    if causal is not None:
        lr = jnp.where(causal, lr, 0.0)
    return lb, lr, e


def _split_hi_lo(x):
    hi = x.astype(BF16)
    lo = (x - hi.astype(F32)).astype(BF16)
    return jnp.concatenate([hi, lo], axis=1)


def _att_consts(prefix):
    r = lax.broadcasted_iota(jnp.int32, (2 * ATT_T, ATT_T), 0) % ATT_T
    c = lax.broadcasted_iota(jnp.int32, (2 * ATT_T, ATT_T), 1)
    tri2 = jnp.where((r <= c) if prefix else (r >= c), 1.0, 0.0).astype(BF16)
    r = lax.broadcasted_iota(jnp.int32, (ATT_T, ATT_T), 0)
    c = lax.broadcasted_iota(jnp.int32, (ATT_T, ATT_T), 1)
    causal = c < r
    head_a = lax.broadcasted_iota(jnp.int32, (1, LANES), 1) < HEAD_DIM
    return tri2, causal, head_a


def _attn_fwd(q, k, v):
    s, d = q.shape
    nq = s // ATT_T

    def body(q_ref, k_ref, v_ref, ob_ref, lsum_ref, acc_a, acc_b, rem_a, rem_b):
        i = pl.program_id(1)
        tri, causal, head_a = _att_consts(prefix=False)
        q2 = q_ref[...]
        zero = jnp.zeros_like(q2)
        qa = jnp.where(head_a, q2, zero)
        qb = jnp.where(head_a, zero, q2)
        acc_a[...] = jnp.zeros_like(acc_a)
        acc_b[...] = jnp.zeros_like(acc_b)
        rem_a[...] = jnp.zeros_like(rem_a)
        rem_b[...] = jnp.zeros_like(rem_b)

        def block(kb, mask):
            rows = pl.ds(pl.multiple_of(kb * ATT_T, ATT_T), ATT_T)
            k2 = k_ref[rows, :]
            v2 = v_ref[rows, :]
            heads = ((qa, acc_a, rem_a), (qb, acc_b, rem_b))
            zs = [_dot(qm, k2, NT) for qm, _, _ in heads]
            terms = [_sb_terms(z, mask) for z in zs]
            sums = [_dot(_split_hi_lo(lr), tri, NN) for _, lr, _ in terms]
            for (_, acc, rem), (lb, lr, _), sincl in zip(heads, terms, sums):
                a = jnp.exp2(lb + (sincl - lr) + rem[...])
                if mask is not None:
                    a = jnp.where(mask, a, 0.0)
                rem[...] += sincl[:, 0:1]
                acc[...] += _dot(a.astype(BF16), v2, NN)

        block(i, causal)

        def step(t, carry):
            block(i - 1 - t, None)
            return carry

        lax.fori_loop(0, i, step, 0)
        ob_ref[...] = jnp.where(head_a, acc_a[...], acc_b[...]).astype(BF16)
        lsum_ref[...] = jnp.where(head_a, rem_a[...], rem_b[...])

    qspec = pl.BlockSpec((ATT_T, LANES), lambda p, i: (i, p))
    kspec = pl.BlockSpec((s, LANES), lambda p, i: (0, p))
    return pl.pallas_call(
        body, name="attn_fwd", grid=(d // LANES, nq), in_specs=[qspec, kspec, kspec],
        out_specs=[qspec, qspec],
        out_shape=[jax.ShapeDtypeStruct((s, d), BF16), jax.ShapeDtypeStruct((s, d), F32)],
        scratch_shapes=[pltpu.VMEM((ATT_T, LANES), F32), pltpu.VMEM((ATT_T, LANES), F32),
                        pltpu.VMEM((ATT_T, 1), F32), pltpu.VMEM((ATT_T, 1), F32)],
        compiler_params=_params(("parallel", "arbitrary")),
    )(q, k, v)


def _attn_bwd(q, k, v, do, lsum, dk_prev=None, dv_prev=None):
    s, d = q.shape
    nq = s // ATT_T
    has_prev = dk_prev is not None

    def body(*refs):
        q_ref, k_ref, v_ref, do_ref, ls_ref = refs[:5]
        n_in = 7 if has_prev else 5
        dq_ref, dk_ref, dv_ref, acc_a, acc_b, pre_a, pre_b, gp_a, gp_b = refs[n_in:]
        i = pl.program_id(1)

        @pl.when(i == 0)
        def _():
            if has_prev:
                dk_ref[...] = refs[5][...]
                dv_ref[...] = refs[6][...]
            else:
                dk_ref[...] = jnp.zeros_like(dk_ref)
                dv_ref[...] = jnp.zeros_like(dv_ref)

        tri, causal, head_a = _att_consts(prefix=True)
        q2 = q_ref[...]
        zero = jnp.zeros_like(q2)
        qa = jnp.where(head_a, q2, zero)
        qb = jnp.where(head_a, zero, q2)
        do2 = do_ref[...]
        doa = jnp.where(head_a, do2, 0.0).astype(BF16)
        dob = jnp.where(head_a, 0.0, do2).astype(BF16)
        ls2 = ls_ref[...]
        tot_a = ls2[:, 0:1]
        tot_b = ls2[:, HEAD_DIM:HEAD_DIM + 1]
        for r in (acc_a, acc_b, pre_a, pre_b, gp_a, gp_b):
            r[...] = jnp.zeros_like(r)

        def block(kb, mask):
            rows = pl.ds(pl.multiple_of(kb * ATT_T, ATT_T), ATT_T)
            k2 = k_ref[rows, :]
            v2 = v_ref[rows, :]
            dk_new = jnp.zeros((ATT_T, LANES), F32)
            dv_new = jnp.zeros((ATT_T, LANES), F32)
            heads = ((qa, doa, tot_a, acc_a, pre_a, gp_a), (qb, dob, tot_b, acc_b, pre_b, gp_b))
            zs = [_dot(h[0], k2, NT) for h in heads]
            das = [_dot(h[1], v2, NT) for h in heads]
            terms = [_sb_terms(z, mask) for z in zs]
            psums = [_dot(_split_hi_lo(lr), tri, NN) for _, lr, _ in terms]
            gs, abs_ = [], []
            for (_, _, tot, _, pre, _), (lb, _, _), pincl, da in zip(heads, terms, psums, das):
                a = jnp.exp2(lb + (tot - (pre[...] + pincl)))
                if mask is not None:
                    a = jnp.where(mask, a, 0.0)
                pre[...] += pincl[:, ATT_T - 1:ATT_T]
                gs.append(a * da)
                abs_.append(a.astype(BF16))
            gsums = [_dot(_split_hi_lo(g), tri, NN) for g in gs]
            dzs = []
            for (_, _, _, _, _, gpre), z, (_, _, e), g, gincl in zip(heads, zs, terms, gs, gsums):
                gbefore = gpre[...] + (gincl - g)
                gpre[...] += gincl[:, ATT_T - 1:ATT_T]
                inv = 1.0 / (1.0 + e)
                beta = jnp.where(z >= 0.0, inv, e * inv)
                dz = g - beta * (g + gbefore)
                if mask is not None:
                    dz = jnp.where(mask, dz, 0.0)
                dzs.append(dz.astype(BF16))
            for (qm, dom, _, acc, _, _), ab, dzb in zip(heads, abs_, dzs):
                dv_new += _dot(ab, dom, TN)
                dk_new += _dot(dzb, qm, TN)
                acc[...] += _dot(dzb, k2, NN)
            dk_ref[rows, :] += dk_new
            dv_ref[rows, :] += dv_new

        def step(kb, carry):
            block(kb, None)
            return carry

        lax.fori_loop(0, i, step, 0)
        block(i, causal)
        dq_ref[...] = (jnp.where(head_a, acc_a[...], acc_b[...]) * (HEAD_DIM ** -0.5)).astype(BF16)

    qspec = pl.BlockSpec((ATT_T, LANES), lambda p, i: (i, p))
    kspec = pl.BlockSpec((s, LANES), lambda p, i: (0, p))
    ins = [q, k, v, do, lsum] + ([dk_prev, dv_prev] if has_prev else [])
    return pl.pallas_call(
        body, name="attn_bwd", grid=(d // LANES, nq),
        in_specs=[qspec, kspec, kspec, qspec, qspec] + ([kspec, kspec] if has_prev else []),
        out_specs=[qspec, kspec, kspec],
        out_shape=[jax.ShapeDtypeStruct((s, d), BF16), jax.ShapeDtypeStruct((s, d), F32), jax.ShapeDtypeStruct((s, d), F32)],
        scratch_shapes=[pltpu.VMEM((ATT_T, LANES), F32), pltpu.VMEM((ATT_T, LANES), F32)]
        + [pltpu.VMEM((ATT_T, 1), F32)] * 4,
        compiler_params=_params(("parallel", "arbitrary")),
    )(*ins)


def _place():
    x, y, c = lax.axis_index("x"), lax.axis_index("y"), lax.axis_index("c")
    chips = [(1 - x, y), (x, 1 - y), (1 - x, 1 - y)]
    return x, y, c, chips


def _any_specs(n):
    return [pl.BlockSpec(memory_space=pl.ANY)] * n


def _gather_weights(bufs):
    n = len(bufs)

    def body(*refs):
        outs = refs[n:2 * n]
        send_sems, recv_sems = refs[2 * n:]
        x, y, c, chips = _place()
        me = 2 * x + y
        sibling = (x, y, 1 - c)

        def half(a, blk, hc):
            h = outs[a].shape[1] // 2
            return outs[a].at[blk, pl.ds(hc * h, h)]

        def copy(a, k, part, to):
            return pltpu.make_async_remote_copy(src_ref=part, dst_ref=part, send_sem=send_sems.at[a, k],
                                                recv_sem=recv_sems.at[a, k], device_id=to, device_id_type=MESH)

        sent = []
        for a in range(n):
            for k, chip in enumerate(chips):
                sent.append(copy(a, k, half(a, me, c), (*chip, c)))
                sent[-1].start()
        for a in range(n):
            for k, chip in enumerate(chips):
                blk = 2 * chip[0] + chip[1]
                copy(a, k, half(a, blk, c), sibling).wait_recv()
                sent.append(copy(a, 3 + k, half(a, blk, c), sibling))
                sent[-1].start()
        for a in range(n):
            for k, chip in enumerate(chips):
                blk = 2 * chip[0] + chip[1]
                copy(a, 3 + k, half(a, blk, 1 - c), sibling).wait_recv()
        for cp in sent:
            cp.wait_send()

    return pl.pallas_call(
        body, name="gather_weights", in_specs=_any_specs(n), out_specs=_any_specs(n),
        out_shape=[jax.ShapeDtypeStruct(w.shape, w.dtype) for w in bufs],
        input_output_aliases={a: a for a in range(n)},
        scratch_shapes=[pltpu.SemaphoreType.DMA((n, 6)), pltpu.SemaphoreType.DMA((n, 6))],
        compiler_params=pltpu.CompilerParams(has_side_effects=True),
    )(*bufs)


def _pair_exchange(grads):
    n = len(grads)

    def body(*refs):
        ins, outs = refs[:n], refs[n:2 * n]
        send_sems, recv_sems = refs[2 * n:]
        x, y, c, _ = _place()
        cps = []
        for a in range(n):
            h = ins[a].shape[1] // 2
            cps.append(pltpu.make_async_remote_copy(
                src_ref=ins[a].at[:, pl.ds((1 - c) * h, h)], dst_ref=outs[a], send_sem=send_sems.at[a],
                recv_sem=recv_sems.at[a], device_id=(x, y, 1 - c), device_id_type=MESH))
            cps[-1].start()
        for cp in cps:
            cp.wait()

    return pl.pallas_call(
        body, name="pair_exchange", in_specs=_any_specs(n), out_specs=_any_specs(n),
        out_shape=[jax.ShapeDtypeStruct((g.shape[0], g.shape[1] // 2, g.shape[2]), g.dtype) for g in grads],
        scratch_shapes=[pltpu.SemaphoreType.DMA((n,)), pltpu.SemaphoreType.DMA((n,))],
        compiler_params=pltpu.CompilerParams(has_side_effects=True),
    )(*grads)


def _chip_exchange(parts):
    n = len(parts)

    def body(*refs):
        ins, outs = refs[:n], refs[n:2 * n]
        send_sems, recv_sems = refs[2 * n:]
        x, y, c, chips = _place()
        me = 2 * x + y
        cps = []
        for a in range(n):
            for k, chip in enumerate(chips):
                blk = 2 * chip[0] + chip[1]
                cps.append(pltpu.make_async_remote_copy(
                    src_ref=ins[a].at[blk], dst_ref=outs[a].at[me], send_sem=send_sems.at[a, k],
                    recv_sem=recv_sems.at[a, k], device_id=(*chip, c), device_id_type=MESH))
                cps[-1].start()
        for a in range(n):
            for k, chip in enumerate(chips):
                blk = 2 * chip[0] + chip[1]
                pltpu.make_async_remote_copy(
                    src_ref=ins[a].at[blk], dst_ref=outs[a].at[blk], send_sem=send_sems.at[a, k],
                    recv_sem=recv_sems.at[a, k], device_id=(*chip, c), device_id_type=MESH).wait_recv()
        for cp in cps:
            cp.wait_send()

    return pl.pallas_call(
        body, name="chip_exchange", in_specs=_any_specs(n), out_specs=_any_specs(n),
        out_shape=[jax.ShapeDtypeStruct(p.shape, p.dtype) for p in parts],
        scratch_shapes=[pltpu.SemaphoreType.DMA((n, 3)), pltpu.SemaphoreType.DMA((n, 3))],
        compiler_params=pltpu.CompilerParams(has_side_effects=True),
    )(*parts)


def _half_swap(halves):
    n = len(halves)

    def body(*refs):
        outs = refs[n:2 * n]
        send_sems, recv_sems = refs[2 * n:]
        x, y, c, _ = _place()
        cps = []
        for a in range(n):
            h = outs[a].shape[0] // 2
            mine = outs[a].at[pl.ds(c * h, h)]
            cps.append(pltpu.make_async_remote_copy(
                src_ref=mine, dst_ref=mine, send_sem=send_sems.at[a], recv_sem=recv_sems.at[a],
                device_id=(x, y, 1 - c), device_id_type=MESH))
            cps[-1].start()
        for cp in cps:
            cp.wait()

    return pl.pallas_call(
        body, name="half_swap", in_specs=_any_specs(n), out_specs=_any_specs(n),
        out_shape=[jax.ShapeDtypeStruct(p.shape, p.dtype) for p in halves],
        input_output_aliases={a: a for a in range(n)},
        scratch_shapes=[pltpu.SemaphoreType.DMA((n,)), pltpu.SemaphoreType.DMA((n,))],
        compiler_params=pltpu.CompilerParams(has_side_effects=True),
    )(*halves)


N_DEV = 8


def _all_reduce_small(v):
    nrow, ncol = v.shape

    def body(v_ref, o_ref, land, red, send_sems, recv_sems, send2, recv2, loc_sem):
        x, y, c, _ = _place()
        me = 4 * x + 2 * y + c
        peers = []
        for k in range(1, N_DEV):
            peers.append((x ^ ((k >> 2) & 1), y ^ ((k >> 1) & 1), c ^ (k & 1)))
        own = pltpu.make_async_copy(v_ref.at[pl.ds(me, 1)], land.at[pl.ds(me, 1)], loc_sem)
        own.start()
        cps = []
        for k, peer in enumerate(peers):
            dev = 4 * peer[0] + 2 * peer[1] + peer[2]
            cps.append(pltpu.make_async_remote_copy(
                src_ref=v_ref.at[pl.ds(dev, 1)], dst_ref=land.at[pl.ds(me, 1)], send_sem=send_sems.at[k],
                recv_sem=recv_sems.at[k], device_id=peer, device_id_type=MESH))
            cps[-1].start()
        for k, peer in enumerate(peers):
            dev = 4 * peer[0] + 2 * peer[1] + peer[2]
            pltpu.make_async_remote_copy(
                src_ref=v_ref.at[pl.ds(dev, 1)], dst_ref=land.at[pl.ds(dev, 1)], send_sem=send_sems.at[k],
                recv_sem=recv_sems.at[k], device_id=peer, device_id_type=MESH).wait_recv()
        for cp in cps:
            cp.wait_send()
        own.wait()
        terms = land[...]
        total = terms[0:1, :]
        for d in range(1, N_DEV):
            total = total + terms[d:d + 1, :]
        red[...] = total
        own = pltpu.make_async_copy(red, o_ref.at[pl.ds(me, 1)], loc_sem)
        own.start()
        cps = []
        for k, peer in enumerate(peers):
            cps.append(pltpu.make_async_remote_copy(
                src_ref=red, dst_ref=o_ref.at[pl.ds(me, 1)], send_sem=send2.at[k],
                recv_sem=recv2.at[k], device_id=peer, device_id_type=MESH))
            cps[-1].start()
        for k, peer in enumerate(peers):
            dev = 4 * peer[0] + 2 * peer[1] + peer[2]
            pltpu.make_async_remote_copy(
                src_ref=red, dst_ref=o_ref.at[pl.ds(dev, 1)], send_sem=send2.at[k],
                recv_sem=recv2.at[k], device_id=peer, device_id_type=MESH).wait_recv()
        for cp in cps:
            cp.wait_send()
        own.wait()

    vm = pl.BlockSpec(memory_space=pltpu.VMEM)
    return pl.pallas_call(
        body, name="all_reduce_small", in_specs=[vm], out_specs=vm,
        out_shape=jax.ShapeDtypeStruct((nrow, ncol), F32),
        scratch_shapes=[pltpu.VMEM((nrow, ncol), F32), pltpu.VMEM((1, ncol), F32)]
        + [pltpu.SemaphoreType.DMA((N_DEV - 1,))] * 4 + [pltpu.SemaphoreType.DMA],
        compiler_params=pltpu.CompilerParams(has_side_effects=True, vmem_limit_bytes=VMEM_LIMIT),
    )(v)


def _row_tile(rows):
    return min(rows, 512)


def _pair_sum(g, got, core):
    nb, r, c = g.shape
    h = r // 2
    tr = _row_tile(h)
    nt = h // tr

    def body(core_ref, g_ref, got_ref, p_ref, pb_ref):
        p = g_ref[...] + got_ref[...]
        p_ref[...] = p
        pb_ref[...] = p.astype(BF16)

    spec = pl.BlockSpec((None, tr, c), lambda j, t, core_ref: (j, t, 0))
    grid_spec = pltpu.PrefetchScalarGridSpec(
        num_scalar_prefetch=1, grid=(nb, nt),
        in_specs=[pl.BlockSpec((None, tr, c), lambda j, t, core_ref: (j, core_ref[0] * nt + t, 0)), spec],
        out_specs=[spec, spec])
    return pl.pallas_call(
        body, name="pair_sum", grid_spec=grid_spec,
        out_shape=[jax.ShapeDtypeStruct((nb, h, c), F32), jax.ShapeDtypeStruct((nb, h, c), BF16)],
        compiler_params=_params(("parallel", "parallel")),
    )(core, g, got)


def _chip_sum(p, got, place):
    nb, h, c = p.shape
    tr = _row_tile(h)
    nt = h // tr

    def body(place_ref, p_ref, g1_ref, g2_ref, g3_ref, o_ref):
        o_ref[...] = ((p_ref[...] + g1_ref[...].astype(F32)) + g2_ref[...].astype(F32)) + g3_ref[...].astype(F32)

    def blk(off):
        return pl.BlockSpec((None, tr, c), lambda t, place_ref: ((place_ref[0] + off) % N_CHIPS, t, 0))

    grid_spec = pltpu.PrefetchScalarGridSpec(
        num_scalar_prefetch=1, grid=(nt,), in_specs=[blk(0), blk(1), blk(2), blk(3)],
        out_specs=pl.BlockSpec((tr, c), lambda t, place_ref: (place_ref[1] * nt + t, 0)))
    return pl.pallas_call(
        body, name="chip_sum", grid_spec=grid_spec, out_shape=jax.ShapeDtypeStruct((2 * h, c), F32),
        compiler_params=_params(("parallel",)),
    )(place, p, got, got, got)


def _adamw(w, g, m, v):
    r, c = w.shape
    tr = r if r < 8 else _row_tile(r)

    def body(w_ref, g_ref, m_ref, v_ref, d_ref, nm_ref, nv_ref):
        gv = g_ref[...]
        nm = ADAM_B1 * m_ref[...] + (1.0 - ADAM_B1) * gv
        nv = ADAM_B2 * v_ref[...] + (1.0 - ADAM_B2) * (gv * gv)
        m_hat = nm / (1.0 - ADAM_B1 ** ADAM_STEP)
        v_hat = nv / (1.0 - ADAM_B2 ** ADAM_STEP)
        d_ref[...] = -ADAM_LR * (m_hat / (jnp.sqrt(v_hat) + ADAM_EPS) + ADAM_WD * w_ref[...])
        nm_ref[...] = nm
        nv_ref[...] = nv

    tile = pl.BlockSpec((tr, c), lambda i: (i, 0))
    return pl.pallas_call(
        body, name="adamw", grid=(r // tr,), in_specs=[tile] * 4, out_specs=[tile] * 3,
        out_shape=[jax.ShapeDtypeStruct((r, c), F32)] * 3, compiler_params=_params(("parallel",)),
    )(w, g, m, v)


BIG = ("a_w_in", "a_w_out", "sb_w_k", "sb_w_v", "b_w_q", "b_w_o", "ffn_w1", "ffn_w2")
SMALL = ("a_ln_g", "a_ln_b", "a_w_s", "a_b_s", "mix_ln_g", "mix_ln_b", "ffn_ln_g", "ffn_ln_b")
COL_SHARDED = {"a_w_in": True, "a_w_out": False, "sb_w_k": False, "sb_w_v": False, "b_w_q": False, "b_w_o": False,
               "ffn_w1": True, "ffn_w2": False}


def kernel(x, a_w_in, a_ln_g, a_ln_b, a_w_s, a_b_s, a_w_out, sb_w_k, sb_w_v, b_w_q, b_w_o, mix_ln_g, mix_ln_b, ffn_ln_g, ffn_ln_b, ffn_w1, ffn_w2, loss_target, m_a_w_in, m_a_ln_g, m_a_ln_b, m_a_w_s, m_a_b_s, m_a_w_out, m_sb_w_k, m_sb_w_v, m_b_w_q, m_b_w_o, m_mix_ln_g, m_mix_ln_b, m_ffn_ln_g, m_ffn_ln_b, m_ffn_w1, m_ffn_w2, v_a_w_in, v_a_ln_g, v_a_ln_b, v_a_w_s, v_a_b_s, v_a_w_out, v_sb_w_k, v_sb_w_v, v_b_w_q, v_b_w_o, v_mix_ln_g, v_mix_ln_b, v_ffn_ln_g, v_ffn_ln_b, v_ffn_w1, v_ffn_w2):
    names = BIG + SMALL
    given = dict(a_w_in=a_w_in, a_ln_g=a_ln_g, a_ln_b=a_ln_b, a_w_s=a_w_s, a_b_s=a_b_s, a_w_out=a_w_out, sb_w_k=sb_w_k,
                 sb_w_v=sb_w_v, b_w_q=b_w_q, b_w_o=b_w_o, mix_ln_g=mix_ln_g, mix_ln_b=mix_ln_b, ffn_ln_g=ffn_ln_g,
                 ffn_ln_b=ffn_ln_b, ffn_w1=ffn_w1, ffn_w2=ffn_w2)
    mom = dict(a_w_in=m_a_w_in, a_ln_g=m_a_ln_g, a_ln_b=m_a_ln_b, a_w_s=m_a_w_s, a_b_s=m_a_b_s, a_w_out=m_a_w_out,
               sb_w_k=m_sb_w_k, sb_w_v=m_sb_w_v, b_w_q=m_b_w_q, b_w_o=m_b_w_o, mix_ln_g=m_mix_ln_g, mix_ln_b=m_mix_ln_b,
               ffn_ln_g=m_ffn_ln_g, ffn_ln_b=m_ffn_ln_b, ffn_w1=m_ffn_w1, ffn_w2=m_ffn_w2)
    var = dict(a_w_in=v_a_w_in, a_ln_g=v_a_ln_g, a_ln_b=v_a_ln_b, a_w_s=v_a_w_s, a_b_s=v_a_b_s, a_w_out=v_a_w_out,
               sb_w_k=v_sb_w_k, sb_w_v=v_sb_w_v, b_w_q=v_b_w_q, b_w_o=v_b_w_o, mix_ln_g=v_mix_ln_g, mix_ln_b=v_mix_ln_b,
               ffn_ln_g=v_ffn_ln_g, ffn_ln_b=v_ffn_ln_b, ffn_w1=v_ffn_w1, ffn_w2=v_ffn_w2)

    cx, cy, cc = lax.axis_index("x"), lax.axis_index("y"), lax.axis_index("c")
    chip = (2 * cx + cy).astype(jnp.int32)
    chip_arr = chip.reshape(1)
    core_arr = cc.astype(jnp.int32).reshape(1)

    s, d = x.shape[1], x.shape[2]
    xf = x.reshape(s, d)
    target = loss_target.reshape(s, d)

    def as2d(w):
        return w.reshape(-1, w.shape[-1])

    slots = [_cast_into_slot(as2d(given[n]), chip_arr).reshape((N_CHIPS,) + given[n].shape) for n in BIG]
    ln_gb = jnp.stack([a_ln_g, a_ln_b])
    ln_slot = lax.dynamic_update_slice(jnp.zeros((N_CHIPS,) + ln_gb.shape, F32), ln_gb[None], (chip, 0, 0, 0))
    gathered = _gather_weights(slots + [ln_slot])
    gw = dict(zip(BIG, gathered[:-1]))
    ln_full = gathered[-1].transpose(1, 2, 0, 3).reshape(2, N_A, 1, -1)
    a_ln_g3, a_ln_b3 = ln_full[0], ln_full[1]
    mix_g3, mix_b3 = mix_ln_g[:, None, :], mix_ln_b[:, None, :]
    ffn_g3, ffn_b3 = ffn_ln_g[:, None, :], ffn_ln_b[:, None, :]
    bst = jnp.swapaxes(a_b_s, 1, 2)

    saved = []
    xb = _cast_bf16(xf)
    kb = vb = None
    for l in range(DEPTH):
        sv = dict(x_in=xb)
        if l < N_A:
            h = _mm_fwd("a_in", xb, gw["a_w_in"], l, True)[0]
            vn = _gmlp_norm_fwd(h, a_ln_g3, a_ln_b3, l)
            gated = _gate_fwd(h, vn, a_w_s[l], bst[l])
            xf, xb, xhat, rstd = _mm_resid_ln("a_out", gated, gw["a_w_out"], l, xf, mix_g3, mix_b3, l)
            sv.update(h=h, vn=vn, gated=gated)
        else:
            j = l - N_A
            if l == N_A:
                kb = _mm_fwd("sb_k", xb, gw["sb_w_k"], None, False, _ep_bf16, outs=[(d, BF16)])[0]
                vb = _mm_fwd("sb_v", xb, gw["sb_w_v"], None, False, _ep_bf16, outs=[(d, BF16)])[0]
            q = _mm_fwd("b_q", xb, gw["b_w_q"], j, False, _ep_scale_q, outs=[(d, BF16)])[0]
            ob, lsum = _attn_fwd(q, kb, vb)
            xf, xb, xhat, rstd = _mm_resid_ln("b_out", ob, gw["b_w_o"], j, xf, mix_g3, mix_b3, l)
            sv.update(q=q, lsum=lsum, ob=ob)
        sv.update(x_mid=xb, xhat1=xhat, rstd1=rstd)
        dff = gw["ffn_w1"].shape[-1] * N_CHIPS
        pr, act = _mm_fwd("ffn_1", xb, gw["ffn_w1"], l, True, _ep_relu2, outs=[(dff, BF16), (dff, BF16)])
        xf, xb, xhat, rstd = _mm_resid_ln("ffn_2", act, gw["ffn_w2"], l, xf, ffn_g3, ffn_b3, l)
        sv.update(pr=pr, act=act, xhat2=xhat, rstd2=rstd)
        saved.append(sv)

    dx, sq = _loss_head(xf, target)
    loss = lax.psum(0.5 * sq[0, 0] / d, ("x", "y", "c"))

    gbuf = {n: lax.empty((N_CHIPS,) + given[n].shape, F32) for n in BIG}
    small = {}
    d_mix_g, d_mix_b, d_ffn_g, d_ffn_b = [None] * DEPTH, [None] * DEPTH, [None] * DEPTH, [None] * DEPTH
    d_ln_g, d_ln_b, d_ws, d_bs = [None] * N_A, [None] * N_A, [None] * N_A, [None] * N_A
    dk = dv = None
    for l in reversed(range(DEPTH)):
        sv = saved[l]
        dr, drb, d_ffn_g[l], d_ffn_b[l] = _ln_bwd(dx, sv["xhat2"], sv["rstd2"], ffn_g3, l)
        dff = sv["pr"].shape[1]
        dhd = _mm_bwd_act("ffn_2_dx", drb, gw["ffn_w2"], l, False, _ep_relu2_bwd, (sv["pr"],),
                          (pl.BlockSpec((TM, dff // N_CHIPS), lambda i, j, k: (i, j)),), out_dtype=BF16)
        gbuf["ffn_w2"] = _mm_bwd_w("ffn_2_dw", sv["act"], drb, gbuf["ffn_w2"], l, False)
        dx = _mm_bwd_act("ffn_1_dx", dhd, gw["ffn_w1"], l, True, _ep_resid, (dr,), (_row_spec(d),))
        gbuf["ffn_w1"] = _mm_bwd_w("ffn_1_dw", sv["x_mid"], dhd, gbuf["ffn_w1"], l, True)

        dr, drb, d_mix_g[l], d_mix_b[l] = _ln_bwd(dx, sv["xhat1"], sv["rstd1"], mix_g3, l)
        quarter = pl.BlockSpec((TM, d // N_CHIPS), lambda i, j, k: (i, j))
        if l < N_A:
            dgated = _mm_bwd_act("a_out_dx", drb, gw["a_w_out"], l, False)
            gbuf["a_w_out"] = _mm_bwd_w("a_out_dw", sv["gated"], drb, gbuf["a_w_out"], l, False)
            du, dvn, d_ws[l], dbs_wide = _gate_bwd(dgated, sv["h"], sv["vn"], a_w_s[l], bst[l])
            d_bs[l] = dbs_wide[:, :, 0]
            dh, dlg, dlb = _gmlp_in_bwd(sv["h"], du, dvn, a_ln_g3, l)
            d_ln_g[l], d_ln_b[l] = dlg[0], dlb[0]
            dx = _mm_bwd_act("a_in_dx", dh, gw["a_w_in"], l, True, _ep_resid, (dr,), (_row_spec(d),))
            gbuf["a_w_in"] = _mm_bwd_w("a_in_dw", sv["x_in"], dh, gbuf["a_w_in"], l, True)
        else:
            j = l - N_A
            do = _mm_bwd_act("b_out_dx", drb, gw["b_w_o"], j, False)
            gbuf["b_w_o"] = _mm_bwd_w("b_out_dw", sv["ob"], drb, gbuf["b_w_o"], j, False)
            dq, dk, dv = _attn_bwd(sv["q"], kb, vb, do, sv["lsum"], dk, dv)
            dx = _mm_bwd_act("b_q_dx", dq, gw["b_w_q"], j, False, _ep_resid, (dr,), (quarter,))
            gbuf["b_w_q"] = _mm_bwd_w("b_q_dw", sv["x_in"], dq, gbuf["b_w_q"], j, False)
            if l == N_A:
                dx = _mm_bwd_act("sb_k_dx", dk, gw["sb_w_k"], None, False, _ep_add, (dx,), (quarter,))
                gbuf["sb_w_k"] = _mm_bwd_w("sb_k_dw", sv["x_in"], dk, gbuf["sb_w_k"], None, False)
                dx = _mm_bwd_act("sb_v_dx", dv, gw["sb_w_v"], None, False, _ep_add, (dx,), (quarter,))
                gbuf["sb_w_v"] = _mm_bwd_w("sb_v_dw", sv["x_in"], dv, gbuf["sb_w_v"], None, False)
    grad_x = dx.reshape(x.shape)

    flat = [gbuf[n].reshape(N_CHIPS, -1, gbuf[n].shape[-1]) for n in BIG]
    got = _pair_exchange(flat)
    sums = [_pair_sum(g, r, core_arr) for g, r in zip(flat, got)]
    landed = _chip_exchange([pb for _, pb in sums])
    place_arr = jnp.stack([chip, cc.astype(jnp.int32)])
    halves = [_chip_sum(p, r, place_arr) for (p, _), r in zip(sums, landed)]
    grads = dict(zip(BIG, [g.reshape(given[n].shape) for n, g in zip(BIG, _half_swap(halves))]))

    small_full = dict(a_ln_g=jnp.stack(d_ln_g), a_ln_b=jnp.stack(d_ln_b), a_w_s=jnp.stack(d_ws), a_b_s=jnp.stack(d_bs),
                      mix_ln_g=jnp.concatenate(d_mix_g), mix_ln_b=jnp.concatenate(d_mix_b),
                      ffn_ln_g=jnp.concatenate(d_ffn_g), ffn_ln_b=jnp.concatenate(d_ffn_b))
    packed = jnp.concatenate([small_full[n].reshape(-1) for n in SMALL])
    total = packed.shape[0]
    ncol = -(-total // (N_DEV * LANES)) * LANES
    packed = jnp.pad(packed, (0, N_DEV * ncol - total)).reshape(N_DEV, ncol)
    reduced = _all_reduce_small(packed).reshape(-1)
    off = 0
    for n in SMALL:
        size = small_full[n].size
        g = reduced[off:off + size].reshape(small_full[n].shape)
        off += size
        if n in ("a_ln_g", "a_ln_b"):
            wq = given[n].shape[1]
            g = lax.dynamic_slice_in_dim(g, chip * wq, wq, axis=1)
        grads[n] = g

    delta, new_m, new_v = {}, {}, {}
    for n in names:
        shape = given[n].shape
        dl, nm, nv = _adamw(as2d(given[n]), as2d(grads[n]), as2d(mom[n]), as2d(var[n]))
        delta[n], new_m[n], new_v[n] = dl.reshape(shape), nm.reshape(shape), nv.reshape(shape)

    order = ("a_w_in", "a_ln_g", "a_ln_b", "a_w_s", "a_b_s", "a_w_out", "sb_w_k", "sb_w_v", "b_w_q", "b_w_o",
             "mix_ln_g", "mix_ln_b", "ffn_ln_g", "ffn_ln_b", "ffn_w1", "ffn_w2")
    return (loss, grad_x, *[grads[n] for n in order], *[delta[n] for n in order],
            *[new_m[n] for n in order], *[new_v[n] for n in order])
```

```python
import math

import jax
import jax.numpy as jnp
from jax import lax
from jax.experimental import pallas as pl
from jax.experimental.pallas import tpu as pltpu

F32 = jnp.float32
BF16 = jnp.bfloat16
MESH = pl.DeviceIdType.MESH

N_CHIPS = 4
DEPTH = 4
N_A = 2
ALPHA = float((2 * DEPTH) ** 0.25)
LN_EPS = 1e-5
CHUNK = 64
GMLP_BLOCK = 128
GMLP_GROUPS = 8
HEAD_DIM = 64
LANES = 128
ATT_T = 256
ADAM_LR = 0.001
ADAM_B1 = 0.9
ADAM_B2 = 0.999
ADAM_EPS = 1e-08
ADAM_WD = 0.01
ADAM_STEP = 10
VMEM_LIMIT = 56 * 1024 * 1024
TM = 512
TS = 1024

NN = ((1,), (0,))
NT = ((1,), (1,))
TN = ((0,), (0,))


def _params(sem):
    return pltpu.CompilerParams(dimension_semantics=sem, vmem_limit_bytes=VMEM_LIMIT)


def _dot(a, b, contract):
    return lax.dot_general(a, b, (contract, ((), ())), preferred_element_type=F32)


def _matmul(name, operands, in_specs, out_shapes, out_specs, grid, contract, epilogue, acc_shape, aliases=None,
            chunks=None):
    nk = grid[2]
    n_in, n_out = len(operands), len(out_shapes)

    def body(*refs):
        ins, outs = refs[:n_in], refs[n_in:n_in + n_out]
        if chunks is None:
            p = _dot(ins[0][...].astype(BF16), ins[1][...].astype(BF16), contract)
        else:
            width = ins[0].shape[1] // chunks
            p = None
            for j in range(chunks):
                pj = _dot(ins[0][:, j * width:(j + 1) * width].astype(BF16), ins[1][j].astype(BF16), contract)
                p = pj if p is None else p + pj
        if nk == 1:
            epilogue(p, ins[2:], outs)
            return
        acc = refs[-1]
        k = pl.program_id(2)

        @pl.when(k == 0)
        def _():
            acc[...] = p

        @pl.when((k > 0) & (k < nk - 1))
        def _():
            acc[...] += p

        @pl.when(k == nk - 1)
        def _():
            epilogue(acc[...] + p, ins[2:], outs)

    return pl.pallas_call(
        body, name=name, grid=grid, in_specs=in_specs, out_specs=out_specs, out_shape=out_shapes,
        scratch_shapes=[] if nk == 1 else [pltpu.VMEM(acc_shape, F32)],
        input_output_aliases=aliases or {},
        compiler_params=_params(("parallel", "parallel", "arbitrary")),
    )(*operands)


def _wspec(w, layer, whole=False):
    r, c = w.shape[-2:]
    lead = N_CHIPS if whole else None
    if w.ndim == 4:
        return pl.BlockSpec((lead, None, r, c), lambda j, i, k: (0 if whole else j, layer, 0, 0))
    return pl.BlockSpec((lead, r, c), lambda j, i, k: (0 if whole else j, 0, 0))


def _ep_store(p, ins, outs):
    for o in outs:
        o[...] = p.astype(o.dtype)


def _mm_fwd(name, a, w, layer, col_sharded, epilogue=_ep_store, extras=(), extra_specs=(), outs=None):
    s = a.shape[0]
    r, c = w.shape[-2:]
    if col_sharded:
        grid = (N_CHIPS, s // TM, 1)
        a_spec = pl.BlockSpec((TM, r), lambda j, i, k: (i, 0))
        n_out = N_CHIPS * c
    else:
        grid = (1, s // TM, 1)
        a_spec = pl.BlockSpec((TM, N_CHIPS * r), lambda j, i, k: (i, 0))
        n_out = c
    if outs is None:
        outs = [(n_out, F32)]
    out_shapes = [jax.ShapeDtypeStruct((s, n), dt) for n, dt in outs]
    out_specs = [pl.BlockSpec((TM, c if n == n_out else n), lambda j, i, k: (i, j)) for n, _ in outs]
    return _matmul(name, (a, w) + tuple(extras), [a_spec, _wspec(w, layer, not col_sharded)] + list(extra_specs),
                   out_shapes, out_specs, grid, NN, epilogue, (TM, c), chunks=None if col_sharded else N_CHIPS)


def _mm_bwd_act(name, dy, w, layer, col_sharded, epilogue=_ep_store, extras=(), extra_specs=(), out_dtype=F32):
    s = dy.shape[0]
    r, c = w.shape[-2:]
    if col_sharded:
        grid = (1, s // TM, 1)
        a_spec = pl.BlockSpec((TM, N_CHIPS * c), lambda j, i, k: (i, 0))
        n_out = r
    else:
        grid = (N_CHIPS, s // TM, 1)
        a_spec = pl.BlockSpec((TM, c), lambda j, i, k: (i, 0))
        n_out = N_CHIPS * r
    o_spec = pl.BlockSpec((TM, r), lambda j, i, k: (i, j))
    return _matmul(name, (dy, w) + tuple(extras), [a_spec, _wspec(w, layer, col_sharded)] + list(extra_specs),
                   [jax.ShapeDtypeStruct((s, n_out), out_dtype)], [o_spec], grid, NT, epilogue, (TM, r),
                   chunks=N_CHIPS if col_sharded else None)[0]


def _mm_bwd_w(name, a, dy, buf, layer, col_sharded):
    s = a.shape[0]
    r, c = buf.shape[-2:]
    ts = min(TS, s)
    grid = (N_CHIPS, 1, s // ts)
    if col_sharded:
        a_spec = pl.BlockSpec((ts, r), lambda j, i, k: (k, 0))
        b_spec = pl.BlockSpec((ts, c), lambda j, i, k: (k, j))
    else:
        a_spec = pl.BlockSpec((ts, r), lambda j, i, k: (k, j))
        b_spec = pl.BlockSpec((ts, c), lambda j, i, k: (k, 0))

    def epilogue(p, ins, outs):
        outs[0][...] = p

    return _matmul(name, (a, dy, buf), [a_spec, b_spec, pl.BlockSpec(memory_space=pl.ANY)],
                   [jax.ShapeDtypeStruct(buf.shape, F32)], [_wspec(buf, layer)], grid, TN, epilogue, (r, c),
                   aliases={2: 0})[0]


def _row_spec(n):
    return pl.BlockSpec((TM, n), lambda j, i, k: (i, 0))


def _vec_spec(layer, n):
    return pl.BlockSpec((None, 1, n), lambda j, i, k: (layer, 0, 0))


def _ep_resid_ln(p, ins, outs):
    x_ref, g_ref, b_ref = ins
    xf_ref, xb_ref, xhat_ref, rstd_ref = outs
    r = ALPHA * x_ref[...] + p
    mu = jnp.mean(r, axis=-1, keepdims=True)
    d = r - mu
    var = jnp.mean(d * d, axis=-1, keepdims=True)
    rstd = lax.rsqrt(var + LN_EPS)
    xhat = d * rstd
    y = xhat * g_ref[...] + b_ref[...]
    xf_ref[...] = y
    xb_ref[...] = y.astype(BF16)
    xhat_ref[...] = xhat
    rstd_ref[...] = rstd


def _mm_resid_ln(name, a, w, layer, x, g3, b3, ln_layer):
    d = x.shape[1]
    return _mm_fwd(name, a, w, layer, False, _ep_resid_ln, (x, g3, b3),
                   (_row_spec(d), _vec_spec(ln_layer, d), _vec_spec(ln_layer, d)),
                   outs=[(d, F32), (d, BF16), (d, F32), (1, F32)])


def _ep_relu2(p, ins, outs):
    h = jnp.maximum(p, 0.0)
    outs[0][...] = h.astype(BF16)
    outs[1][...] = (h * h).astype(BF16)


def _ep_scale_q(p, ins, outs):
    outs[0][...] = (p * (HEAD_DIM ** -0.5)).astype(BF16)


def _ep_bf16(p, ins, outs):
    outs[0][...] = p.astype(BF16)


def _ep_relu2_bwd(p, ins, outs):
    outs[0][...] = (p * (2.0 * ins[0][...].astype(F32))).astype(BF16)


def _ep_resid(p, ins, outs):
    outs[0][...] = ALPHA * ins[0][...] + p


def _ep_add(p, ins, outs):
    outs[0][...] = ins[0][...] + p


def _gelu_grad(x):
    c0 = math.sqrt(2.0 / math.pi)
    t = jnp.tanh(c0 * (x + 0.044715 * (x * x * x)))
    return 0.5 * (1.0 + t) + (0.5 * x) * (1.0 - t * t) * (c0 * (1.0 + 3.0 * 0.044715 * (x * x)))


def _cast_bf16(w2d):
    r, c = w2d.shape
    tr = min(r, 512)

    def body(w_ref, o_ref):
        o_ref[...] = w_ref[...].astype(BF16)

    return pl.pallas_call(
        body, name="cast_bf16", grid=(r // tr,),
        in_specs=[pl.BlockSpec((tr, c), lambda i: (i, 0))], out_specs=pl.BlockSpec((tr, c), lambda i: (i, 0)),
        out_shape=jax.ShapeDtypeStruct((r, c), BF16), compiler_params=_params(("parallel",)),
    )(w2d)


def _cast_into_slot(w2d, chip):
    r, c = w2d.shape
    tr = min(r, 512)

    def body(chip_ref, w_ref, o_ref):
        o_ref[...] = w_ref[...].astype(BF16)

    grid_spec = pltpu.PrefetchScalarGridSpec(
        num_scalar_prefetch=1, grid=(r // tr,),
        in_specs=[pl.BlockSpec((tr, c), lambda i, chip_ref: (i, 0))],
        out_specs=pl.BlockSpec((None, tr, c), lambda i, chip_ref: (chip_ref[0], i, 0)))
    return pl.pallas_call(
        body, name="cast_into_slot", grid_spec=grid_spec,
        out_shape=jax.ShapeDtypeStruct((N_CHIPS, r, c), BF16), compiler_params=_params(("parallel",)),
    )(chip, w2d)


def _gmlp_norm_fwd(h, g3, b3, layer):
    s, w2 = h.shape
    w = w2 // 2

    def body(h_ref, g_ref, b_ref, o_ref):
        z = jax.nn.gelu(h_ref[...])
        mu = jnp.mean(z, axis=-1, keepdims=True)
        d = z - mu
        var = jnp.mean(d * d, axis=-1, keepdims=True)
        o_ref[...] = (d * lax.rsqrt(var + LN_EPS) * g_ref[...] + b_ref[...]).astype(BF16)

    vec = pl.BlockSpec((None, 1, w), lambda i: (layer, 0, 0))
    return pl.pallas_call(
        body, name="gmlp_norm_fwd", grid=(s // TM,),
        in_specs=[pl.BlockSpec((TM, w), lambda i: (i, 1)), vec, vec],
        out_specs=pl.BlockSpec((TM, w), lambda i: (i, 0)),
        out_shape=jax.ShapeDtypeStruct((s, w), BF16), compiler_params=_params(("parallel",)),
    )(h, g3, b3)


def _chunk_mask():
    t = lax.broadcasted_iota(jnp.int32, (GMLP_BLOCK, GMLP_BLOCK), 0)
    s = lax.broadcasted_iota(jnp.int32, (GMLP_BLOCK, GMLP_BLOCK), 1)
    return (s // CHUNK) <= (t // CHUNK)


SG_ROWS = 512


def _gate_fwd(h, vn, ws, bst):
    s, w = vn.shape
    gd = w // GMLP_GROUPS

    def body(h_ref, v_ref, ws_ref, bs_ref, o_ref):
        mask = _chunk_mask()
        for g in range(GMLP_GROUPS):
            wm = jnp.where(mask, ws_ref[g], 0.0).astype(BF16)
            bias = bs_ref[:, g:g + 1]
            cols = slice(g * gd, (g + 1) * gd)
            for n in range(SG_ROWS // GMLP_BLOCK):
                rows = slice(n * GMLP_BLOCK, (n + 1) * GMLP_BLOCK)
                sp = _dot(wm, v_ref[rows, cols], NN) + bias
                o_ref[rows, cols] = (jax.nn.gelu(h_ref[rows, cols]) * sp).astype(BF16)

    return pl.pallas_call(
        body, name="gate_fwd", grid=(s // SG_ROWS,),
        in_specs=[pl.BlockSpec((SG_ROWS, w), lambda i: (i, 0)), pl.BlockSpec((SG_ROWS, w), lambda i: (i, 0)),
                  pl.BlockSpec(ws.shape, lambda i: (0, 0, 0)), pl.BlockSpec(bst.shape, lambda i: (0, 0))],
        out_specs=pl.BlockSpec((SG_ROWS, w), lambda i: (i, 0)),
        out_shape=jax.ShapeDtypeStruct((s, w), BF16), compiler_params=_params(("parallel",)),
    )(h, vn, ws, bst)


def _gate_bwd(dgated, h, vn, ws, bst):
    s, w = vn.shape
    gd = w // GMLP_GROUPS
    nsteps = s // SG_ROWS

    def body(dg_ref, h_ref, v_ref, ws_ref, bs_ref, du_ref, dv_ref, dws_ref, dbs_ref, dsum):
        i = pl.program_id(0)

        @pl.when(i == 0)
        def _():
            dws_ref[...] = jnp.zeros_like(dws_ref)
            dsum[...] = jnp.zeros_like(dsum)

        mask = _chunk_mask()
        for g in range(GMLP_GROUPS):
            wm = jnp.where(mask, ws_ref[g], 0.0).astype(BF16)
            bias = bs_ref[:, g:g + 1]
            cols = slice(g * gd, (g + 1) * gd)
            dw = jnp.zeros((GMLP_BLOCK, GMLP_BLOCK), F32)
            dsg = jnp.zeros((GMLP_BLOCK, gd), F32)
            for n in range(SG_ROWS // GMLP_BLOCK):
                rows = slice(n * GMLP_BLOCK, (n + 1) * GMLP_BLOCK)
                vb = v_ref[rows, cols]
                sp = _dot(wm, vb, NN) + bias
                dg = dg_ref[rows, cols]
                du_ref[rows, cols] = dg * sp
                ds = dg * jax.nn.gelu(h_ref[rows, cols])
                dsb = ds.astype(BF16)
                dw += _dot(dsb, vb, NT)
                dsg += ds
                dv_ref[rows, cols] = _dot(wm, dsb, TN)
            dws_ref[g] += dw
            dsum[:, cols] += dsg

        @pl.when(i == nsteps - 1)
        def _():
            for g in range(GMLP_GROUPS):
                dws_ref[g] = jnp.where(mask, dws_ref[g], 0.0)
                tot = jnp.sum(dsum[:, g * gd:(g + 1) * gd], axis=-1, keepdims=True)
                dbs_ref[g] = jnp.broadcast_to(tot, (GMLP_BLOCK, LANES))

    tile = pl.BlockSpec((SG_ROWS, w), lambda i: (i, 0))
    return pl.pallas_call(
        body, name="gate_bwd", grid=(nsteps,),
        in_specs=[tile, tile, tile, pl.BlockSpec(ws.shape, lambda i: (0, 0, 0)), pl.BlockSpec(bst.shape, lambda i: (0, 0))],
        out_specs=[tile, tile, pl.BlockSpec(ws.shape, lambda i: (0, 0, 0)),
                   pl.BlockSpec((GMLP_GROUPS, GMLP_BLOCK, LANES), lambda i: (0, 0, 0))],
        out_shape=[jax.ShapeDtypeStruct((s, w), F32), jax.ShapeDtypeStruct((s, w), F32),
                   jax.ShapeDtypeStruct(ws.shape, F32), jax.ShapeDtypeStruct((GMLP_GROUPS, GMLP_BLOCK, LANES), F32)],
        scratch_shapes=[pltpu.VMEM((GMLP_BLOCK, w), F32)],
        compiler_params=_params(("arbitrary",)),
    )(dgated, h, vn, ws, bst)


GB_ROWS = 256


def _gmlp_in_bwd(h, du, dvn, g3, layer):
    s, w2 = h.shape
    w = w2 // 2
    nsteps = s // GB_ROWS

    def body(h_ref, du_ref, dv_ref, g_ref, dh_ref, dg_ref, db_ref):
        i = pl.program_id(0)

        @pl.when(i == 0)
        def _():
            dg_ref[...] = jnp.zeros_like(dg_ref)
            db_ref[...] = jnp.zeros_like(db_ref)

        hu = h_ref[:, :w]
        hv = h_ref[:, w:]
        dh_ref[:, :w] = (du_ref[...] * _gelu_grad(hu)).astype(BF16)
        z = jax.nn.gelu(hv)
        mu = jnp.mean(z, axis=-1, keepdims=True)
        d = z - mu
        var = jnp.mean(d * d, axis=-1, keepdims=True)
        rstd = lax.rsqrt(var + LN_EPS)
        xhat = d * rstd
        dy = dv_ref[...]
        db_ref[...] += jnp.sum(dy, axis=0, keepdims=True)
        dg_ref[...] += jnp.sum(dy * xhat, axis=0, keepdims=True)
        dxh = dy * g_ref[...]
        m1 = jnp.mean(dxh, axis=-1, keepdims=True)
        m2 = jnp.mean(dxh * xhat, axis=-1, keepdims=True)
        dz = rstd * (dxh - m1 - xhat * m2)
        dh_ref[:, w:] = (dz * _gelu_grad(hv)).astype(BF16)

    half = pl.BlockSpec((GB_ROWS, w), lambda i: (i, 0))
    vec = pl.BlockSpec((1, w), lambda i: (0, 0))
    return pl.pallas_call(
        body, name="gmlp_in_bwd", grid=(nsteps,),
        in_specs=[pl.BlockSpec((GB_ROWS, w2), lambda i: (i, 0)), half, half,
                  pl.BlockSpec((None, 1, w), lambda i: (layer, 0, 0))],
        out_specs=[pl.BlockSpec((GB_ROWS, w2), lambda i: (i, 0)), vec, vec],
        out_shape=[jax.ShapeDtypeStruct((s, w2), BF16), jax.ShapeDtypeStruct((1, w), F32), jax.ShapeDtypeStruct((1, w), F32)],
        compiler_params=_params(("arbitrary",)),
    )(h, du, dvn, g3)


def _ln_bwd(dy, xhat, rstd, g3, layer):
    s, d = dy.shape
    nsteps = s // TM

    def body(dy_ref, xh_ref, rs_ref, g_ref, dr_ref, drb_ref, dg_ref, db_ref):
        i = pl.program_id(0)

        @pl.when(i == 0)
        def _():
            dg_ref[...] = jnp.zeros_like(dg_ref)
            db_ref[...] = jnp.zeros_like(db_ref)

        dyv = dy_ref[...]
        xhat_v = xh_ref[...]
        db_ref[...] += jnp.sum(dyv, axis=0, keepdims=True)
        dg_ref[...] += jnp.sum(dyv * xhat_v, axis=0, keepdims=True)
        dxh = dyv * g_ref[...]
        m1 = jnp.mean(dxh, axis=-1, keepdims=True)
        m2 = jnp.mean(dxh * xhat_v, axis=-1, keepdims=True)
        dr = rs_ref[...] * (dxh - m1 - xhat_v * m2)
        dr_ref[...] = dr
        drb_ref[...] = dr.astype(BF16)

    tile = pl.BlockSpec((TM, d), lambda i: (i, 0))
    vec = pl.BlockSpec((1, d), lambda i: (0, 0))
    return pl.pallas_call(
        body, name="ln_bwd", grid=(nsteps,),
        in_specs=[tile, tile, pl.BlockSpec((TM, 1), lambda i: (i, 0)), pl.BlockSpec((None, 1, d), lambda i: (layer, 0, 0))],
        out_specs=[tile, tile, vec, vec],
        out_shape=[jax.ShapeDtypeStruct((s, d), F32), jax.ShapeDtypeStruct((s, d), BF16),
                   jax.ShapeDtypeStruct((1, d), F32), jax.ShapeDtypeStruct((1, d), F32)],
        compiler_params=_params(("arbitrary",)),
    )(dy, xhat, rstd, g3)


def _loss_head(y, target):
    s, d = y.shape

    def body(y_ref, t_ref, dy_ref, l_ref):
        i = pl.program_id(0)

        @pl.when(i == 0)
        def _():
            l_ref[...] = jnp.zeros_like(l_ref)

        e = y_ref[...] - t_ref[...]
        dy_ref[...] = e * (1.0 / d)
        l_ref[...] += jnp.sum(jnp.sum(e * e, axis=1, keepdims=True), axis=0, keepdims=True)

    tile = pl.BlockSpec((TM, d), lambda i: (i, 0))
    return pl.pallas_call(
        body, name="loss_head", grid=(s // TM,), in_specs=[tile, tile],
        out_specs=[tile, pl.BlockSpec((1, 1), lambda i: (0, 0))],
        out_shape=[jax.ShapeDtypeStruct((s, d), F32), jax.ShapeDtypeStruct((1, 1), F32)],
        compiler_params=_params(("arbitrary",)),
    )(y, target)


LOG2E = 1.4426950408889634
DEAD_LOG2 = -160.0
FIRST_LANE = 1


def _sb_terms(z, causal):
    z2 = z * LOG2E
    e = jnp.exp2(-jnp.abs(z2))
    l1p = jnp.log2(1.0 + e)
    lb = jnp.minimum(z2, 0.0) - l1p
    lr = lb - z2
    if causal is not None:
        lr = jnp.where(causal, lr, 0.0)
    return lb, lr, e


def _split_hi_lo(x):
    hi = x.astype(BF16)
    lo = (x - hi.astype(F32)).astype(BF16)
    return jnp.concatenate([hi, lo], axis=1)


def _att_consts(prefix):
    r = lax.broadcasted_iota(jnp.int32, (2 * ATT_T, ATT_T), 0) % ATT_T
    c = lax.broadcasted_iota(jnp.int32, (2 * ATT_T, ATT_T), 1)
    tri2 = jnp.where((r <= c) if prefix else (r >= c), 1.0, 0.0).astype(BF16)
    r = lax.broadcasted_iota(jnp.int32, (ATT_T, ATT_T), 0)
    c = lax.broadcasted_iota(jnp.int32, (ATT_T, ATT_T), 1)
    causal = c < r
    head_a = lax.broadcasted_iota(jnp.int32, (1, LANES), 1) < HEAD_DIM
    return tri2, causal, head_a


def _attn_fwd(q, k, v):
    s, d = q.shape
    nq = s // ATT_T

    def body(q_ref, k_ref, v_ref, ob_ref, lsum_ref, acc_a, acc_b, rem_a, rem_b):
        i = pl.program_id(1)
        tri, causal, head_a = _att_consts(prefix=False)
        q2 = q_ref[...]
        zero = jnp.zeros_like(q2)
        qa = jnp.where(head_a, q2, zero)
        qb = jnp.where(head_a, zero, q2)
        acc_a[...] = jnp.zeros_like(acc_a)
        acc_b[...] = jnp.zeros_like(acc_b)
        rem_a[...] = jnp.zeros_like(rem_a)
        rem_b[...] = jnp.zeros_like(rem_b)

        def block(kb, mask):
            rows = pl.ds(pl.multiple_of(kb * ATT_T, ATT_T), ATT_T)
            k2 = k_ref[rows, :]
            v2 = v_ref[rows, :]
            heads = ((qa, acc_a, rem_a), (qb, acc_b, rem_b))
            zs = [_dot(qm, k2, NT) for qm, _, _ in heads]
            terms = [_sb_terms(z, mask) for z in zs]
            sums = [_dot(_split_hi_lo(lr), tri, NN) for _, lr, _ in terms]
            for (_, acc, rem), (lb, lr, _), sincl in zip(heads, terms, sums):
                a = jnp.exp2(lb + (sincl - lr) + rem[...])
                if mask is not None:
                    a = jnp.where(mask, a, 0.0)
                rem[...] += sincl[:, 0:1]
                acc[...] += _dot(a.astype(BF16), v2, NN)

        block(i, causal)

        def live():
            return jnp.maximum(jnp.max(rem_a[...]), jnp.max(rem_b[...])) > DEAD_LOG2

        def go_on(carry):
            t, alive = carry
            return (t < i) & alive

        def step(carry):
            t, _ = carry
            block(i - 1 - t, None)
            return t + 1, live()

        done, _ = lax.while_loop(go_on, step, (jnp.int32(0), live()))
        first = (i - done).astype(F32)
        ob_ref[...] = jnp.where(head_a, acc_a[...], acc_b[...]).astype(BF16)
        lane = lax.broadcasted_iota(jnp.int32, (1, LANES), 1)
        lsum_ref[...] = jnp.where(lane == FIRST_LANE, first, jnp.where(head_a, rem_a[...], rem_b[...]))

    qspec = pl.BlockSpec((ATT_T, LANES), lambda p, i: (i, p))
    kspec = pl.BlockSpec((s, LANES), lambda p, i: (0, p))
    return pl.pallas_call(
        body, name="attn_fwd", grid=(d // LANES, nq), in_specs=[qspec, kspec, kspec],
        out_specs=[qspec, qspec],
        out_shape=[jax.ShapeDtypeStruct((s, d), BF16), jax.ShapeDtypeStruct((s, d), F32)],
        scratch_shapes=[pltpu.VMEM((ATT_T, LANES), F32), pltpu.VMEM((ATT_T, LANES), F32),
                        pltpu.VMEM((ATT_T, 1), F32), pltpu.VMEM((ATT_T, 1), F32)],
        compiler_params=_params(("parallel", "arbitrary")),
    )(q, k, v)


def _attn_bwd(q, k, v, do, lsum, dk_prev=None, dv_prev=None):
    s, d = q.shape
    nq = s // ATT_T
    has_prev = dk_prev is not None

    def body(*refs):
        q_ref, k_ref, v_ref, do_ref, ls_ref = refs[:5]
        n_in = 7 if has_prev else 5
        dq_ref, dk_ref, dv_ref, acc_a, acc_b, pre_a, pre_b, gp_a, gp_b = refs[n_in:]
        i = pl.program_id(1)

        @pl.when(i == 0)
        def _():
            if has_prev:
                dk_ref[...] = refs[5][...]
                dv_ref[...] = refs[6][...]
            else:
                dk_ref[...] = jnp.zeros_like(dk_ref)
                dv_ref[...] = jnp.zeros_like(dv_ref)

        tri, causal, head_a = _att_consts(prefix=True)
        q2 = q_ref[...]
        zero = jnp.zeros_like(q2)
        qa = jnp.where(head_a, q2, zero)
        qb = jnp.where(head_a, zero, q2)
        do2 = do_ref[...]
        doa = jnp.where(head_a, do2, 0.0).astype(BF16)
        dob = jnp.where(head_a, 0.0, do2).astype(BF16)
        ls2 = ls_ref[...]
        tot_a = ls2[:, 0:1]
        tot_b = ls2[:, HEAD_DIM:HEAD_DIM + 1]
        for r in (acc_a, acc_b, pre_a, pre_b, gp_a, gp_b):
            r[...] = jnp.zeros_like(r)

        def block(kb, mask):
            rows = pl.ds(pl.multiple_of(kb * ATT_T, ATT_T), ATT_T)
            k2 = k_ref[rows, :]
            v2 = v_ref[rows, :]
            dk_new = jnp.zeros((ATT_T, LANES), F32)
            dv_new = jnp.zeros((ATT_T, LANES), F32)
            heads = ((qa, doa, tot_a, acc_a, pre_a, gp_a), (qb, dob, tot_b, acc_b, pre_b, gp_b))
            zs = [_dot(h[0], k2, NT) for h in heads]
            das = [_dot(h[1], v2, NT) for h in heads]
            terms = [_sb_terms(z, mask) for z in zs]
            psums = [_dot(_split_hi_lo(lr), tri, NN) for _, lr, _ in terms]
            gs, abs_ = [], []
            for (_, _, tot, _, pre, _), (lb, _, _), pincl, da in zip(heads, terms, psums, das):
                a = jnp.exp2(lb + (tot - (pre[...] + pincl)))
                if mask is not None:
                    a = jnp.where(mask, a, 0.0)
                pre[...] += pincl[:, ATT_T - 1:ATT_T]
                gs.append(a * da)
                abs_.append(a.astype(BF16))
            gsums = [_dot(_split_hi_lo(g), tri, NN) for g in gs]
            dzs = []
            for (_, _, _, _, _, gpre), z, (_, _, e), g, gincl in zip(heads, zs, terms, gs, gsums):
                gbefore = gpre[...] + (gincl - g)
                gpre[...] += gincl[:, ATT_T - 1:ATT_T]
                inv = 1.0 / (1.0 + e)
                beta = jnp.where(z >= 0.0, inv, e * inv)
                dz = g - beta * (g + gbefore)
                if mask is not None:
                    dz = jnp.where(mask, dz, 0.0)
                dzs.append(dz.astype(BF16))
            for (qm, dom, _, acc, _, _), ab, dzb in zip(heads, abs_, dzs):
                dv_new += _dot(ab, dom, TN)
                dk_new += _dot(dzb, qm, TN)
                acc[...] += _dot(dzb, k2, NN)
            dk_ref[rows, :] += dk_new
            dv_ref[rows, :] += dv_new

        def step(kb, carry):
            block(kb, None)
            return carry

        first = jnp.clip(jnp.max(ls2[:, FIRST_LANE:FIRST_LANE + 1]).astype(jnp.int32), 0, i)
        lax.fori_loop(first, i, step, 0)
        block(i, causal)
        dq_ref[...] = (jnp.where(head_a, acc_a[...], acc_b[...]) * (HEAD_DIM ** -0.5)).astype(BF16)

    qspec = pl.BlockSpec((ATT_T, LANES), lambda p, i: (i, p))
    kspec = pl.BlockSpec((s, LANES), lambda p, i: (0, p))
    ins = [q, k, v, do, lsum] + ([dk_prev, dv_prev] if has_prev else [])
    return pl.pallas_call(
        body, name="attn_bwd", grid=(d // LANES, nq),
        in_specs=[qspec, kspec, kspec, qspec, qspec] + ([kspec, kspec] if has_prev else []),
        out_specs=[qspec, kspec, kspec],
        out_shape=[jax.ShapeDtypeStruct((s, d), BF16), jax.ShapeDtypeStruct((s, d), F32), jax.ShapeDtypeStruct((s, d), F32)],
        scratch_shapes=[pltpu.VMEM((ATT_T, LANES), F32), pltpu.VMEM((ATT_T, LANES), F32)]
        + [pltpu.VMEM((ATT_T, 1), F32)] * 4,
        compiler_params=_params(("parallel", "arbitrary")),
    )(*ins)


def _place():
    x, y, c = lax.axis_index("x"), lax.axis_index("y"), lax.axis_index("c")
    chips = [(1 - x, y), (x, 1 - y), (1 - x, 1 - y)]
    return x, y, c, chips


def _any_specs(n):
    return [pl.BlockSpec(memory_space=pl.ANY)] * n


def _gather_weights(bufs):
    n = len(bufs)

    def body(*refs):
        outs = refs[n:2 * n]
        send_sems, recv_sems = refs[2 * n:]
        x, y, c, chips = _place()
        me = 2 * x + y
        sibling = (x, y, 1 - c)

        def half(a, blk, hc):
            h = outs[a].shape[1] // 2
            return outs[a].at[blk, pl.ds(hc * h, h)]

        def copy(a, k, part, to):
            return pltpu.make_async_remote_copy(src_ref=part, dst_ref=part, send_sem=send_sems.at[a, k],
                                                recv_sem=recv_sems.at[a, k], device_id=to, device_id_type=MESH)

        sent = []
        for a in range(n):
            for k, chip in enumerate(chips):
                sent.append(copy(a, k, half(a, me, c), (*chip, c)))
                sent[-1].start()
        for a in range(n):
            for k, chip in enumerate(chips):
                blk = 2 * chip[0] + chip[1]
                copy(a, k, half(a, blk, c), sibling).wait_recv()
                sent.append(copy(a, 3 + k, half(a, blk, c), sibling))
                sent[-1].start()
        for a in range(n):
            for k, chip in enumerate(chips):
                blk = 2 * chip[0] + chip[1]
                copy(a, 3 + k, half(a, blk, 1 - c), sibling).wait_recv()
        for cp in sent:
            cp.wait_send()

    return pl.pallas_call(
        body, name="gather_weights", in_specs=_any_specs(n), out_specs=_any_specs(n),
        out_shape=[jax.ShapeDtypeStruct(w.shape, w.dtype) for w in bufs],
        input_output_aliases={a: a for a in range(n)},
        scratch_shapes=[pltpu.SemaphoreType.DMA((n, 6)), pltpu.SemaphoreType.DMA((n, 6))],
        compiler_params=pltpu.CompilerParams(has_side_effects=True),
    )(*bufs)


def _pair_exchange(grads):
    n = len(grads)

    def body(*refs):
        ins, outs = refs[:n], refs[n:2 * n]
        send_sems, recv_sems = refs[2 * n:]
        x, y, c, _ = _place()
        cps = []
        for a in range(n):
            h = ins[a].shape[1] // 2
            cps.append(pltpu.make_async_remote_copy(
                src_ref=ins[a].at[:, pl.ds((1 - c) * h, h)], dst_ref=outs[a], send_sem=send_sems.at[a],
                recv_sem=recv_sems.at[a], device_id=(x, y, 1 - c), device_id_type=MESH))
            cps[-1].start()
        for cp in cps:
            cp.wait()

    return pl.pallas_call(
        body, name="pair_exchange", in_specs=_any_specs(n), out_specs=_any_specs(n),
        out_shape=[jax.ShapeDtypeStruct((g.shape[0], g.shape[1] // 2, g.shape[2]), g.dtype) for g in grads],
        scratch_shapes=[pltpu.SemaphoreType.DMA((n,)), pltpu.SemaphoreType.DMA((n,))],
        compiler_params=pltpu.CompilerParams(has_side_effects=True),
    )(*grads)


def _chip_exchange(parts):
    n = len(parts)

    def body(*refs):
        ins, outs = refs[:n], refs[n:2 * n]
        send_sems, recv_sems = refs[2 * n:]
        x, y, c, chips = _place()
        me = 2 * x + y
        cps = []
        for a in range(n):
            for k, chip in enumerate(chips):
                blk = 2 * chip[0] + chip[1]
                cps.append(pltpu.make_async_remote_copy(
                    src_ref=ins[a].at[blk], dst_ref=outs[a].at[me], send_sem=send_sems.at[a, k],
                    recv_sem=recv_sems.at[a, k], device_id=(*chip, c), device_id_type=MESH))
                cps[-1].start()
        for a in range(n):
            for k, chip in enumerate(chips):
                blk = 2 * chip[0] + chip[1]
                pltpu.make_async_remote_copy(
                    src_ref=ins[a].at[blk], dst_ref=outs[a].at[blk], send_sem=send_sems.at[a, k],
                    recv_sem=recv_sems.at[a, k], device_id=(*chip, c), device_id_type=MESH).wait_recv()
        for cp in cps:
            cp.wait_send()

    return pl.pallas_call(
        body, name="chip_exchange", in_specs=_any_specs(n), out_specs=_any_specs(n),
        out_shape=[jax.ShapeDtypeStruct(p.shape, p.dtype) for p in parts],
        scratch_shapes=[pltpu.SemaphoreType.DMA((n, 3)), pltpu.SemaphoreType.DMA((n, 3))],
        compiler_params=pltpu.CompilerParams(has_side_effects=True),
    )(*parts)


def _half_swap(halves):
    n = len(halves)

    def body(*refs):
        outs = refs[n:2 * n]
        send_sems, recv_sems = refs[2 * n:]
        x, y, c, _ = _place()
        cps = []
        for a in range(n):
            h = outs[a].shape[0] // 2
            mine = outs[a].at[pl.ds(c * h, h)]
            cps.append(pltpu.make_async_remote_copy(
                src_ref=mine, dst_ref=mine, send_sem=send_sems.at[a], recv_sem=recv_sems.at[a],
                device_id=(x, y, 1 - c), device_id_type=MESH))
            cps[-1].start()
        for cp in cps:
            cp.wait()

    return pl.pallas_call(
        body, name="half_swap", in_specs=_any_specs(n), out_specs=_any_specs(n),
        out_shape=[jax.ShapeDtypeStruct(p.shape, p.dtype) for p in halves],
        input_output_aliases={a: a for a in range(n)},
        scratch_shapes=[pltpu.SemaphoreType.DMA((n,)), pltpu.SemaphoreType.DMA((n,))],
        compiler_params=pltpu.CompilerParams(has_side_effects=True),
    )(*halves)


N_DEV = 8


def _all_reduce_small(v):
    nrow, ncol = v.shape

    def body(v_ref, o_ref, land, red, send_sems, recv_sems, send2, recv2, loc_sem):
        x, y, c, _ = _place()
        me = 4 * x + 2 * y + c
        peers = []
        for k in range(1, N_DEV):
            peers.append((x ^ ((k >> 2) & 1), y ^ ((k >> 1) & 1), c ^ (k & 1)))
        own = pltpu.make_async_copy(v_ref.at[pl.ds(me, 1)], land.at[pl.ds(me, 1)], loc_sem)
        own.start()
        cps = []
        for k, peer in enumerate(peers):
            dev = 4 * peer[0] + 2 * peer[1] + peer[2]
            cps.append(pltpu.make_async_remote_copy(
                src_ref=v_ref.at[pl.ds(dev, 1)], dst_ref=land.at[pl.ds(me, 1)], send_sem=send_sems.at[k],
                recv_sem=recv_sems.at[k], device_id=peer, device_id_type=MESH))
            cps[-1].start()
        for k, peer in enumerate(peers):
            dev = 4 * peer[0] + 2 * peer[1] + peer[2]
            pltpu.make_async_remote_copy(
                src_ref=v_ref.at[pl.ds(dev, 1)], dst_ref=land.at[pl.ds(dev, 1)], send_sem=send_sems.at[k],
                recv_sem=recv_sems.at[k], device_id=peer, device_id_type=MESH).wait_recv()
        for cp in cps:
            cp.wait_send()
        own.wait()
        terms = land[...]
        total = terms[0:1, :]
        for d in range(1, N_DEV):
            total = total + terms[d:d + 1, :]
        red[...] = total
        own = pltpu.make_async_copy(red, o_ref.at[pl.ds(me, 1)], loc_sem)
        own.start()
        cps = []
        for k, peer in enumerate(peers):
            cps.append(pltpu.make_async_remote_copy(
                src_ref=red, dst_ref=o_ref.at[pl.ds(me, 1)], send_sem=send2.at[k],
                recv_sem=recv2.at[k], device_id=peer, device_id_type=MESH))
            cps[-1].start()
        for k, peer in enumerate(peers):
            dev = 4 * peer[0] + 2 * peer[1] + peer[2]
            pltpu.make_async_remote_copy(
                src_ref=red, dst_ref=o_ref.at[pl.ds(dev, 1)], send_sem=send2.at[k],
                recv_sem=recv2.at[k], device_id=peer, device_id_type=MESH).wait_recv()
        for cp in cps:
            cp.wait_send()
        own.wait()

    vm = pl.BlockSpec(memory_space=pltpu.VMEM)
    return pl.pallas_call(
        body, name="all_reduce_small", in_specs=[vm], out_specs=vm,
        out_shape=jax.ShapeDtypeStruct((nrow, ncol), F32),
        scratch_shapes=[pltpu.VMEM((nrow, ncol), F32), pltpu.VMEM((1, ncol), F32)]
        + [pltpu.SemaphoreType.DMA((N_DEV - 1,))] * 4 + [pltpu.SemaphoreType.DMA],
        compiler_params=pltpu.CompilerParams(has_side_effects=True, vmem_limit_bytes=VMEM_LIMIT),
    )(v)


def _row_tile(rows):
    return min(rows, 512)


def _pair_sum(g, got, core):
    nb, r, c = g.shape
    h = r // 2
    tr = _row_tile(h)
    nt = h // tr

    def body(core_ref, g_ref, got_ref, p_ref, pb_ref):
        p = g_ref[...] + got_ref[...]
        p_ref[...] = p
        pb_ref[...] = p.astype(BF16)

    spec = pl.BlockSpec((None, tr, c), lambda j, t, core_ref: (j, t, 0))
    grid_spec = pltpu.PrefetchScalarGridSpec(
        num_scalar_prefetch=1, grid=(nb, nt),
        in_specs=[pl.BlockSpec((None, tr, c), lambda j, t, core_ref: (j, core_ref[0] * nt + t, 0)), spec],
        out_specs=[spec, spec])
    return pl.pallas_call(
        body, name="pair_sum", grid_spec=grid_spec,
        out_shape=[jax.ShapeDtypeStruct((nb, h, c), F32), jax.ShapeDtypeStruct((nb, h, c), BF16)],
        compiler_params=_params(("parallel", "parallel")),
    )(core, g, got)


def _chip_sum(p, got, place):
    nb, h, c = p.shape
    tr = _row_tile(h)
    nt = h // tr

    def body(place_ref, p_ref, g1_ref, g2_ref, g3_ref, o_ref):
        o_ref[...] = ((p_ref[...] + g1_ref[...].astype(F32)) + g2_ref[...].astype(F32)) + g3_ref[...].astype(F32)

    def blk(off):
        return pl.BlockSpec((None, tr, c), lambda t, place_ref: ((place_ref[0] + off) % N_CHIPS, t, 0))

    grid_spec = pltpu.PrefetchScalarGridSpec(
        num_scalar_prefetch=1, grid=(nt,), in_specs=[blk(0), blk(1), blk(2), blk(3)],
        out_specs=pl.BlockSpec((tr, c), lambda t, place_ref: (place_ref[1] * nt + t, 0)))
    return pl.pallas_call(
        body, name="chip_sum", grid_spec=grid_spec, out_shape=jax.ShapeDtypeStruct((2 * h, c), F32),
        compiler_params=_params(("parallel",)),
    )(place, p, got, got, got)


def _adamw(w, g, m, v):
    r, c = w.shape
    tr = r if r < 8 else _row_tile(r)

    def body(w_ref, g_ref, m_ref, v_ref, d_ref, nm_ref, nv_ref):
        gv = g_ref[...]
        nm = ADAM_B1 * m_ref[...] + (1.0 - ADAM_B1) * gv
        nv = ADAM_B2 * v_ref[...] + (1.0 - ADAM_B2) * (gv * gv)
        m_hat = nm / (1.0 - ADAM_B1 ** ADAM_STEP)
        v_hat = nv / (1.0 - ADAM_B2 ** ADAM_STEP)
        d_ref[...] = -ADAM_LR * (m_hat / (jnp.sqrt(v_hat) + ADAM_EPS) + ADAM_WD * w_ref[...])
        nm_ref[...] = nm
        nv_ref[...] = nv

    tile = pl.BlockSpec((tr, c), lambda i: (i, 0))
    return pl.pallas_call(
        body, name="adamw", grid=(r // tr,), in_specs=[tile] * 4, out_specs=[tile] * 3,
        out_shape=[jax.ShapeDtypeStruct((r, c), F32)] * 3, compiler_params=_params(("parallel",)),
    )(w, g, m, v)


BIG = ("a_w_in", "a_w_out", "sb_w_k", "sb_w_v", "b_w_q", "b_w_o", "ffn_w1", "ffn_w2")
SMALL = ("a_ln_g", "a_ln_b", "a_w_s", "a_b_s", "mix_ln_g", "mix_ln_b", "ffn_ln_g", "ffn_ln_b")
COL_SHARDED = {"a_w_in": True, "a_w_out": False, "sb_w_k": False, "sb_w_v": False, "b_w_q": False, "b_w_o": False,
               "ffn_w1": True, "ffn_w2": False}


def kernel(x, a_w_in, a_ln_g, a_ln_b, a_w_s, a_b_s, a_w_out, sb_w_k, sb_w_v, b_w_q, b_w_o, mix_ln_g, mix_ln_b, ffn_ln_g, ffn_ln_b, ffn_w1, ffn_w2, loss_target, m_a_w_in, m_a_ln_g, m_a_ln_b, m_a_w_s, m_a_b_s, m_a_w_out, m_sb_w_k, m_sb_w_v, m_b_w_q, m_b_w_o, m_mix_ln_g, m_mix_ln_b, m_ffn_ln_g, m_ffn_ln_b, m_ffn_w1, m_ffn_w2, v_a_w_in, v_a_ln_g, v_a_ln_b, v_a_w_s, v_a_b_s, v_a_w_out, v_sb_w_k, v_sb_w_v, v_b_w_q, v_b_w_o, v_mix_ln_g, v_mix_ln_b, v_ffn_ln_g, v_ffn_ln_b, v_ffn_w1, v_ffn_w2):
    names = BIG + SMALL
    given = dict(a_w_in=a_w_in, a_ln_g=a_ln_g, a_ln_b=a_ln_b, a_w_s=a_w_s, a_b_s=a_b_s, a_w_out=a_w_out, sb_w_k=sb_w_k,
                 sb_w_v=sb_w_v, b_w_q=b_w_q, b_w_o=b_w_o, mix_ln_g=mix_ln_g, mix_ln_b=mix_ln_b, ffn_ln_g=ffn_ln_g,
                 ffn_ln_b=ffn_ln_b, ffn_w1=ffn_w1, ffn_w2=ffn_w2)
    mom = dict(a_w_in=m_a_w_in, a_ln_g=m_a_ln_g, a_ln_b=m_a_ln_b, a_w_s=m_a_w_s, a_b_s=m_a_b_s, a_w_out=m_a_w_out,
               sb_w_k=m_sb_w_k, sb_w_v=m_sb_w_v, b_w_q=m_b_w_q, b_w_o=m_b_w_o, mix_ln_g=m_mix_ln_g, mix_ln_b=m_mix_ln_b,
               ffn_ln_g=m_ffn_ln_g, ffn_ln_b=m_ffn_ln_b, ffn_w1=m_ffn_w1, ffn_w2=m_ffn_w2)
    var = dict(a_w_in=v_a_w_in, a_ln_g=v_a_ln_g, a_ln_b=v_a_ln_b, a_w_s=v_a_w_s, a_b_s=v_a_b_s, a_w_out=v_a_w_out,
               sb_w_k=v_sb_w_k, sb_w_v=v_sb_w_v, b_w_q=v_b_w_q, b_w_o=v_b_w_o, mix_ln_g=v_mix_ln_g, mix_ln_b=v_mix_ln_b,
               ffn_ln_g=v_ffn_ln_g, ffn_ln_b=v_ffn_ln_b, ffn_w1=v_ffn_w1, ffn_w2=v_ffn_w2)

    cx, cy, cc = lax.axis_index("x"), lax.axis_index("y"), lax.axis_index("c")
    chip = (2 * cx + cy).astype(jnp.int32)
    chip_arr = chip.reshape(1)
    core_arr = cc.astype(jnp.int32).reshape(1)

    s, d = x.shape[1], x.shape[2]
    xf = x.reshape(s, d)
    target = loss_target.reshape(s, d)

    def as2d(w):
        return w.reshape(-1, w.shape[-1])

    slots = [_cast_into_slot(as2d(given[n]), chip_arr).reshape((N_CHIPS,) + given[n].shape) for n in BIG]
    ln_gb = jnp.stack([a_ln_g, a_ln_b])
    ln_slot = lax.dynamic_update_slice(jnp.zeros((N_CHIPS,) + ln_gb.shape, F32), ln_gb[None], (chip, 0, 0, 0))
    gathered = _gather_weights(slots + [ln_slot])
    gw = dict(zip(BIG, gathered[:-1]))
    ln_full = gathered[-1].transpose(1, 2, 0, 3).reshape(2, N_A, 1, -1)
    a_ln_g3, a_ln_b3 = ln_full[0], ln_full[1]
    mix_g3, mix_b3 = mix_ln_g[:, None, :], mix_ln_b[:, None, :]
    ffn_g3, ffn_b3 = ffn_ln_g[:, None, :], ffn_ln_b[:, None, :]
    bst = jnp.swapaxes(a_b_s, 1, 2)

    saved = []
    xb = _cast_bf16(xf)
    kb = vb = None
    for l in range(DEPTH):
        sv = dict(x_in=xb)
        if l < N_A:
            h = _mm_fwd("a_in", xb, gw["a_w_in"], l, True)[0]
            vn = _gmlp_norm_fwd(h, a_ln_g3, a_ln_b3, l)
            gated = _gate_fwd(h, vn, a_w_s[l], bst[l])
            xf, xb, xhat, rstd = _mm_resid_ln("a_out", gated, gw["a_w_out"], l, xf, mix_g3, mix_b3, l)
            sv.update(h=h, vn=vn, gated=gated)
        else:
            j = l - N_A
            if l == N_A:
                kb = _mm_fwd("sb_k", xb, gw["sb_w_k"], None, False, _ep_bf16, outs=[(d, BF16)])[0]
                vb = _mm_fwd("sb_v", xb, gw["sb_w_v"], None, False, _ep_bf16, outs=[(d, BF16)])[0]
            q = _mm_fwd("b_q", xb, gw["b_w_q"], j, False, _ep_scale_q, outs=[(d, BF16)])[0]
            ob, lsum = _attn_fwd(q, kb, vb)
            xf, xb, xhat, rstd = _mm_resid_ln("b_out", ob, gw["b_w_o"], j, xf, mix_g3, mix_b3, l)
            sv.update(q=q, lsum=lsum, ob=ob)
        sv.update(x_mid=xb, xhat1=xhat, rstd1=rstd)
        dff = gw["ffn_w1"].shape[-1] * N_CHIPS
        pr, act = _mm_fwd("ffn_1", xb, gw["ffn_w1"], l, True, _ep_relu2, outs=[(dff, BF16), (dff, BF16)])
        xf, xb, xhat, rstd = _mm_resid_ln("ffn_2", act, gw["ffn_w2"], l, xf, ffn_g3, ffn_b3, l)
        sv.update(pr=pr, act=act, xhat2=xhat, rstd2=rstd)
        saved.append(sv)

    dx, sq = _loss_head(xf, target)
    loss = lax.psum(0.5 * sq[0, 0] / d, ("x", "y", "c"))

    gbuf = {n: lax.empty((N_CHIPS,) + given[n].shape, F32) for n in BIG}
    small = {}
    d_mix_g, d_mix_b, d_ffn_g, d_ffn_b = [None] * DEPTH, [None] * DEPTH, [None] * DEPTH, [None] * DEPTH
    d_ln_g, d_ln_b, d_ws, d_bs = [None] * N_A, [None] * N_A, [None] * N_A, [None] * N_A
    dk = dv = None
    for l in reversed(range(DEPTH)):
        sv = saved[l]
        dr, drb, d_ffn_g[l], d_ffn_b[l] = _ln_bwd(dx, sv["xhat2"], sv["rstd2"], ffn_g3, l)
        dff = sv["pr"].shape[1]
        dhd = _mm_bwd_act("ffn_2_dx", drb, gw["ffn_w2"], l, False, _ep_relu2_bwd, (sv["pr"],),
                          (pl.BlockSpec((TM, dff // N_CHIPS), lambda j, i, k: (i, j)),), out_dtype=BF16)
        gbuf["ffn_w2"] = _mm_bwd_w("ffn_2_dw", sv["act"], drb, gbuf["ffn_w2"], l, False)
        dx = _mm_bwd_act("ffn_1_dx", dhd, gw["ffn_w1"], l, True, _ep_resid, (dr,), (_row_spec(d),))
        gbuf["ffn_w1"] = _mm_bwd_w("ffn_1_dw", sv["x_mid"], dhd, gbuf["ffn_w1"], l, True)

        dr, drb, d_mix_g[l], d_mix_b[l] = _ln_bwd(dx, sv["xhat1"], sv["rstd1"], mix_g3, l)
        quarter = pl.BlockSpec((TM, d // N_CHIPS), lambda j, i, k: (i, j))
        if l < N_A:
            dgated = _mm_bwd_act("a_out_dx", drb, gw["a_w_out"], l, False)
            gbuf["a_w_out"] = _mm_bwd_w("a_out_dw", sv["gated"], drb, gbuf["a_w_out"], l, False)
            du, dvn, d_ws[l], dbs_wide = _gate_bwd(dgated, sv["h"], sv["vn"], a_w_s[l], bst[l])
            d_bs[l] = dbs_wide[:, :, 0]
            dh, dlg, dlb = _gmlp_in_bwd(sv["h"], du, dvn, a_ln_g3, l)
            d_ln_g[l], d_ln_b[l] = dlg[0], dlb[0]
            dx = _mm_bwd_act("a_in_dx", dh, gw["a_w_in"], l, True, _ep_resid, (dr,), (_row_spec(d),))
            gbuf["a_w_in"] = _mm_bwd_w("a_in_dw", sv["x_in"], dh, gbuf["a_w_in"], l, True)
        else:
            j = l - N_A
            do = _mm_bwd_act("b_out_dx", drb, gw["b_w_o"], j, False)
            gbuf["b_w_o"] = _mm_bwd_w("b_out_dw", sv["ob"], drb, gbuf["b_w_o"], j, False)
            dq, dk, dv = _attn_bwd(sv["q"], kb, vb, do, sv["lsum"], dk, dv)
            dx = _mm_bwd_act("b_q_dx", dq, gw["b_w_q"], j, False, _ep_resid, (dr,), (quarter,))
            gbuf["b_w_q"] = _mm_bwd_w("b_q_dw", sv["x_in"], dq, gbuf["b_w_q"], j, False)
            if l == N_A:
                dx = _mm_bwd_act("sb_k_dx", dk, gw["sb_w_k"], None, False, _ep_add, (dx,), (quarter,))
                gbuf["sb_w_k"] = _mm_bwd_w("sb_k_dw", sv["x_in"], dk, gbuf["sb_w_k"], None, False)
                dx = _mm_bwd_act("sb_v_dx", dv, gw["sb_w_v"], None, False, _ep_add, (dx,), (quarter,))
                gbuf["sb_w_v"] = _mm_bwd_w("sb_v_dw", sv["x_in"], dv, gbuf["sb_w_v"], None, False)
    grad_x = dx.reshape(x.shape)

    flat = [gbuf[n].reshape(N_CHIPS, -1, gbuf[n].shape[-1]) for n in BIG]
    got = _pair_exchange(flat)
    sums = [_pair_sum(g, r, core_arr) for g, r in zip(flat, got)]
    landed = _chip_exchange([pb for _, pb in sums])
    place_arr = jnp.stack([chip, cc.astype(jnp.int32)])
    halves = [_chip_sum(p, r, place_arr) for (p, _), r in zip(sums, landed)]
    grads = dict(zip(BIG, [g.reshape(given[n].shape) for n, g in zip(BIG, _half_swap(halves))]))

    small_full = dict(a_ln_g=jnp.stack(d_ln_g), a_ln_b=jnp.stack(d_ln_b), a_w_s=jnp.stack(d_ws), a_b_s=jnp.stack(d_bs),
                      mix_ln_g=jnp.concatenate(d_mix_g), mix_ln_b=jnp.concatenate(d_mix_b),
                      ffn_ln_g=jnp.concatenate(d_ffn_g), ffn_ln_b=jnp.concatenate(d_ffn_b))
    packed = jnp.concatenate([small_full[n].reshape(-1) for n in SMALL])
    total = packed.shape[0]
    ncol = -(-total // (N_DEV * LANES)) * LANES
    packed = jnp.pad(packed, (0, N_DEV * ncol - total)).reshape(N_DEV, ncol)
    reduced = _all_reduce_small(packed).reshape(-1)
    off = 0
    for n in SMALL:
        size = small_full[n].size
        g = reduced[off:off + size].reshape(small_full[n].shape)
        off += size
        if n in ("a_ln_g", "a_ln_b"):
            wq = given[n].shape[1]
            g = lax.dynamic_slice_in_dim(g, chip * wq, wq, axis=1)
        grads[n] = g

    delta, new_m, new_v = {}, {}, {}
    for n in names:
        shape = given[n].shape
        dl, nm, nv = _adamw(as2d(given[n]), as2d(grads[n]), as2d(mom[n]), as2d(var[n]))
        delta[n], new_m[n], new_v[n] = dl.reshape(shape), nm.reshape(shape), nv.reshape(shape)

    order = ("a_w_in", "a_ln_g", "a_ln_b", "a_w_s", "a_b_s", "a_w_out", "sb_w_k", "sb_w_v", "b_w_q", "b_w_o",
             "mix_ln_g", "mix_ln_b", "ffn_ln_g", "ffn_ln_b", "ffn_w1", "ffn_w2")
    return (loss, grad_x, *[grads[n] for n in order], *[delta[n] for n in order],
            *[new_m[n] for n in order], *[new_v[n] for n in order])
```

```python
import math

import jax
import jax.numpy as jnp
from jax import lax
from jax.experimental import pallas as pl
from jax.experimental.pallas import tpu as pltpu

F32 = jnp.float32
BF16 = jnp.bfloat16
MESH = pl.DeviceIdType.MESH

N_CHIPS = 4
DEPTH = 4
N_A = 2
ALPHA = float((2 * DEPTH) ** 0.25)
LN_EPS = 1e-5
CHUNK = 64
GMLP_BLOCK = 128
GMLP_GROUPS = 8
HEAD_DIM = 64
LANES = 128
ATT_T = 256
ADAM_LR = 0.001
ADAM_B1 = 0.9
ADAM_B2 = 0.999
ADAM_EPS = 1e-08
ADAM_WD = 0.01
ADAM_STEP = 10
VMEM_LIMIT = 56 * 1024 * 1024
TM = 512
TS = 1024

NN = ((1,), (0,))
NT = ((1,), (1,))
TN = ((0,), (0,))


def _params(sem):
    return pltpu.CompilerParams(dimension_semantics=sem, vmem_limit_bytes=VMEM_LIMIT)


def _dot(a, b, contract):
    return lax.dot_general(a, b, (contract, ((), ())), preferred_element_type=F32)


def _rider_copies(kind, buf, send_sems, recv_sems, base):
    x, y, c, chips = _place()
    me = 2 * x + y
    h = buf.shape[1] // 2
    starts, arrivals = [], []
    for k, chip in enumerate(chips):
        blk = 2 * chip[0] + chip[1]

        def copy(part, to):
            return pltpu.make_async_remote_copy(src_ref=part, dst_ref=part, send_sem=send_sems.at[base + k],
                                                recv_sem=recv_sems.at[base + k], device_id=to, device_id_type=MESH)

        if kind == "ici":
            starts.append(copy(buf.at[me, pl.ds(c * h, h)], (*chip, c)))
            arrivals.append(copy(buf.at[blk, pl.ds(c * h, h)], (*chip, c)))
        else:
            starts.append(copy(buf.at[blk, pl.ds(c * h, h)], (x, y, 1 - c)))
            arrivals.append(copy(buf.at[blk, pl.ds((1 - c) * h, h)], (x, y, 1 - c)))
    return starts, arrivals


def _matmul(name, operands, in_specs, out_shapes, out_specs, grid, contract, epilogue, acc_shape, aliases=None,
            chunks=None, riders=()):
    nk = grid[2]
    n_in, n_out, nr = len(operands), len(out_shapes), len(riders)
    n_plain = n_in + nr + n_out

    def body(*refs):
        ins, outs = refs[:n_in], refs[n_in + nr:n_plain]
        if nr:
            bufs = refs[n_plain:n_plain + nr]
            send_sems, recv_sems = refs[-2:]
            pid = [pl.program_id(ax) for ax in range(3)]
            first = (pid[0] == 0) & (pid[1] == 0) & (pid[2] == 0)
            last = (pid[0] == grid[0] - 1) & (pid[1] == grid[1] - 1) & (pid[2] == grid[2] - 1)

            @pl.when(first)
            def _():
                for n, (kind, _) in enumerate(riders):
                    for cp in _rider_copies(kind, bufs[n], send_sems, recv_sems, 3 * n)[0]:
                        cp.start()

        compute(refs, ins, outs)
        if nr:
            @pl.when(last)
            def _():
                for n, (kind, _) in enumerate(riders):
                    starts, arrivals = _rider_copies(kind, bufs[n], send_sems, recv_sems, 3 * n)
                    for cp in arrivals:
                        cp.wait_recv()
                    for cp in starts:
                        cp.wait_send()

    def compute(refs, ins, outs):
        if chunks is None:
            p = _dot(ins[0][...].astype(BF16), ins[1][...].astype(BF16), contract)
        else:
            width = ins[0].shape[1] // chunks
            p = None
            for j in range(chunks):
                pj = _dot(ins[0][:, j * width:(j + 1) * width].astype(BF16), ins[1][j].astype(BF16), contract)
                p = pj if p is None else p + pj
        if nk == 1:
            epilogue(p, ins[2:], outs)
            return
        acc = refs[n_plain + nr]
        k = pl.program_id(2)

        @pl.when(k == 0)
        def _():
            acc[...] = p

        @pl.when((k > 0) & (k < nk - 1))
        def _():
            acc[...] += p

        @pl.when(k == nk - 1)
        def _():
            epilogue(acc[...] + p, ins[2:], outs)

    rbufs = [b for _, b in riders]
    scratch = ([] if nk == 1 else [pltpu.VMEM(acc_shape, F32)]) + [pltpu.SemaphoreType.DMA((3 * nr,))] * (2 if nr else 0)
    return pl.pallas_call(
        body, name=name, grid=grid, in_specs=list(in_specs) + _any_specs(nr), out_specs=list(out_specs) + _any_specs(nr),
        out_shape=list(out_shapes) + [jax.ShapeDtypeStruct(b.shape, b.dtype) for b in rbufs],
        scratch_shapes=scratch,
        input_output_aliases={**(aliases or {}), **{n_in + n: n_out + n for n in range(nr)}},
        compiler_params=_params(("arbitrary",) * 3 if nr else ("parallel", "parallel", "arbitrary")),
    )(*operands, *rbufs)


def _wspec(w, layer, whole=False):
    r, c = w.shape[-2:]
    lead = N_CHIPS if whole else None
    if w.ndim == 4:
        return pl.BlockSpec((lead, None, r, c), lambda j, i, k: (0 if whole else j, layer, 0, 0))
    return pl.BlockSpec((lead, r, c), lambda j, i, k: (0 if whole else j, 0, 0))


def _ep_store(p, ins, outs):
    for o in outs:
        o[...] = p.astype(o.dtype)


def _mm_fwd(name, a, w, layer, col_sharded, epilogue=_ep_store, extras=(), extra_specs=(), outs=None, riders=()):
    s = a.shape[0]
    r, c = w.shape[-2:]
    if col_sharded:
        grid = (N_CHIPS, s // TM, 1)
        a_spec = pl.BlockSpec((TM, r), lambda j, i, k: (i, 0))
        n_out = N_CHIPS * c
    else:
        grid = (1, s // TM, 1)
        a_spec = pl.BlockSpec((TM, N_CHIPS * r), lambda j, i, k: (i, 0))
        n_out = c
    if outs is None:
        outs = [(n_out, F32)]
    out_shapes = [jax.ShapeDtypeStruct((s, n), dt) for n, dt in outs]
    out_specs = [pl.BlockSpec((TM, c if n == n_out else n), lambda j, i, k: (i, j)) for n, _ in outs]
    return _matmul(name, (a, w) + tuple(extras), [a_spec, _wspec(w, layer, not col_sharded)] + list(extra_specs),
                   out_shapes, out_specs, grid, NN, epilogue, (TM, c), chunks=None if col_sharded else N_CHIPS,
                   riders=riders)


def _mm_bwd_act(name, dy, w, layer, col_sharded, epilogue=_ep_store, extras=(), extra_specs=(), out_dtype=F32):
    s = dy.shape[0]
    r, c = w.shape[-2:]
    if col_sharded:
        grid = (1, s // TM, 1)
        a_spec = pl.BlockSpec((TM, N_CHIPS * c), lambda j, i, k: (i, 0))
        n_out = r
    else:
        grid = (N_CHIPS, s // TM, 1)
        a_spec = pl.BlockSpec((TM, c), lambda j, i, k: (i, 0))
        n_out = N_CHIPS * r
    o_spec = pl.BlockSpec((TM, r), lambda j, i, k: (i, j))
    return _matmul(name, (dy, w) + tuple(extras), [a_spec, _wspec(w, layer, col_sharded)] + list(extra_specs),
                   [jax.ShapeDtypeStruct((s, n_out), out_dtype)], [o_spec], grid, NT, epilogue, (TM, r),
                   chunks=N_CHIPS if col_sharded else None)[0]


def _mm_bwd_w(name, a, dy, buf, layer, col_sharded):
    s = a.shape[0]
    r, c = buf.shape[-2:]
    ts = min(TS, s)
    grid = (N_CHIPS, 1, s // ts)
    if col_sharded:
        a_spec = pl.BlockSpec((ts, r), lambda j, i, k: (k, 0))
        b_spec = pl.BlockSpec((ts, c), lambda j, i, k: (k, j))
    else:
        a_spec = pl.BlockSpec((ts, r), lambda j, i, k: (k, j))
        b_spec = pl.BlockSpec((ts, c), lambda j, i, k: (k, 0))

    def epilogue(p, ins, outs):
        outs[0][...] = p

    return _matmul(name, (a, dy, buf), [a_spec, b_spec, pl.BlockSpec(memory_space=pl.ANY)],
                   [jax.ShapeDtypeStruct(buf.shape, F32)], [_wspec(buf, layer)], grid, TN, epilogue, (r, c),
                   aliases={2: 0})[0]


def _row_spec(n):
    return pl.BlockSpec((TM, n), lambda j, i, k: (i, 0))


def _vec_spec(layer, n):
    return pl.BlockSpec((None, 1, n), lambda j, i, k: (layer, 0, 0))


def _ep_resid_ln(p, ins, outs):
    x_ref, g_ref, b_ref = ins
    xf_ref, xb_ref, xhat_ref, rstd_ref = outs
    r = ALPHA * x_ref[...] + p
    mu = jnp.mean(r, axis=-1, keepdims=True)
    d = r - mu
    var = jnp.mean(d * d, axis=-1, keepdims=True)
    rstd = lax.rsqrt(var + LN_EPS)
    xhat = d * rstd
    y = xhat * g_ref[...] + b_ref[...]
    xf_ref[...] = y
    xb_ref[...] = y.astype(BF16)
    xhat_ref[...] = xhat
    rstd_ref[...] = rstd


def _mm_resid_ln(name, a, w, x, g3, b3, ln_layer, riders=()):
    d = x.shape[1]
    return _mm_fwd(name, a, w, None, False, _ep_resid_ln, (x, g3, b3),
                   (_row_spec(d), _vec_spec(ln_layer, d), _vec_spec(ln_layer, d)),
                   outs=[(d, F32), (d, BF16), (d, F32), (1, F32)], riders=riders)


def _ep_relu2(p, ins, outs):
    h = jnp.maximum(p, 0.0)
    outs[0][...] = h.astype(BF16)
    outs[1][...] = (h * h).astype(BF16)


def _ep_scale_q(p, ins, outs):
    outs[0][...] = (p * (HEAD_DIM ** -0.5)).astype(BF16)


def _ep_bf16(p, ins, outs):
    outs[0][...] = p.astype(BF16)


def _ep_relu2_bwd(p, ins, outs):
    outs[0][...] = (p * (2.0 * ins[0][...].astype(F32))).astype(BF16)


def _ep_resid(p, ins, outs):
    outs[0][...] = ALPHA * ins[0][...] + p


def _ep_add(p, ins, outs):
    outs[0][...] = ins[0][...] + p


def _gelu_grad(x):
    c0 = math.sqrt(2.0 / math.pi)
    t = jnp.tanh(c0 * (x + 0.044715 * (x * x * x)))
    return 0.5 * (1.0 + t) + (0.5 * x) * (1.0 - t * t) * (c0 * (1.0 + 3.0 * 0.044715 * (x * x)))


def _cast_bf16(w2d):
    r, c = w2d.shape
    tr = min(r, 512)

    def body(w_ref, o_ref):
        o_ref[...] = w_ref[...].astype(BF16)

    return pl.pallas_call(
        body, name="cast_bf16", grid=(r // tr,),
        in_specs=[pl.BlockSpec((tr, c), lambda i: (i, 0))], out_specs=pl.BlockSpec((tr, c), lambda i: (i, 0)),
        out_shape=jax.ShapeDtypeStruct((r, c), BF16), compiler_params=_params(("parallel",)),
    )(w2d)


def _cast_into_slot(w, layer, chip):
    r, c = w.shape[-2:]
    tr = min(r, 512)

    def body(chip_ref, w_ref, o_ref):
        o_ref[...] = w_ref[...].astype(BF16)

    if layer is None:
        w_spec = pl.BlockSpec((tr, c), lambda i, chip_ref: (i, 0))
    else:
        w_spec = pl.BlockSpec((None, tr, c), lambda i, chip_ref: (layer, i, 0))
    grid_spec = pltpu.PrefetchScalarGridSpec(
        num_scalar_prefetch=1, grid=(r // tr,), in_specs=[w_spec],
        out_specs=pl.BlockSpec((None, tr, c), lambda i, chip_ref: (chip_ref[0], i, 0)))
    return pl.pallas_call(
        body, name="cast_into_slot", grid_spec=grid_spec,
        out_shape=jax.ShapeDtypeStruct((N_CHIPS, r, c), BF16), compiler_params=_params(("parallel",)),
    )(chip, w)


def _gmlp_norm_fwd(h, g3, b3, layer):
    s, w2 = h.shape
    w = w2 // 2

    def body(h_ref, g_ref, b_ref, o_ref):
        z = jax.nn.gelu(h_ref[...])
        mu = jnp.mean(z, axis=-1, keepdims=True)
        d = z - mu
        var = jnp.mean(d * d, axis=-1, keepdims=True)
        o_ref[...] = (d * lax.rsqrt(var + LN_EPS) * g_ref[...] + b_ref[...]).astype(BF16)

    vec = pl.BlockSpec((None, 1, w), lambda i: (layer, 0, 0))
    return pl.pallas_call(
        body, name="gmlp_norm_fwd", grid=(s // TM,),
        in_specs=[pl.BlockSpec((TM, w), lambda i: (i, 1)), vec, vec],
        out_specs=pl.BlockSpec((TM, w), lambda i: (i, 0)),
        out_shape=jax.ShapeDtypeStruct((s, w), BF16), compiler_params=_params(("parallel",)),
    )(h, g3, b3)


def _chunk_mask():
    t = lax.broadcasted_iota(jnp.int32, (GMLP_BLOCK, GMLP_BLOCK), 0)
    s = lax.broadcasted_iota(jnp.int32, (GMLP_BLOCK, GMLP_BLOCK), 1)
    return (s // CHUNK) <= (t // CHUNK)


SG_ROWS = 512


def _gate_fwd(h, vn, ws, bst):
    s, w = vn.shape
    gd = w // GMLP_GROUPS

    def body(h_ref, v_ref, ws_ref, bs_ref, o_ref):
        mask = _chunk_mask()
        for g in range(GMLP_GROUPS):
            wm = jnp.where(mask, ws_ref[g], 0.0).astype(BF16)
            bias = bs_ref[:, g:g + 1]
            cols = slice(g * gd, (g + 1) * gd)
            for n in range(SG_ROWS // GMLP_BLOCK):
                rows = slice(n * GMLP_BLOCK, (n + 1) * GMLP_BLOCK)
                sp = _dot(wm, v_ref[rows, cols], NN) + bias
                o_ref[rows, cols] = (jax.nn.gelu(h_ref[rows, cols]) * sp).astype(BF16)

    return pl.pallas_call(
        body, name="gate_fwd", grid=(s // SG_ROWS,),
        in_specs=[pl.BlockSpec((SG_ROWS, w), lambda i: (i, 0)), pl.BlockSpec((SG_ROWS, w), lambda i: (i, 0)),
                  pl.BlockSpec(ws.shape, lambda i: (0, 0, 0)), pl.BlockSpec(bst.shape, lambda i: (0, 0))],
        out_specs=pl.BlockSpec((SG_ROWS, w), lambda i: (i, 0)),
        out_shape=jax.ShapeDtypeStruct((s, w), BF16), compiler_params=_params(("parallel",)),
    )(h, vn, ws, bst)


def _gate_bwd(dgated, h, vn, ws, bst):
    s, w = vn.shape
    gd = w // GMLP_GROUPS
    nsteps = s // SG_ROWS

    def body(dg_ref, h_ref, v_ref, ws_ref, bs_ref, du_ref, dv_ref, dws_ref, dbs_ref, dsum):
        i = pl.program_id(0)

        @pl.when(i == 0)
        def _():
            dws_ref[...] = jnp.zeros_like(dws_ref)
            dsum[...] = jnp.zeros_like(dsum)

        mask = _chunk_mask()
        for g in range(GMLP_GROUPS):
            wm = jnp.where(mask, ws_ref[g], 0.0).astype(BF16)
            bias = bs_ref[:, g:g + 1]
            cols = slice(g * gd, (g + 1) * gd)
            dw = jnp.zeros((GMLP_BLOCK, GMLP_BLOCK), F32)
            dsg = jnp.zeros((GMLP_BLOCK, gd), F32)
            for n in range(SG_ROWS // GMLP_BLOCK):
                rows = slice(n * GMLP_BLOCK, (n + 1) * GMLP_BLOCK)
                vb = v_ref[rows, cols]
                sp = _dot(wm, vb, NN) + bias
                dg = dg_ref[rows, cols]
                du_ref[rows, cols] = dg * sp
                ds = dg * jax.nn.gelu(h_ref[rows, cols])
                dsb = ds.astype(BF16)
                dw += _dot(dsb, vb, NT)
                dsg += ds
                dv_ref[rows, cols] = _dot(wm, dsb, TN)
            dws_ref[g] += dw
            dsum[:, cols] += dsg

        @pl.when(i == nsteps - 1)
        def _():
            for g in range(GMLP_GROUPS):
                dws_ref[g] = jnp.where(mask, dws_ref[g], 0.0)
                tot = jnp.sum(dsum[:, g * gd:(g + 1) * gd], axis=-1, keepdims=True)
                dbs_ref[g] = jnp.broadcast_to(tot, (GMLP_BLOCK, LANES))

    tile = pl.BlockSpec((SG_ROWS, w), lambda i: (i, 0))
    return pl.pallas_call(
        body, name="gate_bwd", grid=(nsteps,),
        in_specs=[tile, tile, tile, pl.BlockSpec(ws.shape, lambda i: (0, 0, 0)), pl.BlockSpec(bst.shape, lambda i: (0, 0))],
        out_specs=[tile, tile, pl.BlockSpec(ws.shape, lambda i: (0, 0, 0)),
                   pl.BlockSpec((GMLP_GROUPS, GMLP_BLOCK, LANES), lambda i: (0, 0, 0))],
        out_shape=[jax.ShapeDtypeStruct((s, w), F32), jax.ShapeDtypeStruct((s, w), F32),
                   jax.ShapeDtypeStruct(ws.shape, F32), jax.ShapeDtypeStruct((GMLP_GROUPS, GMLP_BLOCK, LANES), F32)],
        scratch_shapes=[pltpu.VMEM((GMLP_BLOCK, w), F32)],
        compiler_params=_params(("arbitrary",)),
    )(dgated, h, vn, ws, bst)


GB_ROWS = 256


def _gmlp_in_bwd(h, du, dvn, g3, layer):
    s, w2 = h.shape
    w = w2 // 2
    nsteps = s // GB_ROWS

    def body(h_ref, du_ref, dv_ref, g_ref, dh_ref, dg_ref, db_ref):
        i = pl.program_id(0)

        @pl.when(i == 0)
        def _():
            dg_ref[...] = jnp.zeros_like(dg_ref)
            db_ref[...] = jnp.zeros_like(db_ref)

        hu = h_ref[:, :w]
        hv = h_ref[:, w:]
        dh_ref[:, :w] = (du_ref[...] * _gelu_grad(hu)).astype(BF16)
        z = jax.nn.gelu(hv)
        mu = jnp.mean(z, axis=-1, keepdims=True)
        d = z - mu
        var = jnp.mean(d * d, axis=-1, keepdims=True)
        rstd = lax.rsqrt(var + LN_EPS)
        xhat = d * rstd
        dy = dv_ref[...]
        db_ref[...] += jnp.sum(dy, axis=0, keepdims=True)
        dg_ref[...] += jnp.sum(dy * xhat, axis=0, keepdims=True)
        dxh = dy * g_ref[...]
        m1 = jnp.mean(dxh, axis=-1, keepdims=True)
        m2 = jnp.mean(dxh * xhat, axis=-1, keepdims=True)
        dz = rstd * (dxh - m1 - xhat * m2)
        dh_ref[:, w:] = (dz * _gelu_grad(hv)).astype(BF16)

    half = pl.BlockSpec((GB_ROWS, w), lambda i: (i, 0))
    vec = pl.BlockSpec((1, w), lambda i: (0, 0))
    return pl.pallas_call(
        body, name="gmlp_in_bwd", grid=(nsteps,),
        in_specs=[pl.BlockSpec((GB_ROWS, w2), lambda i: (i, 0)), half, half,
                  pl.BlockSpec((None, 1, w), lambda i: (layer, 0, 0))],
        out_specs=[pl.BlockSpec((GB_ROWS, w2), lambda i: (i, 0)), vec, vec],
        out_shape=[jax.ShapeDtypeStruct((s, w2), BF16), jax.ShapeDtypeStruct((1, w), F32), jax.ShapeDtypeStruct((1, w), F32)],
        compiler_params=_params(("arbitrary",)),
    )(h, du, dvn, g3)


def _ln_bwd(dy, xhat, rstd, g3, layer):
    s, d = dy.shape
    nsteps = s // TM

    def body(dy_ref, xh_ref, rs_ref, g_ref, dr_ref, drb_ref, dg_ref, db_ref):
        i = pl.program_id(0)

        @pl.when(i == 0)
        def _():
            dg_ref[...] = jnp.zeros_like(dg_ref)
            db_ref[...] = jnp.zeros_like(db_ref)

        dyv = dy_ref[...]
        xhat_v = xh_ref[...]
        db_ref[...] += jnp.sum(dyv, axis=0, keepdims=True)
        dg_ref[...] += jnp.sum(dyv * xhat_v, axis=0, keepdims=True)
        dxh = dyv * g_ref[...]
        m1 = jnp.mean(dxh, axis=-1, keepdims=True)
        m2 = jnp.mean(dxh * xhat_v, axis=-1, keepdims=True)
        dr = rs_ref[...] * (dxh - m1 - xhat_v * m2)
        dr_ref[...] = dr
        drb_ref[...] = dr.astype(BF16)

    tile = pl.BlockSpec((TM, d), lambda i: (i, 0))
    vec = pl.BlockSpec((1, d), lambda i: (0, 0))
    return pl.pallas_call(
        body, name="ln_bwd", grid=(nsteps,),
        in_specs=[tile, tile, pl.BlockSpec((TM, 1), lambda i: (i, 0)), pl.BlockSpec((None, 1, d), lambda i: (layer, 0, 0))],
        out_specs=[tile, tile, vec, vec],
        out_shape=[jax.ShapeDtypeStruct((s, d), F32), jax.ShapeDtypeStruct((s, d), BF16),
                   jax.ShapeDtypeStruct((1, d), F32), jax.ShapeDtypeStruct((1, d), F32)],
        compiler_params=_params(("arbitrary",)),
    )(dy, xhat, rstd, g3)


def _loss_head(y, target):
    s, d = y.shape

    def body(y_ref, t_ref, dy_ref, l_ref):
        i = pl.program_id(0)

        @pl.when(i == 0)
        def _():
            l_ref[...] = jnp.zeros_like(l_ref)

        e = y_ref[...] - t_ref[...]
        dy_ref[...] = e * (1.0 / d)
        l_ref[...] += jnp.sum(jnp.sum(e * e, axis=1, keepdims=True), axis=0, keepdims=True)

    tile = pl.BlockSpec((TM, d), lambda i: (i, 0))
    return pl.pallas_call(
        body, name="loss_head", grid=(s // TM,), in_specs=[tile, tile],
        out_specs=[tile, pl.BlockSpec((1, 1), lambda i: (0, 0))],
        out_shape=[jax.ShapeDtypeStruct((s, d), F32), jax.ShapeDtypeStruct((1, 1), F32)],
        compiler_params=_params(("arbitrary",)),
    )(y, target)


LOG2E = 1.4426950408889634
DEAD_LOG2 = -160.0
FIRST_LANE = 1


def _sb_terms(z, causal):
    z2 = z * LOG2E
    e = jnp.exp2(-jnp.abs(z2))
    l1p = jnp.log2(1.0 + e)
    lb = jnp.minimum(z2, 0.0) - l1p
    lr = lb - z2
    if causal is not None:
        lr = jnp.where(causal, lr, 0.0)
    return lb, lr, e


def _split_hi_lo(x):
    hi = x.astype(BF16)
    lo = (x - hi.astype(F32)).astype(BF16)
    return jnp.concatenate([hi, lo], axis=1)


def _att_consts(prefix):
    r = lax.broadcasted_iota(jnp.int32, (2 * ATT_T, ATT_T), 0) % ATT_T
    c = lax.broadcasted_iota(jnp.int32, (2 * ATT_T, ATT_T), 1)
    tri2 = jnp.where((r <= c) if prefix else (r >= c), 1.0, 0.0).astype(BF16)
    r = lax.broadcasted_iota(jnp.int32, (ATT_T, ATT_T), 0)
    c = lax.broadcasted_iota(jnp.int32, (ATT_T, ATT_T), 1)
    causal = c < r
    head_a = lax.broadcasted_iota(jnp.int32, (1, LANES), 1) < HEAD_DIM
    return tri2, causal, head_a


def _attn_fwd(q, k, v):
    s, d = q.shape
    nq = s // ATT_T

    def body(q_ref, k_ref, v_ref, ob_ref, lsum_ref, acc_a, acc_b, rem_a, rem_b):
        i = pl.program_id(1)
        tri, causal, head_a = _att_consts(prefix=False)
        q2 = q_ref[...]
        zero = jnp.zeros_like(q2)
        qa = jnp.where(head_a, q2, zero)
        qb = jnp.where(head_a, zero, q2)
        acc_a[...] = jnp.zeros_like(acc_a)
        acc_b[...] = jnp.zeros_like(acc_b)
        rem_a[...] = jnp.zeros_like(rem_a)
        rem_b[...] = jnp.zeros_like(rem_b)

        def block(kb, mask):
            rows = pl.ds(pl.multiple_of(kb * ATT_T, ATT_T), ATT_T)
            k2 = k_ref[rows, :]
            v2 = v_ref[rows, :]
            heads = ((qa, acc_a, rem_a), (qb, acc_b, rem_b))
            zs = [_dot(qm, k2, NT) for qm, _, _ in heads]
            terms = [_sb_terms(z, mask) for z in zs]
            sums = [_dot(_split_hi_lo(lr), tri, NN) for _, lr, _ in terms]
            for (_, acc, rem), (lb, lr, _), sincl in zip(heads, terms, sums):
                a = jnp.exp2(lb + (sincl - lr) + rem[...])
                if mask is not None:
                    a = jnp.where(mask, a, 0.0)
                rem[...] += sincl[:, 0:1]
                acc[...] += _dot(a.astype(BF16), v2, NN)

        block(i, causal)

        def live():
            return jnp.maximum(jnp.max(rem_a[...]), jnp.max(rem_b[...])) > DEAD_LOG2

        def go_on(carry):
            t, alive = carry
            return (t < i) & alive

        def step(carry):
            t, _ = carry
            block(i - 1 - t, None)
            return t + 1, live()

        done, _ = lax.while_loop(go_on, step, (jnp.int32(0), live()))
        first = (i - done).astype(F32)
        ob_ref[...] = jnp.where(head_a, acc_a[...], acc_b[...]).astype(BF16)
        lane = lax.broadcasted_iota(jnp.int32, (1, LANES), 1)
        lsum_ref[...] = jnp.where(lane == FIRST_LANE, first, jnp.where(head_a, rem_a[...], rem_b[...]))

    qspec = pl.BlockSpec((ATT_T, LANES), lambda p, i: (i, p))
    kspec = pl.BlockSpec((s, LANES), lambda p, i: (0, p))
    return pl.pallas_call(
        body, name="attn_fwd", grid=(d // LANES, nq), in_specs=[qspec, kspec, kspec],
        out_specs=[qspec, qspec],
        out_shape=[jax.ShapeDtypeStruct((s, d), BF16), jax.ShapeDtypeStruct((s, d), F32)],
        scratch_shapes=[pltpu.VMEM((ATT_T, LANES), F32), pltpu.VMEM((ATT_T, LANES), F32),
                        pltpu.VMEM((ATT_T, 1), F32), pltpu.VMEM((ATT_T, 1), F32)],
        compiler_params=_params(("parallel", "arbitrary")),
    )(q, k, v)


def _attn_bwd(q, k, v, do, lsum, dk_prev=None, dv_prev=None):
    s, d = q.shape
    nq = s // ATT_T
    has_prev = dk_prev is not None

    def body(*refs):
        q_ref, k_ref, v_ref, do_ref, ls_ref = refs[:5]
        n_in = 7 if has_prev else 5
        dq_ref, dk_ref, dv_ref, acc_a, acc_b, pre_a, pre_b, gp_a, gp_b = refs[n_in:]
        i = pl.program_id(1)

        @pl.when(i == 0)
        def _():
            if has_prev:
                dk_ref[...] = refs[5][...]
                dv_ref[...] = refs[6][...]
            else:
                dk_ref[...] = jnp.zeros_like(dk_ref)
                dv_ref[...] = jnp.zeros_like(dv_ref)

        tri, causal, head_a = _att_consts(prefix=True)
        q2 = q_ref[...]
        zero = jnp.zeros_like(q2)
        qa = jnp.where(head_a, q2, zero)
        qb = jnp.where(head_a, zero, q2)
        do2 = do_ref[...]
        doa = jnp.where(head_a, do2, 0.0).astype(BF16)
        dob = jnp.where(head_a, 0.0, do2).astype(BF16)
        ls2 = ls_ref[...]
        tot_a = ls2[:, 0:1]
        tot_b = ls2[:, HEAD_DIM:HEAD_DIM + 1]
        for r in (acc_a, acc_b, pre_a, pre_b, gp_a, gp_b):
            r[...] = jnp.zeros_like(r)

        def block(kb, mask):
            rows = pl.ds(pl.multiple_of(kb * ATT_T, ATT_T), ATT_T)
            k2 = k_ref[rows, :]
            v2 = v_ref[rows, :]
            dk_new = jnp.zeros((ATT_T, LANES), F32)
            dv_new = jnp.zeros((ATT_T, LANES), F32)
            heads = ((qa, doa, tot_a, acc_a, pre_a, gp_a), (qb, dob, tot_b, acc_b, pre_b, gp_b))
            zs = [_dot(h[0], k2, NT) for h in heads]
            das = [_dot(h[1], v2, NT) for h in heads]
            terms = [_sb_terms(z, mask) for z in zs]
            psums = [_dot(_split_hi_lo(lr), tri, NN) for _, lr, _ in terms]
            gs, abs_ = [], []
            for (_, _, tot, _, pre, _), (lb, _, _), pincl, da in zip(heads, terms, psums, das):
                a = jnp.exp2(lb + (tot - (pre[...] + pincl)))
                if mask is not None:
                    a = jnp.where(mask, a, 0.0)
                pre[...] += pincl[:, ATT_T - 1:ATT_T]
                gs.append(a * da)
                abs_.append(a.astype(BF16))
            gsums = [_dot(_split_hi_lo(g), tri, NN) for g in gs]
            dzs = []
            for (_, _, _, _, _, gpre), z, (_, _, e), g, gincl in zip(heads, zs, terms, gs, gsums):
                gbefore = gpre[...] + (gincl - g)
                gpre[...] += gincl[:, ATT_T - 1:ATT_T]
                inv = 1.0 / (1.0 + e)
                beta = jnp.where(z >= 0.0, inv, e * inv)
                dz = g - beta * (g + gbefore)
                if mask is not None:
                    dz = jnp.where(mask, dz, 0.0)
                dzs.append(dz.astype(BF16))
            for (qm, dom, _, acc, _, _), ab, dzb in zip(heads, abs_, dzs):
                dv_new += _dot(ab, dom, TN)
                dk_new += _dot(dzb, qm, TN)
                acc[...] += _dot(dzb, k2, NN)
            dk_ref[rows, :] += dk_new
            dv_ref[rows, :] += dv_new

        def step(kb, carry):
            block(kb, None)
            return carry

        first = jnp.clip(jnp.max(ls2[:, FIRST_LANE:FIRST_LANE + 1]).astype(jnp.int32), 0, i)
        lax.fori_loop(first, i, step, 0)
        block(i, causal)
        dq_ref[...] = (jnp.where(head_a, acc_a[...], acc_b[...]) * (HEAD_DIM ** -0.5)).astype(BF16)

    qspec = pl.BlockSpec((ATT_T, LANES), lambda p, i: (i, p))
    kspec = pl.BlockSpec((s, LANES), lambda p, i: (0, p))
    ins = [q, k, v, do, lsum] + ([dk_prev, dv_prev] if has_prev else [])
    return pl.pallas_call(
        body, name="attn_bwd", grid=(d // LANES, nq),
        in_specs=[qspec, kspec, kspec, qspec, qspec] + ([kspec, kspec] if has_prev else []),
        out_specs=[qspec, kspec, kspec],
        out_shape=[jax.ShapeDtypeStruct((s, d), BF16), jax.ShapeDtypeStruct((s, d), F32), jax.ShapeDtypeStruct((s, d), F32)],
        scratch_shapes=[pltpu.VMEM((ATT_T, LANES), F32), pltpu.VMEM((ATT_T, LANES), F32)]
        + [pltpu.VMEM((ATT_T, 1), F32)] * 4,
        compiler_params=_params(("parallel", "arbitrary")),
    )(*ins)


def _place():
    x, y, c = lax.axis_index("x"), lax.axis_index("y"), lax.axis_index("c")
    chips = [(1 - x, y), (x, 1 - y), (1 - x, 1 - y)]
    return x, y, c, chips


def _any_specs(n):
    return [pl.BlockSpec(memory_space=pl.ANY)] * n


def _gather_weights(bufs):
    n = len(bufs)

    def body(*refs):
        outs = refs[n:2 * n]
        send_sems, recv_sems = refs[2 * n:]
        x, y, c, chips = _place()
        me = 2 * x + y
        sibling = (x, y, 1 - c)

        def half(a, blk, hc):
            h = outs[a].shape[1] // 2
            return outs[a].at[blk, pl.ds(hc * h, h)]

        def copy(a, k, part, to):
            return pltpu.make_async_remote_copy(src_ref=part, dst_ref=part, send_sem=send_sems.at[a, k],
                                                recv_sem=recv_sems.at[a, k], device_id=to, device_id_type=MESH)

        sent = []
        for a in range(n):
            for k, chip in enumerate(chips):
                sent.append(copy(a, k, half(a, me, c), (*chip, c)))
                sent[-1].start()
        for a in range(n):
            for k, chip in enumerate(chips):
                blk = 2 * chip[0] + chip[1]
                copy(a, k, half(a, blk, c), sibling).wait_recv()
                sent.append(copy(a, 3 + k, half(a, blk, c), sibling))
                sent[-1].start()
        for a in range(n):
            for k, chip in enumerate(chips):
                blk = 2 * chip[0] + chip[1]
                copy(a, 3 + k, half(a, blk, 1 - c), sibling).wait_recv()
        for cp in sent:
            cp.wait_send()

    return pl.pallas_call(
        body, name="gather_weights", in_specs=_any_specs(n), out_specs=_any_specs(n),
        out_shape=[jax.ShapeDtypeStruct(w.shape, w.dtype) for w in bufs],
        input_output_aliases={a: a for a in range(n)},
        scratch_shapes=[pltpu.SemaphoreType.DMA((n, 6)), pltpu.SemaphoreType.DMA((n, 6))],
        compiler_params=pltpu.CompilerParams(has_side_effects=True),
    )(*bufs)


def _pair_exchange(grads):
    n = len(grads)

    def body(*refs):
        ins, outs = refs[:n], refs[n:2 * n]
        send_sems, recv_sems = refs[2 * n:]
        x, y, c, _ = _place()
        cps = []
        for a in range(n):
            h = ins[a].shape[1] // 2
            cps.append(pltpu.make_async_remote_copy(
                src_ref=ins[a].at[:, pl.ds((1 - c) * h, h)], dst_ref=outs[a], send_sem=send_sems.at[a],
                recv_sem=recv_sems.at[a], device_id=(x, y, 1 - c), device_id_type=MESH))
            cps[-1].start()
        for cp in cps:
            cp.wait()

    return pl.pallas_call(
        body, name="pair_exchange", in_specs=_any_specs(n), out_specs=_any_specs(n),
        out_shape=[jax.ShapeDtypeStruct((g.shape[0], g.shape[1] // 2, g.shape[2]), g.dtype) for g in grads],
        scratch_shapes=[pltpu.SemaphoreType.DMA((n,)), pltpu.SemaphoreType.DMA((n,))],
        compiler_params=pltpu.CompilerParams(has_side_effects=True),
    )(*grads)


def _chip_exchange(parts):
    n = len(parts)

    def body(*refs):
        ins, outs = refs[:n], refs[n:2 * n]
        send_sems, recv_sems = refs[2 * n:]
        x, y, c, chips = _place()
        me = 2 * x + y
        cps = []
        for a in range(n):
            for k, chip in enumerate(chips):
                blk = 2 * chip[0] + chip[1]
                cps.append(pltpu.make_async_remote_copy(
                    src_ref=ins[a].at[blk], dst_ref=outs[a].at[me], send_sem=send_sems.at[a, k],
                    recv_sem=recv_sems.at[a, k], device_id=(*chip, c), device_id_type=MESH))
                cps[-1].start()
        for a in range(n):
            for k, chip in enumerate(chips):
                blk = 2 * chip[0] + chip[1]
                pltpu.make_async_remote_copy(
                    src_ref=ins[a].at[blk], dst_ref=outs[a].at[blk], send_sem=send_sems.at[a, k],
                    recv_sem=recv_sems.at[a, k], device_id=(*chip, c), device_id_type=MESH).wait_recv()
        for cp in cps:
            cp.wait_send()

    return pl.pallas_call(
        body, name="chip_exchange", in_specs=_any_specs(n), out_specs=_any_specs(n),
        out_shape=[jax.ShapeDtypeStruct(p.shape, p.dtype) for p in parts],
        scratch_shapes=[pltpu.SemaphoreType.DMA((n, 3)), pltpu.SemaphoreType.DMA((n, 3))],
        compiler_params=pltpu.CompilerParams(has_side_effects=True),
    )(*parts)


def _half_swap(halves):
    n = len(halves)

    def body(*refs):
        outs = refs[n:2 * n]
        send_sems, recv_sems = refs[2 * n:]
        x, y, c, _ = _place()
        cps = []
        for a in range(n):
            h = outs[a].shape[0] // 2
            mine = outs[a].at[pl.ds(c * h, h)]
            cps.append(pltpu.make_async_remote_copy(
                src_ref=mine, dst_ref=mine, send_sem=send_sems.at[a], recv_sem=recv_sems.at[a],
                device_id=(x, y, 1 - c), device_id_type=MESH))
            cps[-1].start()
        for cp in cps:
            cp.wait()

    return pl.pallas_call(
        body, name="half_swap", in_specs=_any_specs(n), out_specs=_any_specs(n),
        out_shape=[jax.ShapeDtypeStruct(p.shape, p.dtype) for p in halves],
        input_output_aliases={a: a for a in range(n)},
        scratch_shapes=[pltpu.SemaphoreType.DMA((n,)), pltpu.SemaphoreType.DMA((n,))],
        compiler_params=pltpu.CompilerParams(has_side_effects=True),
    )(*halves)


N_DEV = 8


def _all_reduce_small(v):
    nrow, ncol = v.shape

    def body(v_ref, o_ref, land, red, send_sems, recv_sems, send2, recv2, loc_sem):
        x, y, c, _ = _place()
        me = 4 * x + 2 * y + c
        peers = []
        for k in range(1, N_DEV):
            peers.append((x ^ ((k >> 2) & 1), y ^ ((k >> 1) & 1), c ^ (k & 1)))
        own = pltpu.make_async_copy(v_ref.at[pl.ds(me, 1)], land.at[pl.ds(me, 1)], loc_sem)
        own.start()
        cps = []
        for k, peer in enumerate(peers):
            dev = 4 * peer[0] + 2 * peer[1] + peer[2]
            cps.append(pltpu.make_async_remote_copy(
                src_ref=v_ref.at[pl.ds(dev, 1)], dst_ref=land.at[pl.ds(me, 1)], send_sem=send_sems.at[k],
                recv_sem=recv_sems.at[k], device_id=peer, device_id_type=MESH))
            cps[-1].start()
        for k, peer in enumerate(peers):
            dev = 4 * peer[0] + 2 * peer[1] + peer[2]
            pltpu.make_async_remote_copy(
                src_ref=v_ref.at[pl.ds(dev, 1)], dst_ref=land.at[pl.ds(dev, 1)], send_sem=send_sems.at[k],
                recv_sem=recv_sems.at[k], device_id=peer, device_id_type=MESH).wait_recv()
        for cp in cps:
            cp.wait_send()
        own.wait()
        terms = land[...]
        total = terms[0:1, :]
        for d in range(1, N_DEV):
            total = total + terms[d:d + 1, :]
        red[...] = total
        own = pltpu.make_async_copy(red, o_ref.at[pl.ds(me, 1)], loc_sem)
        own.start()
        cps = []
        for k, peer in enumerate(peers):
            cps.append(pltpu.make_async_remote_copy(
                src_ref=red, dst_ref=o_ref.at[pl.ds(me, 1)], send_sem=send2.at[k],
                recv_sem=recv2.at[k], device_id=peer, device_id_type=MESH))
            cps[-1].start()
        for k, peer in enumerate(peers):
            dev = 4 * peer[0] + 2 * peer[1] + peer[2]
            pltpu.make_async_remote_copy(
                src_ref=red, dst_ref=o_ref.at[pl.ds(dev, 1)], send_sem=send2.at[k],
                recv_sem=recv2.at[k], device_id=peer, device_id_type=MESH).wait_recv()
        for cp in cps:
            cp.wait_send()
        own.wait()

    vm = pl.BlockSpec(memory_space=pltpu.VMEM)
    return pl.pallas_call(
        body, name="all_reduce_small", in_specs=[vm], out_specs=vm,
        out_shape=jax.ShapeDtypeStruct((nrow, ncol), F32),
        scratch_shapes=[pltpu.VMEM((nrow, ncol), F32), pltpu.VMEM((1, ncol), F32)]
        + [pltpu.SemaphoreType.DMA((N_DEV - 1,))] * 4 + [pltpu.SemaphoreType.DMA],
        compiler_params=pltpu.CompilerParams(has_side_effects=True, vmem_limit_bytes=VMEM_LIMIT),
    )(v)


def _row_tile(rows):
    return min(rows, 512)


def _pair_sum(g, got, core):
    nb, r, c = g.shape
    h = r // 2
    tr = _row_tile(h)
    nt = h // tr

    def body(core_ref, g_ref, got_ref, p_ref, pb_ref):
        p = g_ref[...] + got_ref[...]
        p_ref[...] = p
        pb_ref[...] = p.astype(BF16)

    spec = pl.BlockSpec((None, tr, c), lambda j, t, core_ref: (j, t, 0))
    grid_spec = pltpu.PrefetchScalarGridSpec(
        num_scalar_prefetch=1, grid=(nb, nt),
        in_specs=[pl.BlockSpec((None, tr, c), lambda j, t, core_ref: (j, core_ref[0] * nt + t, 0)), spec],
        out_specs=[spec, spec])
    return pl.pallas_call(
        body, name="pair_sum", grid_spec=grid_spec,
        out_shape=[jax.ShapeDtypeStruct((nb, h, c), F32), jax.ShapeDtypeStruct((nb, h, c), BF16)],
        compiler_params=_params(("parallel", "parallel")),
    )(core, g, got)


def _chip_sum(p, got, place):
    nb, h, c = p.shape
    tr = _row_tile(h)
    nt = h // tr

    def body(place_ref, p_ref, g1_ref, g2_ref, g3_ref, o_ref):
        o_ref[...] = ((p_ref[...] + g1_ref[...].astype(F32)) + g2_ref[...].astype(F32)) + g3_ref[...].astype(F32)

    def blk(off):
        return pl.BlockSpec((None, tr, c), lambda t, place_ref: ((place_ref[0] + off) % N_CHIPS, t, 0))

    grid_spec = pltpu.PrefetchScalarGridSpec(
        num_scalar_prefetch=1, grid=(nt,), in_specs=[blk(0), blk(1), blk(2), blk(3)],
        out_specs=pl.BlockSpec((tr, c), lambda t, place_ref: (place_ref[1] * nt + t, 0)))
    return pl.pallas_call(
        body, name="chip_sum", grid_spec=grid_spec, out_shape=jax.ShapeDtypeStruct((2 * h, c), F32),
        compiler_params=_params(("parallel",)),
    )(place, p, got, got, got)


def _adamw(w, g, m, v):
    r, c = w.shape
    tr = r if r < 8 else _row_tile(r)

    def body(w_ref, g_ref, m_ref, v_ref, d_ref, nm_ref, nv_ref):
        gv = g_ref[...]
        nm = ADAM_B1 * m_ref[...] + (1.0 - ADAM_B1) * gv
        nv = ADAM_B2 * v_ref[...] + (1.0 - ADAM_B2) * (gv * gv)
        m_hat = nm / (1.0 - ADAM_B1 ** ADAM_STEP)
        v_hat = nv / (1.0 - ADAM_B2 ** ADAM_STEP)
        d_ref[...] = -ADAM_LR * (m_hat / (jnp.sqrt(v_hat) + ADAM_EPS) + ADAM_WD * w_ref[...])
        nm_ref[...] = nm
        nv_ref[...] = nv

    tile = pl.BlockSpec((tr, c), lambda i: (i, 0))
    return pl.pallas_call(
        body, name="adamw", grid=(r // tr,), in_specs=[tile] * 4, out_specs=[tile] * 3,
        out_shape=[jax.ShapeDtypeStruct((r, c), F32)] * 3, compiler_params=_params(("parallel",)),
    )(w, g, m, v)


BIG = ("a_w_in", "a_w_out", "sb_w_k", "sb_w_v", "b_w_q", "b_w_o", "ffn_w1", "ffn_w2")
SMALL = ("a_ln_g", "a_ln_b", "a_w_s", "a_b_s", "mix_ln_g", "mix_ln_b", "ffn_ln_g", "ffn_ln_b")
COL_SHARDED = {"a_w_in": True, "a_w_out": False, "sb_w_k": False, "sb_w_v": False, "b_w_q": False, "b_w_o": False,
               "ffn_w1": True, "ffn_w2": False}


def kernel(x, a_w_in, a_ln_g, a_ln_b, a_w_s, a_b_s, a_w_out, sb_w_k, sb_w_v, b_w_q, b_w_o, mix_ln_g, mix_ln_b, ffn_ln_g, ffn_ln_b, ffn_w1, ffn_w2, loss_target, m_a_w_in, m_a_ln_g, m_a_ln_b, m_a_w_s, m_a_b_s, m_a_w_out, m_sb_w_k, m_sb_w_v, m_b_w_q, m_b_w_o, m_mix_ln_g, m_mix_ln_b, m_ffn_ln_g, m_ffn_ln_b, m_ffn_w1, m_ffn_w2, v_a_w_in, v_a_ln_g, v_a_ln_b, v_a_w_s, v_a_b_s, v_a_w_out, v_sb_w_k, v_sb_w_v, v_b_w_q, v_b_w_o, v_mix_ln_g, v_mix_ln_b, v_ffn_ln_g, v_ffn_ln_b, v_ffn_w1, v_ffn_w2):
    names = BIG + SMALL
    given = dict(a_w_in=a_w_in, a_ln_g=a_ln_g, a_ln_b=a_ln_b, a_w_s=a_w_s, a_b_s=a_b_s, a_w_out=a_w_out, sb_w_k=sb_w_k,
                 sb_w_v=sb_w_v, b_w_q=b_w_q, b_w_o=b_w_o, mix_ln_g=mix_ln_g, mix_ln_b=mix_ln_b, ffn_ln_g=ffn_ln_g,
                 ffn_ln_b=ffn_ln_b, ffn_w1=ffn_w1, ffn_w2=ffn_w2)
    mom = dict(a_w_in=m_a_w_in, a_ln_g=m_a_ln_g, a_ln_b=m_a_ln_b, a_w_s=m_a_w_s, a_b_s=m_a_b_s, a_w_out=m_a_w_out,
               sb_w_k=m_sb_w_k, sb_w_v=m_sb_w_v, b_w_q=m_b_w_q, b_w_o=m_b_w_o, mix_ln_g=m_mix_ln_g, mix_ln_b=m_mix_ln_b,
               ffn_ln_g=m_ffn_ln_g, ffn_ln_b=m_ffn_ln_b, ffn_w1=m_ffn_w1, ffn_w2=m_ffn_w2)
    var = dict(a_w_in=v_a_w_in, a_ln_g=v_a_ln_g, a_ln_b=v_a_ln_b, a_w_s=v_a_w_s, a_b_s=v_a_b_s, a_w_out=v_a_w_out,
               sb_w_k=v_sb_w_k, sb_w_v=v_sb_w_v, b_w_q=v_b_w_q, b_w_o=v_b_w_o, mix_ln_g=v_mix_ln_g, mix_ln_b=v_mix_ln_b,
               ffn_ln_g=v_ffn_ln_g, ffn_ln_b=v_ffn_ln_b, ffn_w1=v_ffn_w1, ffn_w2=v_ffn_w2)

    cx, cy, cc = lax.axis_index("x"), lax.axis_index("y"), lax.axis_index("c")
    chip = (2 * cx + cy).astype(jnp.int32)
    chip_arr = chip.reshape(1)
    core_arr = cc.astype(jnp.int32).reshape(1)

    s, d = x.shape[1], x.shape[2]
    xf = x.reshape(s, d)
    target = loss_target.reshape(s, d)

    def as2d(w):
        return w.reshape(-1, w.shape[-1])

    gw = {}
    for n in BIG:
        for l in ([None] if given[n].ndim == 2 else range(given[n].shape[0])):
            gw[(n, l)] = _cast_into_slot(given[n], l, chip_arr)
    ln_gb = jnp.stack([a_ln_g, a_ln_b])
    ln_slot = lax.dynamic_update_slice(jnp.zeros((N_CHIPS,) + ln_gb.shape, F32), ln_gb[None], (chip, 0, 0, 0))
    layer0 = [("a_w_in", 0), ("a_w_out", 0), ("ffn_w1", 0), ("ffn_w2", 0)]
    gathered = _gather_weights([gw[k] for k in layer0] + [ln_slot])
    gw.update(zip(layer0, gathered[:-1]))
    mixer = {1: [("a_w_in", 1), ("a_w_out", 1)], 2: [("sb_w_k", None), ("sb_w_v", None), ("b_w_q", 0), ("b_w_o", 0)],
             3: [("b_w_q", 1), ("b_w_o", 1)]}

    def riding(d2d=(), ici=()):
        keys = list(d2d) + list(ici)
        return keys, [("d2d", gw[k]) for k in d2d] + [("ici", gw[k]) for k in ici]

    def landed_in(keys, bufs):
        gw.update(zip(keys, bufs))

    ln_full = gathered[-1].transpose(1, 2, 0, 3).reshape(2, N_A, 1, -1)
    a_ln_g3, a_ln_b3 = ln_full[0], ln_full[1]
    mix_g3, mix_b3 = mix_ln_g[:, None, :], mix_ln_b[:, None, :]
    ffn_g3, ffn_b3 = ffn_ln_g[:, None, :], ffn_ln_b[:, None, :]
    bst = jnp.swapaxes(a_b_s, 1, 2)

    saved = []
    xb = _cast_bf16(xf)
    kb = vb = None
    for l in range(DEPTH):
        sv = dict(x_in=xb)
        last = l == DEPTH - 1
        keys, riders = riding(d2d=[("ffn_w1", l), ("ffn_w2", l)] if l else [], ici=mixer[l + 1] if l < N_A else [])
        if l < N_A:
            h, *bufs = _mm_fwd("a_in", xb, gw[("a_w_in", l)], None, True, riders=riders)
            landed_in(keys, bufs)
            vn = _gmlp_norm_fwd(h, a_ln_g3, a_ln_b3, l)
            gated = _gate_fwd(h, vn, a_w_s[l], bst[l])
            xf, xb, xhat, rstd = _mm_resid_ln("a_out", gated, gw[("a_w_out", l)], xf, mix_g3, mix_b3, l)
            sv.update(h=h, vn=vn, gated=gated)
        else:
            j = l - N_A
            if l == N_A:
                kb, *bufs = _mm_fwd("sb_k", xb, gw[("sb_w_k", None)], None, False, _ep_bf16, outs=[(d, BF16)],
                                    riders=riders)
                landed_in(keys, bufs)
                keys, riders = [], ()
                vb = _mm_fwd("sb_v", xb, gw[("sb_w_v", None)], None, False, _ep_bf16, outs=[(d, BF16)])[0]
            q, *bufs = _mm_fwd("b_q", xb, gw[("b_w_q", j)], None, False, _ep_scale_q, outs=[(d, BF16)], riders=riders)
            landed_in(keys, bufs)
            ob, lsum = _attn_fwd(q, kb, vb)
            keys, riders = riding(ici=[] if last else mixer[l + 1])
            xf, xb, xhat, rstd, *bufs = _mm_resid_ln("b_out", ob, gw[("b_w_o", j)], xf, mix_g3, mix_b3, l, riders)
            landed_in(keys, bufs)
            sv.update(q=q, lsum=lsum, ob=ob)
        sv.update(x_mid=xb, xhat1=xhat, rstd1=rstd)
        dff = gw[("ffn_w1", l)].shape[-1] * N_CHIPS
        keys, riders = riding(ici=[] if last else [("ffn_w1", l + 1)])
        pr, act, *bufs = _mm_fwd("ffn_1", xb, gw[("ffn_w1", l)], None, True, _ep_relu2,
                                 outs=[(dff, BF16), (dff, BF16)], riders=riders)
        landed_in(keys, bufs)
        keys, riders = riding(d2d=[] if last else mixer[l + 1], ici=[] if last else [("ffn_w2", l + 1)])
        xf, xb, xhat, rstd, *bufs = _mm_resid_ln("ffn_2", act, gw[("ffn_w2", l)], xf, ffn_g3, ffn_b3, l, riders)
        landed_in(keys, bufs)
        sv.update(pr=pr, act=act, xhat2=xhat, rstd2=rstd)
        saved.append(sv)

    dx, sq = _loss_head(xf, target)
    loss = lax.psum(0.5 * sq[0, 0] / d, ("x", "y", "c"))

    gbuf = {n: lax.empty((N_CHIPS,) + given[n].shape, F32) for n in BIG}
    small = {}
    d_mix_g, d_mix_b, d_ffn_g, d_ffn_b = [None] * DEPTH, [None] * DEPTH, [None] * DEPTH, [None] * DEPTH
    d_ln_g, d_ln_b, d_ws, d_bs = [None] * N_A, [None] * N_A, [None] * N_A, [None] * N_A
    dk = dv = None
    for l in reversed(range(DEPTH)):
        sv = saved[l]
        dr, drb, d_ffn_g[l], d_ffn_b[l] = _ln_bwd(dx, sv["xhat2"], sv["rstd2"], ffn_g3, l)
        dff = sv["pr"].shape[1]
        dhd = _mm_bwd_act("ffn_2_dx", drb, gw[("ffn_w2", l)], None, False, _ep_relu2_bwd, (sv["pr"],),
                          (pl.BlockSpec((TM, dff // N_CHIPS), lambda j, i, k: (i, j)),), out_dtype=BF16)
        gbuf["ffn_w2"] = _mm_bwd_w("ffn_2_dw", sv["act"], drb, gbuf["ffn_w2"], l, False)
        dx = _mm_bwd_act("ffn_1_dx", dhd, gw[("ffn_w1", l)], None, True, _ep_resid, (dr,), (_row_spec(d),))
        gbuf["ffn_w1"] = _mm_bwd_w("ffn_1_dw", sv["x_mid"], dhd, gbuf["ffn_w1"], l, True)

        dr, drb, d_mix_g[l], d_mix_b[l] = _ln_bwd(dx, sv["xhat1"], sv["rstd1"], mix_g3, l)
        quarter = pl.BlockSpec((TM, d // N_CHIPS), lambda j, i, k: (i, j))
        if l < N_A:
            dgated = _mm_bwd_act("a_out_dx", drb, gw[("a_w_out", l)], None, False)
            gbuf["a_w_out"] = _mm_bwd_w("a_out_dw", sv["gated"], drb, gbuf["a_w_out"], l, False)
            du, dvn, d_ws[l], dbs_wide = _gate_bwd(dgated, sv["h"], sv["vn"], a_w_s[l], bst[l])
            d_bs[l] = dbs_wide[:, :, 0]
            dh, dlg, dlb = _gmlp_in_bwd(sv["h"], du, dvn, a_ln_g3, l)
            d_ln_g[l], d_ln_b[l] = dlg[0], dlb[0]
            dx = _mm_bwd_act("a_in_dx", dh, gw[("a_w_in", l)], None, True, _ep_resid, (dr,), (_row_spec(d),))
            gbuf["a_w_in"] = _mm_bwd_w("a_in_dw", sv["x_in"], dh, gbuf["a_w_in"], l, True)
        else:
            j = l - N_A
            do = _mm_bwd_act("b_out_dx", drb, gw[("b_w_o", j)], None, False)
            gbuf["b_w_o"] = _mm_bwd_w("b_out_dw", sv["ob"], drb, gbuf["b_w_o"], j, False)
            dq, dk, dv = _attn_bwd(sv["q"], kb, vb, do, sv["lsum"], dk, dv)
            dx = _mm_bwd_act("b_q_dx", dq, gw[("b_w_q", j)], None, False, _ep_resid, (dr,), (quarter,))
            gbuf["b_w_q"] = _mm_bwd_w("b_q_dw", sv["x_in"], dq, gbuf["b_w_q"], j, False)
            if l == N_A:
                dx = _mm_bwd_act("sb_k_dx", dk, gw[("sb_w_k", None)], None, False, _ep_add, (dx,), (quarter,))
                gbuf["sb_w_k"] = _mm_bwd_w("sb_k_dw", sv["x_in"], dk, gbuf["sb_w_k"], None, False)
                dx = _mm_bwd_act("sb_v_dx", dv, gw[("sb_w_v", None)], None, False, _ep_add, (dx,), (quarter,))
                gbuf["sb_w_v"] = _mm_bwd_w("sb_v_dw", sv["x_in"], dv, gbuf["sb_w_v"], None, False)
    grad_x = dx.reshape(x.shape)

    flat = [gbuf[n].reshape(N_CHIPS, -1, gbuf[n].shape[-1]) for n in BIG]
    got = _pair_exchange(flat)
    sums = [_pair_sum(g, r, core_arr) for g, r in zip(flat, got)]
    landed = _chip_exchange([pb for _, pb in sums])
    place_arr = jnp.stack([chip, cc.astype(jnp.int32)])
    halves = [_chip_sum(p, r, place_arr) for (p, _), r in zip(sums, landed)]
    grads = dict(zip(BIG, [g.reshape(given[n].shape) for n, g in zip(BIG, _half_swap(halves))]))

    small_full = dict(a_ln_g=jnp.stack(d_ln_g), a_ln_b=jnp.stack(d_ln_b), a_w_s=jnp.stack(d_ws), a_b_s=jnp.stack(d_bs),
                      mix_ln_g=jnp.concatenate(d_mix_g), mix_ln_b=jnp.concatenate(d_mix_b),
                      ffn_ln_g=jnp.concatenate(d_ffn_g), ffn_ln_b=jnp.concatenate(d_ffn_b))
    packed = jnp.concatenate([small_full[n].reshape(-1) for n in SMALL])
    total = packed.shape[0]
    ncol = -(-total // (N_DEV * LANES)) * LANES
    packed = jnp.pad(packed, (0, N_DEV * ncol - total)).reshape(N_DEV, ncol)
    reduced = _all_reduce_small(packed).reshape(-1)
    off = 0
    for n in SMALL:
        size = small_full[n].size
        g = reduced[off:off + size].reshape(small_full[n].shape)
        off += size
        if n in ("a_ln_g", "a_ln_b"):
            wq = given[n].shape[1]
            g = lax.dynamic_slice_in_dim(g, chip * wq, wq, axis=1)
        grads[n] = g

    delta, new_m, new_v = {}, {}, {}
    for n in names:
        shape = given[n].shape
        dl, nm, nv = _adamw(as2d(given[n]), as2d(grads[n]), as2d(mom[n]), as2d(var[n]))
        delta[n], new_m[n], new_v[n] = dl.reshape(shape), nm.reshape(shape), nv.reshape(shape)

    order = ("a_w_in", "a_ln_g", "a_ln_b", "a_w_s", "a_b_s", "a_w_out", "sb_w_k", "sb_w_v", "b_w_q", "b_w_o",
             "mix_ln_g", "mix_ln_b", "ffn_ln_g", "ffn_ln_b", "ffn_w1", "ffn_w2")
    return (loss, grad_x, *[grads[n] for n in order], *[delta[n] for n in order],
            *[new_m[n] for n in order], *[new_v[n] for n in order])
```

```python
import math

import jax
import jax.numpy as jnp
from jax import lax
from jax.experimental import pallas as pl
from jax.experimental.pallas import tpu as pltpu

F32 = jnp.float32
BF16 = jnp.bfloat16
MESH = pl.DeviceIdType.MESH

N_CHIPS = 4
DEPTH = 4
N_A = 2
ALPHA = float((2 * DEPTH) ** 0.25)
LN_EPS = 1e-5
CHUNK = 64
GMLP_BLOCK = 128
GMLP_GROUPS = 8
HEAD_DIM = 64
LANES = 128
ATT_T = 256
ADAM_LR = 0.001
ADAM_B1 = 0.9
ADAM_B2 = 0.999
ADAM_EPS = 1e-08
ADAM_WD = 0.01
ADAM_STEP = 10
VMEM_LIMIT = 56 * 1024 * 1024
TM = 512
TS = 1024

NN = ((1,), (0,))
NT = ((1,), (1,))
TN = ((0,), (0,))


def _params(sem):
    return pltpu.CompilerParams(dimension_semantics=sem, vmem_limit_bytes=VMEM_LIMIT)


def _dot(a, b, contract):
    return lax.dot_general(a, b, (contract, ((), ())), preferred_element_type=F32)


def _rider_out(kind, arr):
    shape = (arr.shape[0], arr.shape[1] // 2, arr.shape[2]) if kind == "pair" else arr.shape
    return jax.ShapeDtypeStruct(shape, arr.dtype)


def _rider_copies(kind, src, dst, send_sems, recv_sems, base):
    x, y, c, chips = _place()
    me = 2 * x + y
    sibling = (x, y, 1 - c)

    def copy(k, part, land, to):
        return pltpu.make_async_remote_copy(src_ref=part, dst_ref=land, send_sem=send_sems.at[base + k],
                                            recv_sem=recv_sems.at[base + k], device_id=to, device_id_type=MESH)

    if kind == "pair":
        h = src.shape[1] // 2
        cp = copy(0, src.at[:, pl.ds((1 - c) * h, h)], dst, sibling)
        return [cp], [cp]
    h = dst.shape[1] // 2
    starts, arrivals = [], []
    for k, chip in enumerate(chips):
        blk = 2 * chip[0] + chip[1]
        if kind == "ici":
            starts.append(copy(k, dst.at[me, pl.ds(c * h, h)], dst.at[me, pl.ds(c * h, h)], (*chip, c)))
            arrivals.append(copy(k, dst.at[blk, pl.ds(c * h, h)], dst.at[blk, pl.ds(c * h, h)], (*chip, c)))
        elif kind == "d2d":
            starts.append(copy(k, dst.at[blk, pl.ds(c * h, h)], dst.at[blk, pl.ds(c * h, h)], sibling))
            arrivals.append(copy(k, dst.at[blk, pl.ds((1 - c) * h, h)], dst.at[blk, pl.ds((1 - c) * h, h)], sibling))
        else:
            starts.append(copy(k, src.at[blk], dst.at[me], (*chip, c)))
            arrivals.append(copy(k, src.at[blk], dst.at[blk], (*chip, c)))
    return starts, arrivals


RIDER_SEMS = 3


def _matmul(name, operands, in_specs, out_shapes, out_specs, grid, contract, epilogue, acc_shape, aliases=None,
            chunks=None, riders=()):
    nk = grid[2]
    n_in, n_out, nr = len(operands), len(out_shapes), len(riders)
    n_plain = n_in + nr + n_out

    def body(*refs):
        ins, outs = refs[:n_in], refs[n_in + nr:n_plain]
        if nr:
            srcs, dsts = refs[n_in:n_in + nr], refs[n_plain:n_plain + nr]
            send_sems, recv_sems = refs[-2:]
            pid = [pl.program_id(ax) for ax in range(3)]
            first = (pid[0] == 0) & (pid[1] == 0) & (pid[2] == 0)
            last = (pid[0] == grid[0] - 1) & (pid[1] == grid[1] - 1) & (pid[2] == grid[2] - 1)

            def copies(n):
                return _rider_copies(riders[n][0], srcs[n], dsts[n], send_sems, recv_sems, RIDER_SEMS * n)

            @pl.when(first)
            def _():
                for n in range(nr):
                    for cp in copies(n)[0]:
                        cp.start()

        compute(refs, ins, outs)
        if nr:
            @pl.when(last)
            def _():
                for n in range(nr):
                    starts, arrivals = copies(n)
                    for cp in arrivals:
                        cp.wait_recv()
                    for cp in starts:
                        cp.wait_send()

    def compute(refs, ins, outs):
        if chunks is None:
            p = _dot(ins[0][...].astype(BF16), ins[1][...].astype(BF16), contract)
        else:
            width = ins[0].shape[1] // chunks
            p = None
            for j in range(chunks):
                pj = _dot(ins[0][:, j * width:(j + 1) * width].astype(BF16), ins[1][j].astype(BF16), contract)
                p = pj if p is None else p + pj
        if nk == 1:
            epilogue(p, ins[2:], outs)
            return
        acc = refs[n_plain + nr]
        k = pl.program_id(2)

        @pl.when(k == 0)
        def _():
            acc[...] = p

        @pl.when((k > 0) & (k < nk - 1))
        def _():
            acc[...] += p

        @pl.when(k == nk - 1)
        def _():
            epilogue(acc[...] + p, ins[2:], outs)

    rbufs = [b for _, b in riders]
    in_place = {n_in + n: n_out + n for n, (kind, _) in enumerate(riders) if kind in ("ici", "d2d")}
    scratch = ([] if nk == 1 else [pltpu.VMEM(acc_shape, F32)]) \
        + [pltpu.SemaphoreType.DMA((RIDER_SEMS * nr,))] * (2 if nr else 0)
    return pl.pallas_call(
        body, name=name, grid=grid, in_specs=list(in_specs) + _any_specs(nr), out_specs=list(out_specs) + _any_specs(nr),
        out_shape=list(out_shapes) + [_rider_out(kind, b) for kind, b in riders],
        scratch_shapes=scratch,
        input_output_aliases={**(aliases or {}), **in_place},
        compiler_params=_params(("arbitrary",) * 3 if nr else ("parallel", "parallel", "arbitrary")),
    )(*operands, *rbufs)


def _wspec(w, layer, whole=False):
    r, c = w.shape[-2:]
    lead = N_CHIPS if whole else None
    if w.ndim == 4:
        return pl.BlockSpec((lead, None, r, c), lambda j, i, k: (0 if whole else j, layer, 0, 0))
    return pl.BlockSpec((lead, r, c), lambda j, i, k: (0 if whole else j, 0, 0))


def _ep_store(p, ins, outs):
    for o in outs:
        o[...] = p.astype(o.dtype)


def _mm_fwd(name, a, w, layer, col_sharded, epilogue=_ep_store, extras=(), extra_specs=(), outs=None, riders=()):
    s = a.shape[0]
    r, c = w.shape[-2:]
    if col_sharded:
        grid = (N_CHIPS, s // TM, 1)
        a_spec = pl.BlockSpec((TM, r), lambda j, i, k: (i, 0))
        n_out = N_CHIPS * c
    else:
        grid = (1, s // TM, 1)
        a_spec = pl.BlockSpec((TM, N_CHIPS * r), lambda j, i, k: (i, 0))
        n_out = c
    if outs is None:
        outs = [(n_out, F32)]
    out_shapes = [jax.ShapeDtypeStruct((s, n), dt) for n, dt in outs]
    out_specs = [pl.BlockSpec((TM, c if n == n_out else n), lambda j, i, k: (i, j)) for n, _ in outs]
    return _matmul(name, (a, w) + tuple(extras), [a_spec, _wspec(w, layer, not col_sharded)] + list(extra_specs),
                   out_shapes, out_specs, grid, NN, epilogue, (TM, c), chunks=None if col_sharded else N_CHIPS,
                   riders=riders)


def _mm_bwd_act(name, dy, w, layer, col_sharded, epilogue=_ep_store, extras=(), extra_specs=(), out_dtype=F32,
                riders=()):
    s = dy.shape[0]
    r, c = w.shape[-2:]
    if col_sharded:
        grid = (1, s // TM, 1)
        a_spec = pl.BlockSpec((TM, N_CHIPS * c), lambda j, i, k: (i, 0))
        n_out = r
    else:
        grid = (N_CHIPS, s // TM, 1)
        a_spec = pl.BlockSpec((TM, c), lambda j, i, k: (i, 0))
        n_out = N_CHIPS * r
    o_spec = pl.BlockSpec((TM, r), lambda j, i, k: (i, j))
    return _matmul(name, (dy, w) + tuple(extras), [a_spec, _wspec(w, layer, col_sharded)] + list(extra_specs),
                   [jax.ShapeDtypeStruct((s, n_out), out_dtype)], [o_spec], grid, NT, epilogue, (TM, r),
                   chunks=N_CHIPS if col_sharded else None, riders=riders)


def _mm_bwd_w(name, a, dy, w, col_sharded, riders=()):
    s = a.shape[0]
    r, c = w.shape[-2:]
    ts = min(TS, s)
    grid = (N_CHIPS, 1, s // ts)
    if col_sharded:
        a_spec = pl.BlockSpec((ts, r), lambda j, i, k: (k, 0))
        b_spec = pl.BlockSpec((ts, c), lambda j, i, k: (k, j))
    else:
        a_spec = pl.BlockSpec((ts, r), lambda j, i, k: (k, j))
        b_spec = pl.BlockSpec((ts, c), lambda j, i, k: (k, 0))

    def epilogue(p, ins, outs):
        outs[0][...] = p

    return _matmul(name, (a, dy), [a_spec, b_spec], [jax.ShapeDtypeStruct(w.shape, F32)], [_wspec(w, None)], grid, TN,
                   epilogue, (r, c), riders=riders)


def _row_spec(n):
    return pl.BlockSpec((TM, n), lambda j, i, k: (i, 0))


def _vec_spec(layer, n):
    return pl.BlockSpec((None, 1, n), lambda j, i, k: (layer, 0, 0))


def _ep_resid_ln(p, ins, outs):
    x_ref, g_ref, b_ref = ins
    xf_ref, xb_ref, xhat_ref, rstd_ref = outs
    r = ALPHA * x_ref[...] + p
    mu = jnp.mean(r, axis=-1, keepdims=True)
    d = r - mu
    var = jnp.mean(d * d, axis=-1, keepdims=True)
    rstd = lax.rsqrt(var + LN_EPS)
    xhat = d * rstd
    y = xhat * g_ref[...] + b_ref[...]
    xf_ref[...] = y
    xb_ref[...] = y.astype(BF16)
    xhat_ref[...] = xhat
    rstd_ref[...] = rstd


def _mm_resid_ln(name, a, w, x, g3, b3, ln_layer, riders=()):
    d = x.shape[1]
    return _mm_fwd(name, a, w, None, False, _ep_resid_ln, (x, g3, b3),
                   (_row_spec(d), _vec_spec(ln_layer, d), _vec_spec(ln_layer, d)),
                   outs=[(d, F32), (d, BF16), (d, F32), (1, F32)], riders=riders)


def _ep_relu2(p, ins, outs):
    h = jnp.maximum(p, 0.0)
    outs[0][...] = h.astype(BF16)
    outs[1][...] = (h * h).astype(BF16)


def _ep_scale_q(p, ins, outs):
    outs[0][...] = (p * (HEAD_DIM ** -0.5)).astype(BF16)


def _ep_bf16(p, ins, outs):
    outs[0][...] = p.astype(BF16)


def _ep_relu2_bwd(p, ins, outs):
    outs[0][...] = (p * (2.0 * ins[0][...].astype(F32))).astype(BF16)


def _ep_resid(p, ins, outs):
    outs[0][...] = ALPHA * ins[0][...] + p


def _ep_add(p, ins, outs):
    outs[0][...] = ins[0][...] + p


def _gelu_grad(x):
    c0 = math.sqrt(2.0 / math.pi)
    t = jnp.tanh(c0 * (x + 0.044715 * (x * x * x)))
    return 0.5 * (1.0 + t) + (0.5 * x) * (1.0 - t * t) * (c0 * (1.0 + 3.0 * 0.044715 * (x * x)))


def _cast_bf16(w2d):
    r, c = w2d.shape
    tr = min(r, 512)

    def body(w_ref, o_ref):
        o_ref[...] = w_ref[...].astype(BF16)

    return pl.pallas_call(
        body, name="cast_bf16", grid=(r // tr,),
        in_specs=[pl.BlockSpec((tr, c), lambda i: (i, 0))], out_specs=pl.BlockSpec((tr, c), lambda i: (i, 0)),
        out_shape=jax.ShapeDtypeStruct((r, c), BF16), compiler_params=_params(("parallel",)),
    )(w2d)


def _cast_into_slot(w, layer, chip):
    r, c = w.shape[-2:]
    tr = min(r, 512)

    def body(chip_ref, w_ref, o_ref):
        o_ref[...] = w_ref[...].astype(BF16)

    if layer is None:
        w_spec = pl.BlockSpec((tr, c), lambda i, chip_ref: (i, 0))
    else:
        w_spec = pl.BlockSpec((None, tr, c), lambda i, chip_ref: (layer, i, 0))
    grid_spec = pltpu.PrefetchScalarGridSpec(
        num_scalar_prefetch=1, grid=(r // tr,), in_specs=[w_spec],
        out_specs=pl.BlockSpec((None, tr, c), lambda i, chip_ref: (chip_ref[0], i, 0)))
    return pl.pallas_call(
        body, name="cast_into_slot", grid_spec=grid_spec,
        out_shape=jax.ShapeDtypeStruct((N_CHIPS, r, c), BF16), compiler_params=_params(("parallel",)),
    )(chip, w)


def _gmlp_norm_fwd(h, g3, b3, layer):
    s, w2 = h.shape
    w = w2 // 2

    def body(h_ref, g_ref, b_ref, o_ref):
        z = jax.nn.gelu(h_ref[...])
        mu = jnp.mean(z, axis=-1, keepdims=True)
        d = z - mu
        var = jnp.mean(d * d, axis=-1, keepdims=True)
        o_ref[...] = (d * lax.rsqrt(var + LN_EPS) * g_ref[...] + b_ref[...]).astype(BF16)

    vec = pl.BlockSpec((None, 1, w), lambda i: (layer, 0, 0))
    return pl.pallas_call(
        body, name="gmlp_norm_fwd", grid=(s // TM,),
        in_specs=[pl.BlockSpec((TM, w), lambda i: (i, 1)), vec, vec],
        out_specs=pl.BlockSpec((TM, w), lambda i: (i, 0)),
        out_shape=jax.ShapeDtypeStruct((s, w), BF16), compiler_params=_params(("parallel",)),
    )(h, g3, b3)


def _chunk_mask():
    t = lax.broadcasted_iota(jnp.int32, (GMLP_BLOCK, GMLP_BLOCK), 0)
    s = lax.broadcasted_iota(jnp.int32, (GMLP_BLOCK, GMLP_BLOCK), 1)
    return (s // CHUNK) <= (t // CHUNK)


SG_ROWS = 512


def _gate_fwd(h, vn, ws, bst):
    s, w = vn.shape
    gd = w // GMLP_GROUPS

    def body(h_ref, v_ref, ws_ref, bs_ref, o_ref):
        mask = _chunk_mask()
        for g in range(GMLP_GROUPS):
            wm = jnp.where(mask, ws_ref[g], 0.0).astype(BF16)
            bias = bs_ref[:, g:g + 1]
            cols = slice(g * gd, (g + 1) * gd)
            for n in range(SG_ROWS // GMLP_BLOCK):
                rows = slice(n * GMLP_BLOCK, (n + 1) * GMLP_BLOCK)
                sp = _dot(wm, v_ref[rows, cols], NN) + bias
                o_ref[rows, cols] = (jax.nn.gelu(h_ref[rows, cols]) * sp).astype(BF16)

    return pl.pallas_call(
        body, name="gate_fwd", grid=(s // SG_ROWS,),
        in_specs=[pl.BlockSpec((SG_ROWS, w), lambda i: (i, 0)), pl.BlockSpec((SG_ROWS, w), lambda i: (i, 0)),
                  pl.BlockSpec(ws.shape, lambda i: (0, 0, 0)), pl.BlockSpec(bst.shape, lambda i: (0, 0))],
        out_specs=pl.BlockSpec((SG_ROWS, w), lambda i: (i, 0)),
        out_shape=jax.ShapeDtypeStruct((s, w), BF16), compiler_params=_params(("parallel",)),
    )(h, vn, ws, bst)


def _gate_bwd(dgated, h, vn, ws, bst):
    s, w = vn.shape
    gd = w // GMLP_GROUPS
    nsteps = s // SG_ROWS

    def body(dg_ref, h_ref, v_ref, ws_ref, bs_ref, du_ref, dv_ref, dws_ref, dbs_ref, dsum):
        i = pl.program_id(0)

        @pl.when(i == 0)
        def _():
            dws_ref[...] = jnp.zeros_like(dws_ref)
            dsum[...] = jnp.zeros_like(dsum)

        mask = _chunk_mask()
        for g in range(GMLP_GROUPS):
            wm = jnp.where(mask, ws_ref[g], 0.0).astype(BF16)
            bias = bs_ref[:, g:g + 1]
            cols = slice(g * gd, (g + 1) * gd)
            dw = jnp.zeros((GMLP_BLOCK, GMLP_BLOCK), F32)
            dsg = jnp.zeros((GMLP_BLOCK, gd), F32)
            for n in range(SG_ROWS // GMLP_BLOCK):
                rows = slice(n * GMLP_BLOCK, (n + 1) * GMLP_BLOCK)
                vb = v_ref[rows, cols]
                sp = _dot(wm, vb, NN) + bias
                dg = dg_ref[rows, cols]
                du_ref[rows, cols] = dg * sp
                ds = dg * jax.nn.gelu(h_ref[rows, cols])
                dsb = ds.astype(BF16)
                dw += _dot(dsb, vb, NT)
                dsg += ds
                dv_ref[rows, cols] = _dot(wm, dsb, TN)
            dws_ref[g] += dw
            dsum[:, cols] += dsg

        @pl.when(i == nsteps - 1)
        def _():
            for g in range(GMLP_GROUPS):
                dws_ref[g] = jnp.where(mask, dws_ref[g], 0.0)
                tot = jnp.sum(dsum[:, g * gd:(g + 1) * gd], axis=-1, keepdims=True)
                dbs_ref[g] = jnp.broadcast_to(tot, (GMLP_BLOCK, LANES))

    tile = pl.BlockSpec((SG_ROWS, w), lambda i: (i, 0))
    return pl.pallas_call(
        body, name="gate_bwd", grid=(nsteps,),
        in_specs=[tile, tile, tile, pl.BlockSpec(ws.shape, lambda i: (0, 0, 0)), pl.BlockSpec(bst.shape, lambda i: (0, 0))],
        out_specs=[tile, tile, pl.BlockSpec(ws.shape, lambda i: (0, 0, 0)),
                   pl.BlockSpec((GMLP_GROUPS, GMLP_BLOCK, LANES), lambda i: (0, 0, 0))],
        out_shape=[jax.ShapeDtypeStruct((s, w), F32), jax.ShapeDtypeStruct((s, w), F32),
                   jax.ShapeDtypeStruct(ws.shape, F32), jax.ShapeDtypeStruct((GMLP_GROUPS, GMLP_BLOCK, LANES), F32)],
        scratch_shapes=[pltpu.VMEM((GMLP_BLOCK, w), F32)],
        compiler_params=_params(("arbitrary",)),
    )(dgated, h, vn, ws, bst)


GB_ROWS = 256


def _gmlp_in_bwd(h, du, dvn, g3, layer):
    s, w2 = h.shape
    w = w2 // 2
    nsteps = s // GB_ROWS

    def body(h_ref, du_ref, dv_ref, g_ref, dh_ref, dg_ref, db_ref):
        i = pl.program_id(0)

        @pl.when(i == 0)
        def _():
            dg_ref[...] = jnp.zeros_like(dg_ref)
            db_ref[...] = jnp.zeros_like(db_ref)

        hu = h_ref[:, :w]
        hv = h_ref[:, w:]
        dh_ref[:, :w] = (du_ref[...] * _gelu_grad(hu)).astype(BF16)
        z = jax.nn.gelu(hv)
        mu = jnp.mean(z, axis=-1, keepdims=True)
        d = z - mu
        var = jnp.mean(d * d, axis=-1, keepdims=True)
        rstd = lax.rsqrt(var + LN_EPS)
        xhat = d * rstd
        dy = dv_ref[...]
        db_ref[...] += jnp.sum(dy, axis=0, keepdims=True)
        dg_ref[...] += jnp.sum(dy * xhat, axis=0, keepdims=True)
        dxh = dy * g_ref[...]
        m1 = jnp.mean(dxh, axis=-1, keepdims=True)
        m2 = jnp.mean(dxh * xhat, axis=-1, keepdims=True)
        dz = rstd * (dxh - m1 - xhat * m2)
        dh_ref[:, w:] = (dz * _gelu_grad(hv)).astype(BF16)

    half = pl.BlockSpec((GB_ROWS, w), lambda i: (i, 0))
    vec = pl.BlockSpec((1, w), lambda i: (0, 0))
    return pl.pallas_call(
        body, name="gmlp_in_bwd", grid=(nsteps,),
        in_specs=[pl.BlockSpec((GB_ROWS, w2), lambda i: (i, 0)), half, half,
                  pl.BlockSpec((None, 1, w), lambda i: (layer, 0, 0))],
        out_specs=[pl.BlockSpec((GB_ROWS, w2), lambda i: (i, 0)), vec, vec],
        out_shape=[jax.ShapeDtypeStruct((s, w2), BF16), jax.ShapeDtypeStruct((1, w), F32), jax.ShapeDtypeStruct((1, w), F32)],
        compiler_params=_params(("arbitrary",)),
    )(h, du, dvn, g3)


def _ln_bwd(dy, xhat, rstd, g3, layer):
    s, d = dy.shape
    nsteps = s // TM

    def body(dy_ref, xh_ref, rs_ref, g_ref, dr_ref, drb_ref, dg_ref, db_ref):
        i = pl.program_id(0)

        @pl.when(i == 0)
        def _():
            dg_ref[...] = jnp.zeros_like(dg_ref)
            db_ref[...] = jnp.zeros_like(db_ref)

        dyv = dy_ref[...]
        xhat_v = xh_ref[...]
        db_ref[...] += jnp.sum(dyv, axis=0, keepdims=True)
        dg_ref[...] += jnp.sum(dyv * xhat_v, axis=0, keepdims=True)
        dxh = dyv * g_ref[...]
        m1 = jnp.mean(dxh, axis=-1, keepdims=True)
        m2 = jnp.mean(dxh * xhat_v, axis=-1, keepdims=True)
        dr = rs_ref[...] * (dxh - m1 - xhat_v * m2)
        dr_ref[...] = dr
        drb_ref[...] = dr.astype(BF16)

    tile = pl.BlockSpec((TM, d), lambda i: (i, 0))
    vec = pl.BlockSpec((1, d), lambda i: (0, 0))
    return pl.pallas_call(
        body, name="ln_bwd", grid=(nsteps,),
        in_specs=[tile, tile, pl.BlockSpec((TM, 1), lambda i: (i, 0)), pl.BlockSpec((None, 1, d), lambda i: (layer, 0, 0))],
        out_specs=[tile, tile, vec, vec],
        out_shape=[jax.ShapeDtypeStruct((s, d), F32), jax.ShapeDtypeStruct((s, d), BF16),
                   jax.ShapeDtypeStruct((1, d), F32), jax.ShapeDtypeStruct((1, d), F32)],
        compiler_params=_params(("arbitrary",)),
    )(dy, xhat, rstd, g3)


def _loss_head(y, target):
    s, d = y.shape

    def body(y_ref, t_ref, dy_ref, l_ref):
        i = pl.program_id(0)

        @pl.when(i == 0)
        def _():
            l_ref[...] = jnp.zeros_like(l_ref)

        e = y_ref[...] - t_ref[...]
        dy_ref[...] = e * (1.0 / d)
        l_ref[...] += jnp.sum(jnp.sum(e * e, axis=1, keepdims=True), axis=0, keepdims=True)

    tile = pl.BlockSpec((TM, d), lambda i: (i, 0))
    return pl.pallas_call(
        body, name="loss_head", grid=(s // TM,), in_specs=[tile, tile],
        out_specs=[tile, pl.BlockSpec((1, 1), lambda i: (0, 0))],
        out_shape=[jax.ShapeDtypeStruct((s, d), F32), jax.ShapeDtypeStruct((1, 1), F32)],
        compiler_params=_params(("arbitrary",)),
    )(y, target)


LOG2E = 1.4426950408889634
DEAD_LOG2 = -160.0
FIRST_LANE = 1


def _sb_terms(z, causal):
    z2 = z * LOG2E
    e = jnp.exp2(-jnp.abs(z2))
    l1p = jnp.log2(1.0 + e)
    lb = jnp.minimum(z2, 0.0) - l1p
    lr = lb - z2
    if causal is not None:
        lr = jnp.where(causal, lr, 0.0)
    return lb, lr, e


def _split_hi_lo(x):
    hi = x.astype(BF16)
    lo = (x - hi.astype(F32)).astype(BF16)
    return jnp.concatenate([hi, lo], axis=1)


def _att_consts(prefix):
    r = lax.broadcasted_iota(jnp.int32, (2 * ATT_T, ATT_T), 0) % ATT_T
    c = lax.broadcasted_iota(jnp.int32, (2 * ATT_T, ATT_T), 1)
    tri2 = jnp.where((r <= c) if prefix else (r >= c), 1.0, 0.0).astype(BF16)
    r = lax.broadcasted_iota(jnp.int32, (ATT_T, ATT_T), 0)
    c = lax.broadcasted_iota(jnp.int32, (ATT_T, ATT_T), 1)
    causal = c < r
    head_a = lax.broadcasted_iota(jnp.int32, (1, LANES), 1) < HEAD_DIM
    return tri2, causal, head_a


def _attn_fwd(q, k, v):
    s, d = q.shape
    nq = s // ATT_T

    def body(q_ref, k_ref, v_ref, ob_ref, lsum_ref, acc_a, acc_b, rem_a, rem_b):
        i = pl.program_id(1)
        tri, causal, head_a = _att_consts(prefix=False)
        q2 = q_ref[...]
        zero = jnp.zeros_like(q2)
        qa = jnp.where(head_a, q2, zero)
        qb = jnp.where(head_a, zero, q2)
        acc_a[...] = jnp.zeros_like(acc_a)
        acc_b[...] = jnp.zeros_like(acc_b)
        rem_a[...] = jnp.zeros_like(rem_a)
        rem_b[...] = jnp.zeros_like(rem_b)

        def block(kb, mask):
            rows = pl.ds(pl.multiple_of(kb * ATT_T, ATT_T), ATT_T)
            k2 = k_ref[rows, :]
            v2 = v_ref[rows, :]
            heads = ((qa, acc_a, rem_a), (qb, acc_b, rem_b))
            zs = [_dot(qm, k2, NT) for qm, _, _ in heads]
            terms = [_sb_terms(z, mask) for z in zs]
            sums = [_dot(_split_hi_lo(lr), tri, NN) for _, lr, _ in terms]
            for (_, acc, rem), (lb, lr, _), sincl in zip(heads, terms, sums):
                a = jnp.exp2(lb + (sincl - lr) + rem[...])
                if mask is not None:
                    a = jnp.where(mask, a, 0.0)
                rem[...] += sincl[:, 0:1]
                acc[...] += _dot(a.astype(BF16), v2, NN)

        block(i, causal)

        def live():
            return jnp.maximum(jnp.max(rem_a[...]), jnp.max(rem_b[...])) > DEAD_LOG2

        def go_on(carry):
            t, alive = carry
            return (t < i) & alive

        def step(carry):
            t, _ = carry
            block(i - 1 - t, None)
            return t + 1, live()

        done, _ = lax.while_loop(go_on, step, (jnp.int32(0), live()))
        first = (i - done).astype(F32)
        ob_ref[...] = jnp.where(head_a, acc_a[...], acc_b[...]).astype(BF16)
        lane = lax.broadcasted_iota(jnp.int32, (1, LANES), 1)
        lsum_ref[...] = jnp.where(lane == FIRST_LANE, first, jnp.where(head_a, rem_a[...], rem_b[...]))

    qspec = pl.BlockSpec((ATT_T, LANES), lambda p, i: (i, p))
    kspec = pl.BlockSpec((s, LANES), lambda p, i: (0, p))
    return pl.pallas_call(
        body, name="attn_fwd", grid=(d // LANES, nq), in_specs=[qspec, kspec, kspec],
        out_specs=[qspec, qspec],
        out_shape=[jax.ShapeDtypeStruct((s, d), BF16), jax.ShapeDtypeStruct((s, d), F32)],
        scratch_shapes=[pltpu.VMEM((ATT_T, LANES), F32), pltpu.VMEM((ATT_T, LANES), F32),
                        pltpu.VMEM((ATT_T, 1), F32), pltpu.VMEM((ATT_T, 1), F32)],
        compiler_params=_params(("parallel", "arbitrary")),
    )(q, k, v)


def _attn_bwd(q, k, v, do, lsum, dk_prev=None, dv_prev=None):
    s, d = q.shape
    nq = s // ATT_T
    has_prev = dk_prev is not None

    def body(*refs):
        q_ref, k_ref, v_ref, do_ref, ls_ref = refs[:5]
        n_in = 7 if has_prev else 5
        dq_ref, dk_ref, dv_ref, acc_a, acc_b, pre_a, pre_b, gp_a, gp_b = refs[n_in:]
        i = pl.program_id(1)

        @pl.when(i == 0)
        def _():
            if has_prev:
                dk_ref[...] = refs[5][...]
                dv_ref[...] = refs[6][...]
            else:
                dk_ref[...] = jnp.zeros_like(dk_ref)
                dv_ref[...] = jnp.zeros_like(dv_ref)

        tri, causal, head_a = _att_consts(prefix=True)
        q2 = q_ref[...]
        zero = jnp.zeros_like(q2)
        qa = jnp.where(head_a, q2, zero)
        qb = jnp.where(head_a, zero, q2)
        do2 = do_ref[...]
        doa = jnp.where(head_a, do2, 0.0).astype(BF16)
        dob = jnp.where(head_a, 0.0, do2).astype(BF16)
        ls2 = ls_ref[...]
        tot_a = ls2[:, 0:1]
        tot_b = ls2[:, HEAD_DIM:HEAD_DIM + 1]
        for r in (acc_a, acc_b, pre_a, pre_b, gp_a, gp_b):
            r[...] = jnp.zeros_like(r)

        def block(kb, mask):
            rows = pl.ds(pl.multiple_of(kb * ATT_T, ATT_T), ATT_T)
            k2 = k_ref[rows, :]
            v2 = v_ref[rows, :]
            dk_new = jnp.zeros((ATT_T, LANES), F32)
            dv_new = jnp.zeros((ATT_T, LANES), F32)
            heads = ((qa, doa, tot_a, acc_a, pre_a, gp_a), (qb, dob, tot_b, acc_b, pre_b, gp_b))
            zs = [_dot(h[0], k2, NT) for h in heads]
            das = [_dot(h[1], v2, NT) for h in heads]
            terms = [_sb_terms(z, mask) for z in zs]
            psums = [_dot(_split_hi_lo(lr), tri, NN) for _, lr, _ in terms]
            gs, abs_ = [], []
            for (_, _, tot, _, pre, _), (lb, _, _), pincl, da in zip(heads, terms, psums, das):
                a = jnp.exp2(lb + (tot - (pre[...] + pincl)))
                if mask is not None:
                    a = jnp.where(mask, a, 0.0)
                pre[...] += pincl[:, ATT_T - 1:ATT_T]
                gs.append(a * da)
                abs_.append(a.astype(BF16))
            gsums = [_dot(_split_hi_lo(g), tri, NN) for g in gs]
            dzs = []
            for (_, _, _, _, _, gpre), z, (_, _, e), g, gincl in zip(heads, zs, terms, gs, gsums):
                gbefore = gpre[...] + (gincl - g)
                gpre[...] += gincl[:, ATT_T - 1:ATT_T]
                inv = 1.0 / (1.0 + e)
                beta = jnp.where(z >= 0.0, inv, e * inv)
                dz = g - beta * (g + gbefore)
                if mask is not None:
                    dz = jnp.where(mask, dz, 0.0)
                dzs.append(dz.astype(BF16))
            for (qm, dom, _, acc, _, _), ab, dzb in zip(heads, abs_, dzs):
                dv_new += _dot(ab, dom, TN)
                dk_new += _dot(dzb, qm, TN)
                acc[...] += _dot(dzb, k2, NN)
            dk_ref[rows, :] += dk_new
            dv_ref[rows, :] += dv_new

        def step(kb, carry):
            block(kb, None)
            return carry

        first = jnp.clip(jnp.max(ls2[:, FIRST_LANE:FIRST_LANE + 1]).astype(jnp.int32), 0, i)
        lax.fori_loop(first, i, step, 0)
        block(i, causal)
        dq_ref[...] = (jnp.where(head_a, acc_a[...], acc_b[...]) * (HEAD_DIM ** -0.5)).astype(BF16)

    qspec = pl.BlockSpec((ATT_T, LANES), lambda p, i: (i, p))
    kspec = pl.BlockSpec((s, LANES), lambda p, i: (0, p))
    ins = [q, k, v, do, lsum] + ([dk_prev, dv_prev] if has_prev else [])
    return pl.pallas_call(
        body, name="attn_bwd", grid=(d // LANES, nq),
        in_specs=[qspec, kspec, kspec, qspec, qspec] + ([kspec, kspec] if has_prev else []),
        out_specs=[qspec, kspec, kspec],
        out_shape=[jax.ShapeDtypeStruct((s, d), BF16), jax.ShapeDtypeStruct((s, d), F32), jax.ShapeDtypeStruct((s, d), F32)],
        scratch_shapes=[pltpu.VMEM((ATT_T, LANES), F32), pltpu.VMEM((ATT_T, LANES), F32)]
        + [pltpu.VMEM((ATT_T, 1), F32)] * 4,
        compiler_params=_params(("parallel", "arbitrary")),
    )(*ins)


def _place():
    x, y, c = lax.axis_index("x"), lax.axis_index("y"), lax.axis_index("c")
    chips = [(1 - x, y), (x, 1 - y), (1 - x, 1 - y)]
    return x, y, c, chips


def _any_specs(n):
    return [pl.BlockSpec(memory_space=pl.ANY)] * n


def _gather_weights(bufs):
    n = len(bufs)

    def body(*refs):
        outs = refs[n:2 * n]
        send_sems, recv_sems = refs[2 * n:]
        x, y, c, chips = _place()
        me = 2 * x + y
        sibling = (x, y, 1 - c)

        def half(a, blk, hc):
            h = outs[a].shape[1] // 2
            return outs[a].at[blk, pl.ds(hc * h, h)]

        def copy(a, k, part, to):
            return pltpu.make_async_remote_copy(src_ref=part, dst_ref=part, send_sem=send_sems.at[a, k],
                                                recv_sem=recv_sems.at[a, k], device_id=to, device_id_type=MESH)

        sent = []
        for a in range(n):
            for k, chip in enumerate(chips):
                sent.append(copy(a, k, half(a, me, c), (*chip, c)))
                sent[-1].start()
        for a in range(n):
            for k, chip in enumerate(chips):
                blk = 2 * chip[0] + chip[1]
                copy(a, k, half(a, blk, c), sibling).wait_recv()
                sent.append(copy(a, 3 + k, half(a, blk, c), sibling))
                sent[-1].start()
        for a in range(n):
            for k, chip in enumerate(chips):
                blk = 2 * chip[0] + chip[1]
                copy(a, 3 + k, half(a, blk, 1 - c), sibling).wait_recv()
        for cp in sent:
            cp.wait_send()

    return pl.pallas_call(
        body, name="gather_weights", in_specs=_any_specs(n), out_specs=_any_specs(n),
        out_shape=[jax.ShapeDtypeStruct(w.shape, w.dtype) for w in bufs],
        input_output_aliases={a: a for a in range(n)},
        scratch_shapes=[pltpu.SemaphoreType.DMA((n, 6)), pltpu.SemaphoreType.DMA((n, 6))],
        compiler_params=pltpu.CompilerParams(has_side_effects=True),
    )(*bufs)


def _pair_exchange(grads):
    n = len(grads)

    def body(*refs):
        ins, outs = refs[:n], refs[n:2 * n]
        send_sems, recv_sems = refs[2 * n:]
        x, y, c, _ = _place()
        cps = []
        for a in range(n):
            h = ins[a].shape[1] // 2
            cps.append(pltpu.make_async_remote_copy(
                src_ref=ins[a].at[:, pl.ds((1 - c) * h, h)], dst_ref=outs[a], send_sem=send_sems.at[a],
                recv_sem=recv_sems.at[a], device_id=(x, y, 1 - c), device_id_type=MESH))
            cps[-1].start()
        for cp in cps:
            cp.wait()

    return pl.pallas_call(
        body, name="pair_exchange", in_specs=_any_specs(n), out_specs=_any_specs(n),
        out_shape=[jax.ShapeDtypeStruct((g.shape[0], g.shape[1] // 2, g.shape[2]), g.dtype) for g in grads],
        scratch_shapes=[pltpu.SemaphoreType.DMA((n,)), pltpu.SemaphoreType.DMA((n,))],
        compiler_params=pltpu.CompilerParams(has_side_effects=True),
    )(*grads)


def _chip_exchange(parts):
    n = len(parts)

    def body(*refs):
        ins, outs = refs[:n], refs[n:2 * n]
        send_sems, recv_sems = refs[2 * n:]
        x, y, c, chips = _place()
        me = 2 * x + y
        cps = []
        for a in range(n):
            for k, chip in enumerate(chips):
                blk = 2 * chip[0] + chip[1]
                cps.append(pltpu.make_async_remote_copy(
                    src_ref=ins[a].at[blk], dst_ref=outs[a].at[me], send_sem=send_sems.at[a, k],
                    recv_sem=recv_sems.at[a, k], device_id=(*chip, c), device_id_type=MESH))
                cps[-1].start()
        for a in range(n):
            for k, chip in enumerate(chips):
                blk = 2 * chip[0] + chip[1]
                pltpu.make_async_remote_copy(
                    src_ref=ins[a].at[blk], dst_ref=outs[a].at[blk], send_sem=send_sems.at[a, k],
                    recv_sem=recv_sems.at[a, k], device_id=(*chip, c), device_id_type=MESH).wait_recv()
        for cp in cps:
            cp.wait_send()

    return pl.pallas_call(
        body, name="chip_exchange", in_specs=_any_specs(n), out_specs=_any_specs(n),
        out_shape=[jax.ShapeDtypeStruct(p.shape, p.dtype) for p in parts],
        scratch_shapes=[pltpu.SemaphoreType.DMA((n, 3)), pltpu.SemaphoreType.DMA((n, 3))],
        compiler_params=pltpu.CompilerParams(has_side_effects=True),
    )(*parts)


def _half_swap(halves):
    n = len(halves)

    def body(*refs):
        outs = refs[n:2 * n]
        send_sems, recv_sems = refs[2 * n:]
        x, y, c, _ = _place()
        cps = []
        for a in range(n):
            h = outs[a].shape[1] // 2
            mine = outs[a].at[:, pl.ds(c * h, h)]
            cps.append(pltpu.make_async_remote_copy(
                src_ref=mine, dst_ref=mine, send_sem=send_sems.at[a], recv_sem=recv_sems.at[a],
                device_id=(x, y, 1 - c), device_id_type=MESH))
            cps[-1].start()
        for cp in cps:
            cp.wait()

    return pl.pallas_call(
        body, name="half_swap", in_specs=_any_specs(n), out_specs=_any_specs(n),
        out_shape=[jax.ShapeDtypeStruct(p.shape, p.dtype) for p in halves],
        input_output_aliases={a: a for a in range(n)},
        scratch_shapes=[pltpu.SemaphoreType.DMA((n,)), pltpu.SemaphoreType.DMA((n,))],
        compiler_params=pltpu.CompilerParams(has_side_effects=True),
    )(*halves)


N_DEV = 8


def _all_reduce_small(v):
    nrow, ncol = v.shape

    def body(v_ref, o_ref, land, red, send_sems, recv_sems, send2, recv2, loc_sem):
        x, y, c, _ = _place()
        me = 4 * x + 2 * y + c
        peers = []
        for k in range(1, N_DEV):
            peers.append((x ^ ((k >> 2) & 1), y ^ ((k >> 1) & 1), c ^ (k & 1)))
        own = pltpu.make_async_copy(v_ref.at[pl.ds(me, 1)], land.at[pl.ds(me, 1)], loc_sem)
        own.start()
        cps = []
        for k, peer in enumerate(peers):
            dev = 4 * peer[0] + 2 * peer[1] + peer[2]
            cps.append(pltpu.make_async_remote_copy(
                src_ref=v_ref.at[pl.ds(dev, 1)], dst_ref=land.at[pl.ds(me, 1)], send_sem=send_sems.at[k],
                recv_sem=recv_sems.at[k], device_id=peer, device_id_type=MESH))
            cps[-1].start()
        for k, peer in enumerate(peers):
            dev = 4 * peer[0] + 2 * peer[1] + peer[2]
            pltpu.make_async_remote_copy(
                src_ref=v_ref.at[pl.ds(dev, 1)], dst_ref=land.at[pl.ds(dev, 1)], send_sem=send_sems.at[k],
                recv_sem=recv_sems.at[k], device_id=peer, device_id_type=MESH).wait_recv()
        for cp in cps:
            cp.wait_send()
        own.wait()
        terms = land[...]
        total = terms[0:1, :]
        for d in range(1, N_DEV):
            total = total + terms[d:d + 1, :]
        red[...] = total
        own = pltpu.make_async_copy(red, o_ref.at[pl.ds(me, 1)], loc_sem)
        own.start()
        cps = []
        for k, peer in enumerate(peers):
            cps.append(pltpu.make_async_remote_copy(
                src_ref=red, dst_ref=o_ref.at[pl.ds(me, 1)], send_sem=send2.at[k],
                recv_sem=recv2.at[k], device_id=peer, device_id_type=MESH))
            cps[-1].start()
        for k, peer in enumerate(peers):
            dev = 4 * peer[0] + 2 * peer[1] + peer[2]
            pltpu.make_async_remote_copy(
                src_ref=red, dst_ref=o_ref.at[pl.ds(dev, 1)], send_sem=send2.at[k],
                recv_sem=recv2.at[k], device_id=peer, device_id_type=MESH).wait_recv()
        for cp in cps:
            cp.wait_send()
        own.wait()

    vm = pl.BlockSpec(memory_space=pltpu.VMEM)
    return pl.pallas_call(
        body, name="all_reduce_small", in_specs=[vm], out_specs=vm,
        out_shape=jax.ShapeDtypeStruct((nrow, ncol), F32),
        scratch_shapes=[pltpu.VMEM((nrow, ncol), F32), pltpu.VMEM((1, ncol), F32)]
        + [pltpu.SemaphoreType.DMA((N_DEV - 1,))] * 4 + [pltpu.SemaphoreType.DMA],
        compiler_params=pltpu.CompilerParams(has_side_effects=True, vmem_limit_bytes=VMEM_LIMIT),
    )(v)


def _row_tile(rows):
    return min(rows, 512)


def _pair_sum(g, got, core):
    nb, r, c = g.shape
    h = r // 2
    tr = _row_tile(h)
    nt = h // tr

    def body(core_ref, g_ref, got_ref, p_ref, pb_ref):
        p = g_ref[...] + got_ref[...]
        p_ref[...] = p
        pb_ref[...] = p.astype(BF16)

    spec = pl.BlockSpec((None, tr, c), lambda j, t, core_ref: (j, t, 0))
    grid_spec = pltpu.PrefetchScalarGridSpec(
        num_scalar_prefetch=1, grid=(nb, nt),
        in_specs=[pl.BlockSpec((None, tr, c), lambda j, t, core_ref: (j, core_ref[0] * nt + t, 0)), spec],
        out_specs=[spec, spec])
    return pl.pallas_call(
        body, name="pair_sum", grid_spec=grid_spec,
        out_shape=[jax.ShapeDtypeStruct((nb, h, c), F32), jax.ShapeDtypeStruct((nb, h, c), BF16)],
        compiler_params=_params(("parallel", "parallel")),
    )(core, g, got)


def _chip_sum(p, got, place, out, layer):
    nb, h, c = p.shape
    tr = _row_tile(h)
    nt = h // tr

    def body(place_ref, p_ref, g1_ref, g2_ref, g3_ref, old_ref, o_ref):
        o_ref[...] = ((p_ref[...] + g1_ref[...].astype(F32)) + g2_ref[...].astype(F32)) + g3_ref[...].astype(F32)

    def blk(off):
        return pl.BlockSpec((None, tr, c), lambda t, place_ref: ((place_ref[0] + off) % N_CHIPS, t, 0))

    grid_spec = pltpu.PrefetchScalarGridSpec(
        num_scalar_prefetch=1, grid=(nt,),
        in_specs=[blk(0), blk(1), blk(2), blk(3), pl.BlockSpec(memory_space=pl.ANY)],
        out_specs=pl.BlockSpec((None, tr, c), lambda t, place_ref: (layer, place_ref[1] * nt + t, 0)))
    return pl.pallas_call(
        body, name="chip_sum", grid_spec=grid_spec, out_shape=jax.ShapeDtypeStruct(out.shape, F32),
        input_output_aliases={5: 0}, compiler_params=_params(("parallel",)),
    )(place, p, got, got, got, out)


def _adamw(w, g, m, v):
    r, c = w.shape
    tr = r if r < 8 else _row_tile(r)

    def body(w_ref, g_ref, m_ref, v_ref, d_ref, nm_ref, nv_ref):
        gv = g_ref[...]
        nm = ADAM_B1 * m_ref[...] + (1.0 - ADAM_B1) * gv
        nv = ADAM_B2 * v_ref[...] + (1.0 - ADAM_B2) * (gv * gv)
        m_hat = nm / (1.0 - ADAM_B1 ** ADAM_STEP)
        v_hat = nv / (1.0 - ADAM_B2 ** ADAM_STEP)
        d_ref[...] = -ADAM_LR * (m_hat / (jnp.sqrt(v_hat) + ADAM_EPS) + ADAM_WD * w_ref[...])
        nm_ref[...] = nm
        nv_ref[...] = nv

    tile = pl.BlockSpec((tr, c), lambda i: (i, 0))
    return pl.pallas_call(
        body, name="adamw", grid=(r // tr,), in_specs=[tile] * 4, out_specs=[tile] * 3,
        out_shape=[jax.ShapeDtypeStruct((r, c), F32)] * 3, compiler_params=_params(("parallel",)),
    )(w, g, m, v)


BIG = ("a_w_in", "a_w_out", "sb_w_k", "sb_w_v", "b_w_q", "b_w_o", "ffn_w1", "ffn_w2")
SMALL = ("a_ln_g", "a_ln_b", "a_w_s", "a_b_s", "mix_ln_g", "mix_ln_b", "ffn_ln_g", "ffn_ln_b")
COL_SHARDED = {"a_w_in": True, "a_w_out": False, "sb_w_k": False, "sb_w_v": False, "b_w_q": False, "b_w_o": False,
               "ffn_w1": True, "ffn_w2": False}


def kernel(x, a_w_in, a_ln_g, a_ln_b, a_w_s, a_b_s, a_w_out, sb_w_k, sb_w_v, b_w_q, b_w_o, mix_ln_g, mix_ln_b, ffn_ln_g, ffn_ln_b, ffn_w1, ffn_w2, loss_target, m_a_w_in, m_a_ln_g, m_a_ln_b, m_a_w_s, m_a_b_s, m_a_w_out, m_sb_w_k, m_sb_w_v, m_b_w_q, m_b_w_o, m_mix_ln_g, m_mix_ln_b, m_ffn_ln_g, m_ffn_ln_b, m_ffn_w1, m_ffn_w2, v_a_w_in, v_a_ln_g, v_a_ln_b, v_a_w_s, v_a_b_s, v_a_w_out, v_sb_w_k, v_sb_w_v, v_b_w_q, v_b_w_o, v_mix_ln_g, v_mix_ln_b, v_ffn_ln_g, v_ffn_ln_b, v_ffn_w1, v_ffn_w2):
    names = BIG + SMALL
    given = dict(a_w_in=a_w_in, a_ln_g=a_ln_g, a_ln_b=a_ln_b, a_w_s=a_w_s, a_b_s=a_b_s, a_w_out=a_w_out, sb_w_k=sb_w_k,
                 sb_w_v=sb_w_v, b_w_q=b_w_q, b_w_o=b_w_o, mix_ln_g=mix_ln_g, mix_ln_b=mix_ln_b, ffn_ln_g=ffn_ln_g,
                 ffn_ln_b=ffn_ln_b, ffn_w1=ffn_w1, ffn_w2=ffn_w2)
    mom = dict(a_w_in=m_a_w_in, a_ln_g=m_a_ln_g, a_ln_b=m_a_ln_b, a_w_s=m_a_w_s, a_b_s=m_a_b_s, a_w_out=m_a_w_out,
               sb_w_k=m_sb_w_k, sb_w_v=m_sb_w_v, b_w_q=m_b_w_q, b_w_o=m_b_w_o, mix_ln_g=m_mix_ln_g, mix_ln_b=m_mix_ln_b,
               ffn_ln_g=m_ffn_ln_g, ffn_ln_b=m_ffn_ln_b, ffn_w1=m_ffn_w1, ffn_w2=m_ffn_w2)
    var = dict(a_w_in=v_a_w_in, a_ln_g=v_a_ln_g, a_ln_b=v_a_ln_b, a_w_s=v_a_w_s, a_b_s=v_a_b_s, a_w_out=v_a_w_out,
               sb_w_k=v_sb_w_k, sb_w_v=v_sb_w_v, b_w_q=v_b_w_q, b_w_o=v_b_w_o, mix_ln_g=v_mix_ln_g, mix_ln_b=v_mix_ln_b,
               ffn_ln_g=v_ffn_ln_g, ffn_ln_b=v_ffn_ln_b, ffn_w1=v_ffn_w1, ffn_w2=v_ffn_w2)

    cx, cy, cc = lax.axis_index("x"), lax.axis_index("y"), lax.axis_index("c")
    chip = (2 * cx + cy).astype(jnp.int32)
    chip_arr = chip.reshape(1)
    core_arr = cc.astype(jnp.int32).reshape(1)

    s, d = x.shape[1], x.shape[2]
    xf = x.reshape(s, d)
    target = loss_target.reshape(s, d)

    def as2d(w):
        return w.reshape(-1, w.shape[-1])

    gw = {}
    for n in BIG:
        for l in ([None] if given[n].ndim == 2 else range(given[n].shape[0])):
            gw[(n, l)] = _cast_into_slot(given[n], l, chip_arr)
    ln_gb = jnp.stack([a_ln_g, a_ln_b])
    ln_slot = lax.dynamic_update_slice(jnp.zeros((N_CHIPS,) + ln_gb.shape, F32), ln_gb[None], (chip, 0, 0, 0))
    layer0 = [("a_w_in", 0), ("a_w_out", 0), ("ffn_w1", 0), ("ffn_w2", 0)]
    gathered = _gather_weights([gw[k] for k in layer0] + [ln_slot])
    gw.update(zip(layer0, gathered[:-1]))
    mixer = {1: [("a_w_in", 1), ("a_w_out", 1)], 2: [("sb_w_k", None), ("sb_w_v", None), ("b_w_q", 0), ("b_w_o", 0)],
             3: [("b_w_q", 1), ("b_w_o", 1)]}

    def riding(d2d=(), ici=()):
        keys = list(d2d) + list(ici)
        return keys, [("d2d", gw[k]) for k in d2d] + [("ici", gw[k]) for k in ici]

    def landed_in(keys, bufs):
        gw.update(zip(keys, bufs))

    ln_full = gathered[-1].transpose(1, 2, 0, 3).reshape(2, N_A, 1, -1)
    a_ln_g3, a_ln_b3 = ln_full[0], ln_full[1]
    mix_g3, mix_b3 = mix_ln_g[:, None, :], mix_ln_b[:, None, :]
    ffn_g3, ffn_b3 = ffn_ln_g[:, None, :], ffn_ln_b[:, None, :]
    bst = jnp.swapaxes(a_b_s, 1, 2)

    saved = []
    xb = _cast_bf16(xf)
    kb = vb = None
    for l in range(DEPTH):
        sv = dict(x_in=xb)
        last = l == DEPTH - 1
        keys, riders = riding(d2d=[("ffn_w1", l), ("ffn_w2", l)] if l else [], ici=mixer[l + 1] if l < N_A else [])
        if l < N_A:
            h, *bufs = _mm_fwd("a_in", xb, gw[("a_w_in", l)], None, True, riders=riders)
            landed_in(keys, bufs)
            vn = _gmlp_norm_fwd(h, a_ln_g3, a_ln_b3, l)
            gated = _gate_fwd(h, vn, a_w_s[l], bst[l])
            xf, xb, xhat, rstd = _mm_resid_ln("a_out", gated, gw[("a_w_out", l)], xf, mix_g3, mix_b3, l)
            sv.update(h=h, vn=vn, gated=gated)
        else:
            j = l - N_A
            if l == N_A:
                kb, *bufs = _mm_fwd("sb_k", xb, gw[("sb_w_k", None)], None, False, _ep_bf16, outs=[(d, BF16)],
                                    riders=riders)
                landed_in(keys, bufs)
                keys, riders = [], ()
                vb = _mm_fwd("sb_v", xb, gw[("sb_w_v", None)], None, False, _ep_bf16, outs=[(d, BF16)])[0]
            q, *bufs = _mm_fwd("b_q", xb, gw[("b_w_q", j)], None, False, _ep_scale_q, outs=[(d, BF16)], riders=riders)
            landed_in(keys, bufs)
            ob, lsum = _attn_fwd(q, kb, vb)
            keys, riders = riding(ici=[] if last else mixer[l + 1])
            xf, xb, xhat, rstd, *bufs = _mm_resid_ln("b_out", ob, gw[("b_w_o", j)], xf, mix_g3, mix_b3, l, riders)
            landed_in(keys, bufs)
            sv.update(q=q, lsum=lsum, ob=ob)
        sv.update(x_mid=xb, xhat1=xhat, rstd1=rstd)
        dff = gw[("ffn_w1", l)].shape[-1] * N_CHIPS
        keys, riders = riding(ici=[] if last else [("ffn_w1", l + 1)])
        pr, act, *bufs = _mm_fwd("ffn_1", xb, gw[("ffn_w1", l)], None, True, _ep_relu2,
                                 outs=[(dff, BF16), (dff, BF16)], riders=riders)
        landed_in(keys, bufs)
        keys, riders = riding(d2d=[] if last else mixer[l + 1], ici=[] if last else [("ffn_w2", l + 1)])
        xf, xb, xhat, rstd, *bufs = _mm_resid_ln("ffn_2", act, gw[("ffn_w2", l)], xf, ffn_g3, ffn_b3, l, riders)
        landed_in(keys, bufs)
        sv.update(pr=pr, act=act, xhat2=xhat, rstd2=rstd)
        saved.append(sv)

    dx, sq = _loss_head(xf, target)
    loss = lax.psum(0.5 * sq[0, 0] / d, ("x", "y", "c"))

    pending = []
    pair_sums, landed = {}, {}

    def arrived(took, outs):
        for (kind, key, arr), out in zip(took, outs):
            if kind == "pair":
                pair_sums[key] = _pair_sum(arr, out, core_arr)
                pending.append(("chip", key, pair_sums[key][1]))
            else:
                landed[key] = out

    def carrying(call, *args, **kw):
        took = list(pending)
        pending.clear()
        out, *rest = call(*args, riders=[(kind, arr) for kind, _, arr in took], **kw)
        arrived(took, rest)
        return out

    def bwd_act(*args, **kw):
        return carrying(_mm_bwd_act, *args, **kw)

    def bwd_w(key, name, a, dy):
        pending.append(("pair", key, carrying(_mm_bwd_w, name, a, dy, gw[key], COL_SHARDED[key[0]])))

    d_mix_g, d_mix_b, d_ffn_g, d_ffn_b = [None] * DEPTH, [None] * DEPTH, [None] * DEPTH, [None] * DEPTH
    d_ln_g, d_ln_b, d_ws, d_bs = [None] * N_A, [None] * N_A, [None] * N_A, [None] * N_A
    dk = dv = None
    for l in reversed(range(DEPTH)):
        sv = saved[l]
        dr, drb, d_ffn_g[l], d_ffn_b[l] = _ln_bwd(dx, sv["xhat2"], sv["rstd2"], ffn_g3, l)
        dff = sv["pr"].shape[1]
        dhd = bwd_act("ffn_2_dx", drb, gw[("ffn_w2", l)], None, False, _ep_relu2_bwd, (sv["pr"],),
                      (pl.BlockSpec((TM, dff // N_CHIPS), lambda j, i, k: (i, j)),), out_dtype=BF16)
        bwd_w(("ffn_w2", l), "ffn_2_dw", sv["act"], drb)
        dx = bwd_act("ffn_1_dx", dhd, gw[("ffn_w1", l)], None, True, _ep_resid, (dr,), (_row_spec(d),))
        bwd_w(("ffn_w1", l), "ffn_1_dw", sv["x_mid"], dhd)

        dr, drb, d_mix_g[l], d_mix_b[l] = _ln_bwd(dx, sv["xhat1"], sv["rstd1"], mix_g3, l)
        quarter = pl.BlockSpec((TM, d // N_CHIPS), lambda j, i, k: (i, j))
        if l < N_A:
            dgated = bwd_act("a_out_dx", drb, gw[("a_w_out", l)], None, False)
            bwd_w(("a_w_out", l), "a_out_dw", sv["gated"], drb)
            du, dvn, d_ws[l], dbs_wide = _gate_bwd(dgated, sv["h"], sv["vn"], a_w_s[l], bst[l])
            d_bs[l] = dbs_wide[:, :, 0]
            dh, dlg, dlb = _gmlp_in_bwd(sv["h"], du, dvn, a_ln_g3, l)
            d_ln_g[l], d_ln_b[l] = dlg[0], dlb[0]
            dx = bwd_act("a_in_dx", dh, gw[("a_w_in", l)], None, True, _ep_resid, (dr,), (_row_spec(d),))
            bwd_w(("a_w_in", l), "a_in_dw", sv["x_in"], dh)
        else:
            j = l - N_A
            do = bwd_act("b_out_dx", drb, gw[("b_w_o", j)], None, False)
            bwd_w(("b_w_o", j), "b_out_dw", sv["ob"], drb)
            dq, dk, dv = _attn_bwd(sv["q"], kb, vb, do, sv["lsum"], dk, dv)
            dx = bwd_act("b_q_dx", dq, gw[("b_w_q", j)], None, False, _ep_resid, (dr,), (quarter,))
            bwd_w(("b_w_q", j), "b_q_dw", sv["x_in"], dq)
            if l == N_A:
                dx = bwd_act("sb_k_dx", dk, gw[("sb_w_k", None)], None, False, _ep_add, (dx,), (quarter,))
                bwd_w(("sb_w_k", None), "sb_k_dw", sv["x_in"], dk)
                dx = bwd_act("sb_v_dx", dv, gw[("sb_w_v", None)], None, False, _ep_add, (dx,), (quarter,))
                bwd_w(("sb_w_v", None), "sb_v_dw", sv["x_in"], dv)
    grad_x = dx.reshape(x.shape)

    while pending:
        took = list(pending)
        pending.clear()
        for kind, exchange in (("pair", _pair_exchange), ("chip", _chip_exchange)):
            some = [t for t in took if t[0] == kind]
            if some:
                arrived(some, exchange([arr for _, _, arr in some]))

    place_arr = jnp.stack([chip, cc.astype(jnp.int32)])
    stacked = []
    for n in BIG:
        layers = [None] if given[n].ndim == 2 else range(given[n].shape[0])
        out = lax.empty((len(layers),) + given[n].shape[-2:], F32)
        for at, l in enumerate(layers):
            out = _chip_sum(pair_sums[(n, l)][0], landed[(n, l)], place_arr, out, at)
        stacked.append(out)
    grads = {n: g.reshape(given[n].shape) for n, g in zip(BIG, _half_swap(stacked))}

    small_full = dict(a_ln_g=jnp.stack(d_ln_g), a_ln_b=jnp.stack(d_ln_b), a_w_s=jnp.stack(d_ws), a_b_s=jnp.stack(d_bs),
                      mix_ln_g=jnp.concatenate(d_mix_g), mix_ln_b=jnp.concatenate(d_mix_b),
                      ffn_ln_g=jnp.concatenate(d_ffn_g), ffn_ln_b=jnp.concatenate(d_ffn_b))
    packed = jnp.concatenate([small_full[n].reshape(-1) for n in SMALL])
    total = packed.shape[0]
    ncol = -(-total // (N_DEV * LANES)) * LANES
    packed = jnp.pad(packed, (0, N_DEV * ncol - total)).reshape(N_DEV, ncol)
    reduced = _all_reduce_small(packed).reshape(-1)
    off = 0
    for n in SMALL:
        size = small_full[n].size
        g = reduced[off:off + size].reshape(small_full[n].shape)
        off += size
        if n in ("a_ln_g", "a_ln_b"):
            wq = given[n].shape[1]
            g = lax.dynamic_slice_in_dim(g, chip * wq, wq, axis=1)
        grads[n] = g

    delta, new_m, new_v = {}, {}, {}
    for n in names:
        shape = given[n].shape
        dl, nm, nv = _adamw(as2d(given[n]), as2d(grads[n]), as2d(mom[n]), as2d(var[n]))
        delta[n], new_m[n], new_v[n] = dl.reshape(shape), nm.reshape(shape), nv.reshape(shape)

    order = ("a_w_in", "a_ln_g", "a_ln_b", "a_w_s", "a_b_s", "a_w_out", "sb_w_k", "sb_w_v", "b_w_q", "b_w_o",
             "mix_ln_g", "mix_ln_b", "ffn_ln_g", "ffn_ln_b", "ffn_w1", "ffn_w2")
    return (loss, grad_x, *[grads[n] for n in order], *[delta[n] for n in order],
            *[new_m[n] for n in order], *[new_v[n] for n in order])
```

```python
import math

import jax
import jax.numpy as jnp
from jax import lax
from jax.experimental import pallas as pl
from jax.experimental.pallas import tpu as pltpu

F32 = jnp.float32
BF16 = jnp.bfloat16
MESH = pl.DeviceIdType.MESH

N_CHIPS = 4
DEPTH = 4
N_A = 2
ALPHA = float((2 * DEPTH) ** 0.25)
LN_EPS = 1e-5
CHUNK = 64
GMLP_BLOCK = 128
GMLP_GROUPS = 8
HEAD_DIM = 64
LANES = 128
ATT_T = 256
ADAM_LR = 0.001
ADAM_B1 = 0.9
ADAM_B2 = 0.999
ADAM_EPS = 1e-08
ADAM_WD = 0.01
ADAM_STEP = 10
VMEM_LIMIT = 56 * 1024 * 1024
TM = 512
TS = 1024

NN = ((1,), (0,))
NT = ((1,), (1,))
TN = ((0,), (0,))


def _params(sem):
    return pltpu.CompilerParams(dimension_semantics=sem, vmem_limit_bytes=VMEM_LIMIT)


def _dot(a, b, contract):
    return lax.dot_general(a, b, (contract, ((), ())), preferred_element_type=F32)


def _rider_out(kind, arr):
    shape = (arr.shape[0], arr.shape[1] // 2, arr.shape[2]) if kind == "pair" else arr.shape
    return jax.ShapeDtypeStruct(shape, arr.dtype)


def _rider_copies(kind, src, dst, send_sems, recv_sems, base):
    x, y, c, chips = _place()
    me = 2 * x + y
    sibling = (x, y, 1 - c)

    def copy(k, part, land, to):
        return pltpu.make_async_remote_copy(src_ref=part, dst_ref=land, send_sem=send_sems.at[base + k],
                                            recv_sem=recv_sems.at[base + k], device_id=to, device_id_type=MESH)

    if kind == "pair":
        h = src.shape[1] // 2
        cp = copy(0, src.at[:, pl.ds((1 - c) * h, h)], dst, sibling)
        return [cp], [cp]
    h = dst.shape[1] // 2
    starts, arrivals = [], []
    for k, chip in enumerate(chips):
        blk = 2 * chip[0] + chip[1]
        if kind == "ici":
            starts.append(copy(k, dst.at[me, pl.ds(c * h, h)], dst.at[me, pl.ds(c * h, h)], (*chip, c)))
            arrivals.append(copy(k, dst.at[blk, pl.ds(c * h, h)], dst.at[blk, pl.ds(c * h, h)], (*chip, c)))
        elif kind == "d2d":
            starts.append(copy(k, dst.at[blk, pl.ds(c * h, h)], dst.at[blk, pl.ds(c * h, h)], sibling))
            arrivals.append(copy(k, dst.at[blk, pl.ds((1 - c) * h, h)], dst.at[blk, pl.ds((1 - c) * h, h)], sibling))
        else:
            starts.append(copy(k, src.at[blk], dst.at[me], (*chip, c)))
            arrivals.append(copy(k, src.at[blk], dst.at[blk], (*chip, c)))
    return starts, arrivals


RIDER_SEMS = 3


def _matmul(name, operands, in_specs, out_shapes, out_specs, grid, contract, epilogue, acc_shape, aliases=None,
            chunks=None, riders=()):
    nk = grid[2]
    n_in, n_out, nr = len(operands), len(out_shapes), len(riders)
    n_plain = n_in + nr + n_out

    def body(*refs):
        ins, outs = refs[:n_in], refs[n_in + nr:n_plain]
        if nr:
            srcs, dsts = refs[n_in:n_in + nr], refs[n_plain:n_plain + nr]
            send_sems, recv_sems = refs[-2:]
            pid = [pl.program_id(ax) for ax in range(3)]
            first = (pid[0] == 0) & (pid[1] == 0) & (pid[2] == 0)
            last = (pid[0] == grid[0] - 1) & (pid[1] == grid[1] - 1) & (pid[2] == grid[2] - 1)

            def copies(n):
                return _rider_copies(riders[n][0], srcs[n], dsts[n], send_sems, recv_sems, RIDER_SEMS * n)

            @pl.when(first)
            def _():
                for n in range(nr):
                    for cp in copies(n)[0]:
                        cp.start()

        compute(refs, ins, outs)
        if nr:
            @pl.when(last)
            def _():
                for n in range(nr):
                    starts, arrivals = copies(n)
                    for cp in arrivals:
                        cp.wait_recv()
                    for cp in starts:
                        cp.wait_send()

    def compute(refs, ins, outs):
        if chunks is None:
            p = _dot(ins[0][...].astype(BF16), ins[1][...].astype(BF16), contract)
        else:
            width = ins[0].shape[1] // chunks
            p = None
            for j in range(chunks):
                pj = _dot(ins[0][:, j * width:(j + 1) * width].astype(BF16), ins[1][j].astype(BF16), contract)
                p = pj if p is None else p + pj
        if nk == 1:
            epilogue(p, ins[2:], outs)
            return
        acc = refs[n_plain + nr]
        k = pl.program_id(2)

        @pl.when(k == 0)
        def _():
            acc[...] = p

        @pl.when((k > 0) & (k < nk - 1))
        def _():
            acc[...] += p

        @pl.when(k == nk - 1)
        def _():
            epilogue(acc[...] + p, ins[2:], outs)

    rbufs = [b for _, b in riders]
    in_place = {n_in + n: n_out + n for n, (kind, _) in enumerate(riders) if kind in ("ici", "d2d")}
    scratch = ([] if nk == 1 else [pltpu.VMEM(acc_shape, F32)]) \
        + [pltpu.SemaphoreType.DMA((RIDER_SEMS * nr,))] * (2 if nr else 0)
    return pl.pallas_call(
        body, name=name, grid=grid, in_specs=list(in_specs) + _any_specs(nr), out_specs=list(out_specs) + _any_specs(nr),
        out_shape=list(out_shapes) + [_rider_out(kind, b) for kind, b in riders],
        scratch_shapes=scratch,
        input_output_aliases={**(aliases or {}), **in_place},
        compiler_params=_params(("arbitrary",) * 3 if nr else ("parallel", "parallel", "arbitrary")),
    )(*operands, *rbufs)


def _wspec(w, layer, whole=False):
    r, c = w.shape[-2:]
    lead = N_CHIPS if whole else None
    if w.ndim == 4:
        return pl.BlockSpec((lead, None, r, c), lambda j, i, k: (0 if whole else j, layer, 0, 0))
    return pl.BlockSpec((lead, r, c), lambda j, i, k: (0 if whole else j, 0, 0))


def _ep_store(p, ins, outs):
    for o in outs:
        o[...] = p.astype(o.dtype)


def _mm_fwd(name, a, w, layer, col_sharded, epilogue=_ep_store, extras=(), extra_specs=(), outs=None, riders=()):
    s = a.shape[0]
    r, c = w.shape[-2:]
    if col_sharded:
        grid = (N_CHIPS, s // TM, 1)
        a_spec = pl.BlockSpec((TM, r), lambda j, i, k: (i, 0))
        n_out = N_CHIPS * c
    else:
        grid = (1, s // TM, 1)
        a_spec = pl.BlockSpec((TM, N_CHIPS * r), lambda j, i, k: (i, 0))
        n_out = c
    if outs is None:
        outs = [(n_out, F32)]
    out_shapes = [jax.ShapeDtypeStruct((s, n), dt) for n, dt in outs]
    out_specs = [pl.BlockSpec((TM, c if n == n_out else n), lambda j, i, k: (i, j)) for n, _ in outs]
    return _matmul(name, (a, w) + tuple(extras), [a_spec, _wspec(w, layer, not col_sharded)] + list(extra_specs),
                   out_shapes, out_specs, grid, NN, epilogue, (TM, c), chunks=None if col_sharded else N_CHIPS,
                   riders=riders)


def _mm_bwd_act(name, dy, w, layer, col_sharded, epilogue=_ep_store, extras=(), extra_specs=(), out_dtype=F32,
                riders=()):
    s = dy.shape[0]
    r, c = w.shape[-2:]
    if col_sharded:
        grid = (1, s // TM, 1)
        a_spec = pl.BlockSpec((TM, N_CHIPS * c), lambda j, i, k: (i, 0))
        n_out = r
    else:
        grid = (N_CHIPS, s // TM, 1)
        a_spec = pl.BlockSpec((TM, c), lambda j, i, k: (i, 0))
        n_out = N_CHIPS * r
    o_spec = pl.BlockSpec((TM, r), lambda j, i, k: (i, j))
    return _matmul(name, (dy, w) + tuple(extras), [a_spec, _wspec(w, layer, col_sharded)] + list(extra_specs),
                   [jax.ShapeDtypeStruct((s, n_out), out_dtype)], [o_spec], grid, NT, epilogue, (TM, r),
                   chunks=N_CHIPS if col_sharded else None, riders=riders)


def _mm_bwd_w(name, a, dy, w, col_sharded, riders=()):
    s = a.shape[0]
    r, c = w.shape[-2:]
    ts = min(TS, s)
    grid = (N_CHIPS, 1, s // ts)
    if col_sharded:
        a_spec = pl.BlockSpec((ts, r), lambda j, i, k: (k, 0))
        b_spec = pl.BlockSpec((ts, c), lambda j, i, k: (k, j))
    else:
        a_spec = pl.BlockSpec((ts, r), lambda j, i, k: (k, j))
        b_spec = pl.BlockSpec((ts, c), lambda j, i, k: (k, 0))

    def epilogue(p, ins, outs):
        outs[0][...] = p

    return _matmul(name, (a, dy), [a_spec, b_spec], [jax.ShapeDtypeStruct(w.shape, F32)], [_wspec(w, None)], grid, TN,
                   epilogue, (r, c), riders=riders)


def _row_spec(n):
    return pl.BlockSpec((TM, n), lambda j, i, k: (i, 0))


def _vec_spec(layer, n):
    return pl.BlockSpec((None, 1, n), lambda j, i, k: (layer, 0, 0))


def _ep_resid_ln(p, ins, outs):
    x_ref, g_ref, b_ref = ins
    xf_ref, xb_ref, xhat_ref, rstd_ref = outs
    r = ALPHA * x_ref[...] + p
    mu = jnp.mean(r, axis=-1, keepdims=True)
    d = r - mu
    var = jnp.mean(d * d, axis=-1, keepdims=True)
    rstd = lax.rsqrt(var + LN_EPS)
    xhat = d * rstd
    y = xhat * g_ref[...] + b_ref[...]
    xf_ref[...] = y
    xb_ref[...] = y.astype(BF16)
    xhat_ref[...] = xhat
    rstd_ref[...] = rstd


def _mm_resid_ln(name, a, w, x, g3, b3, ln_layer, riders=()):
    d = x.shape[1]
    return _mm_fwd(name, a, w, None, False, _ep_resid_ln, (x, g3, b3),
                   (_row_spec(d), _vec_spec(ln_layer, d), _vec_spec(ln_layer, d)),
                   outs=[(d, F32), (d, BF16), (d, F32), (1, F32)], riders=riders)


def _ep_relu2(p, ins, outs):
    h = jnp.maximum(p, 0.0)
    outs[0][...] = h.astype(BF16)
    outs[1][...] = (h * h).astype(BF16)


def _ep_scale_q(p, ins, outs):
    outs[0][...] = (p * (HEAD_DIM ** -0.5)).astype(BF16)


def _ep_bf16(p, ins, outs):
    outs[0][...] = p.astype(BF16)


def _ep_relu2_bwd(p, ins, outs):
    outs[0][...] = (p * (2.0 * ins[0][...].astype(F32))).astype(BF16)


def _ep_resid(p, ins, outs):
    outs[0][...] = ALPHA * ins[0][...] + p


def _ep_add(p, ins, outs):
    outs[0][...] = ins[0][...] + p


def _gelu_grad(x):
    c0 = math.sqrt(2.0 / math.pi)
    t = jnp.tanh(c0 * (x + 0.044715 * (x * x * x)))
    return 0.5 * (1.0 + t) + (0.5 * x) * (1.0 - t * t) * (c0 * (1.0 + 3.0 * 0.044715 * (x * x)))


def _cast_bf16(w2d):
    r, c = w2d.shape
    tr = min(r, 512)

    def body(w_ref, o_ref):
        o_ref[...] = w_ref[...].astype(BF16)

    return pl.pallas_call(
        body, name="cast_bf16", grid=(r // tr,),
        in_specs=[pl.BlockSpec((tr, c), lambda i: (i, 0))], out_specs=pl.BlockSpec((tr, c), lambda i: (i, 0)),
        out_shape=jax.ShapeDtypeStruct((r, c), BF16), compiler_params=_params(("parallel",)),
    )(w2d)


def _cast_into_slot(w, layer, chip):
    r, c = w.shape[-2:]
    tr = min(r, 512)

    def body(chip_ref, w_ref, o_ref):
        o_ref[...] = w_ref[...].astype(BF16)

    if layer is None:
        w_spec = pl.BlockSpec((tr, c), lambda i, chip_ref: (i, 0))
    else:
        w_spec = pl.BlockSpec((None, tr, c), lambda i, chip_ref: (layer, i, 0))
    grid_spec = pltpu.PrefetchScalarGridSpec(
        num_scalar_prefetch=1, grid=(r // tr,), in_specs=[w_spec],
        out_specs=pl.BlockSpec((None, tr, c), lambda i, chip_ref: (chip_ref[0], i, 0)))
    return pl.pallas_call(
        body, name="cast_into_slot", grid_spec=grid_spec,
        out_shape=jax.ShapeDtypeStruct((N_CHIPS, r, c), BF16), compiler_params=_params(("parallel",)),
    )(chip, w)


def _gmlp_norm_fwd(h, g3, b3, layer):
    s, w2 = h.shape
    w = w2 // 2

    def body(h_ref, g_ref, b_ref, o_ref):
        z = jax.nn.gelu(h_ref[...])
        mu = jnp.mean(z, axis=-1, keepdims=True)
        d = z - mu
        var = jnp.mean(d * d, axis=-1, keepdims=True)
        o_ref[...] = (d * lax.rsqrt(var + LN_EPS) * g_ref[...] + b_ref[...]).astype(BF16)

    vec = pl.BlockSpec((None, 1, w), lambda i: (layer, 0, 0))
    return pl.pallas_call(
        body, name="gmlp_norm_fwd", grid=(s // TM,),
        in_specs=[pl.BlockSpec((TM, w), lambda i: (i, 1)), vec, vec],
        out_specs=pl.BlockSpec((TM, w), lambda i: (i, 0)),
        out_shape=jax.ShapeDtypeStruct((s, w), BF16), compiler_params=_params(("parallel",)),
    )(h, g3, b3)


def _chunk_mask():
    t = lax.broadcasted_iota(jnp.int32, (GMLP_BLOCK, GMLP_BLOCK), 0)
    s = lax.broadcasted_iota(jnp.int32, (GMLP_BLOCK, GMLP_BLOCK), 1)
    return (s // CHUNK) <= (t // CHUNK)


SG_ROWS = 512


def _gate_fwd(h, vn, ws, bst):
    s, w = vn.shape
    gd = w // GMLP_GROUPS

    def body(h_ref, v_ref, ws_ref, bs_ref, o_ref):
        mask = _chunk_mask()
        for g in range(GMLP_GROUPS):
            wm = jnp.where(mask, ws_ref[g], 0.0).astype(BF16)
            bias = bs_ref[:, g:g + 1]
            cols = slice(g * gd, (g + 1) * gd)
            for n in range(SG_ROWS // GMLP_BLOCK):
                rows = slice(n * GMLP_BLOCK, (n + 1) * GMLP_BLOCK)
                sp = _dot(wm, v_ref[rows, cols], NN) + bias
                o_ref[rows, cols] = (jax.nn.gelu(h_ref[rows, cols]) * sp).astype(BF16)

    return pl.pallas_call(
        body, name="gate_fwd", grid=(s // SG_ROWS,),
        in_specs=[pl.BlockSpec((SG_ROWS, w), lambda i: (i, 0)), pl.BlockSpec((SG_ROWS, w), lambda i: (i, 0)),
                  pl.BlockSpec(ws.shape, lambda i: (0, 0, 0)), pl.BlockSpec(bst.shape, lambda i: (0, 0))],
        out_specs=pl.BlockSpec((SG_ROWS, w), lambda i: (i, 0)),
        out_shape=jax.ShapeDtypeStruct((s, w), BF16), compiler_params=_params(("parallel",)),
    )(h, vn, ws, bst)


def _gate_bwd(dgated, h, vn, ws, bst):
    s, w = vn.shape
    gd = w // GMLP_GROUPS
    nsteps = s // SG_ROWS

    def body(dg_ref, h_ref, v_ref, ws_ref, bs_ref, du_ref, dv_ref, dws_ref, dbs_ref, dsum):
        i = pl.program_id(0)

        @pl.when(i == 0)
        def _():
            dws_ref[...] = jnp.zeros_like(dws_ref)
            dsum[...] = jnp.zeros_like(dsum)

        mask = _chunk_mask()
        for g in range(GMLP_GROUPS):
            wm = jnp.where(mask, ws_ref[g], 0.0).astype(BF16)
            bias = bs_ref[:, g:g + 1]
            cols = slice(g * gd, (g + 1) * gd)
            dw = jnp.zeros((GMLP_BLOCK, GMLP_BLOCK), F32)
            dsg = jnp.zeros((GMLP_BLOCK, gd), F32)
            for n in range(SG_ROWS // GMLP_BLOCK):
                rows = slice(n * GMLP_BLOCK, (n + 1) * GMLP_BLOCK)
                vb = v_ref[rows, cols]
                sp = _dot(wm, vb, NN) + bias
                dg = dg_ref[rows, cols]
                du_ref[rows, cols] = dg * sp
                ds = dg * jax.nn.gelu(h_ref[rows, cols])
                dsb = ds.astype(BF16)
                dw += _dot(dsb, vb, NT)
                dsg += ds
                dv_ref[rows, cols] = _dot(wm, dsb, TN)
            dws_ref[g] += dw
            dsum[:, cols] += dsg

        @pl.when(i == nsteps - 1)
        def _():
            for g in range(GMLP_GROUPS):
                dws_ref[g] = jnp.where(mask, dws_ref[g], 0.0)
                tot = jnp.sum(dsum[:, g * gd:(g + 1) * gd], axis=-1, keepdims=True)
                dbs_ref[g] = jnp.broadcast_to(tot, (GMLP_BLOCK, LANES))

    tile = pl.BlockSpec((SG_ROWS, w), lambda i: (i, 0))
    return pl.pallas_call(
        body, name="gate_bwd", grid=(nsteps,),
        in_specs=[tile, tile, tile, pl.BlockSpec(ws.shape, lambda i: (0, 0, 0)), pl.BlockSpec(bst.shape, lambda i: (0, 0))],
        out_specs=[tile, tile, pl.BlockSpec(ws.shape, lambda i: (0, 0, 0)),
                   pl.BlockSpec((GMLP_GROUPS, GMLP_BLOCK, LANES), lambda i: (0, 0, 0))],
        out_shape=[jax.ShapeDtypeStruct((s, w), F32), jax.ShapeDtypeStruct((s, w), F32),
                   jax.ShapeDtypeStruct(ws.shape, F32), jax.ShapeDtypeStruct((GMLP_GROUPS, GMLP_BLOCK, LANES), F32)],
        scratch_shapes=[pltpu.VMEM((GMLP_BLOCK, w), F32)],
        compiler_params=_params(("arbitrary",)),
    )(dgated, h, vn, ws, bst)


GB_ROWS = 256


def _gmlp_in_bwd(h, du, dvn, g3, layer):
    s, w2 = h.shape
    w = w2 // 2
    nsteps = s // GB_ROWS

    def body(h_ref, du_ref, dv_ref, g_ref, dh_ref, dg_ref, db_ref):
        i = pl.program_id(0)

        @pl.when(i == 0)
        def _():
            dg_ref[...] = jnp.zeros_like(dg_ref)
            db_ref[...] = jnp.zeros_like(db_ref)

        hu = h_ref[:, :w]
        hv = h_ref[:, w:]
        dh_ref[:, :w] = (du_ref[...] * _gelu_grad(hu)).astype(BF16)
        z = jax.nn.gelu(hv)
        mu = jnp.mean(z, axis=-1, keepdims=True)
        d = z - mu
        var = jnp.mean(d * d, axis=-1, keepdims=True)
        rstd = lax.rsqrt(var + LN_EPS)
        xhat = d * rstd
        dy = dv_ref[...]
        db_ref[...] += jnp.sum(dy, axis=0, keepdims=True)
        dg_ref[...] += jnp.sum(dy * xhat, axis=0, keepdims=True)
        dxh = dy * g_ref[...]
        m1 = jnp.mean(dxh, axis=-1, keepdims=True)
        m2 = jnp.mean(dxh * xhat, axis=-1, keepdims=True)
        dz = rstd * (dxh - m1 - xhat * m2)
        dh_ref[:, w:] = (dz * _gelu_grad(hv)).astype(BF16)

    half = pl.BlockSpec((GB_ROWS, w), lambda i: (i, 0))
    vec = pl.BlockSpec((1, w), lambda i: (0, 0))
    return pl.pallas_call(
        body, name="gmlp_in_bwd", grid=(nsteps,),
        in_specs=[pl.BlockSpec((GB_ROWS, w2), lambda i: (i, 0)), half, half,
                  pl.BlockSpec((None, 1, w), lambda i: (layer, 0, 0))],
        out_specs=[pl.BlockSpec((GB_ROWS, w2), lambda i: (i, 0)), vec, vec],
        out_shape=[jax.ShapeDtypeStruct((s, w2), BF16), jax.ShapeDtypeStruct((1, w), F32), jax.ShapeDtypeStruct((1, w), F32)],
        compiler_params=_params(("arbitrary",)),
    )(h, du, dvn, g3)


def _ln_bwd(dy, xhat, rstd, g3, layer):
    s, d = dy.shape
    nsteps = s // TM

    def body(dy_ref, xh_ref, rs_ref, g_ref, dr_ref, drb_ref, dg_ref, db_ref):
        i = pl.program_id(0)

        @pl.when(i == 0)
        def _():
            dg_ref[...] = jnp.zeros_like(dg_ref)
            db_ref[...] = jnp.zeros_like(db_ref)

        dyv = dy_ref[...]
        xhat_v = xh_ref[...]
        db_ref[...] += jnp.sum(dyv, axis=0, keepdims=True)
        dg_ref[...] += jnp.sum(dyv * xhat_v, axis=0, keepdims=True)
        dxh = dyv * g_ref[...]
        m1 = jnp.mean(dxh, axis=-1, keepdims=True)
        m2 = jnp.mean(dxh * xhat_v, axis=-1, keepdims=True)
        dr = rs_ref[...] * (dxh - m1 - xhat_v * m2)
        dr_ref[...] = dr
        drb_ref[...] = dr.astype(BF16)

    tile = pl.BlockSpec((TM, d), lambda i: (i, 0))
    vec = pl.BlockSpec((1, d), lambda i: (0, 0))
    return pl.pallas_call(
        body, name="ln_bwd", grid=(nsteps,),
        in_specs=[tile, tile, pl.BlockSpec((TM, 1), lambda i: (i, 0)), pl.BlockSpec((None, 1, d), lambda i: (layer, 0, 0))],
        out_specs=[tile, tile, vec, vec],
        out_shape=[jax.ShapeDtypeStruct((s, d), F32), jax.ShapeDtypeStruct((s, d), BF16),
                   jax.ShapeDtypeStruct((1, d), F32), jax.ShapeDtypeStruct((1, d), F32)],
        compiler_params=_params(("arbitrary",)),
    )(dy, xhat, rstd, g3)


def _loss_head(y, target):
    s, d = y.shape

    def body(y_ref, t_ref, dy_ref, l_ref):
        i = pl.program_id(0)

        @pl.when(i == 0)
        def _():
            l_ref[...] = jnp.zeros_like(l_ref)

        e = y_ref[...] - t_ref[...]
        dy_ref[...] = e * (1.0 / d)
        l_ref[...] += jnp.sum(jnp.sum(e * e, axis=1, keepdims=True), axis=0, keepdims=True)

    tile = pl.BlockSpec((TM, d), lambda i: (i, 0))
    return pl.pallas_call(
        body, name="loss_head", grid=(s // TM,), in_specs=[tile, tile],
        out_specs=[tile, pl.BlockSpec((1, 1), lambda i: (0, 0))],
        out_shape=[jax.ShapeDtypeStruct((s, d), F32), jax.ShapeDtypeStruct((1, 1), F32)],
        compiler_params=_params(("arbitrary",)),
    )(y, target)


LOG2E = 1.4426950408889634
DEAD_LOG2 = -160.0
FIRST_LANE = 1


def _sb_terms(z, causal):
    z2 = z * LOG2E
    e = jnp.exp2(-jnp.abs(z2))
    l1p = jnp.log2(1.0 + e)
    lb = jnp.minimum(z2, 0.0) - l1p
    lr = lb - z2
    if causal is not None:
        lr = jnp.where(causal, lr, 0.0)
    return lb, lr, e


def _split_hi_lo(x):
    hi = x.astype(BF16)
    lo = (x - hi.astype(F32)).astype(BF16)
    return jnp.concatenate([hi, lo], axis=1)


def _att_consts(prefix):
    r = lax.broadcasted_iota(jnp.int32, (2 * ATT_T, ATT_T), 0) % ATT_T
    c = lax.broadcasted_iota(jnp.int32, (2 * ATT_T, ATT_T), 1)
    tri2 = jnp.where((r <= c) if prefix else (r >= c), 1.0, 0.0).astype(BF16)
    r = lax.broadcasted_iota(jnp.int32, (ATT_T, ATT_T), 0)
    c = lax.broadcasted_iota(jnp.int32, (ATT_T, ATT_T), 1)
    causal = c < r
    head_a = lax.broadcasted_iota(jnp.int32, (1, LANES), 1) < HEAD_DIM
    return tri2, causal, head_a


def _attn_fwd(q, k, v):
    s, d = q.shape
    nq = s // ATT_T

    def body(q_ref, k_ref, v_ref, ob_ref, lsum_ref, acc_a, acc_b, rem_a, rem_b):
        i = pl.program_id(1)
        tri, causal, head_a = _att_consts(prefix=False)
        q2 = q_ref[...]
        zero = jnp.zeros_like(q2)
        qa = jnp.where(head_a, q2, zero)
        qb = jnp.where(head_a, zero, q2)
        acc_a[...] = jnp.zeros_like(acc_a)
        acc_b[...] = jnp.zeros_like(acc_b)
        rem_a[...] = jnp.zeros_like(rem_a)
        rem_b[...] = jnp.zeros_like(rem_b)

        def block(kb, mask):
            rows = pl.ds(pl.multiple_of(kb * ATT_T, ATT_T), ATT_T)
            k2 = k_ref[rows, :]
            v2 = v_ref[rows, :]
            heads = ((qa, acc_a, rem_a), (qb, acc_b, rem_b))
            zs = [_dot(qm, k2, NT) for qm, _, _ in heads]
            terms = [_sb_terms(z, mask) for z in zs]
            sums = [_dot(_split_hi_lo(lr), tri, NN) for _, lr, _ in terms]
            for (_, acc, rem), (lb, lr, _), sincl in zip(heads, terms, sums):
                a = jnp.exp2(lb + (sincl - lr) + rem[...])
                if mask is not None:
                    a = jnp.where(mask, a, 0.0)
                rem[...] += sincl[:, 0:1]
                acc[...] += _dot(a.astype(BF16), v2, NN)

        block(i, causal)

        def live():
            return jnp.maximum(jnp.max(rem_a[...]), jnp.max(rem_b[...])) > DEAD_LOG2

        def go_on(carry):
            t, alive = carry
            return (t < i) & alive

        def step(carry):
            t, _ = carry
            block(i - 1 - t, None)
            return t + 1, live()

        done, _ = lax.while_loop(go_on, step, (jnp.int32(0), live()))
        first = (i - done).astype(F32)
        ob_ref[...] = jnp.where(head_a, acc_a[...], acc_b[...]).astype(BF16)
        lane = lax.broadcasted_iota(jnp.int32, (1, LANES), 1)
        lsum_ref[...] = jnp.where(lane == FIRST_LANE, first, jnp.where(head_a, rem_a[...], rem_b[...]))

    qspec = pl.BlockSpec((ATT_T, LANES), lambda p, i: (i, p))
    kspec = pl.BlockSpec((s, LANES), lambda p, i: (0, p))
    return pl.pallas_call(
        body, name="attn_fwd", grid=(d // LANES, nq), in_specs=[qspec, kspec, kspec],
        out_specs=[qspec, qspec],
        out_shape=[jax.ShapeDtypeStruct((s, d), BF16), jax.ShapeDtypeStruct((s, d), F32)],
        scratch_shapes=[pltpu.VMEM((ATT_T, LANES), F32), pltpu.VMEM((ATT_T, LANES), F32),
                        pltpu.VMEM((ATT_T, 1), F32), pltpu.VMEM((ATT_T, 1), F32)],
        compiler_params=_params(("parallel", "arbitrary")),
    )(q, k, v)


def _attn_bwd(q, k, v, do, lsum, dk_prev=None, dv_prev=None):
    s, d = q.shape
    nq = s // ATT_T
    has_prev = dk_prev is not None

    def body(*refs):
        q_ref, k_ref, v_ref, do_ref, ls_ref = refs[:5]
        n_in = 7 if has_prev else 5
        dq_ref, dk_ref, dv_ref, acc_a, acc_b, pre_a, pre_b, gp_a, gp_b = refs[n_in:]
        i = pl.program_id(1)

        @pl.when(i == 0)
        def _():
            if has_prev:
                dk_ref[...] = refs[5][...]
                dv_ref[...] = refs[6][...]
            else:
                dk_ref[...] = jnp.zeros_like(dk_ref)
                dv_ref[...] = jnp.zeros_like(dv_ref)

        tri, causal, head_a = _att_consts(prefix=True)
        q2 = q_ref[...]
        zero = jnp.zeros_like(q2)
        qa = jnp.where(head_a, q2, zero)
        qb = jnp.where(head_a, zero, q2)
        do2 = do_ref[...]
        doa = jnp.where(head_a, do2, 0.0).astype(BF16)
        dob = jnp.where(head_a, 0.0, do2).astype(BF16)
        ls2 = ls_ref[...]
        tot_a = ls2[:, 0:1]
        tot_b = ls2[:, HEAD_DIM:HEAD_DIM + 1]
        for r in (acc_a, acc_b, pre_a, pre_b, gp_a, gp_b):
            r[...] = jnp.zeros_like(r)

        def block(kb, mask):
            rows = pl.ds(pl.multiple_of(kb * ATT_T, ATT_T), ATT_T)
            k2 = k_ref[rows, :]
            v2 = v_ref[rows, :]
            dk_new = jnp.zeros((ATT_T, LANES), F32)
            dv_new = jnp.zeros((ATT_T, LANES), F32)
            heads = ((qa, doa, tot_a, acc_a, pre_a, gp_a), (qb, dob, tot_b, acc_b, pre_b, gp_b))
            zs = [_dot(h[0], k2, NT) for h in heads]
            das = [_dot(h[1], v2, NT) for h in heads]
            terms = [_sb_terms(z, mask) for z in zs]
            psums = [_dot(_split_hi_lo(lr), tri, NN) for _, lr, _ in terms]
            gs, abs_ = [], []
            for (_, _, tot, _, pre, _), (lb, _, _), pincl, da in zip(heads, terms, psums, das):
                a = jnp.exp2(lb + (tot - (pre[...] + pincl)))
                if mask is not None:
                    a = jnp.where(mask, a, 0.0)
                pre[...] += pincl[:, ATT_T - 1:ATT_T]
                gs.append(a * da)
                abs_.append(a.astype(BF16))
            gsums = [_dot(_split_hi_lo(g), tri, NN) for g in gs]
            dzs = []
            for (_, _, _, _, _, gpre), z, (_, _, e), g, gincl in zip(heads, zs, terms, gs, gsums):
                gbefore = gpre[...] + (gincl - g)
                gpre[...] += gincl[:, ATT_T - 1:ATT_T]
                inv = 1.0 / (1.0 + e)
                beta = jnp.where(z >= 0.0, inv, e * inv)
                dz = g - beta * (g + gbefore)
                if mask is not None:
                    dz = jnp.where(mask, dz, 0.0)
                dzs.append(dz.astype(BF16))
            for (qm, dom, _, acc, _, _), ab, dzb in zip(heads, abs_, dzs):
                dv_new += _dot(ab, dom, TN)
                dk_new += _dot(dzb, qm, TN)
                acc[...] += _dot(dzb, k2, NN)
            dk_ref[rows, :] += dk_new
            dv_ref[rows, :] += dv_new

        def step(kb, carry):
            block(kb, None)
            return carry

        first = jnp.clip(jnp.max(ls2[:, FIRST_LANE:FIRST_LANE + 1]).astype(jnp.int32), 0, i)
        lax.fori_loop(first, i, step, 0)
        block(i, causal)
        dq_ref[...] = (jnp.where(head_a, acc_a[...], acc_b[...]) * (HEAD_DIM ** -0.5)).astype(BF16)

    qspec = pl.BlockSpec((ATT_T, LANES), lambda p, i: (i, p))
    kspec = pl.BlockSpec((s, LANES), lambda p, i: (0, p))
    ins = [q, k, v, do, lsum] + ([dk_prev, dv_prev] if has_prev else [])
    return pl.pallas_call(
        body, name="attn_bwd", grid=(d // LANES, nq),
        in_specs=[qspec, kspec, kspec, qspec, qspec] + ([kspec, kspec] if has_prev else []),
        out_specs=[qspec, kspec, kspec],
        out_shape=[jax.ShapeDtypeStruct((s, d), BF16), jax.ShapeDtypeStruct((s, d), F32), jax.ShapeDtypeStruct((s, d), F32)],
        scratch_shapes=[pltpu.VMEM((ATT_T, LANES), F32), pltpu.VMEM((ATT_T, LANES), F32)]
        + [pltpu.VMEM((ATT_T, 1), F32)] * 4,
        compiler_params=_params(("parallel", "arbitrary")),
    )(*ins)


def _place():
    x, y, c = lax.axis_index("x"), lax.axis_index("y"), lax.axis_index("c")
    chips = [(1 - x, y), (x, 1 - y), (1 - x, 1 - y)]
    return x, y, c, chips


def _any_specs(n):
    return [pl.BlockSpec(memory_space=pl.ANY)] * n


def _gather_weights(bufs):
    n = len(bufs)

    def body(*refs):
        outs = refs[n:2 * n]
        send_sems, recv_sems = refs[2 * n:]
        x, y, c, chips = _place()
        me = 2 * x + y
        sibling = (x, y, 1 - c)

        def half(a, blk, hc):
            h = outs[a].shape[1] // 2
            return outs[a].at[blk, pl.ds(hc * h, h)]

        def copy(a, k, part, to):
            return pltpu.make_async_remote_copy(src_ref=part, dst_ref=part, send_sem=send_sems.at[a, k],
                                                recv_sem=recv_sems.at[a, k], device_id=to, device_id_type=MESH)

        sent = []
        for a in range(n):
            for k, chip in enumerate(chips):
                sent.append(copy(a, k, half(a, me, c), (*chip, c)))
                sent[-1].start()
        for a in range(n):
            for k, chip in enumerate(chips):
                blk = 2 * chip[0] + chip[1]
                copy(a, k, half(a, blk, c), sibling).wait_recv()
                sent.append(copy(a, 3 + k, half(a, blk, c), sibling))
                sent[-1].start()
        for a in range(n):
            for k, chip in enumerate(chips):
                blk = 2 * chip[0] + chip[1]
                copy(a, 3 + k, half(a, blk, 1 - c), sibling).wait_recv()
        for cp in sent:
            cp.wait_send()

    return pl.pallas_call(
        body, name="gather_weights", in_specs=_any_specs(n), out_specs=_any_specs(n),
        out_shape=[jax.ShapeDtypeStruct(w.shape, w.dtype) for w in bufs],
        input_output_aliases={a: a for a in range(n)},
        scratch_shapes=[pltpu.SemaphoreType.DMA((n, 6)), pltpu.SemaphoreType.DMA((n, 6))],
        compiler_params=pltpu.CompilerParams(has_side_effects=True),
    )(*bufs)


def _pair_exchange(grads):
    n = len(grads)

    def body(*refs):
        ins, outs = refs[:n], refs[n:2 * n]
        send_sems, recv_sems = refs[2 * n:]
        x, y, c, _ = _place()
        cps = []
        for a in range(n):
            h = ins[a].shape[1] // 2
            cps.append(pltpu.make_async_remote_copy(
                src_ref=ins[a].at[:, pl.ds((1 - c) * h, h)], dst_ref=outs[a], send_sem=send_sems.at[a],
                recv_sem=recv_sems.at[a], device_id=(x, y, 1 - c), device_id_type=MESH))
            cps[-1].start()
        for cp in cps:
            cp.wait()

    return pl.pallas_call(
        body, name="pair_exchange", in_specs=_any_specs(n), out_specs=_any_specs(n),
        out_shape=[jax.ShapeDtypeStruct((g.shape[0], g.shape[1] // 2, g.shape[2]), g.dtype) for g in grads],
        scratch_shapes=[pltpu.SemaphoreType.DMA((n,)), pltpu.SemaphoreType.DMA((n,))],
        compiler_params=pltpu.CompilerParams(has_side_effects=True),
    )(*grads)


def _chip_exchange(parts):
    n = len(parts)

    def body(*refs):
        ins, outs = refs[:n], refs[n:2 * n]
        send_sems, recv_sems = refs[2 * n:]
        x, y, c, chips = _place()
        me = 2 * x + y
        cps = []
        for a in range(n):
            for k, chip in enumerate(chips):
                blk = 2 * chip[0] + chip[1]
                cps.append(pltpu.make_async_remote_copy(
                    src_ref=ins[a].at[blk], dst_ref=outs[a].at[me], send_sem=send_sems.at[a, k],
                    recv_sem=recv_sems.at[a, k], device_id=(*chip, c), device_id_type=MESH))
                cps[-1].start()
        for a in range(n):
            for k, chip in enumerate(chips):
                blk = 2 * chip[0] + chip[1]
                pltpu.make_async_remote_copy(
                    src_ref=ins[a].at[blk], dst_ref=outs[a].at[blk], send_sem=send_sems.at[a, k],
                    recv_sem=recv_sems.at[a, k], device_id=(*chip, c), device_id_type=MESH).wait_recv()
        for cp in cps:
            cp.wait_send()

    return pl.pallas_call(
        body, name="chip_exchange", in_specs=_any_specs(n), out_specs=_any_specs(n),
        out_shape=[jax.ShapeDtypeStruct(p.shape, p.dtype) for p in parts],
        scratch_shapes=[pltpu.SemaphoreType.DMA((n, 3)), pltpu.SemaphoreType.DMA((n, 3))],
        compiler_params=pltpu.CompilerParams(has_side_effects=True),
    )(*parts)


def _half_swap(halves):
    n = len(halves)

    def body(*refs):
        outs = refs[n:2 * n]
        send_sems, recv_sems = refs[2 * n:]
        x, y, c, _ = _place()
        cps = []
        for a in range(n):
            h = outs[a].shape[1] // 2
            mine = outs[a].at[:, pl.ds(c * h, h)]
            cps.append(pltpu.make_async_remote_copy(
                src_ref=mine, dst_ref=mine, send_sem=send_sems.at[a], recv_sem=recv_sems.at[a],
                device_id=(x, y, 1 - c), device_id_type=MESH))
            cps[-1].start()
        for cp in cps:
            cp.wait()

    return pl.pallas_call(
        body, name="half_swap", in_specs=_any_specs(n), out_specs=_any_specs(n),
        out_shape=[jax.ShapeDtypeStruct(p.shape, p.dtype) for p in halves],
        input_output_aliases={a: a for a in range(n)},
        scratch_shapes=[pltpu.SemaphoreType.DMA((n,)), pltpu.SemaphoreType.DMA((n,))],
        compiler_params=pltpu.CompilerParams(has_side_effects=True),
    )(*halves)


N_DEV = 8


def _all_reduce_small(v):
    nrow, ncol = v.shape

    def body(v_ref, o_ref, land, red, send_sems, recv_sems, send2, recv2, loc_sem):
        x, y, c, _ = _place()
        me = 4 * x + 2 * y + c
        peers = []
        for k in range(1, N_DEV):
            peers.append((x ^ ((k >> 2) & 1), y ^ ((k >> 1) & 1), c ^ (k & 1)))
        own = pltpu.make_async_copy(v_ref.at[pl.ds(me, 1)], land.at[pl.ds(me, 1)], loc_sem)
        own.start()
        cps = []
        for k, peer in enumerate(peers):
            dev = 4 * peer[0] + 2 * peer[1] + peer[2]
            cps.append(pltpu.make_async_remote_copy(
                src_ref=v_ref.at[pl.ds(dev, 1)], dst_ref=land.at[pl.ds(me, 1)], send_sem=send_sems.at[k],
                recv_sem=recv_sems.at[k], device_id=peer, device_id_type=MESH))
            cps[-1].start()
        for k, peer in enumerate(peers):
            dev = 4 * peer[0] + 2 * peer[1] + peer[2]
            pltpu.make_async_remote_copy(
                src_ref=v_ref.at[pl.ds(dev, 1)], dst_ref=land.at[pl.ds(dev, 1)], send_sem=send_sems.at[k],
                recv_sem=recv_sems.at[k], device_id=peer, device_id_type=MESH).wait_recv()
        for cp in cps:
            cp.wait_send()
        own.wait()
        terms = land[...]
        total = terms[0:1, :]
        for d in range(1, N_DEV):
            total = total + terms[d:d + 1, :]
        red[...] = total
        own = pltpu.make_async_copy(red, o_ref.at[pl.ds(me, 1)], loc_sem)
        own.start()
        cps = []
        for k, peer in enumerate(peers):
            cps.append(pltpu.make_async_remote_copy(
                src_ref=red, dst_ref=o_ref.at[pl.ds(me, 1)], send_sem=send2.at[k],
                recv_sem=recv2.at[k], device_id=peer, device_id_type=MESH))
            cps[-1].start()
        for k, peer in enumerate(peers):
            dev = 4 * peer[0] + 2 * peer[1] + peer[2]
            pltpu.make_async_remote_copy(
                src_ref=red, dst_ref=o_ref.at[pl.ds(dev, 1)], send_sem=send2.at[k],
                recv_sem=recv2.at[k], device_id=peer, device_id_type=MESH).wait_recv()
        for cp in cps:
            cp.wait_send()
        own.wait()

    vm = pl.BlockSpec(memory_space=pltpu.VMEM)
    return pl.pallas_call(
        body, name="all_reduce_small", in_specs=[vm], out_specs=vm,
        out_shape=jax.ShapeDtypeStruct((nrow, ncol), F32),
        scratch_shapes=[pltpu.VMEM((nrow, ncol), F32), pltpu.VMEM((1, ncol), F32)]
        + [pltpu.SemaphoreType.DMA((N_DEV - 1,))] * 4 + [pltpu.SemaphoreType.DMA],
        compiler_params=pltpu.CompilerParams(has_side_effects=True, vmem_limit_bytes=VMEM_LIMIT),
    )(v)


def _row_tile(rows):
    return min(rows, 512)


def _pair_sum(g, got, place):
    nb, r, c = g.shape
    h = r // 2
    tr = _row_tile(h)
    nt = h // tr

    def body(place_ref, g_ref, got_ref, p_ref, pb_ref):
        p = g_ref[...] + got_ref[...]
        pb_ref[...] = p.astype(BF16)

        @pl.when(pl.program_id(1) == place_ref[0])
        def _():
            p_ref[...] = p

    spec = pl.BlockSpec((None, tr, c), lambda t, j, place_ref: (j, t, 0))
    grid_spec = pltpu.PrefetchScalarGridSpec(
        num_scalar_prefetch=1, grid=(nt, nb),
        in_specs=[pl.BlockSpec((None, tr, c), lambda t, j, place_ref: (j, place_ref[1] * nt + t, 0)), spec],
        out_specs=[pl.BlockSpec((tr, c), lambda t, j, place_ref: (t, 0)), spec])
    return pl.pallas_call(
        body, name="pair_sum", grid_spec=grid_spec,
        out_shape=[jax.ShapeDtypeStruct((h, c), F32), jax.ShapeDtypeStruct((nb, h, c), BF16)],
        compiler_params=_params(("parallel", "arbitrary")),
    )(place, g, got)


def _chip_sum(p, got, place, out, layer):
    h, c = p.shape
    tr = _row_tile(h)
    nt = h // tr

    def body(place_ref, p_ref, g1_ref, g2_ref, g3_ref, old_ref, o_ref):
        o_ref[...] = ((p_ref[...] + g1_ref[...].astype(F32)) + g2_ref[...].astype(F32)) + g3_ref[...].astype(F32)

    def blk(off):
        return pl.BlockSpec((None, tr, c), lambda t, place_ref: ((place_ref[0] + off) % N_CHIPS, t, 0))

    grid_spec = pltpu.PrefetchScalarGridSpec(
        num_scalar_prefetch=1, grid=(nt,),
        in_specs=[pl.BlockSpec((tr, c), lambda t, place_ref: (t, 0)), blk(1), blk(2), blk(3),
                  pl.BlockSpec(memory_space=pl.ANY)],
        out_specs=pl.BlockSpec((None, tr, c), lambda t, place_ref: (layer, place_ref[1] * nt + t, 0)))
    return pl.pallas_call(
        body, name="chip_sum", grid_spec=grid_spec, out_shape=jax.ShapeDtypeStruct(out.shape, F32),
        input_output_aliases={5: 0}, compiler_params=_params(("parallel",)),
    )(place, p, got, got, got, out)


def _adamw(w, g, m, v):
    r, c = w.shape
    tr = r if r < 8 else _row_tile(r)

    def body(w_ref, g_ref, m_ref, v_ref, d_ref, nm_ref, nv_ref):
        gv = g_ref[...]
        nm = ADAM_B1 * m_ref[...] + (1.0 - ADAM_B1) * gv
        nv = ADAM_B2 * v_ref[...] + (1.0 - ADAM_B2) * (gv * gv)
        m_hat = nm / (1.0 - ADAM_B1 ** ADAM_STEP)
        v_hat = nv / (1.0 - ADAM_B2 ** ADAM_STEP)
        d_ref[...] = -ADAM_LR * (m_hat / (jnp.sqrt(v_hat) + ADAM_EPS) + ADAM_WD * w_ref[...])
        nm_ref[...] = nm
        nv_ref[...] = nv

    tile = pl.BlockSpec((tr, c), lambda i: (i, 0))
    return pl.pallas_call(
        body, name="adamw", grid=(r // tr,), in_specs=[tile] * 4, out_specs=[tile] * 3,
        out_shape=[jax.ShapeDtypeStruct((r, c), F32)] * 3, compiler_params=_params(("parallel",)),
    )(w, g, m, v)


BIG = ("a_w_in", "a_w_out", "sb_w_k", "sb_w_v", "b_w_q", "b_w_o", "ffn_w1", "ffn_w2")
SMALL = ("a_ln_g", "a_ln_b", "a_w_s", "a_b_s", "mix_ln_g", "mix_ln_b", "ffn_ln_g", "ffn_ln_b")
COL_SHARDED = {"a_w_in": True, "a_w_out": False, "sb_w_k": False, "sb_w_v": False, "b_w_q": False, "b_w_o": False,
               "ffn_w1": True, "ffn_w2": False}


def kernel(x, a_w_in, a_ln_g, a_ln_b, a_w_s, a_b_s, a_w_out, sb_w_k, sb_w_v, b_w_q, b_w_o, mix_ln_g, mix_ln_b, ffn_ln_g, ffn_ln_b, ffn_w1, ffn_w2, loss_target, m_a_w_in, m_a_ln_g, m_a_ln_b, m_a_w_s, m_a_b_s, m_a_w_out, m_sb_w_k, m_sb_w_v, m_b_w_q, m_b_w_o, m_mix_ln_g, m_mix_ln_b, m_ffn_ln_g, m_ffn_ln_b, m_ffn_w1, m_ffn_w2, v_a_w_in, v_a_ln_g, v_a_ln_b, v_a_w_s, v_a_b_s, v_a_w_out, v_sb_w_k, v_sb_w_v, v_b_w_q, v_b_w_o, v_mix_ln_g, v_mix_ln_b, v_ffn_ln_g, v_ffn_ln_b, v_ffn_w1, v_ffn_w2):
    names = BIG + SMALL
    given = dict(a_w_in=a_w_in, a_ln_g=a_ln_g, a_ln_b=a_ln_b, a_w_s=a_w_s, a_b_s=a_b_s, a_w_out=a_w_out, sb_w_k=sb_w_k,
                 sb_w_v=sb_w_v, b_w_q=b_w_q, b_w_o=b_w_o, mix_ln_g=mix_ln_g, mix_ln_b=mix_ln_b, ffn_ln_g=ffn_ln_g,
                 ffn_ln_b=ffn_ln_b, ffn_w1=ffn_w1, ffn_w2=ffn_w2)
    mom = dict(a_w_in=m_a_w_in, a_ln_g=m_a_ln_g, a_ln_b=m_a_ln_b, a_w_s=m_a_w_s, a_b_s=m_a_b_s, a_w_out=m_a_w_out,
               sb_w_k=m_sb_w_k, sb_w_v=m_sb_w_v, b_w_q=m_b_w_q, b_w_o=m_b_w_o, mix_ln_g=m_mix_ln_g, mix_ln_b=m_mix_ln_b,
               ffn_ln_g=m_ffn_ln_g, ffn_ln_b=m_ffn_ln_b, ffn_w1=m_ffn_w1, ffn_w2=m_ffn_w2)
    var = dict(a_w_in=v_a_w_in, a_ln_g=v_a_ln_g, a_ln_b=v_a_ln_b, a_w_s=v_a_w_s, a_b_s=v_a_b_s, a_w_out=v_a_w_out,
               sb_w_k=v_sb_w_k, sb_w_v=v_sb_w_v, b_w_q=v_b_w_q, b_w_o=v_b_w_o, mix_ln_g=v_mix_ln_g, mix_ln_b=v_mix_ln_b,
               ffn_ln_g=v_ffn_ln_g, ffn_ln_b=v_ffn_ln_b, ffn_w1=v_ffn_w1, ffn_w2=v_ffn_w2)

    cx, cy, cc = lax.axis_index("x"), lax.axis_index("y"), lax.axis_index("c")
    chip = (2 * cx + cy).astype(jnp.int32)
    chip_arr = chip.reshape(1)

    s, d = x.shape[1], x.shape[2]
    xf = x.reshape(s, d)
    target = loss_target.reshape(s, d)

    def as2d(w):
        return w.reshape(-1, w.shape[-1])

    gw = {}
    for n in BIG:
        for l in ([None] if given[n].ndim == 2 else range(given[n].shape[0])):
            gw[(n, l)] = _cast_into_slot(given[n], l, chip_arr)
    ln_gb = jnp.stack([a_ln_g, a_ln_b])
    ln_slot = lax.dynamic_update_slice(jnp.zeros((N_CHIPS,) + ln_gb.shape, F32), ln_gb[None], (chip, 0, 0, 0))
    layer0 = [("a_w_in", 0), ("a_w_out", 0), ("ffn_w1", 0), ("ffn_w2", 0)]
    gathered = _gather_weights([gw[k] for k in layer0] + [ln_slot])
    gw.update(zip(layer0, gathered[:-1]))
    mixer = {1: [("a_w_in", 1), ("a_w_out", 1)], 2: [("sb_w_k", None), ("sb_w_v", None), ("b_w_q", 0), ("b_w_o", 0)],
             3: [("b_w_q", 1), ("b_w_o", 1)]}

    def riding(d2d=(), ici=()):
        keys = list(d2d) + list(ici)
        return keys, [("d2d", gw[k]) for k in d2d] + [("ici", gw[k]) for k in ici]

    def landed_in(keys, bufs):
        gw.update(zip(keys, bufs))

    ln_full = gathered[-1].transpose(1, 2, 0, 3).reshape(2, N_A, 1, -1)
    a_ln_g3, a_ln_b3 = ln_full[0], ln_full[1]
    mix_g3, mix_b3 = mix_ln_g[:, None, :], mix_ln_b[:, None, :]
    ffn_g3, ffn_b3 = ffn_ln_g[:, None, :], ffn_ln_b[:, None, :]
    bst = jnp.swapaxes(a_b_s, 1, 2)

    saved = []
    xb = _cast_bf16(xf)
    kb = vb = None
    for l in range(DEPTH):
        sv = dict(x_in=xb)
        last = l == DEPTH - 1
        keys, riders = riding(d2d=[("ffn_w1", l), ("ffn_w2", l)] if l else [], ici=mixer[l + 1] if l < N_A else [])
        if l < N_A:
            h, *bufs = _mm_fwd("a_in", xb, gw[("a_w_in", l)], None, True, riders=riders)
            landed_in(keys, bufs)
            vn = _gmlp_norm_fwd(h, a_ln_g3, a_ln_b3, l)
            gated = _gate_fwd(h, vn, a_w_s[l], bst[l])
            xf, xb, xhat, rstd = _mm_resid_ln("a_out", gated, gw[("a_w_out", l)], xf, mix_g3, mix_b3, l)
            sv.update(h=h, vn=vn, gated=gated)
        else:
            j = l - N_A
            if l == N_A:
                kb, *bufs = _mm_fwd("sb_k", xb, gw[("sb_w_k", None)], None, False, _ep_bf16, outs=[(d, BF16)],
                                    riders=riders)
                landed_in(keys, bufs)
                keys, riders = [], ()
                vb = _mm_fwd("sb_v", xb, gw[("sb_w_v", None)], None, False, _ep_bf16, outs=[(d, BF16)])[0]
            q, *bufs = _mm_fwd("b_q", xb, gw[("b_w_q", j)], None, False, _ep_scale_q, outs=[(d, BF16)], riders=riders)
            landed_in(keys, bufs)
            ob, lsum = _attn_fwd(q, kb, vb)
            keys, riders = riding(ici=[] if last else mixer[l + 1])
            xf, xb, xhat, rstd, *bufs = _mm_resid_ln("b_out", ob, gw[("b_w_o", j)], xf, mix_g3, mix_b3, l, riders)
            landed_in(keys, bufs)
            sv.update(q=q, lsum=lsum, ob=ob)
        sv.update(x_mid=xb, xhat1=xhat, rstd1=rstd)
        dff = gw[("ffn_w1", l)].shape[-1] * N_CHIPS
        keys, riders = riding(ici=[] if last else [("ffn_w1", l + 1)])
        pr, act, *bufs = _mm_fwd("ffn_1", xb, gw[("ffn_w1", l)], None, True, _ep_relu2,
                                 outs=[(dff, BF16), (dff, BF16)], riders=riders)
        landed_in(keys, bufs)
        keys, riders = riding(d2d=[] if last else mixer[l + 1], ici=[] if last else [("ffn_w2", l + 1)])
        xf, xb, xhat, rstd, *bufs = _mm_resid_ln("ffn_2", act, gw[("ffn_w2", l)], xf, ffn_g3, ffn_b3, l, riders)
        landed_in(keys, bufs)
        sv.update(pr=pr, act=act, xhat2=xhat, rstd2=rstd)
        saved.append(sv)

    dx, sq = _loss_head(xf, target)
    loss = lax.psum(0.5 * sq[0, 0] / d, ("x", "y", "c"))

    pending = []
    pair_sums, landed = {}, {}
    place_arr = jnp.stack([chip, cc.astype(jnp.int32)])
    ffn = ("ffn_w1", "ffn_w2")

    def arrived(took, outs):
        for (kind, key, arr), out in zip(took, outs):
            if kind == "pair":
                pair_sums[key] = _pair_sum(arr, out, place_arr)
                pending.append(("chip", key, pair_sums[key][1]))
            else:
                landed[key] = out

    def carrying(call, name, *args, **kw):
        took, room = [], name.startswith("ffn")
        for task in list(pending):
            heavy = task[0] == "chip" and task[1][0] in ffn
            if not heavy or room:
                took.append(task)
                pending.remove(task)
                room = room and not heavy
        out, *rest = call(name, *args, riders=[(kind, arr) for kind, _, arr in took], **kw)
        arrived(took, rest)
        return out

    def bwd_act(*args, **kw):
        return carrying(_mm_bwd_act, *args, **kw)

    def bwd_w(key, name, a, dy):
        pending.append(("pair", key, carrying(_mm_bwd_w, name, a, dy, gw[key], COL_SHARDED[key[0]])))

    d_mix_g, d_mix_b, d_ffn_g, d_ffn_b = [None] * DEPTH, [None] * DEPTH, [None] * DEPTH, [None] * DEPTH
    d_ln_g, d_ln_b, d_ws, d_bs = [None] * N_A, [None] * N_A, [None] * N_A, [None] * N_A
    dk = dv = None
    for l in reversed(range(DEPTH)):
        sv = saved[l]
        dr, drb, d_ffn_g[l], d_ffn_b[l] = _ln_bwd(dx, sv["xhat2"], sv["rstd2"], ffn_g3, l)
        dff = sv["pr"].shape[1]
        dhd = bwd_act("ffn_2_dx", drb, gw[("ffn_w2", l)], None, False, _ep_relu2_bwd, (sv["pr"],),
                      (pl.BlockSpec((TM, dff // N_CHIPS), lambda j, i, k: (i, j)),), out_dtype=BF16)
        bwd_w(("ffn_w2", l), "ffn_2_dw", sv["act"], drb)
        dx = bwd_act("ffn_1_dx", dhd, gw[("ffn_w1", l)], None, True, _ep_resid, (dr,), (_row_spec(d),))
        bwd_w(("ffn_w1", l), "ffn_1_dw", sv["x_mid"], dhd)

        dr, drb, d_mix_g[l], d_mix_b[l] = _ln_bwd(dx, sv["xhat1"], sv["rstd1"], mix_g3, l)
        quarter = pl.BlockSpec((TM, d // N_CHIPS), lambda j, i, k: (i, j))
        if l < N_A:
            dgated = bwd_act("a_out_dx", drb, gw[("a_w_out", l)], None, False)
            bwd_w(("a_w_out", l), "a_out_dw", sv["gated"], drb)
            du, dvn, d_ws[l], dbs_wide = _gate_bwd(dgated, sv["h"], sv["vn"], a_w_s[l], bst[l])
            d_bs[l] = dbs_wide[:, :, 0]
            dh, dlg, dlb = _gmlp_in_bwd(sv["h"], du, dvn, a_ln_g3, l)
            d_ln_g[l], d_ln_b[l] = dlg[0], dlb[0]
            dx = bwd_act("a_in_dx", dh, gw[("a_w_in", l)], None, True, _ep_resid, (dr,), (_row_spec(d),))
            bwd_w(("a_w_in", l), "a_in_dw", sv["x_in"], dh)
        else:
            j = l - N_A
            do = bwd_act("b_out_dx", drb, gw[("b_w_o", j)], None, False)
            bwd_w(("b_w_o", j), "b_out_dw", sv["ob"], drb)
            dq, dk, dv = _attn_bwd(sv["q"], kb, vb, do, sv["lsum"], dk, dv)
            dx = bwd_act("b_q_dx", dq, gw[("b_w_q", j)], None, False, _ep_resid, (dr,), (quarter,))
            bwd_w(("b_w_q", j), "b_q_dw", sv["x_in"], dq)
            if l == N_A:
                dx = bwd_act("sb_k_dx", dk, gw[("sb_w_k", None)], None, False, _ep_add, (dx,), (quarter,))
                bwd_w(("sb_w_k", None), "sb_k_dw", sv["x_in"], dk)
                dx = bwd_act("sb_v_dx", dv, gw[("sb_w_v", None)], None, False, _ep_add, (dx,), (quarter,))
                bwd_w(("sb_w_v", None), "sb_v_dw", sv["x_in"], dv)
    grad_x = dx.reshape(x.shape)

    while pending:
        took = list(pending)
        pending.clear()
        for kind, exchange in (("pair", _pair_exchange), ("chip", _chip_exchange)):
            some = [t for t in took if t[0] == kind]
            if some:
                arrived(some, exchange([arr for _, _, arr in some]))

    stacked = []
    for n in BIG:
        layers = [None] if given[n].ndim == 2 else range(given[n].shape[0])
        out = lax.empty((len(layers),) + given[n].shape[-2:], F32)
        for at, l in enumerate(layers):
            out = _chip_sum(pair_sums[(n, l)][0], landed[(n, l)], place_arr, out, at)
        stacked.append(out)
    grads = {n: g.reshape(given[n].shape) for n, g in zip(BIG, _half_swap(stacked))}

    small_full = dict(a_ln_g=jnp.stack(d_ln_g), a_ln_b=jnp.stack(d_ln_b), a_w_s=jnp.stack(d_ws), a_b_s=jnp.stack(d_bs),
                      mix_ln_g=jnp.concatenate(d_mix_g), mix_ln_b=jnp.concatenate(d_mix_b),
                      ffn_ln_g=jnp.concatenate(d_ffn_g), ffn_ln_b=jnp.concatenate(d_ffn_b))
    packed = jnp.concatenate([small_full[n].reshape(-1) for n in SMALL])
    total = packed.shape[0]
    ncol = -(-total // (N_DEV * LANES)) * LANES
    packed = jnp.pad(packed, (0, N_DEV * ncol - total)).reshape(N_DEV, ncol)
    reduced = _all_reduce_small(packed).reshape(-1)
    off = 0
    for n in SMALL:
        size = small_full[n].size
        g = reduced[off:off + size].reshape(small_full[n].shape)
        off += size
        if n in ("a_ln_g", "a_ln_b"):
            wq = given[n].shape[1]
            g = lax.dynamic_slice_in_dim(g, chip * wq, wq, axis=1)
        grads[n] = g

    delta, new_m, new_v = {}, {}, {}
    for n in names:
        shape = given[n].shape
        dl, nm, nv = _adamw(as2d(given[n]), as2d(grads[n]), as2d(mom[n]), as2d(var[n]))
        delta[n], new_m[n], new_v[n] = dl.reshape(shape), nm.reshape(shape), nv.reshape(shape)

    order = ("a_w_in", "a_ln_g", "a_ln_b", "a_w_s", "a_b_s", "a_w_out", "sb_w_k", "sb_w_v", "b_w_q", "b_w_o",
             "mix_ln_g", "mix_ln_b", "ffn_ln_g", "ffn_ln_b", "ffn_w1", "ffn_w2")
    return (loss, grad_x, *[grads[n] for n in order], *[delta[n] for n in order],
            *[new_m[n] for n in order], *[new_v[n] for n in order])
```

```python
import math

import jax
import jax.numpy as jnp
from jax import lax
from jax.experimental import pallas as pl
from jax.experimental.pallas import tpu as pltpu

F32 = jnp.float32
BF16 = jnp.bfloat16
MESH = pl.DeviceIdType.MESH

N_CHIPS = 4
DEPTH = 4
N_A = 2
ALPHA = float((2 * DEPTH) ** 0.25)
LN_EPS = 1e-5
CHUNK = 64
GMLP_BLOCK = 128
GMLP_GROUPS = 8
HEAD_DIM = 64
LANES = 128
ATT_T = 256
ADAM_LR = 0.001
ADAM_B1 = 0.9
ADAM_B2 = 0.999
ADAM_EPS = 1e-08
ADAM_WD = 0.01
ADAM_STEP = 10
VMEM_LIMIT = 56 * 1024 * 1024
TM = 512
TS = 1024

NN = ((1,), (0,))
NT = ((1,), (1,))
TN = ((0,), (0,))


def _params(sem):
    return pltpu.CompilerParams(dimension_semantics=sem, vmem_limit_bytes=VMEM_LIMIT)


def _dot(a, b, contract):
    return lax.dot_general(a, b, (contract, ((), ())), preferred_element_type=F32)


def _rider_out(kind, arr):
    shape = (arr.shape[0], arr.shape[1] // 2, arr.shape[2]) if kind == "pair" else arr.shape
    return jax.ShapeDtypeStruct(shape, arr.dtype)


def _rider_copies(kind, src, dst, send_sems, recv_sems, base):
    x, y, c, chips = _place()
    me = 2 * x + y
    sibling = (x, y, 1 - c)

    def copy(k, part, land, to):
        return pltpu.make_async_remote_copy(src_ref=part, dst_ref=land, send_sem=send_sems.at[base + k],
                                            recv_sem=recv_sems.at[base + k], device_id=to, device_id_type=MESH)

    if kind == "pair":
        h = src.shape[1] // 2
        cp = copy(0, src.at[:, pl.ds((1 - c) * h, h)], dst, sibling)
        return [cp], [cp]
    h = dst.shape[1] // 2
    starts, arrivals = [], []
    for k, chip in enumerate(chips):
        blk = 2 * chip[0] + chip[1]
        if kind == "ici":
            starts.append(copy(k, dst.at[me, pl.ds(c * h, h)], dst.at[me, pl.ds(c * h, h)], (*chip, c)))
            arrivals.append(copy(k, dst.at[blk, pl.ds(c * h, h)], dst.at[blk, pl.ds(c * h, h)], (*chip, c)))
        elif kind == "d2d":
            starts.append(copy(k, dst.at[blk, pl.ds(c * h, h)], dst.at[blk, pl.ds(c * h, h)], sibling))
            arrivals.append(copy(k, dst.at[blk, pl.ds((1 - c) * h, h)], dst.at[blk, pl.ds((1 - c) * h, h)], sibling))
        else:
            starts.append(copy(k, src.at[blk], dst.at[me], (*chip, c)))
            arrivals.append(copy(k, src.at[blk], dst.at[blk], (*chip, c)))
    return starts, arrivals


RIDER_SEMS = 3


def _matmul(name, operands, in_specs, out_shapes, out_specs, grid, contract, epilogue, acc_shape, aliases=None,
            chunks=None, riders=()):
    nk = grid[2]
    n_in, n_out, nr = len(operands), len(out_shapes), len(riders)
    n_plain = n_in + nr + n_out

    def body(*refs):
        ins, outs = refs[:n_in], refs[n_in + nr:n_plain]
        if nr:
            srcs, dsts = refs[n_in:n_in + nr], refs[n_plain:n_plain + nr]
            send_sems, recv_sems = refs[-2:]
            pid = [pl.program_id(ax) for ax in range(3)]
            first = (pid[0] == 0) & (pid[1] == 0) & (pid[2] == 0)
            last = (pid[0] == grid[0] - 1) & (pid[1] == grid[1] - 1) & (pid[2] == grid[2] - 1)

            def copies(n):
                return _rider_copies(riders[n][0], srcs[n], dsts[n], send_sems, recv_sems, RIDER_SEMS * n)

            @pl.when(first)
            def _():
                for n in range(nr):
                    for cp in copies(n)[0]:
                        cp.start()

        compute(refs, ins, outs)
        if nr:
            @pl.when(last)
            def _():
                for n in range(nr):
                    starts, arrivals = copies(n)
                    for cp in arrivals:
                        cp.wait_recv()
                    for cp in starts:
                        cp.wait_send()

    def compute(refs, ins, outs):
        if chunks is None:
            p = _dot(ins[0][...].astype(BF16), ins[1][...].astype(BF16), contract)
        else:
            width = ins[0].shape[1] // chunks
            p = None
            for j in range(chunks):
                pj = _dot(ins[0][:, j * width:(j + 1) * width].astype(BF16), ins[1][j].astype(BF16), contract)
                p = pj if p is None else p + pj
        if nk == 1:
            epilogue(p, ins[2:], outs)
            return
        acc = refs[n_plain + nr]
        k = pl.program_id(2)

        @pl.when(k == 0)
        def _():
            acc[...] = p

        @pl.when((k > 0) & (k < nk - 1))
        def _():
            acc[...] += p

        @pl.when(k == nk - 1)
        def _():
            epilogue(acc[...] + p, ins[2:], outs)

    rbufs = [b for _, b in riders]
    in_place = {n_in + n: n_out + n for n, (kind, _) in enumerate(riders) if kind in ("ici", "d2d")}
    scratch = ([] if nk == 1 else [pltpu.VMEM(acc_shape, F32)]) \
        + [pltpu.SemaphoreType.DMA((RIDER_SEMS * nr,))] * (2 if nr else 0)
    return pl.pallas_call(
        body, name=name, grid=grid, in_specs=list(in_specs) + _any_specs(nr), out_specs=list(out_specs) + _any_specs(nr),
        out_shape=list(out_shapes) + [_rider_out(kind, b) for kind, b in riders],
        scratch_shapes=scratch,
        input_output_aliases={**(aliases or {}), **in_place},
        compiler_params=_params(("arbitrary",) * 3 if nr else ("parallel", "parallel", "arbitrary")),
    )(*operands, *rbufs)


def _wspec(w, layer, whole=False):
    r, c = w.shape[-2:]
    lead = N_CHIPS if whole else None
    if w.ndim == 4:
        return pl.BlockSpec((lead, None, r, c), lambda j, i, k: (0 if whole else j, layer, 0, 0))
    return pl.BlockSpec((lead, r, c), lambda j, i, k: (0 if whole else j, 0, 0))


def _ep_store(p, ins, outs):
    for o in outs:
        o[...] = p.astype(o.dtype)


def _mm_fwd(name, a, w, layer, col_sharded, epilogue=_ep_store, extras=(), extra_specs=(), outs=None, riders=()):
    s = a.shape[0]
    r, c = w.shape[-2:]
    if col_sharded:
        grid = (N_CHIPS, s // TM, 1)
        a_spec = pl.BlockSpec((TM, r), lambda j, i, k: (i, 0))
        n_out = N_CHIPS * c
    else:
        grid = (1, s // TM, 1)
        a_spec = pl.BlockSpec((TM, N_CHIPS * r), lambda j, i, k: (i, 0))
        n_out = c
    if outs is None:
        outs = [(n_out, F32)]
    out_shapes = [jax.ShapeDtypeStruct((s, n), dt) for n, dt in outs]
    out_specs = [pl.BlockSpec((TM, c if n == n_out else n), lambda j, i, k: (i, j)) for n, _ in outs]
    return _matmul(name, (a, w) + tuple(extras), [a_spec, _wspec(w, layer, not col_sharded)] + list(extra_specs),
                   out_shapes, out_specs, grid, NN, epilogue, (TM, c), chunks=None if col_sharded else N_CHIPS,
                   riders=riders)


def _mm_bwd_act(name, dy, w, layer, col_sharded, epilogue=_ep_store, extras=(), extra_specs=(), out_dtype=F32,
                riders=()):
    s = dy.shape[0]
    r, c = w.shape[-2:]
    if col_sharded:
        grid = (1, s // TM, 1)
        a_spec = pl.BlockSpec((TM, N_CHIPS * c), lambda j, i, k: (i, 0))
        n_out = r
    else:
        grid = (N_CHIPS, s // TM, 1)
        a_spec = pl.BlockSpec((TM, c), lambda j, i, k: (i, 0))
        n_out = N_CHIPS * r
    o_spec = pl.BlockSpec((TM, r), lambda j, i, k: (i, j))
    return _matmul(name, (dy, w) + tuple(extras), [a_spec, _wspec(w, layer, col_sharded)] + list(extra_specs),
                   [jax.ShapeDtypeStruct((s, n_out), out_dtype)], [o_spec], grid, NT, epilogue, (TM, r),
                   chunks=N_CHIPS if col_sharded else None, riders=riders)


def _mm_bwd_w(name, a, dy, w, col_sharded, riders=()):
    s = a.shape[0]
    r, c = w.shape[-2:]
    ts = min(TS, s)
    grid = (N_CHIPS, 1, s // ts)
    if col_sharded:
        a_spec = pl.BlockSpec((ts, r), lambda j, i, k: (k, 0))
        b_spec = pl.BlockSpec((ts, c), lambda j, i, k: (k, j))
    else:
        a_spec = pl.BlockSpec((ts, r), lambda j, i, k: (k, j))
        b_spec = pl.BlockSpec((ts, c), lambda j, i, k: (k, 0))

    def epilogue(p, ins, outs):
        outs[0][...] = p

    return _matmul(name, (a, dy), [a_spec, b_spec], [jax.ShapeDtypeStruct(w.shape, F32)], [_wspec(w, None)], grid, TN,
                   epilogue, (r, c), riders=riders)


def _row_spec(n):
    return pl.BlockSpec((TM, n), lambda j, i, k: (i, 0))


def _vec_spec(layer, n):
    return pl.BlockSpec((None, 1, n), lambda j, i, k: (layer, 0, 0))


def _ep_resid_ln(p, ins, outs):
    x_ref, g_ref, b_ref = ins
    xf_ref, xb_ref, xhat_ref, rstd_ref = outs
    r = ALPHA * x_ref[...] + p
    mu = jnp.mean(r, axis=-1, keepdims=True)
    d = r - mu
    var = jnp.mean(d * d, axis=-1, keepdims=True)
    rstd = lax.rsqrt(var + LN_EPS)
    xhat = d * rstd
    y = xhat * g_ref[...] + b_ref[...]
    xf_ref[...] = y
    xb_ref[...] = y.astype(BF16)
    xhat_ref[...] = xhat
    rstd_ref[...] = rstd


def _mm_resid_ln(name, a, w, x, g3, b3, ln_layer, riders=()):
    d = x.shape[1]
    return _mm_fwd(name, a, w, None, False, _ep_resid_ln, (x, g3, b3),
                   (_row_spec(d), _vec_spec(ln_layer, d), _vec_spec(ln_layer, d)),
                   outs=[(d, F32), (d, BF16), (d, F32), (1, F32)], riders=riders)


def _ep_relu2(p, ins, outs):
    h = jnp.maximum(p, 0.0)
    outs[0][...] = h.astype(BF16)
    outs[1][...] = (h * h).astype(BF16)


def _ep_scale_q(p, ins, outs):
    outs[0][...] = (p * (HEAD_DIM ** -0.5)).astype(BF16)


def _ep_bf16(p, ins, outs):
    outs[0][...] = p.astype(BF16)


def _ep_relu2_bwd(p, ins, outs):
    outs[0][...] = (p * (2.0 * ins[0][...].astype(F32))).astype(BF16)


def _ep_resid(p, ins, outs):
    outs[0][...] = ALPHA * ins[0][...] + p


def _ep_add(p, ins, outs):
    outs[0][...] = ins[0][...] + p


def _gelu_grad(x):
    c0 = math.sqrt(2.0 / math.pi)
    t = jnp.tanh(c0 * (x + 0.044715 * (x * x * x)))
    return 0.5 * (1.0 + t) + (0.5 * x) * (1.0 - t * t) * (c0 * (1.0 + 3.0 * 0.044715 * (x * x)))


def _cast_bf16(w2d):
    r, c = w2d.shape
    tr = min(r, 512)

    def body(w_ref, o_ref):
        o_ref[...] = w_ref[...].astype(BF16)

    return pl.pallas_call(
        body, name="cast_bf16", grid=(r // tr,),
        in_specs=[pl.BlockSpec((tr, c), lambda i: (i, 0))], out_specs=pl.BlockSpec((tr, c), lambda i: (i, 0)),
        out_shape=jax.ShapeDtypeStruct((r, c), BF16), compiler_params=_params(("parallel",)),
    )(w2d)


def _cast_into_slot(w, layer, chip):
    r, c = w.shape[-2:]
    tr = min(r, 512)

    def body(chip_ref, w_ref, o_ref):
        o_ref[...] = w_ref[...].astype(BF16)

    if layer is None:
        w_spec = pl.BlockSpec((tr, c), lambda i, chip_ref: (i, 0))
    else:
        w_spec = pl.BlockSpec((None, tr, c), lambda i, chip_ref: (layer, i, 0))
    grid_spec = pltpu.PrefetchScalarGridSpec(
        num_scalar_prefetch=1, grid=(r // tr,), in_specs=[w_spec],
        out_specs=pl.BlockSpec((None, tr, c), lambda i, chip_ref: (chip_ref[0], i, 0)))
    return pl.pallas_call(
        body, name="cast_into_slot", grid_spec=grid_spec,
        out_shape=jax.ShapeDtypeStruct((N_CHIPS, r, c), BF16), compiler_params=_params(("parallel",)),
    )(chip, w)


def _gmlp_norm_fwd(h, g3, b3, layer):
    s, w2 = h.shape
    w = w2 // 2

    def body(h_ref, g_ref, b_ref, o_ref):
        z = jax.nn.gelu(h_ref[...])
        mu = jnp.mean(z, axis=-1, keepdims=True)
        d = z - mu
        var = jnp.mean(d * d, axis=-1, keepdims=True)
        o_ref[...] = (d * lax.rsqrt(var + LN_EPS) * g_ref[...] + b_ref[...]).astype(BF16)

    vec = pl.BlockSpec((None, 1, w), lambda i: (layer, 0, 0))
    return pl.pallas_call(
        body, name="gmlp_norm_fwd", grid=(s // TM,),
        in_specs=[pl.BlockSpec((TM, w), lambda i: (i, 1)), vec, vec],
        out_specs=pl.BlockSpec((TM, w), lambda i: (i, 0)),
        out_shape=jax.ShapeDtypeStruct((s, w), BF16), compiler_params=_params(("parallel",)),
    )(h, g3, b3)


def _chunk_mask():
    t = lax.broadcasted_iota(jnp.int32, (GMLP_BLOCK, GMLP_BLOCK), 0)
    s = lax.broadcasted_iota(jnp.int32, (GMLP_BLOCK, GMLP_BLOCK), 1)
    return (s // CHUNK) <= (t // CHUNK)


SG_ROWS = 512


def _gate_fwd(h, vn, ws, bst):
    s, w = vn.shape
    gd = w // GMLP_GROUPS

    def body(h_ref, v_ref, ws_ref, bs_ref, o_ref):
        mask = _chunk_mask()
        for g in range(GMLP_GROUPS):
            wm = jnp.where(mask, ws_ref[g], 0.0).astype(BF16)
            bias = bs_ref[:, g:g + 1]
            cols = slice(g * gd, (g + 1) * gd)
            for n in range(SG_ROWS // GMLP_BLOCK):
                rows = slice(n * GMLP_BLOCK, (n + 1) * GMLP_BLOCK)
                sp = _dot(wm, v_ref[rows, cols], NN) + bias
                o_ref[rows, cols] = (jax.nn.gelu(h_ref[rows, cols]) * sp).astype(BF16)

    return pl.pallas_call(
        body, name="gate_fwd", grid=(s // SG_ROWS,),
        in_specs=[pl.BlockSpec((SG_ROWS, w), lambda i: (i, 0)), pl.BlockSpec((SG_ROWS, w), lambda i: (i, 0)),
                  pl.BlockSpec(ws.shape, lambda i: (0, 0, 0)), pl.BlockSpec(bst.shape, lambda i: (0, 0))],
        out_specs=pl.BlockSpec((SG_ROWS, w), lambda i: (i, 0)),
        out_shape=jax.ShapeDtypeStruct((s, w), BF16), compiler_params=_params(("parallel",)),
    )(h, vn, ws, bst)


def _gate_bwd(dgated, h, vn, ws, bst):
    s, w = vn.shape
    gd = w // GMLP_GROUPS
    nsteps = s // SG_ROWS

    def body(dg_ref, h_ref, v_ref, ws_ref, bs_ref, du_ref, dv_ref, dws_ref, dbs_ref, dsum):
        i = pl.program_id(0)

        @pl.when(i == 0)
        def _():
            dws_ref[...] = jnp.zeros_like(dws_ref)
            dsum[...] = jnp.zeros_like(dsum)

        mask = _chunk_mask()
        for g in range(GMLP_GROUPS):
            wm = jnp.where(mask, ws_ref[g], 0.0).astype(BF16)
            bias = bs_ref[:, g:g + 1]
            cols = slice(g * gd, (g + 1) * gd)
            dw = jnp.zeros((GMLP_BLOCK, GMLP_BLOCK), F32)
            dsg = jnp.zeros((GMLP_BLOCK, gd), F32)
            for n in range(SG_ROWS // GMLP_BLOCK):
                rows = slice(n * GMLP_BLOCK, (n + 1) * GMLP_BLOCK)
                vb = v_ref[rows, cols]
                sp = _dot(wm, vb, NN) + bias
                dg = dg_ref[rows, cols]
                du_ref[rows, cols] = dg * sp
                ds = dg * jax.nn.gelu(h_ref[rows, cols])
                dsb = ds.astype(BF16)
                dw += _dot(dsb, vb, NT)
                dsg += ds
                dv_ref[rows, cols] = _dot(wm, dsb, TN)
            dws_ref[g] += dw
            dsum[:, cols] += dsg

        @pl.when(i == nsteps - 1)
        def _():
            for g in range(GMLP_GROUPS):
                dws_ref[g] = jnp.where(mask, dws_ref[g], 0.0)
                tot = jnp.sum(dsum[:, g * gd:(g + 1) * gd], axis=-1, keepdims=True)
                dbs_ref[g] = jnp.broadcast_to(tot, (GMLP_BLOCK, LANES))

    tile = pl.BlockSpec((SG_ROWS, w), lambda i: (i, 0))
    return pl.pallas_call(
        body, name="gate_bwd", grid=(nsteps,),
        in_specs=[tile, tile, tile, pl.BlockSpec(ws.shape, lambda i: (0, 0, 0)), pl.BlockSpec(bst.shape, lambda i: (0, 0))],
        out_specs=[tile, tile, pl.BlockSpec(ws.shape, lambda i: (0, 0, 0)),
                   pl.BlockSpec((GMLP_GROUPS, GMLP_BLOCK, LANES), lambda i: (0, 0, 0))],
        out_shape=[jax.ShapeDtypeStruct((s, w), F32), jax.ShapeDtypeStruct((s, w), F32),
                   jax.ShapeDtypeStruct(ws.shape, F32), jax.ShapeDtypeStruct((GMLP_GROUPS, GMLP_BLOCK, LANES), F32)],
        scratch_shapes=[pltpu.VMEM((GMLP_BLOCK, w), F32)],
        compiler_params=_params(("arbitrary",)),
    )(dgated, h, vn, ws, bst)


GB_ROWS = 256


def _gmlp_in_bwd(h, du, dvn, g3, layer):
    s, w2 = h.shape
    w = w2 // 2
    nsteps = s // GB_ROWS

    def body(h_ref, du_ref, dv_ref, g_ref, dh_ref, dg_ref, db_ref):
        i = pl.program_id(0)

        @pl.when(i == 0)
        def _():
            dg_ref[...] = jnp.zeros_like(dg_ref)
            db_ref[...] = jnp.zeros_like(db_ref)

        hu = h_ref[:, :w]
        hv = h_ref[:, w:]
        dh_ref[:, :w] = (du_ref[...] * _gelu_grad(hu)).astype(BF16)
        z = jax.nn.gelu(hv)
        mu = jnp.mean(z, axis=-1, keepdims=True)
        d = z - mu
        var = jnp.mean(d * d, axis=-1, keepdims=True)
        rstd = lax.rsqrt(var + LN_EPS)
        xhat = d * rstd
        dy = dv_ref[...]
        db_ref[...] += jnp.sum(dy, axis=0, keepdims=True)
        dg_ref[...] += jnp.sum(dy * xhat, axis=0, keepdims=True)
        dxh = dy * g_ref[...]
        m1 = jnp.mean(dxh, axis=-1, keepdims=True)
        m2 = jnp.mean(dxh * xhat, axis=-1, keepdims=True)
        dz = rstd * (dxh - m1 - xhat * m2)
        dh_ref[:, w:] = (dz * _gelu_grad(hv)).astype(BF16)

    half = pl.BlockSpec((GB_ROWS, w), lambda i: (i, 0))
    vec = pl.BlockSpec((1, w), lambda i: (0, 0))
    return pl.pallas_call(
        body, name="gmlp_in_bwd", grid=(nsteps,),
        in_specs=[pl.BlockSpec((GB_ROWS, w2), lambda i: (i, 0)), half, half,
                  pl.BlockSpec((None, 1, w), lambda i: (layer, 0, 0))],
        out_specs=[pl.BlockSpec((GB_ROWS, w2), lambda i: (i, 0)), vec, vec],
        out_shape=[jax.ShapeDtypeStruct((s, w2), BF16), jax.ShapeDtypeStruct((1, w), F32), jax.ShapeDtypeStruct((1, w), F32)],
        compiler_params=_params(("arbitrary",)),
    )(h, du, dvn, g3)


def _ln_bwd(dy, xhat, rstd, g3, layer):
    s, d = dy.shape
    nsteps = s // TM

    def body(dy_ref, xh_ref, rs_ref, g_ref, dr_ref, drb_ref, dg_ref, db_ref):
        i = pl.program_id(0)

        @pl.when(i == 0)
        def _():
            dg_ref[...] = jnp.zeros_like(dg_ref)
            db_ref[...] = jnp.zeros_like(db_ref)

        dyv = dy_ref[...]
        xhat_v = xh_ref[...]
        db_ref[...] += jnp.sum(dyv, axis=0, keepdims=True)
        dg_ref[...] += jnp.sum(dyv * xhat_v, axis=0, keepdims=True)
        dxh = dyv * g_ref[...]
        m1 = jnp.mean(dxh, axis=-1, keepdims=True)
        m2 = jnp.mean(dxh * xhat_v, axis=-1, keepdims=True)
        dr = rs_ref[...] * (dxh - m1 - xhat_v * m2)
        dr_ref[...] = dr
        drb_ref[...] = dr.astype(BF16)

    tile = pl.BlockSpec((TM, d), lambda i: (i, 0))
    vec = pl.BlockSpec((1, d), lambda i: (0, 0))
    return pl.pallas_call(
        body, name="ln_bwd", grid=(nsteps,),
        in_specs=[tile, tile, pl.BlockSpec((TM, 1), lambda i: (i, 0)), pl.BlockSpec((None, 1, d), lambda i: (layer, 0, 0))],
        out_specs=[tile, tile, vec, vec],
        out_shape=[jax.ShapeDtypeStruct((s, d), F32), jax.ShapeDtypeStruct((s, d), BF16),
                   jax.ShapeDtypeStruct((1, d), F32), jax.ShapeDtypeStruct((1, d), F32)],
        compiler_params=_params(("arbitrary",)),
    )(dy, xhat, rstd, g3)


def _loss_head(y, target):
    s, d = y.shape

    def body(y_ref, t_ref, dy_ref, l_ref):
        i = pl.program_id(0)

        @pl.when(i == 0)
        def _():
            l_ref[...] = jnp.zeros_like(l_ref)

        e = y_ref[...] - t_ref[...]
        dy_ref[...] = e * (1.0 / d)
        l_ref[...] += jnp.sum(jnp.sum(e * e, axis=1, keepdims=True), axis=0, keepdims=True)

    tile = pl.BlockSpec((TM, d), lambda i: (i, 0))
    return pl.pallas_call(
        body, name="loss_head", grid=(s // TM,), in_specs=[tile, tile],
        out_specs=[tile, pl.BlockSpec((1, 1), lambda i: (0, 0))],
        out_shape=[jax.ShapeDtypeStruct((s, d), F32), jax.ShapeDtypeStruct((1, 1), F32)],
        compiler_params=_params(("arbitrary",)),
    )(y, target)


LOG2E = 1.4426950408889634
DEAD_LOG2 = -160.0
FIRST_LANE = 1


def _sb_terms(z, causal):
    z2 = z * LOG2E
    e = jnp.exp2(-jnp.abs(z2))
    l1p = jnp.log2(1.0 + e)
    lb = jnp.minimum(z2, 0.0) - l1p
    lr = lb - z2
    if causal is not None:
        lr = jnp.where(causal, lr, 0.0)
    return lb, lr, e


def _split_hi_lo(x):
    hi = x.astype(BF16)
    lo = (x - hi.astype(F32)).astype(BF16)
    return jnp.concatenate([hi, lo], axis=1)


def _att_consts(prefix):
    r = lax.broadcasted_iota(jnp.int32, (2 * ATT_T, ATT_T), 0) % ATT_T
    c = lax.broadcasted_iota(jnp.int32, (2 * ATT_T, ATT_T), 1)
    tri2 = jnp.where((r <= c) if prefix else (r >= c), 1.0, 0.0).astype(BF16)
    r = lax.broadcasted_iota(jnp.int32, (ATT_T, ATT_T), 0)
    c = lax.broadcasted_iota(jnp.int32, (ATT_T, ATT_T), 1)
    causal = c < r
    head_a = lax.broadcasted_iota(jnp.int32, (1, LANES), 1) < HEAD_DIM
    return tri2, causal, head_a


def _attn_fwd(q, k, v):
    s, d = q.shape
    nq = s // ATT_T

    def body(q_ref, k_ref, v_ref, ob_ref, lsum_ref, acc_a, acc_b, rem_a, rem_b):
        i = pl.program_id(1)
        tri, causal, head_a = _att_consts(prefix=False)
        q2 = q_ref[...]
        zero = jnp.zeros_like(q2)
        qa = jnp.where(head_a, q2, zero)
        qb = jnp.where(head_a, zero, q2)
        acc_a[...] = jnp.zeros_like(acc_a)
        acc_b[...] = jnp.zeros_like(acc_b)
        rem_a[...] = jnp.zeros_like(rem_a)
        rem_b[...] = jnp.zeros_like(rem_b)

        def block(kb, mask):
            rows = pl.ds(pl.multiple_of(kb * ATT_T, ATT_T), ATT_T)
            k2 = k_ref[rows, :]
            v2 = v_ref[rows, :]
            heads = ((qa, acc_a, rem_a), (qb, acc_b, rem_b))
            zs = [_dot(qm, k2, NT) for qm, _, _ in heads]
            terms = [_sb_terms(z, mask) for z in zs]
            sums = [_dot(_split_hi_lo(lr), tri, NN) for _, lr, _ in terms]
            for (_, acc, rem), (lb, lr, _), sincl in zip(heads, terms, sums):
                a = jnp.exp2(lb + (sincl - lr) + rem[...])
                if mask is not None:
                    a = jnp.where(mask, a, 0.0)
                rem[...] += sincl[:, 0:1]
                acc[...] += _dot(a.astype(BF16), v2, NN)

        block(i, causal)

        def live():
            return jnp.maximum(jnp.max(rem_a[...]), jnp.max(rem_b[...])) > DEAD_LOG2

        def go_on(carry):
            t, alive = carry
            return (t < i) & alive

        def step(carry):
            t, _ = carry
            block(i - 1 - t, None)
            return t + 1, live()

        done, _ = lax.while_loop(go_on, step, (jnp.int32(0), live()))
        first = (i - done).astype(F32)
        ob_ref[...] = jnp.where(head_a, acc_a[...], acc_b[...]).astype(BF16)
        lane = lax.broadcasted_iota(jnp.int32, (1, LANES), 1)
        lsum_ref[...] = jnp.where(lane == FIRST_LANE, first, jnp.where(head_a, rem_a[...], rem_b[...]))

    qspec = pl.BlockSpec((ATT_T, LANES), lambda p, i: (i, p))
    kspec = pl.BlockSpec((s, LANES), lambda p, i: (0, p))
    return pl.pallas_call(
        body, name="attn_fwd", grid=(d // LANES, nq), in_specs=[qspec, kspec, kspec],
        out_specs=[qspec, qspec],
        out_shape=[jax.ShapeDtypeStruct((s, d), BF16), jax.ShapeDtypeStruct((s, d), F32)],
        scratch_shapes=[pltpu.VMEM((ATT_T, LANES), F32), pltpu.VMEM((ATT_T, LANES), F32),
                        pltpu.VMEM((ATT_T, 1), F32), pltpu.VMEM((ATT_T, 1), F32)],
        compiler_params=_params(("parallel", "arbitrary")),
    )(q, k, v)


def _attn_bwd(q, k, v, do, lsum, dk_prev=None, dv_prev=None):
    s, d = q.shape
    nq = s // ATT_T
    has_prev = dk_prev is not None

    def body(*refs):
        q_ref, k_ref, v_ref, do_ref, ls_ref = refs[:5]
        n_in = 7 if has_prev else 5
        dq_ref, dk_ref, dv_ref, acc_a, acc_b, pre_a, pre_b, gp_a, gp_b, dkt, dvt = refs[n_in:]
        i = pl.program_id(1)

        @pl.when(i == 0)
        def _():
            dkt[...] = jnp.zeros_like(dkt)
            dvt[...] = jnp.zeros_like(dvt)

        tri, causal, head_a = _att_consts(prefix=True)
        q2 = q_ref[...]
        zero = jnp.zeros_like(q2)
        qa = jnp.where(head_a, q2, zero)
        qb = jnp.where(head_a, zero, q2)
        do2 = do_ref[...]
        doa = jnp.where(head_a, do2, 0.0).astype(BF16)
        dob = jnp.where(head_a, 0.0, do2).astype(BF16)
        row_a = lax.broadcasted_iota(jnp.int32, (LANES, 1), 0) < HEAD_DIM
        qt = q2.astype(F32).T
        dot_ = do2.T
        qta, qtb = jnp.where(row_a, qt, 0.0).astype(BF16), jnp.where(row_a, 0.0, qt).astype(BF16)
        dota, dotb = jnp.where(row_a, dot_, 0.0).astype(BF16), jnp.where(row_a, 0.0, dot_).astype(BF16)
        ls2 = ls_ref[...]
        tot_a = ls2[:, 0:1]
        tot_b = ls2[:, HEAD_DIM:HEAD_DIM + 1]
        for r in (acc_a, acc_b, pre_a, pre_b, gp_a, gp_b):
            r[...] = jnp.zeros_like(r)

        def block(kb, mask):
            rows = pl.ds(pl.multiple_of(kb * ATT_T, ATT_T), ATT_T)
            k2 = k_ref[rows, :]
            v2 = v_ref[rows, :]
            dk_new = jnp.zeros((LANES, ATT_T), F32)
            dv_new = jnp.zeros((LANES, ATT_T), F32)
            heads = ((qa, doa, tot_a, acc_a, pre_a, gp_a, qta, dota), (qb, dob, tot_b, acc_b, pre_b, gp_b, qtb, dotb))
            zs = [_dot(h[0], k2, NT) for h in heads]
            das = [_dot(h[1], v2, NT) for h in heads]
            terms = [_sb_terms(z, mask) for z in zs]
            psums = [_dot(_split_hi_lo(lr), tri, NN) for _, lr, _ in terms]
            gs, abs_ = [], []
            for h, (lb, _, _), pincl, da in zip(heads, terms, psums, das):
                tot, pre = h[2], h[4]
                a = jnp.exp2(lb + (tot - (pre[...] + pincl)))
                if mask is not None:
                    a = jnp.where(mask, a, 0.0)
                pre[...] += pincl[:, ATT_T - 1:ATT_T]
                gs.append(a * da)
                abs_.append(a.astype(BF16))
            gsums = [_dot(g.astype(BF16), tri[:ATT_T], NN) for g in gs]
            dzs = []
            for h, z, (_, _, e), g, gincl in zip(heads, zs, terms, gs, gsums):
                gpre = h[5]
                gbefore = gpre[...] + (gincl - g)
                gpre[...] += gincl[:, ATT_T - 1:ATT_T]
                inv = 1.0 / (1.0 + e)
                beta = jnp.where(z >= 0.0, inv, e * inv)
                dz = g - beta * (g + gbefore)
                if mask is not None:
                    dz = jnp.where(mask, dz, 0.0)
                dzs.append(dz.astype(BF16))
            for h, ab, dzb in zip(heads, abs_, dzs):
                dv_new += _dot(h[7], ab, NN)
                dk_new += _dot(h[6], dzb, NN)
                h[3][...] += _dot(dzb, k2, NN)
            cols = pl.ds(pl.multiple_of(kb * ATT_T, ATT_T), ATT_T)
            dkt[:, cols] += dk_new
            dvt[:, cols] += dv_new

        def step(kb, carry):
            block(kb, None)
            return carry

        first = jnp.clip(jnp.max(ls2[:, FIRST_LANE:FIRST_LANE + 1]).astype(jnp.int32), 0, i)
        lax.fori_loop(first, i, step, 0)
        block(i, causal)
        dq_ref[...] = (jnp.where(head_a, acc_a[...], acc_b[...]) * (HEAD_DIM ** -0.5)).astype(BF16)

        @pl.when(i == nq - 1)
        def _():
            for n in range(nq):
                rows = slice(n * ATT_T, (n + 1) * ATT_T)
                dkn, dvn = dkt[:, rows].T, dvt[:, rows].T
                if has_prev:
                    dkn, dvn = dkn + refs[5][rows, :], dvn + refs[6][rows, :]
                dk_ref[rows, :] = dkn
                dv_ref[rows, :] = dvn

    qspec = pl.BlockSpec((ATT_T, LANES), lambda p, i: (i, p))
    kspec = pl.BlockSpec((s, LANES), lambda p, i: (0, p))
    ins = [q, k, v, do, lsum] + ([dk_prev, dv_prev] if has_prev else [])
    return pl.pallas_call(
        body, name="attn_bwd", grid=(d // LANES, nq),
        in_specs=[qspec, kspec, kspec, qspec, qspec] + ([kspec, kspec] if has_prev else []),
        out_specs=[qspec, kspec, kspec],
        out_shape=[jax.ShapeDtypeStruct((s, d), BF16), jax.ShapeDtypeStruct((s, d), F32), jax.ShapeDtypeStruct((s, d), F32)],
        scratch_shapes=[pltpu.VMEM((ATT_T, LANES), F32), pltpu.VMEM((ATT_T, LANES), F32)]
        + [pltpu.VMEM((ATT_T, 1), F32)] * 4 + [pltpu.VMEM((LANES, s), F32)] * 2,
        compiler_params=_params(("parallel", "arbitrary")),
    )(*ins)


def _place():
    x, y, c = lax.axis_index("x"), lax.axis_index("y"), lax.axis_index("c")
    chips = [(1 - x, y), (x, 1 - y), (1 - x, 1 - y)]
    return x, y, c, chips


def _any_specs(n):
    return [pl.BlockSpec(memory_space=pl.ANY)] * n


def _gather_weights(bufs):
    n = len(bufs)

    def body(*refs):
        outs = refs[n:2 * n]
        send_sems, recv_sems = refs[2 * n:]
        x, y, c, chips = _place()
        me = 2 * x + y
        sibling = (x, y, 1 - c)

        def half(a, blk, hc):
            h = outs[a].shape[1] // 2
            return outs[a].at[blk, pl.ds(hc * h, h)]

        def copy(a, k, part, to):
            return pltpu.make_async_remote_copy(src_ref=part, dst_ref=part, send_sem=send_sems.at[a, k],
                                                recv_sem=recv_sems.at[a, k], device_id=to, device_id_type=MESH)

        sent = []
        for a in range(n):
            for k, chip in enumerate(chips):
                sent.append(copy(a, k, half(a, me, c), (*chip, c)))
                sent[-1].start()
        for a in range(n):
            for k, chip in enumerate(chips):
                blk = 2 * chip[0] + chip[1]
                copy(a, k, half(a, blk, c), sibling).wait_recv()
                sent.append(copy(a, 3 + k, half(a, blk, c), sibling))
                sent[-1].start()
        for a in range(n):
            for k, chip in enumerate(chips):
                blk = 2 * chip[0] + chip[1]
                copy(a, 3 + k, half(a, blk, 1 - c), sibling).wait_recv()
        for cp in sent:
            cp.wait_send()

    return pl.pallas_call(
        body, name="gather_weights", in_specs=_any_specs(n), out_specs=_any_specs(n),
        out_shape=[jax.ShapeDtypeStruct(w.shape, w.dtype) for w in bufs],
        input_output_aliases={a: a for a in range(n)},
        scratch_shapes=[pltpu.SemaphoreType.DMA((n, 6)), pltpu.SemaphoreType.DMA((n, 6))],
        compiler_params=pltpu.CompilerParams(has_side_effects=True),
    )(*bufs)


def _pair_exchange(grads):
    n = len(grads)

    def body(*refs):
        ins, outs = refs[:n], refs[n:2 * n]
        send_sems, recv_sems = refs[2 * n:]
        x, y, c, _ = _place()
        cps = []
        for a in range(n):
            h = ins[a].shape[1] // 2
            cps.append(pltpu.make_async_remote_copy(
                src_ref=ins[a].at[:, pl.ds((1 - c) * h, h)], dst_ref=outs[a], send_sem=send_sems.at[a],
                recv_sem=recv_sems.at[a], device_id=(x, y, 1 - c), device_id_type=MESH))
            cps[-1].start()
        for cp in cps:
            cp.wait()

    return pl.pallas_call(
        body, name="pair_exchange", in_specs=_any_specs(n), out_specs=_any_specs(n),
        out_shape=[jax.ShapeDtypeStruct((g.shape[0], g.shape[1] // 2, g.shape[2]), g.dtype) for g in grads],
        scratch_shapes=[pltpu.SemaphoreType.DMA((n,)), pltpu.SemaphoreType.DMA((n,))],
        compiler_params=pltpu.CompilerParams(has_side_effects=True),
    )(*grads)


def _chip_exchange(parts):
    n = len(parts)

    def body(*refs):
        ins, outs = refs[:n], refs[n:2 * n]
        send_sems, recv_sems = refs[2 * n:]
        x, y, c, chips = _place()
        me = 2 * x + y
        cps = []
        for a in range(n):
            for k, chip in enumerate(chips):
                blk = 2 * chip[0] + chip[1]
                cps.append(pltpu.make_async_remote_copy(
                    src_ref=ins[a].at[blk], dst_ref=outs[a].at[me], send_sem=send_sems.at[a, k],
                    recv_sem=recv_sems.at[a, k], device_id=(*chip, c), device_id_type=MESH))
                cps[-1].start()
        for a in range(n):
            for k, chip in enumerate(chips):
                blk = 2 * chip[0] + chip[1]
                pltpu.make_async_remote_copy(
                    src_ref=ins[a].at[blk], dst_ref=outs[a].at[blk], send_sem=send_sems.at[a, k],
                    recv_sem=recv_sems.at[a, k], device_id=(*chip, c), device_id_type=MESH).wait_recv()
        for cp in cps:
            cp.wait_send()

    return pl.pallas_call(
        body, name="chip_exchange", in_specs=_any_specs(n), out_specs=_any_specs(n),
        out_shape=[jax.ShapeDtypeStruct(p.shape, p.dtype) for p in parts],
        scratch_shapes=[pltpu.SemaphoreType.DMA((n, 3)), pltpu.SemaphoreType.DMA((n, 3))],
        compiler_params=pltpu.CompilerParams(has_side_effects=True),
    )(*parts)


def _half_swap(halves):
    n = len(halves)

    def body(*refs):
        outs = refs[n:2 * n]
        send_sems, recv_sems = refs[2 * n:]
        x, y, c, _ = _place()
        cps = []
        for a in range(n):
            h = outs[a].shape[1] // 2
            mine = outs[a].at[:, pl.ds(c * h, h)]
            cps.append(pltpu.make_async_remote_copy(
                src_ref=mine, dst_ref=mine, send_sem=send_sems.at[a], recv_sem=recv_sems.at[a],
                device_id=(x, y, 1 - c), device_id_type=MESH))
            cps[-1].start()
        for cp in cps:
            cp.wait()

    return pl.pallas_call(
        body, name="half_swap", in_specs=_any_specs(n), out_specs=_any_specs(n),
        out_shape=[jax.ShapeDtypeStruct(p.shape, p.dtype) for p in halves],
        input_output_aliases={a: a for a in range(n)},
        scratch_shapes=[pltpu.SemaphoreType.DMA((n,)), pltpu.SemaphoreType.DMA((n,))],
        compiler_params=pltpu.CompilerParams(has_side_effects=True),
    )(*halves)


N_DEV = 8


def _all_reduce_small(v):
    nrow, ncol = v.shape

    def body(v_ref, o_ref, land, red, send_sems, recv_sems, send2, recv2, loc_sem):
        x, y, c, _ = _place()
        me = 4 * x + 2 * y + c
        peers = []
        for k in range(1, N_DEV):
            peers.append((x ^ ((k >> 2) & 1), y ^ ((k >> 1) & 1), c ^ (k & 1)))
        own = pltpu.make_async_copy(v_ref.at[pl.ds(me, 1)], land.at[pl.ds(me, 1)], loc_sem)
        own.start()
        cps = []
        for k, peer in enumerate(peers):
            dev = 4 * peer[0] + 2 * peer[1] + peer[2]
            cps.append(pltpu.make_async_remote_copy(
                src_ref=v_ref.at[pl.ds(dev, 1)], dst_ref=land.at[pl.ds(me, 1)], send_sem=send_sems.at[k],
                recv_sem=recv_sems.at[k], device_id=peer, device_id_type=MESH))
            cps[-1].start()
        for k, peer in enumerate(peers):
            dev = 4 * peer[0] + 2 * peer[1] + peer[2]
            pltpu.make_async_remote_copy(
                src_ref=v_ref.at[pl.ds(dev, 1)], dst_ref=land.at[pl.ds(dev, 1)], send_sem=send_sems.at[k],
                recv_sem=recv_sems.at[k], device_id=peer, device_id_type=MESH).wait_recv()
        for cp in cps:
            cp.wait_send()
        own.wait()
        terms = land[...]
        total = terms[0:1, :]
        for d in range(1, N_DEV):
            total = total + terms[d:d + 1, :]
        red[...] = total
        own = pltpu.make_async_copy(red, o_ref.at[pl.ds(me, 1)], loc_sem)
        own.start()
        cps = []
        for k, peer in enumerate(peers):
            cps.append(pltpu.make_async_remote_copy(
                src_ref=red, dst_ref=o_ref.at[pl.ds(me, 1)], send_sem=send2.at[k],
                recv_sem=recv2.at[k], device_id=peer, device_id_type=MESH))
            cps[-1].start()
        for k, peer in enumerate(peers):
            dev = 4 * peer[0] + 2 * peer[1] + peer[2]
            pltpu.make_async_remote_copy(
                src_ref=red, dst_ref=o_ref.at[pl.ds(dev, 1)], send_sem=send2.at[k],
                recv_sem=recv2.at[k], device_id=peer, device_id_type=MESH).wait_recv()
        for cp in cps:
            cp.wait_send()
        own.wait()

    vm = pl.BlockSpec(memory_space=pltpu.VMEM)
    return pl.pallas_call(
        body, name="all_reduce_small", in_specs=[vm], out_specs=vm,
        out_shape=jax.ShapeDtypeStruct((nrow, ncol), F32),
        scratch_shapes=[pltpu.VMEM((nrow, ncol), F32), pltpu.VMEM((1, ncol), F32)]
        + [pltpu.SemaphoreType.DMA((N_DEV - 1,))] * 4 + [pltpu.SemaphoreType.DMA],
        compiler_params=pltpu.CompilerParams(has_side_effects=True, vmem_limit_bytes=VMEM_LIMIT),
    )(v)


def _row_tile(rows):
    return min(rows, 512)


def _pair_sum(g, got, place):
    nb, r, c = g.shape
    h = r // 2
    tr = _row_tile(h)
    nt = h // tr

    def body(place_ref, g_ref, got_ref, p_ref, pb_ref):
        p = g_ref[...] + got_ref[...]
        pb_ref[...] = p.astype(BF16)

        @pl.when(pl.program_id(1) == place_ref[0])
        def _():
            p_ref[...] = p

    spec = pl.BlockSpec((None, tr, c), lambda t, j, place_ref: (j, t, 0))
    grid_spec = pltpu.PrefetchScalarGridSpec(
        num_scalar_prefetch=1, grid=(nt, nb),
        in_specs=[pl.BlockSpec((None, tr, c), lambda t, j, place_ref: (j, place_ref[1] * nt + t, 0)), spec],
        out_specs=[pl.BlockSpec((tr, c), lambda t, j, place_ref: (t, 0)), spec])
    return pl.pallas_call(
        body, name="pair_sum", grid_spec=grid_spec,
        out_shape=[jax.ShapeDtypeStruct((h, c), F32), jax.ShapeDtypeStruct((nb, h, c), BF16)],
        compiler_params=_params(("parallel", "arbitrary")),
    )(place, g, got)


def _chip_sum(p, got, place, out, layer):
    h, c = p.shape
    tr = _row_tile(h)
    nt = h // tr

    def body(place_ref, p_ref, g1_ref, g2_ref, g3_ref, old_ref, o_ref):
        o_ref[...] = ((p_ref[...] + g1_ref[...].astype(F32)) + g2_ref[...].astype(F32)) + g3_ref[...].astype(F32)

    def blk(off):
        return pl.BlockSpec((None, tr, c), lambda t, place_ref: ((place_ref[0] + off) % N_CHIPS, t, 0))

    grid_spec = pltpu.PrefetchScalarGridSpec(
        num_scalar_prefetch=1, grid=(nt,),
        in_specs=[pl.BlockSpec((tr, c), lambda t, place_ref: (t, 0)), blk(1), blk(2), blk(3),
                  pl.BlockSpec(memory_space=pl.ANY)],
        out_specs=pl.BlockSpec((None, tr, c), lambda t, place_ref: (layer, place_ref[1] * nt + t, 0)))
    return pl.pallas_call(
        body, name="chip_sum", grid_spec=grid_spec, out_shape=jax.ShapeDtypeStruct(out.shape, F32),
        input_output_aliases={5: 0}, compiler_params=_params(("parallel",)),
    )(place, p, got, got, got, out)


def _adamw(w, g, m, v):
    r, c = w.shape
    tr = r if r < 8 else _row_tile(r)

    def body(w_ref, g_ref, m_ref, v_ref, d_ref, nm_ref, nv_ref):
        gv = g_ref[...]
        nm = ADAM_B1 * m_ref[...] + (1.0 - ADAM_B1) * gv
        nv = ADAM_B2 * v_ref[...] + (1.0 - ADAM_B2) * (gv * gv)
        m_hat = nm / (1.0 - ADAM_B1 ** ADAM_STEP)
        v_hat = nv / (1.0 - ADAM_B2 ** ADAM_STEP)
        d_ref[...] = -ADAM_LR * (m_hat / (jnp.sqrt(v_hat) + ADAM_EPS) + ADAM_WD * w_ref[...])
        nm_ref[...] = nm
        nv_ref[...] = nv

    tile = pl.BlockSpec((tr, c), lambda i: (i, 0))
    return pl.pallas_call(
        body, name="adamw", grid=(r // tr,), in_specs=[tile] * 4, out_specs=[tile] * 3,
        out_shape=[jax.ShapeDtypeStruct((r, c), F32)] * 3, compiler_params=_params(("parallel",)),
    )(w, g, m, v)


BIG = ("a_w_in", "a_w_out", "sb_w_k", "sb_w_v", "b_w_q", "b_w_o", "ffn_w1", "ffn_w2")
SMALL = ("a_ln_g", "a_ln_b", "a_w_s", "a_b_s", "mix_ln_g", "mix_ln_b", "ffn_ln_g", "ffn_ln_b")
COL_SHARDED = {"a_w_in": True, "a_w_out": False, "sb_w_k": False, "sb_w_v": False, "b_w_q": False, "b_w_o": False,
               "ffn_w1": True, "ffn_w2": False}


def kernel(x, a_w_in, a_ln_g, a_ln_b, a_w_s, a_b_s, a_w_out, sb_w_k, sb_w_v, b_w_q, b_w_o, mix_ln_g, mix_ln_b, ffn_ln_g, ffn_ln_b, ffn_w1, ffn_w2, loss_target, m_a_w_in, m_a_ln_g, m_a_ln_b, m_a_w_s, m_a_b_s, m_a_w_out, m_sb_w_k, m_sb_w_v, m_b_w_q, m_b_w_o, m_mix_ln_g, m_mix_ln_b, m_ffn_ln_g, m_ffn_ln_b, m_ffn_w1, m_ffn_w2, v_a_w_in, v_a_ln_g, v_a_ln_b, v_a_w_s, v_a_b_s, v_a_w_out, v_sb_w_k, v_sb_w_v, v_b_w_q, v_b_w_o, v_mix_ln_g, v_mix_ln_b, v_ffn_ln_g, v_ffn_ln_b, v_ffn_w1, v_ffn_w2):
    names = BIG + SMALL
    given = dict(a_w_in=a_w_in, a_ln_g=a_ln_g, a_ln_b=a_ln_b, a_w_s=a_w_s, a_b_s=a_b_s, a_w_out=a_w_out, sb_w_k=sb_w_k,
                 sb_w_v=sb_w_v, b_w_q=b_w_q, b_w_o=b_w_o, mix_ln_g=mix_ln_g, mix_ln_b=mix_ln_b, ffn_ln_g=ffn_ln_g,
                 ffn_ln_b=ffn_ln_b, ffn_w1=ffn_w1, ffn_w2=ffn_w2)
    mom = dict(a_w_in=m_a_w_in, a_ln_g=m_a_ln_g, a_ln_b=m_a_ln_b, a_w_s=m_a_w_s, a_b_s=m_a_b_s, a_w_out=m_a_w_out,
               sb_w_k=m_sb_w_k, sb_w_v=m_sb_w_v, b_w_q=m_b_w_q, b_w_o=m_b_w_o, mix_ln_g=m_mix_ln_g, mix_ln_b=m_mix_ln_b,
               ffn_ln_g=m_ffn_ln_g, ffn_ln_b=m_ffn_ln_b, ffn_w1=m_ffn_w1, ffn_w2=m_ffn_w2)
    var = dict(a_w_in=v_a_w_in, a_ln_g=v_a_ln_g, a_ln_b=v_a_ln_b, a_w_s=v_a_w_s, a_b_s=v_a_b_s, a_w_out=v_a_w_out,
               sb_w_k=v_sb_w_k, sb_w_v=v_sb_w_v, b_w_q=v_b_w_q, b_w_o=v_b_w_o, mix_ln_g=v_mix_ln_g, mix_ln_b=v_mix_ln_b,
               ffn_ln_g=v_ffn_ln_g, ffn_ln_b=v_ffn_ln_b, ffn_w1=v_ffn_w1, ffn_w2=v_ffn_w2)

    cx, cy, cc = lax.axis_index("x"), lax.axis_index("y"), lax.axis_index("c")
    chip = (2 * cx + cy).astype(jnp.int32)
    chip_arr = chip.reshape(1)

    s, d = x.shape[1], x.shape[2]
    xf = x.reshape(s, d)
    target = loss_target.reshape(s, d)

    def as2d(w):
        return w.reshape(-1, w.shape[-1])

    gw = {}
    for n in BIG:
        for l in ([None] if given[n].ndim == 2 else range(given[n].shape[0])):
            gw[(n, l)] = _cast_into_slot(given[n], l, chip_arr)
    ln_gb = jnp.stack([a_ln_g, a_ln_b])
    ln_slot = lax.dynamic_update_slice(jnp.zeros((N_CHIPS,) + ln_gb.shape, F32), ln_gb[None], (chip, 0, 0, 0))
    layer0 = [("a_w_in", 0), ("a_w_out", 0), ("ffn_w1", 0), ("ffn_w2", 0)]
    gathered = _gather_weights([gw[k] for k in layer0] + [ln_slot])
    gw.update(zip(layer0, gathered[:-1]))
    mixer = {1: [("a_w_in", 1), ("a_w_out", 1)], 2: [("sb_w_k", None), ("sb_w_v", None), ("b_w_q", 0), ("b_w_o", 0)],
             3: [("b_w_q", 1), ("b_w_o", 1)]}

    def riding(d2d=(), ici=()):
        keys = list(d2d) + list(ici)
        return keys, [("d2d", gw[k]) for k in d2d] + [("ici", gw[k]) for k in ici]

    def landed_in(keys, bufs):
        gw.update(zip(keys, bufs))

    ln_full = gathered[-1].transpose(1, 2, 0, 3).reshape(2, N_A, 1, -1)
    a_ln_g3, a_ln_b3 = ln_full[0], ln_full[1]
    mix_g3, mix_b3 = mix_ln_g[:, None, :], mix_ln_b[:, None, :]
    ffn_g3, ffn_b3 = ffn_ln_g[:, None, :], ffn_ln_b[:, None, :]
    bst = jnp.swapaxes(a_b_s, 1, 2)

    saved = []
    xb = _cast_bf16(xf)
    kb = vb = None
    for l in range(DEPTH):
        sv = dict(x_in=xb)
        last = l == DEPTH - 1
        keys, riders = riding(d2d=[("ffn_w1", l), ("ffn_w2", l)] if l else [], ici=mixer[l + 1] if l < N_A else [])
        if l < N_A:
            h, *bufs = _mm_fwd("a_in", xb, gw[("a_w_in", l)], None, True, riders=riders)
            landed_in(keys, bufs)
            vn = _gmlp_norm_fwd(h, a_ln_g3, a_ln_b3, l)
            gated = _gate_fwd(h, vn, a_w_s[l], bst[l])
            xf, xb, xhat, rstd = _mm_resid_ln("a_out", gated, gw[("a_w_out", l)], xf, mix_g3, mix_b3, l)
            sv.update(h=h, vn=vn, gated=gated)
        else:
            j = l - N_A
            if l == N_A:
                kb, *bufs = _mm_fwd("sb_k", xb, gw[("sb_w_k", None)], None, False, _ep_bf16, outs=[(d, BF16)],
                                    riders=riders)
                landed_in(keys, bufs)
                keys, riders = [], ()
                vb = _mm_fwd("sb_v", xb, gw[("sb_w_v", None)], None, False, _ep_bf16, outs=[(d, BF16)])[0]
            q, *bufs = _mm_fwd("b_q", xb, gw[("b_w_q", j)], None, False, _ep_scale_q, outs=[(d, BF16)], riders=riders)
            landed_in(keys, bufs)
            ob, lsum = _attn_fwd(q, kb, vb)
            keys, riders = riding(ici=[] if last else mixer[l + 1])
            xf, xb, xhat, rstd, *bufs = _mm_resid_ln("b_out", ob, gw[("b_w_o", j)], xf, mix_g3, mix_b3, l, riders)
            landed_in(keys, bufs)
            sv.update(q=q, lsum=lsum, ob=ob)
        sv.update(x_mid=xb, xhat1=xhat, rstd1=rstd)
        dff = gw[("ffn_w1", l)].shape[-1] * N_CHIPS
        keys, riders = riding(ici=[] if last else [("ffn_w1", l + 1)])
        pr, act, *bufs = _mm_fwd("ffn_1", xb, gw[("ffn_w1", l)], None, True, _ep_relu2,
                                 outs=[(dff, BF16), (dff, BF16)], riders=riders)
        landed_in(keys, bufs)
        keys, riders = riding(d2d=[] if last else mixer[l + 1], ici=[] if last else [("ffn_w2", l + 1)])
        xf, xb, xhat, rstd, *bufs = _mm_resid_ln("ffn_2", act, gw[("ffn_w2", l)], xf, ffn_g3, ffn_b3, l, riders)
        landed_in(keys, bufs)
        sv.update(pr=pr, act=act, xhat2=xhat, rstd2=rstd)
        saved.append(sv)

    dx, sq = _loss_head(xf, target)
    loss = lax.psum(0.5 * sq[0, 0] / d, ("x", "y", "c"))

    pending = []
    pair_sums, landed = {}, {}
    place_arr = jnp.stack([chip, cc.astype(jnp.int32)])
    ffn = ("ffn_w1", "ffn_w2")

    def arrived(took, outs):
        for (kind, key, arr), out in zip(took, outs):
            if kind == "pair":
                pair_sums[key] = _pair_sum(arr, out, place_arr)
                pending.append(("chip", key, pair_sums[key][1]))
            else:
                landed[key] = out

    def carrying(call, name, *args, **kw):
        took, room = [], name.startswith("ffn")
        for task in list(pending):
            heavy = task[0] == "chip" and task[1][0] in ffn
            if not heavy or room:
                took.append(task)
                pending.remove(task)
                room = room and not heavy
        out, *rest = call(name, *args, riders=[(kind, arr) for kind, _, arr in took], **kw)
        arrived(took, rest)
        return out

    def bwd_act(*args, **kw):
        return carrying(_mm_bwd_act, *args, **kw)

    def bwd_w(key, name, a, dy):
        pending.append(("pair", key, carrying(_mm_bwd_w, name, a, dy, gw[key], COL_SHARDED[key[0]])))

    d_mix_g, d_mix_b, d_ffn_g, d_ffn_b = [None] * DEPTH, [None] * DEPTH, [None] * DEPTH, [None] * DEPTH
    d_ln_g, d_ln_b, d_ws, d_bs = [None] * N_A, [None] * N_A, [None] * N_A, [None] * N_A
    dk = dv = None
    for l in reversed(range(DEPTH)):
        sv = saved[l]
        dr, drb, d_ffn_g[l], d_ffn_b[l] = _ln_bwd(dx, sv["xhat2"], sv["rstd2"], ffn_g3, l)
        dff = sv["pr"].shape[1]
        dhd = bwd_act("ffn_2_dx", drb, gw[("ffn_w2", l)], None, False, _ep_relu2_bwd, (sv["pr"],),
                      (pl.BlockSpec((TM, dff // N_CHIPS), lambda j, i, k: (i, j)),), out_dtype=BF16)
        bwd_w(("ffn_w2", l), "ffn_2_dw", sv["act"], drb)
        dx = bwd_act("ffn_1_dx", dhd, gw[("ffn_w1", l)], None, True, _ep_resid, (dr,), (_row_spec(d),))
        bwd_w(("ffn_w1", l), "ffn_1_dw", sv["x_mid"], dhd)

        dr, drb, d_mix_g[l], d_mix_b[l] = _ln_bwd(dx, sv["xhat1"], sv["rstd1"], mix_g3, l)
        quarter = pl.BlockSpec((TM, d // N_CHIPS), lambda j, i, k: (i, j))
        if l < N_A:
            dgated = bwd_act("a_out_dx", drb, gw[("a_w_out", l)], None, False)
            bwd_w(("a_w_out", l), "a_out_dw", sv["gated"], drb)
            du, dvn, d_ws[l], dbs_wide = _gate_bwd(dgated, sv["h"], sv["vn"], a_w_s[l], bst[l])
            d_bs[l] = dbs_wide[:, :, 0]
            dh, dlg, dlb = _gmlp_in_bwd(sv["h"], du, dvn, a_ln_g3, l)
            d_ln_g[l], d_ln_b[l] = dlg[0], dlb[0]
            dx = bwd_act("a_in_dx", dh, gw[("a_w_in", l)], None, True, _ep_resid, (dr,), (_row_spec(d),))
            bwd_w(("a_w_in", l), "a_in_dw", sv["x_in"], dh)
        else:
            j = l - N_A
            do = bwd_act("b_out_dx", drb, gw[("b_w_o", j)], None, False)
            bwd_w(("b_w_o", j), "b_out_dw", sv["ob"], drb)
            dq, dk, dv = _attn_bwd(sv["q"], kb, vb, do, sv["lsum"], dk, dv)
            dx = bwd_act("b_q_dx", dq, gw[("b_w_q", j)], None, False, _ep_resid, (dr,), (quarter,))
            bwd_w(("b_w_q", j), "b_q_dw", sv["x_in"], dq)
            if l == N_A:
                dx = bwd_act("sb_k_dx", dk, gw[("sb_w_k", None)], None, False, _ep_add, (dx,), (quarter,))
                bwd_w(("sb_w_k", None), "sb_k_dw", sv["x_in"], dk)
                dx = bwd_act("sb_v_dx", dv, gw[("sb_w_v", None)], None, False, _ep_add, (dx,), (quarter,))
                bwd_w(("sb_w_v", None), "sb_v_dw", sv["x_in"], dv)
    grad_x = dx.reshape(x.shape)

    while pending:
        took = list(pending)
        pending.clear()
        for kind, exchange in (("pair", _pair_exchange), ("chip", _chip_exchange)):
            some = [t for t in took if t[0] == kind]
            if some:
                arrived(some, exchange([arr for _, _, arr in some]))

    stacked = []
    for n in BIG:
        layers = [None] if given[n].ndim == 2 else range(given[n].shape[0])
        out = lax.empty((len(layers),) + given[n].shape[-2:], F32)
        for at, l in enumerate(layers):
            out = _chip_sum(pair_sums[(n, l)][0], landed[(n, l)], place_arr, out, at)
        stacked.append(out)
    grads = {n: g.reshape(given[n].shape) for n, g in zip(BIG, _half_swap(stacked))}

    small_full = dict(a_ln_g=jnp.stack(d_ln_g), a_ln_b=jnp.stack(d_ln_b), a_w_s=jnp.stack(d_ws), a_b_s=jnp.stack(d_bs),
                      mix_ln_g=jnp.concatenate(d_mix_g), mix_ln_b=jnp.concatenate(d_mix_b),
                      ffn_ln_g=jnp.concatenate(d_ffn_g), ffn_ln_b=jnp.concatenate(d_ffn_b))
    packed = jnp.concatenate([small_full[n].reshape(-1) for n in SMALL])
    total = packed.shape[0]
    ncol = -(-total // (N_DEV * LANES)) * LANES
    packed = jnp.pad(packed, (0, N_DEV * ncol - total)).reshape(N_DEV, ncol)
    reduced = _all_reduce_small(packed).reshape(-1)
    off = 0
    for n in SMALL:
        size = small_full[n].size
        g = reduced[off:off + size].reshape(small_full[n].shape)
        off += size
        if n in ("a_ln_g", "a_ln_b"):
            wq = given[n].shape[1]
            g = lax.dynamic_slice_in_dim(g, chip * wq, wq, axis=1)
        grads[n] = g

    delta, new_m, new_v = {}, {}, {}
    for n in names:
        shape = given[n].shape
        dl, nm, nv = _adamw(as2d(given[n]), as2d(grads[n]), as2d(mom[n]), as2d(var[n]))
        delta[n], new_m[n], new_v[n] = dl.reshape(shape), nm.reshape(shape), nv.reshape(shape)

    order = ("a_w_in", "a_ln_g", "a_ln_b", "a_w_s", "a_b_s", "a_w_out", "sb_w_k", "sb_w_v", "b_w_q", "b_w_o",
             "mix_ln_g", "mix_ln_b", "ffn_ln_g", "ffn_ln_b", "ffn_w1", "ffn_w2")
    return (loss, grad_x, *[grads[n] for n in order], *[delta[n] for n in order],
            *[new_m[n] for n in order], *[new_v[n] for n in order])
```

```python
import math

import jax
import jax.numpy as jnp
from jax import lax
from jax.experimental import pallas as pl
from jax.experimental.pallas import tpu as pltpu

F32 = jnp.float32
BF16 = jnp.bfloat16
MESH = pl.DeviceIdType.MESH

N_CHIPS = 4
DEPTH = 4
N_A = 2
ALPHA = float((2 * DEPTH) ** 0.25)
LN_EPS = 1e-5
CHUNK = 64
GMLP_BLOCK = 128
GMLP_GROUPS = 8
HEAD_DIM = 64
LANES = 128
ATT_T = 256
ADAM_LR = 0.001
ADAM_B1 = 0.9
ADAM_B2 = 0.999
ADAM_EPS = 1e-08
ADAM_WD = 0.01
ADAM_STEP = 10
VMEM_LIMIT = 56 * 1024 * 1024
TM = 512
TM_WIDE = 1024
TS = 1024

NN = ((1,), (0,))
NT = ((1,), (1,))
TN = ((0,), (0,))


def _params(sem):
    return pltpu.CompilerParams(dimension_semantics=sem, vmem_limit_bytes=VMEM_LIMIT)


def _dot(a, b, contract):
    return lax.dot_general(a, b, (contract, ((), ())), preferred_element_type=F32)


def _rider_out(kind, arr):
    shape = (arr.shape[0], arr.shape[1] // 2, arr.shape[2]) if kind == "pair" else arr.shape
    return jax.ShapeDtypeStruct(shape, arr.dtype)


def _rider_copies(kind, src, dst, send_sems, recv_sems, base):
    x, y, c, chips = _place()
    me = 2 * x + y
    sibling = (x, y, 1 - c)

    def copy(k, part, land, to):
        return pltpu.make_async_remote_copy(src_ref=part, dst_ref=land, send_sem=send_sems.at[base + k],
                                            recv_sem=recv_sems.at[base + k], device_id=to, device_id_type=MESH)

    if kind == "pair":
        h = src.shape[1] // 2
        cp = copy(0, src.at[:, pl.ds((1 - c) * h, h)], dst, sibling)
        return [cp], [cp]
    h = dst.shape[1] // 2
    starts, arrivals = [], []
    for k, chip in enumerate(chips):
        blk = 2 * chip[0] + chip[1]
        if kind == "ici":
            starts.append(copy(k, dst.at[me, pl.ds(c * h, h)], dst.at[me, pl.ds(c * h, h)], (*chip, c)))
            arrivals.append(copy(k, dst.at[blk, pl.ds(c * h, h)], dst.at[blk, pl.ds(c * h, h)], (*chip, c)))
        elif kind == "d2d":
            starts.append(copy(k, dst.at[blk, pl.ds(c * h, h)], dst.at[blk, pl.ds(c * h, h)], sibling))
            arrivals.append(copy(k, dst.at[blk, pl.ds((1 - c) * h, h)], dst.at[blk, pl.ds((1 - c) * h, h)], sibling))
        else:
            starts.append(copy(k, src.at[blk], dst.at[me], (*chip, c)))
            arrivals.append(copy(k, src.at[blk], dst.at[blk], (*chip, c)))
    return starts, arrivals


RIDER_SEMS = 3
CARRIER_PARAMS = 5 * 2 ** 20


def _matmul(name, operands, in_specs, out_shapes, out_specs, grid, contract, epilogue, acc_shape, aliases=None,
            chunks=None, riders=()):
    nk = grid[2]
    n_in, n_out, nr = len(operands), len(out_shapes), len(riders)
    n_plain = n_in + nr + n_out

    def body(*refs):
        ins, outs = refs[:n_in], refs[n_in + nr:n_plain]
        if nr:
            srcs, dsts = refs[n_in:n_in + nr], refs[n_plain:n_plain + nr]
            send_sems, recv_sems = refs[-2:]
            pid = [pl.program_id(ax) for ax in range(3)]
            first = (pid[0] == 0) & (pid[1] == 0) & (pid[2] == 0)
            last = (pid[0] == grid[0] - 1) & (pid[1] == grid[1] - 1) & (pid[2] == grid[2] - 1)

            def copies(n):
                return _rider_copies(riders[n][0], srcs[n], dsts[n], send_sems, recv_sems, RIDER_SEMS * n)

            @pl.when(first)
            def _():
                for n in range(nr):
                    for cp in copies(n)[0]:
                        cp.start()

        compute(refs, ins, outs)
        if nr:
            @pl.when(last)
            def _():
                for n in range(nr):
                    starts, arrivals = copies(n)
                    for cp in arrivals:
                        cp.wait_recv()
                    for cp in starts:
                        cp.wait_send()

    def compute(refs, ins, outs):
        if chunks is None:
            p = _dot(ins[0][...].astype(BF16), ins[1][...].astype(BF16), contract)
        else:
            width = ins[0].shape[1] // chunks
            p = None
            for j in range(chunks):
                pj = _dot(ins[0][:, j * width:(j + 1) * width].astype(BF16), ins[1][j].astype(BF16), contract)
                p = pj if p is None else p + pj
        if nk == 1:
            epilogue(p, ins[2:], outs)
            return
        acc = refs[n_plain + nr]
        k = pl.program_id(2)

        @pl.when(k == 0)
        def _():
            acc[...] = p

        @pl.when((k > 0) & (k < nk - 1))
        def _():
            acc[...] += p

        @pl.when(k == nk - 1)
        def _():
            epilogue(acc[...] + p, ins[2:], outs)

    rbufs = [b for _, b in riders]
    in_place = {n_in + n: n_out + n for n, (kind, _) in enumerate(riders) if kind in ("ici", "d2d")}
    scratch = ([] if nk == 1 else [pltpu.VMEM(acc_shape, F32)]) \
        + [pltpu.SemaphoreType.DMA((RIDER_SEMS * nr,))] * (2 if nr else 0)
    return pl.pallas_call(
        body, name=name, grid=grid, in_specs=list(in_specs) + _any_specs(nr), out_specs=list(out_specs) + _any_specs(nr),
        out_shape=list(out_shapes) + [_rider_out(kind, b) for kind, b in riders],
        scratch_shapes=scratch,
        input_output_aliases={**(aliases or {}), **in_place},
        compiler_params=_params(("arbitrary",) * 3 if nr else ("parallel", "parallel", "arbitrary")),
    )(*operands, *rbufs)


def _wspec(w, layer, whole=False):
    r, c = w.shape[-2:]
    lead = N_CHIPS if whole else None
    if w.ndim == 4:
        return pl.BlockSpec((lead, None, r, c), lambda j, i, k: (0 if whole else j, layer, 0, 0))
    return pl.BlockSpec((lead, r, c), lambda j, i, k: (0 if whole else j, 0, 0))


def _wide_tile(s):
    return min(TM_WIDE, s)


def _ep_store(p, ins, outs):
    for o in outs:
        o[...] = p.astype(o.dtype)


def _mm_fwd(name, a, w, layer, col_sharded, epilogue=_ep_store, extras=(), extra_specs=(), outs=None, riders=()):
    s = a.shape[0]
    r, c = w.shape[-2:]
    if col_sharded:
        tm = _wide_tile(s)
        grid = (N_CHIPS, s // tm, 1)
        a_spec = pl.BlockSpec((tm, r), lambda j, i, k: (i, 0))
        n_out = N_CHIPS * c
    else:
        tm = TM
        grid = (1, s // tm, 1)
        a_spec = pl.BlockSpec((tm, N_CHIPS * r), lambda j, i, k: (i, 0))
        n_out = c
    if outs is None:
        outs = [(n_out, F32)]
    out_shapes = [jax.ShapeDtypeStruct((s, n), dt) for n, dt in outs]
    out_specs = [pl.BlockSpec((tm, c if n == n_out else n), lambda j, i, k: (i, j)) for n, _ in outs]
    return _matmul(name, (a, w) + tuple(extras), [a_spec, _wspec(w, layer, not col_sharded)] + list(extra_specs),
                   out_shapes, out_specs, grid, NN, epilogue, (tm, c), chunks=None if col_sharded else N_CHIPS,
                   riders=riders)


def _mm_bwd_act(name, dy, w, layer, col_sharded, epilogue=_ep_store, extras=(), extra_specs=(), out_dtype=F32,
                riders=()):
    s = dy.shape[0]
    r, c = w.shape[-2:]
    if col_sharded:
        tm = TM
        grid = (1, s // tm, 1)
        a_spec = pl.BlockSpec((tm, N_CHIPS * c), lambda j, i, k: (i, 0))
        n_out = r
    else:
        tm = _wide_tile(s)
        grid = (N_CHIPS, s // tm, 1)
        a_spec = pl.BlockSpec((tm, c), lambda j, i, k: (i, 0))
        n_out = N_CHIPS * r
    o_spec = pl.BlockSpec((tm, r), lambda j, i, k: (i, j))
    return _matmul(name, (dy, w) + tuple(extras), [a_spec, _wspec(w, layer, col_sharded)] + list(extra_specs),
                   [jax.ShapeDtypeStruct((s, n_out), out_dtype)], [o_spec], grid, NT, epilogue, (tm, r),
                   chunks=N_CHIPS if col_sharded else None, riders=riders)


def _mm_bwd_w(name, a, dy, w, col_sharded, riders=()):
    s = a.shape[0]
    r, c = w.shape[-2:]
    ts = min(TS, s)
    grid = (N_CHIPS, 1, s // ts)
    if col_sharded:
        a_spec = pl.BlockSpec((ts, r), lambda j, i, k: (k, 0))
        b_spec = pl.BlockSpec((ts, c), lambda j, i, k: (k, j))
    else:
        a_spec = pl.BlockSpec((ts, r), lambda j, i, k: (k, j))
        b_spec = pl.BlockSpec((ts, c), lambda j, i, k: (k, 0))

    def epilogue(p, ins, outs):
        outs[0][...] = p

    return _matmul(name, (a, dy), [a_spec, b_spec], [jax.ShapeDtypeStruct(w.shape, F32)], [_wspec(w, None)], grid, TN,
                   epilogue, (r, c), riders=riders)


def _row_spec(n):
    return pl.BlockSpec((TM, n), lambda j, i, k: (i, 0))


def _vec_spec(layer, n):
    return pl.BlockSpec((None, 1, n), lambda j, i, k: (layer, 0, 0))


def _ep_resid_ln(p, ins, outs):
    x_ref, g_ref, b_ref = ins
    xf_ref, xb_ref, xhat_ref, rstd_ref = outs
    r = ALPHA * x_ref[...] + p
    mu = jnp.mean(r, axis=-1, keepdims=True)
    d = r - mu
    var = jnp.mean(d * d, axis=-1, keepdims=True)
    rstd = lax.rsqrt(var + LN_EPS)
    xhat = d * rstd
    y = xhat * g_ref[...] + b_ref[...]
    xf_ref[...] = y
    xb_ref[...] = y.astype(BF16)
    xhat_ref[...] = xhat
    rstd_ref[...] = rstd


def _mm_resid_ln(name, a, w, x, g3, b3, ln_layer, riders=()):
    d = x.shape[1]
    return _mm_fwd(name, a, w, None, False, _ep_resid_ln, (x, g3, b3),
                   (_row_spec(d), _vec_spec(ln_layer, d), _vec_spec(ln_layer, d)),
                   outs=[(d, F32), (d, BF16), (d, F32), (1, F32)], riders=riders)


def _ep_relu2(p, ins, outs):
    h = jnp.maximum(p, 0.0)
    outs[0][...] = h.astype(BF16)
    outs[1][...] = (h * h).astype(BF16)


def _ep_scale_q(p, ins, outs):
    outs[0][...] = (p * (HEAD_DIM ** -0.5)).astype(BF16)


def _ep_bf16(p, ins, outs):
    outs[0][...] = p.astype(BF16)


def _ep_relu2_bwd(p, ins, outs):
    outs[0][...] = (p * (2.0 * ins[0][...].astype(F32))).astype(BF16)


def _ep_resid(p, ins, outs):
    outs[0][...] = ALPHA * ins[0][...] + p


def _ep_add(p, ins, outs):
    outs[0][...] = ins[0][...] + p


def _gelu_grad(x):
    c0 = math.sqrt(2.0 / math.pi)
    t = jnp.tanh(c0 * (x + 0.044715 * (x * x * x)))
    return 0.5 * (1.0 + t) + (0.5 * x) * (1.0 - t * t) * (c0 * (1.0 + 3.0 * 0.044715 * (x * x)))


def _cast_bf16(w2d):
    r, c = w2d.shape
    tr = min(r, 512)

    def body(w_ref, o_ref):
        o_ref[...] = w_ref[...].astype(BF16)

    return pl.pallas_call(
        body, name="cast_bf16", grid=(r // tr,),
        in_specs=[pl.BlockSpec((tr, c), lambda i: (i, 0))], out_specs=pl.BlockSpec((tr, c), lambda i: (i, 0)),
        out_shape=jax.ShapeDtypeStruct((r, c), BF16), compiler_params=_params(("parallel",)),
    )(w2d)


def _cast_into_slot(w, layer, chip):
    r, c = w.shape[-2:]
    tr = min(r, 512)

    def body(chip_ref, w_ref, o_ref):
        o_ref[...] = w_ref[...].astype(BF16)

    if layer is None:
        w_spec = pl.BlockSpec((tr, c), lambda i, chip_ref: (i, 0))
    else:
        w_spec = pl.BlockSpec((None, tr, c), lambda i, chip_ref: (layer, i, 0))
    grid_spec = pltpu.PrefetchScalarGridSpec(
        num_scalar_prefetch=1, grid=(r // tr,), in_specs=[w_spec],
        out_specs=pl.BlockSpec((None, tr, c), lambda i, chip_ref: (chip_ref[0], i, 0)))
    return pl.pallas_call(
        body, name="cast_into_slot", grid_spec=grid_spec,
        out_shape=jax.ShapeDtypeStruct((N_CHIPS, r, c), BF16), compiler_params=_params(("parallel",)),
    )(chip, w)


def _gmlp_norm_fwd(h, g3, b3, layer):
    s, w2 = h.shape
    w = w2 // 2

    def body(h_ref, g_ref, b_ref, o_ref):
        z = jax.nn.gelu(h_ref[...])
        mu = jnp.mean(z, axis=-1, keepdims=True)
        d = z - mu
        var = jnp.mean(d * d, axis=-1, keepdims=True)
        o_ref[...] = (d * lax.rsqrt(var + LN_EPS) * g_ref[...] + b_ref[...]).astype(BF16)

    vec = pl.BlockSpec((None, 1, w), lambda i: (layer, 0, 0))
    return pl.pallas_call(
        body, name="gmlp_norm_fwd", grid=(s // TM,),
        in_specs=[pl.BlockSpec((TM, w), lambda i: (i, 1)), vec, vec],
        out_specs=pl.BlockSpec((TM, w), lambda i: (i, 0)),
        out_shape=jax.ShapeDtypeStruct((s, w), BF16), compiler_params=_params(("parallel",)),
    )(h, g3, b3)


def _chunk_mask():
    t = lax.broadcasted_iota(jnp.int32, (GMLP_BLOCK, GMLP_BLOCK), 0)
    s = lax.broadcasted_iota(jnp.int32, (GMLP_BLOCK, GMLP_BLOCK), 1)
    return (s // CHUNK) <= (t // CHUNK)


SG_ROWS = 512


def _gate_fwd(h, vn, ws, bst):
    s, w = vn.shape
    gd = w // GMLP_GROUPS

    def body(h_ref, v_ref, ws_ref, bs_ref, o_ref):
        mask = _chunk_mask()
        for g in range(GMLP_GROUPS):
            wm = jnp.where(mask, ws_ref[g], 0.0).astype(BF16)
            bias = bs_ref[:, g:g + 1]
            cols = slice(g * gd, (g + 1) * gd)
            for n in range(SG_ROWS // GMLP_BLOCK):
                rows = slice(n * GMLP_BLOCK, (n + 1) * GMLP_BLOCK)
                sp = _dot(wm, v_ref[rows, cols], NN) + bias
                o_ref[rows, cols] = (jax.nn.gelu(h_ref[rows, cols]) * sp).astype(BF16)

    return pl.pallas_call(
        body, name="gate_fwd", grid=(s // SG_ROWS,),
        in_specs=[pl.BlockSpec((SG_ROWS, w), lambda i: (i, 0)), pl.BlockSpec((SG_ROWS, w), lambda i: (i, 0)),
                  pl.BlockSpec(ws.shape, lambda i: (0, 0, 0)), pl.BlockSpec(bst.shape, lambda i: (0, 0))],
        out_specs=pl.BlockSpec((SG_ROWS, w), lambda i: (i, 0)),
        out_shape=jax.ShapeDtypeStruct((s, w), BF16), compiler_params=_params(("parallel",)),
    )(h, vn, ws, bst)


def _gate_bwd(dgated, h, vn, ws, bst):
    s, w = vn.shape
    gd = w // GMLP_GROUPS
    nsteps = s // SG_ROWS

    def body(dg_ref, h_ref, v_ref, ws_ref, bs_ref, du_ref, dv_ref, dws_ref, dbs_ref, dsum):
        i = pl.program_id(0)

        @pl.when(i == 0)
        def _():
            dws_ref[...] = jnp.zeros_like(dws_ref)
            dsum[...] = jnp.zeros_like(dsum)

        mask = _chunk_mask()
        for g in range(GMLP_GROUPS):
            wm = jnp.where(mask, ws_ref[g], 0.0).astype(BF16)
            bias = bs_ref[:, g:g + 1]
            cols = slice(g * gd, (g + 1) * gd)
            dw = jnp.zeros((GMLP_BLOCK, GMLP_BLOCK), F32)
            dsg = jnp.zeros((GMLP_BLOCK, gd), F32)
            for n in range(SG_ROWS // GMLP_BLOCK):
                rows = slice(n * GMLP_BLOCK, (n + 1) * GMLP_BLOCK)
                vb = v_ref[rows, cols]
                sp = _dot(wm, vb, NN) + bias
                dg = dg_ref[rows, cols]
                du_ref[rows, cols] = dg * sp
                ds = dg * jax.nn.gelu(h_ref[rows, cols])
                dsb = ds.astype(BF16)
                dw += _dot(dsb, vb, NT)
                dsg += ds
                dv_ref[rows, cols] = _dot(wm, dsb, TN)
            dws_ref[g] += dw
            dsum[:, cols] += dsg

        @pl.when(i == nsteps - 1)
        def _():
            for g in range(GMLP_GROUPS):
                dws_ref[g] = jnp.where(mask, dws_ref[g], 0.0)
                tot = jnp.sum(dsum[:, g * gd:(g + 1) * gd], axis=-1, keepdims=True)
                dbs_ref[g] = jnp.broadcast_to(tot, (GMLP_BLOCK, LANES))

    tile = pl.BlockSpec((SG_ROWS, w), lambda i: (i, 0))
    return pl.pallas_call(
        body, name="gate_bwd", grid=(nsteps,),
        in_specs=[tile, tile, tile, pl.BlockSpec(ws.shape, lambda i: (0, 0, 0)), pl.BlockSpec(bst.shape, lambda i: (0, 0))],
        out_specs=[tile, tile, pl.BlockSpec(ws.shape, lambda i: (0, 0, 0)),
                   pl.BlockSpec((GMLP_GROUPS, GMLP_BLOCK, LANES), lambda i: (0, 0, 0))],
        out_shape=[jax.ShapeDtypeStruct((s, w), F32), jax.ShapeDtypeStruct((s, w), F32),
                   jax.ShapeDtypeStruct(ws.shape, F32), jax.ShapeDtypeStruct((GMLP_GROUPS, GMLP_BLOCK, LANES), F32)],
        scratch_shapes=[pltpu.VMEM((GMLP_BLOCK, w), F32)],
        compiler_params=_params(("arbitrary",)),
    )(dgated, h, vn, ws, bst)


GB_ROWS = 256


def _gmlp_in_bwd(h, du, dvn, g3, layer):
    s, w2 = h.shape
    w = w2 // 2
    nsteps = s // GB_ROWS

    def body(h_ref, du_ref, dv_ref, g_ref, dh_ref, dg_ref, db_ref):
        i = pl.program_id(0)

        @pl.when(i == 0)
        def _():
            dg_ref[...] = jnp.zeros_like(dg_ref)
            db_ref[...] = jnp.zeros_like(db_ref)

        hu = h_ref[:, :w]
        hv = h_ref[:, w:]
        dh_ref[:, :w] = (du_ref[...] * _gelu_grad(hu)).astype(BF16)
        z = jax.nn.gelu(hv)
        mu = jnp.mean(z, axis=-1, keepdims=True)
        d = z - mu
        var = jnp.mean(d * d, axis=-1, keepdims=True)
        rstd = lax.rsqrt(var + LN_EPS)
        xhat = d * rstd
        dy = dv_ref[...]
        db_ref[...] += jnp.sum(dy, axis=0, keepdims=True)
        dg_ref[...] += jnp.sum(dy * xhat, axis=0, keepdims=True)
        dxh = dy * g_ref[...]
        m1 = jnp.mean(dxh, axis=-1, keepdims=True)
        m2 = jnp.mean(dxh * xhat, axis=-1, keepdims=True)
        dz = rstd * (dxh - m1 - xhat * m2)
        dh_ref[:, w:] = (dz * _gelu_grad(hv)).astype(BF16)

    half = pl.BlockSpec((GB_ROWS, w), lambda i: (i, 0))
    vec = pl.BlockSpec((1, w), lambda i: (0, 0))
    return pl.pallas_call(
        body, name="gmlp_in_bwd", grid=(nsteps,),
        in_specs=[pl.BlockSpec((GB_ROWS, w2), lambda i: (i, 0)), half, half,
                  pl.BlockSpec((None, 1, w), lambda i: (layer, 0, 0))],
        out_specs=[pl.BlockSpec((GB_ROWS, w2), lambda i: (i, 0)), vec, vec],
        out_shape=[jax.ShapeDtypeStruct((s, w2), BF16), jax.ShapeDtypeStruct((1, w), F32), jax.ShapeDtypeStruct((1, w), F32)],
        compiler_params=_params(("arbitrary",)),
    )(h, du, dvn, g3)


def _ln_bwd(dy, xhat, rstd, g3, layer):
    s, d = dy.shape
    nsteps = s // TM

    def body(dy_ref, xh_ref, rs_ref, g_ref, dr_ref, drb_ref, dg_ref, db_ref):
        i = pl.program_id(0)

        @pl.when(i == 0)
        def _():
            dg_ref[...] = jnp.zeros_like(dg_ref)
            db_ref[...] = jnp.zeros_like(db_ref)

        dyv = dy_ref[...]
        xhat_v = xh_ref[...]
        db_ref[...] += jnp.sum(dyv, axis=0, keepdims=True)
        dg_ref[...] += jnp.sum(dyv * xhat_v, axis=0, keepdims=True)
        dxh = dyv * g_ref[...]
        m1 = jnp.mean(dxh, axis=-1, keepdims=True)
        m2 = jnp.mean(dxh * xhat_v, axis=-1, keepdims=True)
        dr = rs_ref[...] * (dxh - m1 - xhat_v * m2)
        dr_ref[...] = dr
        drb_ref[...] = dr.astype(BF16)

    tile = pl.BlockSpec((TM, d), lambda i: (i, 0))
    vec = pl.BlockSpec((1, d), lambda i: (0, 0))
    return pl.pallas_call(
        body, name="ln_bwd", grid=(nsteps,),
        in_specs=[tile, tile, pl.BlockSpec((TM, 1), lambda i: (i, 0)), pl.BlockSpec((None, 1, d), lambda i: (layer, 0, 0))],
        out_specs=[tile, tile, vec, vec],
        out_shape=[jax.ShapeDtypeStruct((s, d), F32), jax.ShapeDtypeStruct((s, d), BF16),
                   jax.ShapeDtypeStruct((1, d), F32), jax.ShapeDtypeStruct((1, d), F32)],
        compiler_params=_params(("arbitrary",)),
    )(dy, xhat, rstd, g3)


def _loss_head(y, target):
    s, d = y.shape

    def body(y_ref, t_ref, dy_ref, l_ref):
        i = pl.program_id(0)

        @pl.when(i == 0)
        def _():
            l_ref[...] = jnp.zeros_like(l_ref)

        e = y_ref[...] - t_ref[...]
        dy_ref[...] = e * (1.0 / d)
        l_ref[...] += jnp.sum(jnp.sum(e * e, axis=1, keepdims=True), axis=0, keepdims=True)

    tile = pl.BlockSpec((TM, d), lambda i: (i, 0))
    return pl.pallas_call(
        body, name="loss_head", grid=(s // TM,), in_specs=[tile, tile],
        out_specs=[tile, pl.BlockSpec((1, 1), lambda i: (0, 0))],
        out_shape=[jax.ShapeDtypeStruct((s, d), F32), jax.ShapeDtypeStruct((1, 1), F32)],
        compiler_params=_params(("arbitrary",)),
    )(y, target)


LOG2E = 1.4426950408889634
DEAD_LOG2 = -160.0
FIRST_LANE = 1


def _sb_terms(z, causal):
    z2 = z * LOG2E
    e = jnp.exp2(-jnp.abs(z2))
    l1p = jnp.log2(1.0 + e)
    lb = jnp.minimum(z2, 0.0) - l1p
    lr = lb - z2
    if causal is not None:
        lr = jnp.where(causal, lr, 0.0)
    return lb, lr, e


def _split_hi_lo(x):
    hi = x.astype(BF16)
    lo = (x - hi.astype(F32)).astype(BF16)
    return jnp.concatenate([hi, lo], axis=1)


def _att_consts(prefix):
    r = lax.broadcasted_iota(jnp.int32, (2 * ATT_T, ATT_T), 0) % ATT_T
    c = lax.broadcasted_iota(jnp.int32, (2 * ATT_T, ATT_T), 1)
    tri2 = jnp.where((r <= c) if prefix else (r >= c), 1.0, 0.0).astype(BF16)
    r = lax.broadcasted_iota(jnp.int32, (ATT_T, ATT_T), 0)
    c = lax.broadcasted_iota(jnp.int32, (ATT_T, ATT_T), 1)
    causal = c < r
    head_a = lax.broadcasted_iota(jnp.int32, (1, LANES), 1) < HEAD_DIM
    return tri2, causal, head_a


def _attn_fwd(q, k, v):
    s, d = q.shape
    nq = s // ATT_T

    def body(q_ref, k_ref, v_ref, ob_ref, lsum_ref, acc_a, acc_b, rem_a, rem_b):
        i = pl.program_id(1)
        tri, causal, head_a = _att_consts(prefix=False)
        q2 = q_ref[...]
        zero = jnp.zeros_like(q2)
        qa = jnp.where(head_a, q2, zero)
        qb = jnp.where(head_a, zero, q2)
        acc_a[...] = jnp.zeros_like(acc_a)
        acc_b[...] = jnp.zeros_like(acc_b)
        rem_a[...] = jnp.zeros_like(rem_a)
        rem_b[...] = jnp.zeros_like(rem_b)

        def block(kb, mask):
            rows = pl.ds(pl.multiple_of(kb * ATT_T, ATT_T), ATT_T)
            k2 = k_ref[rows, :]
            v2 = v_ref[rows, :]
            heads = ((qa, acc_a, rem_a), (qb, acc_b, rem_b))
            zs = [_dot(qm, k2, NT) for qm, _, _ in heads]
            terms = [_sb_terms(z, mask) for z in zs]
            sums = [_dot(_split_hi_lo(lr), tri, NN) for _, lr, _ in terms]
            for (_, acc, rem), (lb, lr, _), sincl in zip(heads, terms, sums):
                a = jnp.exp2(lb + (sincl - lr) + rem[...])
                if mask is not None:
                    a = jnp.where(mask, a, 0.0)
                rem[...] += sincl[:, 0:1]
                acc[...] += _dot(a.astype(BF16), v2, NN)

        block(i, causal)

        def live():
            return jnp.maximum(jnp.max(rem_a[...]), jnp.max(rem_b[...])) > DEAD_LOG2

        def go_on(carry):
            t, alive = carry
            return (t < i) & alive

        def step(carry):
            t, _ = carry
            block(i - 1 - t, None)
            return t + 1, live()

        done, _ = lax.while_loop(go_on, step, (jnp.int32(0), live()))
        first = (i - done).astype(F32)
        ob_ref[...] = jnp.where(head_a, acc_a[...], acc_b[...]).astype(BF16)
        lane = lax.broadcasted_iota(jnp.int32, (1, LANES), 1)
        lsum_ref[...] = jnp.where(lane == FIRST_LANE, first, jnp.where(head_a, rem_a[...], rem_b[...]))

    qspec = pl.BlockSpec((ATT_T, LANES), lambda p, i: (i, p))
    kspec = pl.BlockSpec((s, LANES), lambda p, i: (0, p))
    return pl.pallas_call(
        body, name="attn_fwd", grid=(d // LANES, nq), in_specs=[qspec, kspec, kspec],
        out_specs=[qspec, qspec],
        out_shape=[jax.ShapeDtypeStruct((s, d), BF16), jax.ShapeDtypeStruct((s, d), F32)],
        scratch_shapes=[pltpu.VMEM((ATT_T, LANES), F32), pltpu.VMEM((ATT_T, LANES), F32),
                        pltpu.VMEM((ATT_T, 1), F32), pltpu.VMEM((ATT_T, 1), F32)],
        compiler_params=_params(("parallel", "arbitrary")),
    )(q, k, v)


def _attn_bwd(q, k, v, do, lsum, dk_prev=None, dv_prev=None):
    s, d = q.shape
    nq = s // ATT_T
    has_prev = dk_prev is not None

    def body(*refs):
        q_ref, k_ref, v_ref, do_ref, ls_ref = refs[:5]
        n_in = 7 if has_prev else 5
        dq_ref, dk_ref, dv_ref, acc_a, acc_b, pre_a, pre_b, gp_a, gp_b, dkt, dvt = refs[n_in:]
        i = pl.program_id(1)

        @pl.when(i == 0)
        def _():
            dkt[...] = jnp.zeros_like(dkt)
            dvt[...] = jnp.zeros_like(dvt)

        tri, causal, head_a = _att_consts(prefix=True)
        q2 = q_ref[...]
        zero = jnp.zeros_like(q2)
        qa = jnp.where(head_a, q2, zero)
        qb = jnp.where(head_a, zero, q2)
        do2 = do_ref[...]
        doa = jnp.where(head_a, do2, 0.0).astype(BF16)
        dob = jnp.where(head_a, 0.0, do2).astype(BF16)
        row_a = lax.broadcasted_iota(jnp.int32, (LANES, 1), 0) < HEAD_DIM
        qt = q2.astype(F32).T
        dot_ = do2.T
        qta, qtb = jnp.where(row_a, qt, 0.0).astype(BF16), jnp.where(row_a, 0.0, qt).astype(BF16)
        dota, dotb = jnp.where(row_a, dot_, 0.0).astype(BF16), jnp.where(row_a, 0.0, dot_).astype(BF16)
        ls2 = ls_ref[...]
        tot_a = ls2[:, 0:1]
        tot_b = ls2[:, HEAD_DIM:HEAD_DIM + 1]
        for r in (acc_a, acc_b, pre_a, pre_b, gp_a, gp_b):
            r[...] = jnp.zeros_like(r)

        def block(kb, mask):
            rows = pl.ds(pl.multiple_of(kb * ATT_T, ATT_T), ATT_T)
            k2 = k_ref[rows, :]
            v2 = v_ref[rows, :]
            dk_new = jnp.zeros((LANES, ATT_T), F32)
            dv_new = jnp.zeros((LANES, ATT_T), F32)
            heads = ((qa, doa, tot_a, acc_a, pre_a, gp_a, qta, dota), (qb, dob, tot_b, acc_b, pre_b, gp_b, qtb, dotb))
            zs = [_dot(h[0], k2, NT) for h in heads]
            das = [_dot(h[1], v2, NT) for h in heads]
            terms = [_sb_terms(z, mask) for z in zs]
            psums = [_dot(_split_hi_lo(lr), tri, NN) for _, lr, _ in terms]
            gs, abs_ = [], []
            for h, (lb, _, _), pincl, da in zip(heads, terms, psums, das):
                tot, pre = h[2], h[4]
                a = jnp.exp2(lb + (tot - (pre[...] + pincl)))
                if mask is not None:
                    a = jnp.where(mask, a, 0.0)
                pre[...] += pincl[:, ATT_T - 1:ATT_T]
                gs.append(a * da)
                abs_.append(a.astype(BF16))
            gsums = [_dot(g.astype(BF16), tri[:ATT_T], NN) for g in gs]
            dzs = []
            for h, z, (_, _, e), g, gincl in zip(heads, zs, terms, gs, gsums):
                gpre = h[5]
                gbefore = gpre[...] + (gincl - g)
                gpre[...] += gincl[:, ATT_T - 1:ATT_T]
                inv = 1.0 / (1.0 + e)
                beta = jnp.where(z >= 0.0, inv, e * inv)
                dz = g - beta * (g + gbefore)
                if mask is not None:
                    dz = jnp.where(mask, dz, 0.0)
                dzs.append(dz.astype(BF16))
            for h, ab, dzb in zip(heads, abs_, dzs):
                dv_new += _dot(h[7], ab, NN)
                dk_new += _dot(h[6], dzb, NN)
                h[3][...] += _dot(dzb, k2, NN)
            cols = pl.ds(pl.multiple_of(kb * ATT_T, ATT_T), ATT_T)
            dkt[:, cols] += dk_new
            dvt[:, cols] += dv_new

        def step(kb, carry):
            block(kb, None)
            return carry

        first = jnp.clip(jnp.max(ls2[:, FIRST_LANE:FIRST_LANE + 1]).astype(jnp.int32), 0, i)
        lax.fori_loop(first, i, step, 0)
        block(i, causal)
        dq_ref[...] = (jnp.where(head_a, acc_a[...], acc_b[...]) * (HEAD_DIM ** -0.5)).astype(BF16)

        @pl.when(i == nq - 1)
        def _():
            for n in range(nq):
                rows = slice(n * ATT_T, (n + 1) * ATT_T)
                dkn, dvn = dkt[:, rows].T, dvt[:, rows].T
                if has_prev:
                    dkn, dvn = dkn + refs[5][rows, :], dvn + refs[6][rows, :]
                dk_ref[rows, :] = dkn
                dv_ref[rows, :] = dvn

    qspec = pl.BlockSpec((ATT_T, LANES), lambda p, i: (i, p))
    kspec = pl.BlockSpec((s, LANES), lambda p, i: (0, p))
    ins = [q, k, v, do, lsum] + ([dk_prev, dv_prev] if has_prev else [])
    return pl.pallas_call(
        body, name="attn_bwd", grid=(d // LANES, nq),
        in_specs=[qspec, kspec, kspec, qspec, qspec] + ([kspec, kspec] if has_prev else []),
        out_specs=[qspec, kspec, kspec],
        out_shape=[jax.ShapeDtypeStruct((s, d), BF16), jax.ShapeDtypeStruct((s, d), F32), jax.ShapeDtypeStruct((s, d), F32)],
        scratch_shapes=[pltpu.VMEM((ATT_T, LANES), F32), pltpu.VMEM((ATT_T, LANES), F32)]
        + [pltpu.VMEM((ATT_T, 1), F32)] * 4 + [pltpu.VMEM((LANES, s), F32)] * 2,
        compiler_params=_params(("parallel", "arbitrary")),
    )(*ins)


def _place():
    x, y, c = lax.axis_index("x"), lax.axis_index("y"), lax.axis_index("c")
    chips = [(1 - x, y), (x, 1 - y), (1 - x, 1 - y)]
    return x, y, c, chips


def _any_specs(n):
    return [pl.BlockSpec(memory_space=pl.ANY)] * n


def _gather_weights(bufs):
    n = len(bufs)

    def body(*refs):
        outs = refs[n:2 * n]
        send_sems, recv_sems = refs[2 * n:]
        x, y, c, chips = _place()
        me = 2 * x + y
        sibling = (x, y, 1 - c)

        def half(a, blk, hc):
            h = outs[a].shape[1] // 2
            return outs[a].at[blk, pl.ds(hc * h, h)]

        def copy(a, k, part, to):
            return pltpu.make_async_remote_copy(src_ref=part, dst_ref=part, send_sem=send_sems.at[a, k],
                                                recv_sem=recv_sems.at[a, k], device_id=to, device_id_type=MESH)

        sent = []
        for a in range(n):
            for k, chip in enumerate(chips):
                sent.append(copy(a, k, half(a, me, c), (*chip, c)))
                sent[-1].start()
        for a in range(n):
            for k, chip in enumerate(chips):
                blk = 2 * chip[0] + chip[1]
                copy(a, k, half(a, blk, c), sibling).wait_recv()
                sent.append(copy(a, 3 + k, half(a, blk, c), sibling))
                sent[-1].start()
        for a in range(n):
            for k, chip in enumerate(chips):
                blk = 2 * chip[0] + chip[1]
                copy(a, 3 + k, half(a, blk, 1 - c), sibling).wait_recv()
        for cp in sent:
            cp.wait_send()

    return pl.pallas_call(
        body, name="gather_weights", in_specs=_any_specs(n), out_specs=_any_specs(n),
        out_shape=[jax.ShapeDtypeStruct(w.shape, w.dtype) for w in bufs],
        input_output_aliases={a: a for a in range(n)},
        scratch_shapes=[pltpu.SemaphoreType.DMA((n, 6)), pltpu.SemaphoreType.DMA((n, 6))],
        compiler_params=pltpu.CompilerParams(has_side_effects=True),
    )(*bufs)


def _pair_exchange(grads):
    n = len(grads)

    def body(*refs):
        ins, outs = refs[:n], refs[n:2 * n]
        send_sems, recv_sems = refs[2 * n:]
        x, y, c, _ = _place()
        cps = []
        for a in range(n):
            h = ins[a].shape[1] // 2
            cps.append(pltpu.make_async_remote_copy(
                src_ref=ins[a].at[:, pl.ds((1 - c) * h, h)], dst_ref=outs[a], send_sem=send_sems.at[a],
                recv_sem=recv_sems.at[a], device_id=(x, y, 1 - c), device_id_type=MESH))
            cps[-1].start()
        for cp in cps:
            cp.wait()

    return pl.pallas_call(
        body, name="pair_exchange", in_specs=_any_specs(n), out_specs=_any_specs(n),
        out_shape=[jax.ShapeDtypeStruct((g.shape[0], g.shape[1] // 2, g.shape[2]), g.dtype) for g in grads],
        scratch_shapes=[pltpu.SemaphoreType.DMA((n,)), pltpu.SemaphoreType.DMA((n,))],
        compiler_params=pltpu.CompilerParams(has_side_effects=True),
    )(*grads)


def _chip_exchange(parts):
    n = len(parts)

    def body(*refs):
        ins, outs = refs[:n], refs[n:2 * n]
        send_sems, recv_sems = refs[2 * n:]
        x, y, c, chips = _place()
        me = 2 * x + y
        cps = []
        for a in range(n):
            for k, chip in enumerate(chips):
                blk = 2 * chip[0] + chip[1]
                cps.append(pltpu.make_async_remote_copy(
                    src_ref=ins[a].at[blk], dst_ref=outs[a].at[me], send_sem=send_sems.at[a, k],
                    recv_sem=recv_sems.at[a, k], device_id=(*chip, c), device_id_type=MESH))
                cps[-1].start()
        for a in range(n):
            for k, chip in enumerate(chips):
                blk = 2 * chip[0] + chip[1]
                pltpu.make_async_remote_copy(
                    src_ref=ins[a].at[blk], dst_ref=outs[a].at[blk], send_sem=send_sems.at[a, k],
                    recv_sem=recv_sems.at[a, k], device_id=(*chip, c), device_id_type=MESH).wait_recv()
        for cp in cps:
            cp.wait_send()

    return pl.pallas_call(
        body, name="chip_exchange", in_specs=_any_specs(n), out_specs=_any_specs(n),
        out_shape=[jax.ShapeDtypeStruct(p.shape, p.dtype) for p in parts],
        scratch_shapes=[pltpu.SemaphoreType.DMA((n, 3)), pltpu.SemaphoreType.DMA((n, 3))],
        compiler_params=pltpu.CompilerParams(has_side_effects=True),
    )(*parts)


def _half_swap(halves):
    n = len(halves)

    def body(*refs):
        outs = refs[n:2 * n]
        send_sems, recv_sems = refs[2 * n:]
        x, y, c, _ = _place()
        cps = []
        for a in range(n):
            h = outs[a].shape[1] // 2
            mine = outs[a].at[:, pl.ds(c * h, h)]
            cps.append(pltpu.make_async_remote_copy(
                src_ref=mine, dst_ref=mine, send_sem=send_sems.at[a], recv_sem=recv_sems.at[a],
                device_id=(x, y, 1 - c), device_id_type=MESH))
            cps[-1].start()
        for cp in cps:
            cp.wait()

    return pl.pallas_call(
        body, name="half_swap", in_specs=_any_specs(n), out_specs=_any_specs(n),
        out_shape=[jax.ShapeDtypeStruct(p.shape, p.dtype) for p in halves],
        input_output_aliases={a: a for a in range(n)},
        scratch_shapes=[pltpu.SemaphoreType.DMA((n,)), pltpu.SemaphoreType.DMA((n,))],
        compiler_params=pltpu.CompilerParams(has_side_effects=True),
    )(*halves)


N_DEV = 8


def _all_reduce_small(v):
    nrow, ncol = v.shape

    def body(v_ref, o_ref, land, red, send_sems, recv_sems, send2, recv2, loc_sem):
        x, y, c, _ = _place()
        me = 4 * x + 2 * y + c
        peers = []
        for k in range(1, N_DEV):
            peers.append((x ^ ((k >> 2) & 1), y ^ ((k >> 1) & 1), c ^ (k & 1)))
        own = pltpu.make_async_copy(v_ref.at[pl.ds(me, 1)], land.at[pl.ds(me, 1)], loc_sem)
        own.start()
        cps = []
        for k, peer in enumerate(peers):
            dev = 4 * peer[0] + 2 * peer[1] + peer[2]
            cps.append(pltpu.make_async_remote_copy(
                src_ref=v_ref.at[pl.ds(dev, 1)], dst_ref=land.at[pl.ds(me, 1)], send_sem=send_sems.at[k],
                recv_sem=recv_sems.at[k], device_id=peer, device_id_type=MESH))
            cps[-1].start()
        for k, peer in enumerate(peers):
            dev = 4 * peer[0] + 2 * peer[1] + peer[2]
            pltpu.make_async_remote_copy(
                src_ref=v_ref.at[pl.ds(dev, 1)], dst_ref=land.at[pl.ds(dev, 1)], send_sem=send_sems.at[k],
                recv_sem=recv_sems.at[k], device_id=peer, device_id_type=MESH).wait_recv()
        for cp in cps:
            cp.wait_send()
        own.wait()
        terms = land[...]
        total = terms[0:1, :]
        for d in range(1, N_DEV):
            total = total + terms[d:d + 1, :]
        red[...] = total
        own = pltpu.make_async_copy(red, o_ref.at[pl.ds(me, 1)], loc_sem)
        own.start()
        cps = []
        for k, peer in enumerate(peers):
            cps.append(pltpu.make_async_remote_copy(
                src_ref=red, dst_ref=o_ref.at[pl.ds(me, 1)], send_sem=send2.at[k],
                recv_sem=recv2.at[k], device_id=peer, device_id_type=MESH))
            cps[-1].start()
        for k, peer in enumerate(peers):
            dev = 4 * peer[0] + 2 * peer[1] + peer[2]
            pltpu.make_async_remote_copy(
                src_ref=red, dst_ref=o_ref.at[pl.ds(dev, 1)], send_sem=send2.at[k],
                recv_sem=recv2.at[k], device_id=peer, device_id_type=MESH).wait_recv()
        for cp in cps:
            cp.wait_send()
        own.wait()

    vm = pl.BlockSpec(memory_space=pltpu.VMEM)
    return pl.pallas_call(
        body, name="all_reduce_small", in_specs=[vm], out_specs=vm,
        out_shape=jax.ShapeDtypeStruct((nrow, ncol), F32),
        scratch_shapes=[pltpu.VMEM((nrow, ncol), F32), pltpu.VMEM((1, ncol), F32)]
        + [pltpu.SemaphoreType.DMA((N_DEV - 1,))] * 4 + [pltpu.SemaphoreType.DMA],
        compiler_params=pltpu.CompilerParams(has_side_effects=True, vmem_limit_bytes=VMEM_LIMIT),
    )(v)


def _row_tile(rows):
    return min(rows, 512)


def _pair_sum(g, got, place):
    nb, r, c = g.shape
    h = r // 2
    tr = _row_tile(h)
    nt = h // tr

    def body(place_ref, g_ref, got_ref, p_ref, pb_ref):
        p = g_ref[...] + got_ref[...]
        pb_ref[...] = p.astype(BF16)

        @pl.when(pl.program_id(1) == place_ref[0])
        def _():
            p_ref[...] = p

    spec = pl.BlockSpec((None, tr, c), lambda t, j, place_ref: (j, t, 0))
    grid_spec = pltpu.PrefetchScalarGridSpec(
        num_scalar_prefetch=1, grid=(nt, nb),
        in_specs=[pl.BlockSpec((None, tr, c), lambda t, j, place_ref: (j, place_ref[1] * nt + t, 0)), spec],
        out_specs=[pl.BlockSpec((tr, c), lambda t, j, place_ref: (t, 0)), spec])
    return pl.pallas_call(
        body, name="pair_sum", grid_spec=grid_spec,
        out_shape=[jax.ShapeDtypeStruct((h, c), F32), jax.ShapeDtypeStruct((nb, h, c), BF16)],
        compiler_params=_params(("parallel", "arbitrary")),
    )(place, g, got)


def _chip_sum(p, got, place, out, layer):
    h, c = p.shape
    tr = _row_tile(h)
    nt = h // tr

    def body(place_ref, p_ref, g1_ref, g2_ref, g3_ref, old_ref, o_ref):
        o_ref[...] = ((p_ref[...] + g1_ref[...].astype(F32)) + g2_ref[...].astype(F32)) + g3_ref[...].astype(F32)

    def blk(off):
        return pl.BlockSpec((None, tr, c), lambda t, place_ref: ((place_ref[0] + off) % N_CHIPS, t, 0))

    grid_spec = pltpu.PrefetchScalarGridSpec(
        num_scalar_prefetch=1, grid=(nt,),
        in_specs=[pl.BlockSpec((tr, c), lambda t, place_ref: (t, 0)), blk(1), blk(2), blk(3),
                  pl.BlockSpec(memory_space=pl.ANY)],
        out_specs=pl.BlockSpec((None, tr, c), lambda t, place_ref: (layer, place_ref[1] * nt + t, 0)))
    return pl.pallas_call(
        body, name="chip_sum", grid_spec=grid_spec, out_shape=jax.ShapeDtypeStruct(out.shape, F32),
        input_output_aliases={5: 0}, compiler_params=_params(("parallel",)),
    )(place, p, got, got, got, out)


def _adamw(w, g, m, v):
    r, c = w.shape
    tr = r if r < 8 else _row_tile(r)

    def body(w_ref, g_ref, m_ref, v_ref, d_ref, nm_ref, nv_ref):
        gv = g_ref[...]
        nm = ADAM_B1 * m_ref[...] + (1.0 - ADAM_B1) * gv
        nv = ADAM_B2 * v_ref[...] + (1.0 - ADAM_B2) * (gv * gv)
        m_hat = nm / (1.0 - ADAM_B1 ** ADAM_STEP)
        v_hat = nv / (1.0 - ADAM_B2 ** ADAM_STEP)
        d_ref[...] = -ADAM_LR * (m_hat / (jnp.sqrt(v_hat) + ADAM_EPS) + ADAM_WD * w_ref[...])
        nm_ref[...] = nm
        nv_ref[...] = nv

    tile = pl.BlockSpec((tr, c), lambda i: (i, 0))
    return pl.pallas_call(
        body, name="adamw", grid=(r // tr,), in_specs=[tile] * 4, out_specs=[tile] * 3,
        out_shape=[jax.ShapeDtypeStruct((r, c), F32)] * 3, compiler_params=_params(("parallel",)),
    )(w, g, m, v)


BIG = ("a_w_in", "a_w_out", "sb_w_k", "sb_w_v", "b_w_q", "b_w_o", "ffn_w1", "ffn_w2")
SMALL = ("a_ln_g", "a_ln_b", "a_w_s", "a_b_s", "mix_ln_g", "mix_ln_b", "ffn_ln_g", "ffn_ln_b")
COL_SHARDED = {"a_w_in": True, "a_w_out": False, "sb_w_k": False, "sb_w_v": False, "b_w_q": False, "b_w_o": False,
               "ffn_w1": True, "ffn_w2": False}


def kernel(x, a_w_in, a_ln_g, a_ln_b, a_w_s, a_b_s, a_w_out, sb_w_k, sb_w_v, b_w_q, b_w_o, mix_ln_g, mix_ln_b, ffn_ln_g, ffn_ln_b, ffn_w1, ffn_w2, loss_target, m_a_w_in, m_a_ln_g, m_a_ln_b, m_a_w_s, m_a_b_s, m_a_w_out, m_sb_w_k, m_sb_w_v, m_b_w_q, m_b_w_o, m_mix_ln_g, m_mix_ln_b, m_ffn_ln_g, m_ffn_ln_b, m_ffn_w1, m_ffn_w2, v_a_w_in, v_a_ln_g, v_a_ln_b, v_a_w_s, v_a_b_s, v_a_w_out, v_sb_w_k, v_sb_w_v, v_b_w_q, v_b_w_o, v_mix_ln_g, v_mix_ln_b, v_ffn_ln_g, v_ffn_ln_b, v_ffn_w1, v_ffn_w2):
    names = BIG + SMALL
    given = dict(a_w_in=a_w_in, a_ln_g=a_ln_g, a_ln_b=a_ln_b, a_w_s=a_w_s, a_b_s=a_b_s, a_w_out=a_w_out, sb_w_k=sb_w_k,
                 sb_w_v=sb_w_v, b_w_q=b_w_q, b_w_o=b_w_o, mix_ln_g=mix_ln_g, mix_ln_b=mix_ln_b, ffn_ln_g=ffn_ln_g,
                 ffn_ln_b=ffn_ln_b, ffn_w1=ffn_w1, ffn_w2=ffn_w2)
    mom = dict(a_w_in=m_a_w_in, a_ln_g=m_a_ln_g, a_ln_b=m_a_ln_b, a_w_s=m_a_w_s, a_b_s=m_a_b_s, a_w_out=m_a_w_out,
               sb_w_k=m_sb_w_k, sb_w_v=m_sb_w_v, b_w_q=m_b_w_q, b_w_o=m_b_w_o, mix_ln_g=m_mix_ln_g, mix_ln_b=m_mix_ln_b,
               ffn_ln_g=m_ffn_ln_g, ffn_ln_b=m_ffn_ln_b, ffn_w1=m_ffn_w1, ffn_w2=m_ffn_w2)
    var = dict(a_w_in=v_a_w_in, a_ln_g=v_a_ln_g, a_ln_b=v_a_ln_b, a_w_s=v_a_w_s, a_b_s=v_a_b_s, a_w_out=v_a_w_out,
               sb_w_k=v_sb_w_k, sb_w_v=v_sb_w_v, b_w_q=v_b_w_q, b_w_o=v_b_w_o, mix_ln_g=v_mix_ln_g, mix_ln_b=v_mix_ln_b,
               ffn_ln_g=v_ffn_ln_g, ffn_ln_b=v_ffn_ln_b, ffn_w1=v_ffn_w1, ffn_w2=v_ffn_w2)

    cx, cy, cc = lax.axis_index("x"), lax.axis_index("y"), lax.axis_index("c")
    chip = (2 * cx + cy).astype(jnp.int32)
    chip_arr = chip.reshape(1)

    s, d = x.shape[1], x.shape[2]
    xf = x.reshape(s, d)
    target = loss_target.reshape(s, d)

    def as2d(w):
        return w.reshape(-1, w.shape[-1])

    gw = {}
    for n in BIG:
        for l in ([None] if given[n].ndim == 2 else range(given[n].shape[0])):
            gw[(n, l)] = _cast_into_slot(given[n], l, chip_arr)
    ln_gb = jnp.stack([a_ln_g, a_ln_b])
    ln_slot = lax.dynamic_update_slice(jnp.zeros((N_CHIPS,) + ln_gb.shape, F32), ln_gb[None], (chip, 0, 0, 0))
    layer0 = [("a_w_in", 0), ("a_w_out", 0)]
    gathered = _gather_weights([gw[k] for k in layer0] + [ln_slot])
    gw.update(zip(layer0, gathered[:-1]))
    mixer = {1: [("a_w_in", 1), ("a_w_out", 1)], 2: [("sb_w_k", None), ("sb_w_v", None), ("b_w_q", 0), ("b_w_o", 0)],
             3: [("b_w_q", 1), ("b_w_o", 1)]}

    def riding(d2d=(), ici=()):
        keys = list(d2d) + list(ici)
        return keys, [("d2d", gw[k]) for k in d2d] + [("ici", gw[k]) for k in ici]

    def landed_in(keys, bufs):
        gw.update(zip(keys, bufs))

    ln_full = gathered[-1].transpose(1, 2, 0, 3).reshape(2, N_A, 1, -1)
    a_ln_g3, a_ln_b3 = ln_full[0], ln_full[1]
    mix_g3, mix_b3 = mix_ln_g[:, None, :], mix_ln_b[:, None, :]
    ffn_g3, ffn_b3 = ffn_ln_g[:, None, :], ffn_ln_b[:, None, :]
    bst = jnp.swapaxes(a_b_s, 1, 2)

    saved = []
    xb = _cast_bf16(xf)
    kb = vb = None
    for l in range(DEPTH):
        sv = dict(x_in=xb)
        last = l == DEPTH - 1
        if l == 0:
            keys, riders = riding(ici=[("ffn_w1", 0)])
        else:
            keys, riders = riding(d2d=[("ffn_w1", l), ("ffn_w2", l)], ici=mixer[l + 1] if l < N_A else [])
        if l < N_A:
            h, *bufs = _mm_fwd("a_in", xb, gw[("a_w_in", l)], None, True, riders=riders)
            landed_in(keys, bufs)
            vn = _gmlp_norm_fwd(h, a_ln_g3, a_ln_b3, l)
            gated = _gate_fwd(h, vn, a_w_s[l], bst[l])
            keys, riders = riding(d2d=[("ffn_w1", 0)], ici=[("ffn_w2", 0)]) if l == 0 else ([], [])
            xf, xb, xhat, rstd, *bufs = _mm_resid_ln("a_out", gated, gw[("a_w_out", l)], xf, mix_g3, mix_b3, l, riders)
            landed_in(keys, bufs)
            sv.update(h=h, vn=vn, gated=gated)
        else:
            j = l - N_A
            if l == N_A:
                kb, *bufs = _mm_fwd("sb_k", xb, gw[("sb_w_k", None)], None, False, _ep_bf16, outs=[(d, BF16)],
                                    riders=riders)
                landed_in(keys, bufs)
                keys, riders = [], ()
                vb = _mm_fwd("sb_v", xb, gw[("sb_w_v", None)], None, False, _ep_bf16, outs=[(d, BF16)])[0]
            q, *bufs = _mm_fwd("b_q", xb, gw[("b_w_q", j)], None, False, _ep_scale_q, outs=[(d, BF16)], riders=riders)
            landed_in(keys, bufs)
            ob, lsum = _attn_fwd(q, kb, vb)
            keys, riders = riding(ici=[] if last else mixer[l + 1])
            xf, xb, xhat, rstd, *bufs = _mm_resid_ln("b_out", ob, gw[("b_w_o", j)], xf, mix_g3, mix_b3, l, riders)
            landed_in(keys, bufs)
            sv.update(q=q, lsum=lsum, ob=ob)
        sv.update(x_mid=xb, xhat1=xhat, rstd1=rstd)
        dff = gw[("ffn_w1", l)].shape[-1] * N_CHIPS
        if l == 0:
            keys, riders = riding(d2d=[("ffn_w2", 0)], ici=mixer[1] + [("ffn_w1", 1)])
        else:
            keys, riders = riding(ici=[] if last else [("ffn_w1", l + 1)])
        pr, act, *bufs = _mm_fwd("ffn_1", xb, gw[("ffn_w1", l)], None, True, _ep_relu2,
                                 outs=[(dff, BF16), (dff, BF16)], riders=riders)
        landed_in(keys, bufs)
        keys, riders = riding(d2d=[] if last else mixer[l + 1], ici=[] if last else [("ffn_w2", l + 1)])
        xf, xb, xhat, rstd, *bufs = _mm_resid_ln("ffn_2", act, gw[("ffn_w2", l)], xf, ffn_g3, ffn_b3, l, riders)
        landed_in(keys, bufs)
        sv.update(pr=pr, act=act, xhat2=xhat, rstd2=rstd)
        saved.append(sv)

    dx, sq = _loss_head(xf, target)
    loss = lax.psum(0.5 * sq[0, 0] / d, ("x", "y", "c"))

    pending = []
    pair_sums, landed = {}, {}
    place_arr = jnp.stack([chip, cc.astype(jnp.int32)])

    def arrived(took, outs):
        for (kind, key, arr), out in zip(took, outs):
            if kind == "pair":
                pair_sums[key] = _pair_sum(arr, out, place_arr)
                pending.append(("chip", key, pair_sums[key][1]))
            else:
                landed[key] = out

    def carrying(call, name, *args, **kw):
        took = []
        if name.startswith("ffn") or draining[0]:
            room = CARRIER_PARAMS
            for task in list(pending):
                size = given[task[1][0]].shape[-2] * given[task[1][0]].shape[-1] * N_CHIPS
                if task[0] == "pair" or room == CARRIER_PARAMS or size <= room:
                    took.append(task)
                    pending.remove(task)
                    room -= size if task[0] == "chip" else 0
        out, *rest = call(name, *args, riders=[(kind, arr) for kind, _, arr in took], **kw)
        arrived(took, rest)
        return out

    draining = [False]

    def bwd_act(*args, **kw):
        return carrying(_mm_bwd_act, *args, **kw)

    def bwd_w(key, name, a, dy):
        pending.append(("pair", key, carrying(_mm_bwd_w, name, a, dy, gw[key], COL_SHARDED[key[0]])))

    d_mix_g, d_mix_b, d_ffn_g, d_ffn_b = [None] * DEPTH, [None] * DEPTH, [None] * DEPTH, [None] * DEPTH
    d_ln_g, d_ln_b, d_ws, d_bs = [None] * N_A, [None] * N_A, [None] * N_A, [None] * N_A
    dk = dv = None
    for l in reversed(range(DEPTH)):
        sv = saved[l]
        draining[0] = l == 0
        dr, drb, d_ffn_g[l], d_ffn_b[l] = _ln_bwd(dx, sv["xhat2"], sv["rstd2"], ffn_g3, l)
        dff = sv["pr"].shape[1]
        dhd = bwd_act("ffn_2_dx", drb, gw[("ffn_w2", l)], None, False, _ep_relu2_bwd, (sv["pr"],),
                      (pl.BlockSpec((_wide_tile(s), dff // N_CHIPS), lambda j, i, k: (i, j)),), out_dtype=BF16)
        bwd_w(("ffn_w2", l), "ffn_2_dw", sv["act"], drb)
        dx = bwd_act("ffn_1_dx", dhd, gw[("ffn_w1", l)], None, True, _ep_resid, (dr,), (_row_spec(d),))
        bwd_w(("ffn_w1", l), "ffn_1_dw", sv["x_mid"], dhd)

        dr, drb, d_mix_g[l], d_mix_b[l] = _ln_bwd(dx, sv["xhat1"], sv["rstd1"], mix_g3, l)
        quarter = pl.BlockSpec((_wide_tile(s), d // N_CHIPS), lambda j, i, k: (i, j))
        if l < N_A:
            dgated = bwd_act("a_out_dx", drb, gw[("a_w_out", l)], None, False)
            bwd_w(("a_w_out", l), "a_out_dw", sv["gated"], drb)
            du, dvn, d_ws[l], dbs_wide = _gate_bwd(dgated, sv["h"], sv["vn"], a_w_s[l], bst[l])
            d_bs[l] = dbs_wide[:, :, 0]
            dh, dlg, dlb = _gmlp_in_bwd(sv["h"], du, dvn, a_ln_g3, l)
            d_ln_g[l], d_ln_b[l] = dlg[0], dlb[0]
            dx = bwd_act("a_in_dx", dh, gw[("a_w_in", l)], None, True, _ep_resid, (dr,), (_row_spec(d),))
            bwd_w(("a_w_in", l), "a_in_dw", sv["x_in"], dh)
        else:
            j = l - N_A
            do = bwd_act("b_out_dx", drb, gw[("b_w_o", j)], None, False)
            bwd_w(("b_w_o", j), "b_out_dw", sv["ob"], drb)
            dq, dk, dv = _attn_bwd(sv["q"], kb, vb, do, sv["lsum"], dk, dv)
            dx = bwd_act("b_q_dx", dq, gw[("b_w_q", j)], None, False, _ep_resid, (dr,), (quarter,))
            bwd_w(("b_w_q", j), "b_q_dw", sv["x_in"], dq)
            if l == N_A:
                dx = bwd_act("sb_k_dx", dk, gw[("sb_w_k", None)], None, False, _ep_add, (dx,), (quarter,))
                bwd_w(("sb_w_k", None), "sb_k_dw", sv["x_in"], dk)
                dx = bwd_act("sb_v_dx", dv, gw[("sb_w_v", None)], None, False, _ep_add, (dx,), (quarter,))
                bwd_w(("sb_w_v", None), "sb_v_dw", sv["x_in"], dv)
    grad_x = dx.reshape(x.shape)

    while pending:
        took = list(pending)
        pending.clear()
        for kind, exchange in (("pair", _pair_exchange), ("chip", _chip_exchange)):
            some = [t for t in took if t[0] == kind]
            if some:
                arrived(some, exchange([arr for _, _, arr in some]))

    stacked = []
    for n in BIG:
        layers = [None] if given[n].ndim == 2 else range(given[n].shape[0])
        out = lax.empty((len(layers),) + given[n].shape[-2:], F32)
        for at, l in enumerate(layers):
            out = _chip_sum(pair_sums[(n, l)][0], landed[(n, l)], place_arr, out, at)
        stacked.append(out)
    grads = {n: g.reshape(given[n].shape) for n, g in zip(BIG, _half_swap(stacked))}

    small_full = dict(a_ln_g=jnp.stack(d_ln_g), a_ln_b=jnp.stack(d_ln_b), a_w_s=jnp.stack(d_ws), a_b_s=jnp.stack(d_bs),
                      mix_ln_g=jnp.concatenate(d_mix_g), mix_ln_b=jnp.concatenate(d_mix_b),
                      ffn_ln_g=jnp.concatenate(d_ffn_g), ffn_ln_b=jnp.concatenate(d_ffn_b))
    packed = jnp.concatenate([small_full[n].reshape(-1) for n in SMALL])
    total = packed.shape[0]
    ncol = -(-total // (N_DEV * LANES)) * LANES
    packed = jnp.pad(packed, (0, N_DEV * ncol - total)).reshape(N_DEV, ncol)
    reduced = _all_reduce_small(packed).reshape(-1)
    off = 0
    for n in SMALL:
        size = small_full[n].size
        g = reduced[off:off + size].reshape(small_full[n].shape)
        off += size
        if n in ("a_ln_g", "a_ln_b"):
            wq = given[n].shape[1]
            g = lax.dynamic_slice_in_dim(g, chip * wq, wq, axis=1)
        grads[n] = g

    delta, new_m, new_v = {}, {}, {}
    for n in names:
        shape = given[n].shape
        dl, nm, nv = _adamw(as2d(given[n]), as2d(grads[n]), as2d(mom[n]), as2d(var[n]))
        delta[n], new_m[n], new_v[n] = dl.reshape(shape), nm.reshape(shape), nv.reshape(shape)

    order = ("a_w_in", "a_ln_g", "a_ln_b", "a_w_s", "a_b_s", "a_w_out", "sb_w_k", "sb_w_v", "b_w_q", "b_w_o",
             "mix_ln_g", "mix_ln_b", "ffn_ln_g", "ffn_ln_b", "ffn_w1", "ffn_w2")
    return (loss, grad_x, *[grads[n] for n in order], *[delta[n] for n in order],
            *[new_m[n] for n in order], *[new_v[n] for n in order])
```

```python
import math

import jax
import jax.numpy as jnp
from jax import lax
from jax.experimental import pallas as pl
from jax.experimental.pallas import tpu as pltpu

F32 = jnp.float32
BF16 = jnp.bfloat16
MESH = pl.DeviceIdType.MESH

N_CHIPS = 4
DEPTH = 4
N_A = 2
ALPHA = float((2 * DEPTH) ** 0.25)
LN_EPS = 1e-5
CHUNK = 64
GMLP_BLOCK = 128
GMLP_GROUPS = 8
HEAD_DIM = 64
LANES = 128
ATT_T = 256
ADAM_LR = 0.001
ADAM_B1 = 0.9
ADAM_B2 = 0.999
ADAM_EPS = 1e-08
ADAM_WD = 0.01
ADAM_STEP = 10
VMEM_LIMIT = 56 * 1024 * 1024
TM = 512
TM_WIDE = 1024
TS = 1024

NN = ((1,), (0,))
NT = ((1,), (1,))
TN = ((0,), (0,))


def _params(sem):
    return pltpu.CompilerParams(dimension_semantics=sem, vmem_limit_bytes=VMEM_LIMIT)


def _dot(a, b, contract):
    return lax.dot_general(a, b, (contract, ((), ())), preferred_element_type=F32)


def _rider_out(kind, arr):
    shape = (arr.shape[0], arr.shape[1] // 2, arr.shape[2]) if kind == "pair" else arr.shape
    return jax.ShapeDtypeStruct(shape, arr.dtype)


def _rider_copies(kind, src, dst, send_sems, recv_sems, base):
    x, y, c, chips = _place()
    me = 2 * x + y
    sibling = (x, y, 1 - c)

    def copy(k, part, land, to):
        return pltpu.make_async_remote_copy(src_ref=part, dst_ref=land, send_sem=send_sems.at[base + k],
                                            recv_sem=recv_sems.at[base + k], device_id=to, device_id_type=MESH)

    if kind == "pair":
        h = src.shape[1] // 2
        cp = copy(0, src.at[:, pl.ds((1 - c) * h, h)], dst, sibling)
        return [cp], [cp]
    h = dst.shape[1] // 2
    starts, arrivals = [], []
    for k, chip in enumerate(chips):
        blk = 2 * chip[0] + chip[1]
        if kind == "ici":
            starts.append(copy(k, dst.at[me, pl.ds(c * h, h)], dst.at[me, pl.ds(c * h, h)], (*chip, c)))
            arrivals.append(copy(k, dst.at[blk, pl.ds(c * h, h)], dst.at[blk, pl.ds(c * h, h)], (*chip, c)))
        elif kind == "d2d":
            starts.append(copy(k, dst.at[blk, pl.ds(c * h, h)], dst.at[blk, pl.ds(c * h, h)], sibling))
            arrivals.append(copy(k, dst.at[blk, pl.ds((1 - c) * h, h)], dst.at[blk, pl.ds((1 - c) * h, h)], sibling))
        else:
            starts.append(copy(k, src.at[blk], dst.at[me], (*chip, c)))
            arrivals.append(copy(k, src.at[blk], dst.at[blk], (*chip, c)))
    return starts, arrivals


RIDER_SEMS = 3
CARRIER_PARAMS = 5 * 2 ** 20


def _identity(a):
    return a


def _square(a):
    return a * a


def _matmul(name, operands, in_specs, out_shapes, out_specs, grid, contract, epilogue, acc_shape, aliases=None,
            chunks=None, riders=(), pick=False, a_fn=_identity):
    nk = grid[2]
    n_in, n_out, nr = len(operands), len(out_shapes), len(riders)
    n_plain = n_in + nr + n_out

    def body(*refs):
        ins, outs = refs[:n_in], refs[n_in + nr:n_plain]
        if nr:
            srcs, dsts = refs[n_in:n_in + nr], refs[n_plain:n_plain + nr]
            send_sems, recv_sems = refs[-2:]
            pid = [pl.program_id(ax) for ax in range(3)]
            first = (pid[0] == 0) & (pid[1] == 0) & (pid[2] == 0)
            last = (pid[0] == grid[0] - 1) & (pid[1] == grid[1] - 1) & (pid[2] == grid[2] - 1)

            def copies(n):
                return _rider_copies(riders[n][0], srcs[n], dsts[n], send_sems, recv_sems, RIDER_SEMS * n)

            @pl.when(first)
            def _():
                for n in range(nr):
                    for cp in copies(n)[0]:
                        cp.start()

        compute(refs, ins, outs)
        if nr:
            @pl.when(last)
            def _():
                for n in range(nr):
                    starts, arrivals = copies(n)
                    for cp in arrivals:
                        cp.wait_recv()
                    for cp in starts:
                        cp.wait_send()

    def compute(refs, ins, outs):
        if chunks is None:
            b = ins[1][pl.program_id(1)] if pick else ins[1][...]
            p = _dot(a_fn(ins[0][...].astype(BF16)), b.astype(BF16), contract)
        else:
            width = ins[0].shape[1] // chunks
            p = None
            for j in range(chunks):
                pj = _dot(a_fn(ins[0][:, j * width:(j + 1) * width].astype(BF16)), ins[1][j].astype(BF16), contract)
                p = pj if p is None else p + pj
        if nk == 1:
            epilogue(p, ins[2:], outs)
            return
        acc = refs[n_plain + nr]
        k = pl.program_id(2)

        @pl.when(k == 0)
        def _():
            acc[...] = p

        @pl.when((k > 0) & (k < nk - 1))
        def _():
            acc[...] += p

        @pl.when(k == nk - 1)
        def _():
            epilogue(acc[...] + p, ins[2:], outs)

    rbufs = [b for _, b in riders]
    in_place = {n_in + n: n_out + n for n, (kind, _) in enumerate(riders) if kind in ("ici", "d2d")}
    scratch = ([] if nk == 1 else [pltpu.VMEM(acc_shape, F32)]) \
        + [pltpu.SemaphoreType.DMA((RIDER_SEMS * nr,))] * (2 if nr else 0)
    return pl.pallas_call(
        body, name=name, grid=grid, in_specs=list(in_specs) + _any_specs(nr), out_specs=list(out_specs) + _any_specs(nr),
        out_shape=list(out_shapes) + [_rider_out(kind, b) for kind, b in riders],
        scratch_shapes=scratch,
        input_output_aliases={**(aliases or {}), **in_place},
        compiler_params=_params(("arbitrary",) * 3 if nr else ("parallel", "parallel", "arbitrary")),
    )(*operands, *rbufs)


def _wspec(w, layer, whole=False):
    r, c = w.shape[-2:]
    lead = N_CHIPS if whole else None
    if w.ndim == 4:
        return pl.BlockSpec((lead, None, r, c), lambda j, i, k: (0 if whole else j, layer, 0, 0))
    return pl.BlockSpec((lead, r, c), lambda j, i, k: (0 if whole else j, 0, 0))


def _wide_tile(s):
    return min(TM_WIDE, s)


def _ep_store(p, ins, outs):
    for o in outs:
        o[...] = p.astype(o.dtype)


def _rows_first(spec):
    return pl.BlockSpec(spec.block_shape, lambda i, j, k: spec.index_map(j, i, k))


def _mm_fwd(name, a, w, layer, col_sharded, epilogue=_ep_store, extras=(), extra_specs=(), outs=None, riders=(),
            a_fn=_identity):
    s = a.shape[0]
    r, c = w.shape[-2:]
    if col_sharded:
        tm = _wide_tile(s)
        grid = (s // tm, N_CHIPS, 1)
        a_spec = pl.BlockSpec((tm, r), lambda j, i, k: (i, 0))
        n_out = N_CHIPS * c
    else:
        tm = TM
        grid = (1, s // tm, 1)
        a_spec = pl.BlockSpec((tm, N_CHIPS * r), lambda j, i, k: (i, 0))
        n_out = c
    if outs is None:
        outs = [(n_out, F32)]
    out_shapes = [jax.ShapeDtypeStruct((s, n), dt) for n, dt in outs]
    out_specs = [pl.BlockSpec((tm, c if n == n_out else n), lambda j, i, k: (i, j)) for n, _ in outs]
    in_specs = [a_spec, _wspec(w, layer, True)] + list(extra_specs)
    if col_sharded:
        in_specs, out_specs = [_rows_first(sp) for sp in in_specs], [_rows_first(sp) for sp in out_specs]
    return _matmul(name, (a, w) + tuple(extras), in_specs, out_shapes, out_specs, grid, NN, epilogue, (tm, c),
                   chunks=None if col_sharded else N_CHIPS, riders=riders, pick=col_sharded, a_fn=a_fn)


def _mm_bwd_act(name, dy, w, layer, col_sharded, epilogue=_ep_store, extras=(), extra_specs=(), out_dtype=F32,
                riders=()):
    s = dy.shape[0]
    r, c = w.shape[-2:]
    if col_sharded:
        tm = TM
        grid = (1, s // tm, 1)
        a_spec = pl.BlockSpec((tm, N_CHIPS * c), lambda j, i, k: (i, 0))
        n_out = r
    else:
        tm = _wide_tile(s)
        grid = (s // tm, N_CHIPS, 1)
        a_spec = pl.BlockSpec((tm, c), lambda j, i, k: (i, 0))
        n_out = N_CHIPS * r
    in_specs = [a_spec, _wspec(w, layer, True)] + list(extra_specs)
    o_spec = pl.BlockSpec((tm, r), lambda j, i, k: (i, j))
    if not col_sharded:
        in_specs, o_spec = [_rows_first(sp) for sp in in_specs], _rows_first(o_spec)
    return _matmul(name, (dy, w) + tuple(extras), in_specs,
                   [jax.ShapeDtypeStruct((s, n_out), out_dtype)], [o_spec], grid, NT, epilogue, (tm, r),
                   chunks=N_CHIPS if col_sharded else None, riders=riders, pick=not col_sharded)


def _mm_bwd_w(name, a, dy, w, col_sharded, riders=(), a_fn=_identity):
    s = a.shape[0]
    r, c = w.shape[-2:]
    ts = min(TS, s)
    grid = (N_CHIPS, 1, s // ts)
    if col_sharded:
        a_spec = pl.BlockSpec((ts, r), lambda j, i, k: (k, 0))
        b_spec = pl.BlockSpec((ts, c), lambda j, i, k: (k, j))
    else:
        a_spec = pl.BlockSpec((ts, r), lambda j, i, k: (k, j))
        b_spec = pl.BlockSpec((ts, c), lambda j, i, k: (k, 0))

    def epilogue(p, ins, outs):
        outs[0][...] = p

    return _matmul(name, (a, dy), [a_spec, b_spec], [jax.ShapeDtypeStruct(w.shape, F32)], [_wspec(w, None)], grid, TN,
                   epilogue, (r, c), riders=riders, a_fn=a_fn)


def _row_spec(n):
    return pl.BlockSpec((TM, n), lambda j, i, k: (i, 0))


def _vec_spec(layer, n):
    return pl.BlockSpec((None, 1, n), lambda j, i, k: (layer, 0, 0))


def _ep_resid_ln(p, ins, outs):
    x_ref, g_ref, b_ref = ins
    xf_ref, xb_ref, xhat_ref, rstd_ref = outs
    r = ALPHA * x_ref[...] + p
    mu = jnp.mean(r, axis=-1, keepdims=True)
    d = r - mu
    var = jnp.mean(d * d, axis=-1, keepdims=True)
    rstd = lax.rsqrt(var + LN_EPS)
    xhat = d * rstd
    y = xhat * g_ref[...] + b_ref[...]
    xf_ref[...] = y
    xb_ref[...] = y.astype(BF16)
    xhat_ref[...] = xhat
    rstd_ref[...] = rstd


def _mm_resid_ln(name, a, w, x, g3, b3, ln_layer, riders=(), a_fn=_identity):
    d = x.shape[1]
    return _mm_fwd(name, a, w, None, False, _ep_resid_ln, (x, g3, b3),
                   (_row_spec(d), _vec_spec(ln_layer, d), _vec_spec(ln_layer, d)),
                   outs=[(d, F32), (d, BF16), (d, F32), (1, F32)], riders=riders, a_fn=a_fn)


def _ep_relu(p, ins, outs):
    outs[0][...] = jnp.maximum(p, 0.0).astype(BF16)


def _ep_scale_q(p, ins, outs):
    outs[0][...] = (p * (HEAD_DIM ** -0.5)).astype(BF16)


def _ep_bf16(p, ins, outs):
    outs[0][...] = p.astype(BF16)


def _ep_relu2_bwd(p, ins, outs):
    outs[0][...] = (p * (2.0 * ins[0][...].astype(F32))).astype(BF16)


def _ep_resid(p, ins, outs):
    outs[0][...] = ALPHA * ins[0][...] + p


def _ep_add(p, ins, outs):
    outs[0][...] = ins[0][...] + p


def _gelu_grad(x):
    c0 = math.sqrt(2.0 / math.pi)
    t = jnp.tanh(c0 * (x + 0.044715 * (x * x * x)))
    return 0.5 * (1.0 + t) + (0.5 * x) * (1.0 - t * t) * (c0 * (1.0 + 3.0 * 0.044715 * (x * x)))


def _cast_bf16(w2d):
    r, c = w2d.shape
    tr = min(r, 512)

    def body(w_ref, o_ref):
        o_ref[...] = w_ref[...].astype(BF16)

    return pl.pallas_call(
        body, name="cast_bf16", grid=(r // tr,),
        in_specs=[pl.BlockSpec((tr, c), lambda i: (i, 0))], out_specs=pl.BlockSpec((tr, c), lambda i: (i, 0)),
        out_shape=jax.ShapeDtypeStruct((r, c), BF16), compiler_params=_params(("parallel",)),
    )(w2d)


def _cast_into_slot(w, layer, chip):
    r, c = w.shape[-2:]
    tr = min(r, 512)

    def body(chip_ref, w_ref, o_ref):
        o_ref[...] = w_ref[...].astype(BF16)

    if layer is None:
        w_spec = pl.BlockSpec((tr, c), lambda i, chip_ref: (i, 0))
    else:
        w_spec = pl.BlockSpec((None, tr, c), lambda i, chip_ref: (layer, i, 0))
    grid_spec = pltpu.PrefetchScalarGridSpec(
        num_scalar_prefetch=1, grid=(r // tr,), in_specs=[w_spec],
        out_specs=pl.BlockSpec((None, tr, c), lambda i, chip_ref: (chip_ref[0], i, 0)))
    return pl.pallas_call(
        body, name="cast_into_slot", grid_spec=grid_spec,
        out_shape=jax.ShapeDtypeStruct((N_CHIPS, r, c), BF16), compiler_params=_params(("parallel",)),
    )(chip, w)


def _gmlp_norm_fwd(h, g3, b3, layer):
    s, w2 = h.shape
    w = w2 // 2

    def body(h_ref, g_ref, b_ref, o_ref):
        z = jax.nn.gelu(h_ref[...])
        mu = jnp.mean(z, axis=-1, keepdims=True)
        d = z - mu
        var = jnp.mean(d * d, axis=-1, keepdims=True)
        o_ref[...] = (d * lax.rsqrt(var + LN_EPS) * g_ref[...] + b_ref[...]).astype(BF16)

    vec = pl.BlockSpec((None, 1, w), lambda i: (layer, 0, 0))
    return pl.pallas_call(
        body, name="gmlp_norm_fwd", grid=(s // TM,),
        in_specs=[pl.BlockSpec((TM, w), lambda i: (i, 1)), vec, vec],
        out_specs=pl.BlockSpec((TM, w), lambda i: (i, 0)),
        out_shape=jax.ShapeDtypeStruct((s, w), BF16), compiler_params=_params(("parallel",)),
    )(h, g3, b3)


def _chunk_mask():
    t = lax.broadcasted_iota(jnp.int32, (GMLP_BLOCK, GMLP_BLOCK), 0)
    s = lax.broadcasted_iota(jnp.int32, (GMLP_BLOCK, GMLP_BLOCK), 1)
    return (s // CHUNK) <= (t // CHUNK)


SG_ROWS = 512


def _gate_fwd(h, vn, ws, bst):
    s, w = vn.shape
    gd = w // GMLP_GROUPS

    def body(h_ref, v_ref, ws_ref, bs_ref, o_ref):
        mask = _chunk_mask()
        for g in range(GMLP_GROUPS):
            wm = jnp.where(mask, ws_ref[g], 0.0).astype(BF16)
            bias = bs_ref[:, g:g + 1]
            cols = slice(g * gd, (g + 1) * gd)
            for n in range(SG_ROWS // GMLP_BLOCK):
                rows = slice(n * GMLP_BLOCK, (n + 1) * GMLP_BLOCK)
                sp = _dot(wm, v_ref[rows, cols], NN) + bias
                o_ref[rows, cols] = (jax.nn.gelu(h_ref[rows, cols]) * sp).astype(BF16)

    return pl.pallas_call(
        body, name="gate_fwd", grid=(s // SG_ROWS,),
        in_specs=[pl.BlockSpec((SG_ROWS, w), lambda i: (i, 0)), pl.BlockSpec((SG_ROWS, w), lambda i: (i, 0)),
                  pl.BlockSpec(ws.shape, lambda i: (0, 0, 0)), pl.BlockSpec(bst.shape, lambda i: (0, 0))],
        out_specs=pl.BlockSpec((SG_ROWS, w), lambda i: (i, 0)),
        out_shape=jax.ShapeDtypeStruct((s, w), BF16), compiler_params=_params(("parallel",)),
    )(h, vn, ws, bst)


def _gate_bwd(dgated, h, vn, ws, bst):
    s, w = vn.shape
    gd = w // GMLP_GROUPS
    nsteps = s // SG_ROWS

    def body(dg_ref, h_ref, v_ref, ws_ref, bs_ref, du_ref, dv_ref, dws_ref, dbs_ref, dsum):
        i = pl.program_id(0)

        @pl.when(i == 0)
        def _():
            dws_ref[...] = jnp.zeros_like(dws_ref)
            dsum[...] = jnp.zeros_like(dsum)

        mask = _chunk_mask()
        for g in range(GMLP_GROUPS):
            wm = jnp.where(mask, ws_ref[g], 0.0).astype(BF16)
            bias = bs_ref[:, g:g + 1]
            cols = slice(g * gd, (g + 1) * gd)
            dw = jnp.zeros((GMLP_BLOCK, GMLP_BLOCK), F32)
            dsg = jnp.zeros((GMLP_BLOCK, gd), F32)
            for n in range(SG_ROWS // GMLP_BLOCK):
                rows = slice(n * GMLP_BLOCK, (n + 1) * GMLP_BLOCK)
                vb = v_ref[rows, cols]
                sp = _dot(wm, vb, NN) + bias
                dg = dg_ref[rows, cols]
                du_ref[rows, cols] = dg * sp
                ds = dg * jax.nn.gelu(h_ref[rows, cols])
                dsb = ds.astype(BF16)
                dw += _dot(dsb, vb, NT)
                dsg += ds
                dv_ref[rows, cols] = _dot(wm, dsb, TN)
            dws_ref[g] += dw
            dsum[:, cols] += dsg

        @pl.when(i == nsteps - 1)
        def _():
            for g in range(GMLP_GROUPS):
                dws_ref[g] = jnp.where(mask, dws_ref[g], 0.0)
                tot = jnp.sum(dsum[:, g * gd:(g + 1) * gd], axis=-1, keepdims=True)
                dbs_ref[g] = jnp.broadcast_to(tot, (GMLP_BLOCK, LANES))

    tile = pl.BlockSpec((SG_ROWS, w), lambda i: (i, 0))
    return pl.pallas_call(
        body, name="gate_bwd", grid=(nsteps,),
        in_specs=[tile, tile, tile, pl.BlockSpec(ws.shape, lambda i: (0, 0, 0)), pl.BlockSpec(bst.shape, lambda i: (0, 0))],
        out_specs=[tile, tile, pl.BlockSpec(ws.shape, lambda i: (0, 0, 0)),
                   pl.BlockSpec((GMLP_GROUPS, GMLP_BLOCK, LANES), lambda i: (0, 0, 0))],
        out_shape=[jax.ShapeDtypeStruct((s, w), F32), jax.ShapeDtypeStruct((s, w), F32),
                   jax.ShapeDtypeStruct(ws.shape, F32), jax.ShapeDtypeStruct((GMLP_GROUPS, GMLP_BLOCK, LANES), F32)],
        scratch_shapes=[pltpu.VMEM((GMLP_BLOCK, w), F32)],
        compiler_params=_params(("arbitrary",)),
    )(dgated, h, vn, ws, bst)


GB_ROWS = 256


def _gmlp_in_bwd(h, du, dvn, g3, layer):
    s, w2 = h.shape
    w = w2 // 2
    nsteps = s // GB_ROWS

    def body(h_ref, du_ref, dv_ref, g_ref, dh_ref, dg_ref, db_ref):
        i = pl.program_id(0)

        @pl.when(i == 0)
        def _():
            dg_ref[...] = jnp.zeros_like(dg_ref)
            db_ref[...] = jnp.zeros_like(db_ref)

        hu = h_ref[:, :w]
        hv = h_ref[:, w:]
        dh_ref[:, :w] = (du_ref[...] * _gelu_grad(hu)).astype(BF16)
        z = jax.nn.gelu(hv)
        mu = jnp.mean(z, axis=-1, keepdims=True)
        d = z - mu
        var = jnp.mean(d * d, axis=-1, keepdims=True)
        rstd = lax.rsqrt(var + LN_EPS)
        xhat = d * rstd
        dy = dv_ref[...]
        db_ref[...] += jnp.sum(dy, axis=0, keepdims=True)
        dg_ref[...] += jnp.sum(dy * xhat, axis=0, keepdims=True)
        dxh = dy * g_ref[...]
        m1 = jnp.mean(dxh, axis=-1, keepdims=True)
        m2 = jnp.mean(dxh * xhat, axis=-1, keepdims=True)
        dz = rstd * (dxh - m1 - xhat * m2)
        dh_ref[:, w:] = (dz * _gelu_grad(hv)).astype(BF16)

    half = pl.BlockSpec((GB_ROWS, w), lambda i: (i, 0))
    vec = pl.BlockSpec((1, w), lambda i: (0, 0))
    return pl.pallas_call(
        body, name="gmlp_in_bwd", grid=(nsteps,),
        in_specs=[pl.BlockSpec((GB_ROWS, w2), lambda i: (i, 0)), half, half,
                  pl.BlockSpec((None, 1, w), lambda i: (layer, 0, 0))],
        out_specs=[pl.BlockSpec((GB_ROWS, w2), lambda i: (i, 0)), vec, vec],
        out_shape=[jax.ShapeDtypeStruct((s, w2), BF16), jax.ShapeDtypeStruct((1, w), F32), jax.ShapeDtypeStruct((1, w), F32)],
        compiler_params=_params(("arbitrary",)),
    )(h, du, dvn, g3)


def _ln_bwd(dy, xhat, rstd, g3, layer):
    s, d = dy.shape
    nsteps = s // TM

    def body(dy_ref, xh_ref, rs_ref, g_ref, dr_ref, drb_ref, dg_ref, db_ref):
        i = pl.program_id(0)

        @pl.when(i == 0)
        def _():
            dg_ref[...] = jnp.zeros_like(dg_ref)
            db_ref[...] = jnp.zeros_like(db_ref)

        dyv = dy_ref[...]
        xhat_v = xh_ref[...]
        db_ref[...] += jnp.sum(dyv, axis=0, keepdims=True)
        dg_ref[...] += jnp.sum(dyv * xhat_v, axis=0, keepdims=True)
        dxh = dyv * g_ref[...]
        m1 = jnp.mean(dxh, axis=-1, keepdims=True)
        m2 = jnp.mean(dxh * xhat_v, axis=-1, keepdims=True)
        dr = rs_ref[...] * (dxh - m1 - xhat_v * m2)
        dr_ref[...] = dr
        drb_ref[...] = dr.astype(BF16)

    tile = pl.BlockSpec((TM, d), lambda i: (i, 0))
    vec = pl.BlockSpec((1, d), lambda i: (0, 0))
    return pl.pallas_call(
        body, name="ln_bwd", grid=(nsteps,),
        in_specs=[tile, tile, pl.BlockSpec((TM, 1), lambda i: (i, 0)), pl.BlockSpec((None, 1, d), lambda i: (layer, 0, 0))],
        out_specs=[tile, tile, vec, vec],
        out_shape=[jax.ShapeDtypeStruct((s, d), F32), jax.ShapeDtypeStruct((s, d), BF16),
                   jax.ShapeDtypeStruct((1, d), F32), jax.ShapeDtypeStruct((1, d), F32)],
        compiler_params=_params(("arbitrary",)),
    )(dy, xhat, rstd, g3)


def _loss_head(y, target):
    s, d = y.shape

    def body(y_ref, t_ref, dy_ref, l_ref):
        i = pl.program_id(0)

        @pl.when(i == 0)
        def _():
            l_ref[...] = jnp.zeros_like(l_ref)

        e = y_ref[...] - t_ref[...]
        dy_ref[...] = e * (1.0 / d)
        l_ref[...] += jnp.sum(jnp.sum(e * e, axis=1, keepdims=True), axis=0, keepdims=True)

    tile = pl.BlockSpec((TM, d), lambda i: (i, 0))
    return pl.pallas_call(
        body, name="loss_head", grid=(s // TM,), in_specs=[tile, tile],
        out_specs=[tile, pl.BlockSpec((1, 1), lambda i: (0, 0))],
        out_shape=[jax.ShapeDtypeStruct((s, d), F32), jax.ShapeDtypeStruct((1, 1), F32)],
        compiler_params=_params(("arbitrary",)),
    )(y, target)


LOG2E = 1.4426950408889634
DEAD_LOG2 = -160.0
FIRST_LANE = 1


def _sb_terms(z, causal):
    z2 = z * LOG2E
    e = jnp.exp2(-jnp.abs(z2))
    l1p = jnp.log2(1.0 + e)
    lb = jnp.minimum(z2, 0.0) - l1p
    lr = lb - z2
    if causal is not None:
        lr = jnp.where(causal, lr, 0.0)
    return lb, lr, e


def _split_hi_lo(x):
    hi = x.astype(BF16)
    lo = (x - hi.astype(F32)).astype(BF16)
    return jnp.concatenate([hi, lo], axis=1)


def _att_consts(prefix):
    r = lax.broadcasted_iota(jnp.int32, (2 * ATT_T, ATT_T), 0) % ATT_T
    c = lax.broadcasted_iota(jnp.int32, (2 * ATT_T, ATT_T), 1)
    tri2 = jnp.where((r <= c) if prefix else (r >= c), 1.0, 0.0).astype(BF16)
    r = lax.broadcasted_iota(jnp.int32, (ATT_T, ATT_T), 0)
    c = lax.broadcasted_iota(jnp.int32, (ATT_T, ATT_T), 1)
    causal = c < r
    head_a = lax.broadcasted_iota(jnp.int32, (1, LANES), 1) < HEAD_DIM
    return tri2, causal, head_a


def _attn_fwd(q, k, v):
    s, d = q.shape
    nq = s // ATT_T

    def body(q_ref, k_ref, v_ref, ob_ref, lsum_ref, acc_a, acc_b, rem_a, rem_b):
        i = pl.program_id(1)
        tri, causal, head_a = _att_consts(prefix=False)
        q2 = q_ref[...]
        zero = jnp.zeros_like(q2)
        qa = jnp.where(head_a, q2, zero)
        qb = jnp.where(head_a, zero, q2)
        acc_a[...] = jnp.zeros_like(acc_a)
        acc_b[...] = jnp.zeros_like(acc_b)
        rem_a[...] = jnp.zeros_like(rem_a)
        rem_b[...] = jnp.zeros_like(rem_b)

        def block(kb, mask):
            rows = pl.ds(pl.multiple_of(kb * ATT_T, ATT_T), ATT_T)
            k2 = k_ref[rows, :]
            v2 = v_ref[rows, :]
            heads = ((qa, acc_a, rem_a), (qb, acc_b, rem_b))
            zs = [_dot(qm, k2, NT) for qm, _, _ in heads]
            terms = [_sb_terms(z, mask) for z in zs]
            sums = [_dot(_split_hi_lo(lr), tri, NN) for _, lr, _ in terms]
            for (_, acc, rem), (lb, lr, _), sincl in zip(heads, terms, sums):
                a = jnp.exp2(lb + (sincl - lr) + rem[...])
                if mask is not None:
                    a = jnp.where(mask, a, 0.0)
                rem[...] += sincl[:, 0:1]
                acc[...] += _dot(a.astype(BF16), v2, NN)

        block(i, causal)

        def live():
            return jnp.maximum(jnp.max(rem_a[...]), jnp.max(rem_b[...])) > DEAD_LOG2

        def go_on(carry):
            t, alive = carry
            return (t < i) & alive

        def step(carry):
            t, _ = carry
            block(i - 1 - t, None)
            return t + 1, live()

        done, _ = lax.while_loop(go_on, step, (jnp.int32(0), live()))
        first = (i - done).astype(F32)
        ob_ref[...] = jnp.where(head_a, acc_a[...], acc_b[...]).astype(BF16)
        lane = lax.broadcasted_iota(jnp.int32, (1, LANES), 1)
        lsum_ref[...] = jnp.where(lane == FIRST_LANE, first, jnp.where(head_a, rem_a[...], rem_b[...]))

    qspec = pl.BlockSpec((ATT_T, LANES), lambda p, i: (i, p))
    kspec = pl.BlockSpec((s, LANES), lambda p, i: (0, p))
    return pl.pallas_call(
        body, name="attn_fwd", grid=(d // LANES, nq), in_specs=[qspec, kspec, kspec],
        out_specs=[qspec, qspec],
        out_shape=[jax.ShapeDtypeStruct((s, d), BF16), jax.ShapeDtypeStruct((s, d), F32)],
        scratch_shapes=[pltpu.VMEM((ATT_T, LANES), F32), pltpu.VMEM((ATT_T, LANES), F32),
                        pltpu.VMEM((ATT_T, 1), F32), pltpu.VMEM((ATT_T, 1), F32)],
        compiler_params=_params(("parallel", "arbitrary")),
    )(q, k, v)


def _attn_bwd(q, k, v, do, lsum, dk_prev=None, dv_prev=None):
    s, d = q.shape
    nq = s // ATT_T
    has_prev = dk_prev is not None

    def body(*refs):
        q_ref, k_ref, v_ref, do_ref, ls_ref = refs[:5]
        n_in = 7 if has_prev else 5
        dq_ref, dk_ref, dv_ref, acc_a, acc_b, pre_a, pre_b, gp_a, gp_b, dkt, dvt = refs[n_in:]
        i = pl.program_id(1)

        @pl.when(i == 0)
        def _():
            dkt[...] = jnp.zeros_like(dkt)
            dvt[...] = jnp.zeros_like(dvt)

        tri, causal, head_a = _att_consts(prefix=True)
        q2 = q_ref[...]
        zero = jnp.zeros_like(q2)
        qa = jnp.where(head_a, q2, zero)
        qb = jnp.where(head_a, zero, q2)
        do2 = do_ref[...]
        doa = jnp.where(head_a, do2, 0.0).astype(BF16)
        dob = jnp.where(head_a, 0.0, do2).astype(BF16)
        row_a = lax.broadcasted_iota(jnp.int32, (LANES, 1), 0) < HEAD_DIM
        qt = q2.astype(F32).T
        dot_ = do2.T
        qta, qtb = jnp.where(row_a, qt, 0.0).astype(BF16), jnp.where(row_a, 0.0, qt).astype(BF16)
        dota, dotb = jnp.where(row_a, dot_, 0.0).astype(BF16), jnp.where(row_a, 0.0, dot_).astype(BF16)
        ls2 = ls_ref[...]
        tot_a = ls2[:, 0:1]
        tot_b = ls2[:, HEAD_DIM:HEAD_DIM + 1]
        for r in (acc_a, acc_b, pre_a, pre_b, gp_a, gp_b):
            r[...] = jnp.zeros_like(r)

        def block(kb, mask):
            rows = pl.ds(pl.multiple_of(kb * ATT_T, ATT_T), ATT_T)
            k2 = k_ref[rows, :]
            v2 = v_ref[rows, :]
            dk_new = jnp.zeros((LANES, ATT_T), F32)
            dv_new = jnp.zeros((LANES, ATT_T), F32)
            heads = ((qa, doa, tot_a, acc_a, pre_a, gp_a, qta, dota), (qb, dob, tot_b, acc_b, pre_b, gp_b, qtb, dotb))
            zs = [_dot(h[0], k2, NT) for h in heads]
            das = [_dot(h[1], v2, NT) for h in heads]
            terms = [_sb_terms(z, mask) for z in zs]
            psums = [_dot(_split_hi_lo(lr), tri, NN) for _, lr, _ in terms]
            gs, abs_ = [], []
            for h, (lb, _, _), pincl, da in zip(heads, terms, psums, das):
                tot, pre = h[2], h[4]
                a = jnp.exp2(lb + (tot - (pre[...] + pincl)))
                if mask is not None:
                    a = jnp.where(mask, a, 0.0)
                pre[...] += pincl[:, ATT_T - 1:ATT_T]
                gs.append(a * da)
                abs_.append(a.astype(BF16))
            gsums = [_dot(g.astype(BF16), tri[:ATT_T], NN) for g in gs]
            dzs = []
            for h, z, (_, _, e), g, gincl in zip(heads, zs, terms, gs, gsums):
                gpre = h[5]
                gbefore = gpre[...] + (gincl - g)
                gpre[...] += gincl[:, ATT_T - 1:ATT_T]
                inv = 1.0 / (1.0 + e)
                beta = jnp.where(z >= 0.0, inv, e * inv)
                dz = g - beta * (g + gbefore)
                if mask is not None:
                    dz = jnp.where(mask, dz, 0.0)
                dzs.append(dz.astype(BF16))
            for h, ab, dzb in zip(heads, abs_, dzs):
                dv_new += _dot(h[7], ab, NN)
                dk_new += _dot(h[6], dzb, NN)
                h[3][...] += _dot(dzb, k2, NN)
            cols = pl.ds(pl.multiple_of(kb * ATT_T, ATT_T), ATT_T)
            dkt[:, cols] += dk_new
            dvt[:, cols] += dv_new

        def step(kb, carry):
            block(kb, None)
            return carry

        first = jnp.clip(jnp.max(ls2[:, FIRST_LANE:FIRST_LANE + 1]).astype(jnp.int32), 0, i)
        lax.fori_loop(first, i, step, 0)
        block(i, causal)
        dq_ref[...] = (jnp.where(head_a, acc_a[...], acc_b[...]) * (HEAD_DIM ** -0.5)).astype(BF16)

        @pl.when(i == nq - 1)
        def _():
            for n in range(nq):
                rows = slice(n * ATT_T, (n + 1) * ATT_T)
                dkn, dvn = dkt[:, rows].T, dvt[:, rows].T
                if has_prev:
                    dkn, dvn = dkn + refs[5][rows, :], dvn + refs[6][rows, :]
                dk_ref[rows, :] = dkn
                dv_ref[rows, :] = dvn

    qspec = pl.BlockSpec((ATT_T, LANES), lambda p, i: (i, p))
    kspec = pl.BlockSpec((s, LANES), lambda p, i: (0, p))
    ins = [q, k, v, do, lsum] + ([dk_prev, dv_prev] if has_prev else [])
    return pl.pallas_call(
        body, name="attn_bwd", grid=(d // LANES, nq),
        in_specs=[qspec, kspec, kspec, qspec, qspec] + ([kspec, kspec] if has_prev else []),
        out_specs=[qspec, kspec, kspec],
        out_shape=[jax.ShapeDtypeStruct((s, d), BF16), jax.ShapeDtypeStruct((s, d), F32), jax.ShapeDtypeStruct((s, d), F32)],
        scratch_shapes=[pltpu.VMEM((ATT_T, LANES), F32), pltpu.VMEM((ATT_T, LANES), F32)]
        + [pltpu.VMEM((ATT_T, 1), F32)] * 4 + [pltpu.VMEM((LANES, s), F32)] * 2,
        compiler_params=_params(("parallel", "arbitrary")),
    )(*ins)


def _place():
    x, y, c = lax.axis_index("x"), lax.axis_index("y"), lax.axis_index("c")
    chips = [(1 - x, y), (x, 1 - y), (1 - x, 1 - y)]
    return x, y, c, chips


def _any_specs(n):
    return [pl.BlockSpec(memory_space=pl.ANY)] * n


def _gather_weights(bufs):
    n = len(bufs)

    def body(*refs):
        outs = refs[n:2 * n]
        send_sems, recv_sems = refs[2 * n:]
        x, y, c, chips = _place()
        me = 2 * x + y
        sibling = (x, y, 1 - c)

        def half(a, blk, hc):
            h = outs[a].shape[1] // 2
            return outs[a].at[blk, pl.ds(hc * h, h)]

        def copy(a, k, part, to):
            return pltpu.make_async_remote_copy(src_ref=part, dst_ref=part, send_sem=send_sems.at[a, k],
                                                recv_sem=recv_sems.at[a, k], device_id=to, device_id_type=MESH)

        sent = []
        for a in range(n):
            for k, chip in enumerate(chips):
                sent.append(copy(a, k, half(a, me, c), (*chip, c)))
                sent[-1].start()
        for a in range(n):
            for k, chip in enumerate(chips):
                blk = 2 * chip[0] + chip[1]
                copy(a, k, half(a, blk, c), sibling).wait_recv()
                sent.append(copy(a, 3 + k, half(a, blk, c), sibling))
                sent[-1].start()
        for a in range(n):
            for k, chip in enumerate(chips):
                blk = 2 * chip[0] + chip[1]
                copy(a, 3 + k, half(a, blk, 1 - c), sibling).wait_recv()
        for cp in sent:
            cp.wait_send()

    return pl.pallas_call(
        body, name="gather_weights", in_specs=_any_specs(n), out_specs=_any_specs(n),
        out_shape=[jax.ShapeDtypeStruct(w.shape, w.dtype) for w in bufs],
        input_output_aliases={a: a for a in range(n)},
        scratch_shapes=[pltpu.SemaphoreType.DMA((n, 6)), pltpu.SemaphoreType.DMA((n, 6))],
        compiler_params=pltpu.CompilerParams(has_side_effects=True),
    )(*bufs)


def _pair_exchange(grads):
    n = len(grads)

    def body(*refs):
        ins, outs = refs[:n], refs[n:2 * n]
        send_sems, recv_sems = refs[2 * n:]
        x, y, c, _ = _place()
        cps = []
        for a in range(n):
            h = ins[a].shape[1] // 2
            cps.append(pltpu.make_async_remote_copy(
                src_ref=ins[a].at[:, pl.ds((1 - c) * h, h)], dst_ref=outs[a], send_sem=send_sems.at[a],
                recv_sem=recv_sems.at[a], device_id=(x, y, 1 - c), device_id_type=MESH))
            cps[-1].start()
        for cp in cps:
            cp.wait()

    return pl.pallas_call(
        body, name="pair_exchange", in_specs=_any_specs(n), out_specs=_any_specs(n),
        out_shape=[jax.ShapeDtypeStruct((g.shape[0], g.shape[1] // 2, g.shape[2]), g.dtype) for g in grads],
        scratch_shapes=[pltpu.SemaphoreType.DMA((n,)), pltpu.SemaphoreType.DMA((n,))],
        compiler_params=pltpu.CompilerParams(has_side_effects=True),
    )(*grads)


def _chip_exchange(parts):
    n = len(parts)

    def body(*refs):
        ins, outs = refs[:n], refs[n:2 * n]
        send_sems, recv_sems = refs[2 * n:]
        x, y, c, chips = _place()
        me = 2 * x + y
        cps = []
        for a in range(n):
            for k, chip in enumerate(chips):
                blk = 2 * chip[0] + chip[1]
                cps.append(pltpu.make_async_remote_copy(
                    src_ref=ins[a].at[blk], dst_ref=outs[a].at[me], send_sem=send_sems.at[a, k],
                    recv_sem=recv_sems.at[a, k], device_id=(*chip, c), device_id_type=MESH))
                cps[-1].start()
        for a in range(n):
            for k, chip in enumerate(chips):
                blk = 2 * chip[0] + chip[1]
                pltpu.make_async_remote_copy(
                    src_ref=ins[a].at[blk], dst_ref=outs[a].at[blk], send_sem=send_sems.at[a, k],
                    recv_sem=recv_sems.at[a, k], device_id=(*chip, c), device_id_type=MESH).wait_recv()
        for cp in cps:
            cp.wait_send()

    return pl.pallas_call(
        body, name="chip_exchange", in_specs=_any_specs(n), out_specs=_any_specs(n),
        out_shape=[jax.ShapeDtypeStruct(p.shape, p.dtype) for p in parts],
        scratch_shapes=[pltpu.SemaphoreType.DMA((n, 3)), pltpu.SemaphoreType.DMA((n, 3))],
        compiler_params=pltpu.CompilerParams(has_side_effects=True),
    )(*parts)


def _half_swap(halves):
    n = len(halves)

    def body(*refs):
        outs = refs[n:2 * n]
        send_sems, recv_sems = refs[2 * n:]
        x, y, c, _ = _place()
        cps = []
        for a in range(n):
            h = outs[a].shape[1] // 2
            mine = outs[a].at[:, pl.ds(c * h, h)]
            cps.append(pltpu.make_async_remote_copy(
                src_ref=mine, dst_ref=mine, send_sem=send_sems.at[a], recv_sem=recv_sems.at[a],
                device_id=(x, y, 1 - c), device_id_type=MESH))
            cps[-1].start()
        for cp in cps:
            cp.wait()

    return pl.pallas_call(
        body, name="half_swap", in_specs=_any_specs(n), out_specs=_any_specs(n),
        out_shape=[jax.ShapeDtypeStruct(p.shape, p.dtype) for p in halves],
        input_output_aliases={a: a for a in range(n)},
        scratch_shapes=[pltpu.SemaphoreType.DMA((n,)), pltpu.SemaphoreType.DMA((n,))],
        compiler_params=pltpu.CompilerParams(has_side_effects=True),
    )(*halves)


N_DEV = 8


def _all_reduce_small(v):
    nrow, ncol = v.shape

    def body(v_ref, o_ref, land, red, send_sems, recv_sems, send2, recv2, loc_sem):
        x, y, c, _ = _place()
        me = 4 * x + 2 * y + c
        peers = []
        for k in range(1, N_DEV):
            peers.append((x ^ ((k >> 2) & 1), y ^ ((k >> 1) & 1), c ^ (k & 1)))
        own = pltpu.make_async_copy(v_ref.at[pl.ds(me, 1)], land.at[pl.ds(me, 1)], loc_sem)
        own.start()
        cps = []
        for k, peer in enumerate(peers):
            dev = 4 * peer[0] + 2 * peer[1] + peer[2]
            cps.append(pltpu.make_async_remote_copy(
                src_ref=v_ref.at[pl.ds(dev, 1)], dst_ref=land.at[pl.ds(me, 1)], send_sem=send_sems.at[k],
                recv_sem=recv_sems.at[k], device_id=peer, device_id_type=MESH))
            cps[-1].start()
        for k, peer in enumerate(peers):
            dev = 4 * peer[0] + 2 * peer[1] + peer[2]
            pltpu.make_async_remote_copy(
                src_ref=v_ref.at[pl.ds(dev, 1)], dst_ref=land.at[pl.ds(dev, 1)], send_sem=send_sems.at[k],
                recv_sem=recv_sems.at[k], device_id=peer, device_id_type=MESH).wait_recv()
        for cp in cps:
            cp.wait_send()
        own.wait()
        terms = land[...]
        total = terms[0:1, :]
        for d in range(1, N_DEV):
            total = total + terms[d:d + 1, :]
        red[...] = total
        own = pltpu.make_async_copy(red, o_ref.at[pl.ds(me, 1)], loc_sem)
        own.start()
        cps = []
        for k, peer in enumerate(peers):
            cps.append(pltpu.make_async_remote_copy(
                src_ref=red, dst_ref=o_ref.at[pl.ds(me, 1)], send_sem=send2.at[k],
                recv_sem=recv2.at[k], device_id=peer, device_id_type=MESH))
            cps[-1].start()
        for k, peer in enumerate(peers):
            dev = 4 * peer[0] + 2 * peer[1] + peer[2]
            pltpu.make_async_remote_copy(
                src_ref=red, dst_ref=o_ref.at[pl.ds(dev, 1)], send_sem=send2.at[k],
                recv_sem=recv2.at[k], device_id=peer, device_id_type=MESH).wait_recv()
        for cp in cps:
            cp.wait_send()
        own.wait()

    vm = pl.BlockSpec(memory_space=pltpu.VMEM)
    return pl.pallas_call(
        body, name="all_reduce_small", in_specs=[vm], out_specs=vm,
        out_shape=jax.ShapeDtypeStruct((nrow, ncol), F32),
        scratch_shapes=[pltpu.VMEM((nrow, ncol), F32), pltpu.VMEM((1, ncol), F32)]
        + [pltpu.SemaphoreType.DMA((N_DEV - 1,))] * 4 + [pltpu.SemaphoreType.DMA],
        compiler_params=pltpu.CompilerParams(has_side_effects=True, vmem_limit_bytes=VMEM_LIMIT),
    )(v)


def _row_tile(rows):
    return min(rows, 512)


def _pair_sum(g, got, place):
    nb, r, c = g.shape
    h = r // 2
    tr = _row_tile(h)
    nt = h // tr

    def body(place_ref, g_ref, got_ref, p_ref, pb_ref):
        p = g_ref[...] + got_ref[...]
        pb_ref[...] = p.astype(BF16)

        @pl.when(pl.program_id(1) == place_ref[0])
        def _():
            p_ref[...] = p

    spec = pl.BlockSpec((None, tr, c), lambda t, j, place_ref: (j, t, 0))
    grid_spec = pltpu.PrefetchScalarGridSpec(
        num_scalar_prefetch=1, grid=(nt, nb),
        in_specs=[pl.BlockSpec((None, tr, c), lambda t, j, place_ref: (j, place_ref[1] * nt + t, 0)), spec],
        out_specs=[pl.BlockSpec((tr, c), lambda t, j, place_ref: (t, 0)), spec])
    return pl.pallas_call(
        body, name="pair_sum", grid_spec=grid_spec,
        out_shape=[jax.ShapeDtypeStruct((h, c), F32), jax.ShapeDtypeStruct((nb, h, c), BF16)],
        compiler_params=_params(("parallel", "arbitrary")),
    )(place, g, got)


def _chip_sum(p, got, place, out, layer):
    h, c = p.shape
    tr = _row_tile(h)
    nt = h // tr

    def body(place_ref, p_ref, g1_ref, g2_ref, g3_ref, old_ref, o_ref):
        o_ref[...] = ((p_ref[...] + g1_ref[...].astype(F32)) + g2_ref[...].astype(F32)) + g3_ref[...].astype(F32)

    def blk(off):
        return pl.BlockSpec((None, tr, c), lambda t, place_ref: ((place_ref[0] + off) % N_CHIPS, t, 0))

    grid_spec = pltpu.PrefetchScalarGridSpec(
        num_scalar_prefetch=1, grid=(nt,),
        in_specs=[pl.BlockSpec((tr, c), lambda t, place_ref: (t, 0)), blk(1), blk(2), blk(3),
                  pl.BlockSpec(memory_space=pl.ANY)],
        out_specs=pl.BlockSpec((None, tr, c), lambda t, place_ref: (layer, place_ref[1] * nt + t, 0)))
    return pl.pallas_call(
        body, name="chip_sum", grid_spec=grid_spec, out_shape=jax.ShapeDtypeStruct(out.shape, F32),
        input_output_aliases={5: 0}, compiler_params=_params(("parallel",)),
    )(place, p, got, got, got, out)


def _adamw(w, g, m, v):
    r, c = w.shape
    tr = r if r < 8 else _row_tile(r)

    def body(w_ref, g_ref, m_ref, v_ref, d_ref, nm_ref, nv_ref):
        gv = g_ref[...]
        nm = ADAM_B1 * m_ref[...] + (1.0 - ADAM_B1) * gv
        nv = ADAM_B2 * v_ref[...] + (1.0 - ADAM_B2) * (gv * gv)
        m_hat = nm / (1.0 - ADAM_B1 ** ADAM_STEP)
        v_hat = nv / (1.0 - ADAM_B2 ** ADAM_STEP)
        d_ref[...] = -ADAM_LR * (m_hat / (jnp.sqrt(v_hat) + ADAM_EPS) + ADAM_WD * w_ref[...])
        nm_ref[...] = nm
        nv_ref[...] = nv

    tile = pl.BlockSpec((tr, c), lambda i: (i, 0))
    return pl.pallas_call(
        body, name="adamw", grid=(r // tr,), in_specs=[tile] * 4, out_specs=[tile] * 3,
        out_shape=[jax.ShapeDtypeStruct((r, c), F32)] * 3, compiler_params=_params(("parallel",)),
    )(w, g, m, v)


BIG = ("a_w_in", "a_w_out", "sb_w_k", "sb_w_v", "b_w_q", "b_w_o", "ffn_w1", "ffn_w2")
SMALL = ("a_ln_g", "a_ln_b", "a_w_s", "a_b_s", "mix_ln_g", "mix_ln_b", "ffn_ln_g", "ffn_ln_b")
COL_SHARDED = {"a_w_in": True, "a_w_out": False, "sb_w_k": False, "sb_w_v": False, "b_w_q": False, "b_w_o": False,
               "ffn_w1": True, "ffn_w2": False}


def kernel(x, a_w_in, a_ln_g, a_ln_b, a_w_s, a_b_s, a_w_out, sb_w_k, sb_w_v, b_w_q, b_w_o, mix_ln_g, mix_ln_b, ffn_ln_g, ffn_ln_b, ffn_w1, ffn_w2, loss_target, m_a_w_in, m_a_ln_g, m_a_ln_b, m_a_w_s, m_a_b_s, m_a_w_out, m_sb_w_k, m_sb_w_v, m_b_w_q, m_b_w_o, m_mix_ln_g, m_mix_ln_b, m_ffn_ln_g, m_ffn_ln_b, m_ffn_w1, m_ffn_w2, v_a_w_in, v_a_ln_g, v_a_ln_b, v_a_w_s, v_a_b_s, v_a_w_out, v_sb_w_k, v_sb_w_v, v_b_w_q, v_b_w_o, v_mix_ln_g, v_mix_ln_b, v_ffn_ln_g, v_ffn_ln_b, v_ffn_w1, v_ffn_w2):
    names = BIG + SMALL
    given = dict(a_w_in=a_w_in, a_ln_g=a_ln_g, a_ln_b=a_ln_b, a_w_s=a_w_s, a_b_s=a_b_s, a_w_out=a_w_out, sb_w_k=sb_w_k,
                 sb_w_v=sb_w_v, b_w_q=b_w_q, b_w_o=b_w_o, mix_ln_g=mix_ln_g, mix_ln_b=mix_ln_b, ffn_ln_g=ffn_ln_g,
                 ffn_ln_b=ffn_ln_b, ffn_w1=ffn_w1, ffn_w2=ffn_w2)
    mom = dict(a_w_in=m_a_w_in, a_ln_g=m_a_ln_g, a_ln_b=m_a_ln_b, a_w_s=m_a_w_s, a_b_s=m_a_b_s, a_w_out=m_a_w_out,
               sb_w_k=m_sb_w_k, sb_w_v=m_sb_w_v, b_w_q=m_b_w_q, b_w_o=m_b_w_o, mix_ln_g=m_mix_ln_g, mix_ln_b=m_mix_ln_b,
               ffn_ln_g=m_ffn_ln_g, ffn_ln_b=m_ffn_ln_b, ffn_w1=m_ffn_w1, ffn_w2=m_ffn_w2)
    var = dict(a_w_in=v_a_w_in, a_ln_g=v_a_ln_g, a_ln_b=v_a_ln_b, a_w_s=v_a_w_s, a_b_s=v_a_b_s, a_w_out=v_a_w_out,
               sb_w_k=v_sb_w_k, sb_w_v=v_sb_w_v, b_w_q=v_b_w_q, b_w_o=v_b_w_o, mix_ln_g=v_mix_ln_g, mix_ln_b=v_mix_ln_b,
               ffn_ln_g=v_ffn_ln_g, ffn_ln_b=v_ffn_ln_b, ffn_w1=v_ffn_w1, ffn_w2=v_ffn_w2)

    cx, cy, cc = lax.axis_index("x"), lax.axis_index("y"), lax.axis_index("c")
    chip = (2 * cx + cy).astype(jnp.int32)
    chip_arr = chip.reshape(1)

    s, d = x.shape[1], x.shape[2]
    xf = x.reshape(s, d)
    target = loss_target.reshape(s, d)

    def as2d(w):
        return w.reshape(-1, w.shape[-1])

    gw = {}
    for n in BIG:
        for l in ([None] if given[n].ndim == 2 else range(given[n].shape[0])):
            gw[(n, l)] = _cast_into_slot(given[n], l, chip_arr)
    ln_gb = jnp.stack([a_ln_g, a_ln_b])
    ln_slot = lax.dynamic_update_slice(jnp.zeros((N_CHIPS,) + ln_gb.shape, F32), ln_gb[None], (chip, 0, 0, 0))
    layer0 = [("a_w_in", 0), ("a_w_out", 0)]
    gathered = _gather_weights([gw[k] for k in layer0] + [ln_slot])
    gw.update(zip(layer0, gathered[:-1]))
    mixer = {1: [("a_w_in", 1), ("a_w_out", 1)], 2: [("sb_w_k", None), ("sb_w_v", None), ("b_w_q", 0), ("b_w_o", 0)],
             3: [("b_w_q", 1), ("b_w_o", 1)]}

    def riding(d2d=(), ici=()):
        keys = list(d2d) + list(ici)
        return keys, [("d2d", gw[k]) for k in d2d] + [("ici", gw[k]) for k in ici]

    def landed_in(keys, bufs):
        gw.update(zip(keys, bufs))

    ln_full = gathered[-1].transpose(1, 2, 0, 3).reshape(2, N_A, 1, -1)
    a_ln_g3, a_ln_b3 = ln_full[0], ln_full[1]
    mix_g3, mix_b3 = mix_ln_g[:, None, :], mix_ln_b[:, None, :]
    ffn_g3, ffn_b3 = ffn_ln_g[:, None, :], ffn_ln_b[:, None, :]
    bst = jnp.swapaxes(a_b_s, 1, 2)

    saved = []
    xb = _cast_bf16(xf)
    kb = vb = None
    for l in range(DEPTH):
        sv = dict(x_in=xb)
        last = l == DEPTH - 1
        if l == 0:
            keys, riders = riding(ici=[("ffn_w1", 0)])
        else:
            keys, riders = riding(d2d=[("ffn_w1", l), ("ffn_w2", l)], ici=mixer[l + 1] if l < N_A else [])
        if l < N_A:
            h, *bufs = _mm_fwd("a_in", xb, gw[("a_w_in", l)], None, True, riders=riders)
            landed_in(keys, bufs)
            vn = _gmlp_norm_fwd(h, a_ln_g3, a_ln_b3, l)
            gated = _gate_fwd(h, vn, a_w_s[l], bst[l])
            keys, riders = riding(d2d=[("ffn_w1", 0)], ici=[("ffn_w2", 0)]) if l == 0 else ([], [])
            xf, xb, xhat, rstd, *bufs = _mm_resid_ln("a_out", gated, gw[("a_w_out", l)], xf, mix_g3, mix_b3, l, riders)
            landed_in(keys, bufs)
            sv.update(h=h, vn=vn, gated=gated)
        else:
            j = l - N_A
            if l == N_A:
                kb, *bufs = _mm_fwd("sb_k", xb, gw[("sb_w_k", None)], None, False, _ep_bf16, outs=[(d, BF16)],
                                    riders=riders)
                landed_in(keys, bufs)
                keys, riders = [], ()
                vb = _mm_fwd("sb_v", xb, gw[("sb_w_v", None)], None, False, _ep_bf16, outs=[(d, BF16)])[0]
            q, *bufs = _mm_fwd("b_q", xb, gw[("b_w_q", j)], None, False, _ep_scale_q, outs=[(d, BF16)], riders=riders)
            landed_in(keys, bufs)
            ob, lsum = _attn_fwd(q, kb, vb)
            keys, riders = riding(ici=[] if last else mixer[l + 1])
            xf, xb, xhat, rstd, *bufs = _mm_resid_ln("b_out", ob, gw[("b_w_o", j)], xf, mix_g3, mix_b3, l, riders)
            landed_in(keys, bufs)
            sv.update(q=q, lsum=lsum, ob=ob)
        sv.update(x_mid=xb, xhat1=xhat, rstd1=rstd)
        dff = gw[("ffn_w1", l)].shape[-1] * N_CHIPS
        if l == 0:
            keys, riders = riding(d2d=[("ffn_w2", 0)], ici=mixer[1] + [("ffn_w1", 1)])
        else:
            keys, riders = riding(ici=[] if last else [("ffn_w1", l + 1)])
        pr, *bufs = _mm_fwd("ffn_1", xb, gw[("ffn_w1", l)], None, True, _ep_relu, outs=[(dff, BF16)], riders=riders)
        landed_in(keys, bufs)
        keys, riders = riding(d2d=[] if last else mixer[l + 1], ici=[] if last else [("ffn_w2", l + 1)])
        xf, xb, xhat, rstd, *bufs = _mm_resid_ln("ffn_2", pr, gw[("ffn_w2", l)], xf, ffn_g3, ffn_b3, l, riders, _square)
        landed_in(keys, bufs)
        sv.update(pr=pr, xhat2=xhat, rstd2=rstd)
        saved.append(sv)

    dx, sq = _loss_head(xf, target)
    loss = lax.psum(0.5 * sq[0, 0] / d, ("x", "y", "c"))

    pending = []
    pair_sums, landed = {}, {}
    place_arr = jnp.stack([chip, cc.astype(jnp.int32)])

    def arrived(took, outs):
        for (kind, key, arr), out in zip(took, outs):
            if kind == "pair":
                pair_sums[key] = _pair_sum(arr, out, place_arr)
                pending.append(("chip", key, pair_sums[key][1]))
            else:
                landed[key] = out

    def carrying(call, name, *args, **kw):
        took = []
        if name.startswith("ffn") or draining[0]:
            room = CARRIER_PARAMS
            for task in list(pending):
                size = given[task[1][0]].shape[-2] * given[task[1][0]].shape[-1] * N_CHIPS
                if task[0] == "pair" or room == CARRIER_PARAMS or size <= room:
                    took.append(task)
                    pending.remove(task)
                    room -= size if task[0] == "chip" else 0
        out, *rest = call(name, *args, riders=[(kind, arr) for kind, _, arr in took], **kw)
        arrived(took, rest)
        return out

    draining = [False]

    def bwd_act(*args, **kw):
        return carrying(_mm_bwd_act, *args, **kw)

    def bwd_w(key, name, a, dy, **kw):
        pending.append(("pair", key, carrying(_mm_bwd_w, name, a, dy, gw[key], COL_SHARDED[key[0]], **kw)))

    d_mix_g, d_mix_b, d_ffn_g, d_ffn_b = [None] * DEPTH, [None] * DEPTH, [None] * DEPTH, [None] * DEPTH
    d_ln_g, d_ln_b, d_ws, d_bs = [None] * N_A, [None] * N_A, [None] * N_A, [None] * N_A
    dk = dv = None
    for l in reversed(range(DEPTH)):
        sv = saved[l]
        draining[0] = l == 0
        dr, drb, d_ffn_g[l], d_ffn_b[l] = _ln_bwd(dx, sv["xhat2"], sv["rstd2"], ffn_g3, l)
        dff = sv["pr"].shape[1]
        dhd = bwd_act("ffn_2_dx", drb, gw[("ffn_w2", l)], None, False, _ep_relu2_bwd, (sv["pr"],),
                      (pl.BlockSpec((_wide_tile(s), dff // N_CHIPS), lambda j, i, k: (i, j)),), out_dtype=BF16)
        bwd_w(("ffn_w2", l), "ffn_2_dw", sv["pr"], drb, a_fn=_square)
        dx = bwd_act("ffn_1_dx", dhd, gw[("ffn_w1", l)], None, True, _ep_resid, (dr,), (_row_spec(d),))
        bwd_w(("ffn_w1", l), "ffn_1_dw", sv["x_mid"], dhd)

        dr, drb, d_mix_g[l], d_mix_b[l] = _ln_bwd(dx, sv["xhat1"], sv["rstd1"], mix_g3, l)
        quarter = pl.BlockSpec((_wide_tile(s), d // N_CHIPS), lambda j, i, k: (i, j))
        if l < N_A:
            dgated = bwd_act("a_out_dx", drb, gw[("a_w_out", l)], None, False)
            bwd_w(("a_w_out", l), "a_out_dw", sv["gated"], drb)
            du, dvn, d_ws[l], dbs_wide = _gate_bwd(dgated, sv["h"], sv["vn"], a_w_s[l], bst[l])
            d_bs[l] = dbs_wide[:, :, 0]
            dh, dlg, dlb = _gmlp_in_bwd(sv["h"], du, dvn, a_ln_g3, l)
            d_ln_g[l], d_ln_b[l] = dlg[0], dlb[0]
            dx = bwd_act("a_in_dx", dh, gw[("a_w_in", l)], None, True, _ep_resid, (dr,), (_row_spec(d),))
            bwd_w(("a_w_in", l), "a_in_dw", sv["x_in"], dh)
        else:
            j = l - N_A
            do = bwd_act("b_out_dx", drb, gw[("b_w_o", j)], None, False)
            bwd_w(("b_w_o", j), "b_out_dw", sv["ob"], drb)
            dq, dk, dv = _attn_bwd(sv["q"], kb, vb, do, sv["lsum"], dk, dv)
            dx = bwd_act("b_q_dx", dq, gw[("b_w_q", j)], None, False, _ep_resid, (dr,), (quarter,))
            bwd_w(("b_w_q", j), "b_q_dw", sv["x_in"], dq)
            if l == N_A:
                dx = bwd_act("sb_k_dx", dk, gw[("sb_w_k", None)], None, False, _ep_add, (dx,), (quarter,))
                bwd_w(("sb_w_k", None), "sb_k_dw", sv["x_in"], dk)
                dx = bwd_act("sb_v_dx", dv, gw[("sb_w_v", None)], None, False, _ep_add, (dx,), (quarter,))
                bwd_w(("sb_w_v", None), "sb_v_dw", sv["x_in"], dv)
    grad_x = dx.reshape(x.shape)

    while pending:
        took = list(pending)
        pending.clear()
        for kind, exchange in (("pair", _pair_exchange), ("chip", _chip_exchange)):
            some = [t for t in took if t[0] == kind]
            if some:
                arrived(some, exchange([arr for _, _, arr in some]))

    stacked = []
    for n in BIG:
        layers = [None] if given[n].ndim == 2 else range(given[n].shape[0])
        out = lax.empty((len(layers),) + given[n].shape[-2:], F32)
        for at, l in enumerate(layers):
            out = _chip_sum(pair_sums[(n, l)][0], landed[(n, l)], place_arr, out, at)
        stacked.append(out)
    grads = {n: g.reshape(given[n].shape) for n, g in zip(BIG, _half_swap(stacked))}

    small_full = dict(a_ln_g=jnp.stack(d_ln_g), a_ln_b=jnp.stack(d_ln_b), a_w_s=jnp.stack(d_ws), a_b_s=jnp.stack(d_bs),
                      mix_ln_g=jnp.concatenate(d_mix_g), mix_ln_b=jnp.concatenate(d_mix_b),
                      ffn_ln_g=jnp.concatenate(d_ffn_g), ffn_ln_b=jnp.concatenate(d_ffn_b))
    packed = jnp.concatenate([small_full[n].reshape(-1) for n in SMALL])
    total = packed.shape[0]
    ncol = -(-total // (N_DEV * LANES)) * LANES
    packed = jnp.pad(packed, (0, N_DEV * ncol - total)).reshape(N_DEV, ncol)
    reduced = _all_reduce_small(packed).reshape(-1)
    off = 0
    for n in SMALL:
        size = small_full[n].size
        g = reduced[off:off + size].reshape(small_full[n].shape)
        off += size
        if n in ("a_ln_g", "a_ln_b"):
            wq = given[n].shape[1]
            g = lax.dynamic_slice_in_dim(g, chip * wq, wq, axis=1)
        grads[n] = g

    delta, new_m, new_v = {}, {}, {}
    for n in names:
        shape = given[n].shape
        dl, nm, nv = _adamw(as2d(given[n]), as2d(grads[n]), as2d(mom[n]), as2d(var[n]))
        delta[n], new_m[n], new_v[n] = dl.reshape(shape), nm.reshape(shape), nv.reshape(shape)

    order = ("a_w_in", "a_ln_g", "a_ln_b", "a_w_s", "a_b_s", "a_w_out", "sb_w_k", "sb_w_v", "b_w_q", "b_w_o",
             "mix_ln_g", "mix_ln_b", "ffn_ln_g", "ffn_ln_b", "ffn_w1", "ffn_w2")
    return (loss, grad_x, *[grads[n] for n in order], *[delta[n] for n in order],
            *[new_m[n] for n in order], *[new_v[n] for n in order])
```

```python
import math

import jax
import jax.numpy as jnp
from jax import lax
from jax.experimental import pallas as pl
from jax.experimental.pallas import tpu as pltpu

F32 = jnp.float32
BF16 = jnp.bfloat16
MESH = pl.DeviceIdType.MESH

N_CHIPS = 4
DEPTH = 4
N_A = 2
ALPHA = float((2 * DEPTH) ** 0.25)
LN_EPS = 1e-5
CHUNK = 64
GMLP_BLOCK = 128
GMLP_GROUPS = 8
HEAD_DIM = 64
LANES = 128
ATT_T = 256
ADAM_LR = 0.001
ADAM_B1 = 0.9
ADAM_B2 = 0.999
ADAM_EPS = 1e-08
ADAM_WD = 0.01
ADAM_STEP = 10
VMEM_LIMIT = 56 * 1024 * 1024
TM = 512
TM_WIDE = 1024
TS = 1024

NN = ((1,), (0,))
NT = ((1,), (1,))
TN = ((0,), (0,))


def _params(sem):
    return pltpu.CompilerParams(dimension_semantics=sem, vmem_limit_bytes=VMEM_LIMIT)


def _dot(a, b, contract):
    return lax.dot_general(a, b, (contract, ((), ())), preferred_element_type=F32)


def _rider_out(kind, arr):
    shape = (arr.shape[0], arr.shape[1] // 2, arr.shape[2]) if kind == "pair" else arr.shape
    return jax.ShapeDtypeStruct(shape, arr.dtype)


def _rider_copies(kind, src, dst, send_sems, recv_sems, base):
    x, y, c, chips = _place()
    me = 2 * x + y
    sibling = (x, y, 1 - c)

    def copy(k, part, land, to):
        return pltpu.make_async_remote_copy(src_ref=part, dst_ref=land, send_sem=send_sems.at[base + k],
                                            recv_sem=recv_sems.at[base + k], device_id=to, device_id_type=MESH)

    if kind == "pair":
        h = src.shape[1] // 2
        cp = copy(0, src.at[:, pl.ds((1 - c) * h, h)], dst, sibling)
        return [cp], [cp]
    h = dst.shape[1] // 2
    starts, arrivals = [], []
    for k, chip in enumerate(chips):
        blk = 2 * chip[0] + chip[1]
        if kind == "ici":
            starts.append(copy(k, dst.at[me, pl.ds(c * h, h)], dst.at[me, pl.ds(c * h, h)], (*chip, c)))
            arrivals.append(copy(k, dst.at[blk, pl.ds(c * h, h)], dst.at[blk, pl.ds(c * h, h)], (*chip, c)))
        elif kind == "d2d":
            starts.append(copy(k, dst.at[blk, pl.ds(c * h, h)], dst.at[blk, pl.ds(c * h, h)], sibling))
            arrivals.append(copy(k, dst.at[blk, pl.ds((1 - c) * h, h)], dst.at[blk, pl.ds((1 - c) * h, h)], sibling))
        else:
            starts.append(copy(k, src.at[blk], dst.at[me], (*chip, c)))
            arrivals.append(copy(k, src.at[blk], dst.at[blk], (*chip, c)))
    return starts, arrivals


RIDER_SEMS = 3
CARRIER_PARAMS = 5 * 2 ** 20


def _identity(a):
    return a


def _square(a):
    return a * a


def _matmul(name, operands, in_specs, out_shapes, out_specs, grid, contract, epilogue, acc_shape, aliases=None,
            chunks=None, riders=(), pick=False, a_fn=_identity, sequential=False):
    nk = grid[2]
    n_in, n_out, nr = len(operands), len(out_shapes), len(riders)
    n_plain = n_in + nr + n_out

    def body(*refs):
        ins, outs = refs[:n_in], refs[n_in + nr:n_plain]
        if nr:
            srcs, dsts = refs[n_in:n_in + nr], refs[n_plain:n_plain + nr]
            send_sems, recv_sems = refs[-2:]
            pid = [pl.program_id(ax) for ax in range(3)]
            first = (pid[0] == 0) & (pid[1] == 0) & (pid[2] == 0)
            last = (pid[0] == grid[0] - 1) & (pid[1] == grid[1] - 1) & (pid[2] == grid[2] - 1)

            def copies(n):
                return _rider_copies(riders[n][0], srcs[n], dsts[n], send_sems, recv_sems, RIDER_SEMS * n)

            @pl.when(first)
            def _():
                for n in range(nr):
                    for cp in copies(n)[0]:
                        cp.start()

        compute(refs, ins, outs)
        if nr:
            @pl.when(last)
            def _():
                for n in range(nr):
                    starts, arrivals = copies(n)
                    for cp in arrivals:
                        cp.wait_recv()
                    for cp in starts:
                        cp.wait_send()

    def compute(refs, ins, outs):
        if chunks is None:
            b = ins[1][pl.program_id(1)] if pick else ins[1][...]
            p = _dot(a_fn(ins[0][...].astype(BF16)), b.astype(BF16), contract)
        else:
            width = ins[0].shape[1] // chunks
            p = None
            for j in range(chunks):
                pj = _dot(a_fn(ins[0][:, j * width:(j + 1) * width].astype(BF16)), ins[1][j].astype(BF16), contract)
                p = pj if p is None else p + pj
        if nk == 1:
            epilogue(p, ins[2:], outs)
            return
        acc = refs[n_plain + nr]
        k = pl.program_id(2)

        @pl.when(k == 0)
        def _():
            acc[...] = p

        @pl.when((k > 0) & (k < nk - 1))
        def _():
            acc[...] += p

        @pl.when(k == nk - 1)
        def _():
            epilogue(acc[...] + p, ins[2:], outs)

    rbufs = [b for _, b in riders]
    in_place = {n_in + n: n_out + n for n, (kind, _) in enumerate(riders) if kind in ("ici", "d2d")}
    scratch = ([] if nk == 1 else [pltpu.VMEM(acc_shape, F32)]) \
        + [pltpu.SemaphoreType.DMA((RIDER_SEMS * nr,))] * (2 if nr else 0)
    return pl.pallas_call(
        body, name=name, grid=grid, in_specs=list(in_specs) + _any_specs(nr), out_specs=list(out_specs) + _any_specs(nr),
        out_shape=list(out_shapes) + [_rider_out(kind, b) for kind, b in riders],
        scratch_shapes=scratch,
        input_output_aliases={**(aliases or {}), **in_place},
        compiler_params=_params(("arbitrary",) * 3 if nr or sequential else ("parallel", "parallel", "arbitrary")),
    )(*operands, *rbufs)


def _wspec(w, layer, whole=False):
    r, c = w.shape[-2:]
    lead = N_CHIPS if whole else None
    if w.ndim == 4:
        return pl.BlockSpec((lead, None, r, c), lambda j, i, k: (0 if whole else j, layer, 0, 0))
    return pl.BlockSpec((lead, r, c), lambda j, i, k: (0 if whole else j, 0, 0))


def _wide_tile(s):
    return min(TM_WIDE, s)


def _ep_store(p, ins, outs):
    for o in outs:
        o[...] = p.astype(o.dtype)


def _rows_first(spec):
    return pl.BlockSpec(spec.block_shape, lambda i, j, k: spec.index_map(j, i, k))


def _mm_fwd(name, a, w, layer, col_sharded, epilogue=_ep_store, extras=(), extra_specs=(), outs=None, riders=(),
            a_fn=_identity):
    s = a.shape[0]
    r, c = w.shape[-2:]
    if col_sharded:
        tm = _wide_tile(s)
        grid = (s // tm, N_CHIPS, 1)
        a_spec = pl.BlockSpec((tm, r), lambda j, i, k: (i, 0))
        n_out = N_CHIPS * c
    else:
        tm = TM
        grid = (1, s // tm, 1)
        a_spec = pl.BlockSpec((tm, N_CHIPS * r), lambda j, i, k: (i, 0))
        n_out = c
    if outs is None:
        outs = [(n_out, F32)]
    out_shapes = [jax.ShapeDtypeStruct((s, n), dt) for n, dt in outs]
    out_specs = [pl.BlockSpec((tm, c if n == n_out else n), lambda j, i, k: (i, j)) for n, _ in outs]
    in_specs = [a_spec, _wspec(w, layer, True)] + list(extra_specs)
    if col_sharded:
        in_specs, out_specs = [_rows_first(sp) for sp in in_specs], [_rows_first(sp) for sp in out_specs]
    return _matmul(name, (a, w) + tuple(extras), in_specs, out_shapes, out_specs, grid, NN, epilogue, (tm, c),
                   chunks=None if col_sharded else N_CHIPS, riders=riders, pick=col_sharded, a_fn=a_fn)


def _mm_bwd_act(name, dy, w, layer, col_sharded, epilogue=_ep_store, extras=(), extra_specs=(), out_dtype=F32,
                riders=(), through_norm=False):
    s = dy.shape[0]
    r, c = w.shape[-2:]
    if col_sharded:
        tm = TM
        grid = (1, s // tm, 1)
        a_spec = pl.BlockSpec((tm, N_CHIPS * c), lambda j, i, k: (i, 0))
        n_out = r
    else:
        tm = _wide_tile(s)
        grid = (s // tm, N_CHIPS, 1)
        a_spec = pl.BlockSpec((tm, c), lambda j, i, k: (i, 0))
        n_out = N_CHIPS * r
    in_specs = [a_spec, _wspec(w, layer, True)] + list(extra_specs)
    o_spec = pl.BlockSpec((tm, r), lambda j, i, k: (i, j))
    if not col_sharded:
        in_specs, o_spec = [_rows_first(sp) for sp in in_specs], _rows_first(o_spec)
    out_shapes, out_specs = [jax.ShapeDtypeStruct((s, n_out), out_dtype)], [o_spec]
    if through_norm:
        vec = pl.BlockSpec((1, n_out), lambda j, i, k: (0, 0))
        out_shapes = [jax.ShapeDtypeStruct((s, n_out), F32), jax.ShapeDtypeStruct((s, n_out), BF16),
                      jax.ShapeDtypeStruct((1, n_out), F32), jax.ShapeDtypeStruct((1, n_out), F32)]
        out_specs = [o_spec, o_spec, vec, vec]
    return _matmul(name, (dy, w) + tuple(extras), in_specs, out_shapes, out_specs, grid, NT, epilogue, (tm, r),
                   chunks=N_CHIPS if col_sharded else None, riders=riders, pick=not col_sharded,
                   sequential=through_norm)


def _mm_bwd_w(name, a, dy, w, col_sharded, riders=(), a_fn=_identity):
    s = a.shape[0]
    r, c = w.shape[-2:]
    ts = min(TS, s)
    grid = (N_CHIPS, 1, s // ts)
    if col_sharded:
        a_spec = pl.BlockSpec((ts, r), lambda j, i, k: (k, 0))
        b_spec = pl.BlockSpec((ts, c), lambda j, i, k: (k, j))
    else:
        a_spec = pl.BlockSpec((ts, r), lambda j, i, k: (k, j))
        b_spec = pl.BlockSpec((ts, c), lambda j, i, k: (k, 0))

    def epilogue(p, ins, outs):
        outs[0][...] = p

    return _matmul(name, (a, dy), [a_spec, b_spec], [jax.ShapeDtypeStruct(w.shape, F32)], [_wspec(w, None)], grid, TN,
                   epilogue, (r, c), riders=riders, a_fn=a_fn)


def _row_spec(n):
    return pl.BlockSpec((TM, n), lambda j, i, k: (i, 0))


def _vec_spec(layer, n):
    return pl.BlockSpec((None, 1, n), lambda j, i, k: (layer, 0, 0))


def _ep_resid_ln(p, ins, outs):
    x_ref, g_ref, b_ref = ins
    xf_ref, xb_ref, xhat_ref, rstd_ref = outs
    r = ALPHA * x_ref[...] + p
    mu = jnp.mean(r, axis=-1, keepdims=True)
    d = r - mu
    var = jnp.mean(d * d, axis=-1, keepdims=True)
    rstd = lax.rsqrt(var + LN_EPS)
    xhat = d * rstd
    y = xhat * g_ref[...] + b_ref[...]
    xf_ref[...] = y
    xb_ref[...] = y.astype(BF16)
    xhat_ref[...] = xhat
    rstd_ref[...] = rstd


def _mm_resid_ln(name, a, w, x, g3, b3, ln_layer, riders=(), a_fn=_identity):
    d = x.shape[1]
    return _mm_fwd(name, a, w, None, False, _ep_resid_ln, (x, g3, b3),
                   (_row_spec(d), _vec_spec(ln_layer, d), _vec_spec(ln_layer, d)),
                   outs=[(d, F32), (d, BF16), (d, F32), (1, F32)], riders=riders, a_fn=a_fn)


def _ep_relu(p, ins, outs):
    outs[0][...] = jnp.maximum(p, 0.0).astype(BF16)


def _ep_scale_q(p, ins, outs):
    outs[0][...] = (p * (HEAD_DIM ** -0.5)).astype(BF16)


def _ep_bf16(p, ins, outs):
    outs[0][...] = p.astype(BF16)


def _ep_relu2_bwd(p, ins, outs):
    outs[0][...] = (p * (2.0 * ins[0][...].astype(F32))).astype(BF16)


def _ep_resid(p, ins, outs):
    outs[0][...] = ALPHA * ins[0][...] + p


def _ep_resid_ln_bwd(p, ins, outs):
    dr_ref, xh_ref, rs_ref, g_ref = ins
    _ln_bwd_rows(ALPHA * dr_ref[...] + p, xh_ref, rs_ref, g_ref, pl.program_id(1) == 0, *outs)


def _ep_add(p, ins, outs):
    outs[0][...] = ins[0][...] + p


def _gelu_grad(x):
    c0 = math.sqrt(2.0 / math.pi)
    t = jnp.tanh(c0 * (x + 0.044715 * (x * x * x)))
    return 0.5 * (1.0 + t) + (0.5 * x) * (1.0 - t * t) * (c0 * (1.0 + 3.0 * 0.044715 * (x * x)))


def _cast_bf16(w2d):
    r, c = w2d.shape
    tr = min(r, 512)

    def body(w_ref, o_ref):
        o_ref[...] = w_ref[...].astype(BF16)

    return pl.pallas_call(
        body, name="cast_bf16", grid=(r // tr,),
        in_specs=[pl.BlockSpec((tr, c), lambda i: (i, 0))], out_specs=pl.BlockSpec((tr, c), lambda i: (i, 0)),
        out_shape=jax.ShapeDtypeStruct((r, c), BF16), compiler_params=_params(("parallel",)),
    )(w2d)


def _cast_into_slot(w, layer, chip):
    r, c = w.shape[-2:]
    tr = min(r, 512)

    def body(chip_ref, w_ref, o_ref):
        o_ref[...] = w_ref[...].astype(BF16)

    if layer is None:
        w_spec = pl.BlockSpec((tr, c), lambda i, chip_ref: (i, 0))
    else:
        w_spec = pl.BlockSpec((None, tr, c), lambda i, chip_ref: (layer, i, 0))
    grid_spec = pltpu.PrefetchScalarGridSpec(
        num_scalar_prefetch=1, grid=(r // tr,), in_specs=[w_spec],
        out_specs=pl.BlockSpec((None, tr, c), lambda i, chip_ref: (chip_ref[0], i, 0)))
    return pl.pallas_call(
        body, name="cast_into_slot", grid_spec=grid_spec,
        out_shape=jax.ShapeDtypeStruct((N_CHIPS, r, c), BF16), compiler_params=_params(("parallel",)),
    )(chip, w)


def _gmlp_norm_fwd(h, g3, b3, layer):
    s, w2 = h.shape
    w = w2 // 2

    def body(h_ref, g_ref, b_ref, o_ref):
        z = jax.nn.gelu(h_ref[...])
        mu = jnp.mean(z, axis=-1, keepdims=True)
        d = z - mu
        var = jnp.mean(d * d, axis=-1, keepdims=True)
        o_ref[...] = (d * lax.rsqrt(var + LN_EPS) * g_ref[...] + b_ref[...]).astype(BF16)

    vec = pl.BlockSpec((None, 1, w), lambda i: (layer, 0, 0))
    return pl.pallas_call(
        body, name="gmlp_norm_fwd", grid=(s // TM,),
        in_specs=[pl.BlockSpec((TM, w), lambda i: (i, 1)), vec, vec],
        out_specs=pl.BlockSpec((TM, w), lambda i: (i, 0)),
        out_shape=jax.ShapeDtypeStruct((s, w), BF16), compiler_params=_params(("parallel",)),
    )(h, g3, b3)


def _chunk_mask():
    t = lax.broadcasted_iota(jnp.int32, (GMLP_BLOCK, GMLP_BLOCK), 0)
    s = lax.broadcasted_iota(jnp.int32, (GMLP_BLOCK, GMLP_BLOCK), 1)
    return (s // CHUNK) <= (t // CHUNK)


SG_ROWS = 512


def _gate_fwd(h, vn, ws, bst):
    s, w = vn.shape
    gd = w // GMLP_GROUPS

    def body(h_ref, v_ref, ws_ref, bs_ref, o_ref):
        mask = _chunk_mask()
        for g in range(GMLP_GROUPS):
            wm = jnp.where(mask, ws_ref[g], 0.0).astype(BF16)
            bias = bs_ref[:, g:g + 1]
            cols = slice(g * gd, (g + 1) * gd)
            for n in range(SG_ROWS // GMLP_BLOCK):
                rows = slice(n * GMLP_BLOCK, (n + 1) * GMLP_BLOCK)
                sp = _dot(wm, v_ref[rows, cols], NN) + bias
                o_ref[rows, cols] = (jax.nn.gelu(h_ref[rows, cols]) * sp).astype(BF16)

    return pl.pallas_call(
        body, name="gate_fwd", grid=(s // SG_ROWS,),
        in_specs=[pl.BlockSpec((SG_ROWS, w), lambda i: (i, 0)), pl.BlockSpec((SG_ROWS, w), lambda i: (i, 0)),
                  pl.BlockSpec(ws.shape, lambda i: (0, 0, 0)), pl.BlockSpec(bst.shape, lambda i: (0, 0))],
        out_specs=pl.BlockSpec((SG_ROWS, w), lambda i: (i, 0)),
        out_shape=jax.ShapeDtypeStruct((s, w), BF16), compiler_params=_params(("parallel",)),
    )(h, vn, ws, bst)


def _gate_bwd(dgated, h, vn, ws, bst):
    s, w = vn.shape
    gd = w // GMLP_GROUPS
    nsteps = s // SG_ROWS

    def body(dg_ref, h_ref, v_ref, ws_ref, bs_ref, du_ref, dv_ref, dws_ref, dbs_ref, dsum):
        i = pl.program_id(0)

        @pl.when(i == 0)
        def _():
            dws_ref[...] = jnp.zeros_like(dws_ref)
            dsum[...] = jnp.zeros_like(dsum)

        mask = _chunk_mask()
        for g in range(GMLP_GROUPS):
            wm = jnp.where(mask, ws_ref[g], 0.0).astype(BF16)
            bias = bs_ref[:, g:g + 1]
            cols = slice(g * gd, (g + 1) * gd)
            dw = jnp.zeros((GMLP_BLOCK, GMLP_BLOCK), F32)
            dsg = jnp.zeros((GMLP_BLOCK, gd), F32)
            for n in range(SG_ROWS // GMLP_BLOCK):
                rows = slice(n * GMLP_BLOCK, (n + 1) * GMLP_BLOCK)
                vb = v_ref[rows, cols]
                sp = _dot(wm, vb, NN) + bias
                dg = dg_ref[rows, cols]
                du_ref[rows, cols] = dg * sp
                ds = dg * jax.nn.gelu(h_ref[rows, cols])
                dsb = ds.astype(BF16)
                dw += _dot(dsb, vb, NT)
                dsg += ds
                dv_ref[rows, cols] = _dot(wm, dsb, TN)
            dws_ref[g] += dw
            dsum[:, cols] += dsg

        @pl.when(i == nsteps - 1)
        def _():
            for g in range(GMLP_GROUPS):
                dws_ref[g] = jnp.where(mask, dws_ref[g], 0.0)
                tot = jnp.sum(dsum[:, g * gd:(g + 1) * gd], axis=-1, keepdims=True)
                dbs_ref[g] = jnp.broadcast_to(tot, (GMLP_BLOCK, LANES))

    tile = pl.BlockSpec((SG_ROWS, w), lambda i: (i, 0))
    return pl.pallas_call(
        body, name="gate_bwd", grid=(nsteps,),
        in_specs=[tile, tile, tile, pl.BlockSpec(ws.shape, lambda i: (0, 0, 0)), pl.BlockSpec(bst.shape, lambda i: (0, 0))],
        out_specs=[tile, tile, pl.BlockSpec(ws.shape, lambda i: (0, 0, 0)),
                   pl.BlockSpec((GMLP_GROUPS, GMLP_BLOCK, LANES), lambda i: (0, 0, 0))],
        out_shape=[jax.ShapeDtypeStruct((s, w), F32), jax.ShapeDtypeStruct((s, w), F32),
                   jax.ShapeDtypeStruct(ws.shape, F32), jax.ShapeDtypeStruct((GMLP_GROUPS, GMLP_BLOCK, LANES), F32)],
        scratch_shapes=[pltpu.VMEM((GMLP_BLOCK, w), F32)],
        compiler_params=_params(("arbitrary",)),
    )(dgated, h, vn, ws, bst)


GB_ROWS = 256


def _gmlp_in_bwd(h, du, dvn, g3, layer):
    s, w2 = h.shape
    w = w2 // 2
    nsteps = s // GB_ROWS

    def body(h_ref, du_ref, dv_ref, g_ref, dh_ref, dg_ref, db_ref):
        i = pl.program_id(0)

        @pl.when(i == 0)
        def _():
            dg_ref[...] = jnp.zeros_like(dg_ref)
            db_ref[...] = jnp.zeros_like(db_ref)

        hu = h_ref[:, :w]
        hv = h_ref[:, w:]
        dh_ref[:, :w] = (du_ref[...] * _gelu_grad(hu)).astype(BF16)
        z = jax.nn.gelu(hv)
        mu = jnp.mean(z, axis=-1, keepdims=True)
        d = z - mu
        var = jnp.mean(d * d, axis=-1, keepdims=True)
        rstd = lax.rsqrt(var + LN_EPS)
        xhat = d * rstd
        dy = dv_ref[...]
        db_ref[...] += jnp.sum(dy, axis=0, keepdims=True)
        dg_ref[...] += jnp.sum(dy * xhat, axis=0, keepdims=True)
        dxh = dy * g_ref[...]
        m1 = jnp.mean(dxh, axis=-1, keepdims=True)
        m2 = jnp.mean(dxh * xhat, axis=-1, keepdims=True)
        dz = rstd * (dxh - m1 - xhat * m2)
        dh_ref[:, w:] = (dz * _gelu_grad(hv)).astype(BF16)

    half = pl.BlockSpec((GB_ROWS, w), lambda i: (i, 0))
    vec = pl.BlockSpec((1, w), lambda i: (0, 0))
    return pl.pallas_call(
        body, name="gmlp_in_bwd", grid=(nsteps,),
        in_specs=[pl.BlockSpec((GB_ROWS, w2), lambda i: (i, 0)), half, half,
                  pl.BlockSpec((None, 1, w), lambda i: (layer, 0, 0))],
        out_specs=[pl.BlockSpec((GB_ROWS, w2), lambda i: (i, 0)), vec, vec],
        out_shape=[jax.ShapeDtypeStruct((s, w2), BF16), jax.ShapeDtypeStruct((1, w), F32), jax.ShapeDtypeStruct((1, w), F32)],
        compiler_params=_params(("arbitrary",)),
    )(h, du, dvn, g3)


def _ln_bwd_rows(dy, xh_ref, rs_ref, g_ref, first, dr_ref, drb_ref, dg_ref, db_ref):
    @pl.when(first)
    def _():
        dg_ref[...] = jnp.zeros_like(dg_ref)
        db_ref[...] = jnp.zeros_like(db_ref)

    xhat = xh_ref[...]
    db_ref[...] += jnp.sum(dy, axis=0, keepdims=True)
    dg_ref[...] += jnp.sum(dy * xhat, axis=0, keepdims=True)
    dxh = dy * g_ref[...]
    m1 = jnp.mean(dxh, axis=-1, keepdims=True)
    m2 = jnp.mean(dxh * xhat, axis=-1, keepdims=True)
    dr = rs_ref[...] * (dxh - m1 - xhat * m2)
    dr_ref[...] = dr
    drb_ref[...] = dr.astype(BF16)


def _ln_bwd(dy, xhat, rstd, g3, layer, target=None):
    s, d = dy.shape
    nsteps = s // TM
    head = target is not None

    def body(*refs):
        dy_ref, xh_ref, rs_ref, g_ref = refs[:4]
        dr_ref, drb_ref, dg_ref, db_ref = refs[4 + head:8 + head]
        first = pl.program_id(0) == 0
        dyv = dy_ref[...]
        if head:
            l_ref = refs[8 + head]

            @pl.when(first)
            def _():
                l_ref[...] = jnp.zeros_like(l_ref)

            e = dyv - refs[4][...]
            l_ref[...] += jnp.sum(jnp.sum(e * e, axis=1, keepdims=True), axis=0, keepdims=True)
            dyv = e * (1.0 / d)
        _ln_bwd_rows(dyv, xh_ref, rs_ref, g_ref, first, dr_ref, drb_ref, dg_ref, db_ref)

    tile = pl.BlockSpec((TM, d), lambda i: (i, 0))
    vec = pl.BlockSpec((1, d), lambda i: (0, 0))
    one = pl.BlockSpec((1, 1), lambda i: (0, 0))
    return pl.pallas_call(
        body, name="ln_bwd", grid=(nsteps,),
        in_specs=[tile, tile, pl.BlockSpec((TM, 1), lambda i: (i, 0)), pl.BlockSpec((None, 1, d), lambda i: (layer, 0, 0))]
        + ([tile] if head else []),
        out_specs=[tile, tile, vec, vec] + ([one] if head else []),
        out_shape=[jax.ShapeDtypeStruct((s, d), F32), jax.ShapeDtypeStruct((s, d), BF16),
                   jax.ShapeDtypeStruct((1, d), F32), jax.ShapeDtypeStruct((1, d), F32)]
        + ([jax.ShapeDtypeStruct((1, 1), F32)] if head else []),
        compiler_params=_params(("arbitrary",)),
    )(dy, xhat, rstd, g3, *([target] if head else []))


LOG2E = 1.4426950408889634
DEAD_LOG2 = -160.0
FIRST_LANE = 1


def _sb_terms(z, causal):
    z2 = z * LOG2E
    e = jnp.exp2(-jnp.abs(z2))
    l1p = jnp.log2(1.0 + e)
    lb = jnp.minimum(z2, 0.0) - l1p
    lr = lb - z2
    if causal is not None:
        lr = jnp.where(causal, lr, 0.0)
    return lb, lr, e


def _split_hi_lo(x):
    hi = x.astype(BF16)
    lo = (x - hi.astype(F32)).astype(BF16)
    return jnp.concatenate([hi, lo], axis=1)


def _att_consts(prefix):
    r = lax.broadcasted_iota(jnp.int32, (2 * ATT_T, ATT_T), 0) % ATT_T
    c = lax.broadcasted_iota(jnp.int32, (2 * ATT_T, ATT_T), 1)
    tri2 = jnp.where((r <= c) if prefix else (r >= c), 1.0, 0.0).astype(BF16)
    r = lax.broadcasted_iota(jnp.int32, (ATT_T, ATT_T), 0)
    c = lax.broadcasted_iota(jnp.int32, (ATT_T, ATT_T), 1)
    causal = c < r
    head_a = lax.broadcasted_iota(jnp.int32, (1, LANES), 1) < HEAD_DIM
    return tri2, causal, head_a


def _attn_fwd(q, k, v):
    s, d = q.shape
    nq = s // ATT_T

    def body(q_ref, k_ref, v_ref, ob_ref, lsum_ref, acc_a, acc_b, rem_a, rem_b):
        i = pl.program_id(1)
        tri, causal, head_a = _att_consts(prefix=False)
        q2 = q_ref[...]
        zero = jnp.zeros_like(q2)
        qa = jnp.where(head_a, q2, zero)
        qb = jnp.where(head_a, zero, q2)
        acc_a[...] = jnp.zeros_like(acc_a)
        acc_b[...] = jnp.zeros_like(acc_b)
        rem_a[...] = jnp.zeros_like(rem_a)
        rem_b[...] = jnp.zeros_like(rem_b)

        def block(kb, mask):
            rows = pl.ds(pl.multiple_of(kb * ATT_T, ATT_T), ATT_T)
            k2 = k_ref[rows, :]
            v2 = v_ref[rows, :]
            heads = ((qa, acc_a, rem_a), (qb, acc_b, rem_b))
            zs = [_dot(qm, k2, NT) for qm, _, _ in heads]
            terms = [_sb_terms(z, mask) for z in zs]
            sums = [_dot(_split_hi_lo(lr), tri, NN) for _, lr, _ in terms]
            for (_, acc, rem), (lb, lr, _), sincl in zip(heads, terms, sums):
                a = jnp.exp2(lb + (sincl - lr) + rem[...])
                if mask is not None:
                    a = jnp.where(mask, a, 0.0)
                rem[...] += sincl[:, 0:1]
                acc[...] += _dot(a.astype(BF16), v2, NN)

        block(i, causal)

        def live():
            return jnp.maximum(jnp.max(rem_a[...]), jnp.max(rem_b[...])) > DEAD_LOG2

        def go_on(carry):
            t, alive = carry
            return (t < i) & alive

        def step(carry):
            t, _ = carry
            block(i - 1 - t, None)
            return t + 1, live()

        done, _ = lax.while_loop(go_on, step, (jnp.int32(0), live()))
        first = (i - done).astype(F32)
        ob_ref[...] = jnp.where(head_a, acc_a[...], acc_b[...]).astype(BF16)
        lane = lax.broadcasted_iota(jnp.int32, (1, LANES), 1)
        lsum_ref[...] = jnp.where(lane == FIRST_LANE, first, jnp.where(head_a, rem_a[...], rem_b[...]))

    qspec = pl.BlockSpec((ATT_T, LANES), lambda p, i: (i, p))
    kspec = pl.BlockSpec((s, LANES), lambda p, i: (0, p))
    return pl.pallas_call(
        body, name="attn_fwd", grid=(d // LANES, nq), in_specs=[qspec, kspec, kspec],
        out_specs=[qspec, qspec],
        out_shape=[jax.ShapeDtypeStruct((s, d), BF16), jax.ShapeDtypeStruct((s, d), F32)],
        scratch_shapes=[pltpu.VMEM((ATT_T, LANES), F32), pltpu.VMEM((ATT_T, LANES), F32),
                        pltpu.VMEM((ATT_T, 1), F32), pltpu.VMEM((ATT_T, 1), F32)],
        compiler_params=_params(("parallel", "arbitrary")),
    )(q, k, v)


def _attn_bwd(q, k, v, do, lsum, dk_prev=None, dv_prev=None):
    s, d = q.shape
    nq = s // ATT_T
    has_prev = dk_prev is not None

    def body(*refs):
        q_ref, k_ref, v_ref, do_ref, ls_ref = refs[:5]
        n_in = 7 if has_prev else 5
        dq_ref, dk_ref, dv_ref, acc_a, acc_b, pre_a, pre_b, gp_a, gp_b, dkt, dvt = refs[n_in:]
        i = pl.program_id(1)

        @pl.when(i == 0)
        def _():
            dkt[...] = jnp.zeros_like(dkt)
            dvt[...] = jnp.zeros_like(dvt)

        tri, causal, head_a = _att_consts(prefix=True)
        q2 = q_ref[...]
        zero = jnp.zeros_like(q2)
        qa = jnp.where(head_a, q2, zero)
        qb = jnp.where(head_a, zero, q2)
        do2 = do_ref[...]
        doa = jnp.where(head_a, do2, 0.0).astype(BF16)
        dob = jnp.where(head_a, 0.0, do2).astype(BF16)
        row_a = lax.broadcasted_iota(jnp.int32, (LANES, 1), 0) < HEAD_DIM
        qt = q2.astype(F32).T
        dot_ = do2.T
        qta, qtb = jnp.where(row_a, qt, 0.0).astype(BF16), jnp.where(row_a, 0.0, qt).astype(BF16)
        dota, dotb = jnp.where(row_a, dot_, 0.0).astype(BF16), jnp.where(row_a, 0.0, dot_).astype(BF16)
        ls2 = ls_ref[...]
        tot_a = ls2[:, 0:1]
        tot_b = ls2[:, HEAD_DIM:HEAD_DIM + 1]
        for r in (acc_a, acc_b, pre_a, pre_b, gp_a, gp_b):
            r[...] = jnp.zeros_like(r)

        def block(kb, mask):
            rows = pl.ds(pl.multiple_of(kb * ATT_T, ATT_T), ATT_T)
            k2 = k_ref[rows, :]
            v2 = v_ref[rows, :]
            dk_new = jnp.zeros((LANES, ATT_T), F32)
            dv_new = jnp.zeros((LANES, ATT_T), F32)
            heads = ((qa, doa, tot_a, acc_a, pre_a, gp_a, qta, dota), (qb, dob, tot_b, acc_b, pre_b, gp_b, qtb, dotb))
            zs = [_dot(h[0], k2, NT) for h in heads]
            das = [_dot(h[1], v2, NT) for h in heads]
            terms = [_sb_terms(z, mask) for z in zs]
            psums = [_dot(_split_hi_lo(lr), tri, NN) for _, lr, _ in terms]
            gs, abs_ = [], []
            for h, (lb, _, _), pincl, da in zip(heads, terms, psums, das):
                tot, pre = h[2], h[4]
                a = jnp.exp2(lb + (tot - (pre[...] + pincl)))
                if mask is not None:
                    a = jnp.where(mask, a, 0.0)
                pre[...] += pincl[:, ATT_T - 1:ATT_T]
                gs.append(a * da)
                abs_.append(a.astype(BF16))
            gsums = [_dot(g.astype(BF16), tri[:ATT_T], NN) for g in gs]
            dzs = []
            for h, z, (_, _, e), g, gincl in zip(heads, zs, terms, gs, gsums):
                gpre = h[5]
                gbefore = gpre[...] + (gincl - g)
                gpre[...] += gincl[:, ATT_T - 1:ATT_T]
                inv = 1.0 / (1.0 + e)
                beta = jnp.where(z >= 0.0, inv, e * inv)
                dz = g - beta * (g + gbefore)
                if mask is not None:
                    dz = jnp.where(mask, dz, 0.0)
                dzs.append(dz.astype(BF16))
            for h, ab, dzb in zip(heads, abs_, dzs):
                dv_new += _dot(h[7], ab, NN)
                dk_new += _dot(h[6], dzb, NN)
                h[3][...] += _dot(dzb, k2, NN)
            cols = pl.ds(pl.multiple_of(kb * ATT_T, ATT_T), ATT_T)
            dkt[:, cols] += dk_new
            dvt[:, cols] += dv_new

        def step(kb, carry):
            block(kb, None)
            return carry

        first = jnp.clip(jnp.max(ls2[:, FIRST_LANE:FIRST_LANE + 1]).astype(jnp.int32), 0, i)
        lax.fori_loop(first, i, step, 0)
        block(i, causal)
        dq_ref[...] = (jnp.where(head_a, acc_a[...], acc_b[...]) * (HEAD_DIM ** -0.5)).astype(BF16)

        @pl.when(i == nq - 1)
        def _():
            for n in range(nq):
                rows = slice(n * ATT_T, (n + 1) * ATT_T)
                dkn, dvn = dkt[:, rows].T, dvt[:, rows].T
                if has_prev:
                    dkn, dvn = dkn + refs[5][rows, :], dvn + refs[6][rows, :]
                dk_ref[rows, :] = dkn
                dv_ref[rows, :] = dvn

    qspec = pl.BlockSpec((ATT_T, LANES), lambda p, i: (i, p))
    kspec = pl.BlockSpec((s, LANES), lambda p, i: (0, p))
    ins = [q, k, v, do, lsum] + ([dk_prev, dv_prev] if has_prev else [])
    return pl.pallas_call(
        body, name="attn_bwd", grid=(d // LANES, nq),
        in_specs=[qspec, kspec, kspec, qspec, qspec] + ([kspec, kspec] if has_prev else []),
        out_specs=[qspec, kspec, kspec],
        out_shape=[jax.ShapeDtypeStruct((s, d), BF16), jax.ShapeDtypeStruct((s, d), F32), jax.ShapeDtypeStruct((s, d), F32)],
        scratch_shapes=[pltpu.VMEM((ATT_T, LANES), F32), pltpu.VMEM((ATT_T, LANES), F32)]
        + [pltpu.VMEM((ATT_T, 1), F32)] * 4 + [pltpu.VMEM((LANES, s), F32)] * 2,
        compiler_params=_params(("parallel", "arbitrary")),
    )(*ins)


def _place():
    x, y, c = lax.axis_index("x"), lax.axis_index("y"), lax.axis_index("c")
    chips = [(1 - x, y), (x, 1 - y), (1 - x, 1 - y)]
    return x, y, c, chips


def _any_specs(n):
    return [pl.BlockSpec(memory_space=pl.ANY)] * n


def _gather_weights(bufs):
    n = len(bufs)

    def body(*refs):
        outs = refs[n:2 * n]
        send_sems, recv_sems = refs[2 * n:]
        x, y, c, chips = _place()
        me = 2 * x + y
        sibling = (x, y, 1 - c)

        def half(a, blk, hc):
            h = outs[a].shape[1] // 2
            return outs[a].at[blk, pl.ds(hc * h, h)]

        def copy(a, k, part, to):
            return pltpu.make_async_remote_copy(src_ref=part, dst_ref=part, send_sem=send_sems.at[a, k],
                                                recv_sem=recv_sems.at[a, k], device_id=to, device_id_type=MESH)

        sent = []
        for a in range(n):
            for k, chip in enumerate(chips):
                sent.append(copy(a, k, half(a, me, c), (*chip, c)))
                sent[-1].start()
        for a in range(n):
            for k, chip in enumerate(chips):
                blk = 2 * chip[0] + chip[1]
                copy(a, k, half(a, blk, c), sibling).wait_recv()
                sent.append(copy(a, 3 + k, half(a, blk, c), sibling))
                sent[-1].start()
        for a in range(n):
            for k, chip in enumerate(chips):
                blk = 2 * chip[0] + chip[1]
                copy(a, 3 + k, half(a, blk, 1 - c), sibling).wait_recv()
        for cp in sent:
            cp.wait_send()

    return pl.pallas_call(
        body, name="gather_weights", in_specs=_any_specs(n), out_specs=_any_specs(n),
        out_shape=[jax.ShapeDtypeStruct(w.shape, w.dtype) for w in bufs],
        input_output_aliases={a: a for a in range(n)},
        scratch_shapes=[pltpu.SemaphoreType.DMA((n, 6)), pltpu.SemaphoreType.DMA((n, 6))],
        compiler_params=pltpu.CompilerParams(has_side_effects=True),
    )(*bufs)


def _pair_exchange(grads):
    n = len(grads)

    def body(*refs):
        ins, outs = refs[:n], refs[n:2 * n]
        send_sems, recv_sems = refs[2 * n:]
        x, y, c, _ = _place()
        cps = []
        for a in range(n):
            h = ins[a].shape[1] // 2
            cps.append(pltpu.make_async_remote_copy(
                src_ref=ins[a].at[:, pl.ds((1 - c) * h, h)], dst_ref=outs[a], send_sem=send_sems.at[a],
                recv_sem=recv_sems.at[a], device_id=(x, y, 1 - c), device_id_type=MESH))
            cps[-1].start()
        for cp in cps:
            cp.wait()

    return pl.pallas_call(
        body, name="pair_exchange", in_specs=_any_specs(n), out_specs=_any_specs(n),
        out_shape=[jax.ShapeDtypeStruct((g.shape[0], g.shape[1] // 2, g.shape[2]), g.dtype) for g in grads],
        scratch_shapes=[pltpu.SemaphoreType.DMA((n,)), pltpu.SemaphoreType.DMA((n,))],
        compiler_params=pltpu.CompilerParams(has_side_effects=True),
    )(*grads)


def _chip_exchange(parts):
    n = len(parts)

    def body(*refs):
        ins, outs = refs[:n], refs[n:2 * n]
        send_sems, recv_sems = refs[2 * n:]
        x, y, c, chips = _place()
        me = 2 * x + y
        cps = []
        for a in range(n):
            for k, chip in enumerate(chips):
                blk = 2 * chip[0] + chip[1]
                cps.append(pltpu.make_async_remote_copy(
                    src_ref=ins[a].at[blk], dst_ref=outs[a].at[me], send_sem=send_sems.at[a, k],
                    recv_sem=recv_sems.at[a, k], device_id=(*chip, c), device_id_type=MESH))
                cps[-1].start()
        for a in range(n):
            for k, chip in enumerate(chips):
                blk = 2 * chip[0] + chip[1]
                pltpu.make_async_remote_copy(
                    src_ref=ins[a].at[blk], dst_ref=outs[a].at[blk], send_sem=send_sems.at[a, k],
                    recv_sem=recv_sems.at[a, k], device_id=(*chip, c), device_id_type=MESH).wait_recv()
        for cp in cps:
            cp.wait_send()

    return pl.pallas_call(
        body, name="chip_exchange", in_specs=_any_specs(n), out_specs=_any_specs(n),
        out_shape=[jax.ShapeDtypeStruct(p.shape, p.dtype) for p in parts],
        scratch_shapes=[pltpu.SemaphoreType.DMA((n, 3)), pltpu.SemaphoreType.DMA((n, 3))],
        compiler_params=pltpu.CompilerParams(has_side_effects=True),
    )(*parts)


def _half_swap(halves):
    n = len(halves)

    def body(*refs):
        outs = refs[n:2 * n]
        send_sems, recv_sems = refs[2 * n:]
        x, y, c, _ = _place()
        cps = []
        for a in range(n):
            h = outs[a].shape[1] // 2
            mine = outs[a].at[:, pl.ds(c * h, h)]
            cps.append(pltpu.make_async_remote_copy(
                src_ref=mine, dst_ref=mine, send_sem=send_sems.at[a], recv_sem=recv_sems.at[a],
                device_id=(x, y, 1 - c), device_id_type=MESH))
            cps[-1].start()
        for cp in cps:
            cp.wait()

    return pl.pallas_call(
        body, name="half_swap", in_specs=_any_specs(n), out_specs=_any_specs(n),
        out_shape=[jax.ShapeDtypeStruct(p.shape, p.dtype) for p in halves],
        input_output_aliases={a: a for a in range(n)},
        scratch_shapes=[pltpu.SemaphoreType.DMA((n,)), pltpu.SemaphoreType.DMA((n,))],
        compiler_params=pltpu.CompilerParams(has_side_effects=True),
    )(*halves)


N_DEV = 8


def _all_reduce_small(v):
    nrow, ncol = v.shape

    def body(v_ref, o_ref, land, red, send_sems, recv_sems, send2, recv2, loc_sem):
        x, y, c, _ = _place()
        me = 4 * x + 2 * y + c
        peers = []
        for k in range(1, N_DEV):
            peers.append((x ^ ((k >> 2) & 1), y ^ ((k >> 1) & 1), c ^ (k & 1)))
        own = pltpu.make_async_copy(v_ref.at[pl.ds(me, 1)], land.at[pl.ds(me, 1)], loc_sem)
        own.start()
        cps = []
        for k, peer in enumerate(peers):
            dev = 4 * peer[0] + 2 * peer[1] + peer[2]
            cps.append(pltpu.make_async_remote_copy(
                src_ref=v_ref.at[pl.ds(dev, 1)], dst_ref=land.at[pl.ds(me, 1)], send_sem=send_sems.at[k],
                recv_sem=recv_sems.at[k], device_id=peer, device_id_type=MESH))
            cps[-1].start()
        for k, peer in enumerate(peers):
            dev = 4 * peer[0] + 2 * peer[1] + peer[2]
            pltpu.make_async_remote_copy(
                src_ref=v_ref.at[pl.ds(dev, 1)], dst_ref=land.at[pl.ds(dev, 1)], send_sem=send_sems.at[k],
                recv_sem=recv_sems.at[k], device_id=peer, device_id_type=MESH).wait_recv()
        for cp in cps:
            cp.wait_send()
        own.wait()
        terms = land[...]
        total = terms[0:1, :]
        for d in range(1, N_DEV):
            total = total + terms[d:d + 1, :]
        red[...] = total
        own = pltpu.make_async_copy(red, o_ref.at[pl.ds(me, 1)], loc_sem)
        own.start()
        cps = []
        for k, peer in enumerate(peers):
            cps.append(pltpu.make_async_remote_copy(
                src_ref=red, dst_ref=o_ref.at[pl.ds(me, 1)], send_sem=send2.at[k],
                recv_sem=recv2.at[k], device_id=peer, device_id_type=MESH))
            cps[-1].start()
        for k, peer in enumerate(peers):
            dev = 4 * peer[0] + 2 * peer[1] + peer[2]
            pltpu.make_async_remote_copy(
                src_ref=red, dst_ref=o_ref.at[pl.ds(dev, 1)], send_sem=send2.at[k],
                recv_sem=recv2.at[k], device_id=peer, device_id_type=MESH).wait_recv()
        for cp in cps:
            cp.wait_send()
        own.wait()

    vm = pl.BlockSpec(memory_space=pltpu.VMEM)
    return pl.pallas_call(
        body, name="all_reduce_small", in_specs=[vm], out_specs=vm,
        out_shape=jax.ShapeDtypeStruct((nrow, ncol), F32),
        scratch_shapes=[pltpu.VMEM((nrow, ncol), F32), pltpu.VMEM((1, ncol), F32)]
        + [pltpu.SemaphoreType.DMA((N_DEV - 1,))] * 4 + [pltpu.SemaphoreType.DMA],
        compiler_params=pltpu.CompilerParams(has_side_effects=True, vmem_limit_bytes=VMEM_LIMIT),
    )(v)


def _row_tile(rows):
    return min(rows, 512)


def _pair_sum(g, got, place):
    nb, r, c = g.shape
    h = r // 2
    tr = _row_tile(h)
    nt = h // tr

    def body(place_ref, g_ref, got_ref, p_ref, pb_ref):
        p = g_ref[...] + got_ref[...]
        pb_ref[...] = p.astype(BF16)

        @pl.when(pl.program_id(1) == place_ref[0])
        def _():
            p_ref[...] = p

    spec = pl.BlockSpec((None, tr, c), lambda t, j, place_ref: (j, t, 0))
    grid_spec = pltpu.PrefetchScalarGridSpec(
        num_scalar_prefetch=1, grid=(nt, nb),
        in_specs=[pl.BlockSpec((None, tr, c), lambda t, j, place_ref: (j, place_ref[1] * nt + t, 0)), spec],
        out_specs=[pl.BlockSpec((tr, c), lambda t, j, place_ref: (t, 0)), spec])
    return pl.pallas_call(
        body, name="pair_sum", grid_spec=grid_spec,
        out_shape=[jax.ShapeDtypeStruct((h, c), F32), jax.ShapeDtypeStruct((nb, h, c), BF16)],
        compiler_params=_params(("parallel", "arbitrary")),
    )(place, g, got)


def _chip_sum(p, got, place, out, layer):
    h, c = p.shape
    tr = _row_tile(h)
    nt = h // tr

    def body(place_ref, p_ref, g1_ref, g2_ref, g3_ref, old_ref, o_ref):
        o_ref[...] = ((p_ref[...] + g1_ref[...].astype(F32)) + g2_ref[...].astype(F32)) + g3_ref[...].astype(F32)

    def blk(off):
        return pl.BlockSpec((None, tr, c), lambda t, place_ref: ((place_ref[0] + off) % N_CHIPS, t, 0))

    grid_spec = pltpu.PrefetchScalarGridSpec(
        num_scalar_prefetch=1, grid=(nt,),
        in_specs=[pl.BlockSpec((tr, c), lambda t, place_ref: (t, 0)), blk(1), blk(2), blk(3),
                  pl.BlockSpec(memory_space=pl.ANY)],
        out_specs=pl.BlockSpec((None, tr, c), lambda t, place_ref: (layer, place_ref[1] * nt + t, 0)))
    return pl.pallas_call(
        body, name="chip_sum", grid_spec=grid_spec, out_shape=jax.ShapeDtypeStruct(out.shape, F32),
        input_output_aliases={5: 0}, compiler_params=_params(("parallel",)),
    )(place, p, got, got, got, out)


def _adamw(w, g, m, v):
    r, c = w.shape
    tr = r if r < 8 else _row_tile(r)

    def body(w_ref, g_ref, m_ref, v_ref, d_ref, nm_ref, nv_ref):
        gv = g_ref[...]
        nm = ADAM_B1 * m_ref[...] + (1.0 - ADAM_B1) * gv
        nv = ADAM_B2 * v_ref[...] + (1.0 - ADAM_B2) * (gv * gv)
        m_hat = nm / (1.0 - ADAM_B1 ** ADAM_STEP)
        v_hat = nv / (1.0 - ADAM_B2 ** ADAM_STEP)
        d_ref[...] = -ADAM_LR * (m_hat / (jnp.sqrt(v_hat) + ADAM_EPS) + ADAM_WD * w_ref[...])
        nm_ref[...] = nm
        nv_ref[...] = nv

    tile = pl.BlockSpec((tr, c), lambda i: (i, 0))
    return pl.pallas_call(
        body, name="adamw", grid=(r // tr,), in_specs=[tile] * 4, out_specs=[tile] * 3,
        out_shape=[jax.ShapeDtypeStruct((r, c), F32)] * 3, compiler_params=_params(("parallel",)),
    )(w, g, m, v)


BIG = ("a_w_in", "a_w_out", "sb_w_k", "sb_w_v", "b_w_q", "b_w_o", "ffn_w1", "ffn_w2")
SMALL = ("a_ln_g", "a_ln_b", "a_w_s", "a_b_s", "mix_ln_g", "mix_ln_b", "ffn_ln_g", "ffn_ln_b")
COL_SHARDED = {"a_w_in": True, "a_w_out": False, "sb_w_k": False, "sb_w_v": False, "b_w_q": False, "b_w_o": False,
               "ffn_w1": True, "ffn_w2": False}


def kernel(x, a_w_in, a_ln_g, a_ln_b, a_w_s, a_b_s, a_w_out, sb_w_k, sb_w_v, b_w_q, b_w_o, mix_ln_g, mix_ln_b, ffn_ln_g, ffn_ln_b, ffn_w1, ffn_w2, loss_target, m_a_w_in, m_a_ln_g, m_a_ln_b, m_a_w_s, m_a_b_s, m_a_w_out, m_sb_w_k, m_sb_w_v, m_b_w_q, m_b_w_o, m_mix_ln_g, m_mix_ln_b, m_ffn_ln_g, m_ffn_ln_b, m_ffn_w1, m_ffn_w2, v_a_w_in, v_a_ln_g, v_a_ln_b, v_a_w_s, v_a_b_s, v_a_w_out, v_sb_w_k, v_sb_w_v, v_b_w_q, v_b_w_o, v_mix_ln_g, v_mix_ln_b, v_ffn_ln_g, v_ffn_ln_b, v_ffn_w1, v_ffn_w2):
    names = BIG + SMALL
    given = dict(a_w_in=a_w_in, a_ln_g=a_ln_g, a_ln_b=a_ln_b, a_w_s=a_w_s, a_b_s=a_b_s, a_w_out=a_w_out, sb_w_k=sb_w_k,
                 sb_w_v=sb_w_v, b_w_q=b_w_q, b_w_o=b_w_o, mix_ln_g=mix_ln_g, mix_ln_b=mix_ln_b, ffn_ln_g=ffn_ln_g,
                 ffn_ln_b=ffn_ln_b, ffn_w1=ffn_w1, ffn_w2=ffn_w2)
    mom = dict(a_w_in=m_a_w_in, a_ln_g=m_a_ln_g, a_ln_b=m_a_ln_b, a_w_s=m_a_w_s, a_b_s=m_a_b_s, a_w_out=m_a_w_out,
               sb_w_k=m_sb_w_k, sb_w_v=m_sb_w_v, b_w_q=m_b_w_q, b_w_o=m_b_w_o, mix_ln_g=m_mix_ln_g, mix_ln_b=m_mix_ln_b,
               ffn_ln_g=m_ffn_ln_g, ffn_ln_b=m_ffn_ln_b, ffn_w1=m_ffn_w1, ffn_w2=m_ffn_w2)
    var = dict(a_w_in=v_a_w_in, a_ln_g=v_a_ln_g, a_ln_b=v_a_ln_b, a_w_s=v_a_w_s, a_b_s=v_a_b_s, a_w_out=v_a_w_out,
               sb_w_k=v_sb_w_k, sb_w_v=v_sb_w_v, b_w_q=v_b_w_q, b_w_o=v_b_w_o, mix_ln_g=v_mix_ln_g, mix_ln_b=v_mix_ln_b,
               ffn_ln_g=v_ffn_ln_g, ffn_ln_b=v_ffn_ln_b, ffn_w1=v_ffn_w1, ffn_w2=v_ffn_w2)

    cx, cy, cc = lax.axis_index("x"), lax.axis_index("y"), lax.axis_index("c")
    chip = (2 * cx + cy).astype(jnp.int32)
    chip_arr = chip.reshape(1)

    s, d = x.shape[1], x.shape[2]
    xf = x.reshape(s, d)
    target = loss_target.reshape(s, d)

    def as2d(w):
        return w.reshape(-1, w.shape[-1])

    gw = {}
    for n in BIG:
        for l in ([None] if given[n].ndim == 2 else range(given[n].shape[0])):
            gw[(n, l)] = _cast_into_slot(given[n], l, chip_arr)
    ln_gb = jnp.stack([a_ln_g, a_ln_b])
    ln_slot = lax.dynamic_update_slice(jnp.zeros((N_CHIPS,) + ln_gb.shape, F32), ln_gb[None], (chip, 0, 0, 0))
    layer0 = [("a_w_in", 0), ("a_w_out", 0)]
    gathered = _gather_weights([gw[k] for k in layer0] + [ln_slot])
    gw.update(zip(layer0, gathered[:-1]))
    mixer = {1: [("a_w_in", 1), ("a_w_out", 1)], 2: [("sb_w_k", None), ("sb_w_v", None), ("b_w_q", 0), ("b_w_o", 0)],
             3: [("b_w_q", 1), ("b_w_o", 1)]}

    def riding(d2d=(), ici=()):
        keys = list(d2d) + list(ici)
        return keys, [("d2d", gw[k]) for k in d2d] + [("ici", gw[k]) for k in ici]

    def landed_in(keys, bufs):
        gw.update(zip(keys, bufs))

    ln_full = gathered[-1].transpose(1, 2, 0, 3).reshape(2, N_A, 1, -1)
    a_ln_g3, a_ln_b3 = ln_full[0], ln_full[1]
    mix_g3, mix_b3 = mix_ln_g[:, None, :], mix_ln_b[:, None, :]
    ffn_g3, ffn_b3 = ffn_ln_g[:, None, :], ffn_ln_b[:, None, :]
    bst = jnp.swapaxes(a_b_s, 1, 2)

    saved = []
    xb = _cast_bf16(xf)
    kb = vb = None
    for l in range(DEPTH):
        sv = dict(x_in=xb)
        last = l == DEPTH - 1
        if l == 0:
            keys, riders = riding(ici=[("ffn_w1", 0)])
        else:
            keys, riders = riding(d2d=[("ffn_w1", l), ("ffn_w2", l)], ici=mixer[l + 1] if l < N_A else [])
        if l < N_A:
            h, *bufs = _mm_fwd("a_in", xb, gw[("a_w_in", l)], None, True, riders=riders)
            landed_in(keys, bufs)
            vn = _gmlp_norm_fwd(h, a_ln_g3, a_ln_b3, l)
            gated = _gate_fwd(h, vn, a_w_s[l], bst[l])
            keys, riders = riding(d2d=[("ffn_w1", 0)], ici=[("ffn_w2", 0)]) if l == 0 else ([], [])
            xf, xb, xhat, rstd, *bufs = _mm_resid_ln("a_out", gated, gw[("a_w_out", l)], xf, mix_g3, mix_b3, l, riders)
            landed_in(keys, bufs)
            sv.update(h=h, vn=vn, gated=gated)
        else:
            j = l - N_A
            if l == N_A:
                kb, *bufs = _mm_fwd("sb_k", xb, gw[("sb_w_k", None)], None, False, _ep_bf16, outs=[(d, BF16)],
                                    riders=riders)
                landed_in(keys, bufs)
                keys, riders = [], ()
                vb = _mm_fwd("sb_v", xb, gw[("sb_w_v", None)], None, False, _ep_bf16, outs=[(d, BF16)])[0]
            q, *bufs = _mm_fwd("b_q", xb, gw[("b_w_q", j)], None, False, _ep_scale_q, outs=[(d, BF16)], riders=riders)
            landed_in(keys, bufs)
            ob, lsum = _attn_fwd(q, kb, vb)
            keys, riders = riding(ici=[] if last else mixer[l + 1])
            xf, xb, xhat, rstd, *bufs = _mm_resid_ln("b_out", ob, gw[("b_w_o", j)], xf, mix_g3, mix_b3, l, riders)
            landed_in(keys, bufs)
            sv.update(q=q, lsum=lsum, ob=ob)
        sv.update(x_mid=xb, xhat1=xhat, rstd1=rstd)
        dff = gw[("ffn_w1", l)].shape[-1] * N_CHIPS
        if l == 0:
            keys, riders = riding(d2d=[("ffn_w2", 0)], ici=mixer[1] + [("ffn_w1", 1)])
        else:
            keys, riders = riding(ici=[] if last else [("ffn_w1", l + 1)])
        pr, *bufs = _mm_fwd("ffn_1", xb, gw[("ffn_w1", l)], None, True, _ep_relu, outs=[(dff, BF16)], riders=riders)
        landed_in(keys, bufs)
        keys, riders = riding(d2d=[] if last else mixer[l + 1], ici=[] if last else [("ffn_w2", l + 1)])
        xf, xb, xhat, rstd, *bufs = _mm_resid_ln("ffn_2", pr, gw[("ffn_w2", l)], xf, ffn_g3, ffn_b3, l, riders, _square)
        landed_in(keys, bufs)
        sv.update(pr=pr, xhat2=xhat, rstd2=rstd)
        saved.append(sv)

    dx = xf

    pending = []
    pair_sums, landed = {}, {}
    place_arr = jnp.stack([chip, cc.astype(jnp.int32)])

    def arrived(took, outs):
        for (kind, key, arr), out in zip(took, outs):
            if kind == "pair":
                pair_sums[key] = _pair_sum(arr, out, place_arr)
                pending.append(("chip", key, pair_sums[key][1]))
            else:
                landed[key] = out

    def carrying(call, name, *args, **kw):
        took = []
        if name.startswith("ffn") or draining[0]:
            room = CARRIER_PARAMS
            for task in list(pending):
                size = given[task[1][0]].shape[-2] * given[task[1][0]].shape[-1] * N_CHIPS
                if task[0] == "pair" or room == CARRIER_PARAMS or size <= room:
                    took.append(task)
                    pending.remove(task)
                    room -= size if task[0] == "chip" else 0
        results = call(name, *args, riders=[(kind, arr) for kind, _, arr in took], **kw)
        own = len(results) - len(took)
        arrived(took, results[own:])
        return results[0] if own == 1 else results[:own]

    draining = [False]

    def bwd_act(*args, **kw):
        return carrying(_mm_bwd_act, *args, **kw)

    def bwd_w(key, name, a, dy, **kw):
        pending.append(("pair", key, carrying(_mm_bwd_w, name, a, dy, gw[key], COL_SHARDED[key[0]], **kw)))

    d_mix_g, d_mix_b, d_ffn_g, d_ffn_b = [None] * DEPTH, [None] * DEPTH, [None] * DEPTH, [None] * DEPTH
    d_ln_g, d_ln_b, d_ws, d_bs = [None] * N_A, [None] * N_A, [None] * N_A, [None] * N_A
    dk = dv = normed = None
    for l in reversed(range(DEPTH)):
        sv = saved[l]
        draining[0] = l == 0
        if l == DEPTH - 1:
            dr, drb, d_ffn_g[l], d_ffn_b[l], sq = _ln_bwd(dx, sv["xhat2"], sv["rstd2"], ffn_g3, l, target)
            loss = lax.psum(0.5 * sq[0, 0] / d, ("x", "y", "c"))
        elif normed:
            dr, drb = normed
            normed = None
        else:
            dr, drb, d_ffn_g[l], d_ffn_b[l] = _ln_bwd(dx, sv["xhat2"], sv["rstd2"], ffn_g3, l)
        dff = sv["pr"].shape[1]
        dhd = bwd_act("ffn_2_dx", drb, gw[("ffn_w2", l)], None, False, _ep_relu2_bwd, (sv["pr"],),
                      (pl.BlockSpec((_wide_tile(s), dff // N_CHIPS), lambda j, i, k: (i, j)),), out_dtype=BF16)
        bwd_w(("ffn_w2", l), "ffn_2_dw", sv["pr"], drb, a_fn=_square)
        dr, drb, d_mix_g[l], d_mix_b[l] = bwd_act(
            "ffn_1_dx", dhd, gw[("ffn_w1", l)], None, True, _ep_resid_ln_bwd, (dr, sv["xhat1"], sv["rstd1"], mix_g3),
            (_row_spec(d), _row_spec(d), _row_spec(1), _vec_spec(l, d)), through_norm=True)
        bwd_w(("ffn_w1", l), "ffn_1_dw", sv["x_mid"], dhd)

        quarter = pl.BlockSpec((_wide_tile(s), d // N_CHIPS), lambda j, i, k: (i, j))
        if l < N_A:
            dgated = bwd_act("a_out_dx", drb, gw[("a_w_out", l)], None, False)
            bwd_w(("a_w_out", l), "a_out_dw", sv["gated"], drb)
            du, dvn, d_ws[l], dbs_wide = _gate_bwd(dgated, sv["h"], sv["vn"], a_w_s[l], bst[l])
            d_bs[l] = dbs_wide[:, :, 0]
            dh, dlg, dlb = _gmlp_in_bwd(sv["h"], du, dvn, a_ln_g3, l)
            d_ln_g[l], d_ln_b[l] = dlg[0], dlb[0]
            if l:
                below = saved[l - 1]
                *normed, d_ffn_g[l - 1], d_ffn_b[l - 1] = bwd_act(
                    "a_in_dx", dh, gw[("a_w_in", l)], None, True, _ep_resid_ln_bwd,
                    (dr, below["xhat2"], below["rstd2"], ffn_g3),
                    (_row_spec(d), _row_spec(d), _row_spec(1), _vec_spec(l - 1, d)), through_norm=True)
            else:
                dx = bwd_act("a_in_dx", dh, gw[("a_w_in", l)], None, True, _ep_resid, (dr,), (_row_spec(d),))
            bwd_w(("a_w_in", l), "a_in_dw", sv["x_in"], dh)
        else:
            j = l - N_A
            do = bwd_act("b_out_dx", drb, gw[("b_w_o", j)], None, False)
            bwd_w(("b_w_o", j), "b_out_dw", sv["ob"], drb)
            dq, dk, dv = _attn_bwd(sv["q"], kb, vb, do, sv["lsum"], dk, dv)
            dx = bwd_act("b_q_dx", dq, gw[("b_w_q", j)], None, False, _ep_resid, (dr,), (quarter,))
            bwd_w(("b_w_q", j), "b_q_dw", sv["x_in"], dq)
            if l == N_A:
                dx = bwd_act("sb_k_dx", dk, gw[("sb_w_k", None)], None, False, _ep_add, (dx,), (quarter,))
                bwd_w(("sb_w_k", None), "sb_k_dw", sv["x_in"], dk)
                dx = bwd_act("sb_v_dx", dv, gw[("sb_w_v", None)], None, False, _ep_add, (dx,), (quarter,))
                bwd_w(("sb_w_v", None), "sb_v_dw", sv["x_in"], dv)
    grad_x = dx.reshape(x.shape)

    while pending:
        took = list(pending)
        pending.clear()
        for kind, exchange in (("pair", _pair_exchange), ("chip", _chip_exchange)):
            some = [t for t in took if t[0] == kind]
            if some:
                arrived(some, exchange([arr for _, _, arr in some]))

    stacked = []
    for n in BIG:
        layers = [None] if given[n].ndim == 2 else range(given[n].shape[0])
        out = lax.empty((len(layers),) + given[n].shape[-2:], F32)
        for at, l in enumerate(layers):
            out = _chip_sum(pair_sums[(n, l)][0], landed[(n, l)], place_arr, out, at)
        stacked.append(out)
    grads = {n: g.reshape(given[n].shape) for n, g in zip(BIG, _half_swap(stacked))}

    small_full = dict(a_ln_g=jnp.stack(d_ln_g), a_ln_b=jnp.stack(d_ln_b), a_w_s=jnp.stack(d_ws), a_b_s=jnp.stack(d_bs),
                      mix_ln_g=jnp.concatenate(d_mix_g), mix_ln_b=jnp.concatenate(d_mix_b),
                      ffn_ln_g=jnp.concatenate(d_ffn_g), ffn_ln_b=jnp.concatenate(d_ffn_b))
    packed = jnp.concatenate([small_full[n].reshape(-1) for n in SMALL])
    total = packed.shape[0]
    ncol = -(-total // (N_DEV * LANES)) * LANES
    packed = jnp.pad(packed, (0, N_DEV * ncol - total)).reshape(N_DEV, ncol)
    reduced = _all_reduce_small(packed).reshape(-1)
    off = 0
    for n in SMALL:
        size = small_full[n].size
        g = reduced[off:off + size].reshape(small_full[n].shape)
        off += size
        if n in ("a_ln_g", "a_ln_b"):
            wq = given[n].shape[1]
            g = lax.dynamic_slice_in_dim(g, chip * wq, wq, axis=1)
        grads[n] = g

    delta, new_m, new_v = {}, {}, {}
    for n in names:
        shape = given[n].shape
        dl, nm, nv = _adamw(as2d(given[n]), as2d(grads[n]), as2d(mom[n]), as2d(var[n]))
        delta[n], new_m[n], new_v[n] = dl.reshape(shape), nm.reshape(shape), nv.reshape(shape)

    order = ("a_w_in", "a_ln_g", "a_ln_b", "a_w_s", "a_b_s", "a_w_out", "sb_w_k", "sb_w_v", "b_w_q", "b_w_o",
             "mix_ln_g", "mix_ln_b", "ffn_ln_g", "ffn_ln_b", "ffn_w1", "ffn_w2")
    return (loss, grad_x, *[grads[n] for n in order], *[delta[n] for n in order],
            *[new_m[n] for n in order], *[new_v[n] for n in order])
```

```python
import math

import jax
import jax.numpy as jnp
from jax import lax
from jax.experimental import pallas as pl
from jax.experimental.pallas import tpu as pltpu

F32 = jnp.float32
BF16 = jnp.bfloat16
MESH = pl.DeviceIdType.MESH

N_CHIPS = 4
DEPTH = 4
N_A = 2
ALPHA = float((2 * DEPTH) ** 0.25)
LN_EPS = 1e-5
CHUNK = 64
GMLP_BLOCK = 128
GMLP_GROUPS = 8
HEAD_DIM = 64
LANES = 128
ATT_T = 256
ADAM_LR = 0.001
ADAM_B1 = 0.9
ADAM_B2 = 0.999
ADAM_EPS = 1e-08
ADAM_WD = 0.01
ADAM_STEP = 10
VMEM_LIMIT = 56 * 1024 * 1024
TM = 512
TM_WIDE = 1024
TS = 1024

NN = ((1,), (0,))
NT = ((1,), (1,))
TN = ((0,), (0,))


def _params(sem):
    return pltpu.CompilerParams(dimension_semantics=sem, vmem_limit_bytes=VMEM_LIMIT)


def _dot(a, b, contract):
    return lax.dot_general(a, b, (contract, ((), ())), preferred_element_type=F32)


def _rider_out(kind, arr):
    shape = (arr.shape[0], arr.shape[1] // 2, arr.shape[2]) if kind == "pair" else arr.shape
    return jax.ShapeDtypeStruct(shape, arr.dtype)


def _rider_copies(kind, src, dst, send_sems, recv_sems, base):
    x, y, c, chips = _place()
    me = 2 * x + y
    sibling = (x, y, 1 - c)

    def copy(k, part, land, to):
        return pltpu.make_async_remote_copy(src_ref=part, dst_ref=land, send_sem=send_sems.at[base + k],
                                            recv_sem=recv_sems.at[base + k], device_id=to, device_id_type=MESH)

    if kind == "pair":
        h = src.shape[1] // 2
        cp = copy(0, src.at[:, pl.ds((1 - c) * h, h)], dst, sibling)
        return [cp], [cp]
    h = dst.shape[1] // 2
    starts, arrivals = [], []
    for k, chip in enumerate(chips):
        blk = 2 * chip[0] + chip[1]
        if kind == "ici":
            starts.append(copy(k, dst.at[me, pl.ds(c * h, h)], dst.at[me, pl.ds(c * h, h)], (*chip, c)))
            arrivals.append(copy(k, dst.at[blk, pl.ds(c * h, h)], dst.at[blk, pl.ds(c * h, h)], (*chip, c)))
        elif kind == "d2d":
            starts.append(copy(k, dst.at[blk, pl.ds(c * h, h)], dst.at[blk, pl.ds(c * h, h)], sibling))
            arrivals.append(copy(k, dst.at[blk, pl.ds((1 - c) * h, h)], dst.at[blk, pl.ds((1 - c) * h, h)], sibling))
        else:
            starts.append(copy(k, src.at[blk], dst.at[me], (*chip, c)))
            arrivals.append(copy(k, src.at[blk], dst.at[blk], (*chip, c)))
    return starts, arrivals


RIDER_SEMS = 3
CARRIER_PARAMS = 5 * 2 ** 20


def _identity(a):
    return a


def _square(a):
    return a * a


def _matmul(name, operands, in_specs, out_shapes, out_specs, grid, contract, epilogue, acc_shape, aliases=None,
            chunks=None, riders=(), pick=False, a_fn=_identity, sequential=False):
    nk = grid[2]
    n_in, n_out, nr = len(operands), len(out_shapes), len(riders)
    n_plain = n_in + nr + n_out

    def body(*refs):
        ins, outs = refs[:n_in], refs[n_in + nr:n_plain]
        if nr:
            srcs, dsts = refs[n_in:n_in + nr], refs[n_plain:n_plain + nr]
            send_sems, recv_sems = refs[-2:]
            pid = [pl.program_id(ax) for ax in range(3)]
            first = (pid[0] == 0) & (pid[1] == 0) & (pid[2] == 0)
            last = (pid[0] == grid[0] - 1) & (pid[1] == grid[1] - 1) & (pid[2] == grid[2] - 1)

            def copies(n):
                return _rider_copies(riders[n][0], srcs[n], dsts[n], send_sems, recv_sems, RIDER_SEMS * n)

            @pl.when(first)
            def _():
                for n in range(nr):
                    for cp in copies(n)[0]:
                        cp.start()

        compute(refs, ins, outs)
        if nr:
            @pl.when(last)
            def _():
                for n in range(nr):
                    starts, arrivals = copies(n)
                    for cp in arrivals:
                        cp.wait_recv()
                    for cp in starts:
                        cp.wait_send()

    def compute(refs, ins, outs):
        if chunks is None:
            b = ins[1][pl.program_id(1)] if pick else ins[1][...]
            p = _dot(a_fn(ins[0][...].astype(BF16)), b.astype(BF16), contract)
        else:
            width = ins[0].shape[1] // chunks
            p = None
            for j in range(chunks):
                pj = _dot(a_fn(ins[0][:, j * width:(j + 1) * width].astype(BF16)), ins[1][j].astype(BF16), contract)
                p = pj if p is None else p + pj
        if nk == 1:
            epilogue(p, ins[2:], outs)
            return
        acc = refs[n_plain + nr]
        k = pl.program_id(2)

        @pl.when(k == 0)
        def _():
            acc[...] = p

        @pl.when((k > 0) & (k < nk - 1))
        def _():
            acc[...] += p

        @pl.when(k == nk - 1)
        def _():
            epilogue(acc[...] + p, ins[2:], outs)

    rbufs = [b for _, b in riders]
    in_place = {n_in + n: n_out + n for n, (kind, _) in enumerate(riders) if kind in ("ici", "d2d")}
    scratch = ([] if nk == 1 else [pltpu.VMEM(acc_shape, F32)]) \
        + [pltpu.SemaphoreType.DMA((RIDER_SEMS * nr,))] * (2 if nr else 0)
    return pl.pallas_call(
        body, name=name, grid=grid, in_specs=list(in_specs) + _any_specs(nr), out_specs=list(out_specs) + _any_specs(nr),
        out_shape=list(out_shapes) + [_rider_out(kind, b) for kind, b in riders],
        scratch_shapes=scratch,
        input_output_aliases={**(aliases or {}), **in_place},
        compiler_params=_params(("arbitrary",) * 3 if nr or sequential else ("parallel", "parallel", "arbitrary")),
    )(*operands, *rbufs)


def _wspec(w, layer, whole=False):
    r, c = w.shape[-2:]
    lead = N_CHIPS if whole else None
    if w.ndim == 4:
        return pl.BlockSpec((lead, None, r, c), lambda j, i, k: (0 if whole else j, layer, 0, 0))
    return pl.BlockSpec((lead, r, c), lambda j, i, k: (0 if whole else j, 0, 0))


def _wide_tile(s):
    return min(TM_WIDE, s)


def _ep_store(p, ins, outs):
    for o in outs:
        o[...] = p.astype(o.dtype)


def _rows_first(spec):
    return pl.BlockSpec(spec.block_shape, lambda i, j, k: spec.index_map(j, i, k))


def _mm_fwd(name, a, w, layer, col_sharded, epilogue=_ep_store, extras=(), extra_specs=(), outs=None, riders=(),
            a_fn=_identity):
    s = a.shape[0]
    r, c = w.shape[-2:]
    if col_sharded:
        tm = _wide_tile(s)
        grid = (s // tm, N_CHIPS, 1)
        a_spec = pl.BlockSpec((tm, r), lambda j, i, k: (i, 0))
        n_out = N_CHIPS * c
    else:
        tm = TM
        grid = (1, s // tm, 1)
        a_spec = pl.BlockSpec((tm, N_CHIPS * r), lambda j, i, k: (i, 0))
        n_out = c
    if outs is None:
        outs = [(n_out, F32)]
    out_shapes = [jax.ShapeDtypeStruct((s, n), dt) for n, dt in outs]
    out_specs = [pl.BlockSpec((tm, c if n == n_out else n), lambda j, i, k: (i, j)) for n, _ in outs]
    in_specs = [a_spec, _wspec(w, layer, True)] + list(extra_specs)
    if col_sharded:
        in_specs, out_specs = [_rows_first(sp) for sp in in_specs], [_rows_first(sp) for sp in out_specs]
    return _matmul(name, (a, w) + tuple(extras), in_specs, out_shapes, out_specs, grid, NN, epilogue, (tm, c),
                   chunks=None if col_sharded else N_CHIPS, riders=riders, pick=col_sharded, a_fn=a_fn)


def _mm_bwd_act(name, dy, w, layer, col_sharded, epilogue=_ep_store, extras=(), extra_specs=(), out_dtype=F32,
                riders=(), through_norm=False):
    s = dy.shape[0]
    r, c = w.shape[-2:]
    if col_sharded:
        tm = TM
        grid = (1, s // tm, 1)
        a_spec = pl.BlockSpec((tm, N_CHIPS * c), lambda j, i, k: (i, 0))
        n_out = r
    else:
        tm = _wide_tile(s)
        grid = (s // tm, N_CHIPS, 1)
        a_spec = pl.BlockSpec((tm, c), lambda j, i, k: (i, 0))
        n_out = N_CHIPS * r
    in_specs = [a_spec, _wspec(w, layer, True)] + list(extra_specs)
    o_spec = pl.BlockSpec((tm, r), lambda j, i, k: (i, j))
    if not col_sharded:
        in_specs, o_spec = [_rows_first(sp) for sp in in_specs], _rows_first(o_spec)
    out_shapes, out_specs = [jax.ShapeDtypeStruct((s, n_out), out_dtype)], [o_spec]
    if through_norm:
        vec = pl.BlockSpec((1, n_out), lambda j, i, k: (0, 0))
        out_shapes = [jax.ShapeDtypeStruct((s, n_out), F32), jax.ShapeDtypeStruct((s, n_out), BF16),
                      jax.ShapeDtypeStruct((1, n_out), F32), jax.ShapeDtypeStruct((1, n_out), F32)]
        out_specs = [o_spec, o_spec, vec, vec]
    return _matmul(name, (dy, w) + tuple(extras), in_specs, out_shapes, out_specs, grid, NT, epilogue, (tm, r),
                   chunks=N_CHIPS if col_sharded else None, riders=riders, pick=not col_sharded,
                   sequential=through_norm)


def _mm_bwd_w(name, a, dy, w, col_sharded, riders=(), a_fn=_identity):
    s = a.shape[0]
    r, c = w.shape[-2:]
    ts = min(TS, s)
    grid = (N_CHIPS, 1, s // ts)
    if col_sharded:
        a_spec = pl.BlockSpec((ts, r), lambda j, i, k: (k, 0))
        b_spec = pl.BlockSpec((ts, c), lambda j, i, k: (k, j))
    else:
        a_spec = pl.BlockSpec((ts, r), lambda j, i, k: (k, j))
        b_spec = pl.BlockSpec((ts, c), lambda j, i, k: (k, 0))

    def epilogue(p, ins, outs):
        outs[0][...] = p

    return _matmul(name, (a, dy), [a_spec, b_spec], [jax.ShapeDtypeStruct(w.shape, F32)], [_wspec(w, None)], grid, TN,
                   epilogue, (r, c), riders=riders, a_fn=a_fn)


def _row_spec(n):
    return pl.BlockSpec((TM, n), lambda j, i, k: (i, 0))


def _vec_spec(layer, n):
    return pl.BlockSpec((None, 1, n), lambda j, i, k: (layer, 0, 0))


def _ep_resid_ln(p, ins, outs):
    x_ref, g_ref, b_ref = ins
    xf_ref, xb_ref, xhat_ref, rstd_ref = outs
    r = ALPHA * x_ref[...] + p
    mu = jnp.mean(r, axis=-1, keepdims=True)
    d = r - mu
    var = jnp.mean(d * d, axis=-1, keepdims=True)
    rstd = lax.rsqrt(var + LN_EPS)
    xhat = d * rstd
    y = xhat * g_ref[...] + b_ref[...]
    xf_ref[...] = y
    xb_ref[...] = y.astype(BF16)
    xhat_ref[...] = xhat
    rstd_ref[...] = rstd


def _mm_resid_ln(name, a, w, x, g3, b3, ln_layer, riders=(), a_fn=_identity):
    d = x.shape[1]
    return _mm_fwd(name, a, w, None, False, _ep_resid_ln, (x, g3, b3),
                   (_row_spec(d), _vec_spec(ln_layer, d), _vec_spec(ln_layer, d)),
                   outs=[(d, F32), (d, BF16), (d, F32), (1, F32)], riders=riders, a_fn=a_fn)


def _ep_relu(p, ins, outs):
    outs[0][...] = jnp.maximum(p, 0.0).astype(BF16)


def _ep_scale_q(p, ins, outs):
    outs[0][...] = (p * (HEAD_DIM ** -0.5)).astype(BF16)


def _ep_bf16(p, ins, outs):
    outs[0][...] = p.astype(BF16)


def _ep_relu2_bwd(p, ins, outs):
    outs[0][...] = (p * (2.0 * ins[0][...].astype(F32))).astype(BF16)


def _ep_resid(p, ins, outs):
    outs[0][...] = ALPHA * ins[0][...] + p


def _ep_resid_ln_bwd(p, ins, outs):
    dr_ref, xh_ref, rs_ref, g_ref = ins
    _ln_bwd_rows(ALPHA * dr_ref[...] + p, xh_ref, rs_ref, g_ref, pl.program_id(1) == 0, *outs)


def _ep_add(p, ins, outs):
    outs[0][...] = ins[0][...] + p


def _gelu_grad(x):
    c0 = math.sqrt(2.0 / math.pi)
    t = jnp.tanh(c0 * (x + 0.044715 * (x * x * x)))
    return 0.5 * (1.0 + t) + (0.5 * x) * (1.0 - t * t) * (c0 * (1.0 + 3.0 * 0.044715 * (x * x)))


def _cast_bf16(w2d):
    r, c = w2d.shape
    tr = min(r, 512)

    def body(w_ref, o_ref):
        o_ref[...] = w_ref[...].astype(BF16)

    return pl.pallas_call(
        body, name="cast_bf16", grid=(r // tr,),
        in_specs=[pl.BlockSpec((tr, c), lambda i: (i, 0))], out_specs=pl.BlockSpec((tr, c), lambda i: (i, 0)),
        out_shape=jax.ShapeDtypeStruct((r, c), BF16), compiler_params=_params(("parallel",)),
    )(w2d)


def _cast_into_slot(w, layer, chip):
    r, c = w.shape[-2:]
    tr = min(r, 512)

    def body(chip_ref, w_ref, o_ref):
        o_ref[...] = w_ref[...].astype(BF16)

    if layer is None:
        w_spec = pl.BlockSpec((tr, c), lambda i, chip_ref: (i, 0))
    else:
        w_spec = pl.BlockSpec((None, tr, c), lambda i, chip_ref: (layer, i, 0))
    grid_spec = pltpu.PrefetchScalarGridSpec(
        num_scalar_prefetch=1, grid=(r // tr,), in_specs=[w_spec],
        out_specs=pl.BlockSpec((None, tr, c), lambda i, chip_ref: (chip_ref[0], i, 0)))
    return pl.pallas_call(
        body, name="cast_into_slot", grid_spec=grid_spec,
        out_shape=jax.ShapeDtypeStruct((N_CHIPS, r, c), BF16), compiler_params=_params(("parallel",)),
    )(chip, w)


def _gmlp_norm_fwd(h, g3, b3, layer):
    s, w2 = h.shape
    w = w2 // 2

    def body(h_ref, g_ref, b_ref, o_ref):
        z = jax.nn.gelu(h_ref[...])
        mu = jnp.mean(z, axis=-1, keepdims=True)
        d = z - mu
        var = jnp.mean(d * d, axis=-1, keepdims=True)
        o_ref[...] = (d * lax.rsqrt(var + LN_EPS) * g_ref[...] + b_ref[...]).astype(BF16)

    vec = pl.BlockSpec((None, 1, w), lambda i: (layer, 0, 0))
    return pl.pallas_call(
        body, name="gmlp_norm_fwd", grid=(s // TM,),
        in_specs=[pl.BlockSpec((TM, w), lambda i: (i, 1)), vec, vec],
        out_specs=pl.BlockSpec((TM, w), lambda i: (i, 0)),
        out_shape=jax.ShapeDtypeStruct((s, w), BF16), compiler_params=_params(("parallel",)),
    )(h, g3, b3)


def _chunk_mask():
    t = lax.broadcasted_iota(jnp.int32, (GMLP_BLOCK, GMLP_BLOCK), 0)
    s = lax.broadcasted_iota(jnp.int32, (GMLP_BLOCK, GMLP_BLOCK), 1)
    return (s // CHUNK) <= (t // CHUNK)


SG_ROWS = 512


def _gate_fwd(h, vn, ws, bst):
    s, w = vn.shape
    gd = w // GMLP_GROUPS

    def body(h_ref, v_ref, ws_ref, bs_ref, o_ref):
        mask = _chunk_mask()
        for g in range(GMLP_GROUPS):
            wm = jnp.where(mask, ws_ref[g], 0.0).astype(BF16)
            bias = bs_ref[:, g:g + 1]
            cols = slice(g * gd, (g + 1) * gd)
            for n in range(SG_ROWS // GMLP_BLOCK):
                rows = slice(n * GMLP_BLOCK, (n + 1) * GMLP_BLOCK)
                sp = _dot(wm, v_ref[rows, cols], NN) + bias
                o_ref[rows, cols] = (jax.nn.gelu(h_ref[rows, cols]) * sp).astype(BF16)

    return pl.pallas_call(
        body, name="gate_fwd", grid=(s // SG_ROWS,),
        in_specs=[pl.BlockSpec((SG_ROWS, w), lambda i: (i, 0)), pl.BlockSpec((SG_ROWS, w), lambda i: (i, 0)),
                  pl.BlockSpec(ws.shape, lambda i: (0, 0, 0)), pl.BlockSpec(bst.shape, lambda i: (0, 0))],
        out_specs=pl.BlockSpec((SG_ROWS, w), lambda i: (i, 0)),
        out_shape=jax.ShapeDtypeStruct((s, w), BF16), compiler_params=_params(("parallel",)),
    )(h, vn, ws, bst)


def _gate_bwd(dgated, h, vn, ws, bst):
    s, w = vn.shape
    gd = w // GMLP_GROUPS
    nsteps = s // SG_ROWS

    def body(dg_ref, h_ref, v_ref, ws_ref, bs_ref, du_ref, dv_ref, dws_ref, dbs_ref, dsum):
        i = pl.program_id(0)

        @pl.when(i == 0)
        def _():
            dws_ref[...] = jnp.zeros_like(dws_ref)
            dsum[...] = jnp.zeros_like(dsum)

        mask = _chunk_mask()
        for g in range(GMLP_GROUPS):
            wm = jnp.where(mask, ws_ref[g], 0.0).astype(BF16)
            bias = bs_ref[:, g:g + 1]
            cols = slice(g * gd, (g + 1) * gd)
            dw = jnp.zeros((GMLP_BLOCK, GMLP_BLOCK), F32)
            dsg = jnp.zeros((GMLP_BLOCK, gd), F32)
            for n in range(SG_ROWS // GMLP_BLOCK):
                rows = slice(n * GMLP_BLOCK, (n + 1) * GMLP_BLOCK)
                vb = v_ref[rows, cols]
                sp = _dot(wm, vb, NN) + bias
                dg = dg_ref[rows, cols]
                du_ref[rows, cols] = dg * sp
                ds = dg * jax.nn.gelu(h_ref[rows, cols])
                dsb = ds.astype(BF16)
                dw += _dot(dsb, vb, NT)
                dsg += ds
                dv_ref[rows, cols] = _dot(wm, dsb, TN)
            dws_ref[g] += dw
            dsum[:, cols] += dsg

        @pl.when(i == nsteps - 1)
        def _():
            for g in range(GMLP_GROUPS):
                dws_ref[g] = jnp.where(mask, dws_ref[g], 0.0)
                tot = jnp.sum(dsum[:, g * gd:(g + 1) * gd], axis=-1, keepdims=True)
                dbs_ref[g] = jnp.broadcast_to(tot, (GMLP_BLOCK, LANES))

    tile = pl.BlockSpec((SG_ROWS, w), lambda i: (i, 0))
    return pl.pallas_call(
        body, name="gate_bwd", grid=(nsteps,),
        in_specs=[tile, tile, tile, pl.BlockSpec(ws.shape, lambda i: (0, 0, 0)), pl.BlockSpec(bst.shape, lambda i: (0, 0))],
        out_specs=[tile, tile, pl.BlockSpec(ws.shape, lambda i: (0, 0, 0)),
                   pl.BlockSpec((GMLP_GROUPS, GMLP_BLOCK, LANES), lambda i: (0, 0, 0))],
        out_shape=[jax.ShapeDtypeStruct((s, w), F32), jax.ShapeDtypeStruct((s, w), F32),
                   jax.ShapeDtypeStruct(ws.shape, F32), jax.ShapeDtypeStruct((GMLP_GROUPS, GMLP_BLOCK, LANES), F32)],
        scratch_shapes=[pltpu.VMEM((GMLP_BLOCK, w), F32)],
        compiler_params=_params(("arbitrary",)),
    )(dgated, h, vn, ws, bst)


GB_ROWS = 256


def _gmlp_in_bwd(h, du, dvn, g3, layer):
    s, w2 = h.shape
    w = w2 // 2
    nsteps = s // GB_ROWS

    def body(h_ref, du_ref, dv_ref, g_ref, dh_ref, dg_ref, db_ref):
        i = pl.program_id(0)

        @pl.when(i == 0)
        def _():
            dg_ref[...] = jnp.zeros_like(dg_ref)
            db_ref[...] = jnp.zeros_like(db_ref)

        hu = h_ref[:, :w]
        hv = h_ref[:, w:]
        dh_ref[:, :w] = (du_ref[...] * _gelu_grad(hu)).astype(BF16)
        z = jax.nn.gelu(hv)
        mu = jnp.mean(z, axis=-1, keepdims=True)
        d = z - mu
        var = jnp.mean(d * d, axis=-1, keepdims=True)
        rstd = lax.rsqrt(var + LN_EPS)
        xhat = d * rstd
        dy = dv_ref[...]
        db_ref[...] += jnp.sum(dy, axis=0, keepdims=True)
        dg_ref[...] += jnp.sum(dy * xhat, axis=0, keepdims=True)
        dxh = dy * g_ref[...]
        m1 = jnp.mean(dxh, axis=-1, keepdims=True)
        m2 = jnp.mean(dxh * xhat, axis=-1, keepdims=True)
        dz = rstd * (dxh - m1 - xhat * m2)
        dh_ref[:, w:] = (dz * _gelu_grad(hv)).astype(BF16)

    half = pl.BlockSpec((GB_ROWS, w), lambda i: (i, 0))
    vec = pl.BlockSpec((1, w), lambda i: (0, 0))
    return pl.pallas_call(
        body, name="gmlp_in_bwd", grid=(nsteps,),
        in_specs=[pl.BlockSpec((GB_ROWS, w2), lambda i: (i, 0)), half, half,
                  pl.BlockSpec((None, 1, w), lambda i: (layer, 0, 0))],
        out_specs=[pl.BlockSpec((GB_ROWS, w2), lambda i: (i, 0)), vec, vec],
        out_shape=[jax.ShapeDtypeStruct((s, w2), BF16), jax.ShapeDtypeStruct((1, w), F32), jax.ShapeDtypeStruct((1, w), F32)],
        compiler_params=_params(("arbitrary",)),
    )(h, du, dvn, g3)


def _ln_bwd_rows(dy, xh_ref, rs_ref, g_ref, first, dr_ref, drb_ref, dg_ref, db_ref):
    @pl.when(first)
    def _():
        dg_ref[...] = jnp.zeros_like(dg_ref)
        db_ref[...] = jnp.zeros_like(db_ref)

    xhat = xh_ref[...]
    db_ref[...] += jnp.sum(dy, axis=0, keepdims=True)
    dg_ref[...] += jnp.sum(dy * xhat, axis=0, keepdims=True)
    dxh = dy * g_ref[...]
    m1 = jnp.mean(dxh, axis=-1, keepdims=True)
    m2 = jnp.mean(dxh * xhat, axis=-1, keepdims=True)
    dr = rs_ref[...] * (dxh - m1 - xhat * m2)
    dr_ref[...] = dr
    drb_ref[...] = dr.astype(BF16)


def _ln_bwd(dy, xhat, rstd, g3, layer, target=None):
    s, d = dy.shape
    nsteps = s // TM
    head = target is not None

    def body(*refs):
        dy_ref, xh_ref, rs_ref, g_ref = refs[:4]
        dr_ref, drb_ref, dg_ref, db_ref = refs[4 + head:8 + head]
        first = pl.program_id(0) == 0
        dyv = dy_ref[...]
        if head:
            l_ref = refs[8 + head]

            @pl.when(first)
            def _():
                l_ref[...] = jnp.zeros_like(l_ref)

            e = dyv - refs[4][...]
            l_ref[...] += jnp.sum(jnp.sum(e * e, axis=1, keepdims=True), axis=0, keepdims=True)
            dyv = e * (1.0 / d)
        _ln_bwd_rows(dyv, xh_ref, rs_ref, g_ref, first, dr_ref, drb_ref, dg_ref, db_ref)

    tile = pl.BlockSpec((TM, d), lambda i: (i, 0))
    vec = pl.BlockSpec((1, d), lambda i: (0, 0))
    one = pl.BlockSpec((1, 1), lambda i: (0, 0))
    return pl.pallas_call(
        body, name="ln_bwd", grid=(nsteps,),
        in_specs=[tile, tile, pl.BlockSpec((TM, 1), lambda i: (i, 0)), pl.BlockSpec((None, 1, d), lambda i: (layer, 0, 0))]
        + ([tile] if head else []),
        out_specs=[tile, tile, vec, vec] + ([one] if head else []),
        out_shape=[jax.ShapeDtypeStruct((s, d), F32), jax.ShapeDtypeStruct((s, d), BF16),
                   jax.ShapeDtypeStruct((1, d), F32), jax.ShapeDtypeStruct((1, d), F32)]
        + ([jax.ShapeDtypeStruct((1, 1), F32)] if head else []),
        compiler_params=_params(("arbitrary",)),
    )(dy, xhat, rstd, g3, *([target] if head else []))


LOG2E = 1.4426950408889634
DEAD_LOG2 = -160.0
FIRST_LANE = 1


def _sb_terms(z, causal):
    z2 = z * LOG2E
    e = jnp.exp2(-jnp.abs(z2))
    l1p = jnp.log2(1.0 + e)
    lb = jnp.minimum(z2, 0.0) - l1p
    lr = lb - z2
    if causal is not None:
        lr = jnp.where(causal, lr, 0.0)
    return lb, lr, e


def _split_hi_lo(x):
    hi = x.astype(BF16)
    lo = (x - hi.astype(F32)).astype(BF16)
    return jnp.concatenate([hi, lo], axis=1)


def _tri2(prefix):
    r = lax.broadcasted_iota(jnp.int32, (2 * ATT_T, ATT_T), 0) % ATT_T
    c = lax.broadcasted_iota(jnp.int32, (2 * ATT_T, ATT_T), 1)
    return jnp.where((r <= c) if prefix else (r >= c), 1.0, 0.0).astype(BF16)


def _att_masks():
    r = lax.broadcasted_iota(jnp.int32, (ATT_T, ATT_T), 0)
    c = lax.broadcasted_iota(jnp.int32, (ATT_T, ATT_T), 1)
    return c < r, lax.broadcasted_iota(jnp.int32, (1, LANES), 1) < HEAD_DIM


def _attn_fwd(q, k, v):
    s, d = q.shape
    nq = s // ATT_T

    def body(q_ref, k_ref, v_ref, tri_ref, ob_ref, lsum_ref, acc_a, acc_b, rem_a, rem_b):
        i = pl.program_id(1)
        tri = tri_ref[...]
        causal, head_a = _att_masks()
        q2 = q_ref[...]
        zero = jnp.zeros_like(q2)
        qa = jnp.where(head_a, q2, zero)
        qb = jnp.where(head_a, zero, q2)
        acc_a[...] = jnp.zeros_like(acc_a)
        acc_b[...] = jnp.zeros_like(acc_b)
        rem_a[...] = jnp.zeros_like(rem_a)
        rem_b[...] = jnp.zeros_like(rem_b)

        def block(kb, mask):
            rows = pl.ds(pl.multiple_of(kb * ATT_T, ATT_T), ATT_T)
            k2 = k_ref[rows, :]
            v2 = v_ref[rows, :]
            heads = ((qa, acc_a, rem_a), (qb, acc_b, rem_b))
            zs = [_dot(qm, k2, NT) for qm, _, _ in heads]
            terms = [_sb_terms(z, mask) for z in zs]
            sums = [_dot(_split_hi_lo(lr), tri, NN) for _, lr, _ in terms]
            for (_, acc, rem), (lb, lr, _), sincl in zip(heads, terms, sums):
                a = jnp.exp2(lb + (sincl - lr) + rem[...])
                if mask is not None:
                    a = jnp.where(mask, a, 0.0)
                rem[...] += sincl[:, 0:1]
                acc[...] += _dot(a.astype(BF16), v2, NN)

        block(i, causal)

        @pl.when(i > 0)
        def _():
            block(i - 1, None)

        def live():
            return jnp.maximum(jnp.max(rem_a[...]), jnp.max(rem_b[...])) > DEAD_LOG2

        def go_on(carry):
            t, alive = carry
            return (t < i) & alive

        def step(carry):
            t, _ = carry
            block(i - 1 - t, None)
            return t + 1, live()

        done, _ = lax.while_loop(go_on, step, (jnp.minimum(i, 1), live()))
        first = (i - done).astype(F32)
        ob_ref[...] = jnp.where(head_a, acc_a[...], acc_b[...]).astype(BF16)
        lane = lax.broadcasted_iota(jnp.int32, (1, LANES), 1)
        lsum_ref[...] = jnp.where(lane == FIRST_LANE, first, jnp.where(head_a, rem_a[...], rem_b[...]))

    qspec = pl.BlockSpec((ATT_T, LANES), lambda p, i: (i, p))
    kspec = pl.BlockSpec((s, LANES), lambda p, i: (0, p))
    return pl.pallas_call(
        body, name="attn_fwd", grid=(d // LANES, nq),
        in_specs=[qspec, kspec, kspec, pl.BlockSpec((2 * ATT_T, ATT_T), lambda p, i: (0, 0))],
        out_specs=[qspec, qspec],
        out_shape=[jax.ShapeDtypeStruct((s, d), BF16), jax.ShapeDtypeStruct((s, d), F32)],
        scratch_shapes=[pltpu.VMEM((ATT_T, LANES), F32), pltpu.VMEM((ATT_T, LANES), F32),
                        pltpu.VMEM((ATT_T, 1), F32), pltpu.VMEM((ATT_T, 1), F32)],
        compiler_params=_params(("parallel", "arbitrary")),
    )(q, k, v, _tri2(prefix=False))


def _attn_bwd(q, k, v, do, lsum, dk_prev=None, dv_prev=None):
    s, d = q.shape
    nq = s // ATT_T
    has_prev = dk_prev is not None

    def body(*refs):
        q_ref, k_ref, v_ref, do_ref, ls_ref, tri_ref = refs[:6]
        n_in = 8 if has_prev else 6
        dq_ref, dk_ref, dv_ref, acc_a, acc_b, pre_a, pre_b, gp_a, gp_b, dkt, dvt = refs[n_in:]
        i = pl.program_id(1)

        @pl.when(i == 0)
        def _():
            dkt[...] = jnp.zeros_like(dkt)
            dvt[...] = jnp.zeros_like(dvt)

        tri = tri_ref[...]
        causal, head_a = _att_masks()
        q2 = q_ref[...]
        zero = jnp.zeros_like(q2)
        qa = jnp.where(head_a, q2, zero)
        qb = jnp.where(head_a, zero, q2)
        do2 = do_ref[...]
        doa = jnp.where(head_a, do2, 0.0).astype(BF16)
        dob = jnp.where(head_a, 0.0, do2).astype(BF16)
        row_a = lax.broadcasted_iota(jnp.int32, (LANES, 1), 0) < HEAD_DIM
        qt = q2.astype(F32).T
        dot_ = do2.T
        qta, qtb = jnp.where(row_a, qt, 0.0).astype(BF16), jnp.where(row_a, 0.0, qt).astype(BF16)
        dota, dotb = jnp.where(row_a, dot_, 0.0).astype(BF16), jnp.where(row_a, 0.0, dot_).astype(BF16)
        ls2 = ls_ref[...]
        tot_a = ls2[:, 0:1]
        tot_b = ls2[:, HEAD_DIM:HEAD_DIM + 1]
        for r in (acc_a, acc_b, pre_a, pre_b, gp_a, gp_b):
            r[...] = jnp.zeros_like(r)

        def block(kb, mask):
            rows = pl.ds(pl.multiple_of(kb * ATT_T, ATT_T), ATT_T)
            k2 = k_ref[rows, :]
            v2 = v_ref[rows, :]
            dk_new = jnp.zeros((LANES, ATT_T), F32)
            dv_new = jnp.zeros((LANES, ATT_T), F32)
            heads = ((qa, doa, tot_a, acc_a, pre_a, gp_a, qta, dota), (qb, dob, tot_b, acc_b, pre_b, gp_b, qtb, dotb))
            zs = [_dot(h[0], k2, NT) for h in heads]
            das = [_dot(h[1], v2, NT) for h in heads]
            terms = [_sb_terms(z, mask) for z in zs]
            psums = [_dot(_split_hi_lo(lr), tri, NN) for _, lr, _ in terms]
            gs, abs_ = [], []
            for h, (lb, _, _), pincl, da in zip(heads, terms, psums, das):
                tot, pre = h[2], h[4]
                a = jnp.exp2(lb + (tot - (pre[...] + pincl)))
                if mask is not None:
                    a = jnp.where(mask, a, 0.0)
                pre[...] += pincl[:, ATT_T - 1:ATT_T]
                gs.append(a * da)
                abs_.append(a.astype(BF16))
            gsums = [_dot(g.astype(BF16), tri[:ATT_T], NN) for g in gs]
            dzs = []
            for h, z, (_, _, e), g, gincl in zip(heads, zs, terms, gs, gsums):
                gpre = h[5]
                gbefore = gpre[...] + (gincl - g)
                gpre[...] += gincl[:, ATT_T - 1:ATT_T]
                inv = 1.0 / (1.0 + e)
                beta = jnp.where(z >= 0.0, inv, e * inv)
                dz = g - beta * (g + gbefore)
                if mask is not None:
                    dz = jnp.where(mask, dz, 0.0)
                dzs.append(dz.astype(BF16))
            for h, ab, dzb in zip(heads, abs_, dzs):
                dv_new += _dot(h[7], ab, NN)
                dk_new += _dot(h[6], dzb, NN)
                h[3][...] += _dot(dzb, k2, NN)
            cols = pl.ds(pl.multiple_of(kb * ATT_T, ATT_T), ATT_T)
            dkt[:, cols] += dk_new
            dvt[:, cols] += dv_new

        def step(kb, carry):
            block(kb, None)
            return carry

        first = jnp.clip(jnp.max(ls2[:, FIRST_LANE:FIRST_LANE + 1]).astype(jnp.int32), 0, i)
        lax.fori_loop(first, i, step, 0)
        block(i, causal)
        dq_ref[...] = (jnp.where(head_a, acc_a[...], acc_b[...]) * (HEAD_DIM ** -0.5)).astype(BF16)

        @pl.when(i == nq - 1)
        def _():
            for n in range(nq):
                rows = slice(n * ATT_T, (n + 1) * ATT_T)
                dkn, dvn = dkt[:, rows].T, dvt[:, rows].T
                if has_prev:
                    dkn, dvn = dkn + refs[6][rows, :], dvn + refs[7][rows, :]
                dk_ref[rows, :] = dkn
                dv_ref[rows, :] = dvn

    qspec = pl.BlockSpec((ATT_T, LANES), lambda p, i: (i, p))
    kspec = pl.BlockSpec((s, LANES), lambda p, i: (0, p))
    ins = [q, k, v, do, lsum, _tri2(prefix=True)] + ([dk_prev, dv_prev] if has_prev else [])
    return pl.pallas_call(
        body, name="attn_bwd", grid=(d // LANES, nq),
        in_specs=[qspec, kspec, kspec, qspec, qspec, pl.BlockSpec((2 * ATT_T, ATT_T), lambda p, i: (0, 0))]
        + ([kspec, kspec] if has_prev else []),
        out_specs=[qspec, kspec, kspec],
        out_shape=[jax.ShapeDtypeStruct((s, d), BF16), jax.ShapeDtypeStruct((s, d), F32), jax.ShapeDtypeStruct((s, d), F32)],
        scratch_shapes=[pltpu.VMEM((ATT_T, LANES), F32), pltpu.VMEM((ATT_T, LANES), F32)]
        + [pltpu.VMEM((ATT_T, 1), F32)] * 4 + [pltpu.VMEM((LANES, s), F32)] * 2,
        compiler_params=_params(("parallel", "arbitrary")),
    )(*ins)


def _place():
    x, y, c = lax.axis_index("x"), lax.axis_index("y"), lax.axis_index("c")
    chips = [(1 - x, y), (x, 1 - y), (1 - x, 1 - y)]
    return x, y, c, chips


def _any_specs(n):
    return [pl.BlockSpec(memory_space=pl.ANY)] * n


def _gather_weights(bufs):
    n = len(bufs)

    def body(*refs):
        outs = refs[n:2 * n]
        send_sems, recv_sems = refs[2 * n:]
        x, y, c, chips = _place()
        me = 2 * x + y
        sibling = (x, y, 1 - c)

        def half(a, blk, hc):
            h = outs[a].shape[1] // 2
            return outs[a].at[blk, pl.ds(hc * h, h)]

        def copy(a, k, part, to):
            return pltpu.make_async_remote_copy(src_ref=part, dst_ref=part, send_sem=send_sems.at[a, k],
                                                recv_sem=recv_sems.at[a, k], device_id=to, device_id_type=MESH)

        sent = []
        for a in range(n):
            for k, chip in enumerate(chips):
                sent.append(copy(a, k, half(a, me, c), (*chip, c)))
                sent[-1].start()
        for a in range(n):
            for k, chip in enumerate(chips):
                blk = 2 * chip[0] + chip[1]
                copy(a, k, half(a, blk, c), sibling).wait_recv()
                sent.append(copy(a, 3 + k, half(a, blk, c), sibling))
                sent[-1].start()
        for a in range(n):
            for k, chip in enumerate(chips):
                blk = 2 * chip[0] + chip[1]
                copy(a, 3 + k, half(a, blk, 1 - c), sibling).wait_recv()
        for cp in sent:
            cp.wait_send()

    return pl.pallas_call(
        body, name="gather_weights", in_specs=_any_specs(n), out_specs=_any_specs(n),
        out_shape=[jax.ShapeDtypeStruct(w.shape, w.dtype) for w in bufs],
        input_output_aliases={a: a for a in range(n)},
        scratch_shapes=[pltpu.SemaphoreType.DMA((n, 6)), pltpu.SemaphoreType.DMA((n, 6))],
        compiler_params=pltpu.CompilerParams(has_side_effects=True),
    )(*bufs)


def _pair_exchange(grads):
    n = len(grads)

    def body(*refs):
        ins, outs = refs[:n], refs[n:2 * n]
        send_sems, recv_sems = refs[2 * n:]
        x, y, c, _ = _place()
        cps = []
        for a in range(n):
            h = ins[a].shape[1] // 2
            cps.append(pltpu.make_async_remote_copy(
                src_ref=ins[a].at[:, pl.ds((1 - c) * h, h)], dst_ref=outs[a], send_sem=send_sems.at[a],
                recv_sem=recv_sems.at[a], device_id=(x, y, 1 - c), device_id_type=MESH))
            cps[-1].start()
        for cp in cps:
            cp.wait()

    return pl.pallas_call(
        body, name="pair_exchange", in_specs=_any_specs(n), out_specs=_any_specs(n),
        out_shape=[jax.ShapeDtypeStruct((g.shape[0], g.shape[1] // 2, g.shape[2]), g.dtype) for g in grads],
        scratch_shapes=[pltpu.SemaphoreType.DMA((n,)), pltpu.SemaphoreType.DMA((n,))],
        compiler_params=pltpu.CompilerParams(has_side_effects=True),
    )(*grads)


def _chip_exchange(parts):
    n = len(parts)

    def body(*refs):
        ins, outs = refs[:n], refs[n:2 * n]
        send_sems, recv_sems = refs[2 * n:]
        x, y, c, chips = _place()
        me = 2 * x + y
        cps = []
        for a in range(n):
            for k, chip in enumerate(chips):
                blk = 2 * chip[0] + chip[1]
                cps.append(pltpu.make_async_remote_copy(
                    src_ref=ins[a].at[blk], dst_ref=outs[a].at[me], send_sem=send_sems.at[a, k],
                    recv_sem=recv_sems.at[a, k], device_id=(*chip, c), device_id_type=MESH))
                cps[-1].start()
        for a in range(n):
            for k, chip in enumerate(chips):
                blk = 2 * chip[0] + chip[1]
                pltpu.make_async_remote_copy(
                    src_ref=ins[a].at[blk], dst_ref=outs[a].at[blk], send_sem=send_sems.at[a, k],
                    recv_sem=recv_sems.at[a, k], device_id=(*chip, c), device_id_type=MESH).wait_recv()
        for cp in cps:
            cp.wait_send()

    return pl.pallas_call(
        body, name="chip_exchange", in_specs=_any_specs(n), out_specs=_any_specs(n),
        out_shape=[jax.ShapeDtypeStruct(p.shape, p.dtype) for p in parts],
        scratch_shapes=[pltpu.SemaphoreType.DMA((n, 3)), pltpu.SemaphoreType.DMA((n, 3))],
        compiler_params=pltpu.CompilerParams(has_side_effects=True),
    )(*parts)


def _half_swap(halves):
    n = len(halves)

    def body(*refs):
        outs = refs[n:2 * n]
        send_sems, recv_sems = refs[2 * n:]
        x, y, c, _ = _place()
        cps = []
        for a in range(n):
            h = outs[a].shape[1] // 2
            mine = outs[a].at[:, pl.ds(c * h, h)]
            cps.append(pltpu.make_async_remote_copy(
                src_ref=mine, dst_ref=mine, send_sem=send_sems.at[a], recv_sem=recv_sems.at[a],
                device_id=(x, y, 1 - c), device_id_type=MESH))
            cps[-1].start()
        for cp in cps:
            cp.wait()

    return pl.pallas_call(
        body, name="half_swap", in_specs=_any_specs(n), out_specs=_any_specs(n),
        out_shape=[jax.ShapeDtypeStruct(p.shape, p.dtype) for p in halves],
        input_output_aliases={a: a for a in range(n)},
        scratch_shapes=[pltpu.SemaphoreType.DMA((n,)), pltpu.SemaphoreType.DMA((n,))],
        compiler_params=pltpu.CompilerParams(has_side_effects=True),
    )(*halves)


N_DEV = 8


def _all_reduce_small(v):
    nrow, ncol = v.shape

    def body(v_ref, o_ref, land, red, send_sems, recv_sems, send2, recv2, loc_sem):
        x, y, c, _ = _place()
        me = 4 * x + 2 * y + c
        peers = []
        for k in range(1, N_DEV):
            peers.append((x ^ ((k >> 2) & 1), y ^ ((k >> 1) & 1), c ^ (k & 1)))
        own = pltpu.make_async_copy(v_ref.at[pl.ds(me, 1)], land.at[pl.ds(me, 1)], loc_sem)
        own.start()
        cps = []
        for k, peer in enumerate(peers):
            dev = 4 * peer[0] + 2 * peer[1] + peer[2]
            cps.append(pltpu.make_async_remote_copy(
                src_ref=v_ref.at[pl.ds(dev, 1)], dst_ref=land.at[pl.ds(me, 1)], send_sem=send_sems.at[k],
                recv_sem=recv_sems.at[k], device_id=peer, device_id_type=MESH))
            cps[-1].start()
        for k, peer in enumerate(peers):
            dev = 4 * peer[0] + 2 * peer[1] + peer[2]
            pltpu.make_async_remote_copy(
                src_ref=v_ref.at[pl.ds(dev, 1)], dst_ref=land.at[pl.ds(dev, 1)], send_sem=send_sems.at[k],
                recv_sem=recv_sems.at[k], device_id=peer, device_id_type=MESH).wait_recv()
        for cp in cps:
            cp.wait_send()
        own.wait()
        terms = land[...]
        total = terms[0:1, :]
        for d in range(1, N_DEV):
            total = total + terms[d:d + 1, :]
        red[...] = total
        own = pltpu.make_async_copy(red, o_ref.at[pl.ds(me, 1)], loc_sem)
        own.start()
        cps = []
        for k, peer in enumerate(peers):
            cps.append(pltpu.make_async_remote_copy(
                src_ref=red, dst_ref=o_ref.at[pl.ds(me, 1)], send_sem=send2.at[k],
                recv_sem=recv2.at[k], device_id=peer, device_id_type=MESH))
            cps[-1].start()
        for k, peer in enumerate(peers):
            dev = 4 * peer[0] + 2 * peer[1] + peer[2]
            pltpu.make_async_remote_copy(
                src_ref=red, dst_ref=o_ref.at[pl.ds(dev, 1)], send_sem=send2.at[k],
                recv_sem=recv2.at[k], device_id=peer, device_id_type=MESH).wait_recv()
        for cp in cps:
            cp.wait_send()
        own.wait()

    vm = pl.BlockSpec(memory_space=pltpu.VMEM)
    return pl.pallas_call(
        body, name="all_reduce_small", in_specs=[vm], out_specs=vm,
        out_shape=jax.ShapeDtypeStruct((nrow, ncol), F32),
        scratch_shapes=[pltpu.VMEM((nrow, ncol), F32), pltpu.VMEM((1, ncol), F32)]
        + [pltpu.SemaphoreType.DMA((N_DEV - 1,))] * 4 + [pltpu.SemaphoreType.DMA],
        compiler_params=pltpu.CompilerParams(has_side_effects=True, vmem_limit_bytes=VMEM_LIMIT),
    )(v)


def _row_tile(rows):
    return min(rows, 512)


def _pair_sum(g, got, place):
    nb, r, c = g.shape
    h = r // 2
    tr = _row_tile(h)
    nt = h // tr

    def body(place_ref, g_ref, got_ref, p_ref, pb_ref):
        p = g_ref[...] + got_ref[...]
        pb_ref[...] = p.astype(BF16)

        @pl.when(pl.program_id(1) == place_ref[0])
        def _():
            p_ref[...] = p

    spec = pl.BlockSpec((None, tr, c), lambda t, j, place_ref: (j, t, 0))
    grid_spec = pltpu.PrefetchScalarGridSpec(
        num_scalar_prefetch=1, grid=(nt, nb),
        in_specs=[pl.BlockSpec((None, tr, c), lambda t, j, place_ref: (j, place_ref[1] * nt + t, 0)), spec],
        out_specs=[pl.BlockSpec((tr, c), lambda t, j, place_ref: (t, 0)), spec])
    return pl.pallas_call(
        body, name="pair_sum", grid_spec=grid_spec,
        out_shape=[jax.ShapeDtypeStruct((h, c), F32), jax.ShapeDtypeStruct((nb, h, c), BF16)],
        compiler_params=_params(("parallel", "arbitrary")),
    )(place, g, got)


def _chip_sum(p, got, place, out, layer):
    h, c = p.shape
    tr = _row_tile(h)
    nt = h // tr

    def body(place_ref, p_ref, g1_ref, g2_ref, g3_ref, old_ref, o_ref):
        o_ref[...] = ((p_ref[...] + g1_ref[...].astype(F32)) + g2_ref[...].astype(F32)) + g3_ref[...].astype(F32)

    def blk(off):
        return pl.BlockSpec((None, tr, c), lambda t, place_ref: ((place_ref[0] + off) % N_CHIPS, t, 0))

    grid_spec = pltpu.PrefetchScalarGridSpec(
        num_scalar_prefetch=1, grid=(nt,),
        in_specs=[pl.BlockSpec((tr, c), lambda t, place_ref: (t, 0)), blk(1), blk(2), blk(3),
                  pl.BlockSpec(memory_space=pl.ANY)],
        out_specs=pl.BlockSpec((None, tr, c), lambda t, place_ref: (layer, place_ref[1] * nt + t, 0)))
    return pl.pallas_call(
        body, name="chip_sum", grid_spec=grid_spec, out_shape=jax.ShapeDtypeStruct(out.shape, F32),
        input_output_aliases={5: 0}, compiler_params=_params(("parallel",)),
    )(place, p, got, got, got, out)


def _adamw(w, g, m, v, pass_g=False):
    r, c = w.shape
    tr = r if r < 8 else _row_tile(r)
    n_out = 4 if pass_g else 3

    def body(w_ref, g_ref, m_ref, v_ref, d_ref, nm_ref, nv_ref, *g_out):
        gv = g_ref[...]
        if pass_g:
            g_out[0][...] = gv
        nm = ADAM_B1 * m_ref[...] + (1.0 - ADAM_B1) * gv
        nv = ADAM_B2 * v_ref[...] + (1.0 - ADAM_B2) * (gv * gv)
        m_hat = nm / (1.0 - ADAM_B1 ** ADAM_STEP)
        v_hat = nv / (1.0 - ADAM_B2 ** ADAM_STEP)
        d_ref[...] = -ADAM_LR * (m_hat / (jnp.sqrt(v_hat) + ADAM_EPS) + ADAM_WD * w_ref[...])
        nm_ref[...] = nm
        nv_ref[...] = nv

    tile = pl.BlockSpec((tr, c), lambda i: (i, 0))
    return pl.pallas_call(
        body, name="adamw", grid=(r // tr,), in_specs=[tile] * 4, out_specs=[tile] * n_out,
        out_shape=[jax.ShapeDtypeStruct((r, c), F32)] * n_out, compiler_params=_params(("parallel",)),
    )(w, g, m, v)


BIG = ("a_w_in", "a_w_out", "sb_w_k", "sb_w_v", "b_w_q", "b_w_o", "ffn_w1", "ffn_w2")
SMALL = ("a_ln_g", "a_ln_b", "a_w_s", "a_b_s", "mix_ln_g", "mix_ln_b", "ffn_ln_g", "ffn_ln_b")
COL_SHARDED = {"a_w_in": True, "a_w_out": False, "sb_w_k": False, "sb_w_v": False, "b_w_q": False, "b_w_o": False,
               "ffn_w1": True, "ffn_w2": False}


def kernel(x, a_w_in, a_ln_g, a_ln_b, a_w_s, a_b_s, a_w_out, sb_w_k, sb_w_v, b_w_q, b_w_o, mix_ln_g, mix_ln_b, ffn_ln_g, ffn_ln_b, ffn_w1, ffn_w2, loss_target, m_a_w_in, m_a_ln_g, m_a_ln_b, m_a_w_s, m_a_b_s, m_a_w_out, m_sb_w_k, m_sb_w_v, m_b_w_q, m_b_w_o, m_mix_ln_g, m_mix_ln_b, m_ffn_ln_g, m_ffn_ln_b, m_ffn_w1, m_ffn_w2, v_a_w_in, v_a_ln_g, v_a_ln_b, v_a_w_s, v_a_b_s, v_a_w_out, v_sb_w_k, v_sb_w_v, v_b_w_q, v_b_w_o, v_mix_ln_g, v_mix_ln_b, v_ffn_ln_g, v_ffn_ln_b, v_ffn_w1, v_ffn_w2):
    names = BIG + SMALL
    given = dict(a_w_in=a_w_in, a_ln_g=a_ln_g, a_ln_b=a_ln_b, a_w_s=a_w_s, a_b_s=a_b_s, a_w_out=a_w_out, sb_w_k=sb_w_k,
                 sb_w_v=sb_w_v, b_w_q=b_w_q, b_w_o=b_w_o, mix_ln_g=mix_ln_g, mix_ln_b=mix_ln_b, ffn_ln_g=ffn_ln_g,
                 ffn_ln_b=ffn_ln_b, ffn_w1=ffn_w1, ffn_w2=ffn_w2)
    mom = dict(a_w_in=m_a_w_in, a_ln_g=m_a_ln_g, a_ln_b=m_a_ln_b, a_w_s=m_a_w_s, a_b_s=m_a_b_s, a_w_out=m_a_w_out,
               sb_w_k=m_sb_w_k, sb_w_v=m_sb_w_v, b_w_q=m_b_w_q, b_w_o=m_b_w_o, mix_ln_g=m_mix_ln_g, mix_ln_b=m_mix_ln_b,
               ffn_ln_g=m_ffn_ln_g, ffn_ln_b=m_ffn_ln_b, ffn_w1=m_ffn_w1, ffn_w2=m_ffn_w2)
    var = dict(a_w_in=v_a_w_in, a_ln_g=v_a_ln_g, a_ln_b=v_a_ln_b, a_w_s=v_a_w_s, a_b_s=v_a_b_s, a_w_out=v_a_w_out,
               sb_w_k=v_sb_w_k, sb_w_v=v_sb_w_v, b_w_q=v_b_w_q, b_w_o=v_b_w_o, mix_ln_g=v_mix_ln_g, mix_ln_b=v_mix_ln_b,
               ffn_ln_g=v_ffn_ln_g, ffn_ln_b=v_ffn_ln_b, ffn_w1=v_ffn_w1, ffn_w2=v_ffn_w2)

    cx, cy, cc = lax.axis_index("x"), lax.axis_index("y"), lax.axis_index("c")
    chip = (2 * cx + cy).astype(jnp.int32)
    chip_arr = chip.reshape(1)

    s, d = x.shape[1], x.shape[2]
    xf = x.reshape(s, d)
    target = loss_target.reshape(s, d)

    def as2d(w):
        return w.reshape(-1, w.shape[-1])

    gw = {}
    for n in BIG:
        for l in ([None] if given[n].ndim == 2 else range(given[n].shape[0])):
            gw[(n, l)] = _cast_into_slot(given[n], l, chip_arr)
    ln_gb = jnp.stack([a_ln_g, a_ln_b])
    ln_slot = lax.dynamic_update_slice(jnp.zeros((N_CHIPS,) + ln_gb.shape, F32), ln_gb[None], (chip, 0, 0, 0))
    layer0 = [("a_w_in", 0), ("a_w_out", 0)]
    gathered = _gather_weights([gw[k] for k in layer0] + [ln_slot])
    gw.update(zip(layer0, gathered[:-1]))
    mixer = {1: [("a_w_in", 1), ("a_w_out", 1)], 2: [("sb_w_k", None), ("sb_w_v", None), ("b_w_q", 0), ("b_w_o", 0)],
             3: [("b_w_q", 1), ("b_w_o", 1)]}

    def riding(d2d=(), ici=()):
        keys = list(d2d) + list(ici)
        return keys, [("d2d", gw[k]) for k in d2d] + [("ici", gw[k]) for k in ici]

    def landed_in(keys, bufs):
        gw.update(zip(keys, bufs))

    ln_full = gathered[-1].transpose(1, 2, 0, 3).reshape(2, N_A, 1, -1)
    a_ln_g3, a_ln_b3 = ln_full[0], ln_full[1]
    mix_g3, mix_b3 = mix_ln_g[:, None, :], mix_ln_b[:, None, :]
    ffn_g3, ffn_b3 = ffn_ln_g[:, None, :], ffn_ln_b[:, None, :]
    bst = jnp.swapaxes(a_b_s, 1, 2)

    saved = []
    xb = _cast_bf16(xf)
    kb = vb = None
    for l in range(DEPTH):
        sv = dict(x_in=xb)
        last = l == DEPTH - 1
        if l == 0:
            keys, riders = riding(ici=[("ffn_w1", 0)])
        else:
            keys, riders = riding(d2d=[("ffn_w1", l), ("ffn_w2", l)], ici=mixer[l + 1] if l < N_A else [])
        if l < N_A:
            h, *bufs = _mm_fwd("a_in", xb, gw[("a_w_in", l)], None, True, riders=riders)
            landed_in(keys, bufs)
            vn = _gmlp_norm_fwd(h, a_ln_g3, a_ln_b3, l)
            gated = _gate_fwd(h, vn, a_w_s[l], bst[l])
            keys, riders = riding(d2d=[("ffn_w1", 0)], ici=[("ffn_w2", 0)]) if l == 0 else ([], [])
            xf, xb, xhat, rstd, *bufs = _mm_resid_ln("a_out", gated, gw[("a_w_out", l)], xf, mix_g3, mix_b3, l, riders)
            landed_in(keys, bufs)
            sv.update(h=h, vn=vn, gated=gated)
        else:
            j = l - N_A
            if l == N_A:
                kb, *bufs = _mm_fwd("sb_k", xb, gw[("sb_w_k", None)], None, False, _ep_bf16, outs=[(d, BF16)],
                                    riders=riders)
                landed_in(keys, bufs)
                keys, riders = [], ()
                vb = _mm_fwd("sb_v", xb, gw[("sb_w_v", None)], None, False, _ep_bf16, outs=[(d, BF16)])[0]
            q, *bufs = _mm_fwd("b_q", xb, gw[("b_w_q", j)], None, False, _ep_scale_q, outs=[(d, BF16)], riders=riders)
            landed_in(keys, bufs)
            ob, lsum = _attn_fwd(q, kb, vb)
            keys, riders = riding(ici=[] if last else mixer[l + 1])
            xf, xb, xhat, rstd, *bufs = _mm_resid_ln("b_out", ob, gw[("b_w_o", j)], xf, mix_g3, mix_b3, l, riders)
            landed_in(keys, bufs)
            sv.update(q=q, lsum=lsum, ob=ob)
        sv.update(x_mid=xb, xhat1=xhat, rstd1=rstd)
        dff = gw[("ffn_w1", l)].shape[-1] * N_CHIPS
        if l == 0:
            keys, riders = riding(d2d=[("ffn_w2", 0)], ici=mixer[1] + [("ffn_w1", 1)])
        else:
            keys, riders = riding(ici=[] if last else [("ffn_w1", l + 1)])
        pr, *bufs = _mm_fwd("ffn_1", xb, gw[("ffn_w1", l)], None, True, _ep_relu, outs=[(dff, BF16)], riders=riders)
        landed_in(keys, bufs)
        keys, riders = riding(d2d=[] if last else mixer[l + 1], ici=[] if last else [("ffn_w2", l + 1)])
        xf, xb, xhat, rstd, *bufs = _mm_resid_ln("ffn_2", pr, gw[("ffn_w2", l)], xf, ffn_g3, ffn_b3, l, riders, _square)
        landed_in(keys, bufs)
        sv.update(pr=pr, xhat2=xhat, rstd2=rstd)
        saved.append(sv)

    dx = xf

    pending = []
    pair_sums, landed = {}, {}
    place_arr = jnp.stack([chip, cc.astype(jnp.int32)])

    def arrived(took, outs):
        for (kind, key, arr), out in zip(took, outs):
            if kind == "pair":
                pair_sums[key] = _pair_sum(arr, out, place_arr)
                pending.append(("chip", key, pair_sums[key][1]))
            else:
                landed[key] = out

    def carrying(call, name, *args, **kw):
        took = []
        if name.startswith("ffn") or draining[0]:
            room = CARRIER_PARAMS
            for task in list(pending):
                size = given[task[1][0]].shape[-2] * given[task[1][0]].shape[-1] * N_CHIPS
                if task[0] == "pair" or room == CARRIER_PARAMS or size <= room:
                    took.append(task)
                    pending.remove(task)
                    room -= size if task[0] == "chip" else 0
        results = call(name, *args, riders=[(kind, arr) for kind, _, arr in took], **kw)
        own = len(results) - len(took)
        arrived(took, results[own:])
        return results[0] if own == 1 else results[:own]

    draining = [False]

    def bwd_act(*args, **kw):
        return carrying(_mm_bwd_act, *args, **kw)

    def bwd_w(key, name, a, dy, **kw):
        pending.append(("pair", key, carrying(_mm_bwd_w, name, a, dy, gw[key], COL_SHARDED[key[0]], **kw)))

    d_mix_g, d_mix_b, d_ffn_g, d_ffn_b = [None] * DEPTH, [None] * DEPTH, [None] * DEPTH, [None] * DEPTH
    d_ln_g, d_ln_b, d_ws, d_bs = [None] * N_A, [None] * N_A, [None] * N_A, [None] * N_A
    dk = dv = normed = None
    for l in reversed(range(DEPTH)):
        sv = saved[l]
        draining[0] = l == 0
        if l == DEPTH - 1:
            dr, drb, d_ffn_g[l], d_ffn_b[l], sq = _ln_bwd(dx, sv["xhat2"], sv["rstd2"], ffn_g3, l, target)
            loss = lax.psum(0.5 * sq[0, 0] / d, ("x", "y", "c"))
        elif normed:
            dr, drb = normed
            normed = None
        else:
            dr, drb, d_ffn_g[l], d_ffn_b[l] = _ln_bwd(dx, sv["xhat2"], sv["rstd2"], ffn_g3, l)
        dff = sv["pr"].shape[1]
        dhd = bwd_act("ffn_2_dx", drb, gw[("ffn_w2", l)], None, False, _ep_relu2_bwd, (sv["pr"],),
                      (pl.BlockSpec((_wide_tile(s), dff // N_CHIPS), lambda j, i, k: (i, j)),), out_dtype=BF16)
        bwd_w(("ffn_w2", l), "ffn_2_dw", sv["pr"], drb, a_fn=_square)
        dr, drb, d_mix_g[l], d_mix_b[l] = bwd_act(
            "ffn_1_dx", dhd, gw[("ffn_w1", l)], None, True, _ep_resid_ln_bwd, (dr, sv["xhat1"], sv["rstd1"], mix_g3),
            (_row_spec(d), _row_spec(d), _row_spec(1), _vec_spec(l, d)), through_norm=True)
        bwd_w(("ffn_w1", l), "ffn_1_dw", sv["x_mid"], dhd)

        quarter = pl.BlockSpec((_wide_tile(s), d // N_CHIPS), lambda j, i, k: (i, j))
        if l < N_A:
            dgated = bwd_act("a_out_dx", drb, gw[("a_w_out", l)], None, False)
            bwd_w(("a_w_out", l), "a_out_dw", sv["gated"], drb)
            du, dvn, d_ws[l], dbs_wide = _gate_bwd(dgated, sv["h"], sv["vn"], a_w_s[l], bst[l])
            d_bs[l] = dbs_wide[:, :, 0]
            dh, dlg, dlb = _gmlp_in_bwd(sv["h"], du, dvn, a_ln_g3, l)
            d_ln_g[l], d_ln_b[l] = dlg[0], dlb[0]
            if l:
                below = saved[l - 1]
                *normed, d_ffn_g[l - 1], d_ffn_b[l - 1] = bwd_act(
                    "a_in_dx", dh, gw[("a_w_in", l)], None, True, _ep_resid_ln_bwd,
                    (dr, below["xhat2"], below["rstd2"], ffn_g3),
                    (_row_spec(d), _row_spec(d), _row_spec(1), _vec_spec(l - 1, d)), through_norm=True)
            else:
                dx = bwd_act("a_in_dx", dh, gw[("a_w_in", l)], None, True, _ep_resid, (dr,), (_row_spec(d),))
            bwd_w(("a_w_in", l), "a_in_dw", sv["x_in"], dh)
        else:
            j = l - N_A
            do = bwd_act("b_out_dx", drb, gw[("b_w_o", j)], None, False)
            bwd_w(("b_w_o", j), "b_out_dw", sv["ob"], drb)
            dq, dk, dv = _attn_bwd(sv["q"], kb, vb, do, sv["lsum"], dk, dv)
            dx = bwd_act("b_q_dx", dq, gw[("b_w_q", j)], None, False, _ep_resid, (dr,), (quarter,))
            bwd_w(("b_w_q", j), "b_q_dw", sv["x_in"], dq)
            if l == N_A:
                dx = bwd_act("sb_k_dx", dk, gw[("sb_w_k", None)], None, False, _ep_add, (dx,), (quarter,))
                bwd_w(("sb_w_k", None), "sb_k_dw", sv["x_in"], dk)
                dx = bwd_act("sb_v_dx", dv, gw[("sb_w_v", None)], None, False, _ep_add, (dx,), (quarter,))
                bwd_w(("sb_w_v", None), "sb_v_dw", sv["x_in"], dv)
    grad_x = dx.reshape(x.shape)

    while pending:
        took = list(pending)
        pending.clear()
        for kind, exchange in (("pair", _pair_exchange), ("chip", _chip_exchange)):
            some = [t for t in took if t[0] == kind]
            if some:
                arrived(some, exchange([arr for _, _, arr in some]))

    stacked = []
    for n in BIG:
        layers = [None] if given[n].ndim == 2 else range(given[n].shape[0])
        out = lax.empty((len(layers),) + given[n].shape[-2:], F32)
        for at, l in enumerate(layers):
            out = _chip_sum(pair_sums[(n, l)][0], landed[(n, l)], place_arr, out, at)
        stacked.append(out)
    grads = {n: g.reshape(given[n].shape) for n, g in zip(BIG, _half_swap(stacked))}

    small_full = dict(a_ln_g=jnp.stack(d_ln_g), a_ln_b=jnp.stack(d_ln_b), a_w_s=jnp.stack(d_ws), a_b_s=jnp.stack(d_bs),
                      mix_ln_g=jnp.concatenate(d_mix_g), mix_ln_b=jnp.concatenate(d_mix_b),
                      ffn_ln_g=jnp.concatenate(d_ffn_g), ffn_ln_b=jnp.concatenate(d_ffn_b))
    packed = jnp.concatenate([small_full[n].reshape(-1) for n in SMALL])
    total = packed.shape[0]
    ncol = -(-total // (N_DEV * LANES)) * LANES
    packed = jnp.pad(packed, (0, N_DEV * ncol - total)).reshape(N_DEV, ncol)
    reduced = _all_reduce_small(packed).reshape(-1)
    off = 0
    for n in SMALL:
        size = small_full[n].size
        g = reduced[off:off + size].reshape(small_full[n].shape)
        off += size
        if n in ("a_ln_g", "a_ln_b"):
            wq = given[n].shape[1]
            g = lax.dynamic_slice_in_dim(g, chip * wq, wq, axis=1)
        grads[n] = g

    delta, new_m, new_v = {}, {}, {}
    for n in names:
        shape = given[n].shape
        dl, nm, nv, *g = _adamw(as2d(given[n]), as2d(grads[n]), as2d(mom[n]), as2d(var[n]), pass_g=n in BIG)
        delta[n], new_m[n], new_v[n] = dl.reshape(shape), nm.reshape(shape), nv.reshape(shape)
        if g:
            grads[n] = g[0].reshape(shape)

    order = ("a_w_in", "a_ln_g", "a_ln_b", "a_w_s", "a_b_s", "a_w_out", "sb_w_k", "sb_w_v", "b_w_q", "b_w_o",
             "mix_ln_g", "mix_ln_b", "ffn_ln_g", "ffn_ln_b", "ffn_w1", "ffn_w2")
    return (loss, grad_x, *[grads[n] for n in order], *[delta[n] for n in order],
            *[new_m[n] for n in order], *[new_v[n] for n in order])
```

```python
import math

import jax
import jax.numpy as jnp
from jax import lax
from jax.experimental import pallas as pl
from jax.experimental.pallas import tpu as pltpu

F32 = jnp.float32
BF16 = jnp.bfloat16
MESH = pl.DeviceIdType.MESH

N_CHIPS = 4
DEPTH = 4
N_A = 2
ALPHA = float((2 * DEPTH) ** 0.25)
LN_EPS = 1e-5
CHUNK = 64
GMLP_BLOCK = 128
GMLP_GROUPS = 8
HEAD_DIM = 64
LANES = 128
ATT_T = 256
ADAM_LR = 0.001
ADAM_B1 = 0.9
ADAM_B2 = 0.999
ADAM_EPS = 1e-08
ADAM_WD = 0.01
ADAM_STEP = 10
VMEM_LIMIT = 56 * 1024 * 1024
TM = 512
TM_WIDE = 1024
TS = 1024

NN = ((1,), (0,))
NT = ((1,), (1,))
TN = ((0,), (0,))


def _params(sem):
    return pltpu.CompilerParams(dimension_semantics=sem, vmem_limit_bytes=VMEM_LIMIT)


def _dot(a, b, contract):
    return lax.dot_general(a, b, (contract, ((), ())), preferred_element_type=F32)


def _rider_out(kind, arr):
    shape = (arr.shape[0], arr.shape[1] // 2, arr.shape[2]) if kind == "pair" else arr.shape
    return jax.ShapeDtypeStruct(shape, arr.dtype)


def _rider_copies(kind, src, dst, send_sems, recv_sems, base):
    x, y, c, chips = _place()
    me = 2 * x + y
    sibling = (x, y, 1 - c)

    def copy(k, part, land, to):
        return pltpu.make_async_remote_copy(src_ref=part, dst_ref=land, send_sem=send_sems.at[base + k],
                                            recv_sem=recv_sems.at[base + k], device_id=to, device_id_type=MESH)

    if kind == "pair":
        h = src.shape[1] // 2
        cp = copy(0, src.at[:, pl.ds((1 - c) * h, h)], dst, sibling)
        return [cp], [cp]
    h = dst.shape[1] // 2
    starts, arrivals = [], []
    for k, chip in enumerate(chips):
        blk = 2 * chip[0] + chip[1]
        if kind == "ici":
            starts.append(copy(k, dst.at[me, pl.ds(c * h, h)], dst.at[me, pl.ds(c * h, h)], (*chip, c)))
            arrivals.append(copy(k, dst.at[blk, pl.ds(c * h, h)], dst.at[blk, pl.ds(c * h, h)], (*chip, c)))
        elif kind == "d2d":
            starts.append(copy(k, dst.at[blk, pl.ds(c * h, h)], dst.at[blk, pl.ds(c * h, h)], sibling))
            arrivals.append(copy(k, dst.at[blk, pl.ds((1 - c) * h, h)], dst.at[blk, pl.ds((1 - c) * h, h)], sibling))
        else:
            starts.append(copy(k, src.at[blk], dst.at[me], (*chip, c)))
            arrivals.append(copy(k, src.at[blk], dst.at[blk], (*chip, c)))
    return starts, arrivals


RIDER_SEMS = 3
CARRIER_PARAMS = 5 * 2 ** 20


def _identity(a):
    return a


def _square(a):
    return a * a


def _matmul(name, operands, in_specs, out_shapes, out_specs, grid, contract, epilogue, acc_shape, aliases=None,
            chunks=None, riders=(), pick=False, a_fn=_identity, sequential=False):
    nk = grid[2]
    n_in, n_out, nr = len(operands), len(out_shapes), len(riders)
    n_plain = n_in + nr + n_out

    def body(*refs):
        ins, outs = refs[:n_in], refs[n_in + nr:n_plain]
        if nr:
            srcs, dsts = refs[n_in:n_in + nr], refs[n_plain:n_plain + nr]
            send_sems, recv_sems = refs[-2:]
            pid = [pl.program_id(ax) for ax in range(3)]
            first = (pid[0] == 0) & (pid[1] == 0) & (pid[2] == 0)
            last = (pid[0] == grid[0] - 1) & (pid[1] == grid[1] - 1) & (pid[2] == grid[2] - 1)

            def copies(n):
                return _rider_copies(riders[n][0], srcs[n], dsts[n], send_sems, recv_sems, RIDER_SEMS * n)

            @pl.when(first)
            def _():
                for n in range(nr):
                    for cp in copies(n)[0]:
                        cp.start()

        compute(refs, ins, outs)
        if nr:
            @pl.when(last)
            def _():
                for n in range(nr):
                    starts, arrivals = copies(n)
                    for cp in arrivals:
                        cp.wait_recv()
                    for cp in starts:
                        cp.wait_send()

    def compute(refs, ins, outs):
        if chunks is None:
            b = ins[1][pl.program_id(1)] if pick else ins[1][...]
            p = _dot(a_fn(ins[0][...].astype(BF16)), b.astype(BF16), contract)
        else:
            width = ins[0].shape[1] // chunks
            p = None
            for j in range(chunks):
                pj = _dot(a_fn(ins[0][:, j * width:(j + 1) * width].astype(BF16)), ins[1][j].astype(BF16), contract)
                p = pj if p is None else p + pj
        if nk == 1:
            epilogue(p, ins[2:], outs)
            return
        acc = refs[n_plain + nr]
        k = pl.program_id(2)

        @pl.when(k == 0)
        def _():
            acc[...] = p

        @pl.when((k > 0) & (k < nk - 1))
        def _():
            acc[...] += p

        @pl.when(k == nk - 1)
        def _():
            epilogue(acc[...] + p, ins[2:], outs)

    rbufs = [b for _, b in riders]
    in_place = {n_in + n: n_out + n for n, (kind, _) in enumerate(riders) if kind in ("ici", "d2d")}
    scratch = ([] if nk == 1 else [pltpu.VMEM(acc_shape, F32)]) \
        + [pltpu.SemaphoreType.DMA((RIDER_SEMS * nr,))] * (2 if nr else 0)
    return pl.pallas_call(
        body, name=name, grid=grid, in_specs=list(in_specs) + _any_specs(nr), out_specs=list(out_specs) + _any_specs(nr),
        out_shape=list(out_shapes) + [_rider_out(kind, b) for kind, b in riders],
        scratch_shapes=scratch,
        input_output_aliases={**(aliases or {}), **in_place},
        compiler_params=_params(("arbitrary",) * 3 if nr or sequential else ("parallel", "parallel", "arbitrary")),
    )(*operands, *rbufs)


def _wspec(w, layer, whole=False):
    r, c = w.shape[-2:]
    lead = N_CHIPS if whole else None
    if w.ndim == 4:
        return pl.BlockSpec((lead, None, r, c), lambda j, i, k: (0 if whole else j, layer, 0, 0))
    return pl.BlockSpec((lead, r, c), lambda j, i, k: (0 if whole else j, 0, 0))


def _wide_tile(s):
    return min(TM_WIDE, s)


def _ep_store(p, ins, outs):
    for o in outs:
        o[...] = p.astype(o.dtype)


def _rows_first(spec):
    return pl.BlockSpec(spec.block_shape, lambda i, j, k: spec.index_map(j, i, k))


def _mm_fwd(name, a, w, layer, col_sharded, epilogue=_ep_store, extras=(), extra_specs=(), outs=None, riders=(),
            a_fn=_identity):
    s = a.shape[0]
    r, c = w.shape[-2:]
    if col_sharded:
        tm = _wide_tile(s)
        grid = (s // tm, N_CHIPS, 1)
        a_spec = pl.BlockSpec((tm, r), lambda j, i, k: (i, 0))
        n_out = N_CHIPS * c
    else:
        tm = TM
        grid = (1, s // tm, 1)
        a_spec = pl.BlockSpec((tm, N_CHIPS * r), lambda j, i, k: (i, 0))
        n_out = c
    if outs is None:
        outs = [(n_out, F32)]
    out_shapes = [jax.ShapeDtypeStruct((s, n), dt) for n, dt in outs]
    out_specs = [pl.BlockSpec((tm, c if n == n_out else n), lambda j, i, k: (i, j)) for n, _ in outs]
    in_specs = [a_spec, _wspec(w, layer, True)] + list(extra_specs)
    if col_sharded:
        in_specs, out_specs = [_rows_first(sp) for sp in in_specs], [_rows_first(sp) for sp in out_specs]
    return _matmul(name, (a, w) + tuple(extras), in_specs, out_shapes, out_specs, grid, NN, epilogue, (tm, c),
                   chunks=None if col_sharded else N_CHIPS, riders=riders, pick=col_sharded, a_fn=a_fn)


def _mm_bwd_act(name, dy, w, layer, col_sharded, epilogue=_ep_store, extras=(), extra_specs=(), out_dtype=F32,
                riders=(), through_norm=False):
    s = dy.shape[0]
    r, c = w.shape[-2:]
    if col_sharded:
        tm = TM
        grid = (1, s // tm, 1)
        a_spec = pl.BlockSpec((tm, N_CHIPS * c), lambda j, i, k: (i, 0))
        n_out = r
    else:
        tm = _wide_tile(s)
        grid = (s // tm, N_CHIPS, 1)
        a_spec = pl.BlockSpec((tm, c), lambda j, i, k: (i, 0))
        n_out = N_CHIPS * r
    in_specs = [a_spec, _wspec(w, layer, True)] + list(extra_specs)
    o_spec = pl.BlockSpec((tm, r), lambda j, i, k: (i, j))
    if not col_sharded:
        in_specs, o_spec = [_rows_first(sp) for sp in in_specs], _rows_first(o_spec)
    out_shapes, out_specs = [jax.ShapeDtypeStruct((s, n_out), out_dtype)], [o_spec]
    if through_norm:
        vec = pl.BlockSpec((1, n_out), lambda j, i, k: (0, 0))
        out_shapes = [jax.ShapeDtypeStruct((s, n_out), F32), jax.ShapeDtypeStruct((s, n_out), BF16),
                      jax.ShapeDtypeStruct((1, n_out), F32), jax.ShapeDtypeStruct((1, n_out), F32)]
        out_specs = [o_spec, o_spec, vec, vec]
    return _matmul(name, (dy, w) + tuple(extras), in_specs, out_shapes, out_specs, grid, NT, epilogue, (tm, r),
                   chunks=N_CHIPS if col_sharded else None, riders=riders, pick=not col_sharded,
                   sequential=through_norm)


def _mm_bwd_w(name, a, dy, w, col_sharded, riders=(), a_fn=_identity):
    s = a.shape[0]
    r, c = w.shape[-2:]
    ts = min(TS, s)
    grid = (N_CHIPS, 1, s // ts)
    if col_sharded:
        a_spec = pl.BlockSpec((ts, r), lambda j, i, k: (k, 0))
        b_spec = pl.BlockSpec((ts, c), lambda j, i, k: (k, j))
    else:
        a_spec = pl.BlockSpec((ts, r), lambda j, i, k: (k, j))
        b_spec = pl.BlockSpec((ts, c), lambda j, i, k: (k, 0))

    def epilogue(p, ins, outs):
        outs[0][...] = p

    return _matmul(name, (a, dy), [a_spec, b_spec], [jax.ShapeDtypeStruct(w.shape, F32)], [_wspec(w, None)], grid, TN,
                   epilogue, (r, c), riders=riders, a_fn=a_fn)


def _row_spec(n):
    return pl.BlockSpec((TM, n), lambda j, i, k: (i, 0))


def _vec_spec(layer, n):
    return pl.BlockSpec((None, 1, n), lambda j, i, k: (layer, 0, 0))


def _ep_resid_ln(p, ins, outs):
    x_ref, g_ref, b_ref = ins
    xf_ref, xb_ref, xhat_ref, rstd_ref = outs
    r = ALPHA * x_ref[...] + p
    mu = jnp.mean(r, axis=-1, keepdims=True)
    d = r - mu
    var = jnp.mean(d * d, axis=-1, keepdims=True)
    rstd = lax.rsqrt(var + LN_EPS)
    xhat = d * rstd
    y = xhat * g_ref[...] + b_ref[...]
    xf_ref[...] = y
    xb_ref[...] = y.astype(BF16)
    xhat_ref[...] = xhat
    rstd_ref[...] = rstd


def _mm_resid_ln(name, a, w, x, g3, b3, ln_layer, riders=(), a_fn=_identity):
    d = x.shape[1]
    return _mm_fwd(name, a, w, None, False, _ep_resid_ln, (x, g3, b3),
                   (_row_spec(d), _vec_spec(ln_layer, d), _vec_spec(ln_layer, d)),
                   outs=[(d, F32), (d, BF16), (d, F32), (1, F32)], riders=riders, a_fn=a_fn)


def _ep_relu(p, ins, outs):
    outs[0][...] = jnp.maximum(p, 0.0).astype(BF16)


def _ep_scale_q(p, ins, outs):
    outs[0][...] = (p * (HEAD_DIM ** -0.5)).astype(BF16)


def _ep_bf16(p, ins, outs):
    outs[0][...] = p.astype(BF16)


def _ep_relu2_bwd(p, ins, outs):
    outs[0][...] = (p * (2.0 * ins[0][...].astype(F32))).astype(BF16)


def _ep_resid(p, ins, outs):
    outs[0][...] = ALPHA * ins[0][...] + p


def _ep_resid_ln_bwd(p, ins, outs):
    dr_ref, xh_ref, rs_ref, g_ref = ins
    _ln_bwd_rows(ALPHA * dr_ref[...] + p, xh_ref, rs_ref, g_ref, pl.program_id(1) == 0, *outs)


def _ep_add(p, ins, outs):
    outs[0][...] = ins[0][...] + p


def _gelu_grad(x):
    c0 = math.sqrt(2.0 / math.pi)
    t = jnp.tanh(c0 * (x + 0.044715 * (x * x * x)))
    return 0.5 * (1.0 + t) + (0.5 * x) * (1.0 - t * t) * (c0 * (1.0 + 3.0 * 0.044715 * (x * x)))


def _cast_bf16(w2d):
    r, c = w2d.shape
    tr = min(r, 512)

    def body(w_ref, o_ref):
        o_ref[...] = w_ref[...].astype(BF16)

    return pl.pallas_call(
        body, name="cast_bf16", grid=(r // tr,),
        in_specs=[pl.BlockSpec((tr, c), lambda i: (i, 0))], out_specs=pl.BlockSpec((tr, c), lambda i: (i, 0)),
        out_shape=jax.ShapeDtypeStruct((r, c), BF16), compiler_params=_params(("parallel",)),
    )(w2d)


def _cast_into_slot(w, layer, chip):
    r, c = w.shape[-2:]
    tr = min(r, 512)

    def body(chip_ref, w_ref, o_ref):
        o_ref[...] = w_ref[...].astype(BF16)

    if layer is None:
        w_spec = pl.BlockSpec((tr, c), lambda i, chip_ref: (i, 0))
    else:
        w_spec = pl.BlockSpec((None, tr, c), lambda i, chip_ref: (layer, i, 0))
    grid_spec = pltpu.PrefetchScalarGridSpec(
        num_scalar_prefetch=1, grid=(r // tr,), in_specs=[w_spec],
        out_specs=pl.BlockSpec((None, tr, c), lambda i, chip_ref: (chip_ref[0], i, 0)))
    return pl.pallas_call(
        body, name="cast_into_slot", grid_spec=grid_spec,
        out_shape=jax.ShapeDtypeStruct((N_CHIPS, r, c), BF16), compiler_params=_params(("parallel",)),
    )(chip, w)


def _gmlp_norm_fwd(h, g3, b3, layer):
    s, w2 = h.shape
    w = w2 // 2

    def body(h_ref, g_ref, b_ref, o_ref):
        z = jax.nn.gelu(h_ref[...])
        mu = jnp.mean(z, axis=-1, keepdims=True)
        d = z - mu
        var = jnp.mean(d * d, axis=-1, keepdims=True)
        o_ref[...] = (d * lax.rsqrt(var + LN_EPS) * g_ref[...] + b_ref[...]).astype(BF16)

    vec = pl.BlockSpec((None, 1, w), lambda i: (layer, 0, 0))
    return pl.pallas_call(
        body, name="gmlp_norm_fwd", grid=(s // TM,),
        in_specs=[pl.BlockSpec((TM, w), lambda i: (i, 1)), vec, vec],
        out_specs=pl.BlockSpec((TM, w), lambda i: (i, 0)),
        out_shape=jax.ShapeDtypeStruct((s, w), BF16), compiler_params=_params(("parallel",)),
    )(h, g3, b3)


def _chunk_mask():
    t = lax.broadcasted_iota(jnp.int32, (GMLP_BLOCK, GMLP_BLOCK), 0)
    s = lax.broadcasted_iota(jnp.int32, (GMLP_BLOCK, GMLP_BLOCK), 1)
    return (s // CHUNK) <= (t // CHUNK)


SG_ROWS = 512


def _gate_fwd(h, vn, ws, bst):
    s, w = vn.shape
    gd = w // GMLP_GROUPS

    def body(h_ref, v_ref, ws_ref, bs_ref, o_ref):
        mask = _chunk_mask()
        for g in range(GMLP_GROUPS):
            wm = jnp.where(mask, ws_ref[g], 0.0).astype(BF16)
            bias = bs_ref[:, g:g + 1]
            cols = slice(g * gd, (g + 1) * gd)
            for n in range(SG_ROWS // GMLP_BLOCK):
                rows = slice(n * GMLP_BLOCK, (n + 1) * GMLP_BLOCK)
                sp = _dot(wm, v_ref[rows, cols], NN) + bias
                o_ref[rows, cols] = (jax.nn.gelu(h_ref[rows, cols]) * sp).astype(BF16)

    return pl.pallas_call(
        body, name="gate_fwd", grid=(s // SG_ROWS,),
        in_specs=[pl.BlockSpec((SG_ROWS, w), lambda i: (i, 0)), pl.BlockSpec((SG_ROWS, w), lambda i: (i, 0)),
                  pl.BlockSpec(ws.shape, lambda i: (0, 0, 0)), pl.BlockSpec(bst.shape, lambda i: (0, 0))],
        out_specs=pl.BlockSpec((SG_ROWS, w), lambda i: (i, 0)),
        out_shape=jax.ShapeDtypeStruct((s, w), BF16), compiler_params=_params(("parallel",)),
    )(h, vn, ws, bst)


GB_ROWS = 256


def _gmlp_bwd(dgated, h, vn, ws, bst, g3, layer):
    s, w = vn.shape
    gd = w // GMLP_GROUPS
    nsteps = s // SG_ROWS

    def body(dg_ref, h_ref, v_ref, ws_ref, bs_ref, g_ref, dh_ref, dws_ref, dbs_ref, dlg_ref, dlb_ref, dsum, du_s, dv_s):
        i = pl.program_id(0)

        @pl.when(i == 0)
        def _():
            dws_ref[...] = jnp.zeros_like(dws_ref)
            dsum[...] = jnp.zeros_like(dsum)
            dlg_ref[...] = jnp.zeros_like(dlg_ref)
            dlb_ref[...] = jnp.zeros_like(dlb_ref)

        mask = _chunk_mask()
        for g in range(GMLP_GROUPS):
            wm = jnp.where(mask, ws_ref[g], 0.0).astype(BF16)
            bias = bs_ref[:, g:g + 1]
            cols = slice(g * gd, (g + 1) * gd)
            dw = jnp.zeros((GMLP_BLOCK, GMLP_BLOCK), F32)
            dsg = jnp.zeros((GMLP_BLOCK, gd), F32)
            for n in range(SG_ROWS // GMLP_BLOCK):
                rows = slice(n * GMLP_BLOCK, (n + 1) * GMLP_BLOCK)
                vb = v_ref[rows, cols]
                sp = _dot(wm, vb, NN) + bias
                dg = dg_ref[rows, cols]
                du_s[rows, cols] = dg * sp
                ds = dg * jax.nn.gelu(h_ref[rows, cols])
                dsb = ds.astype(BF16)
                dw += _dot(dsb, vb, NT)
                dsg += ds
                dv_s[rows, cols] = _dot(wm, dsb, TN)
            dws_ref[g] += dw
            dsum[:, cols] += dsg

        for r0 in range(0, SG_ROWS, GB_ROWS):
            rows = slice(r0, r0 + GB_ROWS)
            hu = h_ref[rows, :w]
            hv = h_ref[rows, w:]
            dh_ref[rows, :w] = (du_s[rows, :] * _gelu_grad(hu)).astype(BF16)
            z = jax.nn.gelu(hv)
            mu = jnp.mean(z, axis=-1, keepdims=True)
            d = z - mu
            var = jnp.mean(d * d, axis=-1, keepdims=True)
            rstd = lax.rsqrt(var + LN_EPS)
            xhat = d * rstd
            dy = dv_s[rows, :]
            dlb_ref[...] += jnp.sum(dy, axis=0, keepdims=True)
            dlg_ref[...] += jnp.sum(dy * xhat, axis=0, keepdims=True)
            dxh = dy * g_ref[...]
            m1 = jnp.mean(dxh, axis=-1, keepdims=True)
            m2 = jnp.mean(dxh * xhat, axis=-1, keepdims=True)
            dz = rstd * (dxh - m1 - xhat * m2)
            dh_ref[rows, w:] = (dz * _gelu_grad(hv)).astype(BF16)

        @pl.when(i == nsteps - 1)
        def _():
            for g in range(GMLP_GROUPS):
                dws_ref[g] = jnp.where(mask, dws_ref[g], 0.0)
                tot = jnp.sum(dsum[:, g * gd:(g + 1) * gd], axis=-1, keepdims=True)
                dbs_ref[g] = jnp.broadcast_to(tot, (GMLP_BLOCK, LANES))

    tile = pl.BlockSpec((SG_ROWS, w), lambda i: (i, 0))
    wide = pl.BlockSpec((SG_ROWS, 2 * w), lambda i: (i, 0))
    vec = pl.BlockSpec((1, w), lambda i: (0, 0))
    return pl.pallas_call(
        body, name="gmlp_bwd", grid=(nsteps,),
        in_specs=[tile, wide, tile, pl.BlockSpec(ws.shape, lambda i: (0, 0, 0)), pl.BlockSpec(bst.shape, lambda i: (0, 0)),
                  pl.BlockSpec((None, 1, w), lambda i: (layer, 0, 0))],
        out_specs=[wide, pl.BlockSpec(ws.shape, lambda i: (0, 0, 0)),
                   pl.BlockSpec((GMLP_GROUPS, GMLP_BLOCK, LANES), lambda i: (0, 0, 0)), vec, vec],
        out_shape=[jax.ShapeDtypeStruct((s, 2 * w), BF16), jax.ShapeDtypeStruct(ws.shape, F32),
                   jax.ShapeDtypeStruct((GMLP_GROUPS, GMLP_BLOCK, LANES), F32),
                   jax.ShapeDtypeStruct((1, w), F32), jax.ShapeDtypeStruct((1, w), F32)],
        scratch_shapes=[pltpu.VMEM((GMLP_BLOCK, w), F32), pltpu.VMEM((SG_ROWS, w), F32), pltpu.VMEM((SG_ROWS, w), F32)],
        compiler_params=_params(("arbitrary",)),
    )(dgated, h, vn, ws, bst, g3)


def _ln_bwd_rows(dy, xh_ref, rs_ref, g_ref, first, dr_ref, drb_ref, dg_ref, db_ref):
    @pl.when(first)
    def _():
        dg_ref[...] = jnp.zeros_like(dg_ref)
        db_ref[...] = jnp.zeros_like(db_ref)

    xhat = xh_ref[...]
    db_ref[...] += jnp.sum(dy, axis=0, keepdims=True)
    dg_ref[...] += jnp.sum(dy * xhat, axis=0, keepdims=True)
    dxh = dy * g_ref[...]
    m1 = jnp.mean(dxh, axis=-1, keepdims=True)
    m2 = jnp.mean(dxh * xhat, axis=-1, keepdims=True)
    dr = rs_ref[...] * (dxh - m1 - xhat * m2)
    dr_ref[...] = dr
    drb_ref[...] = dr.astype(BF16)


def _ln_bwd(dy, xhat, rstd, g3, layer, target=None):
    s, d = dy.shape
    nsteps = s // TM
    head = target is not None

    def body(*refs):
        dy_ref, xh_ref, rs_ref, g_ref = refs[:4]
        dr_ref, drb_ref, dg_ref, db_ref = refs[4 + head:8 + head]
        first = pl.program_id(0) == 0
        dyv = dy_ref[...]
        if head:
            l_ref = refs[8 + head]

            @pl.when(first)
            def _():
                l_ref[...] = jnp.zeros_like(l_ref)

            e = dyv - refs[4][...]
            l_ref[...] += jnp.sum(jnp.sum(e * e, axis=1, keepdims=True), axis=0, keepdims=True)
            dyv = e * (1.0 / d)
        _ln_bwd_rows(dyv, xh_ref, rs_ref, g_ref, first, dr_ref, drb_ref, dg_ref, db_ref)

    tile = pl.BlockSpec((TM, d), lambda i: (i, 0))
    vec = pl.BlockSpec((1, d), lambda i: (0, 0))
    one = pl.BlockSpec((1, 1), lambda i: (0, 0))
    return pl.pallas_call(
        body, name="ln_bwd", grid=(nsteps,),
        in_specs=[tile, tile, pl.BlockSpec((TM, 1), lambda i: (i, 0)), pl.BlockSpec((None, 1, d), lambda i: (layer, 0, 0))]
        + ([tile] if head else []),
        out_specs=[tile, tile, vec, vec] + ([one] if head else []),
        out_shape=[jax.ShapeDtypeStruct((s, d), F32), jax.ShapeDtypeStruct((s, d), BF16),
                   jax.ShapeDtypeStruct((1, d), F32), jax.ShapeDtypeStruct((1, d), F32)]
        + ([jax.ShapeDtypeStruct((1, 1), F32)] if head else []),
        compiler_params=_params(("arbitrary",)),
    )(dy, xhat, rstd, g3, *([target] if head else []))


LOG2E = 1.4426950408889634
DEAD_LOG2 = -160.0
FIRST_LANE = 1


def _sb_terms(z, causal):
    z2 = z * LOG2E
    e = jnp.exp2(-jnp.abs(z2))
    l1p = jnp.log2(1.0 + e)
    lb = jnp.minimum(z2, 0.0) - l1p
    lr = lb - z2
    if causal is not None:
        lr = jnp.where(causal, lr, 0.0)
    return lb, lr, e


def _split_hi_lo(x):
    hi = x.astype(BF16)
    lo = (x - hi.astype(F32)).astype(BF16)
    return jnp.concatenate([hi, lo], axis=1)


def _tri2(prefix):
    r = lax.broadcasted_iota(jnp.int32, (2 * ATT_T, ATT_T), 0) % ATT_T
    c = lax.broadcasted_iota(jnp.int32, (2 * ATT_T, ATT_T), 1)
    return jnp.where((r <= c) if prefix else (r >= c), 1.0, 0.0).astype(BF16)


def _att_masks():
    r = lax.broadcasted_iota(jnp.int32, (ATT_T, ATT_T), 0)
    c = lax.broadcasted_iota(jnp.int32, (ATT_T, ATT_T), 1)
    return c < r, lax.broadcasted_iota(jnp.int32, (1, LANES), 1) < HEAD_DIM


def _attn_fwd(q, k, v):
    s, d = q.shape
    nq = s // ATT_T

    def body(q_ref, k_ref, v_ref, tri_ref, ob_ref, lsum_ref, acc_a, acc_b, rem_a, rem_b):
        i = pl.program_id(1)
        tri = tri_ref[...]
        causal, head_a = _att_masks()
        q2 = q_ref[...]
        zero = jnp.zeros_like(q2)
        qa = jnp.where(head_a, q2, zero)
        qb = jnp.where(head_a, zero, q2)
        acc_a[...] = jnp.zeros_like(acc_a)
        acc_b[...] = jnp.zeros_like(acc_b)
        rem_a[...] = jnp.zeros_like(rem_a)
        rem_b[...] = jnp.zeros_like(rem_b)

        def block(kb, mask):
            rows = pl.ds(pl.multiple_of(kb * ATT_T, ATT_T), ATT_T)
            k2 = k_ref[rows, :]
            v2 = v_ref[rows, :]
            heads = ((qa, acc_a, rem_a), (qb, acc_b, rem_b))
            zs = [_dot(qm, k2, NT) for qm, _, _ in heads]
            terms = [_sb_terms(z, mask) for z in zs]
            sums = [_dot(_split_hi_lo(lr), tri, NN) for _, lr, _ in terms]
            for (_, acc, rem), (lb, lr, _), sincl in zip(heads, terms, sums):
                a = jnp.exp2(lb + (sincl - lr) + rem[...])
                if mask is not None:
                    a = jnp.where(mask, a, 0.0)
                rem[...] += sincl[:, 0:1]
                acc[...] += _dot(a.astype(BF16), v2, NN)

        block(i, causal)

        def live():
            return jnp.maximum(jnp.max(rem_a[...]), jnp.max(rem_b[...])) > DEAD_LOG2

        def go_on(carry):
            t, alive = carry
            return (t < i) & alive

        def step(carry):
            t, _ = carry
            block(i - 1 - t, None)
            return t + 1, live()

        done, _ = lax.while_loop(go_on, step, (jnp.int32(0), live()))
        first = (i - done).astype(F32)
        ob_ref[...] = jnp.where(head_a, acc_a[...], acc_b[...]).astype(BF16)
        lane = lax.broadcasted_iota(jnp.int32, (1, LANES), 1)
        lsum_ref[...] = jnp.where(lane == FIRST_LANE, first, jnp.where(head_a, rem_a[...], rem_b[...]))

    qspec = pl.BlockSpec((ATT_T, LANES), lambda p, i: (i, p))
    kspec = pl.BlockSpec((s, LANES), lambda p, i: (0, p))
    return pl.pallas_call(
        body, name="attn_fwd", grid=(d // LANES, nq),
        in_specs=[qspec, kspec, kspec, pl.BlockSpec((2 * ATT_T, ATT_T), lambda p, i: (0, 0))],
        out_specs=[qspec, qspec],
        out_shape=[jax.ShapeDtypeStruct((s, d), BF16), jax.ShapeDtypeStruct((s, d), F32)],
        scratch_shapes=[pltpu.VMEM((ATT_T, LANES), F32), pltpu.VMEM((ATT_T, LANES), F32),
                        pltpu.VMEM((ATT_T, 1), F32), pltpu.VMEM((ATT_T, 1), F32)],
        compiler_params=_params(("parallel", "arbitrary")),
    )(q, k, v, _tri2(prefix=False))


def _attn_bwd(q, k, v, do, lsum, dk_prev=None, dv_prev=None):
    s, d = q.shape
    nq = s // ATT_T
    has_prev = dk_prev is not None

    def body(*refs):
        q_ref, k_ref, v_ref, do_ref, ls_ref, tri_ref = refs[:6]
        n_in = 8 if has_prev else 6
        dq_ref, dk_ref, dv_ref, acc_a, acc_b, pre_a, pre_b, gp_a, gp_b, dkt, dvt = refs[n_in:]
        i = pl.program_id(1)

        @pl.when(i == 0)
        def _():
            dkt[...] = jnp.zeros_like(dkt)
            dvt[...] = jnp.zeros_like(dvt)

        tri = tri_ref[...]
        causal, head_a = _att_masks()
        q2 = q_ref[...]
        zero = jnp.zeros_like(q2)
        qa = jnp.where(head_a, q2, zero)
        qb = jnp.where(head_a, zero, q2)
        do2 = do_ref[...]
        doa = jnp.where(head_a, do2, 0.0).astype(BF16)
        dob = jnp.where(head_a, 0.0, do2).astype(BF16)
        row_a = lax.broadcasted_iota(jnp.int32, (LANES, 1), 0) < HEAD_DIM
        qt = q2.astype(F32).T
        dot_ = do2.T
        qta, qtb = jnp.where(row_a, qt, 0.0).astype(BF16), jnp.where(row_a, 0.0, qt).astype(BF16)
        dota, dotb = jnp.where(row_a, dot_, 0.0).astype(BF16), jnp.where(row_a, 0.0, dot_).astype(BF16)
        ls2 = ls_ref[...]
        tot_a = ls2[:, 0:1]
        tot_b = ls2[:, HEAD_DIM:HEAD_DIM + 1]
        for r in (acc_a, acc_b, pre_a, pre_b, gp_a, gp_b):
            r[...] = jnp.zeros_like(r)

        def block(kb, mask):
            rows = pl.ds(pl.multiple_of(kb * ATT_T, ATT_T), ATT_T)
            k2 = k_ref[rows, :]
            v2 = v_ref[rows, :]
            dk_new = jnp.zeros((LANES, ATT_T), F32)
            dv_new = jnp.zeros((LANES, ATT_T), F32)
            heads = ((qa, doa, tot_a, acc_a, pre_a, gp_a, qta, dota), (qb, dob, tot_b, acc_b, pre_b, gp_b, qtb, dotb))
            zs = [_dot(h[0], k2, NT) for h in heads]
            das = [_dot(h[1], v2, NT) for h in heads]
            terms = [_sb_terms(z, mask) for z in zs]
            psums = [_dot(_split_hi_lo(lr), tri, NN) for _, lr, _ in terms]
            gs, abs_ = [], []
            for h, (lb, _, _), pincl, da in zip(heads, terms, psums, das):
                tot, pre = h[2], h[4]
                a = jnp.exp2(lb + (tot - (pre[...] + pincl)))
                if mask is not None:
                    a = jnp.where(mask, a, 0.0)
                pre[...] += pincl[:, ATT_T - 1:ATT_T]
                gs.append(a * da)
                abs_.append(a.astype(BF16))
            gsums = [_dot(g.astype(BF16), tri[:ATT_T], NN) for g in gs]
            dzs = []
            for h, z, (_, _, e), g, gincl in zip(heads, zs, terms, gs, gsums):
                gpre = h[5]
                gbefore = gpre[...] + (gincl - g)
                gpre[...] += gincl[:, ATT_T - 1:ATT_T]
                inv = 1.0 / (1.0 + e)
                beta = jnp.where(z >= 0.0, inv, e * inv)
                dz = g - beta * (g + gbefore)
                if mask is not None:
                    dz = jnp.where(mask, dz, 0.0)
                dzs.append(dz.astype(BF16))
            for h, ab, dzb in zip(heads, abs_, dzs):
                dv_new += _dot(h[7], ab, NN)
                dk_new += _dot(h[6], dzb, NN)
                h[3][...] += _dot(dzb, k2, NN)
            cols = pl.ds(pl.multiple_of(kb * ATT_T, ATT_T), ATT_T)
            dkt[:, cols] += dk_new
            dvt[:, cols] += dv_new

        def step(kb, carry):
            block(kb, None)
            return carry

        first = jnp.clip(jnp.max(ls2[:, FIRST_LANE:FIRST_LANE + 1]).astype(jnp.int32), 0, i)
        lax.fori_loop(first, i, step, 0)
        block(i, causal)
        dq_ref[...] = (jnp.where(head_a, acc_a[...], acc_b[...]) * (HEAD_DIM ** -0.5)).astype(BF16)

        @pl.when(i == nq - 1)
        def _():
            for n in range(nq):
                rows = slice(n * ATT_T, (n + 1) * ATT_T)
                dkn, dvn = dkt[:, rows].T, dvt[:, rows].T
                if has_prev:
                    dkn, dvn = dkn + refs[6][rows, :], dvn + refs[7][rows, :]
                dk_ref[rows, :] = dkn
                dv_ref[rows, :] = dvn

    qspec = pl.BlockSpec((ATT_T, LANES), lambda p, i: (i, p))
    kspec = pl.BlockSpec((s, LANES), lambda p, i: (0, p))
    ins = [q, k, v, do, lsum, _tri2(prefix=True)] + ([dk_prev, dv_prev] if has_prev else [])
    return pl.pallas_call(
        body, name="attn_bwd", grid=(d // LANES, nq),
        in_specs=[qspec, kspec, kspec, qspec, qspec, pl.BlockSpec((2 * ATT_T, ATT_T), lambda p, i: (0, 0))]
        + ([kspec, kspec] if has_prev else []),
        out_specs=[qspec, kspec, kspec],
        out_shape=[jax.ShapeDtypeStruct((s, d), BF16), jax.ShapeDtypeStruct((s, d), F32), jax.ShapeDtypeStruct((s, d), F32)],
        scratch_shapes=[pltpu.VMEM((ATT_T, LANES), F32), pltpu.VMEM((ATT_T, LANES), F32)]
        + [pltpu.VMEM((ATT_T, 1), F32)] * 4 + [pltpu.VMEM((LANES, s), F32)] * 2,
        compiler_params=_params(("parallel", "arbitrary")),
    )(*ins)


def _place():
    x, y, c = lax.axis_index("x"), lax.axis_index("y"), lax.axis_index("c")
    chips = [(1 - x, y), (x, 1 - y), (1 - x, 1 - y)]
    return x, y, c, chips


def _any_specs(n):
    return [pl.BlockSpec(memory_space=pl.ANY)] * n


def _gather_weights(bufs):
    n = len(bufs)

    def body(*refs):
        outs = refs[n:2 * n]
        send_sems, recv_sems = refs[2 * n:]
        x, y, c, chips = _place()
        me = 2 * x + y
        sibling = (x, y, 1 - c)

        def half(a, blk, hc):
            h = outs[a].shape[1] // 2
            return outs[a].at[blk, pl.ds(hc * h, h)]

        def copy(a, k, part, to):
            return pltpu.make_async_remote_copy(src_ref=part, dst_ref=part, send_sem=send_sems.at[a, k],
                                                recv_sem=recv_sems.at[a, k], device_id=to, device_id_type=MESH)

        sent = []
        for a in range(n):
            for k, chip in enumerate(chips):
                sent.append(copy(a, k, half(a, me, c), (*chip, c)))
                sent[-1].start()
        for a in range(n):
            for k, chip in enumerate(chips):
                blk = 2 * chip[0] + chip[1]
                copy(a, k, half(a, blk, c), sibling).wait_recv()
                sent.append(copy(a, 3 + k, half(a, blk, c), sibling))
                sent[-1].start()
        for a in range(n):
            for k, chip in enumerate(chips):
                blk = 2 * chip[0] + chip[1]
                copy(a, 3 + k, half(a, blk, 1 - c), sibling).wait_recv()
        for cp in sent:
            cp.wait_send()

    return pl.pallas_call(
        body, name="gather_weights", in_specs=_any_specs(n), out_specs=_any_specs(n),
        out_shape=[jax.ShapeDtypeStruct(w.shape, w.dtype) for w in bufs],
        input_output_aliases={a: a for a in range(n)},
        scratch_shapes=[pltpu.SemaphoreType.DMA((n, 6)), pltpu.SemaphoreType.DMA((n, 6))],
        compiler_params=pltpu.CompilerParams(has_side_effects=True),
    )(*bufs)


def _pair_exchange(grads):
    n = len(grads)

    def body(*refs):
        ins, outs = refs[:n], refs[n:2 * n]
        send_sems, recv_sems = refs[2 * n:]
        x, y, c, _ = _place()
        cps = []
        for a in range(n):
            h = ins[a].shape[1] // 2
            cps.append(pltpu.make_async_remote_copy(
                src_ref=ins[a].at[:, pl.ds((1 - c) * h, h)], dst_ref=outs[a], send_sem=send_sems.at[a],
                recv_sem=recv_sems.at[a], device_id=(x, y, 1 - c), device_id_type=MESH))
            cps[-1].start()
        for cp in cps:
            cp.wait()

    return pl.pallas_call(
        body, name="pair_exchange", in_specs=_any_specs(n), out_specs=_any_specs(n),
        out_shape=[jax.ShapeDtypeStruct((g.shape[0], g.shape[1] // 2, g.shape[2]), g.dtype) for g in grads],
        scratch_shapes=[pltpu.SemaphoreType.DMA((n,)), pltpu.SemaphoreType.DMA((n,))],
        compiler_params=pltpu.CompilerParams(has_side_effects=True),
    )(*grads)


def _chip_exchange(parts):
    n = len(parts)

    def body(*refs):
        ins, outs = refs[:n], refs[n:2 * n]
        send_sems, recv_sems = refs[2 * n:]
        x, y, c, chips = _place()
        me = 2 * x + y
        cps = []
        for a in range(n):
            for k, chip in enumerate(chips):
                blk = 2 * chip[0] + chip[1]
                cps.append(pltpu.make_async_remote_copy(
                    src_ref=ins[a].at[blk], dst_ref=outs[a].at[me], send_sem=send_sems.at[a, k],
                    recv_sem=recv_sems.at[a, k], device_id=(*chip, c), device_id_type=MESH))
                cps[-1].start()
        for a in range(n):
            for k, chip in enumerate(chips):
                blk = 2 * chip[0] + chip[1]
                pltpu.make_async_remote_copy(
                    src_ref=ins[a].at[blk], dst_ref=outs[a].at[blk], send_sem=send_sems.at[a, k],
                    recv_sem=recv_sems.at[a, k], device_id=(*chip, c), device_id_type=MESH).wait_recv()
        for cp in cps:
            cp.wait_send()

    return pl.pallas_call(
        body, name="chip_exchange", in_specs=_any_specs(n), out_specs=_any_specs(n),
        out_shape=[jax.ShapeDtypeStruct(p.shape, p.dtype) for p in parts],
        scratch_shapes=[pltpu.SemaphoreType.DMA((n, 3)), pltpu.SemaphoreType.DMA((n, 3))],
        compiler_params=pltpu.CompilerParams(has_side_effects=True),
    )(*parts)


def _half_swap(halves):
    n = len(halves)

    def body(*refs):
        outs = refs[n:2 * n]
        send_sems, recv_sems = refs[2 * n:]
        x, y, c, _ = _place()
        cps = []
        for a in range(n):
            h = outs[a].shape[1] // 2
            mine = outs[a].at[:, pl.ds(c * h, h)]
            cps.append(pltpu.make_async_remote_copy(
                src_ref=mine, dst_ref=mine, send_sem=send_sems.at[a], recv_sem=recv_sems.at[a],
                device_id=(x, y, 1 - c), device_id_type=MESH))
            cps[-1].start()
        for cp in cps:
            cp.wait()

    return pl.pallas_call(
        body, name="half_swap", in_specs=_any_specs(n), out_specs=_any_specs(n),
        out_shape=[jax.ShapeDtypeStruct(p.shape, p.dtype) for p in halves],
        input_output_aliases={a: a for a in range(n)},
        scratch_shapes=[pltpu.SemaphoreType.DMA((n,)), pltpu.SemaphoreType.DMA((n,))],
        compiler_params=pltpu.CompilerParams(has_side_effects=True),
    )(*halves)


N_DEV = 8


def _all_reduce_small(v):
    nrow, ncol = v.shape

    def body(v_ref, o_ref, land, red, send_sems, recv_sems, send2, recv2, loc_sem):
        x, y, c, _ = _place()
        me = 4 * x + 2 * y + c
        peers = []
        for k in range(1, N_DEV):
            peers.append((x ^ ((k >> 2) & 1), y ^ ((k >> 1) & 1), c ^ (k & 1)))
        own = pltpu.make_async_copy(v_ref.at[pl.ds(me, 1)], land.at[pl.ds(me, 1)], loc_sem)
        own.start()
        cps = []
        for k, peer in enumerate(peers):
            dev = 4 * peer[0] + 2 * peer[1] + peer[2]
            cps.append(pltpu.make_async_remote_copy(
                src_ref=v_ref.at[pl.ds(dev, 1)], dst_ref=land.at[pl.ds(me, 1)], send_sem=send_sems.at[k],
                recv_sem=recv_sems.at[k], device_id=peer, device_id_type=MESH))
            cps[-1].start()
        for k, peer in enumerate(peers):
            dev = 4 * peer[0] + 2 * peer[1] + peer[2]
            pltpu.make_async_remote_copy(
                src_ref=v_ref.at[pl.ds(dev, 1)], dst_ref=land.at[pl.ds(dev, 1)], send_sem=send_sems.at[k],
                recv_sem=recv_sems.at[k], device_id=peer, device_id_type=MESH).wait_recv()
        for cp in cps:
            cp.wait_send()
        own.wait()
        terms = land[...]
        total = terms[0:1, :]
        for d in range(1, N_DEV):
            total = total + terms[d:d + 1, :]
        red[...] = total
        own = pltpu.make_async_copy(red, o_ref.at[pl.ds(me, 1)], loc_sem)
        own.start()
        cps = []
        for k, peer in enumerate(peers):
            cps.append(pltpu.make_async_remote_copy(
                src_ref=red, dst_ref=o_ref.at[pl.ds(me, 1)], send_sem=send2.at[k],
                recv_sem=recv2.at[k], device_id=peer, device_id_type=MESH))
            cps[-1].start()
        for k, peer in enumerate(peers):
            dev = 4 * peer[0] + 2 * peer[1] + peer[2]
            pltpu.make_async_remote_copy(
                src_ref=red, dst_ref=o_ref.at[pl.ds(dev, 1)], send_sem=send2.at[k],
                recv_sem=recv2.at[k], device_id=peer, device_id_type=MESH).wait_recv()
        for cp in cps:
            cp.wait_send()
        own.wait()

    vm = pl.BlockSpec(memory_space=pltpu.VMEM)
    return pl.pallas_call(
        body, name="all_reduce_small", in_specs=[vm], out_specs=vm,
        out_shape=jax.ShapeDtypeStruct((nrow, ncol), F32),
        scratch_shapes=[pltpu.VMEM((nrow, ncol), F32), pltpu.VMEM((1, ncol), F32)]
        + [pltpu.SemaphoreType.DMA((N_DEV - 1,))] * 4 + [pltpu.SemaphoreType.DMA],
        compiler_params=pltpu.CompilerParams(has_side_effects=True, vmem_limit_bytes=VMEM_LIMIT),
    )(v)


def _row_tile(rows):
    return min(rows, 512)


def _pair_sum(g, got, place):
    nb, r, c = g.shape
    h = r // 2
    tr = _row_tile(h)
    nt = h // tr

    def body(place_ref, g_ref, got_ref, p_ref, pb_ref):
        p = g_ref[...] + got_ref[...]
        pb_ref[...] = p.astype(BF16)

        @pl.when(pl.program_id(1) == place_ref[0])
        def _():
            p_ref[...] = p

    spec = pl.BlockSpec((None, tr, c), lambda t, j, place_ref: (j, t, 0))
    grid_spec = pltpu.PrefetchScalarGridSpec(
        num_scalar_prefetch=1, grid=(nt, nb),
        in_specs=[pl.BlockSpec((None, tr, c), lambda t, j, place_ref: (j, place_ref[1] * nt + t, 0)), spec],
        out_specs=[pl.BlockSpec((tr, c), lambda t, j, place_ref: (t, 0)), spec])
    return pl.pallas_call(
        body, name="pair_sum", grid_spec=grid_spec,
        out_shape=[jax.ShapeDtypeStruct((h, c), F32), jax.ShapeDtypeStruct((nb, h, c), BF16)],
        compiler_params=_params(("parallel", "arbitrary")),
    )(place, g, got)


def _chip_sum(p, got, place, out, layer):
    h, c = p.shape
    tr = _row_tile(h)
    nt = h // tr

    def body(place_ref, p_ref, g1_ref, g2_ref, g3_ref, old_ref, o_ref):
        o_ref[...] = ((p_ref[...] + g1_ref[...].astype(F32)) + g2_ref[...].astype(F32)) + g3_ref[...].astype(F32)

    def blk(off):
        return pl.BlockSpec((None, tr, c), lambda t, place_ref: ((place_ref[0] + off) % N_CHIPS, t, 0))

    grid_spec = pltpu.PrefetchScalarGridSpec(
        num_scalar_prefetch=1, grid=(nt,),
        in_specs=[pl.BlockSpec((tr, c), lambda t, place_ref: (t, 0)), blk(1), blk(2), blk(3),
                  pl.BlockSpec(memory_space=pl.ANY)],
        out_specs=pl.BlockSpec((None, tr, c), lambda t, place_ref: (layer, place_ref[1] * nt + t, 0)))
    return pl.pallas_call(
        body, name="chip_sum", grid_spec=grid_spec, out_shape=jax.ShapeDtypeStruct(out.shape, F32),
        input_output_aliases={5: 0}, compiler_params=_params(("parallel",)),
    )(place, p, got, got, got, out)


def _adamw(w, g, m, v, pass_g=False):
    r, c = w.shape
    tr = r if r < 8 else _row_tile(r)
    n_out = 4 if pass_g else 3

    def body(w_ref, g_ref, m_ref, v_ref, d_ref, nm_ref, nv_ref, *g_out):
        gv = g_ref[...]
        if pass_g:
            g_out[0][...] = gv
        nm = ADAM_B1 * m_ref[...] + (1.0 - ADAM_B1) * gv
        nv = ADAM_B2 * v_ref[...] + (1.0 - ADAM_B2) * (gv * gv)
        m_hat = nm / (1.0 - ADAM_B1 ** ADAM_STEP)
        v_hat = nv / (1.0 - ADAM_B2 ** ADAM_STEP)
        d_ref[...] = -ADAM_LR * (m_hat / (jnp.sqrt(v_hat) + ADAM_EPS) + ADAM_WD * w_ref[...])
        nm_ref[...] = nm
        nv_ref[...] = nv

    tile = pl.BlockSpec((tr, c), lambda i: (i, 0))
    return pl.pallas_call(
        body, name="adamw", grid=(r // tr,), in_specs=[tile] * 4, out_specs=[tile] * n_out,
        out_shape=[jax.ShapeDtypeStruct((r, c), F32)] * n_out, compiler_params=_params(("parallel",)),
    )(w, g, m, v)


BIG = ("a_w_in", "a_w_out", "sb_w_k", "sb_w_v", "b_w_q", "b_w_o", "ffn_w1", "ffn_w2")
SMALL = ("a_ln_g", "a_ln_b", "a_w_s", "a_b_s", "mix_ln_g", "mix_ln_b", "ffn_ln_g", "ffn_ln_b")
COL_SHARDED = {"a_w_in": True, "a_w_out": False, "sb_w_k": False, "sb_w_v": False, "b_w_q": False, "b_w_o": False,
               "ffn_w1": True, "ffn_w2": False}


def kernel(x, a_w_in, a_ln_g, a_ln_b, a_w_s, a_b_s, a_w_out, sb_w_k, sb_w_v, b_w_q, b_w_o, mix_ln_g, mix_ln_b, ffn_ln_g, ffn_ln_b, ffn_w1, ffn_w2, loss_target, m_a_w_in, m_a_ln_g, m_a_ln_b, m_a_w_s, m_a_b_s, m_a_w_out, m_sb_w_k, m_sb_w_v, m_b_w_q, m_b_w_o, m_mix_ln_g, m_mix_ln_b, m_ffn_ln_g, m_ffn_ln_b, m_ffn_w1, m_ffn_w2, v_a_w_in, v_a_ln_g, v_a_ln_b, v_a_w_s, v_a_b_s, v_a_w_out, v_sb_w_k, v_sb_w_v, v_b_w_q, v_b_w_o, v_mix_ln_g, v_mix_ln_b, v_ffn_ln_g, v_ffn_ln_b, v_ffn_w1, v_ffn_w2):
    names = BIG + SMALL
    given = dict(a_w_in=a_w_in, a_ln_g=a_ln_g, a_ln_b=a_ln_b, a_w_s=a_w_s, a_b_s=a_b_s, a_w_out=a_w_out, sb_w_k=sb_w_k,
                 sb_w_v=sb_w_v, b_w_q=b_w_q, b_w_o=b_w_o, mix_ln_g=mix_ln_g, mix_ln_b=mix_ln_b, ffn_ln_g=ffn_ln_g,
                 ffn_ln_b=ffn_ln_b, ffn_w1=ffn_w1, ffn_w2=ffn_w2)
    mom = dict(a_w_in=m_a_w_in, a_ln_g=m_a_ln_g, a_ln_b=m_a_ln_b, a_w_s=m_a_w_s, a_b_s=m_a_b_s, a_w_out=m_a_w_out,
               sb_w_k=m_sb_w_k, sb_w_v=m_sb_w_v, b_w_q=m_b_w_q, b_w_o=m_b_w_o, mix_ln_g=m_mix_ln_g, mix_ln_b=m_mix_ln_b,
               ffn_ln_g=m_ffn_ln_g, ffn_ln_b=m_ffn_ln_b, ffn_w1=m_ffn_w1, ffn_w2=m_ffn_w2)
    var = dict(a_w_in=v_a_w_in, a_ln_g=v_a_ln_g, a_ln_b=v_a_ln_b, a_w_s=v_a_w_s, a_b_s=v_a_b_s, a_w_out=v_a_w_out,
               sb_w_k=v_sb_w_k, sb_w_v=v_sb_w_v, b_w_q=v_b_w_q, b_w_o=v_b_w_o, mix_ln_g=v_mix_ln_g, mix_ln_b=v_mix_ln_b,
               ffn_ln_g=v_ffn_ln_g, ffn_ln_b=v_ffn_ln_b, ffn_w1=v_ffn_w1, ffn_w2=v_ffn_w2)

    cx, cy, cc = lax.axis_index("x"), lax.axis_index("y"), lax.axis_index("c")
    chip = (2 * cx + cy).astype(jnp.int32)
    chip_arr = chip.reshape(1)

    s, d = x.shape[1], x.shape[2]
    xf = x.reshape(s, d)
    target = loss_target.reshape(s, d)

    def as2d(w):
        return w.reshape(-1, w.shape[-1])

    gw = {}
    for n in BIG:
        for l in ([None] if given[n].ndim == 2 else range(given[n].shape[0])):
            gw[(n, l)] = _cast_into_slot(given[n], l, chip_arr)
    ln_gb = jnp.stack([a_ln_g, a_ln_b])
    ln_slot = lax.dynamic_update_slice(jnp.zeros((N_CHIPS,) + ln_gb.shape, F32), ln_gb[None], (chip, 0, 0, 0))
    layer0 = [("a_w_in", 0), ("a_w_out", 0)]
    gathered = _gather_weights([gw[k] for k in layer0] + [ln_slot])
    gw.update(zip(layer0, gathered[:-1]))
    mixer = {1: [("a_w_in", 1), ("a_w_out", 1)], 2: [("sb_w_k", None), ("sb_w_v", None), ("b_w_q", 0), ("b_w_o", 0)],
             3: [("b_w_q", 1), ("b_w_o", 1)]}

    def riding(d2d=(), ici=()):
        keys = list(d2d) + list(ici)
        return keys, [("d2d", gw[k]) for k in d2d] + [("ici", gw[k]) for k in ici]

    def landed_in(keys, bufs):
        gw.update(zip(keys, bufs))

    ln_full = gathered[-1].transpose(1, 2, 0, 3).reshape(2, N_A, 1, -1)
    a_ln_g3, a_ln_b3 = ln_full[0], ln_full[1]
    mix_g3, mix_b3 = mix_ln_g[:, None, :], mix_ln_b[:, None, :]
    ffn_g3, ffn_b3 = ffn_ln_g[:, None, :], ffn_ln_b[:, None, :]
    bst = jnp.swapaxes(a_b_s, 1, 2)

    saved = []
    xb = _cast_bf16(xf)
    kb = vb = None
    for l in range(DEPTH):
        sv = dict(x_in=xb)
        last = l == DEPTH - 1
        if l == 0:
            keys, riders = riding(ici=[("ffn_w1", 0)])
        else:
            keys, riders = riding(d2d=[("ffn_w1", l), ("ffn_w2", l)], ici=mixer[l + 1] if l < N_A else [])
        if l < N_A:
            h, *bufs = _mm_fwd("a_in", xb, gw[("a_w_in", l)], None, True, riders=riders)
            landed_in(keys, bufs)
            vn = _gmlp_norm_fwd(h, a_ln_g3, a_ln_b3, l)
            gated = _gate_fwd(h, vn, a_w_s[l], bst[l])
            keys, riders = riding(d2d=[("ffn_w1", 0)], ici=[("ffn_w2", 0)]) if l == 0 else ([], [])
            xf, xb, xhat, rstd, *bufs = _mm_resid_ln("a_out", gated, gw[("a_w_out", l)], xf, mix_g3, mix_b3, l, riders)
            landed_in(keys, bufs)
            sv.update(h=h, vn=vn, gated=gated)
        else:
            j = l - N_A
            if l == N_A:
                kb, *bufs = _mm_fwd("sb_k", xb, gw[("sb_w_k", None)], None, False, _ep_bf16, outs=[(d, BF16)],
                                    riders=riders)
                landed_in(keys, bufs)
                keys, riders = [], ()
                vb = _mm_fwd("sb_v", xb, gw[("sb_w_v", None)], None, False, _ep_bf16, outs=[(d, BF16)])[0]
            q, *bufs = _mm_fwd("b_q", xb, gw[("b_w_q", j)], None, False, _ep_scale_q, outs=[(d, BF16)], riders=riders)
            landed_in(keys, bufs)
            ob, lsum = _attn_fwd(q, kb, vb)
            keys, riders = riding(ici=[] if last else mixer[l + 1])
            xf, xb, xhat, rstd, *bufs = _mm_resid_ln("b_out", ob, gw[("b_w_o", j)], xf, mix_g3, mix_b3, l, riders)
            landed_in(keys, bufs)
            sv.update(q=q, lsum=lsum, ob=ob)
        sv.update(x_mid=xb, xhat1=xhat, rstd1=rstd)
        dff = gw[("ffn_w1", l)].shape[-1] * N_CHIPS
        if l == 0:
            keys, riders = riding(d2d=[("ffn_w2", 0)], ici=mixer[1] + [("ffn_w1", 1)])
        else:
            keys, riders = riding(ici=[] if last else [("ffn_w1", l + 1)])
        pr, *bufs = _mm_fwd("ffn_1", xb, gw[("ffn_w1", l)], None, True, _ep_relu, outs=[(dff, BF16)], riders=riders)
        landed_in(keys, bufs)
        keys, riders = riding(d2d=[] if last else mixer[l + 1], ici=[] if last else [("ffn_w2", l + 1)])
        xf, xb, xhat, rstd, *bufs = _mm_resid_ln("ffn_2", pr, gw[("ffn_w2", l)], xf, ffn_g3, ffn_b3, l, riders, _square)
        landed_in(keys, bufs)
        sv.update(pr=pr, xhat2=xhat, rstd2=rstd)
        saved.append(sv)

    dx = xf

    pending = []
    pair_sums, landed = {}, {}
    place_arr = jnp.stack([chip, cc.astype(jnp.int32)])

    def arrived(took, outs):
        for (kind, key, arr), out in zip(took, outs):
            if kind == "pair":
                pair_sums[key] = _pair_sum(arr, out, place_arr)
                pending.append(("chip", key, pair_sums[key][1]))
            else:
                landed[key] = out

    def carrying(call, name, *args, **kw):
        took = []
        if name.startswith("ffn") or draining[0]:
            room = CARRIER_PARAMS
            for task in list(pending):
                size = given[task[1][0]].shape[-2] * given[task[1][0]].shape[-1] * N_CHIPS
                if task[0] == "pair" or room == CARRIER_PARAMS or size <= room:
                    took.append(task)
                    pending.remove(task)
                    room -= size if task[0] == "chip" else 0
        results = call(name, *args, riders=[(kind, arr) for kind, _, arr in took], **kw)
        own = len(results) - len(took)
        arrived(took, results[own:])
        return results[0] if own == 1 else results[:own]

    draining = [False]

    def bwd_act(*args, **kw):
        return carrying(_mm_bwd_act, *args, **kw)

    def bwd_w(key, name, a, dy, **kw):
        pending.append(("pair", key, carrying(_mm_bwd_w, name, a, dy, gw[key], COL_SHARDED[key[0]], **kw)))

    d_mix_g, d_mix_b, d_ffn_g, d_ffn_b = [None] * DEPTH, [None] * DEPTH, [None] * DEPTH, [None] * DEPTH
    d_ln_g, d_ln_b, d_ws, d_bs = [None] * N_A, [None] * N_A, [None] * N_A, [None] * N_A
    dk = dv = normed = None
    for l in reversed(range(DEPTH)):
        sv = saved[l]
        draining[0] = l == 0
        if l == DEPTH - 1:
            dr, drb, d_ffn_g[l], d_ffn_b[l], sq = _ln_bwd(dx, sv["xhat2"], sv["rstd2"], ffn_g3, l, target)
            loss = lax.psum(0.5 * sq[0, 0] / d, ("x", "y", "c"))
        elif normed:
            dr, drb = normed
            normed = None
        else:
            dr, drb, d_ffn_g[l], d_ffn_b[l] = _ln_bwd(dx, sv["xhat2"], sv["rstd2"], ffn_g3, l)
        dff = sv["pr"].shape[1]
        dhd = bwd_act("ffn_2_dx", drb, gw[("ffn_w2", l)], None, False, _ep_relu2_bwd, (sv["pr"],),
                      (pl.BlockSpec((_wide_tile(s), dff // N_CHIPS), lambda j, i, k: (i, j)),), out_dtype=BF16)
        bwd_w(("ffn_w2", l), "ffn_2_dw", sv["pr"], drb, a_fn=_square)
        dr, drb, d_mix_g[l], d_mix_b[l] = bwd_act(
            "ffn_1_dx", dhd, gw[("ffn_w1", l)], None, True, _ep_resid_ln_bwd, (dr, sv["xhat1"], sv["rstd1"], mix_g3),
            (_row_spec(d), _row_spec(d), _row_spec(1), _vec_spec(l, d)), through_norm=True)
        bwd_w(("ffn_w1", l), "ffn_1_dw", sv["x_mid"], dhd)

        quarter = pl.BlockSpec((_wide_tile(s), d // N_CHIPS), lambda j, i, k: (i, j))
        if l < N_A:
            dgated = bwd_act("a_out_dx", drb, gw[("a_w_out", l)], None, False)
            bwd_w(("a_w_out", l), "a_out_dw", sv["gated"], drb)
            dh, d_ws[l], dbs_wide, dlg, dlb = _gmlp_bwd(dgated, sv["h"], sv["vn"], a_w_s[l], bst[l], a_ln_g3, l)
            d_bs[l] = dbs_wide[:, :, 0]
            d_ln_g[l], d_ln_b[l] = dlg[0], dlb[0]
            if l:
                below = saved[l - 1]
                *normed, d_ffn_g[l - 1], d_ffn_b[l - 1] = bwd_act(
                    "a_in_dx", dh, gw[("a_w_in", l)], None, True, _ep_resid_ln_bwd,
                    (dr, below["xhat2"], below["rstd2"], ffn_g3),
                    (_row_spec(d), _row_spec(d), _row_spec(1), _vec_spec(l - 1, d)), through_norm=True)
            else:
                dx = bwd_act("a_in_dx", dh, gw[("a_w_in", l)], None, True, _ep_resid, (dr,), (_row_spec(d),))
            bwd_w(("a_w_in", l), "a_in_dw", sv["x_in"], dh)
        else:
            j = l - N_A
            do = bwd_act("b_out_dx", drb, gw[("b_w_o", j)], None, False)
            bwd_w(("b_w_o", j), "b_out_dw", sv["ob"], drb)
            dq, dk, dv = _attn_bwd(sv["q"], kb, vb, do, sv["lsum"], dk, dv)
            dx = bwd_act("b_q_dx", dq, gw[("b_w_q", j)], None, False, _ep_resid, (dr,), (quarter,))
            bwd_w(("b_w_q", j), "b_q_dw", sv["x_in"], dq)
            if l == N_A:
                dx = bwd_act("sb_k_dx", dk, gw[("sb_w_k", None)], None, False, _ep_add, (dx,), (quarter,))
                bwd_w(("sb_w_k", None), "sb_k_dw", sv["x_in"], dk)
                dx = bwd_act("sb_v_dx", dv, gw[("sb_w_v", None)], None, False, _ep_add, (dx,), (quarter,))
                bwd_w(("sb_w_v", None), "sb_v_dw", sv["x_in"], dv)
    grad_x = dx.reshape(x.shape)

    while pending:
        took = list(pending)
        pending.clear()
        for kind, exchange in (("pair", _pair_exchange), ("chip", _chip_exchange)):
            some = [t for t in took if t[0] == kind]
            if some:
                arrived(some, exchange([arr for _, _, arr in some]))

    stacked = []
    for n in BIG:
        layers = [None] if given[n].ndim == 2 else range(given[n].shape[0])
        out = lax.empty((len(layers),) + given[n].shape[-2:], F32)
        for at, l in enumerate(layers):
            out = _chip_sum(pair_sums[(n, l)][0], landed[(n, l)], place_arr, out, at)
        stacked.append(out)
    grads = {n: g.reshape(given[n].shape) for n, g in zip(BIG, _half_swap(stacked))}

    small_full = dict(a_ln_g=jnp.stack(d_ln_g), a_ln_b=jnp.stack(d_ln_b), a_w_s=jnp.stack(d_ws), a_b_s=jnp.stack(d_bs),
                      mix_ln_g=jnp.concatenate(d_mix_g), mix_ln_b=jnp.concatenate(d_mix_b),
                      ffn_ln_g=jnp.concatenate(d_ffn_g), ffn_ln_b=jnp.concatenate(d_ffn_b))
    packed = jnp.concatenate([small_full[n].reshape(-1) for n in SMALL])
    total = packed.shape[0]
    ncol = -(-total // (N_DEV * LANES)) * LANES
    packed = jnp.pad(packed, (0, N_DEV * ncol - total)).reshape(N_DEV, ncol)
    reduced = _all_reduce_small(packed).reshape(-1)
    off = 0
    for n in SMALL:
        size = small_full[n].size
        g = reduced[off:off + size].reshape(small_full[n].shape)
        off += size
        if n in ("a_ln_g", "a_ln_b"):
            wq = given[n].shape[1]
            g = lax.dynamic_slice_in_dim(g, chip * wq, wq, axis=1)
        grads[n] = g

    delta, new_m, new_v = {}, {}, {}
    for n in names:
        shape = given[n].shape
        dl, nm, nv, *g = _adamw(as2d(given[n]), as2d(grads[n]), as2d(mom[n]), as2d(var[n]), pass_g=n in BIG)
        delta[n], new_m[n], new_v[n] = dl.reshape(shape), nm.reshape(shape), nv.reshape(shape)
        if g:
            grads[n] = g[0].reshape(shape)

    order = ("a_w_in", "a_ln_g", "a_ln_b", "a_w_s", "a_b_s", "a_w_out", "sb_w_k", "sb_w_v", "b_w_q", "b_w_o",
             "mix_ln_g", "mix_ln_b", "ffn_ln_g", "ffn_ln_b", "ffn_w1", "ffn_w2")
    return (loss, grad_x, *[grads[n] for n in order], *[delta[n] for n in order],
            *[new_m[n] for n in order], *[new_v[n] for n in order])
```

```python
import math

import jax
import jax.numpy as jnp
from jax import lax
from jax.experimental import pallas as pl
from jax.experimental.pallas import tpu as pltpu

F32 = jnp.float32
BF16 = jnp.bfloat16
MESH = pl.DeviceIdType.MESH

N_CHIPS = 4
DEPTH = 4
N_A = 2
ALPHA = float((2 * DEPTH) ** 0.25)
LN_EPS = 1e-5
CHUNK = 64
GMLP_BLOCK = 128
GMLP_GROUPS = 8
HEAD_DIM = 64
LANES = 128
ATT_T = 256
ADAM_LR = 0.001
ADAM_B1 = 0.9
ADAM_B2 = 0.999
ADAM_EPS = 1e-08
ADAM_WD = 0.01
ADAM_STEP = 10
VMEM_LIMIT = 56 * 1024 * 1024
TM = 512
TM_WIDE = 1024
TS = 2048

NN = ((1,), (0,))
NT = ((1,), (1,))
TN = ((0,), (0,))


def _params(sem):
    return pltpu.CompilerParams(dimension_semantics=sem, vmem_limit_bytes=VMEM_LIMIT)


def _dot(a, b, contract):
    return lax.dot_general(a, b, (contract, ((), ())), preferred_element_type=F32)


def _rider_out(kind, arr):
    shape = (arr.shape[0], arr.shape[1] // 2, arr.shape[2]) if kind == "pair" else arr.shape
    return jax.ShapeDtypeStruct(shape, arr.dtype)


def _rider_copies(kind, src, dst, send_sems, recv_sems, base):
    x, y, c, chips = _place()
    me = 2 * x + y
    sibling = (x, y, 1 - c)

    def copy(k, part, land, to):
        return pltpu.make_async_remote_copy(src_ref=part, dst_ref=land, send_sem=send_sems.at[base + k],
                                            recv_sem=recv_sems.at[base + k], device_id=to, device_id_type=MESH)

    if kind == "pair":
        h = src.shape[1] // 2
        cp = copy(0, src.at[:, pl.ds((1 - c) * h, h)], dst, sibling)
        return [cp], [cp]
    h = dst.shape[1] // 2
    starts, arrivals = [], []
    for k, chip in enumerate(chips):
        blk = 2 * chip[0] + chip[1]
        if kind == "ici":
            starts.append(copy(k, dst.at[me, pl.ds(c * h, h)], dst.at[me, pl.ds(c * h, h)], (*chip, c)))
            arrivals.append(copy(k, dst.at[blk, pl.ds(c * h, h)], dst.at[blk, pl.ds(c * h, h)], (*chip, c)))
        elif kind == "d2d":
            starts.append(copy(k, dst.at[blk, pl.ds(c * h, h)], dst.at[blk, pl.ds(c * h, h)], sibling))
            arrivals.append(copy(k, dst.at[blk, pl.ds((1 - c) * h, h)], dst.at[blk, pl.ds((1 - c) * h, h)], sibling))
        else:
            starts.append(copy(k, src.at[blk], dst.at[me], (*chip, c)))
            arrivals.append(copy(k, src.at[blk], dst.at[blk], (*chip, c)))
    return starts, arrivals


RIDER_SEMS = 3
CARRIER_PARAMS = 5 * 2 ** 20


def _identity(a):
    return a


def _square(a):
    return a * a


def _matmul(name, operands, in_specs, out_shapes, out_specs, grid, contract, epilogue, acc_shape, aliases=None,
            chunks=None, riders=(), pick=False, a_fn=_identity, sequential=False):
    nk = grid[2]
    n_in, n_out, nr = len(operands), len(out_shapes), len(riders)
    n_plain = n_in + nr + n_out

    def body(*refs):
        ins, outs = refs[:n_in], refs[n_in + nr:n_plain]
        if nr:
            srcs, dsts = refs[n_in:n_in + nr], refs[n_plain:n_plain + nr]
            send_sems, recv_sems = refs[-2:]
            pid = [pl.program_id(ax) for ax in range(3)]
            first = (pid[0] == 0) & (pid[1] == 0) & (pid[2] == 0)
            last = (pid[0] == grid[0] - 1) & (pid[1] == grid[1] - 1) & (pid[2] == grid[2] - 1)

            def copies(n):
                return _rider_copies(riders[n][0], srcs[n], dsts[n], send_sems, recv_sems, RIDER_SEMS * n)

            @pl.when(first)
            def _():
                for n in range(nr):
                    for cp in copies(n)[0]:
                        cp.start()

        compute(refs, ins, outs)
        if nr:
            @pl.when(last)
            def _():
                for n in range(nr):
                    starts, arrivals = copies(n)
                    for cp in arrivals:
                        cp.wait_recv()
                    for cp in starts:
                        cp.wait_send()

    def compute(refs, ins, outs):
        if chunks is None:
            b = ins[1][pl.program_id(1)] if pick else ins[1][...]
            p = _dot(a_fn(ins[0][...].astype(BF16)), b.astype(BF16), contract)
        else:
            width = ins[0].shape[1] // chunks
            p = None
            for j in range(chunks):
                pj = _dot(a_fn(ins[0][:, j * width:(j + 1) * width].astype(BF16)), ins[1][j].astype(BF16), contract)
                p = pj if p is None else p + pj
        if nk == 1:
            epilogue(p, ins[2:], outs)
            return
        acc = refs[n_plain + nr]
        k = pl.program_id(2)

        @pl.when(k == 0)
        def _():
            acc[...] = p

        @pl.when((k > 0) & (k < nk - 1))
        def _():
            acc[...] += p

        @pl.when(k == nk - 1)
        def _():
            epilogue(acc[...] + p, ins[2:], outs)

    rbufs = [b for _, b in riders]
    in_place = {n_in + n: n_out + n for n, (kind, _) in enumerate(riders) if kind in ("ici", "d2d")}
    scratch = ([] if nk == 1 else [pltpu.VMEM(acc_shape, F32)]) \
        + [pltpu.SemaphoreType.DMA((RIDER_SEMS * nr,))] * (2 if nr else 0)
    return pl.pallas_call(
        body, name=name, grid=grid, in_specs=list(in_specs) + _any_specs(nr), out_specs=list(out_specs) + _any_specs(nr),
        out_shape=list(out_shapes) + [_rider_out(kind, b) for kind, b in riders],
        scratch_shapes=scratch,
        input_output_aliases={**(aliases or {}), **in_place},
        compiler_params=_params(("arbitrary",) * 3 if nr or sequential else ("parallel", "parallel", "arbitrary")),
    )(*operands, *rbufs)


def _wspec(w, layer, whole=False):
    r, c = w.shape[-2:]
    lead = N_CHIPS if whole else None
    if w.ndim == 4:
        return pl.BlockSpec((lead, None, r, c), lambda j, i, k: (0 if whole else j, layer, 0, 0))
    return pl.BlockSpec((lead, r, c), lambda j, i, k: (0 if whole else j, 0, 0))


def _wide_tile(s):
    return min(TM_WIDE, s)


def _ep_store(p, ins, outs):
    for o in outs:
        o[...] = p.astype(o.dtype)


def _rows_first(spec):
    return pl.BlockSpec(spec.block_shape, lambda i, j, k: spec.index_map(j, i, k))


def _mm_fwd(name, a, w, layer, col_sharded, epilogue=_ep_store, extras=(), extra_specs=(), outs=None, riders=(),
            a_fn=_identity):
    s = a.shape[0]
    r, c = w.shape[-2:]
    if col_sharded:
        tm = _wide_tile(s)
        grid = (s // tm, N_CHIPS, 1)
        a_spec = pl.BlockSpec((tm, r), lambda j, i, k: (i, 0))
        n_out = N_CHIPS * c
    else:
        tm = TM
        grid = (1, s // tm, 1)
        a_spec = pl.BlockSpec((tm, N_CHIPS * r), lambda j, i, k: (i, 0))
        n_out = c
    if outs is None:
        outs = [(n_out, F32)]
    out_shapes = [jax.ShapeDtypeStruct((s, n), dt) for n, dt in outs]
    out_specs = [pl.BlockSpec((tm, c if n == n_out else n), lambda j, i, k: (i, j)) for n, _ in outs]
    in_specs = [a_spec, _wspec(w, layer, True)] + list(extra_specs)
    if col_sharded:
        in_specs, out_specs = [_rows_first(sp) for sp in in_specs], [_rows_first(sp) for sp in out_specs]
    return _matmul(name, (a, w) + tuple(extras), in_specs, out_shapes, out_specs, grid, NN, epilogue, (tm, c),
                   chunks=None if col_sharded else N_CHIPS, riders=riders, pick=col_sharded, a_fn=a_fn)


def _mm_bwd_act(name, dy, w, layer, col_sharded, epilogue=_ep_store, extras=(), extra_specs=(), out_dtype=F32,
                riders=(), through_norm=False):
    s = dy.shape[0]
    r, c = w.shape[-2:]
    if col_sharded:
        tm = TM
        grid = (1, s // tm, 1)
        a_spec = pl.BlockSpec((tm, N_CHIPS * c), lambda j, i, k: (i, 0))
        n_out = r
    else:
        tm = _wide_tile(s)
        grid = (s // tm, N_CHIPS, 1)
        a_spec = pl.BlockSpec((tm, c), lambda j, i, k: (i, 0))
        n_out = N_CHIPS * r
    in_specs = [a_spec, _wspec(w, layer, True)] + list(extra_specs)
    o_spec = pl.BlockSpec((tm, r), lambda j, i, k: (i, j))
    if not col_sharded:
        in_specs, o_spec = [_rows_first(sp) for sp in in_specs], _rows_first(o_spec)
    out_shapes, out_specs = [jax.ShapeDtypeStruct((s, n_out), out_dtype)], [o_spec]
    if through_norm:
        vec = pl.BlockSpec((1, n_out), lambda j, i, k: (0, 0))
        out_shapes = [jax.ShapeDtypeStruct((s, n_out), F32), jax.ShapeDtypeStruct((s, n_out), BF16),
                      jax.ShapeDtypeStruct((1, n_out), F32), jax.ShapeDtypeStruct((1, n_out), F32)]
        out_specs = [o_spec, o_spec, vec, vec]
    return _matmul(name, (dy, w) + tuple(extras), in_specs, out_shapes, out_specs, grid, NT, epilogue, (tm, r),
                   chunks=N_CHIPS if col_sharded else None, riders=riders, pick=not col_sharded,
                   sequential=through_norm)


def _mm_bwd_w(name, a, dy, w, col_sharded, riders=(), a_fn=_identity):
    s = a.shape[0]
    r, c = w.shape[-2:]
    ts = min(TS, s)
    grid = (N_CHIPS, 1, s // ts)
    if col_sharded:
        a_spec = pl.BlockSpec((ts, r), lambda j, i, k: (k, 0))
        b_spec = pl.BlockSpec((ts, c), lambda j, i, k: (k, j))
    else:
        a_spec = pl.BlockSpec((ts, r), lambda j, i, k: (k, j))
        b_spec = pl.BlockSpec((ts, c), lambda j, i, k: (k, 0))

    def epilogue(p, ins, outs):
        outs[0][...] = p

    return _matmul(name, (a, dy), [a_spec, b_spec], [jax.ShapeDtypeStruct(w.shape, F32)], [_wspec(w, None)], grid, TN,
                   epilogue, (r, c), riders=riders, a_fn=a_fn)


def _row_spec(n):
    return pl.BlockSpec((TM, n), lambda j, i, k: (i, 0))


def _vec_spec(layer, n):
    return pl.BlockSpec((None, 1, n), lambda j, i, k: (layer, 0, 0))


def _ep_resid_ln(p, ins, outs):
    x_ref, g_ref, b_ref = ins
    xf_ref, xb_ref, xhat_ref, rstd_ref = outs
    r = ALPHA * x_ref[...] + p
    mu = jnp.mean(r, axis=-1, keepdims=True)
    d = r - mu
    var = jnp.mean(d * d, axis=-1, keepdims=True)
    rstd = lax.rsqrt(var + LN_EPS)
    xhat = d * rstd
    y = xhat * g_ref[...] + b_ref[...]
    xf_ref[...] = y
    xb_ref[...] = y.astype(BF16)
    xhat_ref[...] = xhat
    rstd_ref[...] = rstd


def _mm_resid_ln(name, a, w, x, g3, b3, ln_layer, riders=(), a_fn=_identity):
    d = x.shape[1]
    return _mm_fwd(name, a, w, None, False, _ep_resid_ln, (x, g3, b3),
                   (_row_spec(d), _vec_spec(ln_layer, d), _vec_spec(ln_layer, d)),
                   outs=[(d, F32), (d, BF16), (d, F32), (1, F32)], riders=riders, a_fn=a_fn)


def _ep_relu(p, ins, outs):
    outs[0][...] = jnp.maximum(p, 0.0).astype(BF16)


def _ep_scale_q(p, ins, outs):
    outs[0][...] = (p * (HEAD_DIM ** -0.5)).astype(BF16)


def _ep_bf16(p, ins, outs):
    outs[0][...] = p.astype(BF16)


def _ep_relu2_bwd(p, ins, outs):
    outs[0][...] = (p * (2.0 * ins[0][...].astype(F32))).astype(BF16)


def _ep_resid(p, ins, outs):
    outs[0][...] = ALPHA * ins[0][...] + p


def _ep_resid_ln_bwd(p, ins, outs):
    dr_ref, xh_ref, rs_ref, g_ref = ins
    _ln_bwd_rows(ALPHA * dr_ref[...] + p, xh_ref, rs_ref, g_ref, pl.program_id(1) == 0, *outs)


def _ep_add(p, ins, outs):
    outs[0][...] = ins[0][...] + p


def _gelu_grad(x):
    c0 = math.sqrt(2.0 / math.pi)
    t = jnp.tanh(c0 * (x + 0.044715 * (x * x * x)))
    return 0.5 * (1.0 + t) + (0.5 * x) * (1.0 - t * t) * (c0 * (1.0 + 3.0 * 0.044715 * (x * x)))


def _cast_bf16(w2d):
    r, c = w2d.shape
    tr = min(r, 512)

    def body(w_ref, o_ref):
        o_ref[...] = w_ref[...].astype(BF16)

    return pl.pallas_call(
        body, name="cast_bf16", grid=(r // tr,),
        in_specs=[pl.BlockSpec((tr, c), lambda i: (i, 0))], out_specs=pl.BlockSpec((tr, c), lambda i: (i, 0)),
        out_shape=jax.ShapeDtypeStruct((r, c), BF16), compiler_params=_params(("parallel",)),
    )(w2d)


def _cast_into_slot(w, layer, chip):
    r, c = w.shape[-2:]
    tr = min(r, 512)

    def body(chip_ref, w_ref, o_ref):
        o_ref[...] = w_ref[...].astype(BF16)

    if layer is None:
        w_spec = pl.BlockSpec((tr, c), lambda i, chip_ref: (i, 0))
    else:
        w_spec = pl.BlockSpec((None, tr, c), lambda i, chip_ref: (layer, i, 0))
    grid_spec = pltpu.PrefetchScalarGridSpec(
        num_scalar_prefetch=1, grid=(r // tr,), in_specs=[w_spec],
        out_specs=pl.BlockSpec((None, tr, c), lambda i, chip_ref: (chip_ref[0], i, 0)))
    return pl.pallas_call(
        body, name="cast_into_slot", grid_spec=grid_spec,
        out_shape=jax.ShapeDtypeStruct((N_CHIPS, r, c), BF16), compiler_params=_params(("parallel",)),
    )(chip, w)


def _gmlp_norm_fwd(h, g3, b3, layer):
    s, w2 = h.shape
    w = w2 // 2

    def body(h_ref, g_ref, b_ref, o_ref):
        z = jax.nn.gelu(h_ref[...])
        mu = jnp.mean(z, axis=-1, keepdims=True)
        d = z - mu
        var = jnp.mean(d * d, axis=-1, keepdims=True)
        o_ref[...] = (d * lax.rsqrt(var + LN_EPS) * g_ref[...] + b_ref[...]).astype(BF16)

    vec = pl.BlockSpec((None, 1, w), lambda i: (layer, 0, 0))
    return pl.pallas_call(
        body, name="gmlp_norm_fwd", grid=(s // TM,),
        in_specs=[pl.BlockSpec((TM, w), lambda i: (i, 1)), vec, vec],
        out_specs=pl.BlockSpec((TM, w), lambda i: (i, 0)),
        out_shape=jax.ShapeDtypeStruct((s, w), BF16), compiler_params=_params(("parallel",)),
    )(h, g3, b3)


def _chunk_mask():
    t = lax.broadcasted_iota(jnp.int32, (GMLP_BLOCK, GMLP_BLOCK), 0)
    s = lax.broadcasted_iota(jnp.int32, (GMLP_BLOCK, GMLP_BLOCK), 1)
    return (s // CHUNK) <= (t // CHUNK)


SG_ROWS = 512


def _gate_fwd(h, vn, ws, bst):
    s, w = vn.shape
    gd = w // GMLP_GROUPS

    def body(h_ref, v_ref, ws_ref, bs_ref, o_ref):
        mask = _chunk_mask()
        for g in range(GMLP_GROUPS):
            wm = jnp.where(mask, ws_ref[g], 0.0).astype(BF16)
            bias = bs_ref[:, g:g + 1]
            cols = slice(g * gd, (g + 1) * gd)
            for n in range(SG_ROWS // GMLP_BLOCK):
                rows = slice(n * GMLP_BLOCK, (n + 1) * GMLP_BLOCK)
                sp = _dot(wm, v_ref[rows, cols], NN) + bias
                o_ref[rows, cols] = (jax.nn.gelu(h_ref[rows, cols]) * sp).astype(BF16)

    return pl.pallas_call(
        body, name="gate_fwd", grid=(s // SG_ROWS,),
        in_specs=[pl.BlockSpec((SG_ROWS, w), lambda i: (i, 0)), pl.BlockSpec((SG_ROWS, w), lambda i: (i, 0)),
                  pl.BlockSpec(ws.shape, lambda i: (0, 0, 0)), pl.BlockSpec(bst.shape, lambda i: (0, 0))],
        out_specs=pl.BlockSpec((SG_ROWS, w), lambda i: (i, 0)),
        out_shape=jax.ShapeDtypeStruct((s, w), BF16), compiler_params=_params(("parallel",)),
    )(h, vn, ws, bst)


GB_ROWS = 256


def _gmlp_bwd(dgated, h, vn, ws, bst, g3, layer):
    s, w = vn.shape
    gd = w // GMLP_GROUPS
    nsteps = s // SG_ROWS

    def body(dg_ref, h_ref, v_ref, ws_ref, bs_ref, g_ref, dh_ref, dws_ref, dbs_ref, dlg_ref, dlb_ref, dsum, du_s, dv_s):
        i = pl.program_id(0)

        @pl.when(i == 0)
        def _():
            dws_ref[...] = jnp.zeros_like(dws_ref)
            dsum[...] = jnp.zeros_like(dsum)
            dlg_ref[...] = jnp.zeros_like(dlg_ref)
            dlb_ref[...] = jnp.zeros_like(dlb_ref)

        mask = _chunk_mask()
        for g in range(GMLP_GROUPS):
            wm = jnp.where(mask, ws_ref[g], 0.0).astype(BF16)
            bias = bs_ref[:, g:g + 1]
            cols = slice(g * gd, (g + 1) * gd)
            dw = jnp.zeros((GMLP_BLOCK, GMLP_BLOCK), F32)
            dsg = jnp.zeros((GMLP_BLOCK, gd), F32)
            for n in range(SG_ROWS // GMLP_BLOCK):
                rows = slice(n * GMLP_BLOCK, (n + 1) * GMLP_BLOCK)
                vb = v_ref[rows, cols]
                sp = _dot(wm, vb, NN) + bias
                dg = dg_ref[rows, cols]
                du_s[rows, cols] = dg * sp
                ds = dg * jax.nn.gelu(h_ref[rows, cols])
                dsb = ds.astype(BF16)
                dw += _dot(dsb, vb, NT)
                dsg += ds
                dv_s[rows, cols] = _dot(wm, dsb, TN)
            dws_ref[g] += dw
            dsum[:, cols] += dsg

        for r0 in range(0, SG_ROWS, GB_ROWS):
            rows = slice(r0, r0 + GB_ROWS)
            hu = h_ref[rows, :w]
            hv = h_ref[rows, w:]
            dh_ref[rows, :w] = (du_s[rows, :] * _gelu_grad(hu)).astype(BF16)
            z = jax.nn.gelu(hv)
            mu = jnp.mean(z, axis=-1, keepdims=True)
            d = z - mu
            var = jnp.mean(d * d, axis=-1, keepdims=True)
            rstd = lax.rsqrt(var + LN_EPS)
            xhat = d * rstd
            dy = dv_s[rows, :]
            dlb_ref[...] += jnp.sum(dy, axis=0, keepdims=True)
            dlg_ref[...] += jnp.sum(dy * xhat, axis=0, keepdims=True)
            dxh = dy * g_ref[...]
            m1 = jnp.mean(dxh, axis=-1, keepdims=True)
            m2 = jnp.mean(dxh * xhat, axis=-1, keepdims=True)
            dz = rstd * (dxh - m1 - xhat * m2)
            dh_ref[rows, w:] = (dz * _gelu_grad(hv)).astype(BF16)

        @pl.when(i == nsteps - 1)
        def _():
            for g in range(GMLP_GROUPS):
                dws_ref[g] = jnp.where(mask, dws_ref[g], 0.0)
                tot = jnp.sum(dsum[:, g * gd:(g + 1) * gd], axis=-1, keepdims=True)
                dbs_ref[g] = jnp.broadcast_to(tot, (GMLP_BLOCK, LANES))

    tile = pl.BlockSpec((SG_ROWS, w), lambda i: (i, 0))
    wide = pl.BlockSpec((SG_ROWS, 2 * w), lambda i: (i, 0))
    vec = pl.BlockSpec((1, w), lambda i: (0, 0))
    return pl.pallas_call(
        body, name="gmlp_bwd", grid=(nsteps,),
        in_specs=[tile, wide, tile, pl.BlockSpec(ws.shape, lambda i: (0, 0, 0)), pl.BlockSpec(bst.shape, lambda i: (0, 0)),
                  pl.BlockSpec((None, 1, w), lambda i: (layer, 0, 0))],
        out_specs=[wide, pl.BlockSpec(ws.shape, lambda i: (0, 0, 0)),
                   pl.BlockSpec((GMLP_GROUPS, GMLP_BLOCK, LANES), lambda i: (0, 0, 0)), vec, vec],
        out_shape=[jax.ShapeDtypeStruct((s, 2 * w), BF16), jax.ShapeDtypeStruct(ws.shape, F32),
                   jax.ShapeDtypeStruct((GMLP_GROUPS, GMLP_BLOCK, LANES), F32),
                   jax.ShapeDtypeStruct((1, w), F32), jax.ShapeDtypeStruct((1, w), F32)],
        scratch_shapes=[pltpu.VMEM((GMLP_BLOCK, w), F32), pltpu.VMEM((SG_ROWS, w), F32), pltpu.VMEM((SG_ROWS, w), F32)],
        compiler_params=_params(("arbitrary",)),
    )(dgated, h, vn, ws, bst, g3)


def _ln_bwd_rows(dy, xh_ref, rs_ref, g_ref, first, dr_ref, drb_ref, dg_ref, db_ref):
    @pl.when(first)
    def _():
        dg_ref[...] = jnp.zeros_like(dg_ref)
        db_ref[...] = jnp.zeros_like(db_ref)

    xhat = xh_ref[...]
    db_ref[...] += jnp.sum(dy, axis=0, keepdims=True)
    dg_ref[...] += jnp.sum(dy * xhat, axis=0, keepdims=True)
    dxh = dy * g_ref[...]
    m1 = jnp.mean(dxh, axis=-1, keepdims=True)
    m2 = jnp.mean(dxh * xhat, axis=-1, keepdims=True)
    dr = rs_ref[...] * (dxh - m1 - xhat * m2)
    dr_ref[...] = dr
    drb_ref[...] = dr.astype(BF16)


def _ln_bwd(dy, xhat, rstd, g3, layer, target=None):
    s, d = dy.shape
    nsteps = s // TM
    head = target is not None

    def body(*refs):
        dy_ref, xh_ref, rs_ref, g_ref = refs[:4]
        dr_ref, drb_ref, dg_ref, db_ref = refs[4 + head:8 + head]
        first = pl.program_id(0) == 0
        dyv = dy_ref[...]
        if head:
            l_ref = refs[8 + head]

            @pl.when(first)
            def _():
                l_ref[...] = jnp.zeros_like(l_ref)

            e = dyv - refs[4][...]
            l_ref[...] += jnp.sum(jnp.sum(e * e, axis=1, keepdims=True), axis=0, keepdims=True)
            dyv = e * (1.0 / d)
        _ln_bwd_rows(dyv, xh_ref, rs_ref, g_ref, first, dr_ref, drb_ref, dg_ref, db_ref)

    tile = pl.BlockSpec((TM, d), lambda i: (i, 0))
    vec = pl.BlockSpec((1, d), lambda i: (0, 0))
    one = pl.BlockSpec((1, 1), lambda i: (0, 0))
    return pl.pallas_call(
        body, name="ln_bwd", grid=(nsteps,),
        in_specs=[tile, tile, pl.BlockSpec((TM, 1), lambda i: (i, 0)), pl.BlockSpec((None, 1, d), lambda i: (layer, 0, 0))]
        + ([tile] if head else []),
        out_specs=[tile, tile, vec, vec] + ([one] if head else []),
        out_shape=[jax.ShapeDtypeStruct((s, d), F32), jax.ShapeDtypeStruct((s, d), BF16),
                   jax.ShapeDtypeStruct((1, d), F32), jax.ShapeDtypeStruct((1, d), F32)]
        + ([jax.ShapeDtypeStruct((1, 1), F32)] if head else []),
        compiler_params=_params(("arbitrary",)),
    )(dy, xhat, rstd, g3, *([target] if head else []))


LOG2E = 1.4426950408889634
DEAD_LOG2 = -160.0
FIRST_LANE = 1


def _sb_terms(z, causal):
    z2 = z * LOG2E
    e = jnp.exp2(-jnp.abs(z2))
    l1p = jnp.log2(1.0 + e)
    lb = jnp.minimum(z2, 0.0) - l1p
    lr = lb - z2
    if causal is not None:
        lr = jnp.where(causal, lr, 0.0)
    return lb, lr, e


def _split_hi_lo(x):
    hi = x.astype(BF16)
    lo = (x - hi.astype(F32)).astype(BF16)
    return jnp.concatenate([hi, lo], axis=1)


def _tri2(prefix):
    r = lax.broadcasted_iota(jnp.int32, (2 * ATT_T, ATT_T), 0) % ATT_T
    c = lax.broadcasted_iota(jnp.int32, (2 * ATT_T, ATT_T), 1)
    return jnp.where((r <= c) if prefix else (r >= c), 1.0, 0.0).astype(BF16)


def _att_masks():
    r = lax.broadcasted_iota(jnp.int32, (ATT_T, ATT_T), 0)
    c = lax.broadcasted_iota(jnp.int32, (ATT_T, ATT_T), 1)
    return c < r, lax.broadcasted_iota(jnp.int32, (1, LANES), 1) < HEAD_DIM


def _attn_fwd(q, k, v):
    s, d = q.shape
    nq = s // ATT_T

    def body(q_ref, k_ref, v_ref, tri_ref, ob_ref, lsum_ref, acc_a, acc_b, rem_a, rem_b):
        i = pl.program_id(1)
        tri = tri_ref[...]
        causal, head_a = _att_masks()
        q2 = q_ref[...]
        zero = jnp.zeros_like(q2)
        qa = jnp.where(head_a, q2, zero)
        qb = jnp.where(head_a, zero, q2)
        acc_a[...] = jnp.zeros_like(acc_a)
        acc_b[...] = jnp.zeros_like(acc_b)
        rem_a[...] = jnp.zeros_like(rem_a)
        rem_b[...] = jnp.zeros_like(rem_b)

        def block(kb, mask):
            rows = pl.ds(pl.multiple_of(kb * ATT_T, ATT_T), ATT_T)
            k2 = k_ref[rows, :]
            v2 = v_ref[rows, :]
            heads = ((qa, acc_a, rem_a), (qb, acc_b, rem_b))
            zs = [_dot(qm, k2, NT) for qm, _, _ in heads]
            terms = [_sb_terms(z, mask) for z in zs]
            sums = [_dot(_split_hi_lo(lr), tri, NN) for _, lr, _ in terms]
            for (_, acc, rem), (lb, lr, _), sincl in zip(heads, terms, sums):
                a = jnp.exp2(lb + (sincl - lr) + rem[...])
                if mask is not None:
                    a = jnp.where(mask, a, 0.0)
                rem[...] += sincl[:, 0:1]
                acc[...] += _dot(a.astype(BF16), v2, NN)

        block(i, causal)

        def live():
            return jnp.maximum(jnp.max(rem_a[...]), jnp.max(rem_b[...])) > DEAD_LOG2

        def go_on(carry):
            t, alive = carry
            return (t < i) & alive

        def step(carry):
            t, _ = carry
            block(i - 1 - t, None)
            return t + 1, live()

        done, _ = lax.while_loop(go_on, step, (jnp.int32(0), live()))
        first = (i - done).astype(F32)
        ob_ref[...] = jnp.where(head_a, acc_a[...], acc_b[...]).astype(BF16)
        lane = lax.broadcasted_iota(jnp.int32, (1, LANES), 1)
        lsum_ref[...] = jnp.where(lane == FIRST_LANE, first, jnp.where(head_a, rem_a[...], rem_b[...]))

    qspec = pl.BlockSpec((ATT_T, LANES), lambda p, i: (i, p))
    kspec = pl.BlockSpec((s, LANES), lambda p, i: (0, p))
    return pl.pallas_call(
        body, name="attn_fwd", grid=(d // LANES, nq),
        in_specs=[qspec, kspec, kspec, pl.BlockSpec((2 * ATT_T, ATT_T), lambda p, i: (0, 0))],
        out_specs=[qspec, qspec],
        out_shape=[jax.ShapeDtypeStruct((s, d), BF16), jax.ShapeDtypeStruct((s, d), F32)],
        scratch_shapes=[pltpu.VMEM((ATT_T, LANES), F32), pltpu.VMEM((ATT_T, LANES), F32),
                        pltpu.VMEM((ATT_T, 1), F32), pltpu.VMEM((ATT_T, 1), F32)],
        compiler_params=_params(("parallel", "arbitrary")),
    )(q, k, v, _tri2(prefix=False))


def _attn_bwd(q, k, v, do, lsum, dk_prev=None, dv_prev=None):
    s, d = q.shape
    nq = s // ATT_T
    has_prev = dk_prev is not None

    def body(*refs):
        q_ref, k_ref, v_ref, do_ref, ls_ref, tri_ref = refs[:6]
        n_in = 8 if has_prev else 6
        dq_ref, dk_ref, dv_ref, acc_a, acc_b, pre_a, pre_b, gp_a, gp_b, dkt, dvt = refs[n_in:]
        i = pl.program_id(1)

        @pl.when(i == 0)
        def _():
            dkt[...] = jnp.zeros_like(dkt)
            dvt[...] = jnp.zeros_like(dvt)

        tri = tri_ref[...]
        causal, head_a = _att_masks()
        q2 = q_ref[...]
        zero = jnp.zeros_like(q2)
        qa = jnp.where(head_a, q2, zero)
        qb = jnp.where(head_a, zero, q2)
        do2 = do_ref[...]
        doa = jnp.where(head_a, do2, 0.0).astype(BF16)
        dob = jnp.where(head_a, 0.0, do2).astype(BF16)
        row_a = lax.broadcasted_iota(jnp.int32, (LANES, 1), 0) < HEAD_DIM
        qt = q2.astype(F32).T
        dot_ = do2.T
        qta, qtb = jnp.where(row_a, qt, 0.0).astype(BF16), jnp.where(row_a, 0.0, qt).astype(BF16)
        dota, dotb = jnp.where(row_a, dot_, 0.0).astype(BF16), jnp.where(row_a, 0.0, dot_).astype(BF16)
        ls2 = ls_ref[...]
        tot_a = ls2[:, 0:1]
        tot_b = ls2[:, HEAD_DIM:HEAD_DIM + 1]
        for r in (acc_a, acc_b, pre_a, pre_b, gp_a, gp_b):
            r[...] = jnp.zeros_like(r)

        def block(kb, mask):
            rows = pl.ds(pl.multiple_of(kb * ATT_T, ATT_T), ATT_T)
            k2 = k_ref[rows, :]
            v2 = v_ref[rows, :]
            dk_new = jnp.zeros((LANES, ATT_T), F32)
            dv_new = jnp.zeros((LANES, ATT_T), F32)
            heads = ((qa, doa, tot_a, acc_a, pre_a, gp_a, qta, dota), (qb, dob, tot_b, acc_b, pre_b, gp_b, qtb, dotb))
            zs = [_dot(h[0], k2, NT) for h in heads]
            das = [_dot(h[1], v2, NT) for h in heads]
            terms = [_sb_terms(z, mask) for z in zs]
            psums = [_dot(_split_hi_lo(lr), tri, NN) for _, lr, _ in terms]
            gs, abs_ = [], []
            for h, (lb, _, _), pincl, da in zip(heads, terms, psums, das):
                tot, pre = h[2], h[4]
                a = jnp.exp2(lb + (tot - (pre[...] + pincl)))
                if mask is not None:
                    a = jnp.where(mask, a, 0.0)
                pre[...] += pincl[:, ATT_T - 1:ATT_T]
                gs.append(a * da)
                abs_.append(a.astype(BF16))
            gsums = [_dot(g.astype(BF16), tri[:ATT_T], NN) for g in gs]
            dzs = []
            for h, z, (_, _, e), g, gincl in zip(heads, zs, terms, gs, gsums):
                gpre = h[5]
                gbefore = gpre[...] + (gincl - g)
                gpre[...] += gincl[:, ATT_T - 1:ATT_T]
                inv = 1.0 / (1.0 + e)
                beta = jnp.where(z >= 0.0, inv, e * inv)
                dz = g - beta * (g + gbefore)
                if mask is not None:
                    dz = jnp.where(mask, dz, 0.0)
                dzs.append(dz.astype(BF16))
            for h, ab, dzb in zip(heads, abs_, dzs):
                dv_new += _dot(h[7], ab, NN)
                dk_new += _dot(h[6], dzb, NN)
                h[3][...] += _dot(dzb, k2, NN)
            cols = pl.ds(pl.multiple_of(kb * ATT_T, ATT_T), ATT_T)
            dkt[:, cols] += dk_new
            dvt[:, cols] += dv_new

        def step(kb, carry):
            block(kb, None)
            return carry

        first = jnp.clip(jnp.max(ls2[:, FIRST_LANE:FIRST_LANE + 1]).astype(jnp.int32), 0, i)
        lax.fori_loop(first, i, step, 0)
        block(i, causal)
        dq_ref[...] = (jnp.where(head_a, acc_a[...], acc_b[...]) * (HEAD_DIM ** -0.5)).astype(BF16)

        @pl.when(i == nq - 1)
        def _():
            for n in range(nq):
                rows = slice(n * ATT_T, (n + 1) * ATT_T)
                dkn, dvn = dkt[:, rows].T, dvt[:, rows].T
                if has_prev:
                    dkn, dvn = dkn + refs[6][rows, :], dvn + refs[7][rows, :]
                dk_ref[rows, :] = dkn
                dv_ref[rows, :] = dvn

    qspec = pl.BlockSpec((ATT_T, LANES), lambda p, i: (i, p))
    kspec = pl.BlockSpec((s, LANES), lambda p, i: (0, p))
    ins = [q, k, v, do, lsum, _tri2(prefix=True)] + ([dk_prev, dv_prev] if has_prev else [])
    return pl.pallas_call(
        body, name="attn_bwd", grid=(d // LANES, nq),
        in_specs=[qspec, kspec, kspec, qspec, qspec, pl.BlockSpec((2 * ATT_T, ATT_T), lambda p, i: (0, 0))]
        + ([kspec, kspec] if has_prev else []),
        out_specs=[qspec, kspec, kspec],
        out_shape=[jax.ShapeDtypeStruct((s, d), BF16), jax.ShapeDtypeStruct((s, d), F32), jax.ShapeDtypeStruct((s, d), F32)],
        scratch_shapes=[pltpu.VMEM((ATT_T, LANES), F32), pltpu.VMEM((ATT_T, LANES), F32)]
        + [pltpu.VMEM((ATT_T, 1), F32)] * 4 + [pltpu.VMEM((LANES, s), F32)] * 2,
        compiler_params=_params(("parallel", "arbitrary")),
    )(*ins)


def _place():
    x, y, c = lax.axis_index("x"), lax.axis_index("y"), lax.axis_index("c")
    chips = [(1 - x, y), (x, 1 - y), (1 - x, 1 - y)]
    return x, y, c, chips


def _any_specs(n):
    return [pl.BlockSpec(memory_space=pl.ANY)] * n


def _gather_weights(bufs):
    n = len(bufs)

    def body(*refs):
        outs = refs[n:2 * n]
        send_sems, recv_sems = refs[2 * n:]
        x, y, c, chips = _place()
        me = 2 * x + y
        sibling = (x, y, 1 - c)

        def half(a, blk, hc):
            h = outs[a].shape[1] // 2
            return outs[a].at[blk, pl.ds(hc * h, h)]

        def copy(a, k, part, to):
            return pltpu.make_async_remote_copy(src_ref=part, dst_ref=part, send_sem=send_sems.at[a, k],
                                                recv_sem=recv_sems.at[a, k], device_id=to, device_id_type=MESH)

        sent = []
        for a in range(n):
            for k, chip in enumerate(chips):
                sent.append(copy(a, k, half(a, me, c), (*chip, c)))
                sent[-1].start()
        for a in range(n):
            for k, chip in enumerate(chips):
                blk = 2 * chip[0] + chip[1]
                copy(a, k, half(a, blk, c), sibling).wait_recv()
                sent.append(copy(a, 3 + k, half(a, blk, c), sibling))
                sent[-1].start()
        for a in range(n):
            for k, chip in enumerate(chips):
                blk = 2 * chip[0] + chip[1]
                copy(a, 3 + k, half(a, blk, 1 - c), sibling).wait_recv()
        for cp in sent:
            cp.wait_send()

    return pl.pallas_call(
        body, name="gather_weights", in_specs=_any_specs(n), out_specs=_any_specs(n),
        out_shape=[jax.ShapeDtypeStruct(w.shape, w.dtype) for w in bufs],
        input_output_aliases={a: a for a in range(n)},
        scratch_shapes=[pltpu.SemaphoreType.DMA((n, 6)), pltpu.SemaphoreType.DMA((n, 6))],
        compiler_params=pltpu.CompilerParams(has_side_effects=True),
    )(*bufs)


def _pair_exchange(grads):
    n = len(grads)

    def body(*refs):
        ins, outs = refs[:n], refs[n:2 * n]
        send_sems, recv_sems = refs[2 * n:]
        x, y, c, _ = _place()
        cps = []
        for a in range(n):
            h = ins[a].shape[1] // 2
            cps.append(pltpu.make_async_remote_copy(
                src_ref=ins[a].at[:, pl.ds((1 - c) * h, h)], dst_ref=outs[a], send_sem=send_sems.at[a],
                recv_sem=recv_sems.at[a], device_id=(x, y, 1 - c), device_id_type=MESH))
            cps[-1].start()
        for cp in cps:
            cp.wait()

    return pl.pallas_call(
        body, name="pair_exchange", in_specs=_any_specs(n), out_specs=_any_specs(n),
        out_shape=[jax.ShapeDtypeStruct((g.shape[0], g.shape[1] // 2, g.shape[2]), g.dtype) for g in grads],
        scratch_shapes=[pltpu.SemaphoreType.DMA((n,)), pltpu.SemaphoreType.DMA((n,))],
        compiler_params=pltpu.CompilerParams(has_side_effects=True),
    )(*grads)


def _chip_exchange(parts):
    n = len(parts)

    def body(*refs):
        ins, outs = refs[:n], refs[n:2 * n]
        send_sems, recv_sems = refs[2 * n:]
        x, y, c, chips = _place()
        me = 2 * x + y
        cps = []
        for a in range(n):
            for k, chip in enumerate(chips):
                blk = 2 * chip[0] + chip[1]
                cps.append(pltpu.make_async_remote_copy(
                    src_ref=ins[a].at[blk], dst_ref=outs[a].at[me], send_sem=send_sems.at[a, k],
                    recv_sem=recv_sems.at[a, k], device_id=(*chip, c), device_id_type=MESH))
                cps[-1].start()
        for a in range(n):
            for k, chip in enumerate(chips):
                blk = 2 * chip[0] + chip[1]
                pltpu.make_async_remote_copy(
                    src_ref=ins[a].at[blk], dst_ref=outs[a].at[blk], send_sem=send_sems.at[a, k],
                    recv_sem=recv_sems.at[a, k], device_id=(*chip, c), device_id_type=MESH).wait_recv()
        for cp in cps:
            cp.wait_send()

    return pl.pallas_call(
        body, name="chip_exchange", in_specs=_any_specs(n), out_specs=_any_specs(n),
        out_shape=[jax.ShapeDtypeStruct(p.shape, p.dtype) for p in parts],
        scratch_shapes=[pltpu.SemaphoreType.DMA((n, 3)), pltpu.SemaphoreType.DMA((n, 3))],
        compiler_params=pltpu.CompilerParams(has_side_effects=True),
    )(*parts)


def _half_swap(halves):
    n = len(halves)

    def body(*refs):
        outs = refs[n:2 * n]
        send_sems, recv_sems = refs[2 * n:]
        x, y, c, _ = _place()
        cps = []
        for a in range(n):
            h = outs[a].shape[1] // 2
            mine = outs[a].at[:, pl.ds(c * h, h)]
            cps.append(pltpu.make_async_remote_copy(
                src_ref=mine, dst_ref=mine, send_sem=send_sems.at[a], recv_sem=recv_sems.at[a],
                device_id=(x, y, 1 - c), device_id_type=MESH))
            cps[-1].start()
        for cp in cps:
            cp.wait()

    return pl.pallas_call(
        body, name="half_swap", in_specs=_any_specs(n), out_specs=_any_specs(n),
        out_shape=[jax.ShapeDtypeStruct(p.shape, p.dtype) for p in halves],
        input_output_aliases={a: a for a in range(n)},
        scratch_shapes=[pltpu.SemaphoreType.DMA((n,)), pltpu.SemaphoreType.DMA((n,))],
        compiler_params=pltpu.CompilerParams(has_side_effects=True),
    )(*halves)


N_DEV = 8


def _all_reduce_small(v):
    nrow, ncol = v.shape

    def body(v_ref, o_ref, land, red, send_sems, recv_sems, send2, recv2, loc_sem):
        x, y, c, _ = _place()
        me = 4 * x + 2 * y + c
        peers = []
        for k in range(1, N_DEV):
            peers.append((x ^ ((k >> 2) & 1), y ^ ((k >> 1) & 1), c ^ (k & 1)))
        own = pltpu.make_async_copy(v_ref.at[pl.ds(me, 1)], land.at[pl.ds(me, 1)], loc_sem)
        own.start()
        cps = []
        for k, peer in enumerate(peers):
            dev = 4 * peer[0] + 2 * peer[1] + peer[2]
            cps.append(pltpu.make_async_remote_copy(
                src_ref=v_ref.at[pl.ds(dev, 1)], dst_ref=land.at[pl.ds(me, 1)], send_sem=send_sems.at[k],
                recv_sem=recv_sems.at[k], device_id=peer, device_id_type=MESH))
            cps[-1].start()
        for k, peer in enumerate(peers):
            dev = 4 * peer[0] + 2 * peer[1] + peer[2]
            pltpu.make_async_remote_copy(
                src_ref=v_ref.at[pl.ds(dev, 1)], dst_ref=land.at[pl.ds(dev, 1)], send_sem=send_sems.at[k],
                recv_sem=recv_sems.at[k], device_id=peer, device_id_type=MESH).wait_recv()
        for cp in cps:
            cp.wait_send()
        own.wait()
        terms = land[...]
        total = terms[0:1, :]
        for d in range(1, N_DEV):
            total = total + terms[d:d + 1, :]
        red[...] = total
        own = pltpu.make_async_copy(red, o_ref.at[pl.ds(me, 1)], loc_sem)
        own.start()
        cps = []
        for k, peer in enumerate(peers):
            cps.append(pltpu.make_async_remote_copy(
                src_ref=red, dst_ref=o_ref.at[pl.ds(me, 1)], send_sem=send2.at[k],
                recv_sem=recv2.at[k], device_id=peer, device_id_type=MESH))
            cps[-1].start()
        for k, peer in enumerate(peers):
            dev = 4 * peer[0] + 2 * peer[1] + peer[2]
            pltpu.make_async_remote_copy(
                src_ref=red, dst_ref=o_ref.at[pl.ds(dev, 1)], send_sem=send2.at[k],
                recv_sem=recv2.at[k], device_id=peer, device_id_type=MESH).wait_recv()
        for cp in cps:
            cp.wait_send()
        own.wait()

    vm = pl.BlockSpec(memory_space=pltpu.VMEM)
    return pl.pallas_call(
        body, name="all_reduce_small", in_specs=[vm], out_specs=vm,
        out_shape=jax.ShapeDtypeStruct((nrow, ncol), F32),
        scratch_shapes=[pltpu.VMEM((nrow, ncol), F32), pltpu.VMEM((1, ncol), F32)]
        + [pltpu.SemaphoreType.DMA((N_DEV - 1,))] * 4 + [pltpu.SemaphoreType.DMA],
        compiler_params=pltpu.CompilerParams(has_side_effects=True, vmem_limit_bytes=VMEM_LIMIT),
    )(v)


def _row_tile(rows):
    return min(rows, 512)


def _pair_sum(g, got, place):
    nb, r, c = g.shape
    h = r // 2
    tr = _row_tile(h)
    nt = h // tr

    def body(place_ref, g_ref, got_ref, p_ref, pb_ref):
        p = g_ref[...] + got_ref[...]
        pb_ref[...] = p.astype(BF16)

        @pl.when(pl.program_id(1) == place_ref[0])
        def _():
            p_ref[...] = p

    spec = pl.BlockSpec((None, tr, c), lambda t, j, place_ref: (j, t, 0))
    grid_spec = pltpu.PrefetchScalarGridSpec(
        num_scalar_prefetch=1, grid=(nt, nb),
        in_specs=[pl.BlockSpec((None, tr, c), lambda t, j, place_ref: (j, place_ref[1] * nt + t, 0)), spec],
        out_specs=[pl.BlockSpec((tr, c), lambda t, j, place_ref: (t, 0)), spec])
    return pl.pallas_call(
        body, name="pair_sum", grid_spec=grid_spec,
        out_shape=[jax.ShapeDtypeStruct((h, c), F32), jax.ShapeDtypeStruct((nb, h, c), BF16)],
        compiler_params=_params(("parallel", "arbitrary")),
    )(place, g, got)


def _chip_sum(p, got, place, out, layer):
    h, c = p.shape
    tr = _row_tile(h)
    nt = h // tr

    def body(place_ref, p_ref, g1_ref, g2_ref, g3_ref, old_ref, o_ref):
        o_ref[...] = ((p_ref[...] + g1_ref[...].astype(F32)) + g2_ref[...].astype(F32)) + g3_ref[...].astype(F32)

    def blk(off):
        return pl.BlockSpec((None, tr, c), lambda t, place_ref: ((place_ref[0] + off) % N_CHIPS, t, 0))

    grid_spec = pltpu.PrefetchScalarGridSpec(
        num_scalar_prefetch=1, grid=(nt,),
        in_specs=[pl.BlockSpec((tr, c), lambda t, place_ref: (t, 0)), blk(1), blk(2), blk(3),
                  pl.BlockSpec(memory_space=pl.ANY)],
        out_specs=pl.BlockSpec((None, tr, c), lambda t, place_ref: (layer, place_ref[1] * nt + t, 0)))
    return pl.pallas_call(
        body, name="chip_sum", grid_spec=grid_spec, out_shape=jax.ShapeDtypeStruct(out.shape, F32),
        input_output_aliases={5: 0}, compiler_params=_params(("parallel",)),
    )(place, p, got, got, got, out)


def _adamw(w, g, m, v, pass_g=False):
    r, c = w.shape
    tr = r if r < 8 else _row_tile(r)
    n_out = 4 if pass_g else 3

    def body(w_ref, g_ref, m_ref, v_ref, d_ref, nm_ref, nv_ref, *g_out):
        gv = g_ref[...]
        if pass_g:
            g_out[0][...] = gv
        nm = ADAM_B1 * m_ref[...] + (1.0 - ADAM_B1) * gv
        nv = ADAM_B2 * v_ref[...] + (1.0 - ADAM_B2) * (gv * gv)
        m_hat = nm / (1.0 - ADAM_B1 ** ADAM_STEP)
        v_hat = nv / (1.0 - ADAM_B2 ** ADAM_STEP)
        d_ref[...] = -ADAM_LR * (m_hat / (jnp.sqrt(v_hat) + ADAM_EPS) + ADAM_WD * w_ref[...])
        nm_ref[...] = nm
        nv_ref[...] = nv

    tile = pl.BlockSpec((tr, c), lambda i: (i, 0))
    return pl.pallas_call(
        body, name="adamw", grid=(r // tr,), in_specs=[tile] * 4, out_specs=[tile] * n_out,
        out_shape=[jax.ShapeDtypeStruct((r, c), F32)] * n_out, compiler_params=_params(("parallel",)),
    )(w, g, m, v)


BIG = ("a_w_in", "a_w_out", "sb_w_k", "sb_w_v", "b_w_q", "b_w_o", "ffn_w1", "ffn_w2")
SMALL = ("a_ln_g", "a_ln_b", "a_w_s", "a_b_s", "mix_ln_g", "mix_ln_b", "ffn_ln_g", "ffn_ln_b")
COL_SHARDED = {"a_w_in": True, "a_w_out": False, "sb_w_k": False, "sb_w_v": False, "b_w_q": False, "b_w_o": False,
               "ffn_w1": True, "ffn_w2": False}


def kernel(x, a_w_in, a_ln_g, a_ln_b, a_w_s, a_b_s, a_w_out, sb_w_k, sb_w_v, b_w_q, b_w_o, mix_ln_g, mix_ln_b, ffn_ln_g, ffn_ln_b, ffn_w1, ffn_w2, loss_target, m_a_w_in, m_a_ln_g, m_a_ln_b, m_a_w_s, m_a_b_s, m_a_w_out, m_sb_w_k, m_sb_w_v, m_b_w_q, m_b_w_o, m_mix_ln_g, m_mix_ln_b, m_ffn_ln_g, m_ffn_ln_b, m_ffn_w1, m_ffn_w2, v_a_w_in, v_a_ln_g, v_a_ln_b, v_a_w_s, v_a_b_s, v_a_w_out, v_sb_w_k, v_sb_w_v, v_b_w_q, v_b_w_o, v_mix_ln_g, v_mix_ln_b, v_ffn_ln_g, v_ffn_ln_b, v_ffn_w1, v_ffn_w2):
    names = BIG + SMALL
    given = dict(a_w_in=a_w_in, a_ln_g=a_ln_g, a_ln_b=a_ln_b, a_w_s=a_w_s, a_b_s=a_b_s, a_w_out=a_w_out, sb_w_k=sb_w_k,
                 sb_w_v=sb_w_v, b_w_q=b_w_q, b_w_o=b_w_o, mix_ln_g=mix_ln_g, mix_ln_b=mix_ln_b, ffn_ln_g=ffn_ln_g,
                 ffn_ln_b=ffn_ln_b, ffn_w1=ffn_w1, ffn_w2=ffn_w2)
    mom = dict(a_w_in=m_a_w_in, a_ln_g=m_a_ln_g, a_ln_b=m_a_ln_b, a_w_s=m_a_w_s, a_b_s=m_a_b_s, a_w_out=m_a_w_out,
               sb_w_k=m_sb_w_k, sb_w_v=m_sb_w_v, b_w_q=m_b_w_q, b_w_o=m_b_w_o, mix_ln_g=m_mix_ln_g, mix_ln_b=m_mix_ln_b,
               ffn_ln_g=m_ffn_ln_g, ffn_ln_b=m_ffn_ln_b, ffn_w1=m_ffn_w1, ffn_w2=m_ffn_w2)
    var = dict(a_w_in=v_a_w_in, a_ln_g=v_a_ln_g, a_ln_b=v_a_ln_b, a_w_s=v_a_w_s, a_b_s=v_a_b_s, a_w_out=v_a_w_out,
               sb_w_k=v_sb_w_k, sb_w_v=v_sb_w_v, b_w_q=v_b_w_q, b_w_o=v_b_w_o, mix_ln_g=v_mix_ln_g, mix_ln_b=v_mix_ln_b,
               ffn_ln_g=v_ffn_ln_g, ffn_ln_b=v_ffn_ln_b, ffn_w1=v_ffn_w1, ffn_w2=v_ffn_w2)

    cx, cy, cc = lax.axis_index("x"), lax.axis_index("y"), lax.axis_index("c")
    chip = (2 * cx + cy).astype(jnp.int32)
    chip_arr = chip.reshape(1)

    s, d = x.shape[1], x.shape[2]
    xf = x.reshape(s, d)
    target = loss_target.reshape(s, d)

    def as2d(w):
        return w.reshape(-1, w.shape[-1])

    gw = {}
    for n in BIG:
        for l in ([None] if given[n].ndim == 2 else range(given[n].shape[0])):
            gw[(n, l)] = _cast_into_slot(given[n], l, chip_arr)
    ln_gb = jnp.stack([a_ln_g, a_ln_b])
    ln_slot = lax.dynamic_update_slice(jnp.zeros((N_CHIPS,) + ln_gb.shape, F32), ln_gb[None], (chip, 0, 0, 0))
    layer0 = [("a_w_in", 0), ("a_w_out", 0)]
    gathered = _gather_weights([gw[k] for k in layer0] + [ln_slot])
    gw.update(zip(layer0, gathered[:-1]))
    mixer = {1: [("a_w_in", 1), ("a_w_out", 1)], 2: [("sb_w_k", None), ("sb_w_v", None), ("b_w_q", 0), ("b_w_o", 0)],
             3: [("b_w_q", 1), ("b_w_o", 1)]}

    def riding(d2d=(), ici=()):
        keys = list(d2d) + list(ici)
        return keys, [("d2d", gw[k]) for k in d2d] + [("ici", gw[k]) for k in ici]

    def landed_in(keys, bufs):
        gw.update(zip(keys, bufs))

    ln_full = gathered[-1].transpose(1, 2, 0, 3).reshape(2, N_A, 1, -1)
    a_ln_g3, a_ln_b3 = ln_full[0], ln_full[1]
    mix_g3, mix_b3 = mix_ln_g[:, None, :], mix_ln_b[:, None, :]
    ffn_g3, ffn_b3 = ffn_ln_g[:, None, :], ffn_ln_b[:, None, :]
    bst = jnp.swapaxes(a_b_s, 1, 2)

    saved = []
    xb = _cast_bf16(xf)
    kb = vb = None
    for l in range(DEPTH):
        sv = dict(x_in=xb)
        last = l == DEPTH - 1
        if l == 0:
            keys, riders = riding(ici=[("ffn_w1", 0)])
        else:
            keys, riders = riding(d2d=[("ffn_w1", l), ("ffn_w2", l)], ici=mixer[l + 1][:2] if l < N_A else [])
        if l < N_A:
            h, *bufs = _mm_fwd("a_in", xb, gw[("a_w_in", l)], None, True, riders=riders)
            landed_in(keys, bufs)
            vn = _gmlp_norm_fwd(h, a_ln_g3, a_ln_b3, l)
            gated = _gate_fwd(h, vn, a_w_s[l], bst[l])
            keys, riders = riding(d2d=[("ffn_w1", 0)], ici=[("ffn_w2", 0)]) if l == 0 else riding(ici=mixer[l + 1][2:])
            xf, xb, xhat, rstd, *bufs = _mm_resid_ln("a_out", gated, gw[("a_w_out", l)], xf, mix_g3, mix_b3, l, riders)
            landed_in(keys, bufs)
            sv.update(h=h, vn=vn, gated=gated)
        else:
            j = l - N_A
            if l == N_A:
                kb, *bufs = _mm_fwd("sb_k", xb, gw[("sb_w_k", None)], None, False, _ep_bf16, outs=[(d, BF16)],
                                    riders=riders)
                landed_in(keys, bufs)
                keys, riders = [], ()
                vb = _mm_fwd("sb_v", xb, gw[("sb_w_v", None)], None, False, _ep_bf16, outs=[(d, BF16)])[0]
            q, *bufs = _mm_fwd("b_q", xb, gw[("b_w_q", j)], None, False, _ep_scale_q, outs=[(d, BF16)], riders=riders)
            landed_in(keys, bufs)
            ob, lsum = _attn_fwd(q, kb, vb)
            keys, riders = riding(ici=[] if last else mixer[l + 1])
            xf, xb, xhat, rstd, *bufs = _mm_resid_ln("b_out", ob, gw[("b_w_o", j)], xf, mix_g3, mix_b3, l, riders)
            landed_in(keys, bufs)
            sv.update(q=q, lsum=lsum, ob=ob)
        sv.update(x_mid=xb, xhat1=xhat, rstd1=rstd)
        dff = gw[("ffn_w1", l)].shape[-1] * N_CHIPS
        if l == 0:
            keys, riders = riding(d2d=[("ffn_w2", 0)], ici=mixer[1] + [("ffn_w1", 1)])
        else:
            keys, riders = riding(ici=[] if last else [("ffn_w1", l + 1)])
        pr, *bufs = _mm_fwd("ffn_1", xb, gw[("ffn_w1", l)], None, True, _ep_relu, outs=[(dff, BF16)], riders=riders)
        landed_in(keys, bufs)
        keys, riders = riding(d2d=[] if last else mixer[l + 1], ici=[] if last else [("ffn_w2", l + 1)])
        xf, xb, xhat, rstd, *bufs = _mm_resid_ln("ffn_2", pr, gw[("ffn_w2", l)], xf, ffn_g3, ffn_b3, l, riders, _square)
        landed_in(keys, bufs)
        sv.update(pr=pr, xhat2=xhat, rstd2=rstd)
        saved.append(sv)

    dx = xf

    pending = []
    pair_sums, landed = {}, {}
    place_arr = jnp.stack([chip, cc.astype(jnp.int32)])

    def arrived(took, outs):
        for (kind, key, arr), out in zip(took, outs):
            if kind == "pair":
                pair_sums[key] = _pair_sum(arr, out, place_arr)
                pending.append(("chip", key, pair_sums[key][1]))
            else:
                landed[key] = out

    def carrying(call, name, *args, **kw):
        took = []
        if name.startswith("ffn") or draining[0]:
            room = CARRIER_PARAMS
            for task in list(pending):
                size = given[task[1][0]].shape[-2] * given[task[1][0]].shape[-1] * N_CHIPS
                if task[0] == "pair" or room == CARRIER_PARAMS or size <= room:
                    took.append(task)
                    pending.remove(task)
                    room -= size if task[0] == "chip" else 0
        results = call(name, *args, riders=[(kind, arr) for kind, _, arr in took], **kw)
        own = len(results) - len(took)
        arrived(took, results[own:])
        return results[0] if own == 1 else results[:own]

    draining = [False]

    def bwd_act(*args, **kw):
        return carrying(_mm_bwd_act, *args, **kw)

    def bwd_w(key, name, a, dy, **kw):
        pending.append(("pair", key, carrying(_mm_bwd_w, name, a, dy, gw[key], COL_SHARDED[key[0]], **kw)))

    d_mix_g, d_mix_b, d_ffn_g, d_ffn_b = [None] * DEPTH, [None] * DEPTH, [None] * DEPTH, [None] * DEPTH
    d_ln_g, d_ln_b, d_ws, d_bs = [None] * N_A, [None] * N_A, [None] * N_A, [None] * N_A
    dk = dv = normed = None
    for l in reversed(range(DEPTH)):
        sv = saved[l]
        draining[0] = l == 0
        if l == DEPTH - 1:
            dr, drb, d_ffn_g[l], d_ffn_b[l], sq = _ln_bwd(dx, sv["xhat2"], sv["rstd2"], ffn_g3, l, target)
            loss = lax.psum(0.5 * sq[0, 0] / d, ("x", "y", "c"))
        elif normed:
            dr, drb = normed
            normed = None
        else:
            dr, drb, d_ffn_g[l], d_ffn_b[l] = _ln_bwd(dx, sv["xhat2"], sv["rstd2"], ffn_g3, l)
        dff = sv["pr"].shape[1]
        dhd = bwd_act("ffn_2_dx", drb, gw[("ffn_w2", l)], None, False, _ep_relu2_bwd, (sv["pr"],),
                      (pl.BlockSpec((_wide_tile(s), dff // N_CHIPS), lambda j, i, k: (i, j)),), out_dtype=BF16)
        bwd_w(("ffn_w2", l), "ffn_2_dw", sv["pr"], drb, a_fn=_square)
        dr, drb, d_mix_g[l], d_mix_b[l] = bwd_act(
            "ffn_1_dx", dhd, gw[("ffn_w1", l)], None, True, _ep_resid_ln_bwd, (dr, sv["xhat1"], sv["rstd1"], mix_g3),
            (_row_spec(d), _row_spec(d), _row_spec(1), _vec_spec(l, d)), through_norm=True)
        bwd_w(("ffn_w1", l), "ffn_1_dw", sv["x_mid"], dhd)

        quarter = pl.BlockSpec((_wide_tile(s), d // N_CHIPS), lambda j, i, k: (i, j))
        if l < N_A:
            dgated = bwd_act("a_out_dx", drb, gw[("a_w_out", l)], None, False)
            bwd_w(("a_w_out", l), "a_out_dw", sv["gated"], drb)
            dh, d_ws[l], dbs_wide, dlg, dlb = _gmlp_bwd(dgated, sv["h"], sv["vn"], a_w_s[l], bst[l], a_ln_g3, l)
            d_bs[l] = dbs_wide[:, :, 0]
            d_ln_g[l], d_ln_b[l] = dlg[0], dlb[0]
            if l:
                below = saved[l - 1]
                *normed, d_ffn_g[l - 1], d_ffn_b[l - 1] = bwd_act(
                    "a_in_dx", dh, gw[("a_w_in", l)], None, True, _ep_resid_ln_bwd,
                    (dr, below["xhat2"], below["rstd2"], ffn_g3),
                    (_row_spec(d), _row_spec(d), _row_spec(1), _vec_spec(l - 1, d)), through_norm=True)
            else:
                dx = bwd_act("a_in_dx", dh, gw[("a_w_in", l)], None, True, _ep_resid, (dr,), (_row_spec(d),))
            bwd_w(("a_w_in", l), "a_in_dw", sv["x_in"], dh)
        else:
            j = l - N_A
            do = bwd_act("b_out_dx", drb, gw[("b_w_o", j)], None, False)
            bwd_w(("b_w_o", j), "b_out_dw", sv["ob"], drb)
            dq, dk, dv = _attn_bwd(sv["q"], kb, vb, do, sv["lsum"], dk, dv)
            dx = bwd_act("b_q_dx", dq, gw[("b_w_q", j)], None, False, _ep_resid, (dr,), (quarter,))
            bwd_w(("b_w_q", j), "b_q_dw", sv["x_in"], dq)
            if l == N_A:
                dx = bwd_act("sb_k_dx", dk, gw[("sb_w_k", None)], None, False, _ep_add, (dx,), (quarter,))
                bwd_w(("sb_w_k", None), "sb_k_dw", sv["x_in"], dk)
                dx = bwd_act("sb_v_dx", dv, gw[("sb_w_v", None)], None, False, _ep_add, (dx,), (quarter,))
                bwd_w(("sb_w_v", None), "sb_v_dw", sv["x_in"], dv)
    grad_x = dx.reshape(x.shape)

    while pending:
        took = list(pending)
        pending.clear()
        for kind, exchange in (("pair", _pair_exchange), ("chip", _chip_exchange)):
            some = [t for t in took if t[0] == kind]
            if some:
                arrived(some, exchange([arr for _, _, arr in some]))

    stacked = []
    for n in BIG:
        layers = [None] if given[n].ndim == 2 else range(given[n].shape[0])
        out = lax.empty((len(layers),) + given[n].shape[-2:], F32)
        for at, l in enumerate(layers):
            out = _chip_sum(pair_sums[(n, l)][0], landed[(n, l)], place_arr, out, at)
        stacked.append(out)
    grads = {n: g.reshape(given[n].shape) for n, g in zip(BIG, _half_swap(stacked))}

    small_full = dict(a_ln_g=jnp.stack(d_ln_g), a_ln_b=jnp.stack(d_ln_b), a_w_s=jnp.stack(d_ws), a_b_s=jnp.stack(d_bs),
                      mix_ln_g=jnp.concatenate(d_mix_g), mix_ln_b=jnp.concatenate(d_mix_b),
                      ffn_ln_g=jnp.concatenate(d_ffn_g), ffn_ln_b=jnp.concatenate(d_ffn_b))
    packed = jnp.concatenate([small_full[n].reshape(-1) for n in SMALL])
    total = packed.shape[0]
    ncol = -(-total // (N_DEV * LANES)) * LANES
    packed = jnp.pad(packed, (0, N_DEV * ncol - total)).reshape(N_DEV, ncol)
    reduced = _all_reduce_small(packed).reshape(-1)
    off = 0
    for n in SMALL:
        size = small_full[n].size
        g = reduced[off:off + size].reshape(small_full[n].shape)
        off += size
        if n in ("a_ln_g", "a_ln_b"):
            wq = given[n].shape[1]
            g = lax.dynamic_slice_in_dim(g, chip * wq, wq, axis=1)
        grads[n] = g

    delta, new_m, new_v = {}, {}, {}
    for n in names:
        shape = given[n].shape
        dl, nm, nv, *g = _adamw(as2d(given[n]), as2d(grads[n]), as2d(mom[n]), as2d(var[n]), pass_g=n in BIG)
        delta[n], new_m[n], new_v[n] = dl.reshape(shape), nm.reshape(shape), nv.reshape(shape)
        if g:
            grads[n] = g[0].reshape(shape)

    order = ("a_w_in", "a_ln_g", "a_ln_b", "a_w_s", "a_b_s", "a_w_out", "sb_w_k", "sb_w_v", "b_w_q", "b_w_o",
             "mix_ln_g", "mix_ln_b", "ffn_ln_g", "ffn_ln_b", "ffn_w1", "ffn_w2")
    return (loss, grad_x, *[grads[n] for n in order], *[delta[n] for n in order],
            *[new_m[n] for n in order], *[new_v[n] for n in order])
```

```python
import math

import jax
import jax.numpy as jnp
from jax import lax
from jax.experimental import pallas as pl
from jax.experimental.pallas import tpu as pltpu

F32 = jnp.float32
BF16 = jnp.bfloat16
MESH = pl.DeviceIdType.MESH

N_CHIPS = 4
DEPTH = 4
N_A = 2
ALPHA = float((2 * DEPTH) ** 0.25)
LN_EPS = 1e-5
CHUNK = 64
GMLP_BLOCK = 128
GMLP_GROUPS = 8
HEAD_DIM = 64
LANES = 128
ATT_T = 256
ADAM_LR = 0.001
ADAM_B1 = 0.9
ADAM_B2 = 0.999
ADAM_EPS = 1e-08
ADAM_WD = 0.01
ADAM_STEP = 10
VMEM_LIMIT = 56 * 1024 * 1024
TM = 512
TM_WIDE = 1024
TS = 2048

NN = ((1,), (0,))
NT = ((1,), (1,))
TN = ((0,), (0,))


def _params(sem):
    return pltpu.CompilerParams(dimension_semantics=sem, vmem_limit_bytes=VMEM_LIMIT)


def _dot(a, b, contract):
    return lax.dot_general(a, b, (contract, ((), ())), preferred_element_type=F32)


def _rider_out(kind, arr):
    shape = (arr.shape[0], arr.shape[1] // 2, arr.shape[2]) if kind == "pair" else arr.shape
    return jax.ShapeDtypeStruct(shape, arr.dtype)


def _rider_copies(kind, src, dst, send_sems, recv_sems, base):
    x, y, c, chips = _place()
    me = 2 * x + y
    sibling = (x, y, 1 - c)

    def copy(k, part, land, to):
        return pltpu.make_async_remote_copy(src_ref=part, dst_ref=land, send_sem=send_sems.at[base + k],
                                            recv_sem=recv_sems.at[base + k], device_id=to, device_id_type=MESH)

    if kind == "pair":
        h = src.shape[1] // 2
        cp = copy(0, src.at[:, pl.ds((1 - c) * h, h)], dst, sibling)
        return [cp], [cp]
    h = dst.shape[1] // 2
    starts, arrivals = [], []
    for k, chip in enumerate(chips):
        blk = 2 * chip[0] + chip[1]
        if kind == "ici":
            starts.append(copy(k, dst.at[me, pl.ds(c * h, h)], dst.at[me, pl.ds(c * h, h)], (*chip, c)))
            arrivals.append(copy(k, dst.at[blk, pl.ds(c * h, h)], dst.at[blk, pl.ds(c * h, h)], (*chip, c)))
        elif kind == "d2d":
            starts.append(copy(k, dst.at[blk, pl.ds(c * h, h)], dst.at[blk, pl.ds(c * h, h)], sibling))
            arrivals.append(copy(k, dst.at[blk, pl.ds((1 - c) * h, h)], dst.at[blk, pl.ds((1 - c) * h, h)], sibling))
        else:
            starts.append(copy(k, src.at[blk], dst.at[me], (*chip, c)))
            arrivals.append(copy(k, src.at[blk], dst.at[blk], (*chip, c)))
    return starts, arrivals


RIDER_SEMS = 3
CARRIER_PARAMS = 5 * 2 ** 20


def _identity(a):
    return a


def _square(a):
    return a * a


def _matmul(name, operands, in_specs, out_shapes, out_specs, grid, contract, epilogue, acc_shape, aliases=None,
            chunks=None, riders=(), pick=False, a_fn=_identity, sequential=False):
    nk = grid[2]
    n_in, n_out, nr = len(operands), len(out_shapes), len(riders)
    n_plain = n_in + nr + n_out

    def body(*refs):
        ins, outs = refs[:n_in], refs[n_in + nr:n_plain]
        if nr:
            srcs, dsts = refs[n_in:n_in + nr], refs[n_plain:n_plain + nr]
            send_sems, recv_sems = refs[-2:]
            pid = [pl.program_id(ax) for ax in range(3)]
            first = (pid[0] == 0) & (pid[1] == 0) & (pid[2] == 0)
            last = (pid[0] == grid[0] - 1) & (pid[1] == grid[1] - 1) & (pid[2] == grid[2] - 1)

            def copies(n):
                return _rider_copies(riders[n][0], srcs[n], dsts[n], send_sems, recv_sems, RIDER_SEMS * n)

            @pl.when(first)
            def _():
                for n in range(nr):
                    for cp in copies(n)[0]:
                        cp.start()

        compute(refs, ins, outs)
        if nr:
            @pl.when(last)
            def _():
                for n in range(nr):
                    starts, arrivals = copies(n)
                    for cp in arrivals:
                        cp.wait_recv()
                    for cp in starts:
                        cp.wait_send()

    def compute(refs, ins, outs):
        if chunks is None:
            b = ins[1][pl.program_id(1)] if pick else ins[1][...]
            p = _dot(a_fn(ins[0][...].astype(BF16)), b.astype(BF16), contract)
        else:
            width = ins[0].shape[1] // chunks
            p = None
            for j in range(chunks):
                pj = _dot(a_fn(ins[0][:, j * width:(j + 1) * width].astype(BF16)), ins[1][j].astype(BF16), contract)
                p = pj if p is None else p + pj
        if nk == 1:
            epilogue(p, ins[2:], outs)
            return
        acc = refs[n_plain + nr]
        k = pl.program_id(2)

        @pl.when(k == 0)
        def _():
            acc[...] = p

        @pl.when((k > 0) & (k < nk - 1))
        def _():
            acc[...] += p

        @pl.when(k == nk - 1)
        def _():
            epilogue(acc[...] + p, ins[2:], outs)

    rbufs = [b for _, b in riders]
    in_place = {n_in + n: n_out + n for n, (kind, _) in enumerate(riders) if kind in ("ici", "d2d")}
    scratch = ([] if nk == 1 else [pltpu.VMEM(acc_shape, F32)]) \
        + [pltpu.SemaphoreType.DMA((RIDER_SEMS * nr,))] * (2 if nr else 0)
    return pl.pallas_call(
        body, name=name, grid=grid, in_specs=list(in_specs) + _any_specs(nr), out_specs=list(out_specs) + _any_specs(nr),
        out_shape=list(out_shapes) + [_rider_out(kind, b) for kind, b in riders],
        scratch_shapes=scratch,
        input_output_aliases={**(aliases or {}), **in_place},
        compiler_params=_params(("arbitrary",) * 3 if nr or sequential else ("parallel", "parallel", "arbitrary")),
    )(*operands, *rbufs)


def _wspec(w, layer, whole=False):
    r, c = w.shape[-2:]
    lead = N_CHIPS if whole else None
    if w.ndim == 4:
        return pl.BlockSpec((lead, None, r, c), lambda j, i, k: (0 if whole else j, layer, 0, 0))
    return pl.BlockSpec((lead, r, c), lambda j, i, k: (0 if whole else j, 0, 0))


def _wide_tile(s):
    return min(TM_WIDE, s)


def _ep_store(p, ins, outs):
    for o in outs:
        o[...] = p.astype(o.dtype)


def _rows_first(spec):
    return pl.BlockSpec(spec.block_shape, lambda i, j, k: spec.index_map(j, i, k))


def _mm_fwd(name, a, w, layer, col_sharded, epilogue=_ep_store, extras=(), extra_specs=(), outs=None, riders=(),
            a_fn=_identity):
    s = a.shape[0]
    r, c = w.shape[-2:]
    if col_sharded:
        tm = _wide_tile(s)
        grid = (s // tm, N_CHIPS, 1)
        a_spec = pl.BlockSpec((tm, r), lambda j, i, k: (i, 0))
        n_out = N_CHIPS * c
    else:
        tm = TM
        grid = (1, s // tm, 1)
        a_spec = pl.BlockSpec((tm, N_CHIPS * r), lambda j, i, k: (i, 0))
        n_out = c
    if outs is None:
        outs = [(n_out, F32)]
    out_shapes = [jax.ShapeDtypeStruct((s, n), dt) for n, dt in outs]
    out_specs = [pl.BlockSpec((tm, c if n == n_out else n), lambda j, i, k: (i, j)) for n, _ in outs]
    in_specs = [a_spec, _wspec(w, layer, True)] + list(extra_specs)
    if col_sharded:
        in_specs, out_specs = [_rows_first(sp) for sp in in_specs], [_rows_first(sp) for sp in out_specs]
    return _matmul(name, (a, w) + tuple(extras), in_specs, out_shapes, out_specs, grid, NN, epilogue, (tm, c),
                   chunks=None if col_sharded else N_CHIPS, riders=riders, pick=col_sharded, a_fn=a_fn)


def _mm_bwd_act(name, dy, w, layer, col_sharded, epilogue=_ep_store, extras=(), extra_specs=(), out_dtype=F32,
                riders=(), through_norm=False):
    s = dy.shape[0]
    r, c = w.shape[-2:]
    if col_sharded:
        tm = TM
        grid = (1, s // tm, 1)
        a_spec = pl.BlockSpec((tm, N_CHIPS * c), lambda j, i, k: (i, 0))
        n_out = r
    else:
        tm = _wide_tile(s)
        grid = (s // tm, N_CHIPS, 1)
        a_spec = pl.BlockSpec((tm, c), lambda j, i, k: (i, 0))
        n_out = N_CHIPS * r
    in_specs = [a_spec, _wspec(w, layer, True)] + list(extra_specs)
    o_spec = pl.BlockSpec((tm, r), lambda j, i, k: (i, j))
    if not col_sharded:
        in_specs, o_spec = [_rows_first(sp) for sp in in_specs], _rows_first(o_spec)
    out_shapes, out_specs = [jax.ShapeDtypeStruct((s, n_out), out_dtype)], [o_spec]
    if through_norm:
        vec = pl.BlockSpec((1, n_out), lambda j, i, k: (0, 0))
        out_shapes = [jax.ShapeDtypeStruct((s, n_out), F32), jax.ShapeDtypeStruct((s, n_out), BF16),
                      jax.ShapeDtypeStruct((1, n_out), F32), jax.ShapeDtypeStruct((1, n_out), F32)]
        out_specs = [o_spec, o_spec, vec, vec]
    return _matmul(name, (dy, w) + tuple(extras), in_specs, out_shapes, out_specs, grid, NT, epilogue, (tm, r),
                   chunks=N_CHIPS if col_sharded else None, riders=riders, pick=not col_sharded,
                   sequential=through_norm)


def _mm_bwd_w(name, a, dy, w, col_sharded, riders=(), a_fn=_identity):
    s = a.shape[0]
    r, c = w.shape[-2:]
    ts = min(TS, s)
    grid = (N_CHIPS, 1, s // ts)
    if col_sharded:
        a_spec = pl.BlockSpec((ts, r), lambda j, i, k: (k, 0))
        b_spec = pl.BlockSpec((ts, c), lambda j, i, k: (k, j))
    else:
        a_spec = pl.BlockSpec((ts, r), lambda j, i, k: (k, j))
        b_spec = pl.BlockSpec((ts, c), lambda j, i, k: (k, 0))

    def epilogue(p, ins, outs):
        outs[0][...] = p

    return _matmul(name, (a, dy), [a_spec, b_spec], [jax.ShapeDtypeStruct(w.shape, F32)], [_wspec(w, None)], grid, TN,
                   epilogue, (r, c), riders=riders, a_fn=a_fn)


def _row_spec(n):
    return pl.BlockSpec((TM, n), lambda j, i, k: (i, 0))


def _vec_spec(layer, n):
    return pl.BlockSpec((None, 1, n), lambda j, i, k: (layer, 0, 0))


def _ep_resid_ln(p, ins, outs):
    x_ref, g_ref, b_ref = ins
    xf_ref, xb_ref, xhat_ref, rstd_ref = outs
    r = ALPHA * x_ref[...] + p
    mu = jnp.mean(r, axis=-1, keepdims=True)
    d = r - mu
    var = jnp.mean(d * d, axis=-1, keepdims=True)
    rstd = lax.rsqrt(var + LN_EPS)
    xhat = d * rstd
    y = xhat * g_ref[...] + b_ref[...]
    xf_ref[...] = y
    xb_ref[...] = y.astype(BF16)
    xhat_ref[...] = xhat
    rstd_ref[...] = rstd


def _mm_resid_ln(name, a, w, x, g3, b3, ln_layer, riders=(), a_fn=_identity):
    d = x.shape[1]
    return _mm_fwd(name, a, w, None, False, _ep_resid_ln, (x, g3, b3),
                   (_row_spec(d), _vec_spec(ln_layer, d), _vec_spec(ln_layer, d)),
                   outs=[(d, F32), (d, BF16), (d, F32), (1, F32)], riders=riders, a_fn=a_fn)


def _ep_relu(p, ins, outs):
    outs[0][...] = jnp.maximum(p, 0.0).astype(BF16)


def _ep_scale_q(p, ins, outs):
    outs[0][...] = (p * (HEAD_DIM ** -0.5)).astype(BF16)


def _ep_bf16(p, ins, outs):
    outs[0][...] = p.astype(BF16)


def _ep_relu2_bwd(p, ins, outs):
    outs[0][...] = (p * (2.0 * ins[0][...].astype(F32))).astype(BF16)


def _ep_resid(p, ins, outs):
    outs[0][...] = ALPHA * ins[0][...] + p


def _ep_resid_ln_bwd(p, ins, outs):
    dr_ref, xh_ref, rs_ref, g_ref = ins
    _ln_bwd_rows(ALPHA * dr_ref[...] + p, xh_ref, rs_ref, g_ref, pl.program_id(1) == 0, *outs)


def _ep_add(p, ins, outs):
    outs[0][...] = ins[0][...] + p


def _gelu_grad(x):
    c0 = math.sqrt(2.0 / math.pi)
    t = jnp.tanh(c0 * (x + 0.044715 * (x * x * x)))
    return 0.5 * (1.0 + t) + (0.5 * x) * (1.0 - t * t) * (c0 * (1.0 + 3.0 * 0.044715 * (x * x)))


def _cast_bf16(w2d):
    r, c = w2d.shape
    tr = min(r, 512)

    def body(w_ref, o_ref):
        o_ref[...] = w_ref[...].astype(BF16)

    return pl.pallas_call(
        body, name="cast_bf16", grid=(r // tr,),
        in_specs=[pl.BlockSpec((tr, c), lambda i: (i, 0))], out_specs=pl.BlockSpec((tr, c), lambda i: (i, 0)),
        out_shape=jax.ShapeDtypeStruct((r, c), BF16), compiler_params=_params(("parallel",)),
    )(w2d)


def _cast_into_slot(w, layer, chip):
    r, c = w.shape[-2:]
    tr = min(r, 512)

    def body(chip_ref, w_ref, o_ref):
        o_ref[...] = w_ref[...].astype(BF16)

    if layer is None:
        w_spec = pl.BlockSpec((tr, c), lambda i, chip_ref: (i, 0))
    else:
        w_spec = pl.BlockSpec((None, tr, c), lambda i, chip_ref: (layer, i, 0))
    grid_spec = pltpu.PrefetchScalarGridSpec(
        num_scalar_prefetch=1, grid=(r // tr,), in_specs=[w_spec],
        out_specs=pl.BlockSpec((None, tr, c), lambda i, chip_ref: (chip_ref[0], i, 0)))
    return pl.pallas_call(
        body, name="cast_into_slot", grid_spec=grid_spec,
        out_shape=jax.ShapeDtypeStruct((N_CHIPS, r, c), BF16), compiler_params=_params(("parallel",)),
    )(chip, w)


def _gmlp_norm_fwd(h, g3, b3, layer):
    s, w2 = h.shape
    w = w2 // 2

    def body(h_ref, g_ref, b_ref, o_ref):
        z = jax.nn.gelu(h_ref[...])
        mu = jnp.mean(z, axis=-1, keepdims=True)
        d = z - mu
        var = jnp.mean(d * d, axis=-1, keepdims=True)
        o_ref[...] = (d * lax.rsqrt(var + LN_EPS) * g_ref[...] + b_ref[...]).astype(BF16)

    vec = pl.BlockSpec((None, 1, w), lambda i: (layer, 0, 0))
    return pl.pallas_call(
        body, name="gmlp_norm_fwd", grid=(s // TM,),
        in_specs=[pl.BlockSpec((TM, w), lambda i: (i, 1)), vec, vec],
        out_specs=pl.BlockSpec((TM, w), lambda i: (i, 0)),
        out_shape=jax.ShapeDtypeStruct((s, w), BF16), compiler_params=_params(("parallel",)),
    )(h, g3, b3)


def _chunk_mask():
    t = lax.broadcasted_iota(jnp.int32, (GMLP_BLOCK, GMLP_BLOCK), 0)
    s = lax.broadcasted_iota(jnp.int32, (GMLP_BLOCK, GMLP_BLOCK), 1)
    return (s // CHUNK) <= (t // CHUNK)


SG_ROWS = 512


def _gate_fwd(h, vn, ws, bst):
    s, w = vn.shape
    gd = w // GMLP_GROUPS

    def body(h_ref, v_ref, ws_ref, bs_ref, o_ref):
        mask = _chunk_mask()
        for g in range(GMLP_GROUPS):
            wm = jnp.where(mask, ws_ref[g], 0.0).astype(BF16)
            bias = bs_ref[:, g:g + 1]
            cols = slice(g * gd, (g + 1) * gd)
            for n in range(SG_ROWS // GMLP_BLOCK):
                rows = slice(n * GMLP_BLOCK, (n + 1) * GMLP_BLOCK)
                sp = _dot(wm, v_ref[rows, cols], NN) + bias
                o_ref[rows, cols] = (jax.nn.gelu(h_ref[rows, cols]) * sp).astype(BF16)

    return pl.pallas_call(
        body, name="gate_fwd", grid=(s // SG_ROWS,),
        in_specs=[pl.BlockSpec((SG_ROWS, w), lambda i: (i, 0)), pl.BlockSpec((SG_ROWS, w), lambda i: (i, 0)),
                  pl.BlockSpec(ws.shape, lambda i: (0, 0, 0)), pl.BlockSpec(bst.shape, lambda i: (0, 0))],
        out_specs=pl.BlockSpec((SG_ROWS, w), lambda i: (i, 0)),
        out_shape=jax.ShapeDtypeStruct((s, w), BF16), compiler_params=_params(("parallel",)),
    )(h, vn, ws, bst)


GB_ROWS = 256


def _gmlp_bwd(dgated, h, vn, ws, bst, g3, layer):
    s, w = vn.shape
    gd = w // GMLP_GROUPS
    nsteps = s // SG_ROWS

    def body(dg_ref, h_ref, v_ref, ws_ref, bs_ref, g_ref, dh_ref, dws_ref, dbs_ref, dlg_ref, dlb_ref, dsum, du_s, dv_s):
        i = pl.program_id(0)

        @pl.when(i == 0)
        def _():
            dws_ref[...] = jnp.zeros_like(dws_ref)
            dsum[...] = jnp.zeros_like(dsum)
            dlg_ref[...] = jnp.zeros_like(dlg_ref)
            dlb_ref[...] = jnp.zeros_like(dlb_ref)

        mask = _chunk_mask()
        for g in range(GMLP_GROUPS):
            wm = jnp.where(mask, ws_ref[g], 0.0).astype(BF16)
            bias = bs_ref[:, g:g + 1]
            cols = slice(g * gd, (g + 1) * gd)
            dw = jnp.zeros((GMLP_BLOCK, GMLP_BLOCK), F32)
            dsg = jnp.zeros((GMLP_BLOCK, gd), F32)
            for n in range(SG_ROWS // GMLP_BLOCK):
                rows = slice(n * GMLP_BLOCK, (n + 1) * GMLP_BLOCK)
                vb = v_ref[rows, cols]
                sp = _dot(wm, vb, NN) + bias
                dg = dg_ref[rows, cols]
                du_s[rows, cols] = dg * sp
                ds = dg * jax.nn.gelu(h_ref[rows, cols])
                dsb = ds.astype(BF16)
                dw += _dot(dsb, vb, NT)
                dsg += ds
                dv_s[rows, cols] = _dot(wm, dsb, TN)
            dws_ref[g] += dw
            dsum[:, cols] += dsg

        for r0 in range(0, SG_ROWS, GB_ROWS):
            rows = slice(r0, r0 + GB_ROWS)
            hu = h_ref[rows, :w]
            hv = h_ref[rows, w:]
            dh_ref[rows, :w] = (du_s[rows, :] * _gelu_grad(hu)).astype(BF16)
            z = jax.nn.gelu(hv)
            mu = jnp.mean(z, axis=-1, keepdims=True)
            d = z - mu
            var = jnp.mean(d * d, axis=-1, keepdims=True)
            rstd = lax.rsqrt(var + LN_EPS)
            xhat = d * rstd
            dy = dv_s[rows, :]
            dlb_ref[...] += jnp.sum(dy, axis=0, keepdims=True)
            dlg_ref[...] += jnp.sum(dy * xhat, axis=0, keepdims=True)
            dxh = dy * g_ref[...]
            m1 = jnp.mean(dxh, axis=-1, keepdims=True)
            m2 = jnp.mean(dxh * xhat, axis=-1, keepdims=True)
            dz = rstd * (dxh - m1 - xhat * m2)
            dh_ref[rows, w:] = (dz * _gelu_grad(hv)).astype(BF16)

        @pl.when(i == nsteps - 1)
        def _():
            for g in range(GMLP_GROUPS):
                dws_ref[g] = jnp.where(mask, dws_ref[g], 0.0)
                tot = jnp.sum(dsum[:, g * gd:(g + 1) * gd], axis=-1, keepdims=True)
                dbs_ref[g] = jnp.broadcast_to(tot, (GMLP_BLOCK, LANES))

    tile = pl.BlockSpec((SG_ROWS, w), lambda i: (i, 0))
    wide = pl.BlockSpec((SG_ROWS, 2 * w), lambda i: (i, 0))
    vec = pl.BlockSpec((1, w), lambda i: (0, 0))
    return pl.pallas_call(
        body, name="gmlp_bwd", grid=(nsteps,),
        in_specs=[tile, wide, tile, pl.BlockSpec(ws.shape, lambda i: (0, 0, 0)), pl.BlockSpec(bst.shape, lambda i: (0, 0)),
                  pl.BlockSpec((None, 1, w), lambda i: (layer, 0, 0))],
        out_specs=[wide, pl.BlockSpec(ws.shape, lambda i: (0, 0, 0)),
                   pl.BlockSpec((GMLP_GROUPS, GMLP_BLOCK, LANES), lambda i: (0, 0, 0)), vec, vec],
        out_shape=[jax.ShapeDtypeStruct((s, 2 * w), BF16), jax.ShapeDtypeStruct(ws.shape, F32),
                   jax.ShapeDtypeStruct((GMLP_GROUPS, GMLP_BLOCK, LANES), F32),
                   jax.ShapeDtypeStruct((1, w), F32), jax.ShapeDtypeStruct((1, w), F32)],
        scratch_shapes=[pltpu.VMEM((GMLP_BLOCK, w), F32), pltpu.VMEM((SG_ROWS, w), F32), pltpu.VMEM((SG_ROWS, w), F32)],
        compiler_params=_params(("arbitrary",)),
    )(dgated, h, vn, ws, bst, g3)


def _ln_bwd_rows(dy, xh_ref, rs_ref, g_ref, first, dr_ref, drb_ref, dg_ref, db_ref):
    @pl.when(first)
    def _():
        dg_ref[...] = jnp.zeros_like(dg_ref)
        db_ref[...] = jnp.zeros_like(db_ref)

    xhat = xh_ref[...]
    db_ref[...] += jnp.sum(dy, axis=0, keepdims=True)
    dg_ref[...] += jnp.sum(dy * xhat, axis=0, keepdims=True)
    dxh = dy * g_ref[...]
    m1 = jnp.mean(dxh, axis=-1, keepdims=True)
    m2 = jnp.mean(dxh * xhat, axis=-1, keepdims=True)
    dr = rs_ref[...] * (dxh - m1 - xhat * m2)
    dr_ref[...] = dr
    drb_ref[...] = dr.astype(BF16)


def _ln_bwd(dy, xhat, rstd, g3, layer, target=None):
    s, d = dy.shape
    nsteps = s // TM
    head = target is not None

    def body(*refs):
        dy_ref, xh_ref, rs_ref, g_ref = refs[:4]
        dr_ref, drb_ref, dg_ref, db_ref = refs[4 + head:8 + head]
        first = pl.program_id(0) == 0
        dyv = dy_ref[...]
        if head:
            l_ref = refs[8 + head]

            @pl.when(first)
            def _():
                l_ref[...] = jnp.zeros_like(l_ref)

            e = dyv - refs[4][...]
            l_ref[...] += jnp.sum(jnp.sum(e * e, axis=1, keepdims=True), axis=0, keepdims=True)
            dyv = e * (1.0 / d)
        _ln_bwd_rows(dyv, xh_ref, rs_ref, g_ref, first, dr_ref, drb_ref, dg_ref, db_ref)

    tile = pl.BlockSpec((TM, d), lambda i: (i, 0))
    vec = pl.BlockSpec((1, d), lambda i: (0, 0))
    one = pl.BlockSpec((1, 1), lambda i: (0, 0))
    return pl.pallas_call(
        body, name="ln_bwd", grid=(nsteps,),
        in_specs=[tile, tile, pl.BlockSpec((TM, 1), lambda i: (i, 0)), pl.BlockSpec((None, 1, d), lambda i: (layer, 0, 0))]
        + ([tile] if head else []),
        out_specs=[tile, tile, vec, vec] + ([one] if head else []),
        out_shape=[jax.ShapeDtypeStruct((s, d), F32), jax.ShapeDtypeStruct((s, d), BF16),
                   jax.ShapeDtypeStruct((1, d), F32), jax.ShapeDtypeStruct((1, d), F32)]
        + ([jax.ShapeDtypeStruct((1, 1), F32)] if head else []),
        compiler_params=_params(("arbitrary",)),
    )(dy, xhat, rstd, g3, *([target] if head else []))


LOG2E = 1.4426950408889634
DEAD_LOG2 = -160.0
FIRST_LANE = 1


def _sb_terms(z, causal):
    z2 = z * LOG2E
    e = jnp.exp2(-jnp.abs(z2))
    l1p = jnp.log2(1.0 + e)
    lb = jnp.minimum(z2, 0.0) - l1p
    lr = lb - z2
    if causal is not None:
        lr = jnp.where(causal, lr, 0.0)
    return lb, lr, e


def _split_hi_lo(x):
    hi = x.astype(BF16)
    lo = (x - hi.astype(F32)).astype(BF16)
    return jnp.concatenate([hi, lo], axis=1)


def _tri2(prefix):
    r = lax.broadcasted_iota(jnp.int32, (2 * ATT_T, ATT_T), 0) % ATT_T
    c = lax.broadcasted_iota(jnp.int32, (2 * ATT_T, ATT_T), 1)
    return jnp.where((r <= c) if prefix else (r >= c), 1.0, 0.0).astype(BF16)


def _att_masks():
    r = lax.broadcasted_iota(jnp.int32, (ATT_T, ATT_T), 0)
    c = lax.broadcasted_iota(jnp.int32, (ATT_T, ATT_T), 1)
    return c < r, lax.broadcasted_iota(jnp.int32, (1, LANES), 1) < HEAD_DIM


def _attn_fwd(q, k, v):
    s, d = q.shape
    nq = s // ATT_T

    def body(q_ref, k_ref, v_ref, tri_ref, ob_ref, lsum_ref, acc_a, acc_b, rem_a, rem_b):
        i = pl.program_id(1)
        tri = tri_ref[...]
        causal, head_a = _att_masks()
        q2 = q_ref[...]
        zero = jnp.zeros_like(q2)
        qa = jnp.where(head_a, q2, zero)
        qb = jnp.where(head_a, zero, q2)
        acc_a[...] = jnp.zeros_like(acc_a)
        acc_b[...] = jnp.zeros_like(acc_b)
        rem_a[...] = jnp.zeros_like(rem_a)
        rem_b[...] = jnp.zeros_like(rem_b)

        def block(kb, mask):
            rows = pl.ds(pl.multiple_of(kb * ATT_T, ATT_T), ATT_T)
            k2 = k_ref[rows, :]
            v2 = v_ref[rows, :]
            heads = ((qa, acc_a, rem_a), (qb, acc_b, rem_b))
            zs = [_dot(qm, k2, NT) for qm, _, _ in heads]
            terms = [_sb_terms(z, mask) for z in zs]
            sums = [_dot(_split_hi_lo(lr), tri, NN) for _, lr, _ in terms]
            for (_, acc, rem), (lb, lr, _), sincl in zip(heads, terms, sums):
                a = jnp.exp2(lb + (sincl - lr) + rem[...])
                if mask is not None:
                    a = jnp.where(mask, a, 0.0)
                rem[...] += sincl[:, 0:1]
                acc[...] += _dot(a.astype(BF16), v2, NN)

        block(i, causal)

        def live():
            return jnp.maximum(jnp.max(rem_a[...]), jnp.max(rem_b[...])) > DEAD_LOG2

        def go_on(carry):
            t, alive = carry
            return (t < i) & alive

        def step(carry):
            t, _ = carry
            block(i - 1 - t, None)
            return t + 1, live()

        done, _ = lax.while_loop(go_on, step, (jnp.int32(0), live()))
        first = (i - done).astype(F32)
        ob_ref[...] = jnp.where(head_a, acc_a[...], acc_b[...]).astype(BF16)
        lane = lax.broadcasted_iota(jnp.int32, (1, LANES), 1)
        lsum_ref[...] = jnp.where(lane == FIRST_LANE, first, jnp.where(head_a, rem_a[...], rem_b[...]))

    qspec = pl.BlockSpec((ATT_T, LANES), lambda p, i: (i, p))
    kspec = pl.BlockSpec((s, LANES), lambda p, i: (0, p))
    return pl.pallas_call(
        body, name="attn_fwd", grid=(d // LANES, nq),
        in_specs=[qspec, kspec, kspec, pl.BlockSpec((2 * ATT_T, ATT_T), lambda p, i: (0, 0))],
        out_specs=[qspec, qspec],
        out_shape=[jax.ShapeDtypeStruct((s, d), BF16), jax.ShapeDtypeStruct((s, d), F32)],
        scratch_shapes=[pltpu.VMEM((ATT_T, LANES), F32), pltpu.VMEM((ATT_T, LANES), F32),
                        pltpu.VMEM((ATT_T, 1), F32), pltpu.VMEM((ATT_T, 1), F32)],
        compiler_params=_params(("parallel", "arbitrary")),
    )(q, k, v, _tri2(prefix=False))


def _attn_bwd(q, k, v, do, lsum, dk_prev=None, dv_prev=None):
    s, d = q.shape
    nq = s // ATT_T
    has_prev = dk_prev is not None

    def body(*refs):
        q_ref, k_ref, v_ref, do_ref, ls_ref, tri_ref = refs[:6]
        n_in = 8 if has_prev else 6
        dq_ref, dk_ref, dv_ref, acc_a, acc_b, pre_a, pre_b, gp_a, gp_b, dkt, dvt = refs[n_in:]
        i = pl.program_id(1)

        @pl.when(i == 0)
        def _():
            dkt[...] = jnp.zeros_like(dkt)
            dvt[...] = jnp.zeros_like(dvt)

        tri = tri_ref[...]
        causal, head_a = _att_masks()
        q2 = q_ref[...]
        zero = jnp.zeros_like(q2)
        qa = jnp.where(head_a, q2, zero)
        qb = jnp.where(head_a, zero, q2)
        do2 = do_ref[...]
        doa = jnp.where(head_a, do2, 0.0).astype(BF16)
        dob = jnp.where(head_a, 0.0, do2).astype(BF16)
        row_a = lax.broadcasted_iota(jnp.int32, (LANES, 1), 0) < HEAD_DIM
        qt = q2.astype(F32).T
        dot_ = do2.T
        qta, qtb = jnp.where(row_a, qt, 0.0).astype(BF16), jnp.where(row_a, 0.0, qt).astype(BF16)
        dota, dotb = jnp.where(row_a, dot_, 0.0).astype(BF16), jnp.where(row_a, 0.0, dot_).astype(BF16)
        ls2 = ls_ref[...]
        tot_a = ls2[:, 0:1]
        tot_b = ls2[:, HEAD_DIM:HEAD_DIM + 1]
        for r in (acc_a, acc_b, pre_a, pre_b, gp_a, gp_b):
            r[...] = jnp.zeros_like(r)

        def block(kb, mask):
            rows = pl.ds(pl.multiple_of(kb * ATT_T, ATT_T), ATT_T)
            k2 = k_ref[rows, :]
            v2 = v_ref[rows, :]
            dk_new = jnp.zeros((LANES, ATT_T), F32)
            dv_new = jnp.zeros((LANES, ATT_T), F32)
            heads = ((qa, doa, tot_a, acc_a, pre_a, gp_a, qta, dota), (qb, dob, tot_b, acc_b, pre_b, gp_b, qtb, dotb))
            zs = [_dot(h[0], k2, NT) for h in heads]
            das = [_dot(h[1], v2, NT) for h in heads]
            terms = [_sb_terms(z, mask) for z in zs]
            psums = [_dot(_split_hi_lo(lr), tri, NN) for _, lr, _ in terms]
            gs, abs_ = [], []
            for h, (lb, _, _), pincl, da in zip(heads, terms, psums, das):
                tot, pre = h[2], h[4]
                a = jnp.exp2(lb + (tot - (pre[...] + pincl)))
                if mask is not None:
                    a = jnp.where(mask, a, 0.0)
                pre[...] += pincl[:, ATT_T - 1:ATT_T]
                gs.append(a * da)
                abs_.append(a.astype(BF16))
            gsums = [_dot(g.astype(BF16), tri[:ATT_T], NN) for g in gs]
            dzs = []
            for h, z, (_, _, e), g, gincl in zip(heads, zs, terms, gs, gsums):
                gpre = h[5]
                gbefore = gpre[...] + (gincl - g)
                gpre[...] += gincl[:, ATT_T - 1:ATT_T]
                inv = 1.0 / (1.0 + e)
                beta = jnp.where(z >= 0.0, inv, e * inv)
                dz = g - beta * (g + gbefore)
                if mask is not None:
                    dz = jnp.where(mask, dz, 0.0)
                dzs.append(dz.astype(BF16))
            for h, ab, dzb in zip(heads, abs_, dzs):
                dv_new += _dot(h[7], ab, NN)
                dk_new += _dot(h[6], dzb, NN)
                h[3][...] += _dot(dzb, k2, NN)
            cols = pl.ds(pl.multiple_of(kb * ATT_T, ATT_T), ATT_T)
            dkt[:, cols] += dk_new
            dvt[:, cols] += dv_new

        def step(kb, carry):
            block(kb, None)
            return carry

        first = jnp.clip(jnp.max(ls2[:, FIRST_LANE:FIRST_LANE + 1]).astype(jnp.int32), 0, i)
        lax.fori_loop(first, i, step, 0)
        block(i, causal)
        dq_ref[...] = (jnp.where(head_a, acc_a[...], acc_b[...]) * (HEAD_DIM ** -0.5)).astype(BF16)

        @pl.when(i == nq - 1)
        def _():
            for n in range(nq):
                rows = slice(n * ATT_T, (n + 1) * ATT_T)
                dkn, dvn = dkt[:, rows].T, dvt[:, rows].T
                if has_prev:
                    dkn, dvn = dkn + refs[6][rows, :], dvn + refs[7][rows, :]
                dk_ref[rows, :] = dkn
                dv_ref[rows, :] = dvn

    qspec = pl.BlockSpec((ATT_T, LANES), lambda p, i: (i, p))
    kspec = pl.BlockSpec((s, LANES), lambda p, i: (0, p))
    ins = [q, k, v, do, lsum, _tri2(prefix=True)] + ([dk_prev, dv_prev] if has_prev else [])
    return pl.pallas_call(
        body, name="attn_bwd", grid=(d // LANES, nq),
        in_specs=[qspec, kspec, kspec, qspec, qspec, pl.BlockSpec((2 * ATT_T, ATT_T), lambda p, i: (0, 0))]
        + ([kspec, kspec] if has_prev else []),
        out_specs=[qspec, kspec, kspec],
        out_shape=[jax.ShapeDtypeStruct((s, d), BF16), jax.ShapeDtypeStruct((s, d), F32), jax.ShapeDtypeStruct((s, d), F32)],
        scratch_shapes=[pltpu.VMEM((ATT_T, LANES), F32), pltpu.VMEM((ATT_T, LANES), F32)]
        + [pltpu.VMEM((ATT_T, 1), F32)] * 4 + [pltpu.VMEM((LANES, s), F32)] * 2,
        compiler_params=_params(("parallel", "arbitrary")),
    )(*ins)


def _place():
    x, y, c = lax.axis_index("x"), lax.axis_index("y"), lax.axis_index("c")
    chips = [(1 - x, y), (x, 1 - y), (1 - x, 1 - y)]
    return x, y, c, chips


def _any_specs(n):
    return [pl.BlockSpec(memory_space=pl.ANY)] * n


def _gather_weights(bufs):
    n = len(bufs)

    def body(*refs):
        outs = refs[n:2 * n]
        send_sems, recv_sems = refs[2 * n:]
        x, y, c, chips = _place()
        me = 2 * x + y
        sibling = (x, y, 1 - c)

        def half(a, blk, hc):
            h = outs[a].shape[1] // 2
            return outs[a].at[blk, pl.ds(hc * h, h)]

        def copy(a, k, part, to):
            return pltpu.make_async_remote_copy(src_ref=part, dst_ref=part, send_sem=send_sems.at[a, k],
                                                recv_sem=recv_sems.at[a, k], device_id=to, device_id_type=MESH)

        sent = []
        for a in range(n):
            for k, chip in enumerate(chips):
                sent.append(copy(a, k, half(a, me, c), (*chip, c)))
                sent[-1].start()
        for a in range(n):
            for k, chip in enumerate(chips):
                blk = 2 * chip[0] + chip[1]
                copy(a, k, half(a, blk, c), sibling).wait_recv()
                sent.append(copy(a, 3 + k, half(a, blk, c), sibling))
                sent[-1].start()
        for a in range(n):
            for k, chip in enumerate(chips):
                blk = 2 * chip[0] + chip[1]
                copy(a, 3 + k, half(a, blk, 1 - c), sibling).wait_recv()
        for cp in sent:
            cp.wait_send()

    return pl.pallas_call(
        body, name="gather_weights", in_specs=_any_specs(n), out_specs=_any_specs(n),
        out_shape=[jax.ShapeDtypeStruct(w.shape, w.dtype) for w in bufs],
        input_output_aliases={a: a for a in range(n)},
        scratch_shapes=[pltpu.SemaphoreType.DMA((n, 6)), pltpu.SemaphoreType.DMA((n, 6))],
        compiler_params=pltpu.CompilerParams(has_side_effects=True),
    )(*bufs)


def _pair_exchange(grads):
    n = len(grads)

    def body(*refs):
        ins, outs = refs[:n], refs[n:2 * n]
        send_sems, recv_sems = refs[2 * n:]
        x, y, c, _ = _place()
        cps = []
        for a in range(n):
            h = ins[a].shape[1] // 2
            cps.append(pltpu.make_async_remote_copy(
                src_ref=ins[a].at[:, pl.ds((1 - c) * h, h)], dst_ref=outs[a], send_sem=send_sems.at[a],
                recv_sem=recv_sems.at[a], device_id=(x, y, 1 - c), device_id_type=MESH))
            cps[-1].start()
        for cp in cps:
            cp.wait()

    return pl.pallas_call(
        body, name="pair_exchange", in_specs=_any_specs(n), out_specs=_any_specs(n),
        out_shape=[jax.ShapeDtypeStruct((g.shape[0], g.shape[1] // 2, g.shape[2]), g.dtype) for g in grads],
        scratch_shapes=[pltpu.SemaphoreType.DMA((n,)), pltpu.SemaphoreType.DMA((n,))],
        compiler_params=pltpu.CompilerParams(has_side_effects=True),
    )(*grads)


def _chip_exchange(parts):
    n = len(parts)

    def body(*refs):
        ins, outs = refs[:n], refs[n:2 * n]
        send_sems, recv_sems = refs[2 * n:]
        x, y, c, chips = _place()
        me = 2 * x + y
        cps = []
        for a in range(n):
            for k, chip in enumerate(chips):
                blk = 2 * chip[0] + chip[1]
                cps.append(pltpu.make_async_remote_copy(
                    src_ref=ins[a].at[blk], dst_ref=outs[a].at[me], send_sem=send_sems.at[a, k],
                    recv_sem=recv_sems.at[a, k], device_id=(*chip, c), device_id_type=MESH))
                cps[-1].start()
        for a in range(n):
            for k, chip in enumerate(chips):
                blk = 2 * chip[0] + chip[1]
                pltpu.make_async_remote_copy(
                    src_ref=ins[a].at[blk], dst_ref=outs[a].at[blk], send_sem=send_sems.at[a, k],
                    recv_sem=recv_sems.at[a, k], device_id=(*chip, c), device_id_type=MESH).wait_recv()
        for cp in cps:
            cp.wait_send()

    return pl.pallas_call(
        body, name="chip_exchange", in_specs=_any_specs(n), out_specs=_any_specs(n),
        out_shape=[jax.ShapeDtypeStruct(p.shape, p.dtype) for p in parts],
        scratch_shapes=[pltpu.SemaphoreType.DMA((n, 3)), pltpu.SemaphoreType.DMA((n, 3))],
        compiler_params=pltpu.CompilerParams(has_side_effects=True),
    )(*parts)


def _half_swap(halves):
    n = len(halves)

    def body(*refs):
        outs = refs[n:2 * n]
        send_sems, recv_sems = refs[2 * n:]
        x, y, c, _ = _place()
        cps = []
        for a in range(n):
            h = outs[a].shape[1] // 2
            mine = outs[a].at[:, pl.ds(c * h, h)]
            cps.append(pltpu.make_async_remote_copy(
                src_ref=mine, dst_ref=mine, send_sem=send_sems.at[a], recv_sem=recv_sems.at[a],
                device_id=(x, y, 1 - c), device_id_type=MESH))
            cps[-1].start()
        for cp in cps:
            cp.wait()

    return pl.pallas_call(
        body, name="half_swap", in_specs=_any_specs(n), out_specs=_any_specs(n),
        out_shape=[jax.ShapeDtypeStruct(p.shape, p.dtype) for p in halves],
        input_output_aliases={a: a for a in range(n)},
        scratch_shapes=[pltpu.SemaphoreType.DMA((n,)), pltpu.SemaphoreType.DMA((n,))],
        compiler_params=pltpu.CompilerParams(has_side_effects=True),
    )(*halves)


N_DEV = 8


def _all_reduce_small(v):
    nrow, ncol = v.shape

    def body(v_ref, o_ref, land, red, send_sems, recv_sems, send2, recv2, loc_sem):
        x, y, c, _ = _place()
        me = 4 * x + 2 * y + c
        peers = []
        for k in range(1, N_DEV):
            peers.append((x ^ ((k >> 2) & 1), y ^ ((k >> 1) & 1), c ^ (k & 1)))
        own = pltpu.make_async_copy(v_ref.at[pl.ds(me, 1)], land.at[pl.ds(me, 1)], loc_sem)
        own.start()
        cps = []
        for k, peer in enumerate(peers):
            dev = 4 * peer[0] + 2 * peer[1] + peer[2]
            cps.append(pltpu.make_async_remote_copy(
                src_ref=v_ref.at[pl.ds(dev, 1)], dst_ref=land.at[pl.ds(me, 1)], send_sem=send_sems.at[k],
                recv_sem=recv_sems.at[k], device_id=peer, device_id_type=MESH))
            cps[-1].start()
        for k, peer in enumerate(peers):
            dev = 4 * peer[0] + 2 * peer[1] + peer[2]
            pltpu.make_async_remote_copy(
                src_ref=v_ref.at[pl.ds(dev, 1)], dst_ref=land.at[pl.ds(dev, 1)], send_sem=send_sems.at[k],
                recv_sem=recv_sems.at[k], device_id=peer, device_id_type=MESH).wait_recv()
        for cp in cps:
            cp.wait_send()
        own.wait()
        terms = land[...]
        total = terms[0:1, :]
        for d in range(1, N_DEV):
            total = total + terms[d:d + 1, :]
        red[...] = total
        own = pltpu.make_async_copy(red, o_ref.at[pl.ds(me, 1)], loc_sem)
        own.start()
        cps = []
        for k, peer in enumerate(peers):
            cps.append(pltpu.make_async_remote_copy(
                src_ref=red, dst_ref=o_ref.at[pl.ds(me, 1)], send_sem=send2.at[k],
                recv_sem=recv2.at[k], device_id=peer, device_id_type=MESH))
            cps[-1].start()
        for k, peer in enumerate(peers):
            dev = 4 * peer[0] + 2 * peer[1] + peer[2]
            pltpu.make_async_remote_copy(
                src_ref=red, dst_ref=o_ref.at[pl.ds(dev, 1)], send_sem=send2.at[k],
                recv_sem=recv2.at[k], device_id=peer, device_id_type=MESH).wait_recv()
        for cp in cps:
            cp.wait_send()
        own.wait()

    vm = pl.BlockSpec(memory_space=pltpu.VMEM)
    return pl.pallas_call(
        body, name="all_reduce_small", in_specs=[vm], out_specs=vm,
        out_shape=jax.ShapeDtypeStruct((nrow, ncol), F32),
        scratch_shapes=[pltpu.VMEM((nrow, ncol), F32), pltpu.VMEM((1, ncol), F32)]
        + [pltpu.SemaphoreType.DMA((N_DEV - 1,))] * 4 + [pltpu.SemaphoreType.DMA],
        compiler_params=pltpu.CompilerParams(has_side_effects=True, vmem_limit_bytes=VMEM_LIMIT),
    )(v)


def _row_tile(rows):
    return min(rows, 512)


def _pair_sum(g, got, place):
    nb, r, c = g.shape
    h = r // 2
    tr = _row_tile(h)
    nt = h // tr

    def body(place_ref, g_ref, got_ref, p_ref, pb_ref):
        p = g_ref[...] + got_ref[...]
        pb_ref[...] = p.astype(BF16)

        @pl.when(pl.program_id(1) == place_ref[0])
        def _():
            p_ref[...] = p

    spec = pl.BlockSpec((None, tr, c), lambda t, j, place_ref: (j, t, 0))
    grid_spec = pltpu.PrefetchScalarGridSpec(
        num_scalar_prefetch=1, grid=(nt, nb),
        in_specs=[pl.BlockSpec((None, tr, c), lambda t, j, place_ref: (j, place_ref[1] * nt + t, 0)), spec],
        out_specs=[pl.BlockSpec((tr, c), lambda t, j, place_ref: (t, 0)), spec])
    return pl.pallas_call(
        body, name="pair_sum", grid_spec=grid_spec,
        out_shape=[jax.ShapeDtypeStruct((h, c), F32), jax.ShapeDtypeStruct((nb, h, c), BF16)],
        compiler_params=_params(("parallel", "arbitrary")),
    )(place, g, got)


def _chip_sum(p, got, place, out, layer):
    h, c = p.shape
    tr = _row_tile(h)
    nt = h // tr

    def body(place_ref, p_ref, g1_ref, g2_ref, g3_ref, old_ref, o_ref):
        o_ref[...] = ((p_ref[...] + g1_ref[...].astype(F32)) + g2_ref[...].astype(F32)) + g3_ref[...].astype(F32)

    def blk(off):
        return pl.BlockSpec((None, tr, c), lambda t, place_ref: ((place_ref[0] + off) % N_CHIPS, t, 0))

    grid_spec = pltpu.PrefetchScalarGridSpec(
        num_scalar_prefetch=1, grid=(nt,),
        in_specs=[pl.BlockSpec((tr, c), lambda t, place_ref: (t, 0)), blk(1), blk(2), blk(3),
                  pl.BlockSpec(memory_space=pl.ANY)],
        out_specs=pl.BlockSpec((None, tr, c), lambda t, place_ref: (layer, place_ref[1] * nt + t, 0)))
    return pl.pallas_call(
        body, name="chip_sum", grid_spec=grid_spec, out_shape=jax.ShapeDtypeStruct(out.shape, F32),
        input_output_aliases={5: 0}, compiler_params=_params(("parallel",)),
    )(place, p, got, got, got, out)


def _adamw(w, g, m, v, pass_g=False):
    r, c = w.shape
    tr = r if r < 8 else _row_tile(r)
    n_out = 4 if pass_g else 3

    def body(w_ref, g_ref, m_ref, v_ref, d_ref, nm_ref, nv_ref, *g_out):
        gv = g_ref[...]
        if pass_g:
            g_out[0][...] = gv
        nm = ADAM_B1 * m_ref[...] + (1.0 - ADAM_B1) * gv
        nv = ADAM_B2 * v_ref[...] + (1.0 - ADAM_B2) * (gv * gv)
        m_hat = nm / (1.0 - ADAM_B1 ** ADAM_STEP)
        v_hat = nv / (1.0 - ADAM_B2 ** ADAM_STEP)
        d_ref[...] = -ADAM_LR * (m_hat / (jnp.sqrt(v_hat) + ADAM_EPS) + ADAM_WD * w_ref[...])
        nm_ref[...] = nm
        nv_ref[...] = nv

    tile = pl.BlockSpec((tr, c), lambda i: (i, 0))
    return pl.pallas_call(
        body, name="adamw", grid=(r // tr,), in_specs=[tile] * 4, out_specs=[tile] * n_out,
        out_shape=[jax.ShapeDtypeStruct((r, c), F32)] * n_out, compiler_params=_params(("parallel",)),
    )(w, g, m, v)


BIG = ("a_w_in", "a_w_out", "sb_w_k", "sb_w_v", "b_w_q", "b_w_o", "ffn_w1", "ffn_w2")
SMALL = ("a_ln_g", "a_ln_b", "a_w_s", "a_b_s", "mix_ln_g", "mix_ln_b", "ffn_ln_g", "ffn_ln_b")
COL_SHARDED = {"a_w_in": True, "a_w_out": False, "sb_w_k": False, "sb_w_v": False, "b_w_q": False, "b_w_o": False,
               "ffn_w1": True, "ffn_w2": False}


def kernel(x, a_w_in, a_ln_g, a_ln_b, a_w_s, a_b_s, a_w_out, sb_w_k, sb_w_v, b_w_q, b_w_o, mix_ln_g, mix_ln_b, ffn_ln_g, ffn_ln_b, ffn_w1, ffn_w2, loss_target, m_a_w_in, m_a_ln_g, m_a_ln_b, m_a_w_s, m_a_b_s, m_a_w_out, m_sb_w_k, m_sb_w_v, m_b_w_q, m_b_w_o, m_mix_ln_g, m_mix_ln_b, m_ffn_ln_g, m_ffn_ln_b, m_ffn_w1, m_ffn_w2, v_a_w_in, v_a_ln_g, v_a_ln_b, v_a_w_s, v_a_b_s, v_a_w_out, v_sb_w_k, v_sb_w_v, v_b_w_q, v_b_w_o, v_mix_ln_g, v_mix_ln_b, v_ffn_ln_g, v_ffn_ln_b, v_ffn_w1, v_ffn_w2):
    names = BIG + SMALL
    given = dict(a_w_in=a_w_in, a_ln_g=a_ln_g, a_ln_b=a_ln_b, a_w_s=a_w_s, a_b_s=a_b_s, a_w_out=a_w_out, sb_w_k=sb_w_k,
                 sb_w_v=sb_w_v, b_w_q=b_w_q, b_w_o=b_w_o, mix_ln_g=mix_ln_g, mix_ln_b=mix_ln_b, ffn_ln_g=ffn_ln_g,
                 ffn_ln_b=ffn_ln_b, ffn_w1=ffn_w1, ffn_w2=ffn_w2)
    mom = dict(a_w_in=m_a_w_in, a_ln_g=m_a_ln_g, a_ln_b=m_a_ln_b, a_w_s=m_a_w_s, a_b_s=m_a_b_s, a_w_out=m_a_w_out,
               sb_w_k=m_sb_w_k, sb_w_v=m_sb_w_v, b_w_q=m_b_w_q, b_w_o=m_b_w_o, mix_ln_g=m_mix_ln_g, mix_ln_b=m_mix_ln_b,
               ffn_ln_g=m_ffn_ln_g, ffn_ln_b=m_ffn_ln_b, ffn_w1=m_ffn_w1, ffn_w2=m_ffn_w2)
    var = dict(a_w_in=v_a_w_in, a_ln_g=v_a_ln_g, a_ln_b=v_a_ln_b, a_w_s=v_a_w_s, a_b_s=v_a_b_s, a_w_out=v_a_w_out,
               sb_w_k=v_sb_w_k, sb_w_v=v_sb_w_v, b_w_q=v_b_w_q, b_w_o=v_b_w_o, mix_ln_g=v_mix_ln_g, mix_ln_b=v_mix_ln_b,
               ffn_ln_g=v_ffn_ln_g, ffn_ln_b=v_ffn_ln_b, ffn_w1=v_ffn_w1, ffn_w2=v_ffn_w2)

    cx, cy, cc = lax.axis_index("x"), lax.axis_index("y"), lax.axis_index("c")
    chip = (2 * cx + cy).astype(jnp.int32)
    chip_arr = chip.reshape(1)

    s, d = x.shape[1], x.shape[2]
    xf = x.reshape(s, d)
    target = loss_target.reshape(s, d)

    def as2d(w):
        return w.reshape(-1, w.shape[-1])

    gw = {}
    for n in BIG:
        for l in ([None] if given[n].ndim == 2 else range(given[n].shape[0])):
            gw[(n, l)] = _cast_into_slot(given[n], l, chip_arr)
    ln_gb = jnp.stack([a_ln_g, a_ln_b])
    ln_slot = lax.dynamic_update_slice(jnp.zeros((N_CHIPS,) + ln_gb.shape, F32), ln_gb[None], (chip, 0, 0, 0))
    layer0 = [("a_w_in", 0), ("a_w_out", 0)]
    gathered = _gather_weights([gw[k] for k in layer0] + [ln_slot])
    gw.update(zip(layer0, gathered[:-1]))
    mixer = {1: [("a_w_in", 1), ("a_w_out", 1)], 2: [("sb_w_k", None), ("sb_w_v", None), ("b_w_q", 0), ("b_w_o", 0)],
             3: [("b_w_q", 1), ("b_w_o", 1)]}

    def riding(d2d=(), ici=()):
        keys = list(d2d) + list(ici)
        return keys, [("d2d", gw[k]) for k in d2d] + [("ici", gw[k]) for k in ici]

    def landed_in(keys, bufs):
        gw.update(zip(keys, bufs))

    ln_full = gathered[-1].transpose(1, 2, 0, 3).reshape(2, N_A, 1, -1)
    a_ln_g3, a_ln_b3 = ln_full[0], ln_full[1]
    mix_g3, mix_b3 = mix_ln_g[:, None, :], mix_ln_b[:, None, :]
    ffn_g3, ffn_b3 = ffn_ln_g[:, None, :], ffn_ln_b[:, None, :]
    bst = jnp.swapaxes(a_b_s, 1, 2)

    saved = []
    xb = _cast_bf16(xf)
    kb = vb = None
    for l in range(DEPTH):
        sv = dict(x_in=xb)
        last = l == DEPTH - 1
        if l == 0:
            keys, riders = riding(ici=[("ffn_w1", 0)])
        elif l == 1:
            keys, riders = riding(d2d=[("ffn_w1", 1)], ici=[("ffn_w2", 1)])
        else:
            keys, riders = riding(d2d=[("ffn_w1", l), ("ffn_w2", l)])
        if l < N_A:
            h, *bufs = _mm_fwd("a_in", xb, gw[("a_w_in", l)], None, True, riders=riders)
            landed_in(keys, bufs)
            vn = _gmlp_norm_fwd(h, a_ln_g3, a_ln_b3, l)
            gated = _gate_fwd(h, vn, a_w_s[l], bst[l])
            keys, riders = riding(d2d=[("ffn_w1", 0)], ici=[("ffn_w2", 0)]) if l == 0 else riding(ici=mixer[l + 1][:2])
            xf, xb, xhat, rstd, *bufs = _mm_resid_ln("a_out", gated, gw[("a_w_out", l)], xf, mix_g3, mix_b3, l, riders)
            landed_in(keys, bufs)
            sv.update(h=h, vn=vn, gated=gated)
        else:
            j = l - N_A
            if l == N_A:
                kb, *bufs = _mm_fwd("sb_k", xb, gw[("sb_w_k", None)], None, False, _ep_bf16, outs=[(d, BF16)],
                                    riders=riders)
                landed_in(keys, bufs)
                keys, riders = [], ()
                vb = _mm_fwd("sb_v", xb, gw[("sb_w_v", None)], None, False, _ep_bf16, outs=[(d, BF16)])[0]
            q, *bufs = _mm_fwd("b_q", xb, gw[("b_w_q", j)], None, False, _ep_scale_q, outs=[(d, BF16)], riders=riders)
            landed_in(keys, bufs)
            ob, lsum = _attn_fwd(q, kb, vb)
            keys, riders = riding(ici=[] if last else mixer[l + 1])
            xf, xb, xhat, rstd, *bufs = _mm_resid_ln("b_out", ob, gw[("b_w_o", j)], xf, mix_g3, mix_b3, l, riders)
            landed_in(keys, bufs)
            sv.update(q=q, lsum=lsum, ob=ob)
        sv.update(x_mid=xb, xhat1=xhat, rstd1=rstd)
        dff = gw[("ffn_w1", l)].shape[-1] * N_CHIPS
        if l == 0:
            keys, riders = riding(d2d=[("ffn_w2", 0)], ici=mixer[1])
        elif l == 1:
            keys, riders = riding(d2d=[("ffn_w2", 1)], ici=mixer[2][2:] + [("ffn_w1", 2)])
        else:
            keys, riders = riding(ici=[] if last else [("ffn_w1", l + 1)])
        pr, *bufs = _mm_fwd("ffn_1", xb, gw[("ffn_w1", l)], None, True, _ep_relu, outs=[(dff, BF16)], riders=riders)
        landed_in(keys, bufs)
        if l == 0:
            keys, riders = riding(d2d=mixer[1], ici=[("ffn_w1", 1)])
        else:
            keys, riders = riding(d2d=[] if last else mixer[l + 1], ici=[] if last else [("ffn_w2", l + 1)])
        xf, xb, xhat, rstd, *bufs = _mm_resid_ln("ffn_2", pr, gw[("ffn_w2", l)], xf, ffn_g3, ffn_b3, l, riders, _square)
        landed_in(keys, bufs)
        sv.update(pr=pr, xhat2=xhat, rstd2=rstd)
        saved.append(sv)

    dx = xf

    pending = []
    pair_sums, landed = {}, {}
    place_arr = jnp.stack([chip, cc.astype(jnp.int32)])

    def arrived(took, outs):
        for (kind, key, arr), out in zip(took, outs):
            if kind == "pair":
                pair_sums[key] = _pair_sum(arr, out, place_arr)
                pending.append(("chip", key, pair_sums[key][1]))
            else:
                landed[key] = out

    def carrying(call, name, *args, **kw):
        took = []
        if name.startswith("ffn") or draining[0]:
            room = CARRIER_PARAMS
            for task in list(pending):
                size = given[task[1][0]].shape[-2] * given[task[1][0]].shape[-1] * N_CHIPS
                if task[0] == "pair" or room == CARRIER_PARAMS or size <= room:
                    took.append(task)
                    pending.remove(task)
                    room -= size if task[0] == "chip" else 0
        results = call(name, *args, riders=[(kind, arr) for kind, _, arr in took], **kw)
        own = len(results) - len(took)
        arrived(took, results[own:])
        return results[0] if own == 1 else results[:own]

    draining = [False]

    def bwd_act(*args, **kw):
        return carrying(_mm_bwd_act, *args, **kw)

    def bwd_w(key, name, a, dy, **kw):
        pending.append(("pair", key, carrying(_mm_bwd_w, name, a, dy, gw[key], COL_SHARDED[key[0]], **kw)))

    d_mix_g, d_mix_b, d_ffn_g, d_ffn_b = [None] * DEPTH, [None] * DEPTH, [None] * DEPTH, [None] * DEPTH
    d_ln_g, d_ln_b, d_ws, d_bs = [None] * N_A, [None] * N_A, [None] * N_A, [None] * N_A
    dk = dv = normed = None
    for l in reversed(range(DEPTH)):
        sv = saved[l]
        draining[0] = l == 0
        if l == DEPTH - 1:
            dr, drb, d_ffn_g[l], d_ffn_b[l], sq = _ln_bwd(dx, sv["xhat2"], sv["rstd2"], ffn_g3, l, target)
            loss = lax.psum(0.5 * sq[0, 0] / d, ("x", "y", "c"))
        elif normed:
            dr, drb = normed
            normed = None
        else:
            dr, drb, d_ffn_g[l], d_ffn_b[l] = _ln_bwd(dx, sv["xhat2"], sv["rstd2"], ffn_g3, l)
        dff = sv["pr"].shape[1]
        dhd = bwd_act("ffn_2_dx", drb, gw[("ffn_w2", l)], None, False, _ep_relu2_bwd, (sv["pr"],),
                      (pl.BlockSpec((_wide_tile(s), dff // N_CHIPS), lambda j, i, k: (i, j)),), out_dtype=BF16)
        bwd_w(("ffn_w2", l), "ffn_2_dw", sv["pr"], drb, a_fn=_square)
        dr, drb, d_mix_g[l], d_mix_b[l] = bwd_act(
            "ffn_1_dx", dhd, gw[("ffn_w1", l)], None, True, _ep_resid_ln_bwd, (dr, sv["xhat1"], sv["rstd1"], mix_g3),
            (_row_spec(d), _row_spec(d), _row_spec(1), _vec_spec(l, d)), through_norm=True)
        bwd_w(("ffn_w1", l), "ffn_1_dw", sv["x_mid"], dhd)

        quarter = pl.BlockSpec((_wide_tile(s), d // N_CHIPS), lambda j, i, k: (i, j))
        if l < N_A:
            dgated = bwd_act("a_out_dx", drb, gw[("a_w_out", l)], None, False)
            bwd_w(("a_w_out", l), "a_out_dw", sv["gated"], drb)
            dh, d_ws[l], dbs_wide, dlg, dlb = _gmlp_bwd(dgated, sv["h"], sv["vn"], a_w_s[l], bst[l], a_ln_g3, l)
            d_bs[l] = dbs_wide[:, :, 0]
            d_ln_g[l], d_ln_b[l] = dlg[0], dlb[0]
            if l:
                below = saved[l - 1]
                *normed, d_ffn_g[l - 1], d_ffn_b[l - 1] = bwd_act(
                    "a_in_dx", dh, gw[("a_w_in", l)], None, True, _ep_resid_ln_bwd,
                    (dr, below["xhat2"], below["rstd2"], ffn_g3),
                    (_row_spec(d), _row_spec(d), _row_spec(1), _vec_spec(l - 1, d)), through_norm=True)
            else:
                dx = bwd_act("a_in_dx", dh, gw[("a_w_in", l)], None, True, _ep_resid, (dr,), (_row_spec(d),))
            bwd_w(("a_w_in", l), "a_in_dw", sv["x_in"], dh)
        else:
            j = l - N_A
            do = bwd_act("b_out_dx", drb, gw[("b_w_o", j)], None, False)
            bwd_w(("b_w_o", j), "b_out_dw", sv["ob"], drb)
            dq, dk, dv = _attn_bwd(sv["q"], kb, vb, do, sv["lsum"], dk, dv)
            dx = bwd_act("b_q_dx", dq, gw[("b_w_q", j)], None, False, _ep_resid, (dr,), (quarter,))
            bwd_w(("b_w_q", j), "b_q_dw", sv["x_in"], dq)
            if l == N_A:
                dx = bwd_act("sb_k_dx", dk, gw[("sb_w_k", None)], None, False, _ep_add, (dx,), (quarter,))
                bwd_w(("sb_w_k", None), "sb_k_dw", sv["x_in"], dk)
                dx = bwd_act("sb_v_dx", dv, gw[("sb_w_v", None)], None, False, _ep_add, (dx,), (quarter,))
                bwd_w(("sb_w_v", None), "sb_v_dw", sv["x_in"], dv)
    grad_x = dx.reshape(x.shape)

    while pending:
        took = list(pending)
        pending.clear()
        for kind, exchange in (("pair", _pair_exchange), ("chip", _chip_exchange)):
            some = [t for t in took if t[0] == kind]
            if some:
                arrived(some, exchange([arr for _, _, arr in some]))

    stacked = []
    for n in BIG:
        layers = [None] if given[n].ndim == 2 else range(given[n].shape[0])
        out = lax.empty((len(layers),) + given[n].shape[-2:], F32)
        for at, l in enumerate(layers):
            out = _chip_sum(pair_sums[(n, l)][0], landed[(n, l)], place_arr, out, at)
        stacked.append(out)
    grads = {n: g.reshape(given[n].shape) for n, g in zip(BIG, _half_swap(stacked))}

    small_full = dict(a_ln_g=jnp.stack(d_ln_g), a_ln_b=jnp.stack(d_ln_b), a_w_s=jnp.stack(d_ws), a_b_s=jnp.stack(d_bs),
                      mix_ln_g=jnp.concatenate(d_mix_g), mix_ln_b=jnp.concatenate(d_mix_b),
                      ffn_ln_g=jnp.concatenate(d_ffn_g), ffn_ln_b=jnp.concatenate(d_ffn_b))
    packed = jnp.concatenate([small_full[n].reshape(-1) for n in SMALL])
    total = packed.shape[0]
    ncol = -(-total // (N_DEV * LANES)) * LANES
    packed = jnp.pad(packed, (0, N_DEV * ncol - total)).reshape(N_DEV, ncol)
    reduced = _all_reduce_small(packed).reshape(-1)
    off = 0
    for n in SMALL:
        size = small_full[n].size
        g = reduced[off:off + size].reshape(small_full[n].shape)
        off += size
        if n in ("a_ln_g", "a_ln_b"):
            wq = given[n].shape[1]
            g = lax.dynamic_slice_in_dim(g, chip * wq, wq, axis=1)
        grads[n] = g

    delta, new_m, new_v = {}, {}, {}
    for n in names:
        shape = given[n].shape
        dl, nm, nv, *g = _adamw(as2d(given[n]), as2d(grads[n]), as2d(mom[n]), as2d(var[n]), pass_g=n in BIG)
        delta[n], new_m[n], new_v[n] = dl.reshape(shape), nm.reshape(shape), nv.reshape(shape)
        if g:
            grads[n] = g[0].reshape(shape)

    order = ("a_w_in", "a_ln_g", "a_ln_b", "a_w_s", "a_b_s", "a_w_out", "sb_w_k", "sb_w_v", "b_w_q", "b_w_o",
             "mix_ln_g", "mix_ln_b", "ffn_ln_g", "ffn_ln_b", "ffn_w1", "ffn_w2")
    return (loss, grad_x, *[grads[n] for n in order], *[delta[n] for n in order],
            *[new_m[n] for n in order], *[new_v[n] for n in order])
```

```python
import math

import jax
import jax.numpy as jnp
from jax import lax
from jax.experimental import pallas as pl
from jax.experimental.pallas import tpu as pltpu

F32 = jnp.float32
BF16 = jnp.bfloat16
MESH = pl.DeviceIdType.MESH

N_CHIPS = 4
DEPTH = 4
N_A = 2
ALPHA = float((2 * DEPTH) ** 0.25)
LN_EPS = 1e-5
CHUNK = 64
GMLP_BLOCK = 128
GMLP_GROUPS = 8
HEAD_DIM = 64
LANES = 128
ATT_T = 256
ADAM_LR = 0.001
ADAM_B1 = 0.9
ADAM_B2 = 0.999
ADAM_EPS = 1e-08
ADAM_WD = 0.01
ADAM_STEP = 10
VMEM_LIMIT = 56 * 1024 * 1024
TM = 512
TM_WIDE = 1024
TS = 2048

NN = ((1,), (0,))
NT = ((1,), (1,))
TN = ((0,), (0,))


def _params(sem):
    return pltpu.CompilerParams(dimension_semantics=sem, vmem_limit_bytes=VMEM_LIMIT)


def _dot(a, b, contract):
    return lax.dot_general(a, b, (contract, ((), ())), preferred_element_type=F32)


def _rider_out(kind, arr):
    shape = (arr.shape[0], arr.shape[1] // 2, arr.shape[2]) if kind == "pair" else arr.shape
    return jax.ShapeDtypeStruct(shape, arr.dtype)


def _rider_copies(kind, src, dst, send_sems, recv_sems, base):
    x, y, c, chips = _place()
    me = 2 * x + y
    sibling = (x, y, 1 - c)

    def copy(k, part, land, to):
        return pltpu.make_async_remote_copy(src_ref=part, dst_ref=land, send_sem=send_sems.at[base + k],
                                            recv_sem=recv_sems.at[base + k], device_id=to, device_id_type=MESH)

    if kind == "pair":
        h = src.shape[1] // 2
        cp = copy(0, src.at[:, pl.ds((1 - c) * h, h)], dst, sibling)
        return [cp], [cp]
    h = dst.shape[1] // 2
    starts, arrivals = [], []
    for k, chip in enumerate(chips):
        blk = 2 * chip[0] + chip[1]
        if kind == "ici":
            starts.append(copy(k, dst.at[me, pl.ds(c * h, h)], dst.at[me, pl.ds(c * h, h)], (*chip, c)))
            arrivals.append(copy(k, dst.at[blk, pl.ds(c * h, h)], dst.at[blk, pl.ds(c * h, h)], (*chip, c)))
        elif kind == "d2d":
            starts.append(copy(k, dst.at[blk, pl.ds(c * h, h)], dst.at[blk, pl.ds(c * h, h)], sibling))
            arrivals.append(copy(k, dst.at[blk, pl.ds((1 - c) * h, h)], dst.at[blk, pl.ds((1 - c) * h, h)], sibling))
        else:
            starts.append(copy(k, src.at[blk], dst.at[me], (*chip, c)))
            arrivals.append(copy(k, src.at[blk], dst.at[blk], (*chip, c)))
    return starts, arrivals


RIDER_SEMS = 3
CARRIER_PARAMS = 5 * 2 ** 20


def _identity(a):
    return a


def _square(a):
    return a * a


def _matmul(name, operands, in_specs, out_shapes, out_specs, grid, contract, epilogue, acc_shape, aliases=None,
            chunks=None, riders=(), pick=False, a_fn=_identity, sequential=False):
    nk = grid[2]
    n_in, n_out, nr = len(operands), len(out_shapes), len(riders)
    n_plain = n_in + nr + n_out

    def body(*refs):
        ins, outs = refs[:n_in], refs[n_in + nr:n_plain]
        if nr:
            srcs, dsts = refs[n_in:n_in + nr], refs[n_plain:n_plain + nr]
            send_sems, recv_sems = refs[-2:]
            pid = [pl.program_id(ax) for ax in range(3)]
            first = (pid[0] == 0) & (pid[1] == 0) & (pid[2] == 0)
            last = (pid[0] == grid[0] - 1) & (pid[1] == grid[1] - 1) & (pid[2] == grid[2] - 1)

            def copies(n):
                return _rider_copies(riders[n][0], srcs[n], dsts[n], send_sems, recv_sems, RIDER_SEMS * n)

            @pl.when(first)
            def _():
                for n in range(nr):
                    for cp in copies(n)[0]:
                        cp.start()

        compute(refs, ins, outs)
        if nr:
            @pl.when(last)
            def _():
                for n in range(nr):
                    starts, arrivals = copies(n)
                    for cp in arrivals:
                        cp.wait_recv()
                    for cp in starts:
                        cp.wait_send()

    def compute(refs, ins, outs):
        if chunks is None:
            b = ins[1][pl.program_id(1)] if pick else ins[1][...]
            p = _dot(a_fn(ins[0][...].astype(BF16)), b.astype(BF16), contract)
        else:
            width = ins[0].shape[1] // chunks
            p = None
            for j in range(chunks):
                pj = _dot(a_fn(ins[0][:, j * width:(j + 1) * width].astype(BF16)), ins[1][j].astype(BF16), contract)
                p = pj if p is None else p + pj
        if nk == 1:
            epilogue(p, ins[2:], outs)
            return
        acc = refs[n_plain + nr]
        k = pl.program_id(2)

        @pl.when(k == 0)
        def _():
            acc[...] = p

        @pl.when((k > 0) & (k < nk - 1))
        def _():
            acc[...] += p

        @pl.when(k == nk - 1)
        def _():
            epilogue(acc[...] + p, ins[2:], outs)

    rbufs = [b for _, b in riders]
    in_place = {n_in + n: n_out + n for n, (kind, _) in enumerate(riders) if kind in ("ici", "d2d")}
    scratch = ([] if nk == 1 else [pltpu.VMEM(acc_shape, F32)]) \
        + [pltpu.SemaphoreType.DMA((RIDER_SEMS * nr,))] * (2 if nr else 0)
    return pl.pallas_call(
        body, name=name, grid=grid, in_specs=list(in_specs) + _any_specs(nr), out_specs=list(out_specs) + _any_specs(nr),
        out_shape=list(out_shapes) + [_rider_out(kind, b) for kind, b in riders],
        scratch_shapes=scratch,
        input_output_aliases={**(aliases or {}), **in_place},
        compiler_params=_params(("arbitrary",) * 3 if nr or sequential else ("parallel", "parallel", "arbitrary")),
    )(*operands, *rbufs)


def _wspec(w, layer, whole=False):
    r, c = w.shape[-2:]
    lead = N_CHIPS if whole else None
    if w.ndim == 4:
        return pl.BlockSpec((lead, None, r, c), lambda j, i, k: (0 if whole else j, layer, 0, 0))
    return pl.BlockSpec((lead, r, c), lambda j, i, k: (0 if whole else j, 0, 0))


def _wide_tile(s):
    return min(TM_WIDE, s)


def _ep_store(p, ins, outs):
    for o in outs:
        o[...] = p.astype(o.dtype)


def _rows_first(spec):
    return pl.BlockSpec(spec.block_shape, lambda i, j, k: spec.index_map(j, i, k))


def _mm_fwd(name, a, w, layer, col_sharded, epilogue=_ep_store, extras=(), extra_specs=(), outs=None, riders=(),
            a_fn=_identity):
    s = a.shape[0]
    r, c = w.shape[-2:]
    if col_sharded:
        tm = _wide_tile(s)
        grid = (s // tm, N_CHIPS, 1)
        a_spec = pl.BlockSpec((tm, r), lambda j, i, k: (i, 0))
        n_out = N_CHIPS * c
    else:
        tm = TM
        grid = (1, s // tm, 1)
        a_spec = pl.BlockSpec((tm, N_CHIPS * r), lambda j, i, k: (i, 0))
        n_out = c
    if outs is None:
        outs = [(n_out, F32)]
    out_shapes = [jax.ShapeDtypeStruct((s, n), dt) for n, dt in outs]
    out_specs = [pl.BlockSpec((tm, c if n == n_out else n), lambda j, i, k: (i, j)) for n, _ in outs]
    in_specs = [a_spec, _wspec(w, layer, True)] + list(extra_specs)
    if col_sharded:
        in_specs, out_specs = [_rows_first(sp) for sp in in_specs], [_rows_first(sp) for sp in out_specs]
    return _matmul(name, (a, w) + tuple(extras), in_specs, out_shapes, out_specs, grid, NN, epilogue, (tm, c),
                   chunks=None if col_sharded else N_CHIPS, riders=riders, pick=col_sharded, a_fn=a_fn)


def _mm_bwd_act(name, dy, w, layer, col_sharded, epilogue=_ep_store, extras=(), extra_specs=(), out_dtype=F32,
                riders=(), through_norm=False):
    s = dy.shape[0]
    r, c = w.shape[-2:]
    if col_sharded:
        tm = TM
        grid = (1, s // tm, 1)
        a_spec = pl.BlockSpec((tm, N_CHIPS * c), lambda j, i, k: (i, 0))
        n_out = r
    else:
        tm = _wide_tile(s)
        grid = (s // tm, N_CHIPS, 1)
        a_spec = pl.BlockSpec((tm, c), lambda j, i, k: (i, 0))
        n_out = N_CHIPS * r
    in_specs = [a_spec, _wspec(w, layer, True)] + list(extra_specs)
    o_spec = pl.BlockSpec((tm, r), lambda j, i, k: (i, j))
    if not col_sharded:
        in_specs, o_spec = [_rows_first(sp) for sp in in_specs], _rows_first(o_spec)
    out_shapes, out_specs = [jax.ShapeDtypeStruct((s, n_out), out_dtype)], [o_spec]
    if through_norm:
        vec = pl.BlockSpec((1, n_out), lambda j, i, k: (0, 0))
        out_shapes = [jax.ShapeDtypeStruct((s, n_out), F32), jax.ShapeDtypeStruct((s, n_out), BF16),
                      jax.ShapeDtypeStruct((1, n_out), F32), jax.ShapeDtypeStruct((1, n_out), F32)]
        out_specs = [o_spec, o_spec, vec, vec]
    return _matmul(name, (dy, w) + tuple(extras), in_specs, out_shapes, out_specs, grid, NT, epilogue, (tm, r),
                   chunks=N_CHIPS if col_sharded else None, riders=riders, pick=not col_sharded,
                   sequential=through_norm)


def _mm_bwd_w(name, a, dy, w, col_sharded, riders=(), a_fn=_identity):
    s = a.shape[0]
    r, c = w.shape[-2:]
    ts = min(TS, s)
    grid = (N_CHIPS, 1, s // ts)
    if col_sharded:
        a_spec = pl.BlockSpec((ts, r), lambda j, i, k: (k, 0))
        b_spec = pl.BlockSpec((ts, c), lambda j, i, k: (k, j))
    else:
        a_spec = pl.BlockSpec((ts, r), lambda j, i, k: (k, j))
        b_spec = pl.BlockSpec((ts, c), lambda j, i, k: (k, 0))

    def epilogue(p, ins, outs):
        outs[0][...] = p

    return _matmul(name, (a, dy), [a_spec, b_spec], [jax.ShapeDtypeStruct(w.shape, F32)], [_wspec(w, None)], grid, TN,
                   epilogue, (r, c), riders=riders, a_fn=a_fn)


def _row_spec(n):
    return pl.BlockSpec((TM, n), lambda j, i, k: (i, 0))


def _vec_spec(layer, n):
    return pl.BlockSpec((None, 1, n), lambda j, i, k: (layer, 0, 0))


def _ep_resid_ln(p, ins, outs):
    x_ref, g_ref, b_ref = ins
    xf_ref, xb_ref, xhat_ref, rstd_ref = outs
    r = ALPHA * x_ref[...] + p
    mu = jnp.mean(r, axis=-1, keepdims=True)
    d = r - mu
    var = jnp.mean(d * d, axis=-1, keepdims=True)
    rstd = lax.rsqrt(var + LN_EPS)
    xhat = d * rstd
    y = xhat * g_ref[...] + b_ref[...]
    xf_ref[...] = y
    xb_ref[...] = y.astype(BF16)
    xhat_ref[...] = xhat
    rstd_ref[...] = rstd


def _mm_resid_ln(name, a, w, x, g3, b3, ln_layer, riders=(), a_fn=_identity):
    d = x.shape[1]
    return _mm_fwd(name, a, w, None, False, _ep_resid_ln, (x, g3, b3),
                   (_row_spec(d), _vec_spec(ln_layer, d), _vec_spec(ln_layer, d)),
                   outs=[(d, F32), (d, BF16), (d, F32), (1, F32)], riders=riders, a_fn=a_fn)


def _ep_relu(p, ins, outs):
    outs[0][...] = jnp.maximum(p, 0.0).astype(BF16)


def _ep_scale_q(p, ins, outs):
    outs[0][...] = (p * (HEAD_DIM ** -0.5)).astype(BF16)


def _ep_bf16(p, ins, outs):
    outs[0][...] = p.astype(BF16)


def _ep_relu2_bwd(p, ins, outs):
    outs[0][...] = (p * (2.0 * ins[0][...].astype(F32))).astype(BF16)


def _ep_resid(p, ins, outs):
    outs[0][...] = ALPHA * ins[0][...] + p


def _ep_resid_ln_bwd(p, ins, outs):
    dr_ref, xh_ref, rs_ref, g_ref = ins
    _ln_bwd_rows(ALPHA * dr_ref[...] + p, xh_ref, rs_ref, g_ref, pl.program_id(1) == 0, *outs)


def _ep_add(p, ins, outs):
    outs[0][...] = ins[0][...] + p


def _gelu_grad(x):
    c0 = math.sqrt(2.0 / math.pi)
    t = jnp.tanh(c0 * (x + 0.044715 * (x * x * x)))
    return 0.5 * (1.0 + t) + (0.5 * x) * (1.0 - t * t) * (c0 * (1.0 + 3.0 * 0.044715 * (x * x)))


def _cast_bf16(w2d):
    r, c = w2d.shape
    tr = min(r, 512)

    def body(w_ref, o_ref):
        o_ref[...] = w_ref[...].astype(BF16)

    return pl.pallas_call(
        body, name="cast_bf16", grid=(r // tr,),
        in_specs=[pl.BlockSpec((tr, c), lambda i: (i, 0))], out_specs=pl.BlockSpec((tr, c), lambda i: (i, 0)),
        out_shape=jax.ShapeDtypeStruct((r, c), BF16), compiler_params=_params(("parallel",)),
    )(w2d)


def _cast_into_slot(w, layer, chip):
    r, c = w.shape[-2:]
    tr = min(r, 512)

    def body(chip_ref, w_ref, o_ref):
        o_ref[...] = w_ref[...].astype(BF16)

    if layer is None:
        w_spec = pl.BlockSpec((tr, c), lambda i, chip_ref: (i, 0))
    else:
        w_spec = pl.BlockSpec((None, tr, c), lambda i, chip_ref: (layer, i, 0))
    grid_spec = pltpu.PrefetchScalarGridSpec(
        num_scalar_prefetch=1, grid=(r // tr,), in_specs=[w_spec],
        out_specs=pl.BlockSpec((None, tr, c), lambda i, chip_ref: (chip_ref[0], i, 0)))
    return pl.pallas_call(
        body, name="cast_into_slot", grid_spec=grid_spec,
        out_shape=jax.ShapeDtypeStruct((N_CHIPS, r, c), BF16), compiler_params=_params(("parallel",)),
    )(chip, w)


def _gmlp_norm_fwd(h, g3, b3, layer):
    s, w2 = h.shape
    w = w2 // 2

    def body(h_ref, g_ref, b_ref, o_ref):
        z = jax.nn.gelu(h_ref[...])
        mu = jnp.mean(z, axis=-1, keepdims=True)
        d = z - mu
        var = jnp.mean(d * d, axis=-1, keepdims=True)
        o_ref[...] = (d * lax.rsqrt(var + LN_EPS) * g_ref[...] + b_ref[...]).astype(BF16)

    vec = pl.BlockSpec((None, 1, w), lambda i: (layer, 0, 0))
    return pl.pallas_call(
        body, name="gmlp_norm_fwd", grid=(s // TM,),
        in_specs=[pl.BlockSpec((TM, w), lambda i: (i, 1)), vec, vec],
        out_specs=pl.BlockSpec((TM, w), lambda i: (i, 0)),
        out_shape=jax.ShapeDtypeStruct((s, w), BF16), compiler_params=_params(("parallel",)),
    )(h, g3, b3)


def _chunk_mask():
    t = lax.broadcasted_iota(jnp.int32, (GMLP_BLOCK, GMLP_BLOCK), 0)
    s = lax.broadcasted_iota(jnp.int32, (GMLP_BLOCK, GMLP_BLOCK), 1)
    return (s // CHUNK) <= (t // CHUNK)


SG_ROWS = 512


def _gate_fwd(h, vn, ws, bst):
    s, w = vn.shape
    gd = w // GMLP_GROUPS

    def body(h_ref, v_ref, ws_ref, bs_ref, o_ref):
        mask = _chunk_mask()
        for g in range(GMLP_GROUPS):
            wm = jnp.where(mask, ws_ref[g], 0.0).astype(BF16)
            bias = bs_ref[:, g:g + 1]
            cols = slice(g * gd, (g + 1) * gd)
            for n in range(SG_ROWS // GMLP_BLOCK):
                rows = slice(n * GMLP_BLOCK, (n + 1) * GMLP_BLOCK)
                sp = _dot(wm, v_ref[rows, cols], NN) + bias
                o_ref[rows, cols] = (jax.nn.gelu(h_ref[rows, cols]) * sp).astype(BF16)

    return pl.pallas_call(
        body, name="gate_fwd", grid=(s // SG_ROWS,),
        in_specs=[pl.BlockSpec((SG_ROWS, w), lambda i: (i, 0)), pl.BlockSpec((SG_ROWS, w), lambda i: (i, 0)),
                  pl.BlockSpec(ws.shape, lambda i: (0, 0, 0)), pl.BlockSpec(bst.shape, lambda i: (0, 0))],
        out_specs=pl.BlockSpec((SG_ROWS, w), lambda i: (i, 0)),
        out_shape=jax.ShapeDtypeStruct((s, w), BF16), compiler_params=_params(("parallel",)),
    )(h, vn, ws, bst)


GB_ROWS = 256


def _gmlp_bwd(dgated, h, vn, ws, bst, g3, layer):
    s, w = vn.shape
    gd = w // GMLP_GROUPS
    nsteps = s // SG_ROWS

    def body(dg_ref, h_ref, v_ref, ws_ref, bs_ref, g_ref, dh_ref, dws_ref, dbs_ref, dlg_ref, dlb_ref, dsum, du_s, dv_s):
        i = pl.program_id(0)

        @pl.when(i == 0)
        def _():
            dws_ref[...] = jnp.zeros_like(dws_ref)
            dsum[...] = jnp.zeros_like(dsum)
            dlg_ref[...] = jnp.zeros_like(dlg_ref)
            dlb_ref[...] = jnp.zeros_like(dlb_ref)

        mask = _chunk_mask()
        for g in range(GMLP_GROUPS):
            wm = jnp.where(mask, ws_ref[g], 0.0).astype(BF16)
            bias = bs_ref[:, g:g + 1]
            cols = slice(g * gd, (g + 1) * gd)
            dw = jnp.zeros((GMLP_BLOCK, GMLP_BLOCK), F32)
            dsg = jnp.zeros((GMLP_BLOCK, gd), F32)
            for n in range(SG_ROWS // GMLP_BLOCK):
                rows = slice(n * GMLP_BLOCK, (n + 1) * GMLP_BLOCK)
                vb = v_ref[rows, cols]
                sp = _dot(wm, vb, NN) + bias
                dg = dg_ref[rows, cols]
                du_s[rows, cols] = dg * sp
                ds = dg * jax.nn.gelu(h_ref[rows, cols])
                dsb = ds.astype(BF16)
                dw += _dot(dsb, vb, NT)
                dsg += ds
                dv_s[rows, cols] = _dot(wm, dsb, TN)
            dws_ref[g] += dw
            dsum[:, cols] += dsg

        for r0 in range(0, SG_ROWS, GB_ROWS):
            rows = slice(r0, r0 + GB_ROWS)
            hu = h_ref[rows, :w]
            hv = h_ref[rows, w:]
            dh_ref[rows, :w] = (du_s[rows, :] * _gelu_grad(hu)).astype(BF16)
            z = jax.nn.gelu(hv)
            mu = jnp.mean(z, axis=-1, keepdims=True)
            d = z - mu
            var = jnp.mean(d * d, axis=-1, keepdims=True)
            rstd = lax.rsqrt(var + LN_EPS)
            xhat = d * rstd
            dy = dv_s[rows, :]
            dlb_ref[...] += jnp.sum(dy, axis=0, keepdims=True)
            dlg_ref[...] += jnp.sum(dy * xhat, axis=0, keepdims=True)
            dxh = dy * g_ref[...]
            m1 = jnp.mean(dxh, axis=-1, keepdims=True)
            m2 = jnp.mean(dxh * xhat, axis=-1, keepdims=True)
            dz = rstd * (dxh - m1 - xhat * m2)
            dh_ref[rows, w:] = (dz * _gelu_grad(hv)).astype(BF16)

        @pl.when(i == nsteps - 1)
        def _():
            for g in range(GMLP_GROUPS):
                dws_ref[g] = jnp.where(mask, dws_ref[g], 0.0)
                tot = jnp.sum(dsum[:, g * gd:(g + 1) * gd], axis=-1, keepdims=True)
                dbs_ref[g] = jnp.broadcast_to(tot, (GMLP_BLOCK, LANES))

    tile = pl.BlockSpec((SG_ROWS, w), lambda i: (i, 0))
    wide = pl.BlockSpec((SG_ROWS, 2 * w), lambda i: (i, 0))
    vec = pl.BlockSpec((1, w), lambda i: (0, 0))
    return pl.pallas_call(
        body, name="gmlp_bwd", grid=(nsteps,),
        in_specs=[tile, wide, tile, pl.BlockSpec(ws.shape, lambda i: (0, 0, 0)), pl.BlockSpec(bst.shape, lambda i: (0, 0)),
                  pl.BlockSpec((None, 1, w), lambda i: (layer, 0, 0))],
        out_specs=[wide, pl.BlockSpec(ws.shape, lambda i: (0, 0, 0)),
                   pl.BlockSpec((GMLP_GROUPS, GMLP_BLOCK, LANES), lambda i: (0, 0, 0)), vec, vec],
        out_shape=[jax.ShapeDtypeStruct((s, 2 * w), BF16), jax.ShapeDtypeStruct(ws.shape, F32),
                   jax.ShapeDtypeStruct((GMLP_GROUPS, GMLP_BLOCK, LANES), F32),
                   jax.ShapeDtypeStruct((1, w), F32), jax.ShapeDtypeStruct((1, w), F32)],
        scratch_shapes=[pltpu.VMEM((GMLP_BLOCK, w), F32), pltpu.VMEM((SG_ROWS, w), F32), pltpu.VMEM((SG_ROWS, w), F32)],
        compiler_params=_params(("arbitrary",)),
    )(dgated, h, vn, ws, bst, g3)


def _ln_bwd_rows(dy, xh_ref, rs_ref, g_ref, first, dr_ref, drb_ref, dg_ref, db_ref):
    @pl.when(first)
    def _():
        dg_ref[...] = jnp.zeros_like(dg_ref)
        db_ref[...] = jnp.zeros_like(db_ref)

    xhat = xh_ref[...]
    db_ref[...] += jnp.sum(dy, axis=0, keepdims=True)
    dg_ref[...] += jnp.sum(dy * xhat, axis=0, keepdims=True)
    dxh = dy * g_ref[...]
    m1 = jnp.mean(dxh, axis=-1, keepdims=True)
    m2 = jnp.mean(dxh * xhat, axis=-1, keepdims=True)
    dr = rs_ref[...] * (dxh - m1 - xhat * m2)
    dr_ref[...] = dr
    drb_ref[...] = dr.astype(BF16)


def _ln_bwd(dy, xhat, rstd, g3, layer, target=None):
    s, d = dy.shape
    nsteps = s // TM
    head = target is not None

    def body(*refs):
        dy_ref, xh_ref, rs_ref, g_ref = refs[:4]
        dr_ref, drb_ref, dg_ref, db_ref = refs[4 + head:8 + head]
        first = pl.program_id(0) == 0
        dyv = dy_ref[...]
        if head:
            l_ref = refs[8 + head]

            @pl.when(first)
            def _():
                l_ref[...] = jnp.zeros_like(l_ref)

            e = dyv - refs[4][...]
            l_ref[...] += jnp.sum(jnp.sum(e * e, axis=1, keepdims=True), axis=0, keepdims=True)
            dyv = e * (1.0 / d)
        _ln_bwd_rows(dyv, xh_ref, rs_ref, g_ref, first, dr_ref, drb_ref, dg_ref, db_ref)

    tile = pl.BlockSpec((TM, d), lambda i: (i, 0))
    vec = pl.BlockSpec((1, d), lambda i: (0, 0))
    one = pl.BlockSpec((1, 1), lambda i: (0, 0))
    return pl.pallas_call(
        body, name="ln_bwd", grid=(nsteps,),
        in_specs=[tile, tile, pl.BlockSpec((TM, 1), lambda i: (i, 0)), pl.BlockSpec((None, 1, d), lambda i: (layer, 0, 0))]
        + ([tile] if head else []),
        out_specs=[tile, tile, vec, vec] + ([one] if head else []),
        out_shape=[jax.ShapeDtypeStruct((s, d), F32), jax.ShapeDtypeStruct((s, d), BF16),
                   jax.ShapeDtypeStruct((1, d), F32), jax.ShapeDtypeStruct((1, d), F32)]
        + ([jax.ShapeDtypeStruct((1, 1), F32)] if head else []),
        compiler_params=_params(("arbitrary",)),
    )(dy, xhat, rstd, g3, *([target] if head else []))


LOG2E = 1.4426950408889634
DEAD_LOG2 = -160.0
FIRST_LANE = 1


def _sb_terms(z, causal):
    z2 = z * LOG2E
    e = jnp.exp2(-jnp.abs(z2))
    l1p = jnp.log2(1.0 + e)
    lb = jnp.minimum(z2, 0.0) - l1p
    lr = lb - z2
    if causal is not None:
        lr = jnp.where(causal, lr, 0.0)
    return lb, lr, e


def _split_hi_lo(x):
    hi = x.astype(BF16)
    lo = (x - hi.astype(F32)).astype(BF16)
    return jnp.concatenate([hi, lo], axis=1)


def _tri2(prefix):
    r = lax.broadcasted_iota(jnp.int32, (2 * ATT_T, ATT_T), 0) % ATT_T
    c = lax.broadcasted_iota(jnp.int32, (2 * ATT_T, ATT_T), 1)
    return jnp.where((r <= c) if prefix else (r >= c), 1.0, 0.0).astype(BF16)


def _att_masks():
    r = lax.broadcasted_iota(jnp.int32, (ATT_T, ATT_T), 0)
    c = lax.broadcasted_iota(jnp.int32, (ATT_T, ATT_T), 1)
    return c < r, lax.broadcasted_iota(jnp.int32, (1, LANES), 1) < HEAD_DIM


def _attn_fwd(q, k, v):
    s, d = q.shape
    nq = s // ATT_T

    def body(q_ref, k_ref, v_ref, tri_ref, ob_ref, lsum_ref, acc_a, acc_b, rem_a, rem_b):
        i = pl.program_id(1)
        tri = tri_ref[...]
        causal, head_a = _att_masks()
        q2 = q_ref[...]
        zero = jnp.zeros_like(q2)
        qa = jnp.where(head_a, q2, zero)
        qb = jnp.where(head_a, zero, q2)
        acc_a[...] = jnp.zeros_like(acc_a)
        acc_b[...] = jnp.zeros_like(acc_b)
        rem_a[...] = jnp.zeros_like(rem_a)
        rem_b[...] = jnp.zeros_like(rem_b)

        def block(kb, mask):
            rows = pl.ds(pl.multiple_of(kb * ATT_T, ATT_T), ATT_T)
            k2 = k_ref[rows, :]
            v2 = v_ref[rows, :]
            heads = ((qa, acc_a, rem_a), (qb, acc_b, rem_b))
            zs = [_dot(qm, k2, NT) for qm, _, _ in heads]
            terms = [_sb_terms(z, mask) for z in zs]
            sums = [_dot(_split_hi_lo(lr), tri, NN) for _, lr, _ in terms]
            for (_, acc, rem), (lb, lr, _), sincl in zip(heads, terms, sums):
                a = jnp.exp2(lb + (sincl - lr) + rem[...])
                if mask is not None:
                    a = jnp.where(mask, a, 0.0)
                rem[...] += sincl[:, 0:1]
                acc[...] += _dot(a.astype(BF16), v2, NN)

        block(i, causal)

        def live():
            return jnp.maximum(jnp.max(rem_a[...]), jnp.max(rem_b[...])) > DEAD_LOG2

        def go_on(carry):
            t, alive = carry
            return (t < i) & alive

        def step(carry):
            t, _ = carry
            block(i - 1 - t, None)
            return t + 1, live()

        done, _ = lax.while_loop(go_on, step, (jnp.int32(0), live()))
        first = (i - done).astype(F32)
        ob_ref[...] = jnp.where(head_a, acc_a[...], acc_b[...]).astype(BF16)
        lane = lax.broadcasted_iota(jnp.int32, (1, LANES), 1)
        lsum_ref[...] = jnp.where(lane == FIRST_LANE, first, jnp.where(head_a, rem_a[...], rem_b[...]))

    qspec = pl.BlockSpec((ATT_T, LANES), lambda p, i: (i, p))
    kspec = pl.BlockSpec((s, LANES), lambda p, i: (0, p))
    return pl.pallas_call(
        body, name="attn_fwd", grid=(d // LANES, nq),
        in_specs=[qspec, kspec, kspec, pl.BlockSpec((2 * ATT_T, ATT_T), lambda p, i: (0, 0))],
        out_specs=[qspec, qspec],
        out_shape=[jax.ShapeDtypeStruct((s, d), BF16), jax.ShapeDtypeStruct((s, d), F32)],
        scratch_shapes=[pltpu.VMEM((ATT_T, LANES), F32), pltpu.VMEM((ATT_T, LANES), F32),
                        pltpu.VMEM((ATT_T, 1), F32), pltpu.VMEM((ATT_T, 1), F32)],
        compiler_params=_params(("parallel", "arbitrary")),
    )(q, k, v, _tri2(prefix=False))


def _attn_bwd(q, k, v, do, lsum, dk_prev=None, dv_prev=None):
    s, d = q.shape
    nq = s // ATT_T
    has_prev = dk_prev is not None

    def body(*refs):
        q_ref, k_ref, v_ref, do_ref, ls_ref, tri_ref = refs[:6]
        n_in = 8 if has_prev else 6
        dq_ref, dk_ref, dv_ref, acc_a, acc_b, pre_a, pre_b, gp_a, gp_b, dkt, dvt = refs[n_in:]
        i = pl.program_id(1)

        @pl.when(i == 0)
        def _():
            dkt[...] = jnp.zeros_like(dkt)
            dvt[...] = jnp.zeros_like(dvt)

        tri = tri_ref[...]
        causal, head_a = _att_masks()
        q2 = q_ref[...]
        zero = jnp.zeros_like(q2)
        qa = jnp.where(head_a, q2, zero)
        qb = jnp.where(head_a, zero, q2)
        do2 = do_ref[...]
        doa = jnp.where(head_a, do2, 0.0).astype(BF16)
        dob = jnp.where(head_a, 0.0, do2).astype(BF16)
        row_a = lax.broadcasted_iota(jnp.int32, (LANES, 1), 0) < HEAD_DIM
        qt = q2.astype(F32).T
        dot_ = do2.T
        qta, qtb = jnp.where(row_a, qt, 0.0).astype(BF16), jnp.where(row_a, 0.0, qt).astype(BF16)
        dota, dotb = jnp.where(row_a, dot_, 0.0).astype(BF16), jnp.where(row_a, 0.0, dot_).astype(BF16)
        ls2 = ls_ref[...]
        tot_a = ls2[:, 0:1]
        tot_b = ls2[:, HEAD_DIM:HEAD_DIM + 1]
        for r in (acc_a, acc_b, gp_a, gp_b):
            r[...] = jnp.zeros_like(r)
        pre_a[...] = tot_a
        pre_b[...] = tot_b

        def block(kb, mask):
            rows = pl.ds(pl.multiple_of(kb * ATT_T, ATT_T), ATT_T)
            k2 = k_ref[rows, :]
            v2 = v_ref[rows, :]
            dk_new = jnp.zeros((LANES, ATT_T), F32)
            dv_new = jnp.zeros((LANES, ATT_T), F32)
            heads = ((qa, doa, tot_a, acc_a, pre_a, gp_a, qta, dota), (qb, dob, tot_b, acc_b, pre_b, gp_b, qtb, dotb))
            zs = [_dot(h[0], k2, NT) for h in heads]
            das = [_dot(h[1], v2, NT) for h in heads]
            terms = [_sb_terms(z, mask) for z in zs]
            psums = [_dot(_split_hi_lo(lr), tri, NN) for _, lr, _ in terms]
            gs, abs_ = [], []
            for h, (lb, _, _), pincl, da in zip(heads, terms, psums, das):
                pre = h[4]
                a = jnp.exp2(lb + (pre[...] - pincl))
                if mask is not None:
                    a = jnp.where(mask, a, 0.0)
                pre[...] -= pincl[:, ATT_T - 1:ATT_T]
                gs.append(a * da)
                abs_.append(a.astype(BF16))
            gsums = [_dot(g.astype(BF16), tri[:ATT_T], NN) for g in gs]
            dzs = []
            for h, z, (_, _, e), g, gincl in zip(heads, zs, terms, gs, gsums):
                gpre = h[5]
                gbefore = gpre[...] + (gincl - g)
                gpre[...] += gincl[:, ATT_T - 1:ATT_T]
                inv = 1.0 / (1.0 + e)
                beta = jnp.where(z >= 0.0, inv, e * inv)
                dz = g - beta * (g + gbefore)
                if mask is not None:
                    dz = jnp.where(mask, dz, 0.0)
                dzs.append(dz.astype(BF16))
            for h, ab, dzb in zip(heads, abs_, dzs):
                dv_new += _dot(h[7], ab, NN)
                dk_new += _dot(h[6], dzb, NN)
                h[3][...] += _dot(dzb, k2, NN)
            cols = pl.ds(pl.multiple_of(kb * ATT_T, ATT_T), ATT_T)
            dkt[:, cols] += dk_new
            dvt[:, cols] += dv_new

        def step(kb, carry):
            block(kb, None)
            return carry

        first = jnp.clip(jnp.max(ls2[:, FIRST_LANE:FIRST_LANE + 1]).astype(jnp.int32), 0, i)
        lax.fori_loop(first, i, step, 0)
        block(i, causal)
        dq_ref[...] = (jnp.where(head_a, acc_a[...], acc_b[...]) * (HEAD_DIM ** -0.5)).astype(BF16)

        @pl.when(i == nq - 1)
        def _():
            for n in range(nq):
                rows = slice(n * ATT_T, (n + 1) * ATT_T)
                dkn, dvn = dkt[:, rows].T, dvt[:, rows].T
                if has_prev:
                    dkn, dvn = dkn + refs[6][rows, :], dvn + refs[7][rows, :]
                dk_ref[rows, :] = dkn
                dv_ref[rows, :] = dvn

    qspec = pl.BlockSpec((ATT_T, LANES), lambda p, i: (i, p))
    kspec = pl.BlockSpec((s, LANES), lambda p, i: (0, p))
    ins = [q, k, v, do, lsum, _tri2(prefix=True)] + ([dk_prev, dv_prev] if has_prev else [])
    return pl.pallas_call(
        body, name="attn_bwd", grid=(d // LANES, nq),
        in_specs=[qspec, kspec, kspec, qspec, qspec, pl.BlockSpec((2 * ATT_T, ATT_T), lambda p, i: (0, 0))]
        + ([kspec, kspec] if has_prev else []),
        out_specs=[qspec, kspec, kspec],
        out_shape=[jax.ShapeDtypeStruct((s, d), BF16), jax.ShapeDtypeStruct((s, d), F32), jax.ShapeDtypeStruct((s, d), F32)],
        scratch_shapes=[pltpu.VMEM((ATT_T, LANES), F32), pltpu.VMEM((ATT_T, LANES), F32)]
        + [pltpu.VMEM((ATT_T, 1), F32)] * 4 + [pltpu.VMEM((LANES, s), F32)] * 2,
        compiler_params=_params(("parallel", "arbitrary")),
    )(*ins)


def _place():
    x, y, c = lax.axis_index("x"), lax.axis_index("y"), lax.axis_index("c")
    chips = [(1 - x, y), (x, 1 - y), (1 - x, 1 - y)]
    return x, y, c, chips


def _any_specs(n):
    return [pl.BlockSpec(memory_space=pl.ANY)] * n


def _gather_weights(bufs):
    n = len(bufs)

    def body(*refs):
        outs = refs[n:2 * n]
        send_sems, recv_sems = refs[2 * n:]
        x, y, c, chips = _place()
        me = 2 * x + y
        sibling = (x, y, 1 - c)

        def half(a, blk, hc):
            h = outs[a].shape[1] // 2
            return outs[a].at[blk, pl.ds(hc * h, h)]

        def copy(a, k, part, to):
            return pltpu.make_async_remote_copy(src_ref=part, dst_ref=part, send_sem=send_sems.at[a, k],
                                                recv_sem=recv_sems.at[a, k], device_id=to, device_id_type=MESH)

        sent = []
        for a in range(n):
            for k, chip in enumerate(chips):
                sent.append(copy(a, k, half(a, me, c), (*chip, c)))
                sent[-1].start()
        for a in range(n):
            for k, chip in enumerate(chips):
                blk = 2 * chip[0] + chip[1]
                copy(a, k, half(a, blk, c), sibling).wait_recv()
                sent.append(copy(a, 3 + k, half(a, blk, c), sibling))
                sent[-1].start()
        for a in range(n):
            for k, chip in enumerate(chips):
                blk = 2 * chip[0] + chip[1]
                copy(a, 3 + k, half(a, blk, 1 - c), sibling).wait_recv()
        for cp in sent:
            cp.wait_send()

    return pl.pallas_call(
        body, name="gather_weights", in_specs=_any_specs(n), out_specs=_any_specs(n),
        out_shape=[jax.ShapeDtypeStruct(w.shape, w.dtype) for w in bufs],
        input_output_aliases={a: a for a in range(n)},
        scratch_shapes=[pltpu.SemaphoreType.DMA((n, 6)), pltpu.SemaphoreType.DMA((n, 6))],
        compiler_params=pltpu.CompilerParams(has_side_effects=True),
    )(*bufs)


def _pair_exchange(grads):
    n = len(grads)

    def body(*refs):
        ins, outs = refs[:n], refs[n:2 * n]
        send_sems, recv_sems = refs[2 * n:]
        x, y, c, _ = _place()
        cps = []
        for a in range(n):
            h = ins[a].shape[1] // 2
            cps.append(pltpu.make_async_remote_copy(
                src_ref=ins[a].at[:, pl.ds((1 - c) * h, h)], dst_ref=outs[a], send_sem=send_sems.at[a],
                recv_sem=recv_sems.at[a], device_id=(x, y, 1 - c), device_id_type=MESH))
            cps[-1].start()
        for cp in cps:
            cp.wait()

    return pl.pallas_call(
        body, name="pair_exchange", in_specs=_any_specs(n), out_specs=_any_specs(n),
        out_shape=[jax.ShapeDtypeStruct((g.shape[0], g.shape[1] // 2, g.shape[2]), g.dtype) for g in grads],
        scratch_shapes=[pltpu.SemaphoreType.DMA((n,)), pltpu.SemaphoreType.DMA((n,))],
        compiler_params=pltpu.CompilerParams(has_side_effects=True),
    )(*grads)


def _chip_exchange(parts):
    n = len(parts)

    def body(*refs):
        ins, outs = refs[:n], refs[n:2 * n]
        send_sems, recv_sems = refs[2 * n:]
        x, y, c, chips = _place()
        me = 2 * x + y
        cps = []
        for a in range(n):
            for k, chip in enumerate(chips):
                blk = 2 * chip[0] + chip[1]
                cps.append(pltpu.make_async_remote_copy(
                    src_ref=ins[a].at[blk], dst_ref=outs[a].at[me], send_sem=send_sems.at[a, k],
                    recv_sem=recv_sems.at[a, k], device_id=(*chip, c), device_id_type=MESH))
                cps[-1].start()
        for a in range(n):
            for k, chip in enumerate(chips):
                blk = 2 * chip[0] + chip[1]
                pltpu.make_async_remote_copy(
                    src_ref=ins[a].at[blk], dst_ref=outs[a].at[blk], send_sem=send_sems.at[a, k],
                    recv_sem=recv_sems.at[a, k], device_id=(*chip, c), device_id_type=MESH).wait_recv()
        for cp in cps:
            cp.wait_send()

    return pl.pallas_call(
        body, name="chip_exchange", in_specs=_any_specs(n), out_specs=_any_specs(n),
        out_shape=[jax.ShapeDtypeStruct(p.shape, p.dtype) for p in parts],
        scratch_shapes=[pltpu.SemaphoreType.DMA((n, 3)), pltpu.SemaphoreType.DMA((n, 3))],
        compiler_params=pltpu.CompilerParams(has_side_effects=True),
    )(*parts)


def _half_swap(halves):
    n = len(halves)

    def body(*refs):
        outs = refs[n:2 * n]
        send_sems, recv_sems = refs[2 * n:]
        x, y, c, _ = _place()
        cps = []
        for a in range(n):
            h = outs[a].shape[1] // 2
            mine = outs[a].at[:, pl.ds(c * h, h)]
            cps.append(pltpu.make_async_remote_copy(
                src_ref=mine, dst_ref=mine, send_sem=send_sems.at[a], recv_sem=recv_sems.at[a],
                device_id=(x, y, 1 - c), device_id_type=MESH))
            cps[-1].start()
        for cp in cps:
            cp.wait()

    return pl.pallas_call(
        body, name="half_swap", in_specs=_any_specs(n), out_specs=_any_specs(n),
        out_shape=[jax.ShapeDtypeStruct(p.shape, p.dtype) for p in halves],
        input_output_aliases={a: a for a in range(n)},
        scratch_shapes=[pltpu.SemaphoreType.DMA((n,)), pltpu.SemaphoreType.DMA((n,))],
        compiler_params=pltpu.CompilerParams(has_side_effects=True),
    )(*halves)


N_DEV = 8


def _all_reduce_small(v):
    nrow, ncol = v.shape

    def body(v_ref, o_ref, land, red, send_sems, recv_sems, send2, recv2, loc_sem):
        x, y, c, _ = _place()
        me = 4 * x + 2 * y + c
        peers = []
        for k in range(1, N_DEV):
            peers.append((x ^ ((k >> 2) & 1), y ^ ((k >> 1) & 1), c ^ (k & 1)))
        own = pltpu.make_async_copy(v_ref.at[pl.ds(me, 1)], land.at[pl.ds(me, 1)], loc_sem)
        own.start()
        cps = []
        for k, peer in enumerate(peers):
            dev = 4 * peer[0] + 2 * peer[1] + peer[2]
            cps.append(pltpu.make_async_remote_copy(
                src_ref=v_ref.at[pl.ds(dev, 1)], dst_ref=land.at[pl.ds(me, 1)], send_sem=send_sems.at[k],
                recv_sem=recv_sems.at[k], device_id=peer, device_id_type=MESH))
            cps[-1].start()
        for k, peer in enumerate(peers):
            dev = 4 * peer[0] + 2 * peer[1] + peer[2]
            pltpu.make_async_remote_copy(
                src_ref=v_ref.at[pl.ds(dev, 1)], dst_ref=land.at[pl.ds(dev, 1)], send_sem=send_sems.at[k],
                recv_sem=recv_sems.at[k], device_id=peer, device_id_type=MESH).wait_recv()
        for cp in cps:
            cp.wait_send()
        own.wait()
        terms = land[...]
        total = terms[0:1, :]
        for d in range(1, N_DEV):
            total = total + terms[d:d + 1, :]
        red[...] = total
        own = pltpu.make_async_copy(red, o_ref.at[pl.ds(me, 1)], loc_sem)
        own.start()
        cps = []
        for k, peer in enumerate(peers):
            cps.append(pltpu.make_async_remote_copy(
                src_ref=red, dst_ref=o_ref.at[pl.ds(me, 1)], send_sem=send2.at[k],
                recv_sem=recv2.at[k], device_id=peer, device_id_type=MESH))
            cps[-1].start()
        for k, peer in enumerate(peers):
            dev = 4 * peer[0] + 2 * peer[1] + peer[2]
            pltpu.make_async_remote_copy(
                src_ref=red, dst_ref=o_ref.at[pl.ds(dev, 1)], send_sem=send2.at[k],
                recv_sem=recv2.at[k], device_id=peer, device_id_type=MESH).wait_recv()
        for cp in cps:
            cp.wait_send()
        own.wait()

    vm = pl.BlockSpec(memory_space=pltpu.VMEM)
    return pl.pallas_call(
        body, name="all_reduce_small", in_specs=[vm], out_specs=vm,
        out_shape=jax.ShapeDtypeStruct((nrow, ncol), F32),
        scratch_shapes=[pltpu.VMEM((nrow, ncol), F32), pltpu.VMEM((1, ncol), F32)]
        + [pltpu.SemaphoreType.DMA((N_DEV - 1,))] * 4 + [pltpu.SemaphoreType.DMA],
        compiler_params=pltpu.CompilerParams(has_side_effects=True, vmem_limit_bytes=VMEM_LIMIT),
    )(v)


def _row_tile(rows):
    return min(rows, 512)


def _pair_sum(g, got, place):
    nb, r, c = g.shape
    h = r // 2
    tr = _row_tile(h)
    nt = h // tr

    def body(place_ref, g_ref, got_ref, p_ref, pb_ref):
        p = g_ref[...] + got_ref[...]
        pb_ref[...] = p.astype(BF16)

        @pl.when(pl.program_id(1) == place_ref[0])
        def _():
            p_ref[...] = p

    spec = pl.BlockSpec((None, tr, c), lambda t, j, place_ref: (j, t, 0))
    grid_spec = pltpu.PrefetchScalarGridSpec(
        num_scalar_prefetch=1, grid=(nt, nb),
        in_specs=[pl.BlockSpec((None, tr, c), lambda t, j, place_ref: (j, place_ref[1] * nt + t, 0)), spec],
        out_specs=[pl.BlockSpec((tr, c), lambda t, j, place_ref: (t, 0)), spec])
    return pl.pallas_call(
        body, name="pair_sum", grid_spec=grid_spec,
        out_shape=[jax.ShapeDtypeStruct((h, c), F32), jax.ShapeDtypeStruct((nb, h, c), BF16)],
        compiler_params=_params(("parallel", "arbitrary")),
    )(place, g, got)


def _chip_sum(p, got, place, out, layer):
    h, c = p.shape
    tr = _row_tile(h)
    nt = h // tr

    def body(place_ref, p_ref, g1_ref, g2_ref, g3_ref, old_ref, o_ref):
        o_ref[...] = ((p_ref[...] + g1_ref[...].astype(F32)) + g2_ref[...].astype(F32)) + g3_ref[...].astype(F32)

    def blk(off):
        return pl.BlockSpec((None, tr, c), lambda t, place_ref: ((place_ref[0] + off) % N_CHIPS, t, 0))

    grid_spec = pltpu.PrefetchScalarGridSpec(
        num_scalar_prefetch=1, grid=(nt,),
        in_specs=[pl.BlockSpec((tr, c), lambda t, place_ref: (t, 0)), blk(1), blk(2), blk(3),
                  pl.BlockSpec(memory_space=pl.ANY)],
        out_specs=pl.BlockSpec((None, tr, c), lambda t, place_ref: (layer, place_ref[1] * nt + t, 0)))
    return pl.pallas_call(
        body, name="chip_sum", grid_spec=grid_spec, out_shape=jax.ShapeDtypeStruct(out.shape, F32),
        input_output_aliases={5: 0}, compiler_params=_params(("parallel",)),
    )(place, p, got, got, got, out)


def _adamw(w, g, m, v, pass_g=False):
    r, c = w.shape
    tr = r if r < 8 else _row_tile(r)
    n_out = 4 if pass_g else 3

    def body(w_ref, g_ref, m_ref, v_ref, d_ref, nm_ref, nv_ref, *g_out):
        gv = g_ref[...]
        if pass_g:
            g_out[0][...] = gv
        nm = ADAM_B1 * m_ref[...] + (1.0 - ADAM_B1) * gv
        nv = ADAM_B2 * v_ref[...] + (1.0 - ADAM_B2) * (gv * gv)
        m_hat = nm / (1.0 - ADAM_B1 ** ADAM_STEP)
        v_hat = nv / (1.0 - ADAM_B2 ** ADAM_STEP)
        d_ref[...] = -ADAM_LR * (m_hat / (jnp.sqrt(v_hat) + ADAM_EPS) + ADAM_WD * w_ref[...])
        nm_ref[...] = nm
        nv_ref[...] = nv

    tile = pl.BlockSpec((tr, c), lambda i: (i, 0))
    return pl.pallas_call(
        body, name="adamw", grid=(r // tr,), in_specs=[tile] * 4, out_specs=[tile] * n_out,
        out_shape=[jax.ShapeDtypeStruct((r, c), F32)] * n_out, compiler_params=_params(("parallel",)),
    )(w, g, m, v)


BIG = ("a_w_in", "a_w_out", "sb_w_k", "sb_w_v", "b_w_q", "b_w_o", "ffn_w1", "ffn_w2")
SMALL = ("a_ln_g", "a_ln_b", "a_w_s", "a_b_s", "mix_ln_g", "mix_ln_b", "ffn_ln_g", "ffn_ln_b")
COL_SHARDED = {"a_w_in": True, "a_w_out": False, "sb_w_k": False, "sb_w_v": False, "b_w_q": False, "b_w_o": False,
               "ffn_w1": True, "ffn_w2": False}


def kernel(x, a_w_in, a_ln_g, a_ln_b, a_w_s, a_b_s, a_w_out, sb_w_k, sb_w_v, b_w_q, b_w_o, mix_ln_g, mix_ln_b, ffn_ln_g, ffn_ln_b, ffn_w1, ffn_w2, loss_target, m_a_w_in, m_a_ln_g, m_a_ln_b, m_a_w_s, m_a_b_s, m_a_w_out, m_sb_w_k, m_sb_w_v, m_b_w_q, m_b_w_o, m_mix_ln_g, m_mix_ln_b, m_ffn_ln_g, m_ffn_ln_b, m_ffn_w1, m_ffn_w2, v_a_w_in, v_a_ln_g, v_a_ln_b, v_a_w_s, v_a_b_s, v_a_w_out, v_sb_w_k, v_sb_w_v, v_b_w_q, v_b_w_o, v_mix_ln_g, v_mix_ln_b, v_ffn_ln_g, v_ffn_ln_b, v_ffn_w1, v_ffn_w2):
    names = BIG + SMALL
    given = dict(a_w_in=a_w_in, a_ln_g=a_ln_g, a_ln_b=a_ln_b, a_w_s=a_w_s, a_b_s=a_b_s, a_w_out=a_w_out, sb_w_k=sb_w_k,
                 sb_w_v=sb_w_v, b_w_q=b_w_q, b_w_o=b_w_o, mix_ln_g=mix_ln_g, mix_ln_b=mix_ln_b, ffn_ln_g=ffn_ln_g,
                 ffn_ln_b=ffn_ln_b, ffn_w1=ffn_w1, ffn_w2=ffn_w2)
    mom = dict(a_w_in=m_a_w_in, a_ln_g=m_a_ln_g, a_ln_b=m_a_ln_b, a_w_s=m_a_w_s, a_b_s=m_a_b_s, a_w_out=m_a_w_out,
               sb_w_k=m_sb_w_k, sb_w_v=m_sb_w_v, b_w_q=m_b_w_q, b_w_o=m_b_w_o, mix_ln_g=m_mix_ln_g, mix_ln_b=m_mix_ln_b,
               ffn_ln_g=m_ffn_ln_g, ffn_ln_b=m_ffn_ln_b, ffn_w1=m_ffn_w1, ffn_w2=m_ffn_w2)
    var = dict(a_w_in=v_a_w_in, a_ln_g=v_a_ln_g, a_ln_b=v_a_ln_b, a_w_s=v_a_w_s, a_b_s=v_a_b_s, a_w_out=v_a_w_out,
               sb_w_k=v_sb_w_k, sb_w_v=v_sb_w_v, b_w_q=v_b_w_q, b_w_o=v_b_w_o, mix_ln_g=v_mix_ln_g, mix_ln_b=v_mix_ln_b,
               ffn_ln_g=v_ffn_ln_g, ffn_ln_b=v_ffn_ln_b, ffn_w1=v_ffn_w1, ffn_w2=v_ffn_w2)

    cx, cy, cc = lax.axis_index("x"), lax.axis_index("y"), lax.axis_index("c")
    chip = (2 * cx + cy).astype(jnp.int32)
    chip_arr = chip.reshape(1)

    s, d = x.shape[1], x.shape[2]
    xf = x.reshape(s, d)
    target = loss_target.reshape(s, d)

    def as2d(w):
        return w.reshape(-1, w.shape[-1])

    gw = {}
    for n in BIG:
        for l in ([None] if given[n].ndim == 2 else range(given[n].shape[0])):
            gw[(n, l)] = _cast_into_slot(given[n], l, chip_arr)
    ln_gb = jnp.stack([a_ln_g, a_ln_b])
    ln_slot = lax.dynamic_update_slice(jnp.zeros((N_CHIPS,) + ln_gb.shape, F32), ln_gb[None], (chip, 0, 0, 0))
    layer0 = [("a_w_in", 0), ("a_w_out", 0)]
    gathered = _gather_weights([gw[k] for k in layer0] + [ln_slot])
    gw.update(zip(layer0, gathered[:-1]))
    mixer = {1: [("a_w_in", 1), ("a_w_out", 1)], 2: [("sb_w_k", None), ("sb_w_v", None), ("b_w_q", 0), ("b_w_o", 0)],
             3: [("b_w_q", 1), ("b_w_o", 1)]}

    def riding(d2d=(), ici=()):
        keys = list(d2d) + list(ici)
        return keys, [("d2d", gw[k]) for k in d2d] + [("ici", gw[k]) for k in ici]

    def landed_in(keys, bufs):
        gw.update(zip(keys, bufs))

    ln_full = gathered[-1].transpose(1, 2, 0, 3).reshape(2, N_A, 1, -1)
    a_ln_g3, a_ln_b3 = ln_full[0], ln_full[1]
    mix_g3, mix_b3 = mix_ln_g[:, None, :], mix_ln_b[:, None, :]
    ffn_g3, ffn_b3 = ffn_ln_g[:, None, :], ffn_ln_b[:, None, :]
    bst = jnp.swapaxes(a_b_s, 1, 2)

    saved = []
    xb = _cast_bf16(xf)
    kb = vb = None
    for l in range(DEPTH):
        sv = dict(x_in=xb)
        last = l == DEPTH - 1
        if l == 0:
            keys, riders = riding(ici=[("ffn_w1", 0)])
        else:
            keys, riders = riding(d2d=[("ffn_w1", l), ("ffn_w2", l)], ici=mixer[l + 1][:2] if l < N_A else [])
        if l < N_A:
            h, *bufs = _mm_fwd("a_in", xb, gw[("a_w_in", l)], None, True, riders=riders)
            landed_in(keys, bufs)
            vn = _gmlp_norm_fwd(h, a_ln_g3, a_ln_b3, l)
            gated = _gate_fwd(h, vn, a_w_s[l], bst[l])
            keys, riders = riding(d2d=[("ffn_w1", 0)], ici=[("ffn_w2", 0)]) if l == 0 else riding(ici=mixer[l + 1][2:])
            xf, xb, xhat, rstd, *bufs = _mm_resid_ln("a_out", gated, gw[("a_w_out", l)], xf, mix_g3, mix_b3, l, riders)
            landed_in(keys, bufs)
            sv.update(h=h, vn=vn, gated=gated)
        else:
            j = l - N_A
            if l == N_A:
                kb, *bufs = _mm_fwd("sb_k", xb, gw[("sb_w_k", None)], None, False, _ep_bf16, outs=[(d, BF16)],
                                    riders=riders)
                landed_in(keys, bufs)
                keys, riders = [], ()
                vb = _mm_fwd("sb_v", xb, gw[("sb_w_v", None)], None, False, _ep_bf16, outs=[(d, BF16)])[0]
            q, *bufs = _mm_fwd("b_q", xb, gw[("b_w_q", j)], None, False, _ep_scale_q, outs=[(d, BF16)], riders=riders)
            landed_in(keys, bufs)
            ob, lsum = _attn_fwd(q, kb, vb)
            keys, riders = riding(ici=[] if last else mixer[l + 1])
            xf, xb, xhat, rstd, *bufs = _mm_resid_ln("b_out", ob, gw[("b_w_o", j)], xf, mix_g3, mix_b3, l, riders)
            landed_in(keys, bufs)
            sv.update(q=q, lsum=lsum, ob=ob)
        sv.update(x_mid=xb, xhat1=xhat, rstd1=rstd)
        dff = gw[("ffn_w1", l)].shape[-1] * N_CHIPS
        if l == 0:
            keys, riders = riding(d2d=[("ffn_w2", 0)], ici=mixer[1] + [("ffn_w1", 1)])
        else:
            keys, riders = riding(ici=[] if last else [("ffn_w1", l + 1)])
        pr, *bufs = _mm_fwd("ffn_1", xb, gw[("ffn_w1", l)], None, True, _ep_relu, outs=[(dff, BF16)], riders=riders)
        landed_in(keys, bufs)
        keys, riders = riding(d2d=[] if last else mixer[l + 1], ici=[] if last else [("ffn_w2", l + 1)])
        xf, xb, xhat, rstd, *bufs = _mm_resid_ln("ffn_2", pr, gw[("ffn_w2", l)], xf, ffn_g3, ffn_b3, l, riders, _square)
        landed_in(keys, bufs)
        sv.update(pr=pr, xhat2=xhat, rstd2=rstd)
        saved.append(sv)

    dx = xf

    pending = []
    pair_sums, landed = {}, {}
    place_arr = jnp.stack([chip, cc.astype(jnp.int32)])

    def arrived(took, outs):
        for (kind, key, arr), out in zip(took, outs):
            if kind == "pair":
                pair_sums[key] = _pair_sum(arr, out, place_arr)
                pending.append(("chip", key, pair_sums[key][1]))
            else:
                landed[key] = out

    def carrying(call, name, *args, **kw):
        took = []
        if name.startswith("ffn") or draining[0]:
            room = CARRIER_PARAMS
            for task in list(pending):
                size = given[task[1][0]].shape[-2] * given[task[1][0]].shape[-1] * N_CHIPS
                if task[0] == "pair" or room == CARRIER_PARAMS or size <= room:
                    took.append(task)
                    pending.remove(task)
                    room -= size if task[0] == "chip" else 0
        results = call(name, *args, riders=[(kind, arr) for kind, _, arr in took], **kw)
        own = len(results) - len(took)
        arrived(took, results[own:])
        return results[0] if own == 1 else results[:own]

    draining = [False]

    def bwd_act(*args, **kw):
        return carrying(_mm_bwd_act, *args, **kw)

    def bwd_w(key, name, a, dy, **kw):
        pending.append(("pair", key, carrying(_mm_bwd_w, name, a, dy, gw[key], COL_SHARDED[key[0]], **kw)))

    d_mix_g, d_mix_b, d_ffn_g, d_ffn_b = [None] * DEPTH, [None] * DEPTH, [None] * DEPTH, [None] * DEPTH
    d_ln_g, d_ln_b, d_ws, d_bs = [None] * N_A, [None] * N_A, [None] * N_A, [None] * N_A
    dk = dv = normed = None
    for l in reversed(range(DEPTH)):
        sv = saved[l]
        draining[0] = l == 0
        if l == DEPTH - 1:
            dr, drb, d_ffn_g[l], d_ffn_b[l], sq = _ln_bwd(dx, sv["xhat2"], sv["rstd2"], ffn_g3, l, target)
            loss = lax.psum(0.5 * sq[0, 0] / d, ("x", "y", "c"))
        elif normed:
            dr, drb = normed
            normed = None
        else:
            dr, drb, d_ffn_g[l], d_ffn_b[l] = _ln_bwd(dx, sv["xhat2"], sv["rstd2"], ffn_g3, l)
        dff = sv["pr"].shape[1]
        dhd = bwd_act("ffn_2_dx", drb, gw[("ffn_w2", l)], None, False, _ep_relu2_bwd, (sv["pr"],),
                      (pl.BlockSpec((_wide_tile(s), dff // N_CHIPS), lambda j, i, k: (i, j)),), out_dtype=BF16)
        bwd_w(("ffn_w2", l), "ffn_2_dw", sv["pr"], drb, a_fn=_square)
        dr, drb, d_mix_g[l], d_mix_b[l] = bwd_act(
            "ffn_1_dx", dhd, gw[("ffn_w1", l)], None, True, _ep_resid_ln_bwd, (dr, sv["xhat1"], sv["rstd1"], mix_g3),
            (_row_spec(d), _row_spec(d), _row_spec(1), _vec_spec(l, d)), through_norm=True)
        bwd_w(("ffn_w1", l), "ffn_1_dw", sv["x_mid"], dhd)

        quarter = pl.BlockSpec((_wide_tile(s), d // N_CHIPS), lambda j, i, k: (i, j))
        if l < N_A:
            dgated = bwd_act("a_out_dx", drb, gw[("a_w_out", l)], None, False)
            bwd_w(("a_w_out", l), "a_out_dw", sv["gated"], drb)
            dh, d_ws[l], dbs_wide, dlg, dlb = _gmlp_bwd(dgated, sv["h"], sv["vn"], a_w_s[l], bst[l], a_ln_g3, l)
            d_bs[l] = dbs_wide[:, :, 0]
            d_ln_g[l], d_ln_b[l] = dlg[0], dlb[0]
            if l:
                below = saved[l - 1]
                *normed, d_ffn_g[l - 1], d_ffn_b[l - 1] = bwd_act(
                    "a_in_dx", dh, gw[("a_w_in", l)], None, True, _ep_resid_ln_bwd,
                    (dr, below["xhat2"], below["rstd2"], ffn_g3),
                    (_row_spec(d), _row_spec(d), _row_spec(1), _vec_spec(l - 1, d)), through_norm=True)
            else:
                dx = bwd_act("a_in_dx", dh, gw[("a_w_in", l)], None, True, _ep_resid, (dr,), (_row_spec(d),))
            bwd_w(("a_w_in", l), "a_in_dw", sv["x_in"], dh)
        else:
            j = l - N_A
            do = bwd_act("b_out_dx", drb, gw[("b_w_o", j)], None, False)
            bwd_w(("b_w_o", j), "b_out_dw", sv["ob"], drb)
            dq, dk, dv = _attn_bwd(sv["q"], kb, vb, do, sv["lsum"], dk, dv)
            dx = bwd_act("b_q_dx", dq, gw[("b_w_q", j)], None, False, _ep_resid, (dr,), (quarter,))
            bwd_w(("b_w_q", j), "b_q_dw", sv["x_in"], dq)
            if l == N_A:
                dx = bwd_act("sb_k_dx", dk, gw[("sb_w_k", None)], None, False, _ep_add, (dx,), (quarter,))
                bwd_w(("sb_w_k", None), "sb_k_dw", sv["x_in"], dk)
                dx = bwd_act("sb_v_dx", dv, gw[("sb_w_v", None)], None, False, _ep_add, (dx,), (quarter,))
                bwd_w(("sb_w_v", None), "sb_v_dw", sv["x_in"], dv)
    grad_x = dx.reshape(x.shape)

    while pending:
        took = list(pending)
        pending.clear()
        for kind, exchange in (("pair", _pair_exchange), ("chip", _chip_exchange)):
            some = [t for t in took if t[0] == kind]
            if some:
                arrived(some, exchange([arr for _, _, arr in some]))

    stacked = []
    for n in BIG:
        layers = [None] if given[n].ndim == 2 else range(given[n].shape[0])
        out = lax.empty((len(layers),) + given[n].shape[-2:], F32)
        for at, l in enumerate(layers):
            out = _chip_sum(pair_sums[(n, l)][0], landed[(n, l)], place_arr, out, at)
        stacked.append(out)
    grads = {n: g.reshape(given[n].shape) for n, g in zip(BIG, _half_swap(stacked))}

    small_full = dict(a_ln_g=jnp.stack(d_ln_g), a_ln_b=jnp.stack(d_ln_b), a_w_s=jnp.stack(d_ws), a_b_s=jnp.stack(d_bs),
                      mix_ln_g=jnp.concatenate(d_mix_g), mix_ln_b=jnp.concatenate(d_mix_b),
                      ffn_ln_g=jnp.concatenate(d_ffn_g), ffn_ln_b=jnp.concatenate(d_ffn_b))
    packed = jnp.concatenate([small_full[n].reshape(-1) for n in SMALL])
    total = packed.shape[0]
    ncol = -(-total // (N_DEV * LANES)) * LANES
    packed = jnp.pad(packed, (0, N_DEV * ncol - total)).reshape(N_DEV, ncol)
    reduced = _all_reduce_small(packed).reshape(-1)
    off = 0
    for n in SMALL:
        size = small_full[n].size
        g = reduced[off:off + size].reshape(small_full[n].shape)
        off += size
        if n in ("a_ln_g", "a_ln_b"):
            wq = given[n].shape[1]
            g = lax.dynamic_slice_in_dim(g, chip * wq, wq, axis=1)
        grads[n] = g

    delta, new_m, new_v = {}, {}, {}
    for n in names:
        shape = given[n].shape
        dl, nm, nv, *g = _adamw(as2d(given[n]), as2d(grads[n]), as2d(mom[n]), as2d(var[n]), pass_g=n in BIG)
        delta[n], new_m[n], new_v[n] = dl.reshape(shape), nm.reshape(shape), nv.reshape(shape)
        if g:
            grads[n] = g[0].reshape(shape)

    order = ("a_w_in", "a_ln_g", "a_ln_b", "a_w_s", "a_b_s", "a_w_out", "sb_w_k", "sb_w_v", "b_w_q", "b_w_o",
             "mix_ln_g", "mix_ln_b", "ffn_ln_g", "ffn_ln_b", "ffn_w1", "ffn_w2")
    return (loss, grad_x, *[grads[n] for n in order], *[delta[n] for n in order],
            *[new_m[n] for n in order], *[new_v[n] for n in order])
```
